```python
import jax, jax.numpy as jnp
from jax import lax
import numpy as np


D_MODEL = 1024
BATCH = 16
SEQ = 2048
DEPTH = 2

DN_HEADS = 4
DN_HEAD_DIM = 128
DN_WIDTH = DN_HEADS * DN_HEAD_DIM
DN_CONV = 4
DN_CHUNK = 64
SB_HEADS = 4
SB_HEAD_DIM = 64
SB_WIDTH = SB_HEADS * SB_HEAD_DIM
SB_BLOCK = 128
SG_GROUPS = 4
SG_GROUP_DIM = 64
SG_WIDTH = SG_GROUPS * SG_GROUP_DIM
SG_CHUNK = 128
MIX_WIDTH = DN_WIDTH + SB_WIDTH + SG_WIDTH
D_FF = 4 * D_MODEL
IN_SPLITS = (3 * DN_WIDTH, DN_WIDTH, DN_HEADS, DN_HEADS, 3 * SB_WIDTH, 2 * SG_WIDTH)
IN_DIM = int(sum(IN_SPLITS))
IN_SPLIT_IDX = tuple(int(i) for i in np.cumsum(IN_SPLITS)[:-1])
NORM_EPS = 1e-6

kernel_name = 'hybrid_deltanet_stickbreaking_sgmlp_block'


def rms_norm(x, gain):
    xf = x.astype(jnp.float32)
    y = xf * lax.rsqrt(jnp.mean(xf * xf, axis=-1, keepdims=True) + NORM_EPS)
    return (y * gain.astype(jnp.float32)).astype(x.dtype)


def l2_norm(x):
    xf = x.astype(jnp.float32)
    return xf * lax.rsqrt(jnp.sum(xf * xf, axis=-1, keepdims=True) + NORM_EPS)


def causal_depthwise_conv(x, w):
    K, C = w.shape
    return lax.conv_general_dilated(
        x, w[:, None, :].astype(x.dtype), window_strides=(1,), padding=[(K - 1, 0)],
        dimension_numbers=('NWC', 'WIO', 'NWC'), feature_group_count=C)


def gated_delta_rule(q, k, v, g, beta):
    f32 = jnp.float32
    q, k, v, g, beta = (t.astype(f32) for t in (q, k, v, g, beta))
    B, T, H, Dk = q.shape
    Dv = v.shape[-1]
    C = DN_CHUNK
    N = T // C
    q = q * (Dk ** -0.5)

    def to_chunks(t):
        return jnp.moveaxis(t.reshape((B, N, C) + t.shape[2:]), 3, 2)

    qc, kc, vc, gc, bc = (to_chunks(t) for t in (q, k, v, g, beta))
    gcum = jnp.cumsum(gc, axis=-1)
    tril_incl = jnp.tril(jnp.ones((C, C), bool))
    tril_strict = jnp.tril(jnp.ones((C, C), bool), -1)
    decay = jnp.exp(jnp.where(tril_incl, gcum[..., :, None] - gcum[..., None, :], -jnp.inf))
    kk = jnp.einsum('bnhid,bnhjd->bnhij', kc, kc)
    lower = jnp.where(tril_strict, bc[..., :, None] * kk * decay, 0.0)
    a_mat = lower + jnp.eye(C, dtype=f32)
    rhs = jnp.concatenate([vc * bc[..., None], kc * (bc * jnp.exp(gcum))[..., None]], axis=-1)
    sol = lax.linalg.triangular_solve(a_mat, rhs, left_side=True, lower=True, unit_diagonal=True)
    u_val, w_dec = sol[..., :Dv], sol[..., Dv:]
    qk = jnp.einsum('bnhid,bnhjd->bnhij', qc, kc) * decay
    q_dec = qc * jnp.exp(gcum)[..., None]
    k_dec = kc * jnp.exp(gcum[..., -1:] - gcum)[..., None]
    chunk_decay = jnp.exp(gcum[..., -1])

    def step(S, inp):
        u_n, w_n, qk_n, qd_n, kd_n, cd_n = inp
        u_new = u_n - jnp.einsum('bhcd,bhde->bhce', w_n, S)
        o = jnp.einsum('bhcd,bhde->bhce', qd_n, S) + jnp.einsum('bhij,bhje->bhie', qk_n, u_new)
        S = S * cd_n[..., None, None] + jnp.einsum('bhcd,bhce->bhde', kd_n, u_new)
        return S, o

    xs = tuple(jnp.moveaxis(t, 1, 0) for t in (u_val, w_dec, qk, q_dec, k_dec, chunk_decay))
    S0 = jnp.zeros((B, H, Dk, Dv), f32)
    _, o = lax.scan(step, S0, xs)
    o = jnp.moveaxis(jnp.moveaxis(o, 0, 1), 2, 3).reshape(B, T, H, Dv)
    return o


def stick_breaking_attention(q, k, v):
    B, T, H, D = q.shape
    scale = D ** -0.5
    outs = []
    for blk in range(T // SB_BLOCK):
        q0 = blk * SB_BLOCK
        q1 = q0 + SB_BLOCK
        qb, kb, vb = q[:, q0:q1], k[:, :q1], v[:, :q1]
        z = jnp.einsum('bthd,bshd->bhts', qb, kb).astype(jnp.float32) * scale
        t_idx = q0 + jnp.arange(SB_BLOCK)
        s_idx = jnp.arange(q1)
        mask = s_idx[None, :] < t_idx[:, None]
        log_1m = jnp.where(mask, jax.nn.log_sigmoid(-z), 0.0)
        remaining = lax.cumsum(log_1m, axis=3, reverse=True) - log_1m
        weights = jnp.where(mask, jnp.exp(jax.nn.log_sigmoid(z) + remaining), 0.0)
        outs.append(jnp.einsum('bhts,bshd->bthd', weights.astype(v.dtype), vb))
    return jnp.concatenate(outs, axis=1)


def chunked_spatial_gating(u, v, v_gain, w_s, b_s):
    B, T, _ = u.shape
    N = T // SG_CHUNK
    u = jax.nn.gelu(u)
    v = jax.nn.gelu(v).reshape(B, T, SG_GROUPS, SG_GROUP_DIM)
    v = rms_norm(v, v_gain.reshape(SG_GROUPS, SG_GROUP_DIM))
    v = v.reshape(B, N, SG_CHUNK, SG_GROUPS, SG_GROUP_DIM)
    w = jnp.where(jnp.tril(jnp.ones((SG_CHUNK, SG_CHUNK), bool)), w_s, 0.0).astype(v.dtype)
    mixed = jnp.einsum('gts,bnsgd->bntgd', w, v) + b_s.T[None, None, :, :, None]
    return u * mixed.reshape(B, T, SG_WIDTH)


def hybrid_mixer(h, w_in, conv_w, a_log, dt_bias, dn_out_g, sb_q_g, sb_k_g, sg_v_g, sg_w, sg_b, w_out):
    B, T, _ = h.shape
    proj = h @ w_in
    dn_qkv, dn_z, dn_a, dn_b, sb_qkv, sg_uv = jnp.split(proj, IN_SPLIT_IDX, axis=-1)
    dn_qkv = jax.nn.silu(causal_depthwise_conv(dn_qkv, conv_w))
    q, k, v = (t.reshape(B, T, DN_HEADS, DN_HEAD_DIM) for t in jnp.split(dn_qkv, 3, axis=-1))
    g = -jnp.exp(a_log.astype(jnp.float32)) * jax.nn.softplus(dn_a.astype(jnp.float32) + dt_bias.astype(jnp.float32))
    beta = jax.nn.sigmoid(dn_b.astype(jnp.float32))
    o_dn = gated_delta_rule(l2_norm(q), l2_norm(k), v, g, beta).astype(h.dtype)
    o_dn = rms_norm(o_dn, dn_out_g) * jax.nn.silu(dn_z.reshape(B, T, DN_HEADS, DN_HEAD_DIM))
    o_dn = o_dn.reshape(B, T, DN_WIDTH)
    sq, sk, sv = (t.reshape(B, T, SB_HEADS, SB_HEAD_DIM) for t in jnp.split(sb_qkv, 3, axis=-1))
    o_sb = stick_breaking_attention(rms_norm(sq, sb_q_g), rms_norm(sk, sb_k_g), sv).reshape(B, T, SB_WIDTH)
    su, svv = jnp.split(sg_uv, 2, axis=-1)
    o_sg = chunked_spatial_gating(su, svv, sg_v_g, sg_w, sg_b)
    return jnp.concatenate([o_dn, o_sb, o_sg], axis=-1) @ w_out


def _fwd_setup_inputs(seed: int = 0) -> dict:
    key = jax.random.key(seed)
    ks = jax.random.split(key, 20)
    f32 = jnp.float32
    nrm = lambda k, shape, s: jax.random.normal(k, shape, f32) * s
    dt = jnp.exp(jax.random.uniform(ks[5], (DEPTH, DN_HEADS), f32, np.log(1e-3), np.log(1e-1)))
    return {
        'x': nrm(ks[0], (BATCH, SEQ, D_MODEL), 1.0),
        'norm1_g': 1.0 + nrm(ks[1], (DEPTH, D_MODEL), 0.1),
        'w_in': nrm(ks[2], (DEPTH, D_MODEL, IN_DIM), D_MODEL ** -0.5),
        'conv_w': nrm(ks[3], (DEPTH, DN_CONV, 3 * DN_WIDTH), DN_CONV ** -0.5),
        'a_log': jnp.log(jax.random.uniform(ks[4], (DEPTH, DN_HEADS), f32, 1.0, 16.0)),
        'dt_bias': jnp.log(jnp.expm1(dt)),
        'dn_out_g': 1.0 + nrm(ks[6], (DEPTH, DN_HEAD_DIM), 0.1),
        'sb_q_g': 1.0 + nrm(ks[7], (DEPTH, SB_HEAD_DIM), 0.1),
        'sb_k_g': 1.0 + nrm(ks[8], (DEPTH, SB_HEAD_DIM), 0.1),
        'sg_v_g': 1.0 + nrm(ks[9], (DEPTH, SG_WIDTH), 0.1),
        'sg_w': nrm(ks[10], (DEPTH, SG_GROUPS, SG_CHUNK, SG_CHUNK), SG_CHUNK ** -0.5),
        'sg_b': 1.0 + nrm(ks[11], (DEPTH, SG_GROUPS, SG_CHUNK), 0.1),
        'w_out': nrm(ks[12], (DEPTH, MIX_WIDTH, D_MODEL), MIX_WIDTH ** -0.5),
        'norm2_g': 1.0 + nrm(ks[13], (DEPTH, D_MODEL), 0.1),
        'w_ff1': nrm(ks[14], (DEPTH, D_MODEL, D_FF), D_MODEL ** -0.5),
        'w_ff2': nrm(ks[15], (DEPTH, D_FF, D_MODEL), D_FF ** -0.5),
    }


def _fwd_reference(x, norm1_g, w_in, conv_w, a_log, dt_bias, dn_out_g, sb_q_g, sb_k_g, sg_v_g, sg_w, sg_b,
              w_out, norm2_g, w_ff1, w_ff2):
    for l in range(DEPTH):
        h = rms_norm(x, norm1_g[l])
        x = x + hybrid_mixer(h, w_in[l], conv_w[l], a_log[l], dt_bias[l], dn_out_g[l], sb_q_g[l], sb_k_g[l],
                             sg_v_g[l], sg_w[l], sg_b[l], w_out[l])
        h = rms_norm(x, norm2_g[l])
        x = x + jnp.square(jax.nn.relu(h @ w_ff1[l])) @ w_ff2[l]
    return x


import jax as _jax
import jax.numpy as _jnp

TWIN_FORMAT = 'train_step'
FWD_PARAMS = ['x', 'norm1_g', 'w_in', 'conv_w', 'a_log', 'dt_bias', 'dn_out_g', 'sb_q_g', 'sb_k_g', 'sg_v_g', 'sg_w', 'sg_b', 'w_out', 'norm2_g', 'w_ff1', 'w_ff2']
TWIN_WEIGHTS = ['norm1_g', 'w_in', 'conv_w', 'a_log', 'dt_bias', 'dn_out_g', 'sb_q_g', 'sb_k_g', 'sg_v_g', 'sg_w', 'sg_b', 'w_out', 'norm2_g', 'w_ff1', 'w_ff2']
TWIN_DIFF_INPUT = 'x'
TWIN_INPUTS = ['x', 'norm1_g', 'w_in', 'conv_w', 'a_log', 'dt_bias', 'dn_out_g', 'sb_q_g', 'sb_k_g', 'sg_v_g', 'sg_w', 'sg_b', 'w_out', 'norm2_g', 'w_ff1', 'w_ff2', 'loss_target', 'm_norm1_g', 'm_w_in', 'm_conv_w', 'm_a_log', 'm_dt_bias', 'm_dn_out_g', 'm_sb_q_g', 'm_sb_k_g', 'm_sg_v_g', 'm_sg_w', 'm_sg_b', 'm_w_out', 'm_norm2_g', 'm_w_ff1', 'm_w_ff2', 'v_norm1_g', 'v_w_in', 'v_conv_w', 'v_a_log', 'v_dt_bias', 'v_dn_out_g', 'v_sb_q_g', 'v_sb_k_g', 'v_sg_v_g', 'v_sg_w', 'v_sg_b', 'v_w_out', 'v_norm2_g', 'v_w_ff1', 'v_w_ff2']
TWIN_OUTPUTS = ['loss', 'grad_x', 'grad_norm1_g', 'grad_w_in', 'grad_conv_w', 'grad_a_log', 'grad_dt_bias', 'grad_dn_out_g', 'grad_sb_q_g', 'grad_sb_k_g', 'grad_sg_v_g', 'grad_sg_w', 'grad_sg_b', 'grad_w_out', 'grad_norm2_g', 'grad_w_ff1', 'grad_w_ff2', 'delta_norm1_g', 'delta_w_in', 'delta_conv_w', 'delta_a_log', 'delta_dt_bias', 'delta_dn_out_g', 'delta_sb_q_g', 'delta_sb_k_g', 'delta_sg_v_g', 'delta_sg_w', 'delta_sg_b', 'delta_w_out', 'delta_norm2_g', 'delta_w_ff1', 'delta_w_ff2', 'new_m_norm1_g', 'new_m_w_in', 'new_m_conv_w', 'new_m_a_log', 'new_m_dt_bias', 'new_m_dn_out_g', 'new_m_sb_q_g', 'new_m_sb_k_g', 'new_m_sg_v_g', 'new_m_sg_w', 'new_m_sg_b', 'new_m_w_out', 'new_m_norm2_g', 'new_m_w_ff1', 'new_m_w_ff2', 'new_v_norm1_g', 'new_v_w_in', 'new_v_conv_w', 'new_v_a_log', 'new_v_dt_bias', 'new_v_dn_out_g', 'new_v_sb_q_g', 'new_v_sb_k_g', 'new_v_sg_v_g', 'new_v_sg_w', 'new_v_sg_b', 'new_v_w_out', 'new_v_norm2_g', 'new_v_w_ff1', 'new_v_w_ff2']
TWIN_LEAF_KINDS = {'loss': 'loss', 'grad_x': 'grad_x', 'grad_norm1_g': 'grad_w', 'grad_w_in': 'grad_w', 'grad_conv_w': 'grad_w', 'grad_a_log': 'grad_w', 'grad_dt_bias': 'grad_w', 'grad_dn_out_g': 'grad_w', 'grad_sb_q_g': 'grad_w', 'grad_sb_k_g': 'grad_w', 'grad_sg_v_g': 'grad_w', 'grad_sg_w': 'grad_w', 'grad_sg_b': 'grad_w', 'grad_w_out': 'grad_w', 'grad_norm2_g': 'grad_w', 'grad_w_ff1': 'grad_w', 'grad_w_ff2': 'grad_w', 'delta_norm1_g': 'delta_w', 'delta_w_in': 'delta_w', 'delta_conv_w': 'delta_w', 'delta_a_log': 'delta_w', 'delta_dt_bias': 'delta_w', 'delta_dn_out_g': 'delta_w', 'delta_sb_q_g': 'delta_w', 'delta_sb_k_g': 'delta_w', 'delta_sg_v_g': 'delta_w', 'delta_sg_w': 'delta_w', 'delta_sg_b': 'delta_w', 'delta_w_out': 'delta_w', 'delta_norm2_g': 'delta_w', 'delta_w_ff1': 'delta_w', 'delta_w_ff2': 'delta_w', 'new_m_norm1_g': 'new_m', 'new_m_w_in': 'new_m', 'new_m_conv_w': 'new_m', 'new_m_a_log': 'new_m', 'new_m_dt_bias': 'new_m', 'new_m_dn_out_g': 'new_m', 'new_m_sb_q_g': 'new_m', 'new_m_sb_k_g': 'new_m', 'new_m_sg_v_g': 'new_m', 'new_m_sg_w': 'new_m', 'new_m_sg_b': 'new_m', 'new_m_w_out': 'new_m', 'new_m_norm2_g': 'new_m', 'new_m_w_ff1': 'new_m', 'new_m_w_ff2': 'new_m', 'new_v_norm1_g': 'new_v', 'new_v_w_in': 'new_v', 'new_v_conv_w': 'new_v', 'new_v_a_log': 'new_v', 'new_v_dt_bias': 'new_v', 'new_v_dn_out_g': 'new_v', 'new_v_sb_q_g': 'new_v', 'new_v_sb_k_g': 'new_v', 'new_v_sg_v_g': 'new_v', 'new_v_sg_w': 'new_v', 'new_v_sg_b': 'new_v', 'new_v_w_out': 'new_v', 'new_v_norm2_g': 'new_v', 'new_v_w_ff1': 'new_v', 'new_v_w_ff2': 'new_v'}


def _forward(args):
    return _fwd_reference(*[args[k] for k in FWD_PARAMS])


def _output_shape():
    out = _jax.eval_shape(lambda: _forward(_fwd_setup_inputs(0)))
    return out.shape, out.dtype

N_MICROBATCH = 1
ADAM_LR = 0.001
ADAM_B1 = 0.9
ADAM_B2 = 0.999
ADAM_EPS = 1e-08
ADAM_WD = 0.01
ADAM_STEP = 10
PER_EXAMPLE_BATCH_AXIS = {'x': 0, 'loss_target': 0}
SHARED_INPUTS = []
_WEIGHT_DTYPES = {'norm1_g': _jnp.float32, 'w_in': _jnp.float32, 'conv_w': _jnp.float32, 'a_log': _jnp.float32, 'dt_bias': _jnp.float32, 'dn_out_g': _jnp.float32, 'sb_q_g': _jnp.float32, 'sb_k_g': _jnp.float32, 'sg_v_g': _jnp.float32, 'sg_w': _jnp.float32, 'sg_b': _jnp.float32, 'w_out': _jnp.float32, 'norm2_g': _jnp.float32, 'w_ff1': _jnp.float32, 'w_ff2': _jnp.float32}
MOMENT_SCALE = {'norm1_g': 1.766666e+01, 'w_in': 3.151679e+00, 'conv_w': 3.211178e+00, 'a_log': 4.904273e+01, 'dt_bias': 4.662951e+01, 'dn_out_g': 5.911272e+01, 'sb_q_g': 7.869410e+00, 'sb_k_g': 7.707778e+00, 'sg_v_g': 7.623242e+00, 'sg_w': 1.941650e+00, 'sg_b': 7.000226e+00, 'w_out': 8.272487e+00, 'norm2_g': 9.909567e+01, 'w_ff1': 4.667220e+00, 'w_ff2': 1.748184e+01}


def _to_microbatches(a, axis):
    t = _jnp.moveaxis(a, axis, 0)
    t = t.reshape((N_MICROBATCH, t.shape[0] // N_MICROBATCH) + t.shape[1:])
    return _jnp.moveaxis(t, 1, axis + 1)


def setup_inputs(seed: int = 0) -> dict:
    inp = _fwd_setup_inputs(seed)
    key = _jax.random.fold_in(_jax.random.key(seed), 7919)
    shape, _ = _output_shape()
    out = dict(inp)
    out["loss_target"] = _jax.random.normal(_jax.random.fold_in(key, 0), shape, _jnp.float32)
    for i, name in enumerate(TWIN_WEIGHTS):
        w = inp[name].astype(_jnp.float32)
        if MOMENT_SCALE is None:
            s = _jnp.sqrt(_jnp.mean(_jnp.square(w)) + 1e-30)
        else:
            s = MOMENT_SCALE[name]
        km, kv = _jax.random.split(_jax.random.fold_in(key, i + 1))
        out[name] = w
        out["m_" + name] = s * _jax.random.normal(km, w.shape, _jnp.float32)
        out["v_" + name] = (s * s) * _jax.random.uniform(kv, w.shape, _jnp.float32, 0.5, 1.5)
    if N_MICROBATCH > 1:
        for name, axis in PER_EXAMPLE_BATCH_AXIS.items():
            out[name] = _to_microbatches(out[name], axis)
    return {'x': out['x'], 'norm1_g': out['norm1_g'], 'w_in': out['w_in'], 'conv_w': out['conv_w'], 'a_log': out['a_log'], 'dt_bias': out['dt_bias'], 'dn_out_g': out['dn_out_g'], 'sb_q_g': out['sb_q_g'], 'sb_k_g': out['sb_k_g'], 'sg_v_g': out['sg_v_g'], 'sg_w': out['sg_w'], 'sg_b': out['sg_b'], 'w_out': out['w_out'], 'norm2_g': out['norm2_g'], 'w_ff1': out['w_ff1'], 'w_ff2': out['w_ff2'], 'loss_target': out['loss_target'], 'm_norm1_g': out['m_norm1_g'], 'm_w_in': out['m_w_in'], 'm_conv_w': out['m_conv_w'], 'm_a_log': out['m_a_log'], 'm_dt_bias': out['m_dt_bias'], 'm_dn_out_g': out['m_dn_out_g'], 'm_sb_q_g': out['m_sb_q_g'], 'm_sb_k_g': out['m_sb_k_g'], 'm_sg_v_g': out['m_sg_v_g'], 'm_sg_w': out['m_sg_w'], 'm_sg_b': out['m_sg_b'], 'm_w_out': out['m_w_out'], 'm_norm2_g': out['m_norm2_g'], 'm_w_ff1': out['m_w_ff1'], 'm_w_ff2': out['m_w_ff2'], 'v_norm1_g': out['v_norm1_g'], 'v_w_in': out['v_w_in'], 'v_conv_w': out['v_conv_w'], 'v_a_log': out['v_a_log'], 'v_dt_bias': out['v_dt_bias'], 'v_dn_out_g': out['v_dn_out_g'], 'v_sb_q_g': out['v_sb_q_g'], 'v_sb_k_g': out['v_sb_k_g'], 'v_sg_v_g': out['v_sg_v_g'], 'v_sg_w': out['v_sg_w'], 'v_sg_b': out['v_sg_b'], 'v_w_out': out['v_w_out'], 'v_norm2_g': out['v_norm2_g'], 'v_w_ff1': out['v_w_ff1'], 'v_w_ff2': out['v_w_ff2']}


def _loss(weights, diff, rest, loss_target):
    with _jax.named_scope("forward"):
        args = {**rest, TWIN_DIFF_INPUT: diff, **{k: w.astype(_WEIGHT_DTYPES[k]) for k, w in weights.items()}}
        y = _forward(args)
    with _jax.named_scope("loss_head"):
        err = _jnp.square(y.astype(_jnp.float32) - loss_target)
        return 0.5 * _jnp.sum(_jnp.mean(err, axis=-1)) if err.ndim else 0.5 * err


def _adamw(w, g, m, v):
    m = ADAM_B1 * m + (1.0 - ADAM_B1) * g
    v = ADAM_B2 * v + (1.0 - ADAM_B2) * _jnp.square(g)
    m_hat = m / (1.0 - ADAM_B1 ** ADAM_STEP)
    v_hat = v / (1.0 - ADAM_B2 ** ADAM_STEP)
    delta = -ADAM_LR * (m_hat / (_jnp.sqrt(v_hat) + ADAM_EPS) + ADAM_WD * w)
    return delta, m, v


def reference(x, norm1_g, w_in, conv_w, a_log, dt_bias, dn_out_g, sb_q_g, sb_k_g, sg_v_g, sg_w, sg_b, w_out, norm2_g, w_ff1, w_ff2, loss_target, m_norm1_g, m_w_in, m_conv_w, m_a_log, m_dt_bias, m_dn_out_g, m_sb_q_g, m_sb_k_g, m_sg_v_g, m_sg_w, m_sg_b, m_w_out, m_norm2_g, m_w_ff1, m_w_ff2, v_norm1_g, v_w_in, v_conv_w, v_a_log, v_dt_bias, v_dn_out_g, v_sb_q_g, v_sb_k_g, v_sg_v_g, v_sg_w, v_sg_b, v_w_out, v_norm2_g, v_w_ff1, v_w_ff2):
    given = dict(x=x, norm1_g=norm1_g, w_in=w_in, conv_w=conv_w, a_log=a_log, dt_bias=dt_bias, dn_out_g=dn_out_g, sb_q_g=sb_q_g, sb_k_g=sb_k_g, sg_v_g=sg_v_g, sg_w=sg_w, sg_b=sg_b, w_out=w_out, norm2_g=norm2_g, w_ff1=w_ff1, w_ff2=w_ff2, loss_target=loss_target, m_norm1_g=m_norm1_g, m_w_in=m_w_in, m_conv_w=m_conv_w, m_a_log=m_a_log, m_dt_bias=m_dt_bias, m_dn_out_g=m_dn_out_g, m_sb_q_g=m_sb_q_g, m_sb_k_g=m_sb_k_g, m_sg_v_g=m_sg_v_g, m_sg_w=m_sg_w, m_sg_b=m_sg_b, m_w_out=m_w_out, m_norm2_g=m_norm2_g, m_w_ff1=m_w_ff1, m_w_ff2=m_w_ff2, v_norm1_g=v_norm1_g, v_w_in=v_w_in, v_conv_w=v_conv_w, v_a_log=v_a_log, v_dt_bias=v_dt_bias, v_dn_out_g=v_dn_out_g, v_sb_q_g=v_sb_q_g, v_sb_k_g=v_sb_k_g, v_sg_v_g=v_sg_v_g, v_sg_w=v_sg_w, v_sg_b=v_sg_b, v_w_out=v_w_out, v_norm2_g=v_norm2_g, v_w_ff1=v_w_ff1, v_w_ff2=v_w_ff2)
    weights = {n: given[n] for n in TWIN_WEIGHTS}
    shared = {n: given[n] for n in SHARED_INPUTS}
    per_example = {n: given[n] for n in ['x']}
    grad_fn = _jax.value_and_grad(_loss, argnums=(0, 1))

    def one_microbatch(ex, loss_target):
        ex = dict(ex)
        diff = ex.pop(TWIN_DIFF_INPUT)
        return grad_fn(weights, diff, {**shared, **ex}, loss_target)

    if N_MICROBATCH == 1:
        loss, (grad_w, grad_x) = one_microbatch(per_example, given["loss_target"])
    else:
        def body(carry, xs):
            loss_sum, grad_sum = carry
            l_k, (gw_k, gx_k) = one_microbatch(xs[0], xs[1])
            with _jax.named_scope("update"):
                return (loss_sum + l_k, _jax.tree.map(_jnp.add, grad_sum, gw_k)), gx_k

        init = (_jnp.zeros((), _jnp.float32), _jax.tree.map(_jnp.zeros_like, weights))
        (loss, grad_w), grad_x = _jax.lax.scan(body, init, (per_example, given["loss_target"]))
    with _jax.named_scope("update"):
        delta_w, new_m, new_v = {}, {}, {}
        for n in TWIN_WEIGHTS:
            delta_w[n], new_m[n], new_v[n] = _adamw(weights[n], grad_w[n], given["m_" + n], given["v_" + n])
    return (loss, grad_x, *[grad_w[n] for n in TWIN_WEIGHTS], *[delta_w[n] for n in TWIN_WEIGHTS],
            *[new_m[n] for n in TWIN_WEIGHTS], *[new_v[n] for n in TWIN_WEIGHTS])
```

```python
import functools

import jax
import jax.numpy as jnp
from jax import lax
from jax.experimental import pallas as pl
from jax.experimental.pallas import tpu as pltpu

F32, BF16 = jnp.float32, jnp.bfloat16
HI = lax.Precision.HIGHEST
EPS = 1e-6
LANES = 128
D = 1024
DFF = 4096
NH = 4
DN_W, SB_W, SG_W = 512, 256, 256
IN_DIM = 3336
NPACK = 3456
CB_Q, CB_K, CB_V, CB_Z, CB_AB, CB_SBQ, CB_SBK, CB_SBV, CB_SGU, CB_SGV = 0, 4, 8, 12, 16, 17, 19, 21, 23, 25
SB_SCALE = 64 ** -0.5
DN_SCALE = 128 ** -0.5
NDEV = 8
VMEM_LIMIT = 56 * 1024 * 1024
ADAM_LR, ADAM_B1, ADAM_B2, ADAM_EPS, ADAM_WD, ADAM_STEP = 0.001, 0.9, 0.999, 1e-08, 0.01, 10
MESH = pl.DeviceIdType.MESH


def _iota(shape, dim):
    return lax.broadcasted_iota(jnp.int32, shape, dim)


def _params(**kw):
    return pltpu.CompilerParams(vmem_limit_bytes=VMEM_LIMIT, **kw)


def _mm(a, b, dims):
    return lax.dot_general(a.astype(BF16), b.astype(BF16), (dims, ((), ())), preferred_element_type=F32)


@jax.custom_vjp
def _dot(a, b):
    return _mm(a, b, ((1,), (0,)))


def _dot_fwd(a, b):
    return _dot(a, b), (a, b)


def _dot_bwd(res, g):
    a, b = res
    return _mm(g, b, ((1,), (1,))).astype(a.dtype), _mm(a, g, ((0,), (0,))).astype(b.dtype)


_dot.defvjp(_dot_fwd, _dot_bwd)


@jax.custom_vjp
def _dot_nt(a, b):
    return _mm(a, b, ((1,), (1,)))


def _dot_nt_fwd(a, b):
    return _dot_nt(a, b), (a, b)


def _dot_nt_bwd(res, g):
    a, b = res
    return _mm(g, b, ((1,), (0,))).astype(a.dtype), _mm(g, a, ((0,), (0,))).astype(b.dtype)


_dot_nt.defvjp(_dot_nt_fwd, _dot_nt_bwd)


def _dotf(a, b):
    return jnp.dot(a, b, precision=HI, preferred_element_type=F32)


def _sigmoid(x):
    return jax.nn.sigmoid(x)


def _silu(x):
    return x * _sigmoid(x)


def _softplus(x):
    return jnp.maximum(x, 0.0) + jnp.log1p(jnp.exp(-jnp.abs(x)))


def _gelu(x):
    return 0.5 * x * (1.0 + jnp.tanh(0.7978845608028654 * (x + 0.044715 * (x * x * x))))


def _rms(x, gain):
    return x * lax.rsqrt(jnp.mean(x * x, axis=-1, keepdims=True) + EPS) * gain


def _shift_down_impl(x, k):
    return jnp.where(_iota(x.shape, 0) >= k, pltpu.roll(x, k, 0), 0.0)


def _shift_up_impl(x, k):
    n = x.shape[0]
    return jnp.where(_iota(x.shape, 0) < n - k, pltpu.roll(x, n - k, 0), 0.0)


@functools.partial(jax.custom_vjp, nondiff_argnums=(1,))
def _shift_down(x, k):
    return _shift_down_impl(x, k)


def _shift_down_fwd(x, k):
    return _shift_down_impl(x, k), None


def _shift_down_bwd(k, _, g):
    return (_shift_up_impl(g, k),)


_shift_down.defvjp(_shift_down_fwd, _shift_down_bwd)


def _dn_conv(x, w0, w1, w2, w3, l2_scale):
    y = _silu(w3 * x + w2 * _shift_down(x, 1) + w1 * _shift_down(x, 2) + w0 * _shift_down(x, 3))
    if l2_scale is None:
        return y
    return y * lax.rsqrt(jnp.sum(y * y, axis=-1, keepdims=True) + EPS) * l2_scale


def _dn_gate(a, b, a_log, dt_bias):
    return -jnp.exp(a_log) * _softplus(a + dt_bias), _sigmoid(b)


def _dn_pair(q, k, v, g, beta, s0):
    n = 128
    row, col = _iota((n, n), 0), _iota((n, n), 1)
    same = (row // 64) == (col // 64)
    tri_incl = jnp.logical_and(same, col <= row)
    tri_strict = jnp.logical_and(same, col < row)
    first = row < 64
    gb = jnp.broadcast_to(g, (n, n))
    gc = _dotf(jnp.where(tri_incl, 1.0, 0.0), gb)
    gl = _dotf(jnp.where(same, 1.0, 0.0), gb)
    diff = gc - gc.T
    decay = jnp.where(tri_incl, jnp.exp(jnp.where(tri_incl, diff, 0.0)), 0.0)
    egc = jnp.exp(gc)
    kk = _dotf(k, k.T)
    lower = jnp.where(tri_strict, beta * kk * decay, 0.0)
    nk = -lower
    inv = jnp.where(row == col, 1.0, 0.0) + nk
    for _ in range(5):
        nk = _dotf(nk, nk)
        inv = inv + _dotf(inv, nk)
    u_val = _dotf(inv, v * beta)
    w_dec = _dotf(inv, k * (beta * egc))
    qk = jnp.where(tri_incl, _dotf(q, k.T) * decay, 0.0)
    q_dec = q * egc
    k_dec = k * jnp.exp(gl - gc)
    cd1 = jnp.exp(jnp.sum(jnp.where(first, gb, 0.0), axis=0, keepdims=True))
    cd2 = jnp.exp(jnp.sum(jnp.where(first, 0.0, gb), axis=0, keepdims=True))
    u1 = u_val - _dotf(w_dec, s0)
    s1 = s0 * cd1 + _dotf(jnp.where(first, k_dec, 0.0).T, u1)
    u2 = u_val - _dotf(w_dec, s1)
    u_new = jnp.where(first, u1, u2)
    s2 = s1 * cd2 + _dotf(jnp.where(first, 0.0, k_dec).T, u_new)
    o = jnp.where(first, _dotf(q_dec, s0), _dotf(q_dec, s1)) + _dotf(qk, u_new)
    return o, s2


def _dn_post(o, z, gain):
    return _rms(o, gain) * _silu(z)


def _lane_pick(x, idx):
    return jnp.sum(jnp.where(_iota(x.shape, 1) == idx, x, 0.0), axis=-1, keepdims=True)


def _taps(cw_ref):
    return tuple(cw_ref[i:i + 1, :] for i in range(4))


def _dn_gate_in(ab_ref, alog_ref, dtb_ref, h):
    ab = ab_ref[...]
    return _lane_pick(ab, h), _lane_pick(ab, h + NH), _lane_pick(alog_ref[...], h), _lane_pick(dtb_ref[...], h)


_DN_L2 = (DN_SCALE, 1.0, None)


def _dn_fwd_call(proj, conv_w, a_log, dt_bias, gain, B, T):
    npair = T // 128

    def body(q_ref, k_ref, v_ref, z_ref, ab_ref, cwq_ref, cwk_ref, cwv_ref, alog_ref, dtb_ref, gain_ref, out_ref,
             q_s, k_s, v_s, g_s, b_s, o_s):
        h = pl.program_id(1)
        for x_ref, cw_ref, x_s, l2 in zip((q_ref, k_ref, v_ref), (cwq_ref, cwk_ref, cwv_ref), (q_s, k_s, v_s), _DN_L2):
            x_s[...] = _dn_conv(x_ref[...], *_taps(cw_ref), l2)
        g_s[...], b_s[...] = _dn_gate(*_dn_gate_in(ab_ref, alog_ref, dtb_ref, h))

        def step(i, s):
            r = pl.ds(pl.multiple_of(i * 128, 128), 128)
            o, s2 = _dn_pair(q_s[r, :], k_s[r, :], v_s[r, :], g_s[r, :], b_s[r, :], s)
            o_s[r, :] = o
            return s2

        lax.fori_loop(0, npair, step, jnp.zeros((128, 128), F32))
        out_ref[...] = _dn_post(o_s[...], z_ref[...], gain_ref[...])

    col = lambda c0: pl.BlockSpec((T, LANES), lambda b, h, c0=c0: (b, c0 + h))
    cw = lambda c0: pl.BlockSpec((4, LANES), lambda b, h, c0=c0: (0, c0 + h))
    vec = pl.BlockSpec((1, LANES), lambda b, h: (0, 0))
    return pl.pallas_call(
        body, name="dn_fwd", grid=(B, NH),
        in_specs=[col(CB_Q), col(CB_K), col(CB_V), col(CB_Z), pl.BlockSpec((T, LANES), lambda b, h: (b, CB_AB)),
                  cw(0), cw(4), cw(8), vec, vec, vec],
        out_specs=pl.BlockSpec((T, LANES), lambda b, h: (b, h)),
        out_shape=jax.ShapeDtypeStruct((B * T, DN_W), F32),
        scratch_shapes=[pltpu.VMEM((T, LANES), F32)] * 3 + [pltpu.VMEM((T, 1), F32)] * 2 + [pltpu.VMEM((T, LANES), F32)],
        compiler_params=_params(dimension_semantics=("arbitrary", "arbitrary")),
    )(proj, proj, proj, proj, proj, conv_w, conv_w, conv_w, a_log, dt_bias, gain)


def _dn_bwd_call(proj, dmix, conv_w, a_log, dt_bias, gain, B, T):
    npair = T // 128

    def body(q_ref, k_ref, v_ref, z_ref, ab_ref, cwq_ref, cwk_ref, cwv_ref, alog_ref, dtb_ref, gain_ref, do_ref,
             dq_ref, dk_ref, dv_ref, dz_ref, dab_ref, dcw_ref, dalog_ref, ddtb_ref, dgain_ref,
             q_s, k_s, v_s, g_s, b_s, o_s, st_s, dq_s, dk_s, dv_s, dg_s, db_s):
        b_i, h = pl.program_id(0), pl.program_id(1)
        pre = tuple(zip((q_ref, k_ref, v_ref), (cwq_ref, cwk_ref, cwv_ref), (q_s, k_s, v_s), _DN_L2))
        for x_ref, cw_ref, x_s, l2 in pre:
            x_s[...] = _dn_conv(x_ref[...], *_taps(cw_ref), l2)
        g_s[...], b_s[...] = _dn_gate(*_dn_gate_in(ab_ref, alog_ref, dtb_ref, h))

        def fstep(i, s):
            r = pl.ds(pl.multiple_of(i * 128, 128), 128)
            st_s[i] = s
            o, s2 = _dn_pair(q_s[r, :], k_s[r, :], v_s[r, :], g_s[r, :], b_s[r, :], s)
            o_s[r, :] = o
            return s2

        lax.fori_loop(0, npair, fstep, jnp.zeros((128, 128), F32))
        _, post_vjp = jax.vjp(_dn_post, o_s[...], z_ref[...], gain_ref[...])
        do, dz, dgain = post_vjp(do_ref[...])
        dz_ref[...] = dz
        o_s[...] = do

        def bstep(ii, ds):
            i = npair - 1 - ii
            r = pl.ds(pl.multiple_of(i * 128, 128), 128)
            _, pair_vjp = jax.vjp(_dn_pair, q_s[r, :], k_s[r, :], v_s[r, :], g_s[r, :], b_s[r, :], st_s[i])
            dq, dk, dv, dg, db, ds0 = pair_vjp((o_s[r, :], ds))
            dq_s[r, :], dk_s[r, :], dv_s[r, :], dg_s[r, :], db_s[r, :] = dq, dk, dv, dg, db
            return ds0

        lax.fori_loop(0, npair, bstep, jnp.zeros((128, 128), F32))
        @pl.when(jnp.logical_and(b_i == 0, h == 0))
        def _():
            dcw_ref[...] = jnp.zeros_like(dcw_ref)
            dalog_ref[...] = jnp.zeros_like(dalog_ref)
            ddtb_ref[...] = jnp.zeros_like(ddtb_ref)
            dgain_ref[...] = jnp.zeros_like(dgain_ref)

        for n, ((x_ref, cw_ref, _x_s, l2), dx_s, dx_ref) in enumerate(zip(pre, (dq_s, dk_s, dv_s), (dq_ref, dk_ref, dv_ref))):
            _, conv_vjp = jax.vjp(functools.partial(_dn_conv, l2_scale=l2), x_ref[...], *_taps(cw_ref))
            dx, *dw = conv_vjp(dx_s[...])
            dx_ref[...] = dx
            for i in range(4):
                dcw_ref[h + 4 * n, i:i + 1, :] += dw[i]
        _, gate_vjp = jax.vjp(_dn_gate, *_dn_gate_in(ab_ref, alog_ref, dtb_ref, h))
        da, db, dalog, ddtb = gate_vjp((dg_s[...], db_s[...]))
        lane = _iota((1, LANES), 1)
        dab = jnp.where(lane == h, da, 0.0) + jnp.where(lane == h + NH, db, 0.0)

        @pl.when(h == 0)
        def _():
            dab_ref[...] = jnp.zeros_like(dab_ref)

        dab_ref[...] += dab
        dalog_ref[...] += jnp.where(lane == h, dalog, 0.0)
        ddtb_ref[...] += jnp.where(lane == h, ddtb, 0.0)
        dgain_ref[...] += dgain

    M = B * T
    col = lambda c0: pl.BlockSpec((T, LANES), lambda b, h, c0=c0: (b, c0 + h))
    cw = lambda c0: pl.BlockSpec((4, LANES), lambda b, h, c0=c0: (0, c0 + h))
    vec = pl.BlockSpec((1, LANES), lambda b, h: (0, 0))
    big = [pltpu.VMEM((T, LANES), F32)]
    small = [pltpu.VMEM((T, 1), F32)]
    dqkvz = jax.ShapeDtypeStruct((M, DN_W), F32)
    vec_shape = jax.ShapeDtypeStruct((1, LANES), F32)
    return pl.pallas_call(
        body, name="dn_bwd", grid=(B, NH),
        in_specs=[col(CB_Q), col(CB_K), col(CB_V), col(CB_Z), pl.BlockSpec((T, LANES), lambda b, h: (b, CB_AB)),
                  cw(0), cw(4), cw(8), vec, vec, vec, pl.BlockSpec((T, LANES), lambda b, h: (b, h))],
        out_specs=[pl.BlockSpec((T, LANES), lambda b, h: (b, h))] * 4
        + [pl.BlockSpec((T, LANES), lambda b, h: (b, 0)), pl.BlockSpec((12, 4, LANES), lambda b, h: (0, 0, 0)), vec, vec, vec],
        out_shape=[dqkvz] * 4 + [jax.ShapeDtypeStruct((M, LANES), F32), jax.ShapeDtypeStruct((12, 4, LANES), F32),
                                 vec_shape, vec_shape, vec_shape],
        scratch_shapes=big * 3 + small * 2 + big + [pltpu.VMEM((npair, 128, 128), F32)] + big * 3 + small * 2,
        compiler_params=_params(dimension_semantics=("arbitrary", "arbitrary")),
    )(proj, proj, proj, proj, proj, conv_w, conv_w, conv_w, a_log, dt_bias, gain, dmix)


def _group_rms(x, gain):
    first = _iota(x.shape, 1) < 64
    sq = x * x
    ss_a = jnp.sum(jnp.where(first, sq, 0.0), axis=-1, keepdims=True)
    ss_b = jnp.sum(jnp.where(first, 0.0, sq), axis=-1, keepdims=True)
    ms = jnp.where(first, ss_a, ss_b) * (1.0 / 64)
    return x * lax.rsqrt(ms + EPS) * gain


def _sb_block(q, k, v, carry, head_lanes, diag):
    n = 128
    row, col = _iota((n, n), 0), _iota((n, n), 1)
    z = _dot_nt(jnp.where(head_lanes, q, 0.0), k) * SB_SCALE
    mask = jnp.logical_or(col < row, jnp.logical_not(diag))
    soft = jnp.log1p(jnp.exp(-jnp.abs(z)))
    ls_pos = jnp.minimum(z, 0.0) - soft
    l1m = jnp.where(mask, jnp.minimum(-z, 0.0) - soft, 0.0)
    rem = _dotf(l1m, jnp.where(row > col, 1.0, 0.0)) + carry
    w = jnp.where(mask, jnp.exp(ls_pos + rem), 0.0)
    o = jnp.where(head_lanes, _dot(w, v), 0.0)
    return o, carry + jnp.sum(l1m, axis=-1, keepdims=True)


def _sb_fwd_call(proj, q_gain, k_gain, B, T):
    nblk = T // 128

    def body(q_ref, k_ref, v_ref, qg_ref, kg_ref, out_ref, q_s, k_s):
        q_s[...] = _group_rms(q_ref[...], qg_ref[...])
        k_s[...] = _group_rms(k_ref[...], kg_ref[...])
        lane = _iota((1, LANES), 1)

        def qblock(i, _):
            ri = pl.ds(pl.multiple_of(i * 128, 128), 128)
            q = q_s[ri, :]
            acc = jnp.zeros((128, LANES), F32)
            for head_lanes in (lane < 64, lane >= 64):
                def kblock(jj, c):
                    j = i - jj
                    rj = pl.ds(pl.multiple_of(j * 128, 128), 128)
                    o, carry = _sb_block(q, k_s[rj, :], v_ref[rj, :], c[1], head_lanes, j == i)
                    return c[0] + o, carry

                acc, _c = lax.fori_loop(0, i + 1, kblock, (acc, jnp.zeros((128, 1), F32)))
            out_ref[ri, :] = acc
            return 0

        lax.fori_loop(0, nblk, qblock, 0)

    col = lambda c0: pl.BlockSpec((T, LANES), lambda b, p, c0=c0: (b, c0 + p))
    vec = pl.BlockSpec((1, LANES), lambda b, p: (0, 0))
    return pl.pallas_call(
        body, name="sb_fwd", grid=(B, 2),
        in_specs=[col(CB_SBQ), col(CB_SBK), col(CB_SBV), vec, vec],
        out_specs=pl.BlockSpec((T, LANES), lambda b, p: (b, p)),
        out_shape=jax.ShapeDtypeStruct((B * T, SB_W), F32),
        scratch_shapes=[pltpu.VMEM((T, LANES), F32)] * 2,
        compiler_params=_params(dimension_semantics=("arbitrary", "arbitrary")),
    )(proj, proj, proj, q_gain, k_gain)


def _sb_bwd_call(proj, dmix, q_gain, k_gain, B, T):
    nblk = T // 128

    def body(q_ref, k_ref, v_ref, qg_ref, kg_ref, do_ref, dq_ref, dk_ref, dv_ref, dqg_ref, dkg_ref,
             q_s, k_s, dq_s, dk_s, dv_s, c_s):
        b_i, p = pl.program_id(0), pl.program_id(1)
        qn, q_vjp = jax.vjp(_group_rms, q_ref[...], qg_ref[...])
        kn, k_vjp = jax.vjp(_group_rms, k_ref[...], kg_ref[...])
        q_s[...], k_s[...] = qn, kn
        dk_s[...] = jnp.zeros_like(dk_s)
        dv_s[...] = jnp.zeros_like(dv_s)
        lane = _iota((1, LANES), 1)

        def qblock(i, _):
            ri = pl.ds(pl.multiple_of(i * 128, 128), 128)
            q = q_s[ri, :]
            do = do_ref[ri, :]
            dq = jnp.zeros((128, LANES), F32)
            for head_lanes in (lane < 64, lane >= 64):
                def carries(jj, carry):
                    j = i - jj
                    rj = pl.ds(pl.multiple_of(j * 128, 128), 128)
                    c_s[j] = jnp.broadcast_to(carry, (128, LANES))
                    _o, c2 = _sb_block(q, k_s[rj, :], v_ref[rj, :], carry, head_lanes, j == i)
                    return c2

                lax.fori_loop(0, i + 1, carries, jnp.zeros((128, 1), F32))

                def kblock(j, c):
                    rj = pl.ds(pl.multiple_of(j * 128, 128), 128)
                    f = lambda q_, k_, v_, c_: _sb_block(q_, k_, v_, c_, head_lanes, j == i)
                    _, vjp = jax.vjp(f, q, k_s[rj, :], v_ref[rj, :], c_s[j, :, 0:1])
                    dq_j, dk_j, dv_j, dc = vjp((do, c[1]))
                    dk_s[rj, :] += dk_j
                    dv_s[rj, :] += dv_j
                    return c[0] + dq_j, dc

                dq, _dc = lax.fori_loop(0, i + 1, kblock, (dq, jnp.zeros((128, 1), F32)))
            dq_s[ri, :] = dq
            return 0

        lax.fori_loop(0, nblk, qblock, 0)
        dq_in, dqg = q_vjp(dq_s[...])
        dk_in, dkg = k_vjp(dk_s[...])
        dq_ref[...], dk_ref[...], dv_ref[...] = dq_in, dk_in, dv_s[...]

        @pl.when(jnp.logical_and(b_i == 0, p == 0))
        def _():
            dqg_ref[...] = jnp.zeros_like(dqg_ref)
            dkg_ref[...] = jnp.zeros_like(dkg_ref)

        dqg_ref[...] += dqg + pltpu.roll(dqg, 64, 1)
        dkg_ref[...] += dkg + pltpu.roll(dkg, 64, 1)

    M = B * T
    col = lambda c0: pl.BlockSpec((T, LANES), lambda b, p, c0=c0: (b, c0 + p))
    vec = pl.BlockSpec((1, LANES), lambda b, p: (0, 0))
    blk = pl.BlockSpec((T, LANES), lambda b, p: (b, p))
    big = [pltpu.VMEM((T, LANES), F32)]
    return pl.pallas_call(
        body, name="sb_bwd", grid=(B, 2),
        in_specs=[col(CB_SBQ), col(CB_SBK), col(CB_SBV), vec, vec, pl.BlockSpec((T, LANES), lambda b, p: (b, 4 + p))],
        out_specs=[blk, blk, blk, vec, vec],
        out_shape=[jax.ShapeDtypeStruct((M, SB_W), F32)] * 3 + [jax.ShapeDtypeStruct((1, LANES), F32)] * 2,
        scratch_shapes=big * 5 + [pltpu.VMEM((nblk, 128, LANES), F32)],
        compiler_params=_params(dimension_semantics=("arbitrary", "arbitrary")),
    )(proj, proj, proj, q_gain, k_gain, dmix)


def _sg_chunk(u, v, gain, w_a, w_b, bias):
    n = 128
    row, col = _iota((n, n), 0), _iota((n, n), 1)
    first = _iota((1, LANES), 1) < 64
    vn = _group_rms(_gelu(v), gain)
    tril = col <= row
    mixed = jnp.where(first, _dot(jnp.where(tril, w_a, 0.0), vn), _dot(jnp.where(tril, w_b, 0.0), vn)) + bias
    return _gelu(u) * mixed


def _sg_fwd_call(proj, gain, sg_w, bias, B, T):
    nchunk = T // 128

    def body(u_ref, v_ref, g_ref, wa_ref, wb_ref, bias_ref, out_ref):
        def step(i, _):
            r = pl.ds(pl.multiple_of(i * 128, 128), 128)
            out_ref[r, :] = _sg_chunk(u_ref[r, :], v_ref[r, :], g_ref[...], wa_ref[0], wb_ref[0], bias_ref[...])
            return 0

        lax.fori_loop(0, nchunk, step, 0)

    col = lambda c0: pl.BlockSpec((T, LANES), lambda b, p, c0=c0: (b, c0 + p))
    return pl.pallas_call(
        body, name="sg_fwd", grid=(B, 2),
        in_specs=[col(CB_SGU), col(CB_SGV), pl.BlockSpec((1, LANES), lambda b, p: (0, p)),
                  pl.BlockSpec((1, 128, 128), lambda b, p: (2 * p, 0, 0)), pl.BlockSpec((1, 128, 128), lambda b, p: (2 * p + 1, 0, 0)),
                  pl.BlockSpec((128, LANES), lambda b, p: (0, p))],
        out_specs=pl.BlockSpec((T, LANES), lambda b, p: (b, p)),
        out_shape=jax.ShapeDtypeStruct((B * T, SG_W), F32),
        compiler_params=_params(dimension_semantics=("arbitrary", "arbitrary")),
    )(proj, proj, gain, sg_w, sg_w, bias)


def _sg_bwd_call(proj, dmix, gain, sg_w, bias, B, T):
    nchunk = T // 128

    def body(u_ref, v_ref, g_ref, wa_ref, wb_ref, bias_ref, do_ref, du_ref, dv_ref, dg_ref, dw_ref, db_ref):
        p, b_i = pl.program_id(0), pl.program_id(1)

        def step(i, c):
            r = pl.ds(pl.multiple_of(i * 128, 128), 128)
            _, vjp = jax.vjp(_sg_chunk, u_ref[r, :], v_ref[r, :], g_ref[...], wa_ref[0], wb_ref[0], bias_ref[...])
            du, dv, dg, dwa, dwb, dbias = vjp(do_ref[r, :])
            du_ref[r, :], dv_ref[r, :] = du, dv
            return c[0] + dg, c[1] + dwa, c[2] + dwb, c[3] + dbias

        z = jnp.zeros((128, 128), F32)
        dg, dwa, dwb, dbias = lax.fori_loop(0, nchunk, step, (jnp.zeros((1, LANES), F32), z, z, z))
        lane = _iota((1, LANES), 1)
        first = lane < 64
        s_a = jnp.sum(jnp.where(first, dbias, 0.0), axis=-1, keepdims=True)
        s_b = jnp.sum(jnp.where(first, 0.0, dbias), axis=-1, keepdims=True)
        dbg = jnp.where(lane == 2 * p, s_a, 0.0) + jnp.where(lane == 2 * p + 1, s_b, 0.0)

        @pl.when(b_i == 0)
        def _():
            dg_ref[...] = jnp.zeros_like(dg_ref)
            dw_ref[...] = jnp.zeros_like(dw_ref)

        @pl.when(jnp.logical_and(b_i == 0, p == 0))
        def _():
            db_ref[...] = jnp.zeros_like(db_ref)

        dg_ref[...] += dg
        dw_ref[0] += dwa
        dw_ref[1] += dwb
        db_ref[...] += dbg

    M = B * T
    col = lambda c0: pl.BlockSpec((T, LANES), lambda p, b, c0=c0: (b, c0 + p))
    blk = pl.BlockSpec((T, LANES), lambda p, b: (b, p))
    return pl.pallas_call(
        body, name="sg_bwd", grid=(2, B),
        in_specs=[col(CB_SGU), col(CB_SGV), pl.BlockSpec((1, LANES), lambda p, b: (0, p)),
                  pl.BlockSpec((1, 128, 128), lambda p, b: (2 * p, 0, 0)), pl.BlockSpec((1, 128, 128), lambda p, b: (2 * p + 1, 0, 0)),
                  pl.BlockSpec((128, LANES), lambda p, b: (0, p)), pl.BlockSpec((T, LANES), lambda p, b: (b, 6 + p))],
        out_specs=[blk, blk, pl.BlockSpec((1, LANES), lambda p, b: (0, p)), pl.BlockSpec((2, 128, 128), lambda p, b: (p, 0, 0)),
                   pl.BlockSpec((128, LANES), lambda p, b: (0, 0))],
        out_shape=[jax.ShapeDtypeStruct((M, SG_W), F32)] * 2 + [jax.ShapeDtypeStruct((1, SG_W), F32),
                                                                jax.ShapeDtypeStruct((4, 128, 128), F32),
                                                                jax.ShapeDtypeStruct((128, LANES), F32)],
        compiler_params=_params(dimension_semantics=("arbitrary", "arbitrary")),
    )(proj, proj, gain, sg_w, sg_w, bias, dmix)


def _row_tile(m):
    return min(m, 512)


def _col_tile(n):
    if n <= 1152:
        return n
    return 1024 if n % 1024 == 0 else 1152


def _in_proj_call(x, gain, w):
    m, n = x.shape[0], w.shape[1]
    tm, tn = _row_tile(m), _col_tile(n)

    def body(x_ref, g_ref, w_ref, out_ref, h_ref):
        @pl.when(pl.program_id(1) == 0)
        def _():
            h_ref[...] = _rms(x_ref[...], g_ref[...]).astype(BF16)

        out_ref[...] = jnp.dot(h_ref[...], w_ref[...], preferred_element_type=F32)

    return pl.pallas_call(
        body, name="in_proj", grid=(m // tm, n // tn),
        in_specs=[pl.BlockSpec((tm, D), lambda i, j: (i, 0)), pl.BlockSpec((1, D), lambda i, j: (0, 0)),
                  pl.BlockSpec((D, tn), lambda i, j: (0, j))],
        out_specs=[pl.BlockSpec((tm, tn), lambda i, j: (i, j)), pl.BlockSpec((tm, D), lambda i, j: (i, 0))],
        out_shape=[jax.ShapeDtypeStruct((m, n), F32), jax.ShapeDtypeStruct((m, D), BF16)],
        compiler_params=_params(dimension_semantics=("arbitrary", "arbitrary")),
    )(x, gain, w)


def _out_proj_call(a, w, res):
    m, k = a.shape
    n = w.shape[1]
    tm = _row_tile(m)

    def body(a_ref, w_ref, res_ref, out_ref):
        out_ref[...] = res_ref[...] + jnp.dot(a_ref[...].astype(BF16), w_ref[...], preferred_element_type=F32)

    return pl.pallas_call(
        body, name="out_proj", grid=(m // tm,),
        in_specs=[pl.BlockSpec((tm, k), lambda i: (i, 0)), pl.BlockSpec((k, n), lambda i: (0, 0)),
                  pl.BlockSpec((tm, n), lambda i: (i, 0))],
        out_specs=pl.BlockSpec((tm, n), lambda i: (i, 0)),
        out_shape=jax.ShapeDtypeStruct((m, n), F32),
        compiler_params=_params(dimension_semantics=("arbitrary",)),
    )(a, w, res)


def _ffn_fwd_call(x, gain, w1, w2):
    m = x.shape[0]
    tm, tf = _row_tile(m), 1024
    nf = DFF // tf

    def body(x_ref, g_ref, w1_ref, w2_ref, out_ref, h_s, acc_s):
        j = pl.program_id(1)

        @pl.when(j == 0)
        def _():
            h_s[...] = _rms(x_ref[...], g_ref[...]).astype(BF16)
            acc_s[...] = jnp.zeros_like(acc_s)

        a = jnp.maximum(jnp.dot(h_s[...], w1_ref[...], preferred_element_type=F32), 0.0)
        acc_s[...] += jnp.dot((a * a).astype(BF16), w2_ref[...], preferred_element_type=F32)

        @pl.when(j == nf - 1)
        def _():
            out_ref[...] = x_ref[...] + acc_s[...]

    return pl.pallas_call(
        body, name="ffn_fwd", grid=(m // tm, nf),
        in_specs=[pl.BlockSpec((tm, D), lambda i, j: (i, 0)), pl.BlockSpec((1, D), lambda i, j: (0, 0)),
                  pl.BlockSpec((D, tf), lambda i, j: (0, j)), pl.BlockSpec((tf, D), lambda i, j: (j, 0))],
        out_specs=pl.BlockSpec((tm, D), lambda i, j: (i, 0)),
        out_shape=jax.ShapeDtypeStruct((m, D), F32),
        scratch_shapes=[pltpu.VMEM((tm, D), BF16), pltpu.VMEM((tm, D), F32)],
        compiler_params=_params(dimension_semantics=("arbitrary", "arbitrary")),
    )(x, gain, w1, w2)


def _ffn_bwd_call(x, dy, gain, w1, w2):
    m = x.shape[0]
    tm, tf = _row_tile(m), 1024
    nf = DFF // tf

    def body(x_ref, dy_ref, g_ref, w1_ref, w2_ref, dx_ref, da_ref, r_ref, h_ref, dg_ref, acc_s):
        i, j = pl.program_id(0), pl.program_id(1)

        @pl.when(j == 0)
        def _():
            h_ref[...] = _rms(x_ref[...], g_ref[...]).astype(BF16)
            acc_s[...] = jnp.zeros_like(acc_s)

        @pl.when(jnp.logical_and(i == 0, j == 0))
        def _():
            dg_ref[...] = jnp.zeros_like(dg_ref)

        a = jnp.maximum(jnp.dot(h_ref[...], w1_ref[...], preferred_element_type=F32), 0.0)
        r_ref[...] = (a * a).astype(BF16)
        dr = lax.dot_general(dy_ref[...].astype(BF16), w2_ref[...], (((1,), (1,)), ((), ())), preferred_element_type=F32)
        da = (dr * (2.0 * a)).astype(BF16)
        da_ref[...] = da
        acc_s[...] += lax.dot_general(da, w1_ref[...], (((1,), (1,)), ((), ())), preferred_element_type=F32)

        @pl.when(j == nf - 1)
        def _():
            _, vjp = jax.vjp(_rms, x_ref[...], g_ref[...])
            dx, dg = vjp(acc_s[...])
            dx_ref[...] = dy_ref[...] + dx
            dg_ref[...] += dg

    return pl.pallas_call(
        body, name="ffn_bwd", grid=(m // tm, nf),
        in_specs=[pl.BlockSpec((tm, D), lambda i, j: (i, 0)), pl.BlockSpec((tm, D), lambda i, j: (i, 0)),
                  pl.BlockSpec((1, D), lambda i, j: (0, 0)),
                  pl.BlockSpec((D, tf), lambda i, j: (0, j)), pl.BlockSpec((tf, D), lambda i, j: (j, 0))],
        out_specs=[pl.BlockSpec((tm, D), lambda i, j: (i, 0)), pl.BlockSpec((tm, tf), lambda i, j: (i, j)),
                   pl.BlockSpec((tm, tf), lambda i, j: (i, j)), pl.BlockSpec((tm, D), lambda i, j: (i, 0)),
                   pl.BlockSpec((1, D), lambda i, j: (0, 0))],
        out_shape=[jax.ShapeDtypeStruct((m, D), F32), jax.ShapeDtypeStruct((m, DFF), BF16), jax.ShapeDtypeStruct((m, DFF), BF16),
                   jax.ShapeDtypeStruct((m, D), BF16), jax.ShapeDtypeStruct((1, D), F32)],
        scratch_shapes=[pltpu.VMEM((tm, D), F32)],
        compiler_params=_params(dimension_semantics=("arbitrary", "arbitrary")),
    )(x, dy, gain, w1, w2)


def _mm_nt_call(a, b, name):
    m, k = a.shape
    n = b.shape[0]
    tm = _row_tile(m)

    def body(a_ref, b_ref, out_ref):
        out_ref[...] = lax.dot_general(a_ref[...].astype(BF16), b_ref[...].astype(BF16), (((1,), (1,)), ((), ())),
                                       preferred_element_type=F32)

    return pl.pallas_call(
        body, name=name, grid=(m // tm,),
        in_specs=[pl.BlockSpec((tm, k), lambda i: (i, 0)), pl.BlockSpec((n, k), lambda i: (0, 0))],
        out_specs=pl.BlockSpec((tm, n), lambda i: (i, 0)),
        out_shape=jax.ShapeDtypeStruct((m, n), F32),
        compiler_params=_params(dimension_semantics=("arbitrary",)),
    )(a, b)


def _in_proj_bwd_call(dproj, w, x, gain, dres):
    m, n = dproj.shape
    tm = _row_tile(m)

    def body(dp_ref, w_ref, x_ref, g_ref, dres_ref, dx_ref, dg_ref):
        @pl.when(pl.program_id(0) == 0)
        def _():
            dg_ref[...] = jnp.zeros_like(dg_ref)

        dh = lax.dot_general(dp_ref[...].astype(BF16), w_ref[...], (((1,), (1,)), ((), ())), preferred_element_type=F32)
        _, vjp = jax.vjp(_rms, x_ref[...], g_ref[...])
        dx, dg = vjp(dh)
        dx_ref[...] = dres_ref[...] + dx
        dg_ref[...] += dg

    return pl.pallas_call(
        body, name="in_proj_bwd", grid=(m // tm,),
        in_specs=[pl.BlockSpec((tm, n), lambda i: (i, 0)), pl.BlockSpec((D, n), lambda i: (0, 0)),
                  pl.BlockSpec((tm, D), lambda i: (i, 0)), pl.BlockSpec((1, D), lambda i: (0, 0)),
                  pl.BlockSpec((tm, D), lambda i: (i, 0))],
        out_specs=[pl.BlockSpec((tm, D), lambda i: (i, 0)), pl.BlockSpec((1, D), lambda i: (0, 0))],
        out_shape=[jax.ShapeDtypeStruct((m, D), F32), jax.ShapeDtypeStruct((1, D), F32)],
        compiler_params=_params(dimension_semantics=("arbitrary",)),
    )(dproj, w, x, gain, dres)


def _mm_tn_call(a, b, name):
    m, k = a.shape
    n = b.shape[1]
    tm, tk, tn = _row_tile(m), min(k, 1024), _col_tile(n)

    def body(a_ref, b_ref, out_ref):
        @pl.when(pl.program_id(2) == 0)
        def _():
            out_ref[...] = jnp.zeros_like(out_ref)

        out_ref[...] += lax.dot_general(a_ref[...].astype(BF16), b_ref[...].astype(BF16), (((0,), (0,)), ((), ())),
                                        preferred_element_type=F32)

    return pl.pallas_call(
        body, name=name, grid=(k // tk, n // tn, m // tm),
        in_specs=[pl.BlockSpec((tm, tk), lambda i, j, s: (s, i)), pl.BlockSpec((tm, tn), lambda i, j, s: (s, j))],
        out_specs=pl.BlockSpec((tk, tn), lambda i, j, s: (i, j)),
        out_shape=jax.ShapeDtypeStruct((k, n), F32),
        compiler_params=_params(dimension_semantics=("arbitrary", "arbitrary", "arbitrary")),
    )(a, b)


def _loss_call(y, target):
    m = y.shape[0]
    tm = _row_tile(m)

    def body(y_ref, t_ref, loss_ref, dy_ref):
        @pl.when(pl.program_id(0) == 0)
        def _():
            loss_ref[...] = jnp.zeros_like(loss_ref)

        err = y_ref[...] - t_ref[...]
        dy_ref[...] = err * (1.0 / D)
        per_row = jnp.mean(err * err, axis=-1, keepdims=True)
        loss_ref[...] += jnp.broadcast_to(0.5 * jnp.sum(per_row, axis=0, keepdims=True), (1, LANES))

    return pl.pallas_call(
        body, name="loss", grid=(m // tm,),
        in_specs=[pl.BlockSpec((tm, D), lambda i: (i, 0))] * 2,
        out_specs=[pl.BlockSpec((1, LANES), lambda i: (0, 0)), pl.BlockSpec((tm, D), lambda i: (i, 0))],
        out_shape=[jax.ShapeDtypeStruct((1, LANES), F32), jax.ShapeDtypeStruct((m, D), F32)],
        compiler_params=_params(dimension_semantics=("arbitrary",)),
    )(y, target)


def _adamw_call(w, g, m, v, name):
    shape = w.shape
    cols = shape[-1] if w.ndim > 1 else w.size
    rows = w.size // cols
    tr = rows if (rows <= 512 or rows % 512) else 512
    c1, c2 = 1.0 - ADAM_B1 ** ADAM_STEP, 1.0 - ADAM_B2 ** ADAM_STEP

    def body(w_ref, g_ref, m_ref, v_ref, d_ref, nm_ref, nv_ref):
        g_ = g_ref[...]
        nm = ADAM_B1 * m_ref[...] + (1.0 - ADAM_B1) * g_
        nv = ADAM_B2 * v_ref[...] + (1.0 - ADAM_B2) * (g_ * g_)
        d_ref[...] = -ADAM_LR * ((nm / c1) / (jnp.sqrt(nv / c2) + ADAM_EPS) + ADAM_WD * w_ref[...])
        nm_ref[...], nv_ref[...] = nm, nv

    spec = pl.BlockSpec((tr, cols), lambda i: (i, 0))
    outs = pl.pallas_call(
        body, name=name, grid=(rows // tr,), in_specs=[spec] * 4, out_specs=[spec] * 3,
        out_shape=[jax.ShapeDtypeStruct((rows, cols), F32)] * 3,
        compiler_params=_params(dimension_semantics=("arbitrary",)),
    )(*(t.reshape(rows, cols) for t in (w, g, m, v)))
    return tuple(o.reshape(shape) for o in outs)


def _sum_call(parts, out_dtype, name):
    rows, cols = parts[0].shape
    tr = rows
    for cand in (8368, 4096, 2048, 1024, 512):
        if rows > cand and rows % cand == 0:
            tr = cand
            break

    def body(*refs):
        acc = refs[0][...].astype(F32)
        for r in refs[1:-1]:
            acc = acc + r[...].astype(F32)
        refs[-1][...] = acc.astype(out_dtype)

    spec = pl.BlockSpec((tr, cols), lambda i: (i, 0))
    return pl.pallas_call(
        body, name=name, grid=(rows // tr,), in_specs=[spec] * len(parts), out_specs=spec,
        out_shape=jax.ShapeDtypeStruct((rows, cols), out_dtype),
        compiler_params=_params(dimension_semantics=("arbitrary",)),
    )(*parts)


def _place():
    return lax.axis_index("x"), lax.axis_index("y"), lax.axis_index("c")


def _all_gather_call(x, name):
    rows, cols = x.shape

    def body(x_ref, out_ref, send_sems, recv_sems, local_sem):
        ax, ay, ac = _place()
        me, sibling = (ax, ay, ac), (ax, ay, 1 - ac)
        chips = [(1 - ax, ay), (ax, 1 - ay), (1 - ax, 1 - ay)]

        def slot(px, py, pc):
            return out_ref.at[4 * px + 2 * py + pc]

        def copy(k, block, to, src=None):
            return pltpu.make_async_remote_copy(
                src_ref=slot(*block) if src is None else src, dst_ref=slot(*block),
                send_sem=send_sems.at[k], recv_sem=recv_sems.at[k], device_id=to, device_id_type=MESH)

        mine = pltpu.make_async_copy(x_ref, slot(*me), local_sem)
        mine.start()
        first = [copy(0, me, sibling, src=x_ref)]
        first += [copy(1 + j, me, (*chip, ac), src=x_ref) for j, chip in enumerate(chips)]
        for cp in first:
            cp.start()
        passed = [copy(4 + j, (*chip, ac), sibling) for j, chip in enumerate(chips)]
        for j, chip in enumerate(chips):
            copy(1 + j, (*chip, ac), me).wait_recv()
            passed[j].start()
        copy(0, sibling, me).wait_recv()
        for j, chip in enumerate(chips):
            copy(4 + j, (*chip, 1 - ac), me).wait_recv()
        for cp in first + passed:
            cp.wait_send()
        mine.wait()

    return pl.pallas_call(
        body, name=name,
        in_specs=[pl.BlockSpec(memory_space=pl.ANY)], out_specs=pl.BlockSpec(memory_space=pl.ANY),
        out_shape=jax.ShapeDtypeStruct((NDEV, rows, cols), x.dtype),
        scratch_shapes=[pltpu.SemaphoreType.DMA((7,)), pltpu.SemaphoreType.DMA((7,)), pltpu.SemaphoreType.DMA],
    )(x)


def _swap_sibling_call(x, name):
    def body(x_ref, out_ref, send_sem, recv_sem):
        ax, ay, ac = _place()
        cp = pltpu.make_async_remote_copy(src_ref=x_ref, dst_ref=out_ref, send_sem=send_sem, recv_sem=recv_sem,
                                          device_id=(ax, ay, 1 - ac), device_id_type=MESH)
        cp.start()
        cp.wait()

    return pl.pallas_call(
        body, name=name,
        in_specs=[pl.BlockSpec(memory_space=pl.ANY)], out_specs=pl.BlockSpec(memory_space=pl.ANY),
        out_shape=jax.ShapeDtypeStruct(x.shape, x.dtype),
        scratch_shapes=[pltpu.SemaphoreType.DMA, pltpu.SemaphoreType.DMA],
    )(x)


def _swap_chips_call(x, name):
    def body(x_ref, out_ref, send_sems, recv_sems):
        ax, ay, ac = _place()
        chips = [(1 - ax, ay), (ax, 1 - ay), (1 - ax, 1 - ay)]
        copies = [pltpu.make_async_remote_copy(src_ref=x_ref.at[2 * cx + cy], dst_ref=out_ref.at[j],
                                               send_sem=send_sems.at[j], recv_sem=recv_sems.at[j],
                                               device_id=(cx, cy, ac), device_id_type=MESH)
                  for j, (cx, cy) in enumerate(chips)]
        for cp in copies:
            cp.start()
        for cp in copies:
            cp.wait()

    return pl.pallas_call(
        body, name=name,
        in_specs=[pl.BlockSpec(memory_space=pl.ANY)], out_specs=pl.BlockSpec(memory_space=pl.ANY),
        out_shape=jax.ShapeDtypeStruct((3,) + x.shape[1:], x.dtype),
        scratch_shapes=[pltpu.SemaphoreType.DMA((3,)), pltpu.SemaphoreType.DMA((3,))],
    )(x)


def _reduce_scatter(g, name):
    ax, ay, ac = _place()
    rows, cols = g.shape[1:]
    by_core = g.reshape(4, 2, rows, cols)
    keep = lax.dynamic_index_in_dim(by_core, ac, axis=1, keepdims=False)
    give = lax.dynamic_index_in_dim(by_core, 1 - ac, axis=1, keepdims=False)
    got = _swap_sibling_call(give, name + "_d2d")
    chip_sum = _sum_call([keep.reshape(4 * rows, cols), got.reshape(4 * rows, cols)], BF16, name + "_pair").reshape(4, rows, cols)
    from_chips = _swap_chips_call(chip_sum, name + "_ici")
    my_chip = 2 * ax + ay
    own = [lax.dynamic_index_in_dim(t, my_chip, axis=0, keepdims=False) for t in (keep, got)]
    return _sum_call(own + [from_chips[j] for j in range(3)], F32, name + "_total")


BIG = ("w_in", "w_out", "w_ff1", "w_ff2")
SMALL = ("norm1_g", "conv_w", "a_log", "dt_bias", "dn_out_g", "sb_q_g", "sb_k_g", "sg_v_g", "sg_w", "sg_b", "norm2_g")
WEIGHTS = ("norm1_g", "w_in", "conv_w", "a_log", "dt_bias", "dn_out_g", "sb_q_g", "sb_k_g", "sg_v_g", "sg_w", "sg_b",
           "w_out", "norm2_g", "w_ff1", "w_ff2")
BIG_SHARD = {"w_in": (D, IN_DIM // NDEV), "w_out": (D // NDEV, D), "w_ff1": (D, DFF // NDEV), "w_ff2": (DFF // NDEV, D)}
SMALL_SHAPE = {"norm1_g": (D,), "conv_w": (4, 3 * DN_W), "a_log": (NH,), "dt_bias": (NH,), "dn_out_g": (128,), "sb_q_g": (64,),
               "sb_k_g": (64,), "sg_v_g": (SG_W,), "sg_w": (NH, 128, 128), "sg_b": (NH, 128), "norm2_g": (D,)}


def _size(shape):
    n = 1
    for s in shape:
        n *= s
    return n


def _pack_in(w):
    return jnp.concatenate([w[..., :2056], jnp.zeros(w.shape[:-1] + (NPACK - IN_DIM,), w.dtype), w[..., 2056:]], axis=-1)


def _unpack_in(w):
    return jnp.concatenate([w[..., :2056], w[..., 2056 + NPACK - IN_DIM:]], axis=-1)


def _to_rows(flat, multiple):
    pad = (-flat.shape[0]) % (LANES * multiple)
    return jnp.pad(flat, (0, pad)).reshape(-1, LANES)


def _full_from_shards(blocks, name):
    r, c = BIG_SHARD[name]
    t = blocks.reshape(NDEV, 2, r, c)
    if name in ("w_in", "w_ff1"):
        return t.transpose(1, 2, 0, 3).reshape(2, r, NDEV * c)
    return t.transpose(1, 0, 2, 3).reshape(2, NDEV * r, c)


def _shards_from_full(full, name):
    r, c = BIG_SHARD[name]
    if name in ("w_in", "w_ff1"):
        t = full.reshape(2, r, NDEV, c).transpose(2, 0, 1, 3)
    else:
        t = full.reshape(2, NDEV, r, c).transpose(1, 0, 2, 3)
    return t.reshape(NDEV, 2, r * c)


def kernel(x, norm1_g, w_in, conv_w, a_log, dt_bias, dn_out_g, sb_q_g, sb_k_g, sg_v_g, sg_w, sg_b, w_out, norm2_g, w_ff1, w_ff2, loss_target, m_norm1_g, m_w_in, m_conv_w, m_a_log, m_dt_bias, m_dn_out_g, m_sb_q_g, m_sb_k_g, m_sg_v_g, m_sg_w, m_sg_b, m_w_out, m_norm2_g, m_w_ff1, m_w_ff2, v_norm1_g, v_w_in, v_conv_w, v_a_log, v_dt_bias, v_dn_out_g, v_sb_q_g, v_sb_k_g, v_sg_v_g, v_sg_w, v_sg_b, v_w_out, v_norm2_g, v_w_ff1, v_w_ff2):
    given = dict(norm1_g=norm1_g, w_in=w_in, conv_w=conv_w, a_log=a_log, dt_bias=dt_bias, dn_out_g=dn_out_g, sb_q_g=sb_q_g,
                 sb_k_g=sb_k_g, sg_v_g=sg_v_g, sg_w=sg_w, sg_b=sg_b, w_out=w_out, norm2_g=norm2_g, w_ff1=w_ff1, w_ff2=w_ff2)
    mom = dict(norm1_g=m_norm1_g, w_in=m_w_in, conv_w=m_conv_w, a_log=m_a_log, dt_bias=m_dt_bias, dn_out_g=m_dn_out_g,
               sb_q_g=m_sb_q_g, sb_k_g=m_sb_k_g, sg_v_g=m_sg_v_g, sg_w=m_sg_w, sg_b=m_sg_b, w_out=m_w_out, norm2_g=m_norm2_g,
               w_ff1=m_w_ff1, w_ff2=m_w_ff2)
    var = dict(norm1_g=v_norm1_g, w_in=v_w_in, conv_w=v_conv_w, a_log=v_a_log, dt_bias=v_dt_bias, dn_out_g=v_dn_out_g,
               sb_q_g=v_sb_q_g, sb_k_g=v_sb_k_g, sg_v_g=v_sg_v_g, sg_w=v_sg_w, sg_b=v_sg_b, w_out=v_w_out, norm2_g=v_norm2_g,
               w_ff1=v_w_ff1, w_ff2=v_w_ff2)
    B, T, _ = x.shape
    M = B * T
    ax, ay, ac = _place()
    me = 4 * ax + 2 * ay + ac

    sizes = [_size(BIG_SHARD[n]) for n in BIG]
    shard = jnp.concatenate([given[n].reshape(2, -1) for n in BIG], axis=1).astype(BF16)
    gathered = _all_gather_call(shard.reshape(-1, LANES), "gather_weights").reshape(NDEV, 2, -1)
    full, off = {}, 0
    for n, sz in zip(BIG, sizes):
        full[n] = _full_from_shards(gathered[:, :, off:off + sz], n)
        off += sz
    w_pack = _pack_in(full["w_in"])
    conv_full = _all_gather_call(_to_rows(conv_w.reshape(-1), 8), "gather_conv")
    conv_full = conv_full.reshape(NDEV, -1)[:, :conv_w.size].reshape(NDEV, 2, 4, -1).transpose(1, 2, 0, 3).reshape(2, 4, 3 * DN_W)

    pad_vec = lambda v: jnp.zeros((1, LANES), F32).at[0, :v.shape[0]].set(v)
    layer = []
    for l in range(2):
        layer.append(dict(
            g1=norm1_g[l].reshape(1, D), g2=norm2_g[l].reshape(1, D), conv=conv_full[l],
            a_log=pad_vec(a_log[l]), dt_bias=pad_vec(dt_bias[l]), dn_g=dn_out_g[l].reshape(1, LANES),
            sb_qg=jnp.tile(sb_q_g[l], 2).reshape(1, LANES), sb_kg=jnp.tile(sb_k_g[l], 2).reshape(1, LANES),
            sg_g=sg_v_g[l].reshape(1, SG_W), sg_w=sg_w[l], sg_bias=jnp.repeat(sg_b[l].T, 64, axis=1),
            w_pack=w_pack[l], w_out=full["w_out"][l], w1=full["w_ff1"][l], w2=full["w_ff2"][l]))

    cur = x.reshape(M, D)
    saved = []
    for p in layer:
        proj, h = _in_proj_call(cur, p["g1"], p["w_pack"])
        o_dn = _dn_fwd_call(proj, p["conv"], p["a_log"], p["dt_bias"], p["dn_g"], B, T)
        o_sb = _sb_fwd_call(proj, p["sb_qg"], p["sb_kg"], B, T)
        o_sg = _sg_fwd_call(proj, p["sg_g"], p["sg_w"], p["sg_bias"], B, T)
        mix = jnp.concatenate([o_dn, o_sb, o_sg], axis=1)
        x1 = _out_proj_call(mix, p["w_out"], cur)
        x2 = _ffn_fwd_call(x1, p["g2"], p["w1"], p["w2"])
        saved.append(dict(x0=cur, proj=proj, h=h, mix=mix, x1=x1))
        cur = x2
    loss_part, dy = _loss_call(cur, loss_target.reshape(M, D))
    loss = lax.psum(loss_part[0, 0], ("x", "y", "c"))

    big_grads = {n: [None, None] for n in BIG}
    small_grads = {n: [None, None] for n in SMALL}
    for l in (1, 0):
        p, s = layer[l], saved[l]
        dx1, da, r, h2, dg2 = _ffn_bwd_call(s["x1"], dy, p["g2"], p["w1"], p["w2"])
        big_grads["w_ff1"][l] = _mm_tn_call(h2, da, "grad_w_ff1")
        big_grads["w_ff2"][l] = _mm_tn_call(r, dy, "grad_w_ff2")
        dmix = _mm_nt_call(dx1, p["w_out"], "dmix")
        big_grads["w_out"][l] = _mm_tn_call(s["mix"], dx1, "grad_w_out")
        dq, dk, dv, dz, dab, dcw, dalog, ddtb, ddn_g = _dn_bwd_call(s["proj"], dmix, p["conv"], p["a_log"], p["dt_bias"], p["dn_g"], B, T)
        dsq, dsk, dsv, dqg, dkg = _sb_bwd_call(s["proj"], dmix, p["sb_qg"], p["sb_kg"], B, T)
        du, dvv, dsg_g, dsg_w, dsg_b = _sg_bwd_call(s["proj"], dmix, p["sg_g"], p["sg_w"], p["sg_bias"], B, T)
        dproj = jnp.concatenate([dq, dk, dv, dz, dab, dsq, dsk, dsv, du, dvv], axis=1)
        dy, dg1 = _in_proj_bwd_call(dproj, p["w_pack"], s["x0"], p["g1"], dx1)
        big_grads["w_in"][l] = _unpack_in(_mm_tn_call(s["h"], dproj, "grad_w_in"))
        for n, val in (("norm1_g", dg1[0]), ("conv_w", dcw.transpose(1, 0, 2).reshape(4, 3 * DN_W)), ("a_log", dalog[0, :NH]),
                       ("dt_bias", ddtb[0, :NH]), ("dn_out_g", ddn_g[0]), ("sb_q_g", dqg[0, :64]), ("sb_k_g", dkg[0, :64]),
                       ("sg_v_g", dsg_g[0]), ("sg_w", dsg_w), ("sg_b", dsg_b[:, :NH].T), ("norm2_g", dg2[0])):
            small_grads[n][l] = val
    grad_x = dy.reshape(B, T, D)

    send = jnp.concatenate([_shards_from_full(jnp.stack(big_grads[n]), n) for n in BIG], axis=2).astype(BF16)
    mine = _reduce_scatter(send.reshape(NDEV, -1, LANES), "reduce_grads").reshape(2, -1)
    grads, off = {}, 0
    for n, sz in zip(BIG, sizes):
        grads[n] = mine[:, off:off + sz].reshape((2,) + BIG_SHARD[n])
        off += sz
    small_flat = jnp.concatenate([jnp.stack(small_grads[n]).reshape(-1) for n in SMALL])
    small_rows = _to_rows(small_flat, 8)
    everyone = _all_gather_call(small_rows, "gather_small_grads")
    small_sum = _sum_call([everyone[k] for k in range(NDEV)], F32, "sum_small_grads").reshape(-1)
    off = 0
    for n in SMALL:
        sz = 2 * _size(SMALL_SHAPE[n])
        grads[n] = small_sum[off:off + sz].reshape((2,) + SMALL_SHAPE[n])
        off += sz
    cshard = conv_w.shape[-1]
    grads["conv_w"] = lax.dynamic_slice_in_dim(grads["conv_w"], me * cshard, cshard, axis=2)

    deltas, new_m, new_v = {}, {}, {}
    for n in WEIGHTS:
        deltas[n], new_m[n], new_v[n] = _adamw_call(given[n], grads[n], mom[n], var[n], "adamw_" + n)
    return (loss, grad_x, *[grads[n] for n in WEIGHTS], *[deltas[n] for n in WEIGHTS], *[new_m[n] for n in WEIGHTS],
            *[new_v[n] for n in WEIGHTS])
```

```python
import functools

import jax
import jax.numpy as jnp
from jax import lax
from jax.experimental import pallas as pl
from jax.experimental.pallas import tpu as pltpu

F32, BF16 = jnp.float32, jnp.bfloat16
HI = lax.Precision.HIGHEST
EPS = 1e-6
LANES = 128
D = 1024
DFF = 4096
NH = 4
DN_W, SB_W, SG_W = 512, 256, 256
IN_DIM = 3336
NPACK = 3456
CB_Q, CB_K, CB_V, CB_Z, CB_AB, CB_SBQ, CB_SBK, CB_SBV, CB_SGU, CB_SGV = 0, 4, 8, 12, 16, 17, 19, 21, 23, 25
SB_SCALE = 64 ** -0.5
DN_SCALE = 128 ** -0.5
NDEV = 8
VMEM_LIMIT = 56 * 1024 * 1024
ADAM_LR, ADAM_B1, ADAM_B2, ADAM_EPS, ADAM_WD, ADAM_STEP = 0.001, 0.9, 0.999, 1e-08, 0.01, 10
MESH = pl.DeviceIdType.MESH


def _iota(shape, dim):
    return lax.broadcasted_iota(jnp.int32, shape, dim)


def _params(**kw):
    return pltpu.CompilerParams(vmem_limit_bytes=VMEM_LIMIT, **kw)


def _mm(a, b, dims):
    return lax.dot_general(a.astype(BF16), b.astype(BF16), (dims, ((), ())), preferred_element_type=F32)


@jax.custom_vjp
def _dot(a, b):
    return _mm(a, b, ((1,), (0,)))


def _dot_fwd(a, b):
    return _dot(a, b), (a, b)


def _dot_bwd(res, g):
    a, b = res
    return _mm(g, b, ((1,), (1,))).astype(a.dtype), _mm(a, g, ((0,), (0,))).astype(b.dtype)


_dot.defvjp(_dot_fwd, _dot_bwd)


@jax.custom_vjp
def _dot_nt(a, b):
    return _mm(a, b, ((1,), (1,)))


def _dot_nt_fwd(a, b):
    return _dot_nt(a, b), (a, b)


def _dot_nt_bwd(res, g):
    a, b = res
    return _mm(g, b, ((1,), (0,))).astype(a.dtype), _mm(g, a, ((0,), (0,))).astype(b.dtype)


_dot_nt.defvjp(_dot_nt_fwd, _dot_nt_bwd)


@jax.custom_vjp
def _dot_tn(a, b):
    return _mm(a, b, ((0,), (0,)))


def _dot_tn_fwd(a, b):
    return _dot_tn(a, b), (a, b)


def _dot_tn_bwd(res, g):
    a, b = res
    return _mm(b, g, ((1,), (1,))).astype(a.dtype), _mm(a, g, ((1,), (0,))).astype(b.dtype)


_dot_tn.defvjp(_dot_tn_fwd, _dot_tn_bwd)


def _split(x):
    hi = x.astype(BF16)
    return hi, (x - hi.astype(F32)).astype(BF16)


def _mm3(a, b, dims):
    (ah, al), (bh, bl) = _split(a), _split(b)
    mm = lambda x, y: lax.dot_general(x, y, (dims, ((), ())), preferred_element_type=F32)
    return mm(ah, bh) + (mm(ah, bl) + mm(al, bh))


@jax.custom_vjp
def _dot3(a, b):
    return _mm3(a, b, ((1,), (0,)))


def _dot3_fwd(a, b):
    return _dot3(a, b), (a, b)


def _dot3_bwd(res, g):
    a, b = res
    return _mm3(g, b, ((1,), (1,))), _mm3(a, g, ((0,), (0,)))


_dot3.defvjp(_dot3_fwd, _dot3_bwd)


def _mm_ones(ones, x, ones_left):
    hi, lo = _split(x)
    mm = (lambda t: jnp.dot(ones, t, preferred_element_type=F32)) if ones_left else \
         (lambda t: jnp.dot(t, ones, preferred_element_type=F32))
    return mm(hi) + mm(lo)


def _pair_ones(kind, transposed):
    row, col = _iota((128, 128), 0), _iota((128, 128), 1)
    m = (row // 64) == (col // 64)
    if kind == "running":
        m = jnp.logical_and(m, (col >= row) if transposed else (col <= row))
    return jnp.where(m, 1.0, 0.0).astype(BF16)


@functools.partial(jax.custom_vjp, nondiff_argnums=(0,))
def _chunk_sum(kind, x):
    return _mm_ones(_pair_ones(kind, False), x, True)


def _chunk_sum_fwd(kind, x):
    return _chunk_sum(kind, x), None


def _chunk_sum_bwd(kind, _, g):
    return (_mm_ones(_pair_ones(kind, True), g, True),)


_chunk_sum.defvjp(_chunk_sum_fwd, _chunk_sum_bwd)


def _tri_ones(n, transposed):
    row, col = _iota((n, n), 0), _iota((n, n), 1)
    return jnp.where((row < col) if transposed else (row > col), 1.0, 0.0).astype(BF16)


@jax.custom_vjp
def _suffix_sum(x):
    return _mm_ones(_tri_ones(x.shape[1], False), x, False)


def _suffix_sum_fwd(x):
    return _suffix_sum(x), None


def _suffix_sum_bwd(_, g):
    return (_mm_ones(_tri_ones(g.shape[1], True), g, False),)


_suffix_sum.defvjp(_suffix_sum_fwd, _suffix_sum_bwd)


def _sigmoid(x):
    return jax.nn.sigmoid(x)


def _silu(x):
    return x * _sigmoid(x)


def _softplus(x):
    return jnp.maximum(x, 0.0) + jnp.log1p(jnp.exp(-jnp.abs(x)))


def _gelu(x):
    return 0.5 * x * (1.0 + jnp.tanh(0.7978845608028654 * (x + 0.044715 * (x * x * x))))


def _rms(x, gain):
    return x * lax.rsqrt(jnp.mean(x * x, axis=-1, keepdims=True) + EPS) * gain


def _shift_down_impl(x, k):
    return jnp.where(_iota(x.shape, 0) >= k, pltpu.roll(x, k, 0), 0.0)


def _shift_up_impl(x, k):
    n = x.shape[0]
    return jnp.where(_iota(x.shape, 0) < n - k, pltpu.roll(x, n - k, 0), 0.0)


@functools.partial(jax.custom_vjp, nondiff_argnums=(1,))
def _shift_down(x, k):
    return _shift_down_impl(x, k)


def _shift_down_fwd(x, k):
    return _shift_down_impl(x, k), None


def _shift_down_bwd(k, _, g):
    return (_shift_up_impl(g, k),)


_shift_down.defvjp(_shift_down_fwd, _shift_down_bwd)


def _dn_conv(x, w0, w1, w2, w3, l2_scale):
    y = _silu(w3 * x + w2 * _shift_down(x, 1) + w1 * _shift_down(x, 2) + w0 * _shift_down(x, 3))
    if l2_scale is None:
        return y
    return y * lax.rsqrt(jnp.sum(y * y, axis=-1, keepdims=True) + EPS) * l2_scale


def _dn_gate(a, b, a_log, dt_bias):
    return -jnp.exp(a_log) * _softplus(a + dt_bias), _sigmoid(b)


def _dn_pair(q, k, v, g, beta, s0):
    n = 128
    row, col = _iota((n, n), 0), _iota((n, n), 1)
    same = (row // 64) == (col // 64)
    tri_incl = jnp.logical_and(same, col <= row)
    tri_strict = jnp.logical_and(same, col < row)
    first = row < 64
    gb = jnp.broadcast_to(g, (n, n))
    gc = _chunk_sum("running", gb)
    gl = _chunk_sum("total", gb)
    diff = gc - gc.T
    decay = jnp.where(tri_incl, jnp.exp(jnp.where(tri_incl, diff, 0.0)), 0.0)
    egc = jnp.exp(gc)
    kk = _dot_nt(k, k)
    lower = jnp.where(tri_strict, beta * kk * decay, 0.0)
    nk = -lower
    inv = jnp.where(row == col, 1.0, 0.0) + nk
    for _ in range(5):
        nk = _dot3(nk, nk)
        inv = inv + _dot3(inv, nk)
    u_val = _dot3(inv, v * beta)
    w_dec = _dot3(inv, k * (beta * egc))
    qk = jnp.where(tri_incl, _dot_nt(q, k) * decay, 0.0)
    q_dec = q * egc
    k_dec = k * jnp.exp(gl - gc)
    cd1 = jnp.exp(jnp.sum(jnp.where(first, gb, 0.0), axis=0, keepdims=True))
    cd2 = jnp.exp(jnp.sum(jnp.where(first, 0.0, gb), axis=0, keepdims=True))
    u1 = u_val - _dot(w_dec, s0)
    s1 = s0 * cd1 + _dot_tn(jnp.where(first, k_dec, 0.0), u1)
    u2 = u_val - _dot(w_dec, s1)
    u_new = jnp.where(first, u1, u2)
    s2 = s1 * cd2 + _dot_tn(jnp.where(first, 0.0, k_dec), u_new)
    o = jnp.where(first, _dot(q_dec, s0), _dot(q_dec, s1)) + _dot(qk, u_new)
    return o, s2


def _dn_post(o, z, gain):
    return _rms(o, gain) * _silu(z)


def _lane_pick(x, idx):
    return jnp.sum(jnp.where(_iota(x.shape, 1) == idx, x, 0.0), axis=-1, keepdims=True)


def _taps(cw_ref, cs):
    return tuple(cw_ref[i:i + 1, cs] for i in range(4))


def _dn_gate_in(ab_ref, alog_ref, dtb_ref, h):
    ab = ab_ref[...]
    return _lane_pick(ab, h), _lane_pick(ab, h + NH), _lane_pick(alog_ref[...], h), _lane_pick(dtb_ref[...], h)


_DN_L2 = (DN_SCALE, 1.0, None)
DN_HPS = 2
_DN_COLS = tuple(slice(i * LANES, (i + 1) * LANES) for i in range(DN_HPS))


def _dn_pack_gate(vals):
    lane = _iota((1, LANES), 1)
    out = 0.0
    for i, (g, beta) in enumerate(vals):
        out = out + jnp.where(lane == 2 * i, g, 0.0) + jnp.where(lane == 2 * i + 1, beta, 0.0)
    return out


def _dn_specs(T):
    wide = DN_HPS * LANES
    one = pl.Buffered(1)
    col = lambda c0: pl.BlockSpec((T, wide), lambda b, h, c0=c0: (b, c0 + h), pipeline_mode=one)
    cw = lambda c0: pl.BlockSpec((4, wide), lambda b, h, c0=c0: (0, c0 + h))
    vec = pl.BlockSpec((1, LANES), lambda b, h: (0, 0))
    nb = NH // DN_HPS
    ins = [col(0), col(nb), col(2 * nb), col(3 * nb), pl.BlockSpec((T, LANES), lambda b, h: (b, CB_AB), pipeline_mode=one),
           cw(0), cw(nb), cw(2 * nb), vec, vec, vec]
    return ins, pl.BlockSpec((T, wide), lambda b, h: (b, h), pipeline_mode=one), vec


def _dn_fwd_call(proj, conv_w, a_log, dt_bias, gain, B, T):
    npair = T // 128

    def body(q_ref, k_ref, v_ref, z_ref, ab_ref, cwq_ref, cwk_ref, cwv_ref, alog_ref, dtb_ref, gain_ref, out_ref,
             q_s, k_s, v_s, o_s, gate_s):
        hp = pl.program_id(1)
        gates = []
        for i, cs in enumerate(_DN_COLS):
            for x_ref, cw_ref, x_s, l2 in zip((q_ref, k_ref, v_ref), (cwq_ref, cwk_ref, cwv_ref), (q_s, k_s, v_s), _DN_L2):
                x_s[:, cs] = _dn_conv(x_ref[:, cs], *_taps(cw_ref, cs), l2)
            gates.append(_dn_gate(*_dn_gate_in(ab_ref, alog_ref, dtb_ref, DN_HPS * hp + i)))
        gate_s[...] = _dn_pack_gate(gates)

        def step(n, states):
            r = pl.ds(pl.multiple_of(n * 128, 128), 128)
            gate = gate_s[r, :]
            new = []
            for i, cs in enumerate(_DN_COLS):
                o, s2 = _dn_pair(q_s[r, cs], k_s[r, cs], v_s[r, cs], _lane_pick(gate, 2 * i), _lane_pick(gate, 2 * i + 1), states[i])
                o_s[r, cs] = o
                new.append(s2)
            return tuple(new)

        lax.fori_loop(0, npair, step, (jnp.zeros((128, 128), F32),) * DN_HPS)
        for cs in _DN_COLS:
            out_ref[:, cs] = _dn_post(o_s[:, cs], z_ref[:, cs], gain_ref[...])

    ins, blk, _ = _dn_specs(T)
    wide = [pltpu.VMEM((T, DN_HPS * LANES), F32)]
    return pl.pallas_call(
        body, name="dn_fwd", grid=(B, NH // DN_HPS), in_specs=ins, out_specs=blk,
        out_shape=jax.ShapeDtypeStruct((B * T, DN_W), F32),
        scratch_shapes=wide * 4 + [pltpu.VMEM((T, LANES), F32)],
        compiler_params=_params(dimension_semantics=("arbitrary", "arbitrary")),
    )(proj, proj, proj, proj, proj, conv_w, conv_w, conv_w, a_log, dt_bias, gain)


def _dn_bwd_call(proj, dmix, conv_w, a_log, dt_bias, gain, B, T):
    npair = T // 128

    def body(q_ref, k_ref, v_ref, z_ref, ab_ref, cwq_ref, cwk_ref, cwv_ref, alog_ref, dtb_ref, gain_ref, do_ref,
             dq_ref, dk_ref, dv_ref, dz_ref, dab_ref, dcw_ref, dalog_ref, ddtb_ref, dgain_ref,
             q_s, k_s, v_s, o_s, gate_s, dgate_s, st_s):
        b_i, hp = pl.program_id(0), pl.program_id(1)
        pre = tuple(zip((q_ref, k_ref, v_ref), (cwq_ref, cwk_ref, cwv_ref), (q_s, k_s, v_s), _DN_L2))
        gates = []
        for i, cs in enumerate(_DN_COLS):
            for x_ref, cw_ref, x_s, l2 in pre:
                x_s[:, cs] = _dn_conv(x_ref[:, cs], *_taps(cw_ref, cs), l2)
            gates.append(_dn_gate(*_dn_gate_in(ab_ref, alog_ref, dtb_ref, DN_HPS * hp + i)))
        gate_s[...] = _dn_pack_gate(gates)

        def pair_in(r, i, cs, gate):
            return q_s[r, cs], k_s[r, cs], v_s[r, cs], _lane_pick(gate, 2 * i), _lane_pick(gate, 2 * i + 1)

        def fstep(n, states):
            r = pl.ds(pl.multiple_of(n * 128, 128), 128)
            gate = gate_s[r, :]
            new = []
            for i, cs in enumerate(_DN_COLS):
                st_s[n, i] = states[i]
                o, s2 = _dn_pair(*pair_in(r, i, cs, gate), states[i])
                o_s[r, cs] = o
                new.append(s2)
            return tuple(new)

        zero_states = (jnp.zeros((128, 128), F32),) * DN_HPS
        lax.fori_loop(0, npair, fstep, zero_states)

        @pl.when(jnp.logical_and(b_i == 0, hp == 0))
        def _():
            dcw_ref[...] = jnp.zeros_like(dcw_ref)
            dalog_ref[...] = jnp.zeros_like(dalog_ref)
            ddtb_ref[...] = jnp.zeros_like(ddtb_ref)
            dgain_ref[...] = jnp.zeros_like(dgain_ref)

        for cs in _DN_COLS:
            _, post_vjp = jax.vjp(_dn_post, o_s[:, cs], z_ref[:, cs], gain_ref[...])
            do, dz, dgain = post_vjp(do_ref[:, cs])
            dz_ref[:, cs] = dz
            o_s[:, cs] = do
            dgain_ref[...] += dgain

        def bstep(nn, dstates):
            n = npair - 1 - nn
            r = pl.ds(pl.multiple_of(n * 128, 128), 128)
            gate = gate_s[r, :]
            new, dgates = [], []
            for i, cs in enumerate(_DN_COLS):
                _, pair_vjp = jax.vjp(_dn_pair, *pair_in(r, i, cs, gate), st_s[n, i])
                dq, dk, dv, dg, db, ds0 = pair_vjp((o_s[r, cs], dstates[i]))
                dq_ref[r, cs], dk_ref[r, cs], dv_ref[r, cs] = dq, dk, dv
                dgates.append((dg, db))
                new.append(ds0)
            dgate_s[r, :] = _dn_pack_gate(dgates)
            return tuple(new)

        lax.fori_loop(0, npair, bstep, zero_states)

        lane = _iota((1, LANES), 1)
        dab = 0.0
        for i, cs in enumerate(_DN_COLS):
            h = DN_HPS * hp + i
            for n, ((x_ref, cw_ref, _x_s, l2), dx_ref) in enumerate(zip(pre, (dq_ref, dk_ref, dv_ref))):
                _, conv_vjp = jax.vjp(functools.partial(_dn_conv, l2_scale=l2), x_ref[:, cs], *_taps(cw_ref, cs))
                dx, *dw = conv_vjp(dx_ref[:, cs])
                dx_ref[:, cs] = dx
                for t in range(4):
                    dcw_ref[h + 4 * n, t:t + 1, :] += dw[t]
            _, gate_vjp = jax.vjp(_dn_gate, *_dn_gate_in(ab_ref, alog_ref, dtb_ref, h))
            dgate = dgate_s[...]
            da, db, dalog, ddtb = gate_vjp((_lane_pick(dgate, 2 * i), _lane_pick(dgate, 2 * i + 1)))
            dab = dab + jnp.where(lane == h, da, 0.0) + jnp.where(lane == h + NH, db, 0.0)
            dalog_ref[...] += jnp.where(lane == h, dalog, 0.0)
            ddtb_ref[...] += jnp.where(lane == h, ddtb, 0.0)

        @pl.when(hp == 0)
        def _():
            dab_ref[...] = jnp.zeros_like(dab_ref)

        dab_ref[...] += dab

    M = B * T
    ins, blk, vec = _dn_specs(T)
    wide = [pltpu.VMEM((T, DN_HPS * LANES), F32)]
    narrow = [pltpu.VMEM((T, LANES), F32)]
    dqkvz = jax.ShapeDtypeStruct((M, DN_W), F32)
    vec_shape = jax.ShapeDtypeStruct((1, LANES), F32)
    return pl.pallas_call(
        body, name="dn_bwd", grid=(B, NH // DN_HPS),
        in_specs=ins + [blk],
        out_specs=[blk] * 4 + [pl.BlockSpec((T, LANES), lambda b, h: (b, 0)), pl.BlockSpec((12, 4, LANES), lambda b, h: (0, 0, 0)),
                               vec, vec, vec],
        out_shape=[dqkvz] * 4 + [jax.ShapeDtypeStruct((M, LANES), F32), jax.ShapeDtypeStruct((12, 4, LANES), F32),
                                 vec_shape, vec_shape, vec_shape],
        scratch_shapes=wide * 4 + narrow * 2 + [pltpu.VMEM((npair, DN_HPS, 128, 128), F32)],
        compiler_params=_params(dimension_semantics=("arbitrary", "arbitrary")),
    )(proj, proj, proj, proj, proj, conv_w, conv_w, conv_w, a_log, dt_bias, gain, dmix)


def _group_rms(x, gain):
    first = _iota(x.shape, 1) < 64
    sq = x * x
    ss_a = jnp.sum(jnp.where(first, sq, 0.0), axis=-1, keepdims=True)
    ss_b = jnp.sum(jnp.where(first, 0.0, sq), axis=-1, keepdims=True)
    ms = jnp.where(first, ss_a, ss_b) * (1.0 / 64)
    return x * lax.rsqrt(ms + EPS) * gain


SBQ = 256


def _sb_block(q, k, v, carry, diag):
    n = SBQ
    first = _iota((1, LANES), 1) < 64
    q2 = jnp.concatenate([jnp.where(first, q, 0.0), jnp.where(first, 0.0, q)], axis=0)
    z, mask = _sb_logits(q2, k, diag)
    soft = jnp.log1p(jnp.exp(-jnp.abs(z)))
    ls_pos = jnp.minimum(z, 0.0) - soft
    l1m = jnp.where(mask, jnp.minimum(-z, 0.0) - soft, 0.0)
    w = jnp.where(mask, jnp.exp(ls_pos + (_suffix_sum(l1m) + carry)), 0.0)
    return _dot(w, v), carry + jnp.sum(l1m, axis=-1, keepdims=True)


def _sb_logits(q2, k, diag):
    n = SBQ
    row, col = jnp.bitwise_and(_iota((2 * n, n), 0), n - 1), _iota((2 * n, n), 1)
    return _dot_nt(q2, k) * SB_SCALE, jnp.logical_or(col < row, jnp.logical_not(diag))


def _sb_rowsum(q, k, diag):
    first = _iota((1, LANES), 1) < 64
    q2 = jnp.concatenate([jnp.where(first, q, 0.0), jnp.where(first, 0.0, q)], axis=0)
    z, mask = _sb_logits(q2, k, diag)
    l1m = jnp.where(mask, jnp.minimum(-z, 0.0) - jnp.log1p(jnp.exp(-jnp.abs(z))), 0.0)
    return jnp.sum(l1m, axis=-1, keepdims=True)


def _sb_fold(acc):
    return jnp.where(_iota((1, LANES), 1) < 64, acc[:SBQ], acc[SBQ:])


def _sb_unfold(do):
    first = _iota((1, LANES), 1) < 64
    return jnp.concatenate([jnp.where(first, do, 0.0), jnp.where(first, 0.0, do)], axis=0)


def _sb_fwd_call(proj, q_gain, k_gain, B, T):
    nblk = T // SBQ

    def body(q_ref, k_ref, v_ref, qg_ref, kg_ref, out_ref, q_s, k_s):
        q_s[...] = _group_rms(q_ref[...], qg_ref[...])
        k_s[...] = _group_rms(k_ref[...], kg_ref[...])

        def qblock(i, _):
            ri = pl.ds(pl.multiple_of(i * SBQ, SBQ), SBQ)
            q = q_s[ri, :]

            def kblock(jj, c):
                j = i - jj
                rj = pl.ds(pl.multiple_of(j * SBQ, SBQ), SBQ)
                pv, carry = _sb_block(q, k_s[rj, :], v_ref[rj, :], c[1], j == i)
                return c[0] + pv, carry

            acc, _c = lax.fori_loop(0, i + 1, kblock, (jnp.zeros((2 * SBQ, LANES), F32), jnp.zeros((2 * SBQ, 1), F32)))
            out_ref[ri, :] = _sb_fold(acc)
            return 0

        lax.fori_loop(0, nblk, qblock, 0)

    col = lambda c0: pl.BlockSpec((T, LANES), lambda b, p, c0=c0: (b, c0 + p))
    vec = pl.BlockSpec((1, LANES), lambda b, p: (0, 0))
    return pl.pallas_call(
        body, name="sb_fwd", grid=(B, 2),
        in_specs=[col(CB_SBQ), col(CB_SBK), col(CB_SBV), vec, vec],
        out_specs=pl.BlockSpec((T, LANES), lambda b, p: (b, p)),
        out_shape=jax.ShapeDtypeStruct((B * T, SB_W), F32),
        scratch_shapes=[pltpu.VMEM((T, LANES), F32)] * 2,
        compiler_params=_params(dimension_semantics=("arbitrary", "arbitrary")),
    )(proj, proj, proj, q_gain, k_gain)


def _sb_bwd_call(proj, dmix, q_gain, k_gain, B, T):
    nblk = T // SBQ

    def body(q_ref, k_ref, v_ref, qg_ref, kg_ref, do_ref, dq_ref, dk_ref, dv_ref, dqg_ref, dkg_ref,
             q_s, k_s, dq_s, dk_s, dv_s, c_s):
        b_i, p = pl.program_id(0), pl.program_id(1)
        qn, q_vjp = jax.vjp(_group_rms, q_ref[...], qg_ref[...])
        kn, k_vjp = jax.vjp(_group_rms, k_ref[...], kg_ref[...])
        q_s[...], k_s[...] = qn, kn
        dk_s[...] = jnp.zeros_like(dk_s)
        dv_s[...] = jnp.zeros_like(dv_s)

        def qblock(i, _):
            ri = pl.ds(pl.multiple_of(i * SBQ, SBQ), SBQ)
            q = q_s[ri, :]
            dacc = _sb_unfold(do_ref[ri, :])

            def carries(jj, carry):
                j = i - jj
                rj = pl.ds(pl.multiple_of(j * SBQ, SBQ), SBQ)
                c_s[j] = carry
                return carry + _sb_rowsum(q, k_s[rj, :], j == i)

            lax.fori_loop(0, i + 1, carries, jnp.zeros((2 * SBQ, 1), F32))

            def kblock(j, c):
                rj = pl.ds(pl.multiple_of(j * SBQ, SBQ), SBQ)
                f = lambda q_, k_, v_, c_: _sb_block(q_, k_, v_, c_, j == i)
                _, vjp = jax.vjp(f, q, k_s[rj, :], v_ref[rj, :], c_s[j])
                dq_j, dk_j, dv_j, dc = vjp((dacc, c[1]))
                dk_s[rj, :] += dk_j
                dv_s[rj, :] += dv_j
                return c[0] + dq_j, dc

            dq, _dc = lax.fori_loop(0, i + 1, kblock, (jnp.zeros((SBQ, LANES), F32), jnp.zeros((2 * SBQ, 1), F32)))
            dq_s[ri, :] = dq
            return 0

        lax.fori_loop(0, nblk, qblock, 0)
        dq_in, dqg = q_vjp(dq_s[...])
        dk_in, dkg = k_vjp(dk_s[...])
        dq_ref[...], dk_ref[...], dv_ref[...] = dq_in, dk_in, dv_s[...]

        @pl.when(jnp.logical_and(b_i == 0, p == 0))
        def _():
            dqg_ref[...] = jnp.zeros_like(dqg_ref)
            dkg_ref[...] = jnp.zeros_like(dkg_ref)

        dqg_ref[...] += dqg + pltpu.roll(dqg, 64, 1)
        dkg_ref[...] += dkg + pltpu.roll(dkg, 64, 1)

    M = B * T
    col = lambda c0: pl.BlockSpec((T, LANES), lambda b, p, c0=c0: (b, c0 + p))
    vec = pl.BlockSpec((1, LANES), lambda b, p: (0, 0))
    blk = pl.BlockSpec((T, LANES), lambda b, p: (b, p))
    big = [pltpu.VMEM((T, LANES), F32)]
    return pl.pallas_call(
        body, name="sb_bwd", grid=(B, 2),
        in_specs=[col(CB_SBQ), col(CB_SBK), col(CB_SBV), vec, vec, pl.BlockSpec((T, LANES), lambda b, p: (b, 4 + p))],
        out_specs=[blk, blk, blk, vec, vec],
        out_shape=[jax.ShapeDtypeStruct((M, SB_W), F32)] * 3 + [jax.ShapeDtypeStruct((1, LANES), F32)] * 2,
        scratch_shapes=big * 5 + [pltpu.VMEM((nblk, 2 * SBQ, 1), F32)],
        compiler_params=_params(dimension_semantics=("arbitrary", "arbitrary")),
    )(proj, proj, proj, q_gain, k_gain, dmix)


def _sg_chunk(u, v, gain, w_a, w_b, bias):
    n = 128
    row, col = _iota((n, n), 0), _iota((n, n), 1)
    first = _iota((1, LANES), 1) < 64
    vn = _group_rms(_gelu(v), gain)
    tril = col <= row
    mixed = jnp.where(first, _dot(jnp.where(tril, w_a, 0.0), vn), _dot(jnp.where(tril, w_b, 0.0), vn)) + bias
    return _gelu(u) * mixed


def _sg_fwd_call(proj, gain, sg_w, bias, B, T):
    nchunk = T // 128

    def body(u_ref, v_ref, g_ref, wa_ref, wb_ref, bias_ref, out_ref):
        def step(i, _):
            r = pl.ds(pl.multiple_of(i * 128, 128), 128)
            out_ref[r, :] = _sg_chunk(u_ref[r, :], v_ref[r, :], g_ref[...], wa_ref[0], wb_ref[0], bias_ref[...])
            return 0

        lax.fori_loop(0, nchunk, step, 0)

    col = lambda c0: pl.BlockSpec((T, LANES), lambda b, p, c0=c0: (b, c0 + p))
    return pl.pallas_call(
        body, name="sg_fwd", grid=(B, 2),
        in_specs=[col(CB_SGU), col(CB_SGV), pl.BlockSpec((1, LANES), lambda b, p: (0, p)),
                  pl.BlockSpec((1, 128, 128), lambda b, p: (2 * p, 0, 0)), pl.BlockSpec((1, 128, 128), lambda b, p: (2 * p + 1, 0, 0)),
                  pl.BlockSpec((128, LANES), lambda b, p: (0, p))],
        out_specs=pl.BlockSpec((T, LANES), lambda b, p: (b, p)),
        out_shape=jax.ShapeDtypeStruct((B * T, SG_W), F32),
        compiler_params=_params(dimension_semantics=("arbitrary", "arbitrary")),
    )(proj, proj, gain, sg_w, sg_w, bias)


def _sg_bwd_call(proj, dmix, gain, sg_w, bias, B, T):
    nchunk = T // 128

    def body(u_ref, v_ref, g_ref, wa_ref, wb_ref, bias_ref, do_ref, du_ref, dv_ref, dg_ref, dw_ref, db_ref):
        p, b_i = pl.program_id(0), pl.program_id(1)

        def step(i, c):
            r = pl.ds(pl.multiple_of(i * 128, 128), 128)
            _, vjp = jax.vjp(_sg_chunk, u_ref[r, :], v_ref[r, :], g_ref[...], wa_ref[0], wb_ref[0], bias_ref[...])
            du, dv, dg, dwa, dwb, dbias = vjp(do_ref[r, :])
            du_ref[r, :], dv_ref[r, :] = du, dv
            return c[0] + dg, c[1] + dwa, c[2] + dwb, c[3] + dbias

        z = jnp.zeros((128, 128), F32)
        dg, dwa, dwb, dbias = lax.fori_loop(0, nchunk, step, (jnp.zeros((1, LANES), F32), z, z, z))
        lane = _iota((1, LANES), 1)
        first = lane < 64
        s_a = jnp.sum(jnp.where(first, dbias, 0.0), axis=-1, keepdims=True)
        s_b = jnp.sum(jnp.where(first, 0.0, dbias), axis=-1, keepdims=True)
        dbg = jnp.where(lane == 2 * p, s_a, 0.0) + jnp.where(lane == 2 * p + 1, s_b, 0.0)

        @pl.when(b_i == 0)
        def _():
            dg_ref[...] = jnp.zeros_like(dg_ref)
            dw_ref[...] = jnp.zeros_like(dw_ref)

        @pl.when(jnp.logical_and(b_i == 0, p == 0))
        def _():
            db_ref[...] = jnp.zeros_like(db_ref)

        dg_ref[...] += dg
        dw_ref[0] += dwa
        dw_ref[1] += dwb
        db_ref[...] += dbg

    M = B * T
    col = lambda c0: pl.BlockSpec((T, LANES), lambda p, b, c0=c0: (b, c0 + p))
    blk = pl.BlockSpec((T, LANES), lambda p, b: (b, p))
    return pl.pallas_call(
        body, name="sg_bwd", grid=(2, B),
        in_specs=[col(CB_SGU), col(CB_SGV), pl.BlockSpec((1, LANES), lambda p, b: (0, p)),
                  pl.BlockSpec((1, 128, 128), lambda p, b: (2 * p, 0, 0)), pl.BlockSpec((1, 128, 128), lambda p, b: (2 * p + 1, 0, 0)),
                  pl.BlockSpec((128, LANES), lambda p, b: (0, p)), pl.BlockSpec((T, LANES), lambda p, b: (b, 6 + p))],
        out_specs=[blk, blk, pl.BlockSpec((1, LANES), lambda p, b: (0, p)), pl.BlockSpec((2, 128, 128), lambda p, b: (p, 0, 0)),
                   pl.BlockSpec((128, LANES), lambda p, b: (0, 0))],
        out_shape=[jax.ShapeDtypeStruct((M, SG_W), F32)] * 2 + [jax.ShapeDtypeStruct((1, SG_W), F32),
                                                                jax.ShapeDtypeStruct((4, 128, 128), F32),
                                                                jax.ShapeDtypeStruct((128, LANES), F32)],
        compiler_params=_params(dimension_semantics=("arbitrary", "arbitrary")),
    )(proj, proj, gain, sg_w, sg_w, bias, dmix)


def _row_tile(m):
    return min(m, 512)


def _col_tile(n):
    if n <= 1152:
        return n
    return 1024 if n % 1024 == 0 else 1152


def _in_proj_call(x, gain, w):
    m, n = x.shape[0], w.shape[1]
    tm, tn = _row_tile(m), _col_tile(n)

    def body(x_ref, g_ref, w_ref, out_ref, h_ref):
        @pl.when(pl.program_id(1) == 0)
        def _():
            h_ref[...] = _rms(x_ref[...], g_ref[...]).astype(BF16)

        out_ref[...] = jnp.dot(h_ref[...], w_ref[...], preferred_element_type=F32)

    return pl.pallas_call(
        body, name="in_proj", grid=(m // tm, n // tn),
        in_specs=[pl.BlockSpec((tm, D), lambda i, j: (i, 0)), pl.BlockSpec((1, D), lambda i, j: (0, 0)),
                  pl.BlockSpec((D, tn), lambda i, j: (0, j))],
        out_specs=[pl.BlockSpec((tm, tn), lambda i, j: (i, j)), pl.BlockSpec((tm, D), lambda i, j: (i, 0))],
        out_shape=[jax.ShapeDtypeStruct((m, n), F32), jax.ShapeDtypeStruct((m, D), BF16)],
        compiler_params=_params(dimension_semantics=("arbitrary", "arbitrary")),
    )(x, gain, w)


def _out_proj_call(a, w, res):
    m, k = a.shape
    n = w.shape[1]
    tm = _row_tile(m)

    def body(a_ref, w_ref, res_ref, out_ref):
        out_ref[...] = res_ref[...] + jnp.dot(a_ref[...].astype(BF16), w_ref[...], preferred_element_type=F32)

    return pl.pallas_call(
        body, name="out_proj", grid=(m // tm,),
        in_specs=[pl.BlockSpec((tm, k), lambda i: (i, 0)), pl.BlockSpec((k, n), lambda i: (0, 0)),
                  pl.BlockSpec((tm, n), lambda i: (i, 0))],
        out_specs=pl.BlockSpec((tm, n), lambda i: (i, 0)),
        out_shape=jax.ShapeDtypeStruct((m, n), F32),
        compiler_params=_params(dimension_semantics=("arbitrary",)),
    )(a, w, res)


def _ffn_fwd_call(x, gain, w1, w2):
    m = x.shape[0]
    tm, tf = _row_tile(m), 1024
    nf = DFF // tf

    def body(x_ref, g_ref, w1_ref, w2_ref, out_ref, h_s, acc_s):
        j = pl.program_id(1)

        @pl.when(j == 0)
        def _():
            h_s[...] = _rms(x_ref[...], g_ref[...]).astype(BF16)
            acc_s[...] = jnp.zeros_like(acc_s)

        a = jnp.maximum(jnp.dot(h_s[...], w1_ref[...], preferred_element_type=F32), 0.0)
        acc_s[...] += jnp.dot((a * a).astype(BF16), w2_ref[...], preferred_element_type=F32)

        @pl.when(j == nf - 1)
        def _():
            out_ref[...] = x_ref[...] + acc_s[...]

    return pl.pallas_call(
        body, name="ffn_fwd", grid=(m // tm, nf),
        in_specs=[pl.BlockSpec((tm, D), lambda i, j: (i, 0)), pl.BlockSpec((1, D), lambda i, j: (0, 0)),
                  pl.BlockSpec((D, tf), lambda i, j: (0, j)), pl.BlockSpec((tf, D), lambda i, j: (j, 0))],
        out_specs=pl.BlockSpec((tm, D), lambda i, j: (i, 0)),
        out_shape=jax.ShapeDtypeStruct((m, D), F32),
        scratch_shapes=[pltpu.VMEM((tm, D), BF16), pltpu.VMEM((tm, D), F32)],
        compiler_params=_params(dimension_semantics=("arbitrary", "arbitrary")),
    )(x, gain, w1, w2)


def _ffn_bwd_call(x, dy, gain, w1, w2):
    m = x.shape[0]
    tm, tf = _row_tile(m), 1024
    nf = DFF // tf

    def body(x_ref, dy_ref, g_ref, w1_ref, w2_ref, dx_ref, da_ref, r_ref, h_ref, dg_ref, acc_s):
        i, j = pl.program_id(0), pl.program_id(1)

        @pl.when(j == 0)
        def _():
            h_ref[...] = _rms(x_ref[...], g_ref[...]).astype(BF16)
            acc_s[...] = jnp.zeros_like(acc_s)

        @pl.when(jnp.logical_and(i == 0, j == 0))
        def _():
            dg_ref[...] = jnp.zeros_like(dg_ref)

        a = jnp.maximum(jnp.dot(h_ref[...], w1_ref[...], preferred_element_type=F32), 0.0)
        r_ref[...] = (a * a).astype(BF16)
        dr = lax.dot_general(dy_ref[...].astype(BF16), w2_ref[...], (((1,), (1,)), ((), ())), preferred_element_type=F32)
        da = (dr * (2.0 * a)).astype(BF16)
        da_ref[...] = da
        acc_s[...] += lax.dot_general(da, w1_ref[...], (((1,), (1,)), ((), ())), preferred_element_type=F32)

        @pl.when(j == nf - 1)
        def _():
            _, vjp = jax.vjp(_rms, x_ref[...], g_ref[...])
            dx, dg = vjp(acc_s[...])
            dx_ref[...] = dy_ref[...] + dx
            dg_ref[...] += dg

    return pl.pallas_call(
        body, name="ffn_bwd", grid=(m // tm, nf),
        in_specs=[pl.BlockSpec((tm, D), lambda i, j: (i, 0)), pl.BlockSpec((tm, D), lambda i, j: (i, 0)),
                  pl.BlockSpec((1, D), lambda i, j: (0, 0)),
                  pl.BlockSpec((D, tf), lambda i, j: (0, j)), pl.BlockSpec((tf, D), lambda i, j: (j, 0))],
        out_specs=[pl.BlockSpec((tm, D), lambda i, j: (i, 0)), pl.BlockSpec((tm, tf), lambda i, j: (i, j)),
                   pl.BlockSpec((tm, tf), lambda i, j: (i, j)), pl.BlockSpec((tm, D), lambda i, j: (i, 0)),
                   pl.BlockSpec((1, D), lambda i, j: (0, 0))],
        out_shape=[jax.ShapeDtypeStruct((m, D), F32), jax.ShapeDtypeStruct((m, DFF), BF16), jax.ShapeDtypeStruct((m, DFF), BF16),
                   jax.ShapeDtypeStruct((m, D), BF16), jax.ShapeDtypeStruct((1, D), F32)],
        scratch_shapes=[pltpu.VMEM((tm, D), F32)],
        compiler_params=_params(dimension_semantics=("arbitrary", "arbitrary")),
    )(x, dy, gain, w1, w2)


def _mm_nt_call(a, b, name):
    m, k = a.shape
    n = b.shape[0]
    tm = _row_tile(m)

    def body(a_ref, b_ref, out_ref):
        out_ref[...] = lax.dot_general(a_ref[...].astype(BF16), b_ref[...].astype(BF16), (((1,), (1,)), ((), ())),
                                       preferred_element_type=F32)

    return pl.pallas_call(
        body, name=name, grid=(m // tm,),
        in_specs=[pl.BlockSpec((tm, k), lambda i: (i, 0)), pl.BlockSpec((n, k), lambda i: (0, 0))],
        out_specs=pl.BlockSpec((tm, n), lambda i: (i, 0)),
        out_shape=jax.ShapeDtypeStruct((m, n), F32),
        compiler_params=_params(dimension_semantics=("arbitrary",)),
    )(a, b)


def _in_proj_bwd_call(dproj, w, x, gain, dres):
    m, n = dproj.shape
    tm = _row_tile(m)

    def body(dp_ref, w_ref, x_ref, g_ref, dres_ref, dx_ref, dg_ref):
        @pl.when(pl.program_id(0) == 0)
        def _():
            dg_ref[...] = jnp.zeros_like(dg_ref)

        dh = lax.dot_general(dp_ref[...].astype(BF16), w_ref[...], (((1,), (1,)), ((), ())), preferred_element_type=F32)
        _, vjp = jax.vjp(_rms, x_ref[...], g_ref[...])
        dx, dg = vjp(dh)
        dx_ref[...] = dres_ref[...] + dx
        dg_ref[...] += dg

    return pl.pallas_call(
        body, name="in_proj_bwd", grid=(m // tm,),
        in_specs=[pl.BlockSpec((tm, n), lambda i: (i, 0)), pl.BlockSpec((D, n), lambda i: (0, 0)),
                  pl.BlockSpec((tm, D), lambda i: (i, 0)), pl.BlockSpec((1, D), lambda i: (0, 0)),
                  pl.BlockSpec((tm, D), lambda i: (i, 0))],
        out_specs=[pl.BlockSpec((tm, D), lambda i: (i, 0)), pl.BlockSpec((1, D), lambda i: (0, 0))],
        out_shape=[jax.ShapeDtypeStruct((m, D), F32), jax.ShapeDtypeStruct((1, D), F32)],
        compiler_params=_params(dimension_semantics=("arbitrary",)),
    )(dproj, w, x, gain, dres)


def _mm_tn_call(a, b, name):
    m, k = a.shape
    n = b.shape[1]
    tm, tk, tn = _row_tile(m), min(k, 1024), _col_tile(n)

    def body(a_ref, b_ref, out_ref):
        @pl.when(pl.program_id(2) == 0)
        def _():
            out_ref[...] = jnp.zeros_like(out_ref)

        out_ref[...] += lax.dot_general(a_ref[...].astype(BF16), b_ref[...].astype(BF16), (((0,), (0,)), ((), ())),
                                        preferred_element_type=F32)

    return pl.pallas_call(
        body, name=name, grid=(k // tk, n // tn, m // tm),
        in_specs=[pl.BlockSpec((tm, tk), lambda i, j, s: (s, i)), pl.BlockSpec((tm, tn), lambda i, j, s: (s, j))],
        out_specs=pl.BlockSpec((tk, tn), lambda i, j, s: (i, j)),
        out_shape=jax.ShapeDtypeStruct((k, n), F32),
        compiler_params=_params(dimension_semantics=("arbitrary", "arbitrary", "arbitrary")),
    )(a, b)


def _loss_call(y, target):
    m = y.shape[0]
    tm = _row_tile(m)

    def body(y_ref, t_ref, loss_ref, dy_ref):
        @pl.when(pl.program_id(0) == 0)
        def _():
            loss_ref[...] = jnp.zeros_like(loss_ref)

        err = y_ref[...] - t_ref[...]
        dy_ref[...] = err * (1.0 / D)
        per_row = jnp.mean(err * err, axis=-1, keepdims=True)
        loss_ref[...] += jnp.broadcast_to(0.5 * jnp.sum(per_row, axis=0, keepdims=True), (1, LANES))

    return pl.pallas_call(
        body, name="loss", grid=(m // tm,),
        in_specs=[pl.BlockSpec((tm, D), lambda i: (i, 0))] * 2,
        out_specs=[pl.BlockSpec((1, LANES), lambda i: (0, 0)), pl.BlockSpec((tm, D), lambda i: (i, 0))],
        out_shape=[jax.ShapeDtypeStruct((1, LANES), F32), jax.ShapeDtypeStruct((m, D), F32)],
        compiler_params=_params(dimension_semantics=("arbitrary",)),
    )(y, target)


def _adamw_call(w, g, m, v, name):
    shape = w.shape
    cols = shape[-1] if w.ndim > 1 else w.size
    rows = w.size // cols
    tr = rows if (rows <= 512 or rows % 512) else 512
    c1, c2 = 1.0 - ADAM_B1 ** ADAM_STEP, 1.0 - ADAM_B2 ** ADAM_STEP

    def body(w_ref, g_ref, m_ref, v_ref, d_ref, nm_ref, nv_ref):
        g_ = g_ref[...]
        nm = ADAM_B1 * m_ref[...] + (1.0 - ADAM_B1) * g_
        nv = ADAM_B2 * v_ref[...] + (1.0 - ADAM_B2) * (g_ * g_)
        d_ref[...] = -ADAM_LR * ((nm / c1) / (jnp.sqrt(nv / c2) + ADAM_EPS) + ADAM_WD * w_ref[...])
        nm_ref[...], nv_ref[...] = nm, nv

    spec = pl.BlockSpec((tr, cols), lambda i: (i, 0))
    outs = pl.pallas_call(
        body, name=name, grid=(rows // tr,), in_specs=[spec] * 4, out_specs=[spec] * 3,
        out_shape=[jax.ShapeDtypeStruct((rows, cols), F32)] * 3,
        compiler_params=_params(dimension_semantics=("arbitrary",)),
    )(*(t.reshape(rows, cols) for t in (w, g, m, v)))
    return tuple(o.reshape(shape) for o in outs)


def _sum_call(parts, out_dtype, name):
    rows, cols = parts[0].shape
    tr = rows
    for cand in (8368, 4096, 2048, 1024, 512):
        if rows > cand and rows % cand == 0:
            tr = cand
            break

    def body(*refs):
        acc = refs[0][...].astype(F32)
        for r in refs[1:-1]:
            acc = acc + r[...].astype(F32)
        refs[-1][...] = acc.astype(out_dtype)

    spec = pl.BlockSpec((tr, cols), lambda i: (i, 0))
    return pl.pallas_call(
        body, name=name, grid=(rows // tr,), in_specs=[spec] * len(parts), out_specs=spec,
        out_shape=jax.ShapeDtypeStruct((rows, cols), out_dtype),
        compiler_params=_params(dimension_semantics=("arbitrary",)),
    )(*parts)


def _place():
    return lax.axis_index("x"), lax.axis_index("y"), lax.axis_index("c")


def _all_gather_call(x, name):
    rows, cols = x.shape

    def body(x_ref, out_ref, send_sems, recv_sems, local_sem):
        ax, ay, ac = _place()
        me, sibling = (ax, ay, ac), (ax, ay, 1 - ac)
        chips = [(1 - ax, ay), (ax, 1 - ay), (1 - ax, 1 - ay)]

        def slot(px, py, pc):
            return out_ref.at[4 * px + 2 * py + pc]

        def copy(k, block, to, src=None):
            return pltpu.make_async_remote_copy(
                src_ref=slot(*block) if src is None else src, dst_ref=slot(*block),
                send_sem=send_sems.at[k], recv_sem=recv_sems.at[k], device_id=to, device_id_type=MESH)

        mine = pltpu.make_async_copy(x_ref, slot(*me), local_sem)
        mine.start()
        first = [copy(0, me, sibling, src=x_ref)]
        first += [copy(1 + j, me, (*chip, ac), src=x_ref) for j, chip in enumerate(chips)]
        for cp in first:
            cp.start()
        passed = [copy(4 + j, (*chip, ac), sibling) for j, chip in enumerate(chips)]
        for j, chip in enumerate(chips):
            copy(1 + j, (*chip, ac), me).wait_recv()
            passed[j].start()
        copy(0, sibling, me).wait_recv()
        for j, chip in enumerate(chips):
            copy(4 + j, (*chip, 1 - ac), me).wait_recv()
        for cp in first + passed:
            cp.wait_send()
        mine.wait()

    return pl.pallas_call(
        body, name=name,
        in_specs=[pl.BlockSpec(memory_space=pl.ANY)], out_specs=pl.BlockSpec(memory_space=pl.ANY),
        out_shape=jax.ShapeDtypeStruct((NDEV, rows, cols), x.dtype),
        scratch_shapes=[pltpu.SemaphoreType.DMA((7,)), pltpu.SemaphoreType.DMA((7,)), pltpu.SemaphoreType.DMA],
    )(x)


def _swap_sibling_call(x, name):
    def body(x_ref, out_ref, send_sem, recv_sem):
        ax, ay, ac = _place()
        cp = pltpu.make_async_remote_copy(src_ref=x_ref, dst_ref=out_ref, send_sem=send_sem, recv_sem=recv_sem,
                                          device_id=(ax, ay, 1 - ac), device_id_type=MESH)
        cp.start()
        cp.wait()

    return pl.pallas_call(
        body, name=name,
        in_specs=[pl.BlockSpec(memory_space=pl.ANY)], out_specs=pl.BlockSpec(memory_space=pl.ANY),
        out_shape=jax.ShapeDtypeStruct(x.shape, x.dtype),
        scratch_shapes=[pltpu.SemaphoreType.DMA, pltpu.SemaphoreType.DMA],
    )(x)


def _swap_chips_call(x, name):
    def body(x_ref, out_ref, send_sems, recv_sems):
        ax, ay, ac = _place()
        chips = [(1 - ax, ay), (ax, 1 - ay), (1 - ax, 1 - ay)]
        copies = [pltpu.make_async_remote_copy(src_ref=x_ref.at[2 * cx + cy], dst_ref=out_ref.at[j],
                                               send_sem=send_sems.at[j], recv_sem=recv_sems.at[j],
                                               device_id=(cx, cy, ac), device_id_type=MESH)
                  for j, (cx, cy) in enumerate(chips)]
        for cp in copies:
            cp.start()
        for cp in copies:
            cp.wait()

    return pl.pallas_call(
        body, name=name,
        in_specs=[pl.BlockSpec(memory_space=pl.ANY)], out_specs=pl.BlockSpec(memory_space=pl.ANY),
        out_shape=jax.ShapeDtypeStruct((3,) + x.shape[1:], x.dtype),
        scratch_shapes=[pltpu.SemaphoreType.DMA((3,)), pltpu.SemaphoreType.DMA((3,))],
    )(x)


def _reduce_scatter(g, name):
    ax, ay, ac = _place()
    rows, cols = g.shape[1:]
    by_core = g.reshape(4, 2, rows, cols)
    keep = lax.dynamic_index_in_dim(by_core, ac, axis=1, keepdims=False)
    give = lax.dynamic_index_in_dim(by_core, 1 - ac, axis=1, keepdims=False)
    got = _swap_sibling_call(give, name + "_d2d")
    chip_sum = _sum_call([keep.reshape(4 * rows, cols), got.reshape(4 * rows, cols)], BF16, name + "_pair").reshape(4, rows, cols)
    from_chips = _swap_chips_call(chip_sum, name + "_ici")
    my_chip = 2 * ax + ay
    own = [lax.dynamic_index_in_dim(t, my_chip, axis=0, keepdims=False) for t in (keep, got)]
    return _sum_call(own + [from_chips[j] for j in range(3)], F32, name + "_total")


BIG = ("w_in", "w_out", "w_ff1", "w_ff2")
SMALL = ("norm1_g", "conv_w", "a_log", "dt_bias", "dn_out_g", "sb_q_g", "sb_k_g", "sg_v_g", "sg_w", "sg_b", "norm2_g")
WEIGHTS = ("norm1_g", "w_in", "conv_w", "a_log", "dt_bias", "dn_out_g", "sb_q_g", "sb_k_g", "sg_v_g", "sg_w", "sg_b",
           "w_out", "norm2_g", "w_ff1", "w_ff2")
BIG_SHARD = {"w_in": (D, IN_DIM // NDEV), "w_out": (D // NDEV, D), "w_ff1": (D, DFF // NDEV), "w_ff2": (DFF // NDEV, D)}
SMALL_SHAPE = {"norm1_g": (D,), "conv_w": (4, 3 * DN_W), "a_log": (NH,), "dt_bias": (NH,), "dn_out_g": (128,), "sb_q_g": (64,),
               "sb_k_g": (64,), "sg_v_g": (SG_W,), "sg_w": (NH, 128, 128), "sg_b": (NH, 128), "norm2_g": (D,)}


def _size(shape):
    n = 1
    for s in shape:
        n *= s
    return n


def _pack_in(w):
    return jnp.concatenate([w[..., :2056], jnp.zeros(w.shape[:-1] + (NPACK - IN_DIM,), w.dtype), w[..., 2056:]], axis=-1)


def _unpack_in(w):
    return jnp.concatenate([w[..., :2056], w[..., 2056 + NPACK - IN_DIM:]], axis=-1)


def _to_rows(flat, multiple):
    pad = (-flat.shape[0]) % (LANES * multiple)
    return jnp.pad(flat, (0, pad)).reshape(-1, LANES)


def _full_from_shards(blocks, name):
    r, c = BIG_SHARD[name]
    t = blocks.reshape(NDEV, 2, r, c)
    if name in ("w_in", "w_ff1"):
        return t.transpose(1, 2, 0, 3).reshape(2, r, NDEV * c)
    return t.transpose(1, 0, 2, 3).reshape(2, NDEV * r, c)


def _shards_from_full(full, name):
    r, c = BIG_SHARD[name]
    if name in ("w_in", "w_ff1"):
        t = full.reshape(2, r, NDEV, c).transpose(2, 0, 1, 3)
    else:
        t = full.reshape(2, NDEV, r, c).transpose(1, 0, 2, 3)
    return t.reshape(NDEV, 2, r * c)


def kernel(x, norm1_g, w_in, conv_w, a_log, dt_bias, dn_out_g, sb_q_g, sb_k_g, sg_v_g, sg_w, sg_b, w_out, norm2_g, w_ff1, w_ff2, loss_target, m_norm1_g, m_w_in, m_conv_w, m_a_log, m_dt_bias, m_dn_out_g, m_sb_q_g, m_sb_k_g, m_sg_v_g, m_sg_w, m_sg_b, m_w_out, m_norm2_g, m_w_ff1, m_w_ff2, v_norm1_g, v_w_in, v_conv_w, v_a_log, v_dt_bias, v_dn_out_g, v_sb_q_g, v_sb_k_g, v_sg_v_g, v_sg_w, v_sg_b, v_w_out, v_norm2_g, v_w_ff1, v_w_ff2):
    given = dict(norm1_g=norm1_g, w_in=w_in, conv_w=conv_w, a_log=a_log, dt_bias=dt_bias, dn_out_g=dn_out_g, sb_q_g=sb_q_g,
                 sb_k_g=sb_k_g, sg_v_g=sg_v_g, sg_w=sg_w, sg_b=sg_b, w_out=w_out, norm2_g=norm2_g, w_ff1=w_ff1, w_ff2=w_ff2)
    mom = dict(norm1_g=m_norm1_g, w_in=m_w_in, conv_w=m_conv_w, a_log=m_a_log, dt_bias=m_dt_bias, dn_out_g=m_dn_out_g,
               sb_q_g=m_sb_q_g, sb_k_g=m_sb_k_g, sg_v_g=m_sg_v_g, sg_w=m_sg_w, sg_b=m_sg_b, w_out=m_w_out, norm2_g=m_norm2_g,
               w_ff1=m_w_ff1, w_ff2=m_w_ff2)
    var = dict(norm1_g=v_norm1_g, w_in=v_w_in, conv_w=v_conv_w, a_log=v_a_log, dt_bias=v_dt_bias, dn_out_g=v_dn_out_g,
               sb_q_g=v_sb_q_g, sb_k_g=v_sb_k_g, sg_v_g=v_sg_v_g, sg_w=v_sg_w, sg_b=v_sg_b, w_out=v_w_out, norm2_g=v_norm2_g,
               w_ff1=v_w_ff1, w_ff2=v_w_ff2)
    B, T, _ = x.shape
    M = B * T
    ax, ay, ac = _place()
    me = 4 * ax + 2 * ay + ac

    sizes = [_size(BIG_SHARD[n]) for n in BIG]
    shard = jnp.concatenate([given[n].reshape(2, -1) for n in BIG], axis=1).astype(BF16)
    gathered = _all_gather_call(shard.reshape(-1, LANES), "gather_weights").reshape(NDEV, 2, -1)
    full, off = {}, 0
    for n, sz in zip(BIG, sizes):
        full[n] = _full_from_shards(gathered[:, :, off:off + sz], n)
        off += sz
    w_pack = _pack_in(full["w_in"])
    conv_full = _all_gather_call(_to_rows(conv_w.reshape(-1), 8), "gather_conv")
    conv_full = conv_full.reshape(NDEV, -1)[:, :conv_w.size].reshape(NDEV, 2, 4, -1).transpose(1, 2, 0, 3).reshape(2, 4, 3 * DN_W)

    pad_vec = lambda v: jnp.zeros((1, LANES), F32).at[0, :v.shape[0]].set(v)
    layer = []
    for l in range(2):
        layer.append(dict(
            g1=norm1_g[l].reshape(1, D), g2=norm2_g[l].reshape(1, D), conv=conv_full[l],
            a_log=pad_vec(a_log[l]), dt_bias=pad_vec(dt_bias[l]), dn_g=dn_out_g[l].reshape(1, LANES),
            sb_qg=jnp.tile(sb_q_g[l], 2).reshape(1, LANES), sb_kg=jnp.tile(sb_k_g[l], 2).reshape(1, LANES),
            sg_g=sg_v_g[l].reshape(1, SG_W), sg_w=sg_w[l], sg_bias=jnp.repeat(sg_b[l].T, 64, axis=1),
            w_pack=w_pack[l], w_out=full["w_out"][l], w1=full["w_ff1"][l], w2=full["w_ff2"][l]))

    cur = x.reshape(M, D)
    saved = []
    for p in layer:
        proj, h = _in_proj_call(cur, p["g1"], p["w_pack"])
        o_dn = _dn_fwd_call(proj, p["conv"], p["a_log"], p["dt_bias"], p["dn_g"], B, T)
        o_sb = _sb_fwd_call(proj, p["sb_qg"], p["sb_kg"], B, T)
        o_sg = _sg_fwd_call(proj, p["sg_g"], p["sg_w"], p["sg_bias"], B, T)
        mix = jnp.concatenate([o_dn, o_sb, o_sg], axis=1)
        x1 = _out_proj_call(mix, p["w_out"], cur)
        x2 = _ffn_fwd_call(x1, p["g2"], p["w1"], p["w2"])
        saved.append(dict(x0=cur, proj=proj, h=h, mix=mix, x1=x1))
        cur = x2
    loss_part, dy = _loss_call(cur, loss_target.reshape(M, D))
    loss = lax.psum(loss_part[0, 0], ("x", "y", "c"))

    big_grads = {n: [None, None] for n in BIG}
    small_grads = {n: [None, None] for n in SMALL}
    for l in (1, 0):
        p, s = layer[l], saved[l]
        dx1, da, r, h2, dg2 = _ffn_bwd_call(s["x1"], dy, p["g2"], p["w1"], p["w2"])
        big_grads["w_ff1"][l] = _mm_tn_call(h2, da, "grad_w_ff1")
        big_grads["w_ff2"][l] = _mm_tn_call(r, dy, "grad_w_ff2")
        dmix = _mm_nt_call(dx1, p["w_out"], "dmix")
        big_grads["w_out"][l] = _mm_tn_call(s["mix"], dx1, "grad_w_out")
        dq, dk, dv, dz, dab, dcw, dalog, ddtb, ddn_g = _dn_bwd_call(s["proj"], dmix, p["conv"], p["a_log"], p["dt_bias"], p["dn_g"], B, T)
        dsq, dsk, dsv, dqg, dkg = _sb_bwd_call(s["proj"], dmix, p["sb_qg"], p["sb_kg"], B, T)
        du, dvv, dsg_g, dsg_w, dsg_b = _sg_bwd_call(s["proj"], dmix, p["sg_g"], p["sg_w"], p["sg_bias"], B, T)
        dproj = jnp.concatenate([dq, dk, dv, dz, dab, dsq, dsk, dsv, du, dvv], axis=1)
        dy, dg1 = _in_proj_bwd_call(dproj, p["w_pack"], s["x0"], p["g1"], dx1)
        big_grads["w_in"][l] = _unpack_in(_mm_tn_call(s["h"], dproj, "grad_w_in"))
        for n, val in (("norm1_g", dg1[0]), ("conv_w", dcw.transpose(1, 0, 2).reshape(4, 3 * DN_W)), ("a_log", dalog[0, :NH]),
                       ("dt_bias", ddtb[0, :NH]), ("dn_out_g", ddn_g[0]), ("sb_q_g", dqg[0, :64]), ("sb_k_g", dkg[0, :64]),
                       ("sg_v_g", dsg_g[0]), ("sg_w", dsg_w), ("sg_b", dsg_b[:, :NH].T), ("norm2_g", dg2[0])):
            small_grads[n][l] = val
    grad_x = dy.reshape(B, T, D)

    send = jnp.concatenate([_shards_from_full(jnp.stack(big_grads[n]), n) for n in BIG], axis=2).astype(BF16)
    mine = _reduce_scatter(send.reshape(NDEV, -1, LANES), "reduce_grads").reshape(2, -1)
    grads, off = {}, 0
    for n, sz in zip(BIG, sizes):
        grads[n] = mine[:, off:off + sz].reshape((2,) + BIG_SHARD[n])
        off += sz
    small_flat = jnp.concatenate([jnp.stack(small_grads[n]).reshape(-1) for n in SMALL])
    small_rows = _to_rows(small_flat, 8)
    everyone = _all_gather_call(small_rows, "gather_small_grads")
    small_sum = _sum_call([everyone[k] for k in range(NDEV)], F32, "sum_small_grads").reshape(-1)
    off = 0
    for n in SMALL:
        sz = 2 * _size(SMALL_SHAPE[n])
        grads[n] = small_sum[off:off + sz].reshape((2,) + SMALL_SHAPE[n])
        off += sz
    cshard = conv_w.shape[-1]
    grads["conv_w"] = lax.dynamic_slice_in_dim(grads["conv_w"], me * cshard, cshard, axis=2)

    deltas, new_m, new_v = {}, {}, {}
    for n in WEIGHTS:
        deltas[n], new_m[n], new_v[n] = _adamw_call(given[n], grads[n], mom[n], var[n], "adamw_" + n)
    return (loss, grad_x, *[grads[n] for n in WEIGHTS], *[deltas[n] for n in WEIGHTS], *[new_m[n] for n in WEIGHTS],
            *[new_v[n] for n in WEIGHTS])
```

```python
import functools

import numpy as np

import jax
import jax.numpy as jnp
from jax import lax
from jax.experimental import pallas as pl
from jax.experimental.pallas import tpu as pltpu

F32, BF16 = jnp.float32, jnp.bfloat16
EPS = 1e-6
LANES = 128
D = 1024
DFF = 4096
NH = 4
DN_W, SB_W, SG_W = 512, 256, 256
IN_DIM = 3336
NDEV = 8
IN_SHARD = IN_DIM // NDEV
IN_SHARD_PAD = 432
FF_SHARD = DFF // NDEV
DN_OFF, SB_OFF, SG_OFF, AB_OFF, NPACK = 0, 2048, 2816, 3328, 3456
SECTIONS = ((DN_OFF, 2048), (SB_OFF, 768), (SG_OFF, 512), (AB_OFF, 128))
SB_SCALE = 64 ** -0.5
DN_SCALE = 128 ** -0.5
VMEM_LIMIT = 56 * 1024 * 1024
ADAM_LR, ADAM_B1, ADAM_B2, ADAM_EPS, ADAM_WD, ADAM_STEP = 0.001, 0.9, 0.999, 1e-08, 0.01, 10
MESH = pl.DeviceIdType.MESH


def _iota(shape, dim):
    return lax.broadcasted_iota(jnp.int32, shape, dim)


def _params(**kw):
    return pltpu.CompilerParams(vmem_limit_bytes=VMEM_LIMIT, **kw)


def _mm(a, b, dims):
    return lax.dot_general(a.astype(BF16), b.astype(BF16), (dims, ((), ())), preferred_element_type=F32)


@jax.custom_vjp
def _dot(a, b):
    return _mm(a, b, ((1,), (0,)))


def _dot_fwd(a, b):
    return _dot(a, b), (a, b)


def _dot_bwd(res, g):
    a, b = res
    return _mm(g, b, ((1,), (1,))).astype(a.dtype), _mm(a, g, ((0,), (0,))).astype(b.dtype)


_dot.defvjp(_dot_fwd, _dot_bwd)


@jax.custom_vjp
def _dot_nt(a, b):
    return _mm(a, b, ((1,), (1,)))


def _dot_nt_fwd(a, b):
    return _dot_nt(a, b), (a, b)


def _dot_nt_bwd(res, g):
    a, b = res
    return _mm(g, b, ((1,), (0,))).astype(a.dtype), _mm(g, a, ((0,), (0,))).astype(b.dtype)


_dot_nt.defvjp(_dot_nt_fwd, _dot_nt_bwd)


@jax.custom_vjp
def _dot_tn(a, b):
    return _mm(a, b, ((0,), (0,)))


def _dot_tn_fwd(a, b):
    return _dot_tn(a, b), (a, b)


def _dot_tn_bwd(res, g):
    a, b = res
    return _mm(b, g, ((1,), (1,))).astype(a.dtype), _mm(a, g, ((1,), (0,))).astype(b.dtype)


_dot_tn.defvjp(_dot_tn_fwd, _dot_tn_bwd)


def _split(x):
    hi = x.astype(BF16)
    return hi, (x - hi.astype(F32)).astype(BF16)


def _mm3(a, b, dims):
    (ah, al), (bh, bl) = _split(a), _split(b)
    mm = lambda x, y: lax.dot_general(x, y, (dims, ((), ())), preferred_element_type=F32)
    return mm(ah, bh) + (mm(ah, bl) + mm(al, bh))


def _mm_ones(ones, x, ones_left):
    hi, lo = _split(x)
    mm = (lambda t: jnp.dot(ones, t, preferred_element_type=F32)) if ones_left else \
         (lambda t: jnp.dot(t, ones, preferred_element_type=F32))
    return mm(hi) + mm(lo)


def _pair_ones(kind, transposed):
    row, col = _iota((128, 128), 0), _iota((128, 128), 1)
    m = (row // 64) == (col // 64)
    if kind == "running":
        m = jnp.logical_and(m, (col >= row) if transposed else (col <= row))
    return jnp.where(m, 1.0, 0.0).astype(BF16)


@functools.partial(jax.custom_vjp, nondiff_argnums=(0,))
def _chunk_sum(kind, x):
    return _mm_ones(_pair_ones(kind, False), x, True)


def _chunk_sum_fwd(kind, x):
    return _chunk_sum(kind, x), None


def _chunk_sum_bwd(kind, _, g):
    return (_mm_ones(_pair_ones(kind, True), g, True),)


_chunk_sum.defvjp(_chunk_sum_fwd, _chunk_sum_bwd)


def _tri_ones(n, transposed):
    row, col = _iota((n, n), 0), _iota((n, n), 1)
    return jnp.where((row < col) if transposed else (row > col), 1.0, 0.0).astype(BF16)


@jax.custom_vjp
def _suffix_sum(x):
    return _mm_ones(_tri_ones(x.shape[1], False), x, False)


def _suffix_sum_fwd(x):
    return _suffix_sum(x), None


def _suffix_sum_bwd(_, g):
    return (_mm_ones(_tri_ones(g.shape[1], True), g, False),)


_suffix_sum.defvjp(_suffix_sum_fwd, _suffix_sum_bwd)


def _sigmoid(x):
    return jax.nn.sigmoid(x)


def _silu(x):
    return x * _sigmoid(x)


def _softplus(x):
    return jnp.maximum(x, 0.0) + jnp.log1p(jnp.exp(-jnp.abs(x)))


def _gelu(x):
    return 0.5 * x * (1.0 + jnp.tanh(0.7978845608028654 * (x + 0.044715 * (x * x * x))))


def _rms(x, gain):
    return x * lax.rsqrt(jnp.mean(x * x, axis=-1, keepdims=True) + EPS) * gain


def _shift_down_impl(x, k):
    return jnp.where(_iota(x.shape, 0) >= k, pltpu.roll(x, k, 0), 0.0)


def _shift_up_impl(x, k):
    n = x.shape[0]
    return jnp.where(_iota(x.shape, 0) < n - k, pltpu.roll(x, n - k, 0), 0.0)


@functools.partial(jax.custom_vjp, nondiff_argnums=(1,))
def _shift_down(x, k):
    return _shift_down_impl(x, k)


def _shift_down_fwd(x, k):
    return _shift_down_impl(x, k), None


def _shift_down_bwd(k, _, g):
    return (_shift_up_impl(g, k),)


_shift_down.defvjp(_shift_down_fwd, _shift_down_bwd)


def _lane_pick(x, idx):
    return jnp.sum(jnp.where(_iota(x.shape, 1) == idx, x, 0.0), axis=-1, keepdims=True)


def _dn_conv(x, w0, w1, w2, w3, l2_scale):
    y = _silu(w3 * x + w2 * _shift_down(x, 1) + w1 * _shift_down(x, 2) + w0 * _shift_down(x, 3))
    if l2_scale is None:
        return y
    return y * lax.rsqrt(jnp.sum(y * y, axis=-1, keepdims=True) + EPS) * l2_scale


def _dn_gate(a, b, a_log, dt_bias):
    return -jnp.exp(a_log) * _softplus(a + dt_bias), _sigmoid(b)


def _unit_lower_inverse(lower):
    n = lower.shape[0]
    nk = -lower
    inv = jnp.where(_iota((n, n), 0) == _iota((n, n), 1), 1.0, 0.0) + nk
    for _ in range(5):
        nk = _mm3(nk, nk, ((1,), (0,)))
        inv = inv + _mm3(inv, nk, ((1,), (0,)))
    return inv


@jax.custom_vjp
def _solve_with(lower, inv, rhs):
    return _mm3(inv, rhs, ((1,), (0,)))


def _solve_with_fwd(lower, inv, rhs):
    x = _mm3(inv, rhs, ((1,), (0,)))
    return x, (inv, x)


def _solve_with_bwd(res, g):
    inv, x = res
    d_rhs = _mm3(inv, g, ((0,), (0,)))
    return -_mm3(d_rhs, x, ((1,), (1,))), jnp.zeros_like(inv), d_rhs


_solve_with.defvjp(_solve_with_fwd, _solve_with_bwd)


def _dn_pair(q, k, v, g, beta, s0, inv=None):
    n = 128
    row, col = _iota((n, n), 0), _iota((n, n), 1)
    same = (row // 64) == (col // 64)
    tri_incl = jnp.logical_and(same, col <= row)
    tri_strict = jnp.logical_and(same, col < row)
    first = row < 64
    gb = jnp.broadcast_to(g, (n, n))
    gc = _chunk_sum("running", gb)
    gl = _chunk_sum("total", gb)
    diff = gc - gc.T
    decay = jnp.where(tri_incl, jnp.exp(jnp.where(tri_incl, diff, 0.0)), 0.0)
    egc = jnp.exp(gc)
    kk = _dot_nt(k, k)
    lower = jnp.where(tri_strict, beta * kk * decay, 0.0)
    if inv is None:
        inv = _unit_lower_inverse(lower)
    u_val = _solve_with(lower, inv, v * beta)
    w_dec = _solve_with(lower, inv, k * (beta * egc))
    qk = jnp.where(tri_incl, _dot_nt(q, k) * decay, 0.0)
    q_dec = q * egc
    k_dec = k * jnp.exp(gl - gc)
    cd1 = jnp.exp(jnp.sum(jnp.where(first, gb, 0.0), axis=0, keepdims=True))
    cd2 = jnp.exp(jnp.sum(jnp.where(first, 0.0, gb), axis=0, keepdims=True))
    u1 = u_val - _dot(w_dec, s0)
    s1 = s0 * cd1 + _dot_tn(jnp.where(first, k_dec, 0.0), u1)
    u2 = u_val - _dot(w_dec, s1)
    u_new = jnp.where(first, u1, u2)
    s2 = s1 * cd2 + _dot_tn(jnp.where(first, 0.0, k_dec), u_new)
    o = jnp.where(first, _dot(q_dec, s0), _dot(q_dec, s1)) + _dot(qk, u_new)
    return o, s2, inv


def _dn_post(o, z, gain):
    return _rms(o, gain) * _silu(z)


def _dn_gate_in(ab_ref, alog_ref, dtb_ref, h):
    ab = ab_ref[...]
    return _lane_pick(ab, h), _lane_pick(ab, h + NH), _lane_pick(alog_ref[...], h), _lane_pick(dtb_ref[...], h)


_DN_L2 = (DN_SCALE, 1.0, None)
DN_HPS = 2
DN_BLK = 4 * DN_HPS * LANES
_DN_COLS = tuple(slice(i * LANES, (i + 1) * LANES) for i in range(DN_HPS))


def _dn_in_cols(s, i):
    return slice((s * DN_HPS + i) * LANES, (s * DN_HPS + i + 1) * LANES)


def _dn_taps(cw_ref, s, i):
    return tuple(cw_ref[t:t + 1, _dn_in_cols(s, i)] for t in range(4))


def _dn_pack_gate(vals):
    lane = _iota((1, LANES), 1)
    out = 0.0
    for i, (g, beta) in enumerate(vals):
        out = out + jnp.where(lane == 2 * i, g, 0.0) + jnp.where(lane == 2 * i + 1, beta, 0.0)
    return out


def _dn_in_specs(T):
    one = pl.Buffered(1)
    vec = pl.BlockSpec((1, LANES), lambda b, h: (0, 0))
    return [pl.BlockSpec((T, DN_BLK), lambda b, h: (b, h), pipeline_mode=one),
            pl.BlockSpec((T, LANES), lambda b, h: (b, 0), pipeline_mode=one),
            pl.BlockSpec((4, 3 * DN_HPS * LANES), lambda b, h: (0, h)), vec, vec, vec]


def _dn_fwd_call(proj_dn, proj_ab, conv_w, a_log, dt_bias, gain, B, T):
    npair = T // 128

    def body(x_ref, ab_ref, cw_ref, alog_ref, dtb_ref, gain_ref, out_ref, q_s, k_s, v_s, o_s, gate_s):
        hp = pl.program_id(1)
        gates = []
        for i, cs in enumerate(_DN_COLS):
            for s, (x_s, l2) in enumerate(zip((q_s, k_s, v_s), _DN_L2)):
                x_s[:, cs] = _dn_conv(x_ref[:, _dn_in_cols(s, i)], *_dn_taps(cw_ref, s, i), l2)
            gates.append(_dn_gate(*_dn_gate_in(ab_ref, alog_ref, dtb_ref, DN_HPS * hp + i)))
        gate_s[...] = _dn_pack_gate(gates)

        def step(n, states):
            r = pl.ds(pl.multiple_of(n * 128, 128), 128)
            gate = gate_s[r, :]
            new = []
            for i, cs in enumerate(_DN_COLS):
                o, s2, _ = _dn_pair(q_s[r, cs], k_s[r, cs], v_s[r, cs], _lane_pick(gate, 2 * i), _lane_pick(gate, 2 * i + 1), states[i])
                o_s[r, cs] = o
                new.append(s2)
            return tuple(new)

        lax.fori_loop(0, npair, step, (jnp.zeros((128, 128), F32),) * DN_HPS)
        for i, cs in enumerate(_DN_COLS):
            out_ref[:, cs] = _dn_post(o_s[:, cs], x_ref[:, _dn_in_cols(3, i)], gain_ref[...])

    wide = [pltpu.VMEM((T, DN_HPS * LANES), F32)]
    return pl.pallas_call(
        body, name="dn_fwd", grid=(B, NH // DN_HPS), in_specs=_dn_in_specs(T),
        out_specs=pl.BlockSpec((T, DN_HPS * LANES), lambda b, h: (b, h), pipeline_mode=pl.Buffered(1)),
        out_shape=jax.ShapeDtypeStruct((B * T, D), F32),
        scratch_shapes=wide * 4 + [pltpu.VMEM((T, LANES), F32)],
        compiler_params=_params(dimension_semantics=("arbitrary", "arbitrary")),
    )(proj_dn, proj_ab, conv_w, a_log, dt_bias, gain)


def _dn_bwd_call(proj_dn, proj_ab, dmix, conv_w, a_log, dt_bias, gain, B, T):
    npair = T // 128

    def body(x_ref, ab_ref, cw_ref, alog_ref, dtb_ref, gain_ref, do_ref,
             dx_ref, dab_ref, dcw_ref, dalog_ref, ddtb_ref, dgain_ref,
             q_s, k_s, v_s, o_s, gate_s, dgate_s, st_s, inv_s):
        b_i, hp = pl.program_id(0), pl.program_id(1)
        gates = []
        for i, cs in enumerate(_DN_COLS):
            for s, (x_s, l2) in enumerate(zip((q_s, k_s, v_s), _DN_L2)):
                x_s[:, cs] = _dn_conv(x_ref[:, _dn_in_cols(s, i)], *_dn_taps(cw_ref, s, i), l2)
            gates.append(_dn_gate(*_dn_gate_in(ab_ref, alog_ref, dtb_ref, DN_HPS * hp + i)))
        gate_s[...] = _dn_pack_gate(gates)

        def pair_in(r, i, cs, gate):
            return q_s[r, cs], k_s[r, cs], v_s[r, cs], _lane_pick(gate, 2 * i), _lane_pick(gate, 2 * i + 1)

        def fstep(n, states):
            r = pl.ds(pl.multiple_of(n * 128, 128), 128)
            gate = gate_s[r, :]
            new = []
            for i, cs in enumerate(_DN_COLS):
                st_s[n, i] = states[i]
                o, s2, inv = _dn_pair(*pair_in(r, i, cs, gate), states[i])
                inv_s[n, i] = inv
                o_s[r, cs] = o
                new.append(s2)
            return tuple(new)

        zero_states = (jnp.zeros((128, 128), F32),) * DN_HPS
        lax.fori_loop(0, npair, fstep, zero_states)

        @pl.when(jnp.logical_and(b_i == 0, hp == 0))
        def _():
            dcw_ref[...] = jnp.zeros_like(dcw_ref)
            dalog_ref[...] = jnp.zeros_like(dalog_ref)
            ddtb_ref[...] = jnp.zeros_like(ddtb_ref)
            dgain_ref[...] = jnp.zeros_like(dgain_ref)

        for i, cs in enumerate(_DN_COLS):
            zc = _dn_in_cols(3, i)
            _, post_vjp = jax.vjp(_dn_post, o_s[:, cs], x_ref[:, zc], gain_ref[...])
            do, dz, dgain = post_vjp(do_ref[:, cs])
            dx_ref[:, zc] = dz
            o_s[:, cs] = do
            dgain_ref[...] += dgain

        def bstep(nn, dstates):
            n = npair - 1 - nn
            r = pl.ds(pl.multiple_of(n * 128, 128), 128)
            gate = gate_s[r, :]
            new, dgates = [], []
            for i, cs in enumerate(_DN_COLS):
                inv = inv_s[n, i]
                pair = lambda q, k, v, g, beta, s0, inv=inv: _dn_pair(q, k, v, g, beta, s0, inv)[:2]
                _, pair_vjp = jax.vjp(pair, *pair_in(r, i, cs, gate), st_s[n, i])
                dq, dk, dv, dg, db, ds0 = pair_vjp((o_s[r, cs], dstates[i]))
                dx_ref[r, _dn_in_cols(0, i)], dx_ref[r, _dn_in_cols(1, i)], dx_ref[r, _dn_in_cols(2, i)] = dq, dk, dv
                dgates.append((dg, db))
                new.append(ds0)
            dgate_s[r, :] = _dn_pack_gate(dgates)
            return tuple(new)

        lax.fori_loop(0, npair, bstep, zero_states)

        lane = _iota((1, LANES), 1)
        dab = 0.0
        for i, cs in enumerate(_DN_COLS):
            h = DN_HPS * hp + i
            for s, l2 in enumerate(_DN_L2):
                xc = _dn_in_cols(s, i)
                _, conv_vjp = jax.vjp(functools.partial(_dn_conv, l2_scale=l2), x_ref[:, xc], *_dn_taps(cw_ref, s, i))
                dx, *dw = conv_vjp(dx_ref[:, xc])
                dx_ref[:, xc] = dx
                for t in range(4):
                    dcw_ref[h + 4 * s, t:t + 1, :] += dw[t]
            _, gate_vjp = jax.vjp(_dn_gate, *_dn_gate_in(ab_ref, alog_ref, dtb_ref, h))
            dgate = dgate_s[...]
            da, db, dalog, ddtb = gate_vjp((_lane_pick(dgate, 2 * i), _lane_pick(dgate, 2 * i + 1)))
            dab = dab + jnp.where(lane == h, da, 0.0) + jnp.where(lane == h + NH, db, 0.0)
            dalog_ref[...] += jnp.where(lane == h, dalog, 0.0)
            ddtb_ref[...] += jnp.where(lane == h, ddtb, 0.0)

        @pl.when(hp == 0)
        def _():
            dab_ref[...] = jnp.zeros_like(dab_ref)

        dab_ref[...] += dab

    M = B * T
    one = pl.Buffered(1)
    vec = pl.BlockSpec((1, LANES), lambda b, h: (0, 0))
    wide = [pltpu.VMEM((T, DN_HPS * LANES), F32)]
    narrow = [pltpu.VMEM((T, LANES), F32)]
    vec_shape = jax.ShapeDtypeStruct((1, LANES), F32)
    return pl.pallas_call(
        body, name="dn_bwd", grid=(B, NH // DN_HPS),
        in_specs=_dn_in_specs(T) + [pl.BlockSpec((T, DN_HPS * LANES), lambda b, h: (b, h), pipeline_mode=one)],
        out_specs=[pl.BlockSpec((T, DN_BLK), lambda b, h: (b, h), pipeline_mode=one), pl.BlockSpec((T, LANES), lambda b, h: (b, 0)),
                   pl.BlockSpec((12, 4, LANES), lambda b, h: (0, 0, 0)), vec, vec, vec],
        out_shape=[jax.ShapeDtypeStruct((M, 4 * DN_W), F32), jax.ShapeDtypeStruct((M, LANES), F32),
                   jax.ShapeDtypeStruct((12, 4, LANES), F32), vec_shape, vec_shape, vec_shape],
        scratch_shapes=wide * 4 + narrow * 2 + [pltpu.VMEM((npair, DN_HPS, 128, 128), F32)] * 2,
        compiler_params=_params(dimension_semantics=("arbitrary", "arbitrary")),
    )(proj_dn, proj_ab, conv_w, a_log, dt_bias, gain, dmix)


SBQ = 256


def _group_rms(x, gain):
    first = _iota(x.shape, 1) < 64
    sq = x * x
    ss_a = jnp.sum(jnp.where(first, sq, 0.0), axis=-1, keepdims=True)
    ss_b = jnp.sum(jnp.where(first, 0.0, sq), axis=-1, keepdims=True)
    ms = jnp.where(first, ss_a, ss_b) * (1.0 / 64)
    return x * lax.rsqrt(ms + EPS) * gain


def _sb_stack(q):
    first = _iota((1, LANES), 1) < 64
    return jnp.concatenate([jnp.where(first, q, 0.0), jnp.where(first, 0.0, q)], axis=0)


def _sb_fold(acc):
    return jnp.where(_iota((1, LANES), 1) < 64, acc[:SBQ], acc[SBQ:])


def _sb_logits(q2, k, diag):
    n = SBQ
    row, col = jnp.bitwise_and(_iota((2 * n, n), 0), n - 1), _iota((2 * n, n), 1)
    return _dot_nt(q2, k) * SB_SCALE, jnp.logical_or(col < row, jnp.logical_not(diag))


def _sb_block(q, k, v, carry, diag):
    z, mask = _sb_logits(_sb_stack(q), k, diag)
    soft = jnp.log1p(jnp.exp(-jnp.abs(z)))
    ls_pos = jnp.minimum(z, 0.0) - soft
    l1m = jnp.where(mask, jnp.minimum(-z, 0.0) - soft, 0.0)
    w = jnp.where(mask, jnp.exp(ls_pos + (_suffix_sum(l1m) + carry)), 0.0)
    return _dot(w, v), carry + jnp.sum(l1m, axis=-1, keepdims=True)


def _sb_rowsum(q, k, diag):
    z, mask = _sb_logits(_sb_stack(q), k, diag)
    l1m = jnp.where(mask, jnp.minimum(-z, 0.0) - jnp.log1p(jnp.exp(-jnp.abs(z))), 0.0)
    return jnp.sum(l1m, axis=-1, keepdims=True)


_SB_Q, _SB_K, _SB_V = (slice(i * LANES, (i + 1) * LANES) for i in range(3))


def _sb_fwd_call(proj_sb, mix, q_gain, k_gain, B, T):
    nblk = T // SBQ

    def body(x_ref, qg_ref, kg_ref, mix_ref, out_ref, q_s, k_s):
        del mix_ref
        q_s[...] = _group_rms(x_ref[:, _SB_Q], qg_ref[...])
        k_s[...] = _group_rms(x_ref[:, _SB_K], kg_ref[...])

        def qblock(i, _):
            ri = pl.ds(pl.multiple_of(i * SBQ, SBQ), SBQ)
            q = q_s[ri, :]

            def kblock(jj, c):
                j = i - jj
                rj = pl.ds(pl.multiple_of(j * SBQ, SBQ), SBQ)
                pv, carry = _sb_block(q, k_s[rj, :], x_ref[rj, _SB_V], c[1], j == i)
                return c[0] + pv, carry

            acc, _c = lax.fori_loop(0, i + 1, kblock, (jnp.zeros((2 * SBQ, LANES), F32), jnp.zeros((2 * SBQ, 1), F32)))
            out_ref[ri, :] = _sb_fold(acc)
            return 0

        lax.fori_loop(0, nblk, qblock, 0)

    vec = pl.BlockSpec((1, LANES), lambda b, p: (0, 0))
    return pl.pallas_call(
        body, name="sb_fwd", grid=(B, 2),
        in_specs=[pl.BlockSpec((T, 3 * LANES), lambda b, p: (b, p)), vec, vec, pl.BlockSpec(memory_space=pl.ANY)],
        out_specs=pl.BlockSpec((T, LANES), lambda b, p: (b, DN_W // LANES + p)),
        out_shape=jax.ShapeDtypeStruct((B * T, D), F32), input_output_aliases={3: 0},
        scratch_shapes=[pltpu.VMEM((T, LANES), F32)] * 2,
        compiler_params=_params(dimension_semantics=("arbitrary", "arbitrary")),
    )(proj_sb, q_gain, k_gain, mix)


def _sb_bwd_call(proj_sb, dmix, q_gain, k_gain, B, T):
    nblk = T // SBQ

    def body(x_ref, qg_ref, kg_ref, do_ref, dx_ref, dqg_ref, dkg_ref, q_s, k_s, dq_s, dk_s, dv_s, c_s):
        b_i, p = pl.program_id(0), pl.program_id(1)
        qn, q_vjp = jax.vjp(_group_rms, x_ref[:, _SB_Q], qg_ref[...])
        kn, k_vjp = jax.vjp(_group_rms, x_ref[:, _SB_K], kg_ref[...])
        q_s[...], k_s[...] = qn, kn
        dk_s[...] = jnp.zeros_like(dk_s)
        dv_s[...] = jnp.zeros_like(dv_s)

        def qblock(i, _):
            ri = pl.ds(pl.multiple_of(i * SBQ, SBQ), SBQ)
            q = q_s[ri, :]
            dacc = _sb_stack(do_ref[ri, :])

            def carries(jj, carry):
                j = i - jj
                rj = pl.ds(pl.multiple_of(j * SBQ, SBQ), SBQ)
                c_s[j] = carry
                return carry + _sb_rowsum(q, k_s[rj, :], j == i)

            lax.fori_loop(0, i + 1, carries, jnp.zeros((2 * SBQ, 1), F32))

            def kblock(j, c):
                rj = pl.ds(pl.multiple_of(j * SBQ, SBQ), SBQ)
                f = lambda q_, k_, v_, c_: _sb_block(q_, k_, v_, c_, j == i)
                _, vjp = jax.vjp(f, q, k_s[rj, :], x_ref[rj, _SB_V], c_s[j])
                dq_j, dk_j, dv_j, dc = vjp((dacc, c[1]))
                dk_s[rj, :] += dk_j
                dv_s[rj, :] += dv_j
                return c[0] + dq_j, dc

            dq, _dc = lax.fori_loop(0, i + 1, kblock, (jnp.zeros((SBQ, LANES), F32), jnp.zeros((2 * SBQ, 1), F32)))
            dq_s[ri, :] = dq
            return 0

        lax.fori_loop(0, nblk, qblock, 0)
        dq_in, dqg = q_vjp(dq_s[...])
        dk_in, dkg = k_vjp(dk_s[...])
        dx_ref[:, _SB_Q], dx_ref[:, _SB_K], dx_ref[:, _SB_V] = dq_in, dk_in, dv_s[...]

        @pl.when(jnp.logical_and(b_i == 0, p == 0))
        def _():
            dqg_ref[...] = jnp.zeros_like(dqg_ref)
            dkg_ref[...] = jnp.zeros_like(dkg_ref)

        dqg_ref[...] += dqg + pltpu.roll(dqg, 64, 1)
        dkg_ref[...] += dkg + pltpu.roll(dkg, 64, 1)

    M = B * T
    vec = pl.BlockSpec((1, LANES), lambda b, p: (0, 0))
    blk = pl.BlockSpec((T, 3 * LANES), lambda b, p: (b, p))
    big = [pltpu.VMEM((T, LANES), F32)]
    return pl.pallas_call(
        body, name="sb_bwd", grid=(B, 2),
        in_specs=[blk, vec, vec, pl.BlockSpec((T, LANES), lambda b, p: (b, DN_W // LANES + p))],
        out_specs=[blk, vec, vec],
        out_shape=[jax.ShapeDtypeStruct((M, 3 * SB_W), F32)] + [jax.ShapeDtypeStruct((1, LANES), F32)] * 2,
        scratch_shapes=big * 5 + [pltpu.VMEM((nblk, 2 * SBQ, 1), F32)],
        compiler_params=_params(dimension_semantics=("arbitrary", "arbitrary")),
    )(proj_sb, q_gain, k_gain, dmix)


def _sg_chunk(u, v, gain, w_a, w_b, bias):
    n = 128
    row, col = _iota((n, n), 0), _iota((n, n), 1)
    first = _iota((1, LANES), 1) < 64
    vn = _group_rms(_gelu(v), gain)
    tril = col <= row
    mixed = jnp.where(first, _dot(jnp.where(tril, w_a, 0.0), vn), _dot(jnp.where(tril, w_b, 0.0), vn)) + bias
    return _gelu(u) * mixed


_SG_U, _SG_V = slice(0, LANES), slice(LANES, 2 * LANES)


def _sg_fwd_call(proj_sg, mix, gain, sg_w, bias, B, T):
    nchunk = T // 128

    def body(x_ref, g_ref, wa_ref, wb_ref, bias_ref, mix_ref, out_ref):
        del mix_ref

        def step(i, _):
            r = pl.ds(pl.multiple_of(i * 128, 128), 128)
            out_ref[r, :] = _sg_chunk(x_ref[r, _SG_U], x_ref[r, _SG_V], g_ref[...], wa_ref[0], wb_ref[0], bias_ref[...])
            return 0

        lax.fori_loop(0, nchunk, step, 0)

    return pl.pallas_call(
        body, name="sg_fwd", grid=(B, 2),
        in_specs=[pl.BlockSpec((T, 2 * LANES), lambda b, p: (b, p)), pl.BlockSpec((1, LANES), lambda b, p: (0, p)),
                  pl.BlockSpec((1, 128, 128), lambda b, p: (2 * p, 0, 0)), pl.BlockSpec((1, 128, 128), lambda b, p: (2 * p + 1, 0, 0)),
                  pl.BlockSpec((128, LANES), lambda b, p: (0, p)), pl.BlockSpec(memory_space=pl.ANY)],
        out_specs=pl.BlockSpec((T, LANES), lambda b, p: (b, (DN_W + SB_W) // LANES + p)),
        out_shape=jax.ShapeDtypeStruct((B * T, D), F32), input_output_aliases={5: 0},
        compiler_params=_params(dimension_semantics=("arbitrary", "arbitrary")),
    )(proj_sg, gain, sg_w, sg_w, bias, mix)


def _sg_bwd_call(proj_sg, dmix, gain, sg_w, bias, B, T):
    nchunk = T // 128

    def body(x_ref, g_ref, wa_ref, wb_ref, bias_ref, do_ref, dx_ref, dg_ref, dw_ref, db_ref):
        p, b_i = pl.program_id(0), pl.program_id(1)

        def step(i, c):
            r = pl.ds(pl.multiple_of(i * 128, 128), 128)
            _, vjp = jax.vjp(_sg_chunk, x_ref[r, _SG_U], x_ref[r, _SG_V], g_ref[...], wa_ref[0], wb_ref[0], bias_ref[...])
            du, dv, dg, dwa, dwb, dbias = vjp(do_ref[r, :])
            dx_ref[r, _SG_U], dx_ref[r, _SG_V] = du, dv
            return c[0] + dg, c[1] + dwa, c[2] + dwb, c[3] + dbias

        z = jnp.zeros((128, 128), F32)
        dg, dwa, dwb, dbias = lax.fori_loop(0, nchunk, step, (jnp.zeros((1, LANES), F32), z, z, z))
        lane = _iota((1, LANES), 1)
        first = lane < 64
        s_a = jnp.sum(jnp.where(first, dbias, 0.0), axis=-1, keepdims=True)
        s_b = jnp.sum(jnp.where(first, 0.0, dbias), axis=-1, keepdims=True)
        dbg = jnp.where(lane == 2 * p, s_a, 0.0) + jnp.where(lane == 2 * p + 1, s_b, 0.0)

        @pl.when(b_i == 0)
        def _():
            dg_ref[...] = jnp.zeros_like(dg_ref)
            dw_ref[...] = jnp.zeros_like(dw_ref)

        @pl.when(jnp.logical_and(b_i == 0, p == 0))
        def _():
            db_ref[...] = jnp.zeros_like(db_ref)

        dg_ref[...] += dg
        dw_ref[0] += dwa
        dw_ref[1] += dwb
        db_ref[...] += dbg

    M = B * T
    blk = pl.BlockSpec((T, 2 * LANES), lambda p, b: (b, p))
    return pl.pallas_call(
        body, name="sg_bwd", grid=(2, B),
        in_specs=[blk, pl.BlockSpec((1, LANES), lambda p, b: (0, p)),
                  pl.BlockSpec((1, 128, 128), lambda p, b: (2 * p, 0, 0)), pl.BlockSpec((1, 128, 128), lambda p, b: (2 * p + 1, 0, 0)),
                  pl.BlockSpec((128, LANES), lambda p, b: (0, p)),
                  pl.BlockSpec((T, LANES), lambda p, b: (b, (DN_W + SB_W) // LANES + p))],
        out_specs=[blk, pl.BlockSpec((1, LANES), lambda p, b: (0, p)), pl.BlockSpec((2, 128, 128), lambda p, b: (p, 0, 0)),
                   pl.BlockSpec((128, LANES), lambda p, b: (0, 0))],
        out_shape=[jax.ShapeDtypeStruct((M, 2 * SG_W), F32), jax.ShapeDtypeStruct((1, SG_W), F32),
                   jax.ShapeDtypeStruct((4, 128, 128), F32), jax.ShapeDtypeStruct((128, LANES), F32)],
        compiler_params=_params(dimension_semantics=("arbitrary", "arbitrary")),
    )(proj_sg, gain, sg_w, sg_w, bias, dmix)


def _row_tile(m):
    return min(m, 512)


def _in_proj_call(x, gain, wt):
    m = x.shape[0]
    tm = _row_tile(m)

    def body(x_ref, g_ref, wt_ref, *out_refs):
        h = _rms(x_ref[...], g_ref[...]).astype(BF16)
        out_refs[-1][...] = h
        for (off, width), out_ref in zip(SECTIONS, out_refs):
            out_ref[...] = lax.dot_general(h, wt_ref[off:off + width, :], (((1,), (1,)), ((), ())), preferred_element_type=F32)

    rows = lambda width: pl.BlockSpec((tm, width), lambda i: (i, 0))
    return pl.pallas_call(
        body, name="in_proj", grid=(m // tm,),
        in_specs=[rows(D), pl.BlockSpec((1, D), lambda i: (0, 0)),
                  pl.BlockSpec((NPACK, D), lambda i: (0, 0), pipeline_mode=pl.Buffered(1))],
        out_specs=[rows(w) for _, w in SECTIONS] + [rows(D)],
        out_shape=[jax.ShapeDtypeStruct((m, w), F32) for _, w in SECTIONS] + [jax.ShapeDtypeStruct((m, D), BF16)],
        compiler_params=_params(dimension_semantics=("arbitrary",)),
    )(x, gain, wt)


def _in_proj_bwd_call(dsections, wt, x, gain, dres):
    m = x.shape[0]
    tm = _row_tile(m)

    def body(*refs):
        ds_refs, (wt_ref, x_ref, g_ref, dres_ref, dx_ref, dg_ref) = refs[:len(SECTIONS)], refs[len(SECTIONS):]

        @pl.when(pl.program_id(0) == 0)
        def _():
            dg_ref[...] = jnp.zeros_like(dg_ref)

        dh = 0.0
        for (off, width), ds_ref in zip(SECTIONS, ds_refs):
            dh = dh + jnp.dot(ds_ref[...].astype(BF16), wt_ref[off:off + width, :], preferred_element_type=F32)
        _, vjp = jax.vjp(_rms, x_ref[...], g_ref[...])
        dx, dg = vjp(dh)
        dx_ref[...] = dres_ref[...] + dx
        dg_ref[...] += dg

    rows = lambda width: pl.BlockSpec((tm, width), lambda i: (i, 0))
    return pl.pallas_call(
        body, name="in_proj_bwd", grid=(m // tm,),
        in_specs=[rows(w) for _, w in SECTIONS] + [pl.BlockSpec((NPACK, D), lambda i: (0, 0), pipeline_mode=pl.Buffered(1)),
                                                   rows(D), pl.BlockSpec((1, D), lambda i: (0, 0)), rows(D)],
        out_specs=[rows(D), pl.BlockSpec((1, D), lambda i: (0, 0))],
        out_shape=[jax.ShapeDtypeStruct((m, D), F32), jax.ShapeDtypeStruct((1, D), F32)],
        compiler_params=_params(dimension_semantics=("arbitrary",)),
    )(*dsections, wt, x, gain, dres)


def _in_proj_grad_call(dsections, h):
    m = h.shape[0]
    tm = min(m, 256)

    def body(*refs):
        ds_refs, (h_ref, out_ref) = refs[:len(SECTIONS)], refs[len(SECTIONS):]

        @pl.when(pl.program_id(0) == 0)
        def _():
            out_ref[...] = jnp.zeros_like(out_ref)

        for (off, width), ds_ref in zip(SECTIONS, ds_refs):
            out_ref[off:off + width, :] += lax.dot_general(ds_ref[...].astype(BF16), h_ref[...], (((0,), (0,)), ((), ())),
                                                           preferred_element_type=F32)

    rows = lambda width: pl.BlockSpec((tm, width), lambda i: (i, 0))
    return pl.pallas_call(
        body, name="grad_w_in", grid=(m // tm,),
        in_specs=[rows(w) for _, w in SECTIONS] + [rows(D)],
        out_specs=pl.BlockSpec((NPACK, D), lambda i: (0, 0), pipeline_mode=pl.Buffered(1)),
        out_shape=jax.ShapeDtypeStruct((NPACK, D), F32),
        compiler_params=_params(dimension_semantics=("arbitrary",)),
    )(*dsections, h)


def _packed_column_of():
    t = np.full(NPACK, -1, np.int64)
    lanes = np.arange(LANES)
    for pair in range(2):
        for s in range(4):
            t[DN_OFF + pair * 1024 + s * 256 + np.arange(256)] = s * DN_W + pair * 256 + np.arange(256)
        for s in range(3):
            t[SB_OFF + pair * 384 + s * LANES + lanes] = 2056 + s * SB_W + pair * LANES + lanes
        for s in range(2):
            t[SG_OFF + pair * 256 + s * LANES + lanes] = 2056 + 3 * SB_W + s * SG_W + pair * LANES + lanes
    t[AB_OFF + np.arange(2 * NH)] = 4 * DN_W + np.arange(2 * NH)
    return t


def _row_tables():
    col = _packed_column_of()
    fwd = np.where(col >= 0, (col // IN_SHARD) * IN_SHARD_PAD + col % IN_SHARD, -1)
    packed_of = np.full(IN_DIM, -1, np.int64)
    packed_of[col[col >= 0]] = np.nonzero(col >= 0)[0]
    r = np.arange(NDEV * IN_SHARD_PAD)
    inside = r % IN_SHARD_PAD < IN_SHARD
    back = np.where(inside, packed_of[np.minimum((r // IN_SHARD_PAD) * IN_SHARD + r % IN_SHARD_PAD, IN_DIM - 1)], -1)
    return fwd, back


def _row_perm_call(src, table, name):
    n_out = table.shape[0]
    touched = [sorted(set((table[b * 128:(b + 1) * 128][table[b * 128:(b + 1) * 128] >= 0] // 128).tolist()))
               for b in range(n_out // 128)]

    def body(tbl_ref, src_ref, out_ref):
        lane = _iota((1, LANES), 1)
        for b, blocks in enumerate(touched):
            want = tbl_ref[b * 128:(b + 1) * 128, :]
            acc = jnp.zeros((128, D), F32)
            for sb in blocks:
                pick = jnp.where(want == sb * 128 + lane, 1.0, 0.0).astype(BF16)
                acc = acc + jnp.dot(pick, src_ref[sb * 128:(sb + 1) * 128, :].astype(BF16), preferred_element_type=F32)
            out_ref[b * 128:(b + 1) * 128, :] = acc.astype(BF16)

    return pl.pallas_call(
        body, name=name, out_shape=jax.ShapeDtypeStruct((n_out, D), BF16),
        in_specs=[pl.BlockSpec(memory_space=pltpu.VMEM)] * 2, out_specs=pl.BlockSpec(memory_space=pltpu.VMEM),
        compiler_params=_params(),
    )(jnp.asarray(table.reshape(-1, 1), jnp.int32), src)


def _out_proj_call(a, w, res):
    m, k = a.shape
    n = w.shape[1]
    tm = _row_tile(m)

    def body(a_ref, w_ref, res_ref, out_ref):
        out_ref[...] = res_ref[...] + jnp.dot(a_ref[...].astype(BF16), w_ref[...], preferred_element_type=F32)

    return pl.pallas_call(
        body, name="out_proj", grid=(m // tm,),
        in_specs=[pl.BlockSpec((tm, k), lambda i: (i, 0)), pl.BlockSpec((k, n), lambda i: (0, 0)),
                  pl.BlockSpec((tm, n), lambda i: (i, 0))],
        out_specs=pl.BlockSpec((tm, n), lambda i: (i, 0)),
        out_shape=jax.ShapeDtypeStruct((m, n), F32),
        compiler_params=_params(dimension_semantics=("arbitrary",)),
    )(a, w, res)


def _ffn_specs(tm):
    return [pl.BlockSpec((1, D, FF_SHARD), lambda i, j: (j, 0, 0)), pl.BlockSpec((FF_SHARD, D), lambda i, j: (j, 0))]


def _ffn_fwd_call(x, gain, w1, w2):
    m = x.shape[0]
    tm = _row_tile(m)

    def body(x_ref, g_ref, w1_ref, w2_ref, out_ref, h_s, acc_s):
        j = pl.program_id(1)

        @pl.when(j == 0)
        def _():
            h_s[...] = _rms(x_ref[...], g_ref[...]).astype(BF16)
            acc_s[...] = jnp.zeros_like(acc_s)

        a = jnp.maximum(jnp.dot(h_s[...], w1_ref[0], preferred_element_type=F32), 0.0)
        acc_s[...] += jnp.dot((a * a).astype(BF16), w2_ref[...], preferred_element_type=F32)

        @pl.when(j == NDEV - 1)
        def _():
            out_ref[...] = x_ref[...] + acc_s[...]

    return pl.pallas_call(
        body, name="ffn_fwd", grid=(m // tm, NDEV),
        in_specs=[pl.BlockSpec((tm, D), lambda i, j: (i, 0)), pl.BlockSpec((1, D), lambda i, j: (0, 0))] + _ffn_specs(tm),
        out_specs=pl.BlockSpec((tm, D), lambda i, j: (i, 0)),
        out_shape=jax.ShapeDtypeStruct((m, D), F32),
        scratch_shapes=[pltpu.VMEM((tm, D), BF16), pltpu.VMEM((tm, D), F32)],
        compiler_params=_params(dimension_semantics=("arbitrary", "arbitrary")),
    )(x, gain, w1, w2)


def _ffn_bwd_call(x, dy, gain, w1, w2):
    m = x.shape[0]
    tm = _row_tile(m)

    def body(x_ref, dy_ref, g_ref, w1_ref, w2_ref, dx_ref, da_ref, r_ref, h_ref, dg_ref, acc_s):
        i, j = pl.program_id(0), pl.program_id(1)

        @pl.when(j == 0)
        def _():
            h_ref[...] = _rms(x_ref[...], g_ref[...]).astype(BF16)
            acc_s[...] = jnp.zeros_like(acc_s)

        @pl.when(jnp.logical_and(i == 0, j == 0))
        def _():
            dg_ref[...] = jnp.zeros_like(dg_ref)

        a = jnp.maximum(jnp.dot(h_ref[...], w1_ref[0], preferred_element_type=F32), 0.0)
        r_ref[...] = (a * a).astype(BF16)
        dr = lax.dot_general(dy_ref[...].astype(BF16), w2_ref[...], (((1,), (1,)), ((), ())), preferred_element_type=F32)
        da = (dr * (2.0 * a)).astype(BF16)
        da_ref[...] = da
        acc_s[...] += lax.dot_general(da, w1_ref[0], (((1,), (1,)), ((), ())), preferred_element_type=F32)

        @pl.when(j == NDEV - 1)
        def _():
            _, vjp = jax.vjp(_rms, x_ref[...], g_ref[...])
            dx, dg = vjp(acc_s[...])
            dx_ref[...] = dy_ref[...] + dx
            dg_ref[...] += dg

    return pl.pallas_call(
        body, name="ffn_bwd", grid=(m // tm, NDEV),
        in_specs=[pl.BlockSpec((tm, D), lambda i, j: (i, 0)), pl.BlockSpec((tm, D), lambda i, j: (i, 0)),
                  pl.BlockSpec((1, D), lambda i, j: (0, 0))] + _ffn_specs(tm),
        out_specs=[pl.BlockSpec((tm, D), lambda i, j: (i, 0)), pl.BlockSpec((tm, FF_SHARD), lambda i, j: (i, j)),
                   pl.BlockSpec((tm, FF_SHARD), lambda i, j: (i, j)), pl.BlockSpec((tm, D), lambda i, j: (i, 0)),
                   pl.BlockSpec((1, D), lambda i, j: (0, 0))],
        out_shape=[jax.ShapeDtypeStruct((m, D), F32), jax.ShapeDtypeStruct((m, DFF), BF16), jax.ShapeDtypeStruct((m, DFF), BF16),
                   jax.ShapeDtypeStruct((m, D), BF16), jax.ShapeDtypeStruct((1, D), F32)],
        scratch_shapes=[pltpu.VMEM((tm, D), F32)],
        compiler_params=_params(dimension_semantics=("arbitrary", "arbitrary")),
    )(x, dy, gain, w1, w2)


def _mm_nt_call(a, b, name):
    m, k = a.shape
    n = b.shape[0]
    tm = _row_tile(m)

    def body(a_ref, b_ref, out_ref):
        out_ref[...] = lax.dot_general(a_ref[...].astype(BF16), b_ref[...].astype(BF16), (((1,), (1,)), ((), ())),
                                       preferred_element_type=F32)

    return pl.pallas_call(
        body, name=name, grid=(m // tm,),
        in_specs=[pl.BlockSpec((tm, k), lambda i: (i, 0)), pl.BlockSpec((n, k), lambda i: (0, 0))],
        out_specs=pl.BlockSpec((tm, n), lambda i: (i, 0)),
        out_shape=jax.ShapeDtypeStruct((m, n), F32),
        compiler_params=_params(dimension_semantics=("arbitrary",)),
    )(a, b)


def _mm_tn_call(a, b, name, col_shards=False):
    m, k = a.shape
    n = b.shape[1]
    tm, tk = _row_tile(m), min(k, 1024)
    tn = n // NDEV if col_shards else min(n, 1024)

    def body(a_ref, b_ref, out_ref, acc_s):
        s = pl.program_id(2)

        @pl.when(s == 0)
        def _():
            acc_s[...] = jnp.zeros_like(acc_s)

        acc_s[...] += lax.dot_general(a_ref[...].astype(BF16), b_ref[...].astype(BF16), (((0,), (0,)), ((), ())),
                                      preferred_element_type=F32)

        @pl.when(s == m // tm - 1)
        def _():
            out_ref[...] = acc_s[...].astype(BF16).reshape(out_ref.shape)

    if col_shards:
        out_spec, out_shape = pl.BlockSpec((1, tk, tn), lambda i, j, s: (j, i, 0)), (NDEV, k, tn)
    else:
        out_spec, out_shape = pl.BlockSpec((tk, tn), lambda i, j, s: (i, j)), (k, n)
    return pl.pallas_call(
        body, name=name, grid=(k // tk, n // tn, m // tm),
        in_specs=[pl.BlockSpec((tm, tk), lambda i, j, s: (s, i)), pl.BlockSpec((tm, tn), lambda i, j, s: (s, j))],
        out_specs=out_spec, out_shape=jax.ShapeDtypeStruct(out_shape, BF16),
        scratch_shapes=[pltpu.VMEM((tk, tn), F32)],
        compiler_params=_params(dimension_semantics=("arbitrary", "arbitrary", "arbitrary")),
    )(a, b)


def _loss_call(y, target):
    m = y.shape[0]
    tm = _row_tile(m)

    def body(y_ref, t_ref, loss_ref, dy_ref):
        @pl.when(pl.program_id(0) == 0)
        def _():
            loss_ref[...] = jnp.zeros_like(loss_ref)

        err = y_ref[...] - t_ref[...]
        dy_ref[...] = err * (1.0 / D)
        per_row = jnp.mean(err * err, axis=-1, keepdims=True)
        loss_ref[...] += jnp.broadcast_to(0.5 * jnp.sum(per_row, axis=0, keepdims=True), (1, LANES))

    return pl.pallas_call(
        body, name="loss", grid=(m // tm,),
        in_specs=[pl.BlockSpec((tm, D), lambda i: (i, 0))] * 2,
        out_specs=[pl.BlockSpec((1, LANES), lambda i: (0, 0)), pl.BlockSpec((tm, D), lambda i: (i, 0))],
        out_shape=[jax.ShapeDtypeStruct((1, LANES), F32), jax.ShapeDtypeStruct((m, D), F32)],
        compiler_params=_params(dimension_semantics=("arbitrary",)),
    )(y, target)


def _adamw_call(w, g, m, v, name):
    shape = w.shape
    cols = shape[-1] if w.ndim > 1 else w.size
    rows = w.size // cols
    tr = rows if (rows <= 512 or rows % 512) else 512
    c1, c2 = 1.0 - ADAM_B1 ** ADAM_STEP, 1.0 - ADAM_B2 ** ADAM_STEP

    def body(w_ref, g_ref, m_ref, v_ref, d_ref, nm_ref, nv_ref):
        g_ = g_ref[...]
        nm = ADAM_B1 * m_ref[...] + (1.0 - ADAM_B1) * g_
        nv = ADAM_B2 * v_ref[...] + (1.0 - ADAM_B2) * (g_ * g_)
        d_ref[...] = -ADAM_LR * ((nm / c1) / (jnp.sqrt(nv / c2) + ADAM_EPS) + ADAM_WD * w_ref[...])
        nm_ref[...], nv_ref[...] = nm, nv

    spec = pl.BlockSpec((tr, cols), lambda i: (i, 0))
    outs = pl.pallas_call(
        body, name=name, grid=(rows // tr,), in_specs=[spec] * 4, out_specs=[spec] * 3,
        out_shape=[jax.ShapeDtypeStruct((rows, cols), F32)] * 3,
        compiler_params=_params(dimension_semantics=("arbitrary",)),
    )(*(t.reshape(rows, cols) for t in (w, g, m, v)))
    return tuple(o.reshape(shape) for o in outs)


def _sum_tile(rows):
    for cand in (2048, 1024, 512, 256, 128):
        if rows > cand and rows % cand == 0:
            return cand
    return rows


def _pair_sum_call(g, got, core, name):
    rows, cols = g.shape[1:]
    tr = _sum_tile(rows)

    def body(core_ref, g_ref, got_ref, out_ref):
        del core_ref
        out_ref[...] = (g_ref[...].astype(F32) + got_ref[...].astype(F32)).astype(BF16)

    grid_spec = pltpu.PrefetchScalarGridSpec(
        num_scalar_prefetch=1, grid=(4, rows // tr),
        in_specs=[pl.BlockSpec((1, tr, cols), lambda ch, t, core_ref: (2 * ch + core_ref[0], t, 0)),
                  pl.BlockSpec((1, tr, cols), lambda ch, t, core_ref: (ch, t, 0))],
        out_specs=pl.BlockSpec((1, tr, cols), lambda ch, t, core_ref: (ch, t, 0)))
    return pl.pallas_call(
        body, name=name, grid_spec=grid_spec, out_shape=jax.ShapeDtypeStruct((4, rows, cols), BF16),
        compiler_params=_params(dimension_semantics=("arbitrary", "arbitrary")),
    )(jnp.asarray(core, jnp.int32).reshape(1), g, got)


def _sum_call(parts, out_dtype, name):
    rows, cols = parts[0][0].shape[1:]
    tr = _sum_tile(rows)
    index = jnp.stack([jnp.asarray(i, jnp.int32) for _, i in parts])

    def body(idx_ref, *refs):
        del idx_ref
        acc = refs[0][0].astype(F32)
        for r in refs[1:-1]:
            acc = acc + r[0].astype(F32)
        refs[-1][...] = acc.astype(out_dtype)

    grid_spec = pltpu.PrefetchScalarGridSpec(
        num_scalar_prefetch=1, grid=(rows // tr,),
        in_specs=[pl.BlockSpec((1, tr, cols), lambda t, idx, n=n: (idx[n], t, 0)) for n in range(len(parts))],
        out_specs=pl.BlockSpec((tr, cols), lambda t, idx: (t, 0)))
    return pl.pallas_call(
        body, name=name, grid_spec=grid_spec, out_shape=jax.ShapeDtypeStruct((rows, cols), out_dtype),
        compiler_params=_params(dimension_semantics=("arbitrary",)),
    )(index, *(a for a, _ in parts))


def _place():
    return lax.axis_index("x"), lax.axis_index("y"), lax.axis_index("c")


def _any_specs(n):
    return [pl.BlockSpec(memory_space=pl.ANY)] * n


def _all_gather_call(xs, name):
    n = len(xs)

    def body(*refs):
        x_refs, out_refs, (send_sems, recv_sems, local_sems) = refs[:n], refs[n:2 * n], refs[2 * n:]
        ax, ay, ac = _place()
        me, sibling = (ax, ay, ac), (ax, ay, 1 - ac)
        chips = [(1 - ax, ay), (ax, 1 - ay), (1 - ax, 1 - ay)]

        def copy(a, k, block, to, src=None):
            slot = out_refs[a].at[4 * block[0] + 2 * block[1] + block[2]]
            return pltpu.make_async_remote_copy(
                src_ref=slot if src is None else src, dst_ref=slot,
                send_sem=send_sems.at[7 * a + k], recv_sem=recv_sems.at[7 * a + k], device_id=to, device_id_type=MESH)

        local = [pltpu.make_async_copy(x_refs[a], out_refs[a].at[4 * ax + 2 * ay + ac], local_sems.at[a]) for a in range(n)]
        for cp in local:
            cp.start()
        first = []
        for a in range(n):
            first.append(copy(a, 0, me, sibling, src=x_refs[a]))
            first += [copy(a, 1 + j, me, (*chip, ac), src=x_refs[a]) for j, chip in enumerate(chips)]
        for cp in first:
            cp.start()
        passed = []
        for j, chip in enumerate(chips):
            for a in range(n):
                copy(a, 1 + j, (*chip, ac), me).wait_recv()
                passed.append(copy(a, 4 + j, (*chip, ac), sibling))
                passed[-1].start()
        for a in range(n):
            copy(a, 0, sibling, me).wait_recv()
            for j, chip in enumerate(chips):
                copy(a, 4 + j, (*chip, 1 - ac), me).wait_recv()
        for cp in first + passed:
            cp.wait_send()
        for cp in local:
            cp.wait()

    return pl.pallas_call(
        body, name=name, in_specs=_any_specs(n), out_specs=_any_specs(n),
        out_shape=[jax.ShapeDtypeStruct((NDEV,) + x.shape, x.dtype) for x in xs],
        scratch_shapes=[pltpu.SemaphoreType.DMA((7 * n,)), pltpu.SemaphoreType.DMA((7 * n,)), pltpu.SemaphoreType.DMA((n,))],
    )(*xs)


def _swap_sibling_call(xs, name):
    n = len(xs)

    def body(*refs):
        x_refs, out_refs, (send_sems, recv_sems) = refs[:n], refs[n:2 * n], refs[2 * n:]
        ax, ay, ac = _place()
        for a in range(n):
            for chip in range(4):
                pltpu.make_async_remote_copy(src_ref=x_refs[a].at[2 * chip + 1 - ac], dst_ref=out_refs[a].at[chip],
                                             send_sem=send_sems.at[a], recv_sem=recv_sems.at[a],
                                             device_id=(ax, ay, 1 - ac), device_id_type=MESH).start()
        for a in range(n):
            four = x_refs[a].at[pl.ds(0, 4)]
            pltpu.make_async_remote_copy(src_ref=four, dst_ref=out_refs[a], send_sem=send_sems.at[a], recv_sem=recv_sems.at[a],
                                         device_id=(ax, ay, 1 - ac), device_id_type=MESH).wait()

    return pl.pallas_call(
        body, name=name, in_specs=_any_specs(n), out_specs=_any_specs(n),
        out_shape=[jax.ShapeDtypeStruct((4,) + x.shape[1:], x.dtype) for x in xs],
        scratch_shapes=[pltpu.SemaphoreType.DMA((n,)), pltpu.SemaphoreType.DMA((n,))],
    )(*xs)


def _swap_chips_call(xs, name):
    n = len(xs)

    def body(*refs):
        x_refs, out_refs, (send_sems, recv_sems) = refs[:n], refs[n:2 * n], refs[2 * n:]
        ax, ay, ac = _place()
        chips = [(1 - ax, ay), (ax, 1 - ay), (1 - ax, 1 - ay)]
        copies = [pltpu.make_async_remote_copy(src_ref=x_refs[a].at[2 * cx + cy], dst_ref=out_refs[a].at[j],
                                               send_sem=send_sems.at[3 * a + j], recv_sem=recv_sems.at[3 * a + j],
                                               device_id=(cx, cy, ac), device_id_type=MESH)
                  for a in range(n) for j, (cx, cy) in enumerate(chips)]
        for cp in copies:
            cp.start()
        for cp in copies:
            cp.wait()

    return pl.pallas_call(
        body, name=name, in_specs=_any_specs(n), out_specs=_any_specs(n),
        out_shape=[jax.ShapeDtypeStruct((3,) + x.shape[1:], x.dtype) for x in xs],
        scratch_shapes=[pltpu.SemaphoreType.DMA((3 * n,)), pltpu.SemaphoreType.DMA((3 * n,))],
    )(*xs)


def _reduce_scatter(gs, name):
    ax, ay, ac = _place()
    me, my_chip = 4 * ax + 2 * ay + ac, 2 * ax + ay
    got = _swap_sibling_call(gs, name + "_d2d")
    chip_sums = [_pair_sum_call(g, t, ac, f"{name}_pair{a}") for a, (g, t) in enumerate(zip(gs, got))]
    from_chips = _swap_chips_call(chip_sums, name + "_ici")
    return [_sum_call([(g, me), (t, my_chip), (f, 0), (f, 1), (f, 2)], F32, f"{name}_total{a}")
            for a, (g, t, f) in enumerate(zip(gs, got, from_chips))]


SMALL = ("norm1_g", "conv_w", "a_log", "dt_bias", "dn_out_g", "sb_q_g", "sb_k_g", "sg_v_g", "sg_w", "sg_b", "norm2_g")
WEIGHTS = ("norm1_g", "w_in", "conv_w", "a_log", "dt_bias", "dn_out_g", "sb_q_g", "sb_k_g", "sg_v_g", "sg_w", "sg_b",
           "w_out", "norm2_g", "w_ff1", "w_ff2")
SMALL_SHAPE = {"norm1_g": (D,), "conv_w": (4, 3 * DN_W), "a_log": (NH,), "dt_bias": (NH,), "dn_out_g": (128,), "sb_q_g": (64,),
               "sb_k_g": (64,), "sg_v_g": (SG_W,), "sg_w": (NH, 128, 128), "sg_b": (NH, 128), "norm2_g": (D,)}


def _size(shape):
    n = 1
    for s in shape:
        n *= s
    return n


def _to_rows(flat, multiple):
    pad = (-flat.shape[0]) % (LANES * multiple)
    return jnp.pad(flat, (0, pad)).reshape(-1, LANES)


def _conv_by_pair(conv):
    return conv.reshape(4, 3, 2, 256).transpose(0, 2, 1, 3).reshape(4, 3 * DN_W)


def kernel(x, norm1_g, w_in, conv_w, a_log, dt_bias, dn_out_g, sb_q_g, sb_k_g, sg_v_g, sg_w, sg_b, w_out, norm2_g, w_ff1, w_ff2, loss_target, m_norm1_g, m_w_in, m_conv_w, m_a_log, m_dt_bias, m_dn_out_g, m_sb_q_g, m_sb_k_g, m_sg_v_g, m_sg_w, m_sg_b, m_w_out, m_norm2_g, m_w_ff1, m_w_ff2, v_norm1_g, v_w_in, v_conv_w, v_a_log, v_dt_bias, v_dn_out_g, v_sb_q_g, v_sb_k_g, v_sg_v_g, v_sg_w, v_sg_b, v_w_out, v_norm2_g, v_w_ff1, v_w_ff2):
    given = dict(norm1_g=norm1_g, w_in=w_in, conv_w=conv_w, a_log=a_log, dt_bias=dt_bias, dn_out_g=dn_out_g, sb_q_g=sb_q_g,
                 sb_k_g=sb_k_g, sg_v_g=sg_v_g, sg_w=sg_w, sg_b=sg_b, w_out=w_out, norm2_g=norm2_g, w_ff1=w_ff1, w_ff2=w_ff2)
    mom = dict(norm1_g=m_norm1_g, w_in=m_w_in, conv_w=m_conv_w, a_log=m_a_log, dt_bias=m_dt_bias, dn_out_g=m_dn_out_g,
               sb_q_g=m_sb_q_g, sb_k_g=m_sb_k_g, sg_v_g=m_sg_v_g, sg_w=m_sg_w, sg_b=m_sg_b, w_out=m_w_out, norm2_g=m_norm2_g,
               w_ff1=m_w_ff1, w_ff2=m_w_ff2)
    var = dict(norm1_g=v_norm1_g, w_in=v_w_in, conv_w=v_conv_w, a_log=v_a_log, dt_bias=v_dt_bias, dn_out_g=v_dn_out_g,
               sb_q_g=v_sb_q_g, sb_k_g=v_sb_k_g, sg_v_g=v_sg_v_g, sg_w=v_sg_w, sg_b=v_sg_b, w_out=v_w_out, norm2_g=v_norm2_g,
               w_ff1=v_w_ff1, w_ff2=v_w_ff2)
    B, T, _ = x.shape
    M = B * T
    ax, ay, ac = _place()
    me = 4 * ax + 2 * ay + ac
    table_fwd, table_back = _row_tables()

    send = []
    for l in range(2):
        w_in_t = jnp.pad(w_in[l].T, ((0, IN_SHARD_PAD - IN_SHARD), (0, 0)))
        send += [w_in_t.astype(BF16), w_out[l].astype(BF16), w_ff1[l].astype(BF16), w_ff2[l].astype(BF16)]
    send.append(_to_rows(conv_w.reshape(-1), 8))
    gathered = _all_gather_call(send, "gather_weights")
    conv_full = gathered[8].reshape(NDEV, -1)[:, :conv_w.size].reshape(NDEV, 2, 4, -1).transpose(1, 2, 0, 3).reshape(2, 4, 3 * DN_W)

    pad_vec = lambda v: jnp.zeros((1, LANES), F32).at[0, :v.shape[0]].set(v)
    layer = []
    for l in range(2):
        g_in, g_out, g_ff1, g_ff2 = gathered[4 * l:4 * l + 4]
        layer.append(dict(
            g1=norm1_g[l].reshape(1, D), g2=norm2_g[l].reshape(1, D), conv=_conv_by_pair(conv_full[l]),
            a_log=pad_vec(a_log[l]), dt_bias=pad_vec(dt_bias[l]), dn_g=dn_out_g[l].reshape(1, LANES),
            sb_qg=jnp.tile(sb_q_g[l], 2).reshape(1, LANES), sb_kg=jnp.tile(sb_k_g[l], 2).reshape(1, LANES),
            sg_g=sg_v_g[l].reshape(1, SG_W), sg_w=sg_w[l], sg_bias=jnp.repeat(sg_b[l].T, 64, axis=1),
            wt=_row_perm_call(g_in.reshape(NDEV * IN_SHARD_PAD, D), table_fwd, "pack_w_in"),
            w_out=g_out.reshape(D, D), w1=g_ff1, w2=g_ff2.reshape(DFF, D)))

    cur = x.reshape(M, D)
    saved = []
    for p in layer:
        p_dn, p_sb, p_sg, p_ab, h = _in_proj_call(cur, p["g1"], p["wt"])
        mix = _dn_fwd_call(p_dn, p_ab, p["conv"], p["a_log"], p["dt_bias"], p["dn_g"], B, T)
        mix = _sb_fwd_call(p_sb, mix, p["sb_qg"], p["sb_kg"], B, T)
        mix = _sg_fwd_call(p_sg, mix, p["sg_g"], p["sg_w"], p["sg_bias"], B, T)
        x1 = _out_proj_call(mix, p["w_out"], cur)
        x2 = _ffn_fwd_call(x1, p["g2"], p["w1"], p["w2"])
        saved.append(dict(x0=cur, p_dn=p_dn, p_sb=p_sb, p_sg=p_sg, p_ab=p_ab, h=h, mix=mix, x1=x1))
        cur = x2
    loss_part, dy = _loss_call(cur, loss_target.reshape(M, D))
    loss = lax.psum(loss_part[0, 0], ("x", "y", "c"))

    big_grads = [None] * 8
    small_grads = {n: [None, None] for n in SMALL}
    for l in (1, 0):
        p, s = layer[l], saved[l]
        dx1, da, r, h2, dg2 = _ffn_bwd_call(s["x1"], dy, p["g2"], p["w1"], p["w2"])
        big_grads[4 * l + 2] = _mm_tn_call(h2, da, "grad_w_ff1", col_shards=True)
        big_grads[4 * l + 3] = _mm_tn_call(r, dy, "grad_w_ff2").reshape(NDEV, FF_SHARD, D)
        dmix = _mm_nt_call(dx1, p["w_out"], "dmix")
        big_grads[4 * l + 1] = _mm_tn_call(s["mix"], dx1, "grad_w_out").reshape(NDEV, D // NDEV, D)
        d_dn, d_ab, dcw, dalog, ddtb, ddn_g = _dn_bwd_call(s["p_dn"], s["p_ab"], dmix, p["conv"], p["a_log"], p["dt_bias"], p["dn_g"], B, T)
        d_sb, dqg, dkg = _sb_bwd_call(s["p_sb"], dmix, p["sb_qg"], p["sb_kg"], B, T)
        d_sg, dsg_g, dsg_w, dsg_b = _sg_bwd_call(s["p_sg"], dmix, p["sg_g"], p["sg_w"], p["sg_bias"], B, T)
        dsections = (d_dn, d_sb, d_sg, d_ab)
        dy, dg1 = _in_proj_bwd_call(dsections, p["wt"], s["x0"], p["g1"], dx1)
        dwt = _in_proj_grad_call(dsections, s["h"])
        big_grads[4 * l] = _row_perm_call(dwt, table_back, "unpack_grad_w_in").reshape(NDEV, IN_SHARD_PAD, D)
        for n, val in (("norm1_g", dg1[0]), ("conv_w", dcw.transpose(1, 0, 2).reshape(4, 3 * DN_W)), ("a_log", dalog[0, :NH]),
                       ("dt_bias", ddtb[0, :NH]), ("dn_out_g", ddn_g[0]), ("sb_q_g", dqg[0, :64]), ("sb_k_g", dkg[0, :64]),
                       ("sg_v_g", dsg_g[0]), ("sg_w", dsg_w), ("sg_b", dsg_b[:, :NH].T), ("norm2_g", dg2[0])):
            small_grads[n][l] = val
    grad_x = dy.reshape(B, T, D)

    mine = _reduce_scatter(big_grads, "reduce_grads")
    grads = {"w_in": jnp.stack([mine[4 * l][:IN_SHARD].T for l in range(2)]), "w_out": jnp.stack([mine[1], mine[5]]),
             "w_ff1": jnp.stack([mine[2], mine[6]]), "w_ff2": jnp.stack([mine[3], mine[7]])}
    small_flat = jnp.concatenate([jnp.stack(small_grads[n]).reshape(-1) for n in SMALL])
    everyone, = _all_gather_call([_to_rows(small_flat, 8)], "gather_small_grads")
    small_sum = _sum_call([(everyone, k) for k in range(NDEV)], F32, "sum_small_grads").reshape(-1)
    off = 0
    for n in SMALL:
        sz = 2 * _size(SMALL_SHAPE[n])
        grads[n] = small_sum[off:off + sz].reshape((2,) + SMALL_SHAPE[n])
        off += sz
    cshard = conv_w.shape[-1]
    grads["conv_w"] = lax.dynamic_slice_in_dim(grads["conv_w"], me * cshard, cshard, axis=2)

    deltas, new_m, new_v = {}, {}, {}
    for n in WEIGHTS:
        deltas[n], new_m[n], new_v[n] = _adamw_call(given[n], grads[n], mom[n], var[n], "adamw_" + n)
    return (loss, grad_x, *[grads[n] for n in WEIGHTS], *[deltas[n] for n in WEIGHTS], *[new_m[n] for n in WEIGHTS],
            *[new_v[n] for n in WEIGHTS])
```

```python
import functools

import numpy as np

import jax
import jax.numpy as jnp
from jax import lax
from jax.experimental import pallas as pl
from jax.experimental.pallas import tpu as pltpu

F32, BF16 = jnp.float32, jnp.bfloat16
EPS = 1e-6
LANES = 128
D = 1024
DFF = 4096
NH = 4
DN_W, SB_W, SG_W = 512, 256, 256
IN_DIM = 3336
NDEV = 8
IN_SHARD = IN_DIM // NDEV
IN_SHARD_PAD = 432
FF_SHARD = DFF // NDEV
DN_OFF, SB_OFF, SG_OFF, AB_OFF, NPACK = 0, 2048, 2816, 3328, 3456
SECTIONS = ((DN_OFF, 2048), (SB_OFF, 768), (SG_OFF, 512), (AB_OFF, 128))
SB_SCALE = 64 ** -0.5
DN_SCALE = 128 ** -0.5
VMEM_LIMIT = 56 * 1024 * 1024
VMEM_LIMIT_MAX = 62 * 1024 * 1024
ADAM_LR, ADAM_B1, ADAM_B2, ADAM_EPS, ADAM_WD, ADAM_STEP = 0.001, 0.9, 0.999, 1e-08, 0.01, 10
MESH = pl.DeviceIdType.MESH


def _iota(shape, dim):
    return lax.broadcasted_iota(jnp.int32, shape, dim)


def _params(**kw):
    return pltpu.CompilerParams(vmem_limit_bytes=VMEM_LIMIT, **kw)


NN, NT, TN = ((1,), (0,)), ((1,), (1,)), ((0,), (0,))


def _mm(a, b, dims):
    return lax.dot_general(a.astype(BF16), b.astype(BF16), (dims, ((), ())), preferred_element_type=F32)


def _plain(a, b, dims):
    return (a.T if dims == TN else a), (b.T if dims == NT else b)


def _mmx(a, b, dims):
    return _mm(*_plain(a, b, dims), NN)


@jax.custom_vjp
def _dot(a, b):
    return _mmx(a, b, NN)


def _dot_fwd(a, b):
    return _dot(a, b), (a, b)


def _dot_bwd(res, g):
    a, b = res
    return _mmx(g, b, NT).astype(a.dtype), _mmx(a, g, TN).astype(b.dtype)


_dot.defvjp(_dot_fwd, _dot_bwd)


@jax.custom_vjp
def _dot_nt(a, b):
    return _mmx(a, b, NT)


def _dot_nt_fwd(a, b):
    return _dot_nt(a, b), (a, b)


def _dot_nt_bwd(res, g):
    a, b = res
    return _mmx(g, b, NN).astype(a.dtype), _mmx(g, a, TN).astype(b.dtype)


_dot_nt.defvjp(_dot_nt_fwd, _dot_nt_bwd)


@jax.custom_vjp
def _dot_tn(a, b):
    return _mmx(a, b, TN)


def _dot_tn_fwd(a, b):
    return _dot_tn(a, b), (a, b)


def _dot_tn_bwd(res, g):
    a, b = res
    return _mmx(b, g, NT).astype(a.dtype), _mmx(a, g, NN).astype(b.dtype)


_dot_tn.defvjp(_dot_tn_fwd, _dot_tn_bwd)


def _split(x):
    hi = x.astype(BF16)
    return hi, (x - hi.astype(F32)).astype(BF16)


def _mm3(a, b, dims):
    a, b = _plain(a, b, dims)
    (ah, al), (bh, bl) = _split(a), _split(b)
    mm = lambda x, y: jnp.dot(x, y, preferred_element_type=F32)
    return mm(ah, bh) + (mm(ah, bl) + mm(al, bh))


def _mm_ones(ones, x, ones_left):
    hi, lo = _split(x)
    mm = (lambda t: jnp.dot(ones, t, preferred_element_type=F32)) if ones_left else \
         (lambda t: jnp.dot(t, ones, preferred_element_type=F32))
    return mm(hi) + mm(lo)


def _pair_ones(kind, transposed):
    row, col = _iota((128, 128), 0), _iota((128, 128), 1)
    m = (row // 64) == (col // 64)
    if kind == "running":
        m = jnp.logical_and(m, (col >= row) if transposed else (col <= row))
    return jnp.where(m, 1.0, 0.0).astype(BF16)


@functools.partial(jax.custom_vjp, nondiff_argnums=(0,))
def _chunk_sum(kind, x):
    return _mm_ones(_pair_ones(kind, False), x, True)


def _chunk_sum_fwd(kind, x):
    return _chunk_sum(kind, x), None


def _chunk_sum_bwd(kind, _, g):
    return (_mm_ones(_pair_ones(kind, True), g, True),)


_chunk_sum.defvjp(_chunk_sum_fwd, _chunk_sum_bwd)


def _tri_ones(n, transposed):
    row, col = _iota((n, n), 0), _iota((n, n), 1)
    return jnp.where((row < col) if transposed else (row > col), 1.0, 0.0).astype(BF16)


@jax.custom_vjp
def _suffix_sum(x):
    return _mm_ones(_tri_ones(x.shape[1], False), x, False)


def _suffix_sum_fwd(x):
    return _suffix_sum(x), None


def _suffix_sum_bwd(_, g):
    return (_mm_ones(_tri_ones(g.shape[1], True), g, False),)


_suffix_sum.defvjp(_suffix_sum_fwd, _suffix_sum_bwd)


def _sigmoid(x):
    return jax.nn.sigmoid(x)


def _silu(x):
    return x * _sigmoid(x)


def _softplus(x):
    return jnp.maximum(x, 0.0) + jnp.log1p(jnp.exp(-jnp.abs(x)))


def _gelu(x):
    return 0.5 * x * (1.0 + jnp.tanh(0.7978845608028654 * (x + 0.044715 * (x * x * x))))


def _rms(x, gain):
    return x * lax.rsqrt(jnp.mean(x * x, axis=-1, keepdims=True) + EPS) * gain


def _shift_down_impl(x, k):
    return jnp.where(_iota(x.shape, 0) >= k, pltpu.roll(x, k, 0), 0.0)


def _shift_up_impl(x, k):
    n = x.shape[0]
    return jnp.where(_iota(x.shape, 0) < n - k, pltpu.roll(x, n - k, 0), 0.0)


@functools.partial(jax.custom_vjp, nondiff_argnums=(1,))
def _shift_down(x, k):
    return _shift_down_impl(x, k)


def _shift_down_fwd(x, k):
    return _shift_down_impl(x, k), None


def _shift_down_bwd(k, _, g):
    return (_shift_up_impl(g, k),)


_shift_down.defvjp(_shift_down_fwd, _shift_down_bwd)


def _lane_pick(x, idx):
    return jnp.sum(jnp.where(_iota(x.shape, 1) == idx, x, 0.0), axis=-1, keepdims=True)


def _dn_conv(x, w0, w1, w2, w3, l2_scale):
    y = _silu(w3 * x + w2 * _shift_down(x, 1) + w1 * _shift_down(x, 2) + w0 * _shift_down(x, 3))
    if l2_scale is None:
        return y
    return y * lax.rsqrt(jnp.sum(y * y, axis=-1, keepdims=True) + EPS) * l2_scale


def _dn_gate(a, b, a_log, dt_bias):
    return -jnp.exp(a_log) * _softplus(a + dt_bias), _sigmoid(b)


def _unit_lower_inverse(lower):
    n = lower.shape[0]
    nk = -lower
    inv = jnp.where(_iota((n, n), 0) == _iota((n, n), 1), 1.0, 0.0) + nk
    for _ in range(5):
        nk = _mm(nk, nk, NN)
        inv = inv + _mm(inv, nk, NN)
    return inv


@jax.custom_vjp
def _solve_with(lower, inv, rhs):
    return _mm3(inv, rhs, ((1,), (0,)))


def _solve_with_fwd(lower, inv, rhs):
    x = _mm3(inv, rhs, ((1,), (0,)))
    return x, (inv, x)


def _solve_with_bwd(res, g):
    inv, x = res
    d_rhs = _mm3(inv, g, ((0,), (0,)))
    return -_mm3(d_rhs, x, ((1,), (1,))), jnp.zeros_like(inv), d_rhs


_solve_with.defvjp(_solve_with_fwd, _solve_with_bwd)


def _dn_local(q, k, v, g, beta, inv=None):
    n = 128
    row, col = _iota((n, n), 0), _iota((n, n), 1)
    same = (row // 64) == (col // 64)
    tri_incl = jnp.logical_and(same, col <= row)
    tri_strict = jnp.logical_and(same, col < row)
    first = row < 64
    gb = jnp.broadcast_to(g, (n, n))
    gc = _chunk_sum("running", gb)
    gl = _chunk_sum("total", gb)
    diff = gc - gc.T
    decay = jnp.where(tri_incl, jnp.exp(jnp.where(tri_incl, diff, 0.0)), 0.0)
    egc = jnp.exp(gc)
    kk = _dot_nt(k, k)
    lower = jnp.where(tri_strict, beta * kk * decay, 0.0)
    if inv is None:
        inv = _unit_lower_inverse(lower)
    u_val = _solve_with(lower, inv, v * beta)
    w_dec = _solve_with(lower, inv, k * (beta * egc))
    qk = jnp.where(tri_incl, _dot_nt(q, k) * decay, 0.0)
    q_dec = q * egc
    k_dec = k * jnp.exp(gl - gc)
    cd1 = jnp.exp(jnp.sum(jnp.where(first, gb, 0.0), axis=0, keepdims=True))
    cd2 = jnp.exp(jnp.sum(jnp.where(first, 0.0, gb), axis=0, keepdims=True))
    return (u_val, w_dec, qk, q_dec, k_dec, cd1, cd2), inv


def _dn_state(u_val, w_dec, qk, q_dec, k_dec, cd1, cd2, s0):
    first = _iota((128, 128), 0) < 64
    u1 = u_val - _dot(w_dec, s0)
    s1 = s0 * cd1 + _dot_tn(jnp.where(first, k_dec, 0.0), u1)
    u2 = u_val - _dot(w_dec, s1)
    u_new = jnp.where(first, u1, u2)
    s2 = s1 * cd2 + _dot_tn(jnp.where(first, 0.0, k_dec), u_new)
    o = jnp.where(first, _dot(q_dec, s0), _dot(q_dec, s1)) + _dot(qk, u_new)
    return o, s2


def _dn_post(o, z, gain):
    return _rms(o, gain) * _silu(z)


def _dn_gate_in(ab_ref, alog_ref, dtb_ref, h):
    ab = ab_ref[...]
    return _lane_pick(ab, h), _lane_pick(ab, h + NH), _lane_pick(alog_ref[...], h), _lane_pick(dtb_ref[...], h)


_DN_L2 = (DN_SCALE, 1.0, None)
DN_HPS = 2
DN_BLK = 4 * DN_HPS * LANES
_DN_COLS = tuple(slice(i * LANES, (i + 1) * LANES) for i in range(DN_HPS))


def _dn_in_cols(s, i):
    return slice((s * DN_HPS + i) * LANES, (s * DN_HPS + i + 1) * LANES)


def _dn_taps(cw_ref, s, i):
    return tuple(cw_ref[t:t + 1, _dn_in_cols(s, i)] for t in range(4))


def _dn_pack_gate(vals):
    lane = _iota((1, LANES), 1)
    out = 0.0
    for i, (g, beta) in enumerate(vals):
        out = out + jnp.where(lane == 2 * i, g, 0.0) + jnp.where(lane == 2 * i + 1, beta, 0.0)
    return out


def _pair_rows(n):
    return pl.ds(pl.multiple_of(n * 128, 128), 128)


def _dn_in_specs(T):
    one = pl.Buffered(1)
    vec = pl.BlockSpec((1, LANES), lambda b, h: (0, 0))
    return [pl.BlockSpec((T, DN_BLK), lambda b, h: (b, h), pipeline_mode=one),
            pl.BlockSpec((T, LANES), lambda b, h: (b, 0), pipeline_mode=one),
            pl.BlockSpec((4, 3 * DN_HPS * LANES), lambda b, h: (0, h)), vec, vec, vec]


def _dn_fwd_call(proj_dn, proj_ab, conv_w, a_log, dt_bias, gain, B, T):
    npair = T // 128

    def body(x_ref, ab_ref, cw_ref, alog_ref, dtb_ref, gain_ref, out_ref, q_s, k_s, v_s, o_s, gate_s):
        hp = pl.program_id(1)
        gates = []
        for i, cs in enumerate(_DN_COLS):
            for s, (x_s, l2) in enumerate(zip((q_s, k_s, v_s), _DN_L2)):
                x_s[:, cs] = _dn_conv(x_ref[:, _dn_in_cols(s, i)], *_dn_taps(cw_ref, s, i), l2)
            gates.append(_dn_gate(*_dn_gate_in(ab_ref, alog_ref, dtb_ref, DN_HPS * hp + i)))
        gate_s[...] = _dn_pack_gate(gates)

        def local_of(pair):
            r = _pair_rows(pair)
            gate = gate_s[r, :]
            return tuple(_dn_local(q_s[r, cs], k_s[r, cs], v_s[r, cs], _lane_pick(gate, 2 * i), _lane_pick(gate, 2 * i + 1))[0]
                         for i, cs in enumerate(_DN_COLS))

        def state_of(n, locs, states):
            new = []
            for i, cs in enumerate(_DN_COLS):
                o, s2 = _dn_state(*locs[i], states[i])
                o_s[_pair_rows(n), cs] = o
                new.append(s2)
            return tuple(new)

        def step(n, carry):
            locs, states = carry
            return local_of(n + 1), state_of(n, locs, states)

        locs, states = lax.fori_loop(0, npair - 1, step, (local_of(0), (jnp.zeros((128, 128), F32),) * DN_HPS))
        state_of(npair - 1, locs, states)
        for i, cs in enumerate(_DN_COLS):
            out_ref[:, cs] = _dn_post(o_s[:, cs], x_ref[:, _dn_in_cols(3, i)], gain_ref[...])

    wide = [pltpu.VMEM((T, DN_HPS * LANES), F32)]
    return pl.pallas_call(
        body, name="dn_fwd", grid=(B, NH // DN_HPS), in_specs=_dn_in_specs(T),
        out_specs=pl.BlockSpec((T, DN_HPS * LANES), lambda b, h: (b, h), pipeline_mode=pl.Buffered(1)),
        out_shape=jax.ShapeDtypeStruct((B * T, D), F32),
        scratch_shapes=wide * 4 + [pltpu.VMEM((T, LANES), F32)],
        compiler_params=_params(dimension_semantics=("arbitrary", "arbitrary")),
    )(proj_dn, proj_ab, conv_w, a_log, dt_bias, gain)


def _dn_bwd_call(proj_dn, proj_ab, dmix, conv_w, a_log, dt_bias, gain, B, T):
    npair = T // 128

    def body(x_ref, ab_ref, cw_ref, alog_ref, dtb_ref, gain_ref, do_ref,
             dx_ref, dab_ref, dcw_ref, dalog_ref, ddtb_ref, dgain_ref,
             q_s, k_s, v_s, o_s, gate_s, dgate_s, st_s, inv_s, dcd_s, *dloc_s):
        b_i, hp = pl.program_id(0), pl.program_id(1)
        gates = []
        for i, cs in enumerate(_DN_COLS):
            for s, (x_s, l2) in enumerate(zip((q_s, k_s, v_s), _DN_L2)):
                x_s[:, cs] = _dn_conv(x_ref[:, _dn_in_cols(s, i)], *_dn_taps(cw_ref, s, i), l2)
            gates.append(_dn_gate(*_dn_gate_in(ab_ref, alog_ref, dtb_ref, DN_HPS * hp + i)))
        gate_s[...] = _dn_pack_gate(gates)

        def pair_in(r, i, cs, gate):
            return q_s[r, cs], k_s[r, cs], v_s[r, cs], _lane_pick(gate, 2 * i), _lane_pick(gate, 2 * i + 1)

        def local_of(pair, known_inverse=False):
            r = _pair_rows(pair)
            gate = gate_s[r, :]
            locs = []
            for i, cs in enumerate(_DN_COLS):
                loc, inv = _dn_local(*pair_in(r, i, cs, gate), inv_s[pair, i] if known_inverse else None)
                if not known_inverse:
                    inv_s[pair, i] = inv
                locs.append(loc)
            return tuple(locs)

        def state_of(n, locs, states):
            new = []
            for i, cs in enumerate(_DN_COLS):
                st_s[n, i] = states[i]
                o, s2 = _dn_state(*locs[i], states[i])
                o_s[_pair_rows(n), cs] = o
                new.append(s2)
            return tuple(new)

        def step(n, carry):
            locs, states = carry
            return local_of(n + 1), state_of(n, locs, states)

        zero_states = (jnp.zeros((128, 128), F32),) * DN_HPS
        locs, states = lax.fori_loop(0, npair - 1, step, (local_of(0), zero_states))
        state_of(npair - 1, locs, states)

        @pl.when(jnp.logical_and(b_i == 0, hp == 0))
        def _():
            dcw_ref[...] = jnp.zeros_like(dcw_ref)
            dalog_ref[...] = jnp.zeros_like(dalog_ref)
            ddtb_ref[...] = jnp.zeros_like(ddtb_ref)
            dgain_ref[...] = jnp.zeros_like(dgain_ref)

        for i, cs in enumerate(_DN_COLS):
            zc = _dn_in_cols(3, i)
            _, post_vjp = jax.vjp(_dn_post, o_s[:, cs], x_ref[:, zc], gain_ref[...])
            do, dz, dgain = post_vjp(do_ref[:, cs])
            dx_ref[:, zc] = dz
            o_s[:, cs] = do
            dgain_ref[...] += dgain

        def state_back(nn, dstates):
            n = npair - 1 - nn
            r = _pair_rows(n)
            locs = local_of(n, known_inverse=True)
            new = []
            for i, cs in enumerate(_DN_COLS):
                _, state_vjp = jax.vjp(_dn_state, *locs[i], st_s[n, i])
                *dloc, ds0 = state_vjp((o_s[r, cs], dstates[i]))
                for d_s, val in zip(dloc_s, dloc[:5]):
                    d_s[r, cs] = val
                dcd_s[n, i, 0:1, :], dcd_s[n, i, 1:2, :] = dloc[5], dloc[6]
                new.append(ds0)
            return tuple(new)

        lax.fori_loop(0, npair, state_back, zero_states)

        def local_back(m, _):
            for pair in (2 * m, 2 * m + 1):
                r = _pair_rows(pair)
                gate = gate_s[r, :]
                dgates = []
                for i, cs in enumerate(_DN_COLS):
                    inv = inv_s[pair, i]
                    local = lambda q, k, v, g, beta, inv=inv: _dn_local(q, k, v, g, beta, inv)[0]
                    _, local_vjp = jax.vjp(local, *pair_in(r, i, cs, gate))
                    dloc = tuple(d_s[r, cs] for d_s in dloc_s) + (dcd_s[pair, i, 0:1, :], dcd_s[pair, i, 1:2, :])
                    dq, dk, dv, dg, db = local_vjp(dloc)
                    dx_ref[r, _dn_in_cols(0, i)], dx_ref[r, _dn_in_cols(1, i)], dx_ref[r, _dn_in_cols(2, i)] = dq, dk, dv
                    dgates.append((dg, db))
                dgate_s[r, :] = _dn_pack_gate(dgates)
            return 0

        lax.fori_loop(0, npair // 2, local_back, 0)

        lane = _iota((1, LANES), 1)
        dab = 0.0
        for i, cs in enumerate(_DN_COLS):
            h = DN_HPS * hp + i
            for s, l2 in enumerate(_DN_L2):
                xc = _dn_in_cols(s, i)
                _, conv_vjp = jax.vjp(functools.partial(_dn_conv, l2_scale=l2), x_ref[:, xc], *_dn_taps(cw_ref, s, i))
                dx, *dw = conv_vjp(dx_ref[:, xc])
                dx_ref[:, xc] = dx
                for t in range(4):
                    dcw_ref[h + 4 * s, t:t + 1, :] += dw[t]
            _, gate_vjp = jax.vjp(_dn_gate, *_dn_gate_in(ab_ref, alog_ref, dtb_ref, h))
            dgate = dgate_s[...]
            da, db, dalog, ddtb = gate_vjp((_lane_pick(dgate, 2 * i), _lane_pick(dgate, 2 * i + 1)))
            dab = dab + jnp.where(lane == h, da, 0.0) + jnp.where(lane == h + NH, db, 0.0)
            dalog_ref[...] += jnp.where(lane == h, dalog, 0.0)
            ddtb_ref[...] += jnp.where(lane == h, ddtb, 0.0)

        @pl.when(hp == 0)
        def _():
            dab_ref[...] = jnp.zeros_like(dab_ref)

        dab_ref[...] += dab

    M = B * T
    one = pl.Buffered(1)
    vec = pl.BlockSpec((1, LANES), lambda b, h: (0, 0))
    wide = [pltpu.VMEM((T, DN_HPS * LANES), F32)]
    narrow = [pltpu.VMEM((T, LANES), F32)]
    vec_shape = jax.ShapeDtypeStruct((1, LANES), F32)
    return pl.pallas_call(
        body, name="dn_bwd", grid=(B, NH // DN_HPS),
        in_specs=_dn_in_specs(T) + [pl.BlockSpec((T, DN_HPS * LANES), lambda b, h: (b, h), pipeline_mode=one)],
        out_specs=[pl.BlockSpec((T, DN_BLK), lambda b, h: (b, h), pipeline_mode=one), pl.BlockSpec((T, LANES), lambda b, h: (b, 0)),
                   pl.BlockSpec((12, 4, LANES), lambda b, h: (0, 0, 0)), vec, vec, vec],
        out_shape=[jax.ShapeDtypeStruct((M, 4 * DN_W), F32), jax.ShapeDtypeStruct((M, LANES), F32),
                   jax.ShapeDtypeStruct((12, 4, LANES), F32), vec_shape, vec_shape, vec_shape],
        scratch_shapes=wide * 4 + narrow * 2 + [pltpu.VMEM((npair, DN_HPS, 128, 128), F32)] * 2
        + [pltpu.VMEM((npair, DN_HPS, 8, LANES), F32)] + wide * 5,
        compiler_params=pltpu.CompilerParams(vmem_limit_bytes=VMEM_LIMIT_MAX, dimension_semantics=("arbitrary", "arbitrary")),
    )(proj_dn, proj_ab, conv_w, a_log, dt_bias, gain, dmix)


SBQ = 256


def _group_rms(x, gain):
    first = _iota(x.shape, 1) < 64
    sq = x * x
    ss_a = jnp.sum(jnp.where(first, sq, 0.0), axis=-1, keepdims=True)
    ss_b = jnp.sum(jnp.where(first, 0.0, sq), axis=-1, keepdims=True)
    ms = jnp.where(first, ss_a, ss_b) * (1.0 / 64)
    return x * lax.rsqrt(ms + EPS) * gain


def _sb_stack(q):
    first = _iota((1, LANES), 1) < 64
    return jnp.concatenate([jnp.where(first, q, 0.0), jnp.where(first, 0.0, q)], axis=0)


def _sb_fold(acc):
    return jnp.where(_iota((1, LANES), 1) < 64, acc[:SBQ], acc[SBQ:])


def _sb_logs(q2, k, diag):
    n = SBQ
    z = _mm(q2, k, ((1,), (1,))) * SB_SCALE
    ls_pos = jnp.minimum(z, 0.0) - jnp.log1p(jnp.exp(-jnp.abs(z)))
    l1m = ls_pos - z
    if not diag:
        return ls_pos, l1m, None
    mask = _iota((2 * n, n), 1) < jnp.bitwise_and(_iota((2 * n, n), 0), n - 1)
    return ls_pos, jnp.where(mask, l1m, 0.0), mask


def _sb_weights(ls_pos, l1m, mask, carry):
    w = jnp.exp(ls_pos + (_mm_ones(_tri_ones(SBQ, False), l1m, False) + carry))
    return w if mask is None else jnp.where(mask, w, 0.0)


def _sb_block(q, k, v, carry, diag):
    ls_pos, l1m, mask = _sb_logs(_sb_stack(q), k, diag)
    w = _sb_weights(ls_pos, l1m, mask, carry)
    return _mm(w, v, ((1,), (0,))), carry + jnp.sum(l1m, axis=-1, keepdims=True)


def _sb_rowsum(q, k, diag):
    return jnp.sum(_sb_logs(_sb_stack(q), k, diag)[1], axis=-1, keepdims=True)


def _sb_block_bwd(q, k, v, carry, diag, dpv, dcarry):
    q2 = _sb_stack(q)
    ls_pos, l1m, mask = _sb_logs(q2, k, diag)
    w = _sb_weights(ls_pos, l1m, mask, carry)
    dv = _mm(w, dpv, ((0,), (0,)))
    de = _mm(dpv, v, ((1,), (1,))) * w
    dl1m = _mm_ones(_tri_ones(SBQ, True), de, False) + dcarry
    if mask is not None:
        dl1m = jnp.where(mask, dl1m, 0.0)
    sig = jnp.exp(ls_pos)
    dz = (de * (1.0 - sig) - dl1m * sig) * SB_SCALE
    dq = _sb_fold(_mm(dz, k, ((1,), (0,))))
    return dq, _mm(dz, q2, ((0,), (0,))), dv, dcarry + jnp.sum(de, axis=-1, keepdims=True)


_SB_Q, _SB_K, _SB_V = (slice(i * LANES, (i + 1) * LANES) for i in range(3))


def _sb_fwd_call(proj_sb, mix, q_gain, k_gain, B, T):
    nblk = T // SBQ

    def body(x_ref, qg_ref, kg_ref, mix_ref, out_ref, q_s, k_s):
        del mix_ref
        q_s[...] = _group_rms(x_ref[:, _SB_Q], qg_ref[...])
        k_s[...] = _group_rms(x_ref[:, _SB_K], kg_ref[...])

        def qblock(i, _):
            ri = pl.ds(pl.multiple_of(i * SBQ, SBQ), SBQ)
            q = q_s[ri, :]

            def kblock(jj, c):
                rj = pl.ds(pl.multiple_of((i - 1 - jj) * SBQ, SBQ), SBQ)
                pv, carry = _sb_block(q, k_s[rj, :], x_ref[rj, _SB_V], c[1], False)
                return c[0] + pv, carry

            on_diag = _sb_block(q, k_s[ri, :], x_ref[ri, _SB_V], jnp.zeros((2 * SBQ, 1), F32), True)
            acc, _c = lax.fori_loop(0, i, kblock, on_diag)
            out_ref[ri, :] = _sb_fold(acc)
            return 0

        lax.fori_loop(0, nblk, qblock, 0)

    vec = pl.BlockSpec((1, LANES), lambda b, p: (0, 0))
    return pl.pallas_call(
        body, name="sb_fwd", grid=(B, 2),
        in_specs=[pl.BlockSpec((T, 3 * LANES), lambda b, p: (b, p)), vec, vec, pl.BlockSpec(memory_space=pl.ANY)],
        out_specs=pl.BlockSpec((T, LANES), lambda b, p: (b, DN_W // LANES + p)),
        out_shape=jax.ShapeDtypeStruct((B * T, D), F32), input_output_aliases={3: 0},
        scratch_shapes=[pltpu.VMEM((T, LANES), F32)] * 2,
        compiler_params=_params(dimension_semantics=("arbitrary", "arbitrary")),
    )(proj_sb, q_gain, k_gain, mix)


def _sb_bwd_call(proj_sb, dmix, q_gain, k_gain, B, T):
    nblk = T // SBQ

    def body(x_ref, qg_ref, kg_ref, do_ref, dx_ref, dqg_ref, dkg_ref, q_s, k_s, dq_s, dk_s, dv_s, c_s):
        b_i, p = pl.program_id(0), pl.program_id(1)
        qn, q_vjp = jax.vjp(_group_rms, x_ref[:, _SB_Q], qg_ref[...])
        kn, k_vjp = jax.vjp(_group_rms, x_ref[:, _SB_K], kg_ref[...])
        q_s[...], k_s[...] = qn, kn
        dk_s[...] = jnp.zeros_like(dk_s)
        dv_s[...] = jnp.zeros_like(dv_s)

        def qblock(i, _):
            ri = pl.ds(pl.multiple_of(i * SBQ, SBQ), SBQ)
            q = q_s[ri, :]
            dacc = _sb_stack(do_ref[ri, :])

            def carries(jj, carry):
                j = i - 1 - jj
                rj = pl.ds(pl.multiple_of(j * SBQ, SBQ), SBQ)
                c_s[j] = carry
                return carry + _sb_rowsum(q, k_s[rj, :], False)

            lax.fori_loop(0, i, carries, _sb_rowsum(q, k_s[ri, :], True))

            def kblock(j, c):
                rj = pl.ds(pl.multiple_of(j * SBQ, SBQ), SBQ)
                dq_j, dk_j, dv_j, dc = _sb_block_bwd(q, k_s[rj, :], x_ref[rj, _SB_V], c_s[j], False, dacc, c[1])
                dk_s[rj, :] += dk_j
                dv_s[rj, :] += dv_j
                return c[0] + dq_j, dc

            dq, dc = lax.fori_loop(0, i, kblock, (jnp.zeros((SBQ, LANES), F32), jnp.zeros((2 * SBQ, 1), F32)))
            dq_i, dk_i, dv_i, _dc = _sb_block_bwd(q, k_s[ri, :], x_ref[ri, _SB_V], jnp.zeros((2 * SBQ, 1), F32), True, dacc, dc)
            dk_s[ri, :] += dk_i
            dv_s[ri, :] += dv_i
            dq_s[ri, :] = dq + dq_i
            return 0

        lax.fori_loop(0, nblk, qblock, 0)
        dq_in, dqg = q_vjp(dq_s[...])
        dk_in, dkg = k_vjp(dk_s[...])
        dx_ref[:, _SB_Q], dx_ref[:, _SB_K], dx_ref[:, _SB_V] = dq_in, dk_in, dv_s[...]

        @pl.when(jnp.logical_and(b_i == 0, p == 0))
        def _():
            dqg_ref[...] = jnp.zeros_like(dqg_ref)
            dkg_ref[...] = jnp.zeros_like(dkg_ref)

        dqg_ref[...] += dqg + pltpu.roll(dqg, 64, 1)
        dkg_ref[...] += dkg + pltpu.roll(dkg, 64, 1)

    M = B * T
    vec = pl.BlockSpec((1, LANES), lambda b, p: (0, 0))
    blk = pl.BlockSpec((T, 3 * LANES), lambda b, p: (b, p))
    big = [pltpu.VMEM((T, LANES), F32)]
    return pl.pallas_call(
        body, name="sb_bwd", grid=(B, 2),
        in_specs=[blk, vec, vec, pl.BlockSpec((T, LANES), lambda b, p: (b, DN_W // LANES + p))],
        out_specs=[blk, vec, vec],
        out_shape=[jax.ShapeDtypeStruct((M, 3 * SB_W), F32)] + [jax.ShapeDtypeStruct((1, LANES), F32)] * 2,
        scratch_shapes=big * 5 + [pltpu.VMEM((nblk, 2 * SBQ, 1), F32)],
        compiler_params=_params(dimension_semantics=("arbitrary", "arbitrary")),
    )(proj_sb, q_gain, k_gain, dmix)


def _sg_chunk(u, v, gain, w_a, w_b, bias):
    n = 128
    row, col = _iota((n, n), 0), _iota((n, n), 1)
    first = _iota((1, LANES), 1) < 64
    vn = _group_rms(_gelu(v), gain)
    tril = col <= row
    mixed = jnp.where(first, _dot(jnp.where(tril, w_a, 0.0), vn), _dot(jnp.where(tril, w_b, 0.0), vn)) + bias
    return _gelu(u) * mixed


_SG_U, _SG_V = slice(0, LANES), slice(LANES, 2 * LANES)


def _sg_fwd_call(proj_sg, mix, gain, sg_w, bias, B, T):
    nchunk = T // 128

    def body(x_ref, g_ref, wa_ref, wb_ref, bias_ref, mix_ref, out_ref):
        del mix_ref

        def step(i, _):
            r = pl.ds(pl.multiple_of(i * 128, 128), 128)
            out_ref[r, :] = _sg_chunk(x_ref[r, _SG_U], x_ref[r, _SG_V], g_ref[...], wa_ref[0], wb_ref[0], bias_ref[...])
            return 0

        lax.fori_loop(0, nchunk, step, 0)

    return pl.pallas_call(
        body, name="sg_fwd", grid=(B, 2),
        in_specs=[pl.BlockSpec((T, 2 * LANES), lambda b, p: (b, p)), pl.BlockSpec((1, LANES), lambda b, p: (0, p)),
                  pl.BlockSpec((1, 128, 128), lambda b, p: (2 * p, 0, 0)), pl.BlockSpec((1, 128, 128), lambda b, p: (2 * p + 1, 0, 0)),
                  pl.BlockSpec((128, LANES), lambda b, p: (0, p)), pl.BlockSpec(memory_space=pl.ANY)],
        out_specs=pl.BlockSpec((T, LANES), lambda b, p: (b, (DN_W + SB_W) // LANES + p)),
        out_shape=jax.ShapeDtypeStruct((B * T, D), F32), input_output_aliases={5: 0},
        compiler_params=_params(dimension_semantics=("arbitrary", "arbitrary")),
    )(proj_sg, gain, sg_w, sg_w, bias, mix)


def _sg_bwd_call(proj_sg, dmix, gain, sg_w, bias, B, T):
    nchunk = T // 128

    def body(x_ref, g_ref, wa_ref, wb_ref, bias_ref, do_ref, dx_ref, dg_ref, dw_ref, db_ref):
        p, b_i = pl.program_id(0), pl.program_id(1)

        def step(i, c):
            r = pl.ds(pl.multiple_of(i * 128, 128), 128)
            _, vjp = jax.vjp(_sg_chunk, x_ref[r, _SG_U], x_ref[r, _SG_V], g_ref[...], wa_ref[0], wb_ref[0], bias_ref[...])
            du, dv, dg, dwa, dwb, dbias = vjp(do_ref[r, :])
            dx_ref[r, _SG_U], dx_ref[r, _SG_V] = du, dv
            return c[0] + dg, c[1] + dwa, c[2] + dwb, c[3] + dbias

        z = jnp.zeros((128, 128), F32)
        dg, dwa, dwb, dbias = lax.fori_loop(0, nchunk, step, (jnp.zeros((1, LANES), F32), z, z, z))
        lane = _iota((1, LANES), 1)
        first = lane < 64
        s_a = jnp.sum(jnp.where(first, dbias, 0.0), axis=-1, keepdims=True)
        s_b = jnp.sum(jnp.where(first, 0.0, dbias), axis=-1, keepdims=True)
        dbg = jnp.where(lane == 2 * p, s_a, 0.0) + jnp.where(lane == 2 * p + 1, s_b, 0.0)

        @pl.when(b_i == 0)
        def _():
            dg_ref[...] = jnp.zeros_like(dg_ref)
            dw_ref[...] = jnp.zeros_like(dw_ref)

        @pl.when(jnp.logical_and(b_i == 0, p == 0))
        def _():
            db_ref[...] = jnp.zeros_like(db_ref)

        dg_ref[...] += dg
        dw_ref[0] += dwa
        dw_ref[1] += dwb
        db_ref[...] += dbg

    M = B * T
    blk = pl.BlockSpec((T, 2 * LANES), lambda p, b: (b, p))
    return pl.pallas_call(
        body, name="sg_bwd", grid=(2, B),
        in_specs=[blk, pl.BlockSpec((1, LANES), lambda p, b: (0, p)),
                  pl.BlockSpec((1, 128, 128), lambda p, b: (2 * p, 0, 0)), pl.BlockSpec((1, 128, 128), lambda p, b: (2 * p + 1, 0, 0)),
                  pl.BlockSpec((128, LANES), lambda p, b: (0, p)),
                  pl.BlockSpec((T, LANES), lambda p, b: (b, (DN_W + SB_W) // LANES + p))],
        out_specs=[blk, pl.BlockSpec((1, LANES), lambda p, b: (0, p)), pl.BlockSpec((2, 128, 128), lambda p, b: (p, 0, 0)),
                   pl.BlockSpec((128, LANES), lambda p, b: (0, 0))],
        out_shape=[jax.ShapeDtypeStruct((M, 2 * SG_W), F32), jax.ShapeDtypeStruct((1, SG_W), F32),
                   jax.ShapeDtypeStruct((4, 128, 128), F32), jax.ShapeDtypeStruct((128, LANES), F32)],
        compiler_params=_params(dimension_semantics=("arbitrary", "arbitrary")),
    )(proj_sg, gain, sg_w, sg_w, bias, dmix)


def _row_tile(m):
    return min(m, 512)


def _in_proj_call(x, gain, wt):
    m = x.shape[0]
    tm = _row_tile(m)

    def body(x_ref, g_ref, wt_ref, *out_refs):
        h = _rms(x_ref[...], g_ref[...]).astype(BF16)
        out_refs[-1][...] = h
        for (off, width), out_ref in zip(SECTIONS, out_refs):
            out_ref[...] = lax.dot_general(h, wt_ref[off:off + width, :], (((1,), (1,)), ((), ())), preferred_element_type=F32)

    rows = lambda width: pl.BlockSpec((tm, width), lambda i: (i, 0))
    return pl.pallas_call(
        body, name="in_proj", grid=(m // tm,),
        in_specs=[rows(D), pl.BlockSpec((1, D), lambda i: (0, 0)),
                  pl.BlockSpec((NPACK, D), lambda i: (0, 0), pipeline_mode=pl.Buffered(1))],
        out_specs=[rows(w) for _, w in SECTIONS] + [rows(D)],
        out_shape=[jax.ShapeDtypeStruct((m, w), F32) for _, w in SECTIONS] + [jax.ShapeDtypeStruct((m, D), BF16)],
        compiler_params=_params(dimension_semantics=("arbitrary",)),
    )(x, gain, wt)


def _in_proj_bwd_call(dsections, wt, x, gain, dres):
    m = x.shape[0]
    tm = _row_tile(m)

    def body(*refs):
        ds_refs, (wt_ref, x_ref, g_ref, dres_ref, dx_ref, dg_ref) = refs[:len(SECTIONS)], refs[len(SECTIONS):]

        @pl.when(pl.program_id(0) == 0)
        def _():
            dg_ref[...] = jnp.zeros_like(dg_ref)

        dh = 0.0
        for (off, width), ds_ref in zip(SECTIONS, ds_refs):
            dh = dh + jnp.dot(ds_ref[...].astype(BF16), wt_ref[off:off + width, :], preferred_element_type=F32)
        _, vjp = jax.vjp(_rms, x_ref[...], g_ref[...])
        dx, dg = vjp(dh)
        dx_ref[...] = dres_ref[...] + dx
        dg_ref[...] += dg

    rows = lambda width: pl.BlockSpec((tm, width), lambda i: (i, 0))
    return pl.pallas_call(
        body, name="in_proj_bwd", grid=(m // tm,),
        in_specs=[rows(w) for _, w in SECTIONS] + [pl.BlockSpec((NPACK, D), lambda i: (0, 0), pipeline_mode=pl.Buffered(1)),
                                                   rows(D), pl.BlockSpec((1, D), lambda i: (0, 0)), rows(D)],
        out_specs=[rows(D), pl.BlockSpec((1, D), lambda i: (0, 0))],
        out_shape=[jax.ShapeDtypeStruct((m, D), F32), jax.ShapeDtypeStruct((1, D), F32)],
        compiler_params=_params(dimension_semantics=("arbitrary",)),
    )(*dsections, wt, x, gain, dres)


def _in_proj_grad_call(dsections, h):
    m = h.shape[0]
    tm = min(m, 256)

    def body(*refs):
        ds_refs, (h_ref, out_ref) = refs[:len(SECTIONS)], refs[len(SECTIONS):]

        @pl.when(pl.program_id(0) == 0)
        def _():
            out_ref[...] = jnp.zeros_like(out_ref)

        for (off, width), ds_ref in zip(SECTIONS, ds_refs):
            out_ref[off:off + width, :] += lax.dot_general(ds_ref[...].astype(BF16), h_ref[...], (((0,), (0,)), ((), ())),
                                                           preferred_element_type=F32)

    rows = lambda width: pl.BlockSpec((tm, width), lambda i: (i, 0))
    return pl.pallas_call(
        body, name="grad_w_in", grid=(m // tm,),
        in_specs=[rows(w) for _, w in SECTIONS] + [rows(D)],
        out_specs=pl.BlockSpec((NPACK, D), lambda i: (0, 0), pipeline_mode=pl.Buffered(1)),
        out_shape=jax.ShapeDtypeStruct((NPACK, D), F32),
        compiler_params=_params(dimension_semantics=("arbitrary",)),
    )(*dsections, h)


def _packed_column_of():
    t = np.full(NPACK, -1, np.int64)
    lanes = np.arange(LANES)
    for pair in range(2):
        for s in range(4):
            t[DN_OFF + pair * 1024 + s * 256 + np.arange(256)] = s * DN_W + pair * 256 + np.arange(256)
        for s in range(3):
            t[SB_OFF + pair * 384 + s * LANES + lanes] = 2056 + s * SB_W + pair * LANES + lanes
        for s in range(2):
            t[SG_OFF + pair * 256 + s * LANES + lanes] = 2056 + 3 * SB_W + s * SG_W + pair * LANES + lanes
    t[AB_OFF + np.arange(2 * NH)] = 4 * DN_W + np.arange(2 * NH)
    return t


def _row_tables():
    col = _packed_column_of()
    fwd = np.where(col >= 0, (col // IN_SHARD) * IN_SHARD_PAD + col % IN_SHARD, -1)
    packed_of = np.full(IN_DIM, -1, np.int64)
    packed_of[col[col >= 0]] = np.nonzero(col >= 0)[0]
    r = np.arange(NDEV * IN_SHARD_PAD)
    inside = r % IN_SHARD_PAD < IN_SHARD
    back = np.where(inside, packed_of[np.minimum((r // IN_SHARD_PAD) * IN_SHARD + r % IN_SHARD_PAD, IN_DIM - 1)], -1)
    return fwd, back


def _row_perm_call(src, table, name):
    n_out = table.shape[0]
    touched = [sorted(set((table[b * 128:(b + 1) * 128][table[b * 128:(b + 1) * 128] >= 0] // 128).tolist()))
               for b in range(n_out // 128)]

    def body(tbl_ref, src_ref, out_ref):
        lane = _iota((1, LANES), 1)
        for b, blocks in enumerate(touched):
            want = tbl_ref[b * 128:(b + 1) * 128, :]
            acc = jnp.zeros((128, D), F32)
            for sb in blocks:
                pick = jnp.where(want == sb * 128 + lane, 1.0, 0.0).astype(BF16)
                acc = acc + jnp.dot(pick, src_ref[sb * 128:(sb + 1) * 128, :].astype(BF16), preferred_element_type=F32)
            out_ref[b * 128:(b + 1) * 128, :] = acc.astype(BF16)

    return pl.pallas_call(
        body, name=name, out_shape=jax.ShapeDtypeStruct((n_out, D), BF16),
        in_specs=[pl.BlockSpec(memory_space=pltpu.VMEM)] * 2, out_specs=pl.BlockSpec(memory_space=pltpu.VMEM),
        compiler_params=_params(),
    )(jnp.asarray(table.reshape(-1, 1), jnp.int32), src)


def _out_proj_call(a, w, res):
    m, k = a.shape
    n = w.shape[1]
    tm = _row_tile(m)

    def body(a_ref, w_ref, res_ref, out_ref):
        out_ref[...] = res_ref[...] + jnp.dot(a_ref[...].astype(BF16), w_ref[...], preferred_element_type=F32)

    return pl.pallas_call(
        body, name="out_proj", grid=(m // tm,),
        in_specs=[pl.BlockSpec((tm, k), lambda i: (i, 0)), pl.BlockSpec((k, n), lambda i: (0, 0)),
                  pl.BlockSpec((tm, n), lambda i: (i, 0))],
        out_specs=pl.BlockSpec((tm, n), lambda i: (i, 0)),
        out_shape=jax.ShapeDtypeStruct((m, n), F32),
        compiler_params=_params(dimension_semantics=("arbitrary",)),
    )(a, w, res)


def _ffn_specs(tm):
    return [pl.BlockSpec((1, D, FF_SHARD), lambda i, j: (j, 0, 0)), pl.BlockSpec((FF_SHARD, D), lambda i, j: (j, 0))]


def _ffn_fwd_call(x, gain, w1, w2):
    m = x.shape[0]
    tm = _row_tile(m)

    def body(x_ref, g_ref, w1_ref, w2_ref, out_ref, h_s, acc_s):
        j = pl.program_id(1)

        @pl.when(j == 0)
        def _():
            h_s[...] = _rms(x_ref[...], g_ref[...]).astype(BF16)
            acc_s[...] = jnp.zeros_like(acc_s)

        a = jnp.maximum(jnp.dot(h_s[...], w1_ref[0], preferred_element_type=F32), 0.0)
        acc_s[...] += jnp.dot((a * a).astype(BF16), w2_ref[...], preferred_element_type=F32)

        @pl.when(j == NDEV - 1)
        def _():
            out_ref[...] = x_ref[...] + acc_s[...]

    return pl.pallas_call(
        body, name="ffn_fwd", grid=(m // tm, NDEV),
        in_specs=[pl.BlockSpec((tm, D), lambda i, j: (i, 0)), pl.BlockSpec((1, D), lambda i, j: (0, 0))] + _ffn_specs(tm),
        out_specs=pl.BlockSpec((tm, D), lambda i, j: (i, 0)),
        out_shape=jax.ShapeDtypeStruct((m, D), F32),
        scratch_shapes=[pltpu.VMEM((tm, D), BF16), pltpu.VMEM((tm, D), F32)],
        compiler_params=_params(dimension_semantics=("arbitrary", "arbitrary")),
    )(x, gain, w1, w2)


def _ffn_bwd_call(x, dy, gain, w1, w2):
    m = x.shape[0]
    tm = _row_tile(m)

    def body(x_ref, dy_ref, g_ref, w1_ref, w2_ref, dx_ref, da_ref, r_ref, h_ref, dg_ref, acc_s):
        i, j = pl.program_id(0), pl.program_id(1)

        @pl.when(j == 0)
        def _():
            h_ref[...] = _rms(x_ref[...], g_ref[...]).astype(BF16)
            acc_s[...] = jnp.zeros_like(acc_s)

        @pl.when(jnp.logical_and(i == 0, j == 0))
        def _():
            dg_ref[...] = jnp.zeros_like(dg_ref)

        a = jnp.maximum(jnp.dot(h_ref[...], w1_ref[0], preferred_element_type=F32), 0.0)
        r_ref[...] = (a * a).astype(BF16)
        dr = lax.dot_general(dy_ref[...].astype(BF16), w2_ref[...], (((1,), (1,)), ((), ())), preferred_element_type=F32)
        da = (dr * (2.0 * a)).astype(BF16)
        da_ref[...] = da
        acc_s[...] += lax.dot_general(da, w1_ref[0], (((1,), (1,)), ((), ())), preferred_element_type=F32)

        @pl.when(j == NDEV - 1)
        def _():
            _, vjp = jax.vjp(_rms, x_ref[...], g_ref[...])
            dx, dg = vjp(acc_s[...])
            dx_ref[...] = dy_ref[...] + dx
            dg_ref[...] += dg

    return pl.pallas_call(
        body, name="ffn_bwd", grid=(m // tm, NDEV),
        in_specs=[pl.BlockSpec((tm, D), lambda i, j: (i, 0)), pl.BlockSpec((tm, D), lambda i, j: (i, 0)),
                  pl.BlockSpec((1, D), lambda i, j: (0, 0))] + _ffn_specs(tm),
        out_specs=[pl.BlockSpec((tm, D), lambda i, j: (i, 0)), pl.BlockSpec((tm, FF_SHARD), lambda i, j: (i, j)),
                   pl.BlockSpec((tm, FF_SHARD), lambda i, j: (i, j)), pl.BlockSpec((tm, D), lambda i, j: (i, 0)),
                   pl.BlockSpec((1, D), lambda i, j: (0, 0))],
        out_shape=[jax.ShapeDtypeStruct((m, D), F32), jax.ShapeDtypeStruct((m, DFF), BF16), jax.ShapeDtypeStruct((m, DFF), BF16),
                   jax.ShapeDtypeStruct((m, D), BF16), jax.ShapeDtypeStruct((1, D), F32)],
        scratch_shapes=[pltpu.VMEM((tm, D), F32)],
        compiler_params=_params(dimension_semantics=("arbitrary", "arbitrary")),
    )(x, dy, gain, w1, w2)


def _mm_nt_call(a, b, name):
    m, k = a.shape
    n = b.shape[0]
    tm = _row_tile(m)

    def body(a_ref, b_ref, out_ref):
        out_ref[...] = lax.dot_general(a_ref[...].astype(BF16), b_ref[...].astype(BF16), (((1,), (1,)), ((), ())),
                                       preferred_element_type=F32)

    return pl.pallas_call(
        body, name=name, grid=(m // tm,),
        in_specs=[pl.BlockSpec((tm, k), lambda i: (i, 0)), pl.BlockSpec((n, k), lambda i: (0, 0))],
        out_specs=pl.BlockSpec((tm, n), lambda i: (i, 0)),
        out_shape=jax.ShapeDtypeStruct((m, n), F32),
        compiler_params=_params(dimension_semantics=("arbitrary",)),
    )(a, b)


def _mm_tn_call(a, b, name, col_shards=False):
    m, k = a.shape
    n = b.shape[1]
    tm, tk = _row_tile(m), min(k, 1024)
    tn = n // NDEV if col_shards else min(n, 1024)

    def body(a_ref, b_ref, out_ref, acc_s):
        s = pl.program_id(2)

        @pl.when(s == 0)
        def _():
            acc_s[...] = jnp.zeros_like(acc_s)

        acc_s[...] += lax.dot_general(a_ref[...].astype(BF16), b_ref[...].astype(BF16), (((0,), (0,)), ((), ())),
                                      preferred_element_type=F32)

        @pl.when(s == m // tm - 1)
        def _():
            out_ref[...] = acc_s[...].astype(BF16).reshape(out_ref.shape)

    if col_shards:
        out_spec, out_shape = pl.BlockSpec((1, tk, tn), lambda i, j, s: (j, i, 0)), (NDEV, k, tn)
    else:
        out_spec, out_shape = pl.BlockSpec((tk, tn), lambda i, j, s: (i, j)), (k, n)
    return pl.pallas_call(
        body, name=name, grid=(k // tk, n // tn, m // tm),
        in_specs=[pl.BlockSpec((tm, tk), lambda i, j, s: (s, i)), pl.BlockSpec((tm, tn), lambda i, j, s: (s, j))],
        out_specs=out_spec, out_shape=jax.ShapeDtypeStruct(out_shape, BF16),
        scratch_shapes=[pltpu.VMEM((tk, tn), F32)],
        compiler_params=_params(dimension_semantics=("arbitrary", "arbitrary", "arbitrary")),
    )(a, b)


def _loss_call(y, target):
    m = y.shape[0]
    tm = _row_tile(m)

    def body(y_ref, t_ref, loss_ref, dy_ref):
        @pl.when(pl.program_id(0) == 0)
        def _():
            loss_ref[...] = jnp.zeros_like(loss_ref)

        err = y_ref[...] - t_ref[...]
        dy_ref[...] = err * (1.0 / D)
        per_row = jnp.mean(err * err, axis=-1, keepdims=True)
        loss_ref[...] += jnp.broadcast_to(0.5 * jnp.sum(per_row, axis=0, keepdims=True), (1, LANES))

    return pl.pallas_call(
        body, name="loss", grid=(m // tm,),
        in_specs=[pl.BlockSpec((tm, D), lambda i: (i, 0))] * 2,
        out_specs=[pl.BlockSpec((1, LANES), lambda i: (0, 0)), pl.BlockSpec((tm, D), lambda i: (i, 0))],
        out_shape=[jax.ShapeDtypeStruct((1, LANES), F32), jax.ShapeDtypeStruct((m, D), F32)],
        compiler_params=_params(dimension_semantics=("arbitrary",)),
    )(y, target)


def _adamw_call(w, g, m, v, name):
    shape = w.shape
    cols = shape[-1] if w.ndim > 1 else w.size
    rows = w.size // cols
    tr = rows if (rows <= 512 or rows % 512) else 512
    c1, c2 = 1.0 - ADAM_B1 ** ADAM_STEP, 1.0 - ADAM_B2 ** ADAM_STEP

    def body(w_ref, g_ref, m_ref, v_ref, d_ref, nm_ref, nv_ref):
        g_ = g_ref[...]
        nm = ADAM_B1 * m_ref[...] + (1.0 - ADAM_B1) * g_
        nv = ADAM_B2 * v_ref[...] + (1.0 - ADAM_B2) * (g_ * g_)
        d_ref[...] = -ADAM_LR * ((nm / c1) / (jnp.sqrt(nv / c2) + ADAM_EPS) + ADAM_WD * w_ref[...])
        nm_ref[...], nv_ref[...] = nm, nv

    spec = pl.BlockSpec((tr, cols), lambda i: (i, 0))
    outs = pl.pallas_call(
        body, name=name, grid=(rows // tr,), in_specs=[spec] * 4, out_specs=[spec] * 3,
        out_shape=[jax.ShapeDtypeStruct((rows, cols), F32)] * 3,
        compiler_params=_params(dimension_semantics=("arbitrary",)),
    )(*(t.reshape(rows, cols) for t in (w, g, m, v)))
    return tuple(o.reshape(shape) for o in outs)


def _sum_tile(rows):
    for cand in (2048, 1024, 512, 256, 128):
        if rows > cand and rows % cand == 0:
            return cand
    return rows


def _pair_sum_call(g, got, core, name):
    rows, cols = g.shape[1:]
    tr = _sum_tile(rows)

    def body(core_ref, g_ref, got_ref, out_ref):
        del core_ref
        out_ref[...] = (g_ref[...].astype(F32) + got_ref[...].astype(F32)).astype(BF16)

    grid_spec = pltpu.PrefetchScalarGridSpec(
        num_scalar_prefetch=1, grid=(4, rows // tr),
        in_specs=[pl.BlockSpec((1, tr, cols), lambda ch, t, core_ref: (2 * ch + core_ref[0], t, 0)),
                  pl.BlockSpec((1, tr, cols), lambda ch, t, core_ref: (ch, t, 0))],
        out_specs=pl.BlockSpec((1, tr, cols), lambda ch, t, core_ref: (ch, t, 0)))
    return pl.pallas_call(
        body, name=name, grid_spec=grid_spec, out_shape=jax.ShapeDtypeStruct((4, rows, cols), BF16),
        compiler_params=_params(dimension_semantics=("arbitrary", "arbitrary")),
    )(jnp.asarray(core, jnp.int32).reshape(1), g, got)


def _sum_call(parts, out_dtype, name):
    rows, cols = parts[0][0].shape[1:]
    tr = _sum_tile(rows)
    index = jnp.stack([jnp.asarray(i, jnp.int32) for _, i in parts])

    def body(idx_ref, *refs):
        del idx_ref
        acc = refs[0][0].astype(F32)
        for r in refs[1:-1]:
            acc = acc + r[0].astype(F32)
        refs[-1][...] = acc.astype(out_dtype)

    grid_spec = pltpu.PrefetchScalarGridSpec(
        num_scalar_prefetch=1, grid=(rows // tr,),
        in_specs=[pl.BlockSpec((1, tr, cols), lambda t, idx, n=n: (idx[n], t, 0)) for n in range(len(parts))],
        out_specs=pl.BlockSpec((tr, cols), lambda t, idx: (t, 0)))
    return pl.pallas_call(
        body, name=name, grid_spec=grid_spec, out_shape=jax.ShapeDtypeStruct((rows, cols), out_dtype),
        compiler_params=_params(dimension_semantics=("arbitrary",)),
    )(index, *(a for a, _ in parts))


def _place():
    return lax.axis_index("x"), lax.axis_index("y"), lax.axis_index("c")


def _any_specs(n):
    return [pl.BlockSpec(memory_space=pl.ANY)] * n


def _all_gather_call(xs, name):
    n = len(xs)

    def body(*refs):
        x_refs, out_refs, (send_sems, recv_sems, local_sems) = refs[:n], refs[n:2 * n], refs[2 * n:]
        ax, ay, ac = _place()
        me, sibling = (ax, ay, ac), (ax, ay, 1 - ac)
        chips = [(1 - ax, ay), (ax, 1 - ay), (1 - ax, 1 - ay)]

        def copy(a, k, block, to, src=None):
            slot = out_refs[a].at[4 * block[0] + 2 * block[1] + block[2]]
            return pltpu.make_async_remote_copy(
                src_ref=slot if src is None else src, dst_ref=slot,
                send_sem=send_sems.at[7 * a + k], recv_sem=recv_sems.at[7 * a + k], device_id=to, device_id_type=MESH)

        local = [pltpu.make_async_copy(x_refs[a], out_refs[a].at[4 * ax + 2 * ay + ac], local_sems.at[a]) for a in range(n)]
        for cp in local:
            cp.start()
        first = []
        for a in range(n):
            first.append(copy(a, 0, me, sibling, src=x_refs[a]))
            first += [copy(a, 1 + j, me, (*chip, ac), src=x_refs[a]) for j, chip in enumerate(chips)]
        for cp in first:
            cp.start()
        passed = []
        for j, chip in enumerate(chips):
            for a in range(n):
                copy(a, 1 + j, (*chip, ac), me).wait_recv()
                passed.append(copy(a, 4 + j, (*chip, ac), sibling))
                passed[-1].start()
        for a in range(n):
            copy(a, 0, sibling, me).wait_recv()
            for j, chip in enumerate(chips):
                copy(a, 4 + j, (*chip, 1 - ac), me).wait_recv()
        for cp in first + passed:
            cp.wait_send()
        for cp in local:
            cp.wait()

    return pl.pallas_call(
        body, name=name, in_specs=_any_specs(n), out_specs=_any_specs(n),
        out_shape=[jax.ShapeDtypeStruct((NDEV,) + x.shape, x.dtype) for x in xs],
        scratch_shapes=[pltpu.SemaphoreType.DMA((7 * n,)), pltpu.SemaphoreType.DMA((7 * n,)), pltpu.SemaphoreType.DMA((n,))],
    )(*xs)


def _swap_sibling_call(xs, name):
    n = len(xs)

    def body(*refs):
        x_refs, out_refs, (send_sems, recv_sems) = refs[:n], refs[n:2 * n], refs[2 * n:]
        ax, ay, ac = _place()
        for a in range(n):
            for chip in range(4):
                pltpu.make_async_remote_copy(src_ref=x_refs[a].at[2 * chip + 1 - ac], dst_ref=out_refs[a].at[chip],
                                             send_sem=send_sems.at[a], recv_sem=recv_sems.at[a],
                                             device_id=(ax, ay, 1 - ac), device_id_type=MESH).start()
        for a in range(n):
            four = x_refs[a].at[pl.ds(0, 4)]
            pltpu.make_async_remote_copy(src_ref=four, dst_ref=out_refs[a], send_sem=send_sems.at[a], recv_sem=recv_sems.at[a],
                                         device_id=(ax, ay, 1 - ac), device_id_type=MESH).wait()

    return pl.pallas_call(
        body, name=name, in_specs=_any_specs(n), out_specs=_any_specs(n),
        out_shape=[jax.ShapeDtypeStruct((4,) + x.shape[1:], x.dtype) for x in xs],
        scratch_shapes=[pltpu.SemaphoreType.DMA((n,)), pltpu.SemaphoreType.DMA((n,))],
    )(*xs)


def _swap_chips_call(xs, name):
    n = len(xs)

    def body(*refs):
        x_refs, out_refs, (send_sems, recv_sems) = refs[:n], refs[n:2 * n], refs[2 * n:]
        ax, ay, ac = _place()
        chips = [(1 - ax, ay), (ax, 1 - ay), (1 - ax, 1 - ay)]
        copies = [pltpu.make_async_remote_copy(src_ref=x_refs[a].at[2 * cx + cy], dst_ref=out_refs[a].at[j],
                                               send_sem=send_sems.at[3 * a + j], recv_sem=recv_sems.at[3 * a + j],
                                               device_id=(cx, cy, ac), device_id_type=MESH)
                  for a in range(n) for j, (cx, cy) in enumerate(chips)]
        for cp in copies:
            cp.start()
        for cp in copies:
            cp.wait()

    return pl.pallas_call(
        body, name=name, in_specs=_any_specs(n), out_specs=_any_specs(n),
        out_shape=[jax.ShapeDtypeStruct((3,) + x.shape[1:], x.dtype) for x in xs],
        scratch_shapes=[pltpu.SemaphoreType.DMA((3 * n,)), pltpu.SemaphoreType.DMA((3 * n,))],
    )(*xs)


def _reduce_scatter(gs, name):
    ax, ay, ac = _place()
    me, my_chip = 4 * ax + 2 * ay + ac, 2 * ax + ay
    got = _swap_sibling_call(gs, name + "_d2d")
    chip_sums = [_pair_sum_call(g, t, ac, f"{name}_pair{a}") for a, (g, t) in enumerate(zip(gs, got))]
    from_chips = _swap_chips_call(chip_sums, name + "_ici")
    return [_sum_call([(g, me), (t, my_chip), (f, 0), (f, 1), (f, 2)], F32, f"{name}_total{a}")
            for a, (g, t, f) in enumerate(zip(gs, got, from_chips))]


SMALL = ("norm1_g", "conv_w", "a_log", "dt_bias", "dn_out_g", "sb_q_g", "sb_k_g", "sg_v_g", "sg_w", "sg_b", "norm2_g")
WEIGHTS = ("norm1_g", "w_in", "conv_w", "a_log", "dt_bias", "dn_out_g", "sb_q_g", "sb_k_g", "sg_v_g", "sg_w", "sg_b",
           "w_out", "norm2_g", "w_ff1", "w_ff2")
SMALL_SHAPE = {"norm1_g": (D,), "conv_w": (4, 3 * DN_W), "a_log": (NH,), "dt_bias": (NH,), "dn_out_g": (128,), "sb_q_g": (64,),
               "sb_k_g": (64,), "sg_v_g": (SG_W,), "sg_w": (NH, 128, 128), "sg_b": (NH, 128), "norm2_g": (D,)}


def _size(shape):
    n = 1
    for s in shape:
        n *= s
    return n


def _to_rows(flat, multiple):
    pad = (-flat.shape[0]) % (LANES * multiple)
    return jnp.pad(flat, (0, pad)).reshape(-1, LANES)


def _conv_by_pair(conv):
    return conv.reshape(4, 3, 2, 256).transpose(0, 2, 1, 3).reshape(4, 3 * DN_W)


def kernel(x, norm1_g, w_in, conv_w, a_log, dt_bias, dn_out_g, sb_q_g, sb_k_g, sg_v_g, sg_w, sg_b, w_out, norm2_g, w_ff1, w_ff2, loss_target, m_norm1_g, m_w_in, m_conv_w, m_a_log, m_dt_bias, m_dn_out_g, m_sb_q_g, m_sb_k_g, m_sg_v_g, m_sg_w, m_sg_b, m_w_out, m_norm2_g, m_w_ff1, m_w_ff2, v_norm1_g, v_w_in, v_conv_w, v_a_log, v_dt_bias, v_dn_out_g, v_sb_q_g, v_sb_k_g, v_sg_v_g, v_sg_w, v_sg_b, v_w_out, v_norm2_g, v_w_ff1, v_w_ff2):
    given = dict(norm1_g=norm1_g, w_in=w_in, conv_w=conv_w, a_log=a_log, dt_bias=dt_bias, dn_out_g=dn_out_g, sb_q_g=sb_q_g,
                 sb_k_g=sb_k_g, sg_v_g=sg_v_g, sg_w=sg_w, sg_b=sg_b, w_out=w_out, norm2_g=norm2_g, w_ff1=w_ff1, w_ff2=w_ff2)
    mom = dict(norm1_g=m_norm1_g, w_in=m_w_in, conv_w=m_conv_w, a_log=m_a_log, dt_bias=m_dt_bias, dn_out_g=m_dn_out_g,
               sb_q_g=m_sb_q_g, sb_k_g=m_sb_k_g, sg_v_g=m_sg_v_g, sg_w=m_sg_w, sg_b=m_sg_b, w_out=m_w_out, norm2_g=m_norm2_g,
               w_ff1=m_w_ff1, w_ff2=m_w_ff2)
    var = dict(norm1_g=v_norm1_g, w_in=v_w_in, conv_w=v_conv_w, a_log=v_a_log, dt_bias=v_dt_bias, dn_out_g=v_dn_out_g,
               sb_q_g=v_sb_q_g, sb_k_g=v_sb_k_g, sg_v_g=v_sg_v_g, sg_w=v_sg_w, sg_b=v_sg_b, w_out=v_w_out, norm2_g=v_norm2_g,
               w_ff1=v_w_ff1, w_ff2=v_w_ff2)
    B, T, _ = x.shape
    M = B * T
    ax, ay, ac = _place()
    me = 4 * ax + 2 * ay + ac
    table_fwd, table_back = _row_tables()

    send = []
    for l in range(2):
        w_in_t = jnp.pad(w_in[l].T, ((0, IN_SHARD_PAD - IN_SHARD), (0, 0)))
        send += [w_in_t.astype(BF16), w_out[l].astype(BF16), w_ff1[l].astype(BF16), w_ff2[l].astype(BF16)]
    send.append(_to_rows(conv_w.reshape(-1), 8))
    gathered = _all_gather_call(send, "gather_weights")
    conv_full = gathered[8].reshape(NDEV, -1)[:, :conv_w.size].reshape(NDEV, 2, 4, -1).transpose(1, 2, 0, 3).reshape(2, 4, 3 * DN_W)

    pad_vec = lambda v: jnp.zeros((1, LANES), F32).at[0, :v.shape[0]].set(v)
    layer = []
    for l in range(2):
        g_in, g_out, g_ff1, g_ff2 = gathered[4 * l:4 * l + 4]
        layer.append(dict(
            g1=norm1_g[l].reshape(1, D), g2=norm2_g[l].reshape(1, D), conv=_conv_by_pair(conv_full[l]),
            a_log=pad_vec(a_log[l]), dt_bias=pad_vec(dt_bias[l]), dn_g=dn_out_g[l].reshape(1, LANES),
            sb_qg=jnp.tile(sb_q_g[l], 2).reshape(1, LANES), sb_kg=jnp.tile(sb_k_g[l], 2).reshape(1, LANES),
            sg_g=sg_v_g[l].reshape(1, SG_W), sg_w=sg_w[l], sg_bias=jnp.repeat(sg_b[l].T, 64, axis=1),
            wt=_row_perm_call(g_in.reshape(NDEV * IN_SHARD_PAD, D), table_fwd, "pack_w_in"),
            w_out=g_out.reshape(D, D), w1=g_ff1, w2=g_ff2.reshape(DFF, D)))

    cur = x.reshape(M, D)
    saved = []
    for p in layer:
        p_dn, p_sb, p_sg, p_ab, h = _in_proj_call(cur, p["g1"], p["wt"])
        mix = _dn_fwd_call(p_dn, p_ab, p["conv"], p["a_log"], p["dt_bias"], p["dn_g"], B, T)
        mix = _sb_fwd_call(p_sb, mix, p["sb_qg"], p["sb_kg"], B, T)
        mix = _sg_fwd_call(p_sg, mix, p["sg_g"], p["sg_w"], p["sg_bias"], B, T)
        x1 = _out_proj_call(mix, p["w_out"], cur)
        x2 = _ffn_fwd_call(x1, p["g2"], p["w1"], p["w2"])
        saved.append(dict(x0=cur, p_dn=p_dn, p_sb=p_sb, p_sg=p_sg, p_ab=p_ab, h=h, mix=mix, x1=x1))
        cur = x2
    loss_part, dy = _loss_call(cur, loss_target.reshape(M, D))
    loss = lax.psum(loss_part[0, 0], ("x", "y", "c"))

    big_grads = [None] * 8
    small_grads = {n: [None, None] for n in SMALL}
    for l in (1, 0):
        p, s = layer[l], saved[l]
        dx1, da, r, h2, dg2 = _ffn_bwd_call(s["x1"], dy, p["g2"], p["w1"], p["w2"])
        big_grads[4 * l + 2] = _mm_tn_call(h2, da, "grad_w_ff1", col_shards=True)
        big_grads[4 * l + 3] = _mm_tn_call(r, dy, "grad_w_ff2").reshape(NDEV, FF_SHARD, D)
        dmix = _mm_nt_call(dx1, p["w_out"], "dmix")
        big_grads[4 * l + 1] = _mm_tn_call(s["mix"], dx1, "grad_w_out").reshape(NDEV, D // NDEV, D)
        d_dn, d_ab, dcw, dalog, ddtb, ddn_g = _dn_bwd_call(s["p_dn"], s["p_ab"], dmix, p["conv"], p["a_log"], p["dt_bias"], p["dn_g"], B, T)
        d_sb, dqg, dkg = _sb_bwd_call(s["p_sb"], dmix, p["sb_qg"], p["sb_kg"], B, T)
        d_sg, dsg_g, dsg_w, dsg_b = _sg_bwd_call(s["p_sg"], dmix, p["sg_g"], p["sg_w"], p["sg_bias"], B, T)
        dsections = (d_dn, d_sb, d_sg, d_ab)
        dy, dg1 = _in_proj_bwd_call(dsections, p["wt"], s["x0"], p["g1"], dx1)
        dwt = _in_proj_grad_call(dsections, s["h"])
        big_grads[4 * l] = _row_perm_call(dwt, table_back, "unpack_grad_w_in").reshape(NDEV, IN_SHARD_PAD, D)
        for n, val in (("norm1_g", dg1[0]), ("conv_w", dcw.transpose(1, 0, 2).reshape(4, 3 * DN_W)), ("a_log", dalog[0, :NH]),
                       ("dt_bias", ddtb[0, :NH]), ("dn_out_g", ddn_g[0]), ("sb_q_g", dqg[0, :64]), ("sb_k_g", dkg[0, :64]),
                       ("sg_v_g", dsg_g[0]), ("sg_w", dsg_w), ("sg_b", dsg_b[:, :NH].T), ("norm2_g", dg2[0])):
            small_grads[n][l] = val
    grad_x = dy.reshape(B, T, D)

    mine = _reduce_scatter(big_grads, "reduce_grads")
    grads = {"w_in": jnp.stack([mine[4 * l][:IN_SHARD].T for l in range(2)]), "w_out": jnp.stack([mine[1], mine[5]]),
             "w_ff1": jnp.stack([mine[2], mine[6]]), "w_ff2": jnp.stack([mine[3], mine[7]])}
    small_flat = jnp.concatenate([jnp.stack(small_grads[n]).reshape(-1) for n in SMALL])
    everyone, = _all_gather_call([_to_rows(small_flat, 8)], "gather_small_grads")
    small_sum = _sum_call([(everyone, k) for k in range(NDEV)], F32, "sum_small_grads").reshape(-1)
    off = 0
    for n in SMALL:
        sz = 2 * _size(SMALL_SHAPE[n])
        grads[n] = small_sum[off:off + sz].reshape((2,) + SMALL_SHAPE[n])
        off += sz
    cshard = conv_w.shape[-1]
    grads["conv_w"] = lax.dynamic_slice_in_dim(grads["conv_w"], me * cshard, cshard, axis=2)

    deltas, new_m, new_v = {}, {}, {}
    for n in WEIGHTS:
        deltas[n], new_m[n], new_v[n] = _adamw_call(given[n], grads[n], mom[n], var[n], "adamw_" + n)
    return (loss, grad_x, *[grads[n] for n in WEIGHTS], *[deltas[n] for n in WEIGHTS], *[new_m[n] for n in WEIGHTS],
            *[new_v[n] for n in WEIGHTS])
```

```python
import functools

import numpy as np

import jax
import jax.numpy as jnp
from jax import lax
from jax.experimental import pallas as pl
from jax.experimental.pallas import tpu as pltpu

F32, BF16 = jnp.float32, jnp.bfloat16
EPS = 1e-6
LANES = 128
D = 1024
DFF = 4096
NH = 4
DN_W, SB_W, SG_W = 512, 256, 256
IN_DIM = 3336
NDEV = 8
IN_SHARD = IN_DIM // NDEV
IN_SHARD_PAD = 432
FF_SHARD = DFF // NDEV
DN_OFF, SB_OFF, SG_OFF, AB_OFF, NPACK = 0, 2048, 2816, 3328, 3456
SECTIONS = ((DN_OFF, 2048), (SB_OFF, 768), (SG_OFF, 512), (AB_OFF, 128))
SB_SCALE = 64 ** -0.5
DN_SCALE = 128 ** -0.5
VMEM_LIMIT = 56 * 1024 * 1024
VMEM_LIMIT_MAX = 62 * 1024 * 1024
ADAM_LR, ADAM_B1, ADAM_B2, ADAM_EPS, ADAM_WD, ADAM_STEP = 0.001, 0.9, 0.999, 1e-08, 0.01, 10
MESH = pl.DeviceIdType.MESH


def _iota(shape, dim):
    return lax.broadcasted_iota(jnp.int32, shape, dim)


def _params(**kw):
    return pltpu.CompilerParams(vmem_limit_bytes=VMEM_LIMIT, **kw)


NN, NT, TN = ((1,), (0,)), ((1,), (1,)), ((0,), (0,))


def _mm(a, b, dims):
    return lax.dot_general(a.astype(BF16), b.astype(BF16), (dims, ((), ())), preferred_element_type=F32)


def _plain(a, b, dims):
    return (a.T if dims == TN else a), (b.T if dims == NT else b)


def _mmx(a, b, dims):
    return _mm(*_plain(a, b, dims), NN)


@jax.custom_vjp
def _dot(a, b):
    return _mmx(a, b, NN)


def _dot_fwd(a, b):
    return _dot(a, b), (a, b)


def _dot_bwd(res, g):
    a, b = res
    return _mmx(g, b, NT).astype(a.dtype), _mmx(a, g, TN).astype(b.dtype)


_dot.defvjp(_dot_fwd, _dot_bwd)


@jax.custom_vjp
def _dot_nt(a, b):
    return _mmx(a, b, NT)


def _dot_nt_fwd(a, b):
    return _dot_nt(a, b), (a, b)


def _dot_nt_bwd(res, g):
    a, b = res
    return _mmx(g, b, NN).astype(a.dtype), _mmx(g, a, TN).astype(b.dtype)


_dot_nt.defvjp(_dot_nt_fwd, _dot_nt_bwd)


@jax.custom_vjp
def _dot_tn(a, b):
    return _mmx(a, b, TN)


def _dot_tn_fwd(a, b):
    return _dot_tn(a, b), (a, b)


def _dot_tn_bwd(res, g):
    a, b = res
    return _mmx(b, g, NT).astype(a.dtype), _mmx(a, g, NN).astype(b.dtype)


_dot_tn.defvjp(_dot_tn_fwd, _dot_tn_bwd)


def _split(x):
    hi = x.astype(BF16)
    return hi, (x - hi.astype(F32)).astype(BF16)


def _mm3(a, b, dims):
    a, b = _plain(a, b, dims)
    (ah, al), (bh, bl) = _split(a), _split(b)
    mm = lambda x, y: jnp.dot(x, y, preferred_element_type=F32)
    return mm(ah, bh) + (mm(ah, bl) + mm(al, bh))


def _mm_ones(ones, x, ones_left):
    hi, lo = _split(x)
    mm = (lambda t: jnp.dot(ones, t, preferred_element_type=F32)) if ones_left else \
         (lambda t: jnp.dot(t, ones, preferred_element_type=F32))
    return mm(hi) + mm(lo)


def _pair_ones(kind, transposed):
    row, col = _iota((128, 128), 0), _iota((128, 128), 1)
    m = (row // 64) == (col // 64)
    if kind == "running":
        m = jnp.logical_and(m, (col >= row) if transposed else (col <= row))
    return jnp.where(m, 1.0, 0.0).astype(BF16)


@functools.partial(jax.custom_vjp, nondiff_argnums=(0,))
def _chunk_sum(kind, x):
    return _mm_ones(_pair_ones(kind, False), x, True)


def _chunk_sum_fwd(kind, x):
    return _chunk_sum(kind, x), None


def _chunk_sum_bwd(kind, _, g):
    return (_mm_ones(_pair_ones(kind, True), g, True),)


_chunk_sum.defvjp(_chunk_sum_fwd, _chunk_sum_bwd)


def _tri_ones(n, transposed):
    row, col = _iota((n, n), 0), _iota((n, n), 1)
    return jnp.where((row < col) if transposed else (row > col), 1.0, 0.0).astype(BF16)


@jax.custom_vjp
def _suffix_sum(x):
    return _mm_ones(_tri_ones(x.shape[1], False), x, False)


def _suffix_sum_fwd(x):
    return _suffix_sum(x), None


def _suffix_sum_bwd(_, g):
    return (_mm_ones(_tri_ones(g.shape[1], True), g, False),)


_suffix_sum.defvjp(_suffix_sum_fwd, _suffix_sum_bwd)


def _sigmoid(x):
    return jax.nn.sigmoid(x)


def _silu(x):
    return x * _sigmoid(x)


def _softplus(x):
    return jnp.maximum(x, 0.0) + jnp.log1p(jnp.exp(-jnp.abs(x)))


def _gelu(x):
    return 0.5 * x * (1.0 + jnp.tanh(0.7978845608028654 * (x + 0.044715 * (x * x * x))))


def _rms(x, gain):
    return x * lax.rsqrt(jnp.mean(x * x, axis=-1, keepdims=True) + EPS) * gain


def _shift_down_impl(x, k):
    return jnp.where(_iota(x.shape, 0) >= k, pltpu.roll(x, k, 0), 0.0)


def _shift_up_impl(x, k):
    n = x.shape[0]
    return jnp.where(_iota(x.shape, 0) < n - k, pltpu.roll(x, n - k, 0), 0.0)


@functools.partial(jax.custom_vjp, nondiff_argnums=(1,))
def _shift_down(x, k):
    return _shift_down_impl(x, k)


def _shift_down_fwd(x, k):
    return _shift_down_impl(x, k), None


def _shift_down_bwd(k, _, g):
    return (_shift_up_impl(g, k),)


_shift_down.defvjp(_shift_down_fwd, _shift_down_bwd)


def _lane_pick(x, idx):
    return jnp.sum(jnp.where(_iota(x.shape, 1) == idx, x, 0.0), axis=-1, keepdims=True)


def _dn_conv(x, w0, w1, w2, w3, l2_scale):
    y = _silu(w3 * x + w2 * _shift_down(x, 1) + w1 * _shift_down(x, 2) + w0 * _shift_down(x, 3))
    if l2_scale is None:
        return y
    return y * lax.rsqrt(jnp.sum(y * y, axis=-1, keepdims=True) + EPS) * l2_scale


def _dn_gate(a, b, a_log, dt_bias):
    return -jnp.exp(a_log) * _softplus(a + dt_bias), _sigmoid(b)


def _unit_lower_inverse(lower):
    n = lower.shape[0]
    nk = -lower
    inv = jnp.where(_iota((n, n), 0) == _iota((n, n), 1), 1.0, 0.0) + nk
    for _ in range(5):
        nk = _mm(nk, nk, NN)
        inv = inv + _mm(inv, nk, NN)
    return inv


@jax.custom_vjp
def _solve_with(lower, inv, rhs):
    return _mm3(inv, rhs, ((1,), (0,)))


def _solve_with_fwd(lower, inv, rhs):
    x = _mm3(inv, rhs, ((1,), (0,)))
    return x, (inv, x)


def _solve_with_bwd(res, g):
    inv, x = res
    d_rhs = _mm3(inv, g, ((0,), (0,)))
    return -_mm3(d_rhs, x, ((1,), (1,))), jnp.zeros_like(inv), d_rhs


_solve_with.defvjp(_solve_with_fwd, _solve_with_bwd)


def _dn_local(q, k, v, g, beta, inv=None):
    n = 128
    row, col = _iota((n, n), 0), _iota((n, n), 1)
    same = (row // 64) == (col // 64)
    tri_incl = jnp.logical_and(same, col <= row)
    tri_strict = jnp.logical_and(same, col < row)
    first = row < 64
    gb = jnp.broadcast_to(g, (n, n))
    gc = _chunk_sum("running", gb)
    gl = _chunk_sum("total", gb)
    diff = gc - gc.T
    decay = jnp.where(tri_incl, jnp.exp(jnp.where(tri_incl, diff, 0.0)), 0.0)
    egc = jnp.exp(gc)
    kk = _dot_nt(k, k)
    lower = jnp.where(tri_strict, beta * kk * decay, 0.0)
    if inv is None:
        inv = _unit_lower_inverse(lower)
    u_val = _solve_with(lower, inv, v * beta)
    w_dec = _solve_with(lower, inv, k * (beta * egc))
    qk = jnp.where(tri_incl, _dot_nt(q, k) * decay, 0.0)
    q_dec = q * egc
    k_dec = k * jnp.exp(gl - gc)
    cd1 = jnp.exp(jnp.sum(jnp.where(first, gb, 0.0), axis=0, keepdims=True))
    cd2 = jnp.exp(jnp.sum(jnp.where(first, 0.0, gb), axis=0, keepdims=True))
    return (u_val, w_dec, qk, q_dec, k_dec, cd1, cd2), inv


def _dn_state(u_val, w_dec, qk, q_dec, k_dec, cd1, cd2, s0):
    first = _iota((128, 128), 0) < 64
    u1 = u_val - _dot(w_dec, s0)
    s1 = s0 * cd1 + _dot_tn(jnp.where(first, k_dec, 0.0), u1)
    u2 = u_val - _dot(w_dec, s1)
    u_new = jnp.where(first, u1, u2)
    s2 = s1 * cd2 + _dot_tn(jnp.where(first, 0.0, k_dec), u_new)
    o = jnp.where(first, _dot(q_dec, s0), _dot(q_dec, s1)) + _dot(qk, u_new)
    return o, s2


def _dn_post(o, z, gain):
    return _rms(o, gain) * _silu(z)


def _dn_gate_in(ab_ref, alog_ref, dtb_ref, h):
    ab = ab_ref[...]
    return _lane_pick(ab, h), _lane_pick(ab, h + NH), _lane_pick(alog_ref[...], h), _lane_pick(dtb_ref[...], h)


_DN_L2 = (DN_SCALE, 1.0, None)
DN_HPS = 2
DN_BLK = 4 * DN_HPS * LANES
_DN_COLS = tuple(slice(i * LANES, (i + 1) * LANES) for i in range(DN_HPS))


def _dn_in_cols(s, i):
    return slice((s * DN_HPS + i) * LANES, (s * DN_HPS + i + 1) * LANES)


def _dn_taps(cw_ref, s, i):
    return tuple(cw_ref[t:t + 1, _dn_in_cols(s, i)] for t in range(4))


def _dn_pack_gate(vals):
    lane = _iota((1, LANES), 1)
    out = 0.0
    for i, (g, beta) in enumerate(vals):
        out = out + jnp.where(lane == 2 * i, g, 0.0) + jnp.where(lane == 2 * i + 1, beta, 0.0)
    return out


def _pair_rows(n):
    return pl.ds(pl.multiple_of(n * 128, 128), 128)


def _dn_in_specs(T):
    one = pl.Buffered(1)
    vec = pl.BlockSpec((1, LANES), lambda b, h: (0, 0))
    return [pl.BlockSpec((T, DN_BLK), lambda b, h: (b, h), pipeline_mode=one),
            pl.BlockSpec((T, LANES), lambda b, h: (b, 0), pipeline_mode=one),
            pl.BlockSpec((4, 3 * DN_HPS * LANES), lambda b, h: (0, h)), vec, vec, vec]


def _dn_fwd_call(proj_dn, proj_ab, conv_w, a_log, dt_bias, gain, B, T, gather=()):
    npair = T // 128
    ng = len(gather)
    nsteps = B * (NH // DN_HPS)

    def body(*refs):
        x_ref, ab_ref, cw_ref, alog_ref, dtb_ref, gain_ref = refs[:6]
        out_ref = refs[6 + ng]
        q_s, k_s, v_s, o_s, gate_s = refs[7 + 2 * ng:12 + 2 * ng]
        step_id = pl.program_id(0) * (NH // DN_HPS) + pl.program_id(1)
        if ng:
            send, forward, finish = _gather_phases(refs[6:6 + ng], refs[7 + ng:7 + 2 * ng], *refs[12 + 2 * ng:])
            pl.when(step_id == 0)(send)
            pl.when(step_id == nsteps // 2)(forward)
        hp = pl.program_id(1)
        gates = []
        for i, cs in enumerate(_DN_COLS):
            for s, (x_s, l2) in enumerate(zip((q_s, k_s, v_s), _DN_L2)):
                x_s[:, cs] = _dn_conv(x_ref[:, _dn_in_cols(s, i)], *_dn_taps(cw_ref, s, i), l2)
            gates.append(_dn_gate(*_dn_gate_in(ab_ref, alog_ref, dtb_ref, DN_HPS * hp + i)))
        gate_s[...] = _dn_pack_gate(gates)

        def local_of(pair):
            r = _pair_rows(pair)
            gate = gate_s[r, :]
            return tuple(_dn_local(q_s[r, cs], k_s[r, cs], v_s[r, cs], _lane_pick(gate, 2 * i), _lane_pick(gate, 2 * i + 1))[0]
                         for i, cs in enumerate(_DN_COLS))

        def state_of(n, locs, states):
            new = []
            for i, cs in enumerate(_DN_COLS):
                o, s2 = _dn_state(*locs[i], states[i])
                o_s[_pair_rows(n), cs] = o
                new.append(s2)
            return tuple(new)

        def step(n, carry):
            locs, states = carry
            return local_of(n + 1), state_of(n, locs, states)

        locs, states = lax.fori_loop(0, npair - 1, step, (local_of(0), (jnp.zeros((128, 128), F32),) * DN_HPS))
        state_of(npair - 1, locs, states)
        for i, cs in enumerate(_DN_COLS):
            out_ref[:, cs] = _dn_post(o_s[:, cs], x_ref[:, _dn_in_cols(3, i)], gain_ref[...])
        if ng:
            pl.when(step_id == nsteps - 1)(finish)

    wide = [pltpu.VMEM((T, DN_HPS * LANES), F32)]
    outs = pl.pallas_call(
        body, name="dn_fwd", grid=(B, NH // DN_HPS), in_specs=_dn_in_specs(T) + _any_specs(ng),
        out_specs=[pl.BlockSpec((T, DN_HPS * LANES), lambda b, h: (b, h), pipeline_mode=pl.Buffered(1))] + _any_specs(ng),
        out_shape=[jax.ShapeDtypeStruct((B * T, D), F32)] + _gather_shapes(gather),
        scratch_shapes=wide * 4 + [pltpu.VMEM((T, LANES), F32)] + (_gather_sems(ng) if ng else []),
        compiler_params=_params(dimension_semantics=("arbitrary", "arbitrary")),
    )(proj_dn, proj_ab, conv_w, a_log, dt_bias, gain, *gather)
    return outs[0], outs[1:]


def _dn_bwd_call(proj_dn, proj_ab, dmix, conv_w, a_log, dt_bias, gain, B, T, swap=()):
    npair = T // 128
    ns = len(swap)
    nsteps = B * (NH // DN_HPS)

    def body(*refs):
        x_ref, ab_ref, cw_ref, alog_ref, dtb_ref, gain_ref, do_ref = refs[:7]
        dx_ref, dab_ref, dcw_ref, dalog_ref, ddtb_ref, dgain_ref = refs[7 + ns:13 + ns]
        q_s, k_s, v_s, o_s, gate_s, dgate_s, st_s, inv_s, dcd_s = refs[13 + 2 * ns:22 + 2 * ns]
        dloc_s = refs[22 + 2 * ns:27 + 2 * ns]
        b_i, hp = pl.program_id(0), pl.program_id(1)
        step_id = b_i * (NH // DN_HPS) + hp
        if ns:
            send, finish = _chip_swap_phases(refs[7:7 + ns], refs[13 + ns:13 + 2 * ns], *refs[27 + 2 * ns:])
            pl.when(step_id == 0)(send)
        gates = []
        for i, cs in enumerate(_DN_COLS):
            for s, (x_s, l2) in enumerate(zip((q_s, k_s, v_s), _DN_L2)):
                x_s[:, cs] = _dn_conv(x_ref[:, _dn_in_cols(s, i)], *_dn_taps(cw_ref, s, i), l2)
            gates.append(_dn_gate(*_dn_gate_in(ab_ref, alog_ref, dtb_ref, DN_HPS * hp + i)))
        gate_s[...] = _dn_pack_gate(gates)

        def pair_in(r, i, cs, gate):
            return q_s[r, cs], k_s[r, cs], v_s[r, cs], _lane_pick(gate, 2 * i), _lane_pick(gate, 2 * i + 1)

        def local_of(pair, known_inverse=False):
            r = _pair_rows(pair)
            gate = gate_s[r, :]
            locs = []
            for i, cs in enumerate(_DN_COLS):
                loc, inv = _dn_local(*pair_in(r, i, cs, gate), inv_s[pair, i] if known_inverse else None)
                if not known_inverse:
                    inv_s[pair, i] = inv
                locs.append(loc)
            return tuple(locs)

        def state_of(n, locs, states):
            new = []
            for i, cs in enumerate(_DN_COLS):
                st_s[n, i] = states[i]
                o, s2 = _dn_state(*locs[i], states[i])
                o_s[_pair_rows(n), cs] = o
                new.append(s2)
            return tuple(new)

        def step(n, carry):
            locs, states = carry
            return local_of(n + 1), state_of(n, locs, states)

        zero_states = (jnp.zeros((128, 128), F32),) * DN_HPS
        locs, states = lax.fori_loop(0, npair - 1, step, (local_of(0), zero_states))
        state_of(npair - 1, locs, states)

        @pl.when(jnp.logical_and(b_i == 0, hp == 0))
        def _():
            dcw_ref[...] = jnp.zeros_like(dcw_ref)
            dalog_ref[...] = jnp.zeros_like(dalog_ref)
            ddtb_ref[...] = jnp.zeros_like(ddtb_ref)
            dgain_ref[...] = jnp.zeros_like(dgain_ref)

        for i, cs in enumerate(_DN_COLS):
            zc = _dn_in_cols(3, i)
            _, post_vjp = jax.vjp(_dn_post, o_s[:, cs], x_ref[:, zc], gain_ref[...])
            do, dz, dgain = post_vjp(do_ref[:, cs])
            dx_ref[:, zc] = dz
            o_s[:, cs] = do
            dgain_ref[...] += dgain

        def state_back(nn, dstates):
            n = npair - 1 - nn
            r = _pair_rows(n)
            locs = local_of(n, known_inverse=True)
            new = []
            for i, cs in enumerate(_DN_COLS):
                _, state_vjp = jax.vjp(_dn_state, *locs[i], st_s[n, i])
                *dloc, ds0 = state_vjp((o_s[r, cs], dstates[i]))
                for d_s, val in zip(dloc_s, dloc[:5]):
                    d_s[r, cs] = val
                dcd_s[n, i, 0:1, :], dcd_s[n, i, 1:2, :] = dloc[5], dloc[6]
                new.append(ds0)
            return tuple(new)

        lax.fori_loop(0, npair, state_back, zero_states)

        def local_back(m, _):
            for pair in (2 * m, 2 * m + 1):
                r = _pair_rows(pair)
                gate = gate_s[r, :]
                dgates = []
                for i, cs in enumerate(_DN_COLS):
                    inv = inv_s[pair, i]
                    local = lambda q, k, v, g, beta, inv=inv: _dn_local(q, k, v, g, beta, inv)[0]
                    _, local_vjp = jax.vjp(local, *pair_in(r, i, cs, gate))
                    dloc = tuple(d_s[r, cs] for d_s in dloc_s) + (dcd_s[pair, i, 0:1, :], dcd_s[pair, i, 1:2, :])
                    dq, dk, dv, dg, db = local_vjp(dloc)
                    dx_ref[r, _dn_in_cols(0, i)], dx_ref[r, _dn_in_cols(1, i)], dx_ref[r, _dn_in_cols(2, i)] = dq, dk, dv
                    dgates.append((dg, db))
                dgate_s[r, :] = _dn_pack_gate(dgates)
            return 0

        lax.fori_loop(0, npair // 2, local_back, 0)

        lane = _iota((1, LANES), 1)
        dab = 0.0
        for i, cs in enumerate(_DN_COLS):
            h = DN_HPS * hp + i
            for s, l2 in enumerate(_DN_L2):
                xc = _dn_in_cols(s, i)
                _, conv_vjp = jax.vjp(functools.partial(_dn_conv, l2_scale=l2), x_ref[:, xc], *_dn_taps(cw_ref, s, i))
                dx, *dw = conv_vjp(dx_ref[:, xc])
                dx_ref[:, xc] = dx
                for t in range(4):
                    dcw_ref[h + 4 * s, t:t + 1, :] += dw[t]
            _, gate_vjp = jax.vjp(_dn_gate, *_dn_gate_in(ab_ref, alog_ref, dtb_ref, h))
            dgate = dgate_s[...]
            da, db, dalog, ddtb = gate_vjp((_lane_pick(dgate, 2 * i), _lane_pick(dgate, 2 * i + 1)))
            dab = dab + jnp.where(lane == h, da, 0.0) + jnp.where(lane == h + NH, db, 0.0)
            dalog_ref[...] += jnp.where(lane == h, dalog, 0.0)
            ddtb_ref[...] += jnp.where(lane == h, ddtb, 0.0)

        @pl.when(hp == 0)
        def _():
            dab_ref[...] = jnp.zeros_like(dab_ref)

        dab_ref[...] += dab
        if ns:
            pl.when(step_id == nsteps - 1)(finish)

    M = B * T
    one = pl.Buffered(1)
    vec = pl.BlockSpec((1, LANES), lambda b, h: (0, 0))
    wide = [pltpu.VMEM((T, DN_HPS * LANES), F32)]
    narrow = [pltpu.VMEM((T, LANES), F32)]
    vec_shape = jax.ShapeDtypeStruct((1, LANES), F32)
    outs = pl.pallas_call(
        body, name="dn_bwd", grid=(B, NH // DN_HPS),
        in_specs=_dn_in_specs(T) + [pl.BlockSpec((T, DN_HPS * LANES), lambda b, h: (b, h), pipeline_mode=one)] + _any_specs(ns),
        out_specs=[pl.BlockSpec((T, DN_BLK), lambda b, h: (b, h), pipeline_mode=one), pl.BlockSpec((T, LANES), lambda b, h: (b, 0)),
                   pl.BlockSpec((12, 4, LANES), lambda b, h: (0, 0, 0)), vec, vec, vec] + _any_specs(ns),
        out_shape=[jax.ShapeDtypeStruct((M, 4 * DN_W), F32), jax.ShapeDtypeStruct((M, LANES), F32),
                   jax.ShapeDtypeStruct((12, 4, LANES), F32), vec_shape, vec_shape, vec_shape] + _chip_swap_shapes(swap),
        scratch_shapes=wide * 4 + narrow * 2 + [pltpu.VMEM((npair, DN_HPS, 128, 128), F32)] * 2
        + [pltpu.VMEM((npair, DN_HPS, 8, LANES), F32)] + wide * 5 + (_chip_swap_sems(ns) if ns else []),
        compiler_params=pltpu.CompilerParams(vmem_limit_bytes=VMEM_LIMIT_MAX, dimension_semantics=("arbitrary", "arbitrary")),
    )(proj_dn, proj_ab, conv_w, a_log, dt_bias, gain, dmix, *swap)
    return outs[:6], outs[6:]


SBQ = 256


def _group_rms(x, gain):
    first = _iota(x.shape, 1) < 64
    sq = x * x
    ss_a = jnp.sum(jnp.where(first, sq, 0.0), axis=-1, keepdims=True)
    ss_b = jnp.sum(jnp.where(first, 0.0, sq), axis=-1, keepdims=True)
    ms = jnp.where(first, ss_a, ss_b) * (1.0 / 64)
    return x * lax.rsqrt(ms + EPS) * gain


def _sb_stack(q):
    first = _iota((1, LANES), 1) < 64
    return jnp.concatenate([jnp.where(first, q, 0.0), jnp.where(first, 0.0, q)], axis=0)


def _sb_fold(acc):
    return jnp.where(_iota((1, LANES), 1) < 64, acc[:SBQ], acc[SBQ:])


def _sb_logs(q2, k, diag):
    n = SBQ
    z = _mm(q2, k, ((1,), (1,))) * SB_SCALE
    ls_pos = jnp.minimum(z, 0.0) - jnp.log1p(jnp.exp(-jnp.abs(z)))
    l1m = ls_pos - z
    if not diag:
        return ls_pos, l1m, None
    mask = _iota((2 * n, n), 1) < jnp.bitwise_and(_iota((2 * n, n), 0), n - 1)
    return ls_pos, jnp.where(mask, l1m, 0.0), mask


def _sb_weights(ls_pos, l1m, mask, carry):
    w = jnp.exp(ls_pos + (_mm_ones(_tri_ones(SBQ, False), l1m, False) + carry))
    return w if mask is None else jnp.where(mask, w, 0.0)


def _sb_block(q, k, v, carry, diag):
    ls_pos, l1m, mask = _sb_logs(_sb_stack(q), k, diag)
    w = _sb_weights(ls_pos, l1m, mask, carry)
    return _mm(w, v, ((1,), (0,))), carry + jnp.sum(l1m, axis=-1, keepdims=True)


def _sb_rowsum(q, k, diag):
    return jnp.sum(_sb_logs(_sb_stack(q), k, diag)[1], axis=-1, keepdims=True)


def _sb_block_bwd(q, k, v, carry, diag, dpv, dcarry):
    q2 = _sb_stack(q)
    ls_pos, l1m, mask = _sb_logs(q2, k, diag)
    w = _sb_weights(ls_pos, l1m, mask, carry)
    dv = _mm(w, dpv, ((0,), (0,)))
    de = _mm(dpv, v, ((1,), (1,))) * w
    dl1m = _mm_ones(_tri_ones(SBQ, True), de, False) + dcarry
    if mask is not None:
        dl1m = jnp.where(mask, dl1m, 0.0)
    sig = jnp.exp(ls_pos)
    dz = (de * (1.0 - sig) - dl1m * sig) * SB_SCALE
    dq = _sb_fold(_mm(dz, k, ((1,), (0,))))
    return dq, _mm(dz, q2, ((0,), (0,))), dv, dcarry + jnp.sum(de, axis=-1, keepdims=True)


_SB_Q, _SB_K, _SB_V = (slice(i * LANES, (i + 1) * LANES) for i in range(3))


def _sb_fwd_call(proj_sb, mix, q_gain, k_gain, B, T):
    nblk = T // SBQ

    def body(x_ref, qg_ref, kg_ref, mix_ref, out_ref, q_s, k_s):
        del mix_ref
        q_s[...] = _group_rms(x_ref[:, _SB_Q], qg_ref[...])
        k_s[...] = _group_rms(x_ref[:, _SB_K], kg_ref[...])

        def qblock(i, _):
            ri = pl.ds(pl.multiple_of(i * SBQ, SBQ), SBQ)
            q = q_s[ri, :]

            def kblock(jj, c):
                rj = pl.ds(pl.multiple_of((i - 1 - jj) * SBQ, SBQ), SBQ)
                pv, carry = _sb_block(q, k_s[rj, :], x_ref[rj, _SB_V], c[1], False)
                return c[0] + pv, carry

            on_diag = _sb_block(q, k_s[ri, :], x_ref[ri, _SB_V], jnp.zeros((2 * SBQ, 1), F32), True)
            acc, _c = lax.fori_loop(0, i, kblock, on_diag)
            out_ref[ri, :] = _sb_fold(acc)
            return 0

        lax.fori_loop(0, nblk, qblock, 0)

    vec = pl.BlockSpec((1, LANES), lambda b, p: (0, 0))
    return pl.pallas_call(
        body, name="sb_fwd", grid=(B, 2),
        in_specs=[pl.BlockSpec((T, 3 * LANES), lambda b, p: (b, p)), vec, vec, pl.BlockSpec(memory_space=pl.ANY)],
        out_specs=pl.BlockSpec((T, LANES), lambda b, p: (b, DN_W // LANES + p)),
        out_shape=jax.ShapeDtypeStruct((B * T, D), F32), input_output_aliases={3: 0},
        scratch_shapes=[pltpu.VMEM((T, LANES), F32)] * 2,
        compiler_params=_params(dimension_semantics=("arbitrary", "arbitrary")),
    )(proj_sb, q_gain, k_gain, mix)


def _sb_bwd_call(proj_sb, dmix, q_gain, k_gain, B, T):
    nblk = T // SBQ

    def body(x_ref, qg_ref, kg_ref, do_ref, dx_ref, dqg_ref, dkg_ref, q_s, k_s, dq_s, dk_s, dv_s, c_s):
        b_i, p = pl.program_id(0), pl.program_id(1)
        qn, q_vjp = jax.vjp(_group_rms, x_ref[:, _SB_Q], qg_ref[...])
        kn, k_vjp = jax.vjp(_group_rms, x_ref[:, _SB_K], kg_ref[...])
        q_s[...], k_s[...] = qn, kn
        dk_s[...] = jnp.zeros_like(dk_s)
        dv_s[...] = jnp.zeros_like(dv_s)

        def qblock(i, _):
            ri = pl.ds(pl.multiple_of(i * SBQ, SBQ), SBQ)
            q = q_s[ri, :]
            dacc = _sb_stack(do_ref[ri, :])

            def carries(jj, carry):
                j = i - 1 - jj
                rj = pl.ds(pl.multiple_of(j * SBQ, SBQ), SBQ)
                c_s[j] = carry
                return carry + _sb_rowsum(q, k_s[rj, :], False)

            lax.fori_loop(0, i, carries, _sb_rowsum(q, k_s[ri, :], True))

            def kblock(j, c):
                rj = pl.ds(pl.multiple_of(j * SBQ, SBQ), SBQ)
                dq_j, dk_j, dv_j, dc = _sb_block_bwd(q, k_s[rj, :], x_ref[rj, _SB_V], c_s[j], False, dacc, c[1])
                dk_s[rj, :] += dk_j
                dv_s[rj, :] += dv_j
                return c[0] + dq_j, dc

            dq, dc = lax.fori_loop(0, i, kblock, (jnp.zeros((SBQ, LANES), F32), jnp.zeros((2 * SBQ, 1), F32)))
            dq_i, dk_i, dv_i, _dc = _sb_block_bwd(q, k_s[ri, :], x_ref[ri, _SB_V], jnp.zeros((2 * SBQ, 1), F32), True, dacc, dc)
            dk_s[ri, :] += dk_i
            dv_s[ri, :] += dv_i
            dq_s[ri, :] = dq + dq_i
            return 0

        lax.fori_loop(0, nblk, qblock, 0)
        dq_in, dqg = q_vjp(dq_s[...])
        dk_in, dkg = k_vjp(dk_s[...])
        dx_ref[:, _SB_Q], dx_ref[:, _SB_K], dx_ref[:, _SB_V] = dq_in, dk_in, dv_s[...]

        @pl.when(jnp.logical_and(b_i == 0, p == 0))
        def _():
            dqg_ref[...] = jnp.zeros_like(dqg_ref)
            dkg_ref[...] = jnp.zeros_like(dkg_ref)

        dqg_ref[...] += dqg + pltpu.roll(dqg, 64, 1)
        dkg_ref[...] += dkg + pltpu.roll(dkg, 64, 1)

    M = B * T
    vec = pl.BlockSpec((1, LANES), lambda b, p: (0, 0))
    blk = pl.BlockSpec((T, 3 * LANES), lambda b, p: (b, p))
    big = [pltpu.VMEM((T, LANES), F32)]
    return pl.pallas_call(
        body, name="sb_bwd", grid=(B, 2),
        in_specs=[blk, vec, vec, pl.BlockSpec((T, LANES), lambda b, p: (b, DN_W // LANES + p))],
        out_specs=[blk, vec, vec],
        out_shape=[jax.ShapeDtypeStruct((M, 3 * SB_W), F32)] + [jax.ShapeDtypeStruct((1, LANES), F32)] * 2,
        scratch_shapes=big * 5 + [pltpu.VMEM((nblk, 2 * SBQ, 1), F32)],
        compiler_params=_params(dimension_semantics=("arbitrary", "arbitrary")),
    )(proj_sb, q_gain, k_gain, dmix)


def _sg_chunk(u, v, gain, w_a, w_b, bias):
    n = 128
    row, col = _iota((n, n), 0), _iota((n, n), 1)
    first = _iota((1, LANES), 1) < 64
    vn = _group_rms(_gelu(v), gain)
    tril = col <= row
    mixed = jnp.where(first, _dot(jnp.where(tril, w_a, 0.0), vn), _dot(jnp.where(tril, w_b, 0.0), vn)) + bias
    return _gelu(u) * mixed


_SG_U, _SG_V = slice(0, LANES), slice(LANES, 2 * LANES)


def _sg_fwd_call(proj_sg, mix, gain, sg_w, bias, B, T):
    nchunk = T // 128

    def body(x_ref, g_ref, wa_ref, wb_ref, bias_ref, mix_ref, out_ref):
        del mix_ref

        def step(i, _):
            r = pl.ds(pl.multiple_of(i * 128, 128), 128)
            out_ref[r, :] = _sg_chunk(x_ref[r, _SG_U], x_ref[r, _SG_V], g_ref[...], wa_ref[0], wb_ref[0], bias_ref[...])
            return 0

        lax.fori_loop(0, nchunk, step, 0)

    return pl.pallas_call(
        body, name="sg_fwd", grid=(B, 2),
        in_specs=[pl.BlockSpec((T, 2 * LANES), lambda b, p: (b, p)), pl.BlockSpec((1, LANES), lambda b, p: (0, p)),
                  pl.BlockSpec((1, 128, 128), lambda b, p: (2 * p, 0, 0)), pl.BlockSpec((1, 128, 128), lambda b, p: (2 * p + 1, 0, 0)),
                  pl.BlockSpec((128, LANES), lambda b, p: (0, p)), pl.BlockSpec(memory_space=pl.ANY)],
        out_specs=pl.BlockSpec((T, LANES), lambda b, p: (b, (DN_W + SB_W) // LANES + p)),
        out_shape=jax.ShapeDtypeStruct((B * T, D), F32), input_output_aliases={5: 0},
        compiler_params=_params(dimension_semantics=("arbitrary", "arbitrary")),
    )(proj_sg, gain, sg_w, sg_w, bias, mix)


def _sg_bwd_call(proj_sg, dmix, gain, sg_w, bias, B, T):
    nchunk = T // 128

    def body(x_ref, g_ref, wa_ref, wb_ref, bias_ref, do_ref, dx_ref, dg_ref, dw_ref, db_ref):
        p, b_i = pl.program_id(0), pl.program_id(1)

        def step(i, c):
            r = pl.ds(pl.multiple_of(i * 128, 128), 128)
            _, vjp = jax.vjp(_sg_chunk, x_ref[r, _SG_U], x_ref[r, _SG_V], g_ref[...], wa_ref[0], wb_ref[0], bias_ref[...])
            du, dv, dg, dwa, dwb, dbias = vjp(do_ref[r, :])
            dx_ref[r, _SG_U], dx_ref[r, _SG_V] = du, dv
            return c[0] + dg, c[1] + dwa, c[2] + dwb, c[3] + dbias

        z = jnp.zeros((128, 128), F32)
        dg, dwa, dwb, dbias = lax.fori_loop(0, nchunk, step, (jnp.zeros((1, LANES), F32), z, z, z))
        lane = _iota((1, LANES), 1)
        first = lane < 64
        s_a = jnp.sum(jnp.where(first, dbias, 0.0), axis=-1, keepdims=True)
        s_b = jnp.sum(jnp.where(first, 0.0, dbias), axis=-1, keepdims=True)
        dbg = jnp.where(lane == 2 * p, s_a, 0.0) + jnp.where(lane == 2 * p + 1, s_b, 0.0)

        @pl.when(b_i == 0)
        def _():
            dg_ref[...] = jnp.zeros_like(dg_ref)
            dw_ref[...] = jnp.zeros_like(dw_ref)

        @pl.when(jnp.logical_and(b_i == 0, p == 0))
        def _():
            db_ref[...] = jnp.zeros_like(db_ref)

        dg_ref[...] += dg
        dw_ref[0] += dwa
        dw_ref[1] += dwb
        db_ref[...] += dbg

    M = B * T
    blk = pl.BlockSpec((T, 2 * LANES), lambda p, b: (b, p))
    return pl.pallas_call(
        body, name="sg_bwd", grid=(2, B),
        in_specs=[blk, pl.BlockSpec((1, LANES), lambda p, b: (0, p)),
                  pl.BlockSpec((1, 128, 128), lambda p, b: (2 * p, 0, 0)), pl.BlockSpec((1, 128, 128), lambda p, b: (2 * p + 1, 0, 0)),
                  pl.BlockSpec((128, LANES), lambda p, b: (0, p)),
                  pl.BlockSpec((T, LANES), lambda p, b: (b, (DN_W + SB_W) // LANES + p))],
        out_specs=[blk, pl.BlockSpec((1, LANES), lambda p, b: (0, p)), pl.BlockSpec((2, 128, 128), lambda p, b: (p, 0, 0)),
                   pl.BlockSpec((128, LANES), lambda p, b: (0, 0))],
        out_shape=[jax.ShapeDtypeStruct((M, 2 * SG_W), F32), jax.ShapeDtypeStruct((1, SG_W), F32),
                   jax.ShapeDtypeStruct((4, 128, 128), F32), jax.ShapeDtypeStruct((128, LANES), F32)],
        compiler_params=_params(dimension_semantics=("arbitrary", "arbitrary")),
    )(proj_sg, gain, sg_w, sg_w, bias, dmix)


def _row_tile(m):
    return min(m, 512)


def _in_proj_call(x, gain, wt):
    m = x.shape[0]
    tm = _row_tile(m)

    def body(x_ref, g_ref, wt_ref, *out_refs):
        h = _rms(x_ref[...], g_ref[...]).astype(BF16)
        out_refs[-1][...] = h
        for (off, width), out_ref in zip(SECTIONS, out_refs):
            out_ref[...] = lax.dot_general(h, wt_ref[off:off + width, :], (((1,), (1,)), ((), ())), preferred_element_type=F32)

    rows = lambda width: pl.BlockSpec((tm, width), lambda i: (i, 0))
    return pl.pallas_call(
        body, name="in_proj", grid=(m // tm,),
        in_specs=[rows(D), pl.BlockSpec((1, D), lambda i: (0, 0)),
                  pl.BlockSpec((NPACK, D), lambda i: (0, 0), pipeline_mode=pl.Buffered(1))],
        out_specs=[rows(w) for _, w in SECTIONS] + [rows(D)],
        out_shape=[jax.ShapeDtypeStruct((m, w), F32) for _, w in SECTIONS] + [jax.ShapeDtypeStruct((m, D), BF16)],
        compiler_params=_params(dimension_semantics=("arbitrary",)),
    )(x, gain, wt)


def _in_proj_bwd_call(dsections, wt, x, gain, dres):
    m = x.shape[0]
    tm = _row_tile(m)

    def body(*refs):
        ds_refs, (wt_ref, x_ref, g_ref, dres_ref, dx_ref, dg_ref) = refs[:len(SECTIONS)], refs[len(SECTIONS):]

        @pl.when(pl.program_id(0) == 0)
        def _():
            dg_ref[...] = jnp.zeros_like(dg_ref)

        dh = 0.0
        for (off, width), ds_ref in zip(SECTIONS, ds_refs):
            dh = dh + jnp.dot(ds_ref[...].astype(BF16), wt_ref[off:off + width, :], preferred_element_type=F32)
        _, vjp = jax.vjp(_rms, x_ref[...], g_ref[...])
        dx, dg = vjp(dh)
        dx_ref[...] = dres_ref[...] + dx
        dg_ref[...] += dg

    rows = lambda width: pl.BlockSpec((tm, width), lambda i: (i, 0))
    return pl.pallas_call(
        body, name="in_proj_bwd", grid=(m // tm,),
        in_specs=[rows(w) for _, w in SECTIONS] + [pl.BlockSpec((NPACK, D), lambda i: (0, 0), pipeline_mode=pl.Buffered(1)),
                                                   rows(D), pl.BlockSpec((1, D), lambda i: (0, 0)), rows(D)],
        out_specs=[rows(D), pl.BlockSpec((1, D), lambda i: (0, 0))],
        out_shape=[jax.ShapeDtypeStruct((m, D), F32), jax.ShapeDtypeStruct((1, D), F32)],
        compiler_params=_params(dimension_semantics=("arbitrary",)),
    )(*dsections, wt, x, gain, dres)


def _in_proj_grad_call(dsections, h):
    m = h.shape[0]
    tm = min(m, 256)

    def body(*refs):
        ds_refs, (h_ref, out_ref) = refs[:len(SECTIONS)], refs[len(SECTIONS):]

        @pl.when(pl.program_id(0) == 0)
        def _():
            out_ref[...] = jnp.zeros_like(out_ref)

        for (off, width), ds_ref in zip(SECTIONS, ds_refs):
            out_ref[off:off + width, :] += lax.dot_general(ds_ref[...].astype(BF16), h_ref[...], (((0,), (0,)), ((), ())),
                                                           preferred_element_type=F32)

    rows = lambda width: pl.BlockSpec((tm, width), lambda i: (i, 0))
    return pl.pallas_call(
        body, name="grad_w_in", grid=(m // tm,),
        in_specs=[rows(w) for _, w in SECTIONS] + [rows(D)],
        out_specs=pl.BlockSpec((NPACK, D), lambda i: (0, 0), pipeline_mode=pl.Buffered(1)),
        out_shape=jax.ShapeDtypeStruct((NPACK, D), F32),
        compiler_params=_params(dimension_semantics=("arbitrary",)),
    )(*dsections, h)


def _packed_column_of():
    t = np.full(NPACK, -1, np.int64)
    lanes = np.arange(LANES)
    for pair in range(2):
        for s in range(4):
            t[DN_OFF + pair * 1024 + s * 256 + np.arange(256)] = s * DN_W + pair * 256 + np.arange(256)
        for s in range(3):
            t[SB_OFF + pair * 384 + s * LANES + lanes] = 2056 + s * SB_W + pair * LANES + lanes
        for s in range(2):
            t[SG_OFF + pair * 256 + s * LANES + lanes] = 2056 + 3 * SB_W + s * SG_W + pair * LANES + lanes
    t[AB_OFF + np.arange(2 * NH)] = 4 * DN_W + np.arange(2 * NH)
    return t


def _row_tables():
    col = _packed_column_of()
    fwd = np.where(col >= 0, (col // IN_SHARD) * IN_SHARD_PAD + col % IN_SHARD, -1)
    packed_of = np.full(IN_DIM, -1, np.int64)
    packed_of[col[col >= 0]] = np.nonzero(col >= 0)[0]
    r = np.arange(NDEV * IN_SHARD_PAD)
    inside = r % IN_SHARD_PAD < IN_SHARD
    back = np.where(inside, packed_of[np.minimum((r // IN_SHARD_PAD) * IN_SHARD + r % IN_SHARD_PAD, IN_DIM - 1)], -1)
    return fwd, back


def _row_perm_call(src, table, name):
    n_out = table.shape[0]
    touched = [sorted(set((table[b * 128:(b + 1) * 128][table[b * 128:(b + 1) * 128] >= 0] // 128).tolist()))
               for b in range(n_out // 128)]

    def body(tbl_ref, src_ref, out_ref):
        lane = _iota((1, LANES), 1)
        for b, blocks in enumerate(touched):
            want = tbl_ref[b * 128:(b + 1) * 128, :]
            acc = jnp.zeros((128, D), F32)
            for sb in blocks:
                pick = jnp.where(want == sb * 128 + lane, 1.0, 0.0).astype(BF16)
                acc = acc + jnp.dot(pick, src_ref[sb * 128:(sb + 1) * 128, :].astype(BF16), preferred_element_type=F32)
            out_ref[b * 128:(b + 1) * 128, :] = acc.astype(BF16)

    return pl.pallas_call(
        body, name=name, out_shape=jax.ShapeDtypeStruct((n_out, D), BF16),
        in_specs=[pl.BlockSpec(memory_space=pltpu.VMEM)] * 2, out_specs=pl.BlockSpec(memory_space=pltpu.VMEM),
        compiler_params=_params(),
    )(jnp.asarray(table.reshape(-1, 1), jnp.int32), src)


def _out_proj_call(a, w, res):
    m, k = a.shape
    n = w.shape[1]
    tm = _row_tile(m)

    def body(a_ref, w_ref, res_ref, out_ref):
        out_ref[...] = res_ref[...] + jnp.dot(a_ref[...].astype(BF16), w_ref[...], preferred_element_type=F32)

    return pl.pallas_call(
        body, name="out_proj", grid=(m // tm,),
        in_specs=[pl.BlockSpec((tm, k), lambda i: (i, 0)), pl.BlockSpec((k, n), lambda i: (0, 0)),
                  pl.BlockSpec((tm, n), lambda i: (i, 0))],
        out_specs=pl.BlockSpec((tm, n), lambda i: (i, 0)),
        out_shape=jax.ShapeDtypeStruct((m, n), F32),
        compiler_params=_params(dimension_semantics=("arbitrary",)),
    )(a, w, res)


def _ffn_specs(tm):
    return [pl.BlockSpec((1, D, FF_SHARD), lambda i, j: (j, 0, 0)), pl.BlockSpec((FF_SHARD, D), lambda i, j: (j, 0))]


def _ffn_fwd_call(x, gain, w1, w2):
    m = x.shape[0]
    tm = _row_tile(m)

    def body(x_ref, g_ref, w1_ref, w2_ref, out_ref, h_s, acc_s):
        j = pl.program_id(1)

        @pl.when(j == 0)
        def _():
            h_s[...] = _rms(x_ref[...], g_ref[...]).astype(BF16)
            acc_s[...] = jnp.zeros_like(acc_s)

        a = jnp.maximum(jnp.dot(h_s[...], w1_ref[0], preferred_element_type=F32), 0.0)
        acc_s[...] += jnp.dot((a * a).astype(BF16), w2_ref[...], preferred_element_type=F32)

        @pl.when(j == NDEV - 1)
        def _():
            out_ref[...] = x_ref[...] + acc_s[...]

    return pl.pallas_call(
        body, name="ffn_fwd", grid=(m // tm, NDEV),
        in_specs=[pl.BlockSpec((tm, D), lambda i, j: (i, 0)), pl.BlockSpec((1, D), lambda i, j: (0, 0))] + _ffn_specs(tm),
        out_specs=pl.BlockSpec((tm, D), lambda i, j: (i, 0)),
        out_shape=jax.ShapeDtypeStruct((m, D), F32),
        scratch_shapes=[pltpu.VMEM((tm, D), BF16), pltpu.VMEM((tm, D), F32)],
        compiler_params=_params(dimension_semantics=("arbitrary", "arbitrary")),
    )(x, gain, w1, w2)


def _ffn_bwd_call(x, dy, gain, w1, w2):
    m = x.shape[0]
    tm = _row_tile(m)

    def body(x_ref, dy_ref, g_ref, w1_ref, w2_ref, dx_ref, da_ref, r_ref, h_ref, dg_ref, acc_s):
        i, j = pl.program_id(0), pl.program_id(1)

        @pl.when(j == 0)
        def _():
            h_ref[...] = _rms(x_ref[...], g_ref[...]).astype(BF16)
            acc_s[...] = jnp.zeros_like(acc_s)

        @pl.when(jnp.logical_and(i == 0, j == 0))
        def _():
            dg_ref[...] = jnp.zeros_like(dg_ref)

        a = jnp.maximum(jnp.dot(h_ref[...], w1_ref[0], preferred_element_type=F32), 0.0)
        r_ref[...] = (a * a).astype(BF16)
        dr = lax.dot_general(dy_ref[...].astype(BF16), w2_ref[...], (((1,), (1,)), ((), ())), preferred_element_type=F32)
        da = (dr * (2.0 * a)).astype(BF16)
        da_ref[...] = da
        acc_s[...] += lax.dot_general(da, w1_ref[0], (((1,), (1,)), ((), ())), preferred_element_type=F32)

        @pl.when(j == NDEV - 1)
        def _():
            _, vjp = jax.vjp(_rms, x_ref[...], g_ref[...])
            dx, dg = vjp(acc_s[...])
            dx_ref[...] = dy_ref[...] + dx
            dg_ref[...] += dg

    return pl.pallas_call(
        body, name="ffn_bwd", grid=(m // tm, NDEV),
        in_specs=[pl.BlockSpec((tm, D), lambda i, j: (i, 0)), pl.BlockSpec((tm, D), lambda i, j: (i, 0)),
                  pl.BlockSpec((1, D), lambda i, j: (0, 0))] + _ffn_specs(tm),
        out_specs=[pl.BlockSpec((tm, D), lambda i, j: (i, 0)), pl.BlockSpec((tm, FF_SHARD), lambda i, j: (i, j)),
                   pl.BlockSpec((tm, FF_SHARD), lambda i, j: (i, j)), pl.BlockSpec((tm, D), lambda i, j: (i, 0)),
                   pl.BlockSpec((1, D), lambda i, j: (0, 0))],
        out_shape=[jax.ShapeDtypeStruct((m, D), F32), jax.ShapeDtypeStruct((m, DFF), BF16), jax.ShapeDtypeStruct((m, DFF), BF16),
                   jax.ShapeDtypeStruct((m, D), BF16), jax.ShapeDtypeStruct((1, D), F32)],
        scratch_shapes=[pltpu.VMEM((tm, D), F32)],
        compiler_params=_params(dimension_semantics=("arbitrary", "arbitrary")),
    )(x, dy, gain, w1, w2)


def _mm_nt_call(a, b, name):
    m, k = a.shape
    n = b.shape[0]
    tm = _row_tile(m)

    def body(a_ref, b_ref, out_ref):
        out_ref[...] = lax.dot_general(a_ref[...].astype(BF16), b_ref[...].astype(BF16), (((1,), (1,)), ((), ())),
                                       preferred_element_type=F32)

    return pl.pallas_call(
        body, name=name, grid=(m // tm,),
        in_specs=[pl.BlockSpec((tm, k), lambda i: (i, 0)), pl.BlockSpec((n, k), lambda i: (0, 0))],
        out_specs=pl.BlockSpec((tm, n), lambda i: (i, 0)),
        out_shape=jax.ShapeDtypeStruct((m, n), F32),
        compiler_params=_params(dimension_semantics=("arbitrary",)),
    )(a, b)


def _mm_tn_call(a, b, name, col_shards=False):
    m, k = a.shape
    n = b.shape[1]
    tm, tk = _row_tile(m), min(k, 1024)
    tn = n // NDEV if col_shards else min(n, 1024)

    def body(a_ref, b_ref, out_ref, acc_s):
        s = pl.program_id(2)

        @pl.when(s == 0)
        def _():
            acc_s[...] = jnp.zeros_like(acc_s)

        acc_s[...] += lax.dot_general(a_ref[...].astype(BF16), b_ref[...].astype(BF16), (((0,), (0,)), ((), ())),
                                      preferred_element_type=F32)

        @pl.when(s == m // tm - 1)
        def _():
            out_ref[...] = acc_s[...].astype(BF16).reshape(out_ref.shape)

    if col_shards:
        out_spec, out_shape = pl.BlockSpec((1, tk, tn), lambda i, j, s: (j, i, 0)), (NDEV, k, tn)
    else:
        out_spec, out_shape = pl.BlockSpec((tk, tn), lambda i, j, s: (i, j)), (k, n)
    return pl.pallas_call(
        body, name=name, grid=(k // tk, n // tn, m // tm),
        in_specs=[pl.BlockSpec((tm, tk), lambda i, j, s: (s, i)), pl.BlockSpec((tm, tn), lambda i, j, s: (s, j))],
        out_specs=out_spec, out_shape=jax.ShapeDtypeStruct(out_shape, BF16),
        scratch_shapes=[pltpu.VMEM((tk, tn), F32)],
        compiler_params=_params(dimension_semantics=("arbitrary", "arbitrary", "arbitrary")),
    )(a, b)


def _loss_call(y, target):
    m = y.shape[0]
    tm = _row_tile(m)

    def body(y_ref, t_ref, loss_ref, dy_ref):
        @pl.when(pl.program_id(0) == 0)
        def _():
            loss_ref[...] = jnp.zeros_like(loss_ref)

        err = y_ref[...] - t_ref[...]
        dy_ref[...] = err * (1.0 / D)
        per_row = jnp.mean(err * err, axis=-1, keepdims=True)
        loss_ref[...] += jnp.broadcast_to(0.5 * jnp.sum(per_row, axis=0, keepdims=True), (1, LANES))

    return pl.pallas_call(
        body, name="loss", grid=(m // tm,),
        in_specs=[pl.BlockSpec((tm, D), lambda i: (i, 0))] * 2,
        out_specs=[pl.BlockSpec((1, LANES), lambda i: (0, 0)), pl.BlockSpec((tm, D), lambda i: (i, 0))],
        out_shape=[jax.ShapeDtypeStruct((1, LANES), F32), jax.ShapeDtypeStruct((m, D), F32)],
        compiler_params=_params(dimension_semantics=("arbitrary",)),
    )(y, target)


def _adamw_call(w, g, m, v, name):
    shape = w.shape
    cols = shape[-1] if w.ndim > 1 else w.size
    rows = w.size // cols
    tr = rows if (rows <= 512 or rows % 512) else 512
    c1, c2 = 1.0 - ADAM_B1 ** ADAM_STEP, 1.0 - ADAM_B2 ** ADAM_STEP

    def body(w_ref, g_ref, m_ref, v_ref, d_ref, nm_ref, nv_ref):
        g_ = g_ref[...]
        nm = ADAM_B1 * m_ref[...] + (1.0 - ADAM_B1) * g_
        nv = ADAM_B2 * v_ref[...] + (1.0 - ADAM_B2) * (g_ * g_)
        d_ref[...] = -ADAM_LR * ((nm / c1) / (jnp.sqrt(nv / c2) + ADAM_EPS) + ADAM_WD * w_ref[...])
        nm_ref[...], nv_ref[...] = nm, nv

    spec = pl.BlockSpec((tr, cols), lambda i: (i, 0))
    outs = pl.pallas_call(
        body, name=name, grid=(rows // tr,), in_specs=[spec] * 4, out_specs=[spec] * 3,
        out_shape=[jax.ShapeDtypeStruct((rows, cols), F32)] * 3,
        compiler_params=_params(dimension_semantics=("arbitrary",)),
    )(*(t.reshape(rows, cols) for t in (w, g, m, v)))
    return tuple(o.reshape(shape) for o in outs)


def _sum_tile(rows):
    for cand in (2048, 1024, 512, 256, 128):
        if rows > cand and rows % cand == 0:
            return cand
    return rows


def _pair_sum_call(g, got, core, name):
    rows, cols = g.shape[1:]
    tr = _sum_tile(rows)

    def body(core_ref, g_ref, got_ref, out_ref):
        del core_ref
        out_ref[...] = (g_ref[...].astype(F32) + got_ref[...].astype(F32)).astype(BF16)

    grid_spec = pltpu.PrefetchScalarGridSpec(
        num_scalar_prefetch=1, grid=(4, rows // tr),
        in_specs=[pl.BlockSpec((1, tr, cols), lambda ch, t, core_ref: (2 * ch + core_ref[0], t, 0)),
                  pl.BlockSpec((1, tr, cols), lambda ch, t, core_ref: (ch, t, 0))],
        out_specs=pl.BlockSpec((1, tr, cols), lambda ch, t, core_ref: (ch, t, 0)))
    return pl.pallas_call(
        body, name=name, grid_spec=grid_spec, out_shape=jax.ShapeDtypeStruct((4, rows, cols), BF16),
        compiler_params=_params(dimension_semantics=("arbitrary", "arbitrary")),
    )(jnp.asarray(core, jnp.int32).reshape(1), g, got)


def _sum_call(parts, out_dtype, name):
    rows, cols = parts[0][0].shape[1:]
    tr = _sum_tile(rows)
    index = jnp.stack([jnp.asarray(i, jnp.int32) for _, i in parts])

    def body(idx_ref, *refs):
        del idx_ref
        acc = refs[0][0].astype(F32)
        for r in refs[1:-1]:
            acc = acc + r[0].astype(F32)
        refs[-1][...] = acc.astype(out_dtype)

    grid_spec = pltpu.PrefetchScalarGridSpec(
        num_scalar_prefetch=1, grid=(rows // tr,),
        in_specs=[pl.BlockSpec((1, tr, cols), lambda t, idx, n=n: (idx[n], t, 0)) for n in range(len(parts))],
        out_specs=pl.BlockSpec((tr, cols), lambda t, idx: (t, 0)))
    return pl.pallas_call(
        body, name=name, grid_spec=grid_spec, out_shape=jax.ShapeDtypeStruct((rows, cols), out_dtype),
        compiler_params=_params(dimension_semantics=("arbitrary",)),
    )(index, *(a for a, _ in parts))


def _place():
    return lax.axis_index("x"), lax.axis_index("y"), lax.axis_index("c")


def _any_specs(n):
    return [pl.BlockSpec(memory_space=pl.ANY)] * n


def _all_gather_call(xs, name):
    n = len(xs)

    def body(*refs):
        for phase in _gather_phases(refs[:n], refs[n:2 * n], *refs[2 * n:]):
            phase()

    return pl.pallas_call(
        body, name=name, in_specs=_any_specs(n), out_specs=_any_specs(n),
        out_shape=_gather_shapes(xs), scratch_shapes=_gather_sems(n),
    )(*xs)


def _gather_shapes(xs):
    return [jax.ShapeDtypeStruct((NDEV,) + x.shape, x.dtype) for x in xs]


def _gather_sems(n):
    return [pltpu.SemaphoreType.DMA((7 * n,)), pltpu.SemaphoreType.DMA((7 * n,)), pltpu.SemaphoreType.DMA((n,))]


def _gather_phases(x_refs, out_refs, send_sems, recv_sems, local_sems):
    n = len(x_refs)
    ax, ay, ac = _place()
    me, sibling = (ax, ay, ac), (ax, ay, 1 - ac)
    chips = [(1 - ax, ay), (ax, 1 - ay), (1 - ax, 1 - ay)]

    def copy(a, k, block, to, src=None):
        slot = out_refs[a].at[4 * block[0] + 2 * block[1] + block[2]]
        return pltpu.make_async_remote_copy(
            src_ref=slot if src is None else src, dst_ref=slot,
            send_sem=send_sems.at[7 * a + k], recv_sem=recv_sems.at[7 * a + k], device_id=to, device_id_type=MESH)

    local = [pltpu.make_async_copy(x_refs[a], out_refs[a].at[4 * ax + 2 * ay + ac], local_sems.at[a]) for a in range(n)]
    first = []
    for a in range(n):
        first.append(copy(a, 0, me, sibling, src=x_refs[a]))
        first += [copy(a, 1 + j, me, (*chip, ac), src=x_refs[a]) for j, chip in enumerate(chips)]
    passed = [copy(a, 4 + j, (*chip, ac), sibling) for j, chip in enumerate(chips) for a in range(n)]

    def send():
        for cp in local + first:
            cp.start()

    def forward():
        for j, chip in enumerate(chips):
            for a in range(n):
                copy(a, 1 + j, (*chip, ac), me).wait_recv()
                passed[j * n + a].start()

    def finish():
        for a in range(n):
            copy(a, 0, sibling, me).wait_recv()
            for j, chip in enumerate(chips):
                copy(a, 4 + j, (*chip, 1 - ac), me).wait_recv()
        for cp in first + passed:
            cp.wait_send()
        for cp in local:
            cp.wait()

    return send, forward, finish


def _swap_sibling_call(xs, name):
    n = len(xs)

    def body(*refs):
        x_refs, out_refs, (send_sems, recv_sems) = refs[:n], refs[n:2 * n], refs[2 * n:]
        ax, ay, ac = _place()
        for a in range(n):
            for chip in range(4):
                pltpu.make_async_remote_copy(src_ref=x_refs[a].at[2 * chip + 1 - ac], dst_ref=out_refs[a].at[chip],
                                             send_sem=send_sems.at[a], recv_sem=recv_sems.at[a],
                                             device_id=(ax, ay, 1 - ac), device_id_type=MESH).start()
        for a in range(n):
            four = x_refs[a].at[pl.ds(0, 4)]
            pltpu.make_async_remote_copy(src_ref=four, dst_ref=out_refs[a], send_sem=send_sems.at[a], recv_sem=recv_sems.at[a],
                                         device_id=(ax, ay, 1 - ac), device_id_type=MESH).wait()

    return pl.pallas_call(
        body, name=name, in_specs=_any_specs(n), out_specs=_any_specs(n),
        out_shape=[jax.ShapeDtypeStruct((4,) + x.shape[1:], x.dtype) for x in xs],
        scratch_shapes=[pltpu.SemaphoreType.DMA((n,)), pltpu.SemaphoreType.DMA((n,))],
    )(*xs)


def _swap_chips_call(xs, name):
    n = len(xs)

    def body(*refs):
        for phase in _chip_swap_phases(refs[:n], refs[n:2 * n], *refs[2 * n:]):
            phase()

    return pl.pallas_call(
        body, name=name, in_specs=_any_specs(n), out_specs=_any_specs(n),
        out_shape=_chip_swap_shapes(xs), scratch_shapes=_chip_swap_sems(n),
    )(*xs)


def _chip_swap_shapes(xs):
    return [jax.ShapeDtypeStruct((3,) + x.shape[1:], x.dtype) for x in xs]


def _chip_swap_sems(n):
    return [pltpu.SemaphoreType.DMA((3 * n,)), pltpu.SemaphoreType.DMA((3 * n,))]


def _chip_swap_phases(x_refs, out_refs, send_sems, recv_sems):
    ax, ay, ac = _place()
    chips = [(1 - ax, ay), (ax, 1 - ay), (1 - ax, 1 - ay)]
    copies = [pltpu.make_async_remote_copy(src_ref=x_refs[a].at[2 * cx + cy], dst_ref=out_refs[a].at[j],
                                           send_sem=send_sems.at[3 * a + j], recv_sem=recv_sems.at[3 * a + j],
                                           device_id=(cx, cy, ac), device_id_type=MESH)
              for a in range(len(x_refs)) for j, (cx, cy) in enumerate(chips)]

    def send():
        for cp in copies:
            cp.start()

    def finish():
        for cp in copies:
            cp.wait()

    return send, finish


def _reduce_begin(gs, name):
    ac = lax.axis_index("c")
    got = _swap_sibling_call(gs, name + "_d2d")
    return got, [_pair_sum_call(g, t, ac, f"{name}_pair{a}") for a, (g, t) in enumerate(zip(gs, got))]


def _reduce_end(gs, got, from_chips, name):
    ax, ay, ac = _place()
    me, my_chip = 4 * ax + 2 * ay + ac, 2 * ax + ay
    return [_sum_call([(g, me), (t, my_chip), (f, 0), (f, 1), (f, 2)], F32, f"{name}_total{a}")
            for a, (g, t, f) in enumerate(zip(gs, got, from_chips))]


SMALL = ("norm1_g", "conv_w", "a_log", "dt_bias", "dn_out_g", "sb_q_g", "sb_k_g", "sg_v_g", "sg_w", "sg_b", "norm2_g")
WEIGHTS = ("norm1_g", "w_in", "conv_w", "a_log", "dt_bias", "dn_out_g", "sb_q_g", "sb_k_g", "sg_v_g", "sg_w", "sg_b",
           "w_out", "norm2_g", "w_ff1", "w_ff2")
SMALL_SHAPE = {"norm1_g": (D,), "conv_w": (4, 3 * DN_W), "a_log": (NH,), "dt_bias": (NH,), "dn_out_g": (128,), "sb_q_g": (64,),
               "sb_k_g": (64,), "sg_v_g": (SG_W,), "sg_w": (NH, 128, 128), "sg_b": (NH, 128), "norm2_g": (D,)}


def _size(shape):
    n = 1
    for s in shape:
        n *= s
    return n


def _to_rows(flat, multiple):
    pad = (-flat.shape[0]) % (LANES * multiple)
    return jnp.pad(flat, (0, pad)).reshape(-1, LANES)


def _conv_by_pair(conv):
    return conv.reshape(4, 3, 2, 256).transpose(0, 2, 1, 3).reshape(4, 3 * DN_W)


def kernel(x, norm1_g, w_in, conv_w, a_log, dt_bias, dn_out_g, sb_q_g, sb_k_g, sg_v_g, sg_w, sg_b, w_out, norm2_g, w_ff1, w_ff2, loss_target, m_norm1_g, m_w_in, m_conv_w, m_a_log, m_dt_bias, m_dn_out_g, m_sb_q_g, m_sb_k_g, m_sg_v_g, m_sg_w, m_sg_b, m_w_out, m_norm2_g, m_w_ff1, m_w_ff2, v_norm1_g, v_w_in, v_conv_w, v_a_log, v_dt_bias, v_dn_out_g, v_sb_q_g, v_sb_k_g, v_sg_v_g, v_sg_w, v_sg_b, v_w_out, v_norm2_g, v_w_ff1, v_w_ff2):
    given = dict(norm1_g=norm1_g, w_in=w_in, conv_w=conv_w, a_log=a_log, dt_bias=dt_bias, dn_out_g=dn_out_g, sb_q_g=sb_q_g,
                 sb_k_g=sb_k_g, sg_v_g=sg_v_g, sg_w=sg_w, sg_b=sg_b, w_out=w_out, norm2_g=norm2_g, w_ff1=w_ff1, w_ff2=w_ff2)
    mom = dict(norm1_g=m_norm1_g, w_in=m_w_in, conv_w=m_conv_w, a_log=m_a_log, dt_bias=m_dt_bias, dn_out_g=m_dn_out_g,
               sb_q_g=m_sb_q_g, sb_k_g=m_sb_k_g, sg_v_g=m_sg_v_g, sg_w=m_sg_w, sg_b=m_sg_b, w_out=m_w_out, norm2_g=m_norm2_g,
               w_ff1=m_w_ff1, w_ff2=m_w_ff2)
    var = dict(norm1_g=v_norm1_g, w_in=v_w_in, conv_w=v_conv_w, a_log=v_a_log, dt_bias=v_dt_bias, dn_out_g=v_dn_out_g,
               sb_q_g=v_sb_q_g, sb_k_g=v_sb_k_g, sg_v_g=v_sg_v_g, sg_w=v_sg_w, sg_b=v_sg_b, w_out=v_w_out, norm2_g=v_norm2_g,
               w_ff1=v_w_ff1, w_ff2=v_w_ff2)
    B, T, _ = x.shape
    M = B * T
    ax, ay, ac = _place()
    me = 4 * ax + 2 * ay + ac
    table_fwd, table_back = _row_tables()

    send = []
    for l in range(2):
        w_in_t = jnp.pad(w_in[l].T, ((0, IN_SHARD_PAD - IN_SHARD), (0, 0)))
        send.append([w_in_t.astype(BF16), w_out[l].astype(BF16), w_ff1[l].astype(BF16), w_ff2[l].astype(BF16)])
    first_in, conv_rows = _all_gather_call([send[0][0], _to_rows(conv_w.reshape(-1), 8)], "gather_first")
    conv_full = conv_rows.reshape(NDEV, -1)[:, :conv_w.size].reshape(NDEV, 2, 4, -1).transpose(1, 2, 0, 3).reshape(2, 4, 3 * DN_W)
    gathered = [[first_in, None, None, None], [None] * 4]

    pad_vec = lambda v: jnp.zeros((1, LANES), F32).at[0, :v.shape[0]].set(v)
    layer = []
    for l in range(2):
        layer.append(dict(
            g1=norm1_g[l].reshape(1, D), g2=norm2_g[l].reshape(1, D), conv=_conv_by_pair(conv_full[l]),
            a_log=pad_vec(a_log[l]), dt_bias=pad_vec(dt_bias[l]), dn_g=dn_out_g[l].reshape(1, LANES),
            sb_qg=jnp.tile(sb_q_g[l], 2).reshape(1, LANES), sb_kg=jnp.tile(sb_k_g[l], 2).reshape(1, LANES),
            sg_g=sg_v_g[l].reshape(1, SG_W), sg_w=sg_w[l], sg_bias=jnp.repeat(sg_b[l].T, 64, axis=1)))

    cur = x.reshape(M, D)
    saved = []
    for l, p in enumerate(layer):
        p["wt"] = _row_perm_call(gathered[l][0].reshape(NDEV * IN_SHARD_PAD, D), table_fwd, "pack_w_in")
        p_dn, p_sb, p_sg, p_ab, h = _in_proj_call(cur, p["g1"], p["wt"])
        later = send[0][1:] + send[1] if l == 0 else []
        mix, arrived = _dn_fwd_call(p_dn, p_ab, p["conv"], p["a_log"], p["dt_bias"], p["dn_g"], B, T, gather=later)
        if l == 0:
            gathered[0][1:], gathered[1] = list(arrived[:3]), list(arrived[3:])
        p["w_out"], p["w1"], p["w2"] = gathered[l][1].reshape(D, D), gathered[l][2], gathered[l][3].reshape(DFF, D)
        mix = _sb_fwd_call(p_sb, mix, p["sb_qg"], p["sb_kg"], B, T)
        mix = _sg_fwd_call(p_sg, mix, p["sg_g"], p["sg_w"], p["sg_bias"], B, T)
        x1 = _out_proj_call(mix, p["w_out"], cur)
        x2 = _ffn_fwd_call(x1, p["g2"], p["w1"], p["w2"])
        saved.append(dict(x0=cur, p_dn=p_dn, p_sb=p_sb, p_sg=p_sg, p_ab=p_ab, h=h, mix=mix, x1=x1))
        cur = x2
    loss_part, dy = _loss_call(cur, loss_target.reshape(M, D))
    loss = lax.psum(loss_part[0, 0], ("x", "y", "c"))

    big_grads = [[None] * 4, [None] * 4]
    small_grads = {n: [None, None] for n in SMALL}
    for l in (1, 0):
        p, s = layer[l], saved[l]
        dx1, da, r, h2, dg2 = _ffn_bwd_call(s["x1"], dy, p["g2"], p["w1"], p["w2"])
        big_grads[l][2] = _mm_tn_call(h2, da, "grad_w_ff1", col_shards=True)
        big_grads[l][3] = _mm_tn_call(r, dy, "grad_w_ff2").reshape(NDEV, FF_SHARD, D)
        dmix = _mm_nt_call(dx1, p["w_out"], "dmix")
        big_grads[l][1] = _mm_tn_call(s["mix"], dx1, "grad_w_out").reshape(NDEV, D // NDEV, D)
        if l == 0:
            early = big_grads[1] + big_grads[0][1:]
            early_got, early_sums = _reduce_begin(early, "reduce_early")
        (d_dn, d_ab, dcw, dalog, ddtb, ddn_g), early_from = _dn_bwd_call(
            s["p_dn"], s["p_ab"], dmix, p["conv"], p["a_log"], p["dt_bias"], p["dn_g"], B, T, swap=early_sums if l == 0 else ())
        d_sb, dqg, dkg = _sb_bwd_call(s["p_sb"], dmix, p["sb_qg"], p["sb_kg"], B, T)
        d_sg, dsg_g, dsg_w, dsg_b = _sg_bwd_call(s["p_sg"], dmix, p["sg_g"], p["sg_w"], p["sg_bias"], B, T)
        dsections = (d_dn, d_sb, d_sg, d_ab)
        dy, dg1 = _in_proj_bwd_call(dsections, p["wt"], s["x0"], p["g1"], dx1)
        dwt = _in_proj_grad_call(dsections, s["h"])
        big_grads[l][0] = _row_perm_call(dwt, table_back, "unpack_grad_w_in").reshape(NDEV, IN_SHARD_PAD, D)
        for n, val in (("norm1_g", dg1[0]), ("conv_w", dcw.transpose(1, 0, 2).reshape(4, 3 * DN_W)), ("a_log", dalog[0, :NH]),
                       ("dt_bias", ddtb[0, :NH]), ("dn_out_g", ddn_g[0]), ("sb_q_g", dqg[0, :64]), ("sb_k_g", dkg[0, :64]),
                       ("sg_v_g", dsg_g[0]), ("sg_w", dsg_w), ("sg_b", dsg_b[:, :NH].T), ("norm2_g", dg2[0])):
            small_grads[n][l] = val
    grad_x = dy.reshape(B, T, D)

    last = big_grads[0][:1]
    last_got, last_sums = _reduce_begin(last, "reduce_last")
    mine0 = _reduce_end(last, last_got, _swap_chips_call(last_sums, "reduce_last_ici"), "reduce_last")
    mine1 = _reduce_end(early, early_got, early_from, "reduce_early")
    grads = {"w_in": jnp.stack([mine0[0][:IN_SHARD].T, mine1[0][:IN_SHARD].T]), "w_out": jnp.stack([mine1[4], mine1[1]]),
             "w_ff1": jnp.stack([mine1[5], mine1[2]]), "w_ff2": jnp.stack([mine1[6], mine1[3]])}
    small_flat = jnp.concatenate([jnp.stack(small_grads[n]).reshape(-1) for n in SMALL])
    everyone, = _all_gather_call([_to_rows(small_flat, 8)], "gather_small_grads")
    small_sum = _sum_call([(everyone, k) for k in range(NDEV)], F32, "sum_small_grads").reshape(-1)
    off = 0
    for n in SMALL:
        sz = 2 * _size(SMALL_SHAPE[n])
        grads[n] = small_sum[off:off + sz].reshape((2,) + SMALL_SHAPE[n])
        off += sz
    cshard = conv_w.shape[-1]
    grads["conv_w"] = lax.dynamic_slice_in_dim(grads["conv_w"], me * cshard, cshard, axis=2)

    deltas, new_m, new_v = {}, {}, {}
    for n in WEIGHTS:
        deltas[n], new_m[n], new_v[n] = _adamw_call(given[n], grads[n], mom[n], var[n], "adamw_" + n)
    return (loss, grad_x, *[grads[n] for n in WEIGHTS], *[deltas[n] for n in WEIGHTS], *[new_m[n] for n in WEIGHTS],
            *[new_v[n] for n in WEIGHTS])
```

```python
import functools

import numpy as np

import jax
import jax.numpy as jnp
from jax import lax
from jax.experimental import pallas as pl
from jax.experimental.pallas import tpu as pltpu

F32, BF16 = jnp.float32, jnp.bfloat16
EPS = 1e-6
LANES = 128
D = 1024
DFF = 4096
NH = 4
DN_W, SB_W, SG_W = 512, 256, 256
IN_DIM = 3336
NDEV = 8
IN_SHARD = IN_DIM // NDEV
IN_SHARD_PAD = 432
FF_SHARD = DFF // NDEV
DN_OFF, SB_OFF, SG_OFF, AB_OFF, NPACK = 0, 2048, 2816, 3328, 3456
SECTIONS = ((DN_OFF, 2048), (SB_OFF, 768), (SG_OFF, 512), (AB_OFF, 128))
SB_SCALE = 64 ** -0.5
DN_SCALE = 128 ** -0.5
VMEM_LIMIT = 56 * 1024 * 1024
VMEM_LIMIT_MAX = 62 * 1024 * 1024
ADAM_LR, ADAM_B1, ADAM_B2, ADAM_EPS, ADAM_WD, ADAM_STEP = 0.001, 0.9, 0.999, 1e-08, 0.01, 10
MESH = pl.DeviceIdType.MESH


def _iota(shape, dim):
    return lax.broadcasted_iota(jnp.int32, shape, dim)


def _params(**kw):
    return pltpu.CompilerParams(vmem_limit_bytes=VMEM_LIMIT, **kw)


NN, NT, TN = ((1,), (0,)), ((1,), (1,)), ((0,), (0,))


def _mm(a, b, dims):
    return lax.dot_general(a.astype(BF16), b.astype(BF16), (dims, ((), ())), preferred_element_type=F32)


def _plain(a, b, dims):
    return (a.T if dims == TN else a), (b.T if dims == NT else b)


def _mmx(a, b, dims):
    return _mm(*_plain(a, b, dims), NN)


@jax.custom_vjp
def _dot(a, b):
    return _mmx(a, b, NN)


def _dot_fwd(a, b):
    return _dot(a, b), (a, b)


def _dot_bwd(res, g):
    a, b = res
    return _mmx(g, b, NT).astype(a.dtype), _mmx(a, g, TN).astype(b.dtype)


_dot.defvjp(_dot_fwd, _dot_bwd)


@jax.custom_vjp
def _dot_nt(a, b):
    return _mmx(a, b, NT)


def _dot_nt_fwd(a, b):
    return _dot_nt(a, b), (a, b)


def _dot_nt_bwd(res, g):
    a, b = res
    return _mmx(g, b, NN).astype(a.dtype), _mmx(g, a, TN).astype(b.dtype)


_dot_nt.defvjp(_dot_nt_fwd, _dot_nt_bwd)


@jax.custom_vjp
def _dot_tn(a, b):
    return _mmx(a, b, TN)


def _dot_tn_fwd(a, b):
    return _dot_tn(a, b), (a, b)


def _dot_tn_bwd(res, g):
    a, b = res
    return _mmx(b, g, NT).astype(a.dtype), _mmx(a, g, NN).astype(b.dtype)


_dot_tn.defvjp(_dot_tn_fwd, _dot_tn_bwd)


def _split(x):
    hi = x.astype(BF16)
    return hi, (x - hi.astype(F32)).astype(BF16)


def _mm3(a, b, dims):
    a, b = _plain(a, b, dims)
    (ah, al), (bh, bl) = _split(a), _split(b)
    mm = lambda x, y: jnp.dot(x, y, preferred_element_type=F32)
    return mm(ah, bh) + (mm(ah, bl) + mm(al, bh))


def _mm_ones(ones, x, ones_left):
    hi, lo = _split(x)
    mm = (lambda t: jnp.dot(ones, t, preferred_element_type=F32)) if ones_left else \
         (lambda t: jnp.dot(t, ones, preferred_element_type=F32))
    return mm(hi) + mm(lo)


def _pair_ones(kind, transposed):
    row, col = _iota((128, 128), 0), _iota((128, 128), 1)
    m = (row // 64) == (col // 64)
    if kind == "running":
        m = jnp.logical_and(m, (col >= row) if transposed else (col <= row))
    return jnp.where(m, 1.0, 0.0).astype(BF16)


@functools.partial(jax.custom_vjp, nondiff_argnums=(0,))
def _chunk_sum(kind, x):
    return _mm_ones(_pair_ones(kind, False), x, True)


def _chunk_sum_fwd(kind, x):
    return _chunk_sum(kind, x), None


def _chunk_sum_bwd(kind, _, g):
    return (_mm_ones(_pair_ones(kind, True), g, True),)


_chunk_sum.defvjp(_chunk_sum_fwd, _chunk_sum_bwd)


def _tri_ones(n, transposed):
    row, col = _iota((n, n), 0), _iota((n, n), 1)
    return jnp.where((row < col) if transposed else (row > col), 1.0, 0.0).astype(BF16)


@jax.custom_vjp
def _suffix_sum(x):
    return _mm_ones(_tri_ones(x.shape[1], False), x, False)


def _suffix_sum_fwd(x):
    return _suffix_sum(x), None


def _suffix_sum_bwd(_, g):
    return (_mm_ones(_tri_ones(g.shape[1], True), g, False),)


_suffix_sum.defvjp(_suffix_sum_fwd, _suffix_sum_bwd)


def _sigmoid(x):
    return jax.nn.sigmoid(x)


def _silu(x):
    return x * _sigmoid(x)


def _softplus(x):
    return jnp.maximum(x, 0.0) + jnp.log1p(jnp.exp(-jnp.abs(x)))


def _gelu(x):
    return 0.5 * x * (1.0 + jnp.tanh(0.7978845608028654 * (x + 0.044715 * (x * x * x))))


def _rms(x, gain):
    return x * lax.rsqrt(jnp.mean(x * x, axis=-1, keepdims=True) + EPS) * gain


def _shift_down_impl(x, k):
    return jnp.where(_iota(x.shape, 0) >= k, pltpu.roll(x, k, 0), 0.0)


def _shift_up_impl(x, k):
    n = x.shape[0]
    return jnp.where(_iota(x.shape, 0) < n - k, pltpu.roll(x, n - k, 0), 0.0)


@functools.partial(jax.custom_vjp, nondiff_argnums=(1,))
def _shift_down(x, k):
    return _shift_down_impl(x, k)


def _shift_down_fwd(x, k):
    return _shift_down_impl(x, k), None


def _shift_down_bwd(k, _, g):
    return (_shift_up_impl(g, k),)


_shift_down.defvjp(_shift_down_fwd, _shift_down_bwd)


def _lane_pick(x, idx):
    return jnp.sum(jnp.where(_iota(x.shape, 1) == idx, x, 0.0), axis=-1, keepdims=True)


def _dn_conv(x, w0, w1, w2, w3, l2_scale):
    y = _silu(w3 * x + w2 * _shift_down(x, 1) + w1 * _shift_down(x, 2) + w0 * _shift_down(x, 3))
    if l2_scale is None:
        return y
    return y * lax.rsqrt(jnp.sum(y * y, axis=-1, keepdims=True) + EPS) * l2_scale


def _dn_gate(a, b, a_log, dt_bias):
    return -jnp.exp(a_log) * _softplus(a + dt_bias), _sigmoid(b)


def _same_head(shape):
    return (_iota(shape, 0) < LANES) == (_iota(shape, 1) < LANES)


def _bd(r2):
    return jnp.where(_same_head((2 * LANES, 2 * LANES)), jnp.concatenate([r2, r2], axis=0), 0.0)


def _bd_t(y2):
    t = y2.T
    return jnp.where(_same_head((2 * LANES, 2 * LANES)), jnp.concatenate([t, t], axis=1), 0.0)


def _pair_prod(kind, a2, b2, mm):
    if kind == NN:
        return mm(a2, _bd(b2))
    if kind == NT:
        return mm(a2, _bd_t(b2))
    full = mm(a2.T, b2)
    return jnp.concatenate([full[:LANES, :LANES], full[LANES:, LANES:]], axis=1)


_MM1 = lambda x, y: _mm(x, y, NN)
_MM3 = lambda x, y: _mm3(x, y, NN)


def _pair_vjp_rule(kind, a2, b2, g, mm):
    if kind == NN:
        return _pair_prod(NT, g, b2, mm), _pair_prod(TN, a2, g, mm)
    if kind == NT:
        return _pair_prod(NN, g, b2, mm), _pair_prod(TN, g, a2, mm)
    return _pair_prod(NT, b2, g, mm), _pair_prod(NN, a2, g, mm)


@functools.partial(jax.custom_vjp, nondiff_argnums=(0,))
def _pdot(kind, a2, b2):
    return _pair_prod(kind, a2, b2, _MM1)


def _pdot_fwd(kind, a2, b2):
    return _pdot(kind, a2, b2), (a2, b2)


def _pdot_bwd(kind, res, g):
    return _pair_vjp_rule(kind, *res, g, _MM1)


_pdot.defvjp(_pdot_fwd, _pdot_bwd)


def _unit_lower_inverse(lower):
    n = lower.shape[0]
    nk = -lower
    inv = jnp.where(_iota(lower.shape, 0) == jnp.bitwise_and(_iota(lower.shape, 1), n - 1), 1.0, 0.0) + nk
    for _ in range(5):
        nk = _pair_prod(NN, nk, nk, _MM1)
        inv = inv + _pair_prod(NN, inv, nk, _MM1)
    return inv


@jax.custom_vjp
def _solve_with(lower, inv, rhs):
    return _pair_prod(NN, inv, rhs, _MM3)


def _solve_with_fwd(lower, inv, rhs):
    x = _pair_prod(NN, inv, rhs, _MM3)
    return x, (inv, x)


def _solve_with_bwd(res, g):
    inv, x = res
    d_rhs = _pair_prod(TN, inv, g, _MM3)
    return -_pair_prod(NT, d_rhs, x, _MM3), jnp.zeros_like(inv), d_rhs


_solve_with.defvjp(_solve_with_fwd, _solve_with_bwd)


def _dn_local(q, k, v, g, beta, inv=None):
    shape = (LANES, 2 * LANES)
    row, col = _iota(shape, 0), jnp.bitwise_and(_iota(shape, 1), LANES - 1)
    same = (row // 64) == (col // 64)
    tri_incl = jnp.logical_and(same, col <= row)
    tri_strict = jnp.logical_and(same, col < row)
    first = row < 64
    gc = _chunk_sum("running", g)
    gl = _chunk_sum("total", g)
    diff = gc - jnp.concatenate([gc[:, :LANES].T, gc[:, LANES:].T], axis=1)
    decay = jnp.where(tri_incl, jnp.exp(jnp.where(tri_incl, diff, 0.0)), 0.0)
    egc = jnp.exp(gc)
    lower = jnp.where(tri_strict, beta * _pdot(NT, k, k) * decay, 0.0)
    if inv is None:
        inv = _unit_lower_inverse(lower)
    u_val = _solve_with(lower, inv, v * beta)
    w_dec = _solve_with(lower, inv, k * (beta * egc))
    qk = jnp.where(tri_incl, _pdot(NT, q, k) * decay, 0.0)
    q_dec = q * egc
    k_dec = k * jnp.exp(gl - gc)
    cd1 = jnp.exp(jnp.sum(jnp.where(first, g, 0.0), axis=0, keepdims=True))
    cd2 = jnp.exp(jnp.sum(jnp.where(first, 0.0, g), axis=0, keepdims=True))
    return (u_val, w_dec, qk, q_dec, k_dec, cd1, cd2), inv


def _dn_state(u_val, w_dec, qk, q_dec, k_dec, cd1, cd2, s0):
    first = _iota((LANES, 2 * LANES), 0) < 64
    u1 = u_val - _pdot(NN, w_dec, s0)
    s1 = s0 * cd1 + _pdot(TN, jnp.where(first, k_dec, 0.0), u1)
    u2 = u_val - _pdot(NN, w_dec, s1)
    u_new = jnp.where(first, u1, u2)
    s2 = s1 * cd2 + _pdot(TN, jnp.where(first, 0.0, k_dec), u_new)
    o = jnp.where(first, _pdot(NN, q_dec, s0), _pdot(NN, q_dec, s1)) + _pdot(NN, qk, u_new)
    return o, s2


def _dn_post(o, z, gain):
    return _rms(o, gain) * _silu(z)


def _dn_gate_in(ab_ref, alog_ref, dtb_ref, h):
    ab = ab_ref[...]
    return _lane_pick(ab, h), _lane_pick(ab, h + NH), _lane_pick(alog_ref[...], h), _lane_pick(dtb_ref[...], h)


_DN_L2 = (DN_SCALE, 1.0, None)
DN_HPS = 2
DN_BLK = 4 * DN_HPS * LANES
_DN_COLS = tuple(slice(i * LANES, (i + 1) * LANES) for i in range(DN_HPS))


def _dn_in_cols(s, i):
    return slice((s * DN_HPS + i) * LANES, (s * DN_HPS + i + 1) * LANES)


def _dn_taps(cw_ref, s, i):
    return tuple(cw_ref[t:t + 1, _dn_in_cols(s, i)] for t in range(4))


def _dn_pack_gate(vals):
    lane = _iota((1, LANES), 1)
    out = 0.0
    for i, (g, beta) in enumerate(vals):
        out = out + jnp.where(lane == 2 * i, g, 0.0) + jnp.where(lane == 2 * i + 1, beta, 0.0)
    return out


def _pair_rows(n):
    return pl.ds(pl.multiple_of(n * 128, 128), 128)


def _dn_gate_rows(gate):
    head_a = _iota((1, DN_HPS * LANES), 1) < LANES
    return (jnp.where(head_a, _lane_pick(gate, 0), _lane_pick(gate, 2)),
            jnp.where(head_a, _lane_pick(gate, 1), _lane_pick(gate, 3)))


def _dn_gate_cols(dg, db):
    head_a = _iota((1, DN_HPS * LANES), 1) < LANES
    fold = lambda t: (jnp.sum(jnp.where(head_a, t, 0.0), axis=-1, keepdims=True),
                      jnp.sum(jnp.where(head_a, 0.0, t), axis=-1, keepdims=True))
    (dg_a, dg_b), (db_a, db_b) = fold(dg), fold(db)
    return [(dg_a, db_a), (dg_b, db_b)]


def _dn_in_specs(T):
    one = pl.Buffered(1)
    vec = pl.BlockSpec((1, LANES), lambda b, h: (0, 0))
    return [pl.BlockSpec((T, DN_BLK), lambda b, h: (b, h), pipeline_mode=one),
            pl.BlockSpec((T, LANES), lambda b, h: (b, 0), pipeline_mode=one),
            pl.BlockSpec((4, 3 * DN_HPS * LANES), lambda b, h: (0, h)), vec, vec, vec]


def _dn_fwd_call(proj_dn, proj_ab, conv_w, a_log, dt_bias, gain, B, T, gather=()):
    npair = T // 128
    ng = len(gather)
    nsteps = B * (NH // DN_HPS)

    def body(*refs):
        x_ref, ab_ref, cw_ref, alog_ref, dtb_ref, gain_ref = refs[:6]
        out_ref = refs[6 + ng]
        q_s, k_s, v_s, o_s, gate_s = refs[7 + 2 * ng:12 + 2 * ng]
        step_id = pl.program_id(0) * (NH // DN_HPS) + pl.program_id(1)
        if ng:
            send, forward, finish = _gather_phases(refs[6:6 + ng], refs[7 + ng:7 + 2 * ng], *refs[12 + 2 * ng:])
            pl.when(step_id == 0)(send)
            pl.when(step_id == nsteps // 2)(forward)
        hp = pl.program_id(1)
        gates = []
        for i, cs in enumerate(_DN_COLS):
            for s, (x_s, l2) in enumerate(zip((q_s, k_s, v_s), _DN_L2)):
                x_s[:, cs] = _dn_conv(x_ref[:, _dn_in_cols(s, i)], *_dn_taps(cw_ref, s, i), l2)
            gates.append(_dn_gate(*_dn_gate_in(ab_ref, alog_ref, dtb_ref, DN_HPS * hp + i)))
        gate_s[...] = _dn_pack_gate(gates)

        def local_of(pair):
            r = _pair_rows(pair)
            return _dn_local(q_s[r, :], k_s[r, :], v_s[r, :], *_dn_gate_rows(gate_s[r, :]))[0]

        def state_of(n, loc, state):
            o, s2 = _dn_state(*loc, state)
            o_s[_pair_rows(n), :] = o
            return s2

        def step(n, carry):
            loc, state = carry
            return local_of(n + 1), state_of(n, loc, state)

        loc, state = lax.fori_loop(0, npair - 1, step, (local_of(0), jnp.zeros((LANES, DN_HPS * LANES), F32)))
        state_of(npair - 1, loc, state)
        for i, cs in enumerate(_DN_COLS):
            out_ref[:, cs] = _dn_post(o_s[:, cs], x_ref[:, _dn_in_cols(3, i)], gain_ref[...])
        if ng:
            pl.when(step_id == nsteps - 1)(finish)

    wide = [pltpu.VMEM((T, DN_HPS * LANES), F32)]
    outs = pl.pallas_call(
        body, name="dn_fwd", grid=(B, NH // DN_HPS), in_specs=_dn_in_specs(T) + _any_specs(ng),
        out_specs=[pl.BlockSpec((T, DN_HPS * LANES), lambda b, h: (b, h), pipeline_mode=pl.Buffered(1))] + _any_specs(ng),
        out_shape=[jax.ShapeDtypeStruct((B * T, D), F32)] + _gather_shapes(gather),
        scratch_shapes=wide * 4 + [pltpu.VMEM((T, LANES), F32)] + (_gather_sems(ng) if ng else []),
        compiler_params=_params(dimension_semantics=("arbitrary", "arbitrary")),
    )(proj_dn, proj_ab, conv_w, a_log, dt_bias, gain, *gather)
    return outs[0], outs[1:]


def _dn_bwd_call(proj_dn, proj_ab, dmix, conv_w, a_log, dt_bias, gain, B, T, swap=()):
    npair = T // 128
    ns = len(swap)
    nsteps = B * (NH // DN_HPS)

    def body(*refs):
        x_ref, ab_ref, cw_ref, alog_ref, dtb_ref, gain_ref, do_ref = refs[:7]
        dx_ref, dab_ref, dcw_ref, dalog_ref, ddtb_ref, dgain_ref = refs[7 + ns:13 + ns]
        q_s, k_s, v_s, o_s, gate_s, dgate_s, st_s, inv_s, dcd_s = refs[13 + 2 * ns:22 + 2 * ns]
        dloc_s = refs[22 + 2 * ns:27 + 2 * ns]
        b_i, hp = pl.program_id(0), pl.program_id(1)
        step_id = b_i * (NH // DN_HPS) + hp
        if ns:
            send, finish = _chip_swap_phases(refs[7:7 + ns], refs[13 + ns:13 + 2 * ns], *refs[27 + 2 * ns:])
            pl.when(step_id == 0)(send)
        gates = []
        for i, cs in enumerate(_DN_COLS):
            for s, (x_s, l2) in enumerate(zip((q_s, k_s, v_s), _DN_L2)):
                x_s[:, cs] = _dn_conv(x_ref[:, _dn_in_cols(s, i)], *_dn_taps(cw_ref, s, i), l2)
            gates.append(_dn_gate(*_dn_gate_in(ab_ref, alog_ref, dtb_ref, DN_HPS * hp + i)))
        gate_s[...] = _dn_pack_gate(gates)

        def pair_in(r):
            return (q_s[r, :], k_s[r, :], v_s[r, :]) + _dn_gate_rows(gate_s[r, :])

        def local_of(pair, known_inverse=False):
            loc, inv = _dn_local(*pair_in(_pair_rows(pair)), inv_s[pair] if known_inverse else None)
            if not known_inverse:
                inv_s[pair] = inv
            return loc

        def state_of(n, loc, state):
            st_s[n] = state
            o, s2 = _dn_state(*loc, state)
            o_s[_pair_rows(n), :] = o
            return s2

        def step(n, carry):
            loc, state = carry
            return local_of(n + 1), state_of(n, loc, state)

        zero_state = jnp.zeros((LANES, DN_HPS * LANES), F32)
        loc, state = lax.fori_loop(0, npair - 1, step, (local_of(0), zero_state))
        state_of(npair - 1, loc, state)

        @pl.when(jnp.logical_and(b_i == 0, hp == 0))
        def _():
            dcw_ref[...] = jnp.zeros_like(dcw_ref)
            dalog_ref[...] = jnp.zeros_like(dalog_ref)
            ddtb_ref[...] = jnp.zeros_like(ddtb_ref)
            dgain_ref[...] = jnp.zeros_like(dgain_ref)

        for i, cs in enumerate(_DN_COLS):
            zc = _dn_in_cols(3, i)
            _, post_vjp = jax.vjp(_dn_post, o_s[:, cs], x_ref[:, zc], gain_ref[...])
            do, dz, dgain = post_vjp(do_ref[:, cs])
            dx_ref[:, zc] = dz
            o_s[:, cs] = do
            dgain_ref[...] += dgain

        def state_back(nn, dstate):
            n = npair - 1 - nn
            r = _pair_rows(n)
            _, state_vjp = jax.vjp(_dn_state, *local_of(n, known_inverse=True), st_s[n])
            *dloc, ds0 = state_vjp((o_s[r, :], dstate))
            for d_s, val in zip(dloc_s, dloc[:5]):
                d_s[r, :] = val
            dcd_s[n, 0:1, :], dcd_s[n, 1:2, :] = dloc[5], dloc[6]
            return ds0

        lax.fori_loop(0, npair, state_back, zero_state)

        wide_cols = lambda s: slice(s * DN_HPS * LANES, (s + 1) * DN_HPS * LANES)

        def local_back(m, _):
            for pair in (2 * m, 2 * m + 1):
                r = _pair_rows(pair)
                inv = inv_s[pair]
                local = lambda q, k, v, g, beta, inv=inv: _dn_local(q, k, v, g, beta, inv)[0]
                _, local_vjp = jax.vjp(local, *pair_in(r))
                dq, dk, dv, dg, db = local_vjp(tuple(d_s[r, :] for d_s in dloc_s) + (dcd_s[pair, 0:1, :], dcd_s[pair, 1:2, :]))
                dx_ref[r, wide_cols(0)], dx_ref[r, wide_cols(1)], dx_ref[r, wide_cols(2)] = dq, dk, dv
                dgate_s[r, :] = _dn_pack_gate(_dn_gate_cols(dg, db))
            return 0

        lax.fori_loop(0, npair // 2, local_back, 0)

        lane = _iota((1, LANES), 1)
        dab = 0.0
        for i, cs in enumerate(_DN_COLS):
            h = DN_HPS * hp + i
            for s, l2 in enumerate(_DN_L2):
                xc = _dn_in_cols(s, i)
                _, conv_vjp = jax.vjp(functools.partial(_dn_conv, l2_scale=l2), x_ref[:, xc], *_dn_taps(cw_ref, s, i))
                dx, *dw = conv_vjp(dx_ref[:, xc])
                dx_ref[:, xc] = dx
                for t in range(4):
                    dcw_ref[h + 4 * s, t:t + 1, :] += dw[t]
            _, gate_vjp = jax.vjp(_dn_gate, *_dn_gate_in(ab_ref, alog_ref, dtb_ref, h))
            dgate = dgate_s[...]
            da, db, dalog, ddtb = gate_vjp((_lane_pick(dgate, 2 * i), _lane_pick(dgate, 2 * i + 1)))
            dab = dab + jnp.where(lane == h, da, 0.0) + jnp.where(lane == h + NH, db, 0.0)
            dalog_ref[...] += jnp.where(lane == h, dalog, 0.0)
            ddtb_ref[...] += jnp.where(lane == h, ddtb, 0.0)

        @pl.when(hp == 0)
        def _():
            dab_ref[...] = jnp.zeros_like(dab_ref)

        dab_ref[...] += dab
        if ns:
            pl.when(step_id == nsteps - 1)(finish)

    M = B * T
    one = pl.Buffered(1)
    vec = pl.BlockSpec((1, LANES), lambda b, h: (0, 0))
    wide = [pltpu.VMEM((T, DN_HPS * LANES), F32)]
    narrow = [pltpu.VMEM((T, LANES), F32)]
    vec_shape = jax.ShapeDtypeStruct((1, LANES), F32)
    outs = pl.pallas_call(
        body, name="dn_bwd", grid=(B, NH // DN_HPS),
        in_specs=_dn_in_specs(T) + [pl.BlockSpec((T, DN_HPS * LANES), lambda b, h: (b, h), pipeline_mode=one)] + _any_specs(ns),
        out_specs=[pl.BlockSpec((T, DN_BLK), lambda b, h: (b, h), pipeline_mode=one), pl.BlockSpec((T, LANES), lambda b, h: (b, 0)),
                   pl.BlockSpec((12, 4, LANES), lambda b, h: (0, 0, 0)), vec, vec, vec] + _any_specs(ns),
        out_shape=[jax.ShapeDtypeStruct((M, 4 * DN_W), F32), jax.ShapeDtypeStruct((M, LANES), F32),
                   jax.ShapeDtypeStruct((12, 4, LANES), F32), vec_shape, vec_shape, vec_shape] + _chip_swap_shapes(swap),
        scratch_shapes=wide * 4 + narrow * 2 + [pltpu.VMEM((npair, LANES, DN_HPS * LANES), F32)] * 2
        + [pltpu.VMEM((npair, 8, DN_HPS * LANES), F32)] + wide * 5 + (_chip_swap_sems(ns) if ns else []),
        compiler_params=pltpu.CompilerParams(vmem_limit_bytes=VMEM_LIMIT_MAX, dimension_semantics=("arbitrary", "arbitrary")),
    )(proj_dn, proj_ab, conv_w, a_log, dt_bias, gain, dmix, *swap)
    return outs[:6], outs[6:]


SBQ = 256


def _group_rms(x, gain):
    first = _iota(x.shape, 1) < 64
    sq = x * x
    ss_a = jnp.sum(jnp.where(first, sq, 0.0), axis=-1, keepdims=True)
    ss_b = jnp.sum(jnp.where(first, 0.0, sq), axis=-1, keepdims=True)
    ms = jnp.where(first, ss_a, ss_b) * (1.0 / 64)
    return x * lax.rsqrt(ms + EPS) * gain


def _sb_stack(q):
    first = _iota((1, LANES), 1) < 64
    return jnp.concatenate([jnp.where(first, q, 0.0), jnp.where(first, 0.0, q)], axis=0)


def _sb_fold(acc):
    return jnp.where(_iota((1, LANES), 1) < 64, acc[:SBQ], acc[SBQ:])


def _sb_logs(q2, k, diag):
    n = SBQ
    z = _mm(q2, k, ((1,), (1,))) * SB_SCALE
    ls_pos = jnp.minimum(z, 0.0) - jnp.log1p(jnp.exp(-jnp.abs(z)))
    l1m = ls_pos - z
    if not diag:
        return ls_pos, l1m, None
    mask = _iota((2 * n, n), 1) < jnp.bitwise_and(_iota((2 * n, n), 0), n - 1)
    return ls_pos, jnp.where(mask, l1m, 0.0), mask


def _sb_weights(ls_pos, l1m, mask, carry):
    w = jnp.exp(ls_pos + (_mm_ones(_tri_ones(SBQ, False), l1m, False) + carry))
    return w if mask is None else jnp.where(mask, w, 0.0)


def _sb_block(q, k, v, carry, diag):
    ls_pos, l1m, mask = _sb_logs(_sb_stack(q), k, diag)
    w = _sb_weights(ls_pos, l1m, mask, carry)
    return _mm(w, v, ((1,), (0,))), carry + jnp.sum(l1m, axis=-1, keepdims=True)


def _sb_rowsum(q, k, diag):
    return jnp.sum(_sb_logs(_sb_stack(q), k, diag)[1], axis=-1, keepdims=True)


def _sb_block_bwd(q, k, v, carry, diag, dpv, dcarry):
    q2 = _sb_stack(q)
    ls_pos, l1m, mask = _sb_logs(q2, k, diag)
    w = _sb_weights(ls_pos, l1m, mask, carry)
    dv = _mm(w, dpv, ((0,), (0,)))
    de = _mm(dpv, v, ((1,), (1,))) * w
    dl1m = _mm_ones(_tri_ones(SBQ, True), de, False) + dcarry
    if mask is not None:
        dl1m = jnp.where(mask, dl1m, 0.0)
    sig = jnp.exp(ls_pos)
    dz = (de * (1.0 - sig) - dl1m * sig) * SB_SCALE
    dq = _sb_fold(_mm(dz, k, ((1,), (0,))))
    return dq, _mm(dz, q2, ((0,), (0,))), dv, dcarry + jnp.sum(de, axis=-1, keepdims=True)


_SB_Q, _SB_K, _SB_V = (slice(i * LANES, (i + 1) * LANES) for i in range(3))


def _sb_fwd_call(proj_sb, mix, q_gain, k_gain, B, T, gather=()):
    nblk = T // SBQ
    ng = len(gather)
    nsteps = 2 * B

    def body(*refs):
        x_ref, qg_ref, kg_ref = refs[:3]
        out_ref = refs[4 + ng]
        q_s, k_s = refs[5 + 2 * ng:7 + 2 * ng]
        step_id = 2 * pl.program_id(0) + pl.program_id(1)
        if ng:
            send, forward, finish = _gather_phases(refs[4:4 + ng], refs[5 + ng:5 + 2 * ng], *refs[7 + 2 * ng:])
            pl.when(step_id == 0)(send)
            pl.when(step_id == nsteps // 2)(forward)
        q_s[...] = _group_rms(x_ref[:, _SB_Q], qg_ref[...])
        k_s[...] = _group_rms(x_ref[:, _SB_K], kg_ref[...])

        def qblock(i, _):
            ri = pl.ds(pl.multiple_of(i * SBQ, SBQ), SBQ)
            q = q_s[ri, :]

            def kblock(jj, c):
                rj = pl.ds(pl.multiple_of((i - 1 - jj) * SBQ, SBQ), SBQ)
                pv, carry = _sb_block(q, k_s[rj, :], x_ref[rj, _SB_V], c[1], False)
                return c[0] + pv, carry

            on_diag = _sb_block(q, k_s[ri, :], x_ref[ri, _SB_V], jnp.zeros((2 * SBQ, 1), F32), True)
            acc, _c = lax.fori_loop(0, i, kblock, on_diag)
            out_ref[ri, :] = _sb_fold(acc)
            return 0

        lax.fori_loop(0, nblk, qblock, 0)
        if ng:
            pl.when(step_id == nsteps - 1)(finish)

    vec = pl.BlockSpec((1, LANES), lambda b, p: (0, 0))
    outs = pl.pallas_call(
        body, name="sb_fwd", grid=(B, 2),
        in_specs=[pl.BlockSpec((T, 3 * LANES), lambda b, p: (b, p)), vec, vec, pl.BlockSpec(memory_space=pl.ANY)] + _any_specs(ng),
        out_specs=[pl.BlockSpec((T, LANES), lambda b, p: (b, DN_W // LANES + p))] + _any_specs(ng),
        out_shape=[jax.ShapeDtypeStruct((B * T, D), F32)] + _gather_shapes(gather), input_output_aliases={3: 0},
        scratch_shapes=[pltpu.VMEM((T, LANES), F32)] * 2 + (_gather_sems(ng) if ng else []),
        compiler_params=_params(dimension_semantics=("arbitrary", "arbitrary")),
    )(proj_sb, q_gain, k_gain, mix, *gather)
    return outs[0], outs[1:]


def _sb_bwd_call(proj_sb, dmix, q_gain, k_gain, B, T):
    nblk = T // SBQ

    def body(x_ref, qg_ref, kg_ref, do_ref, dx_ref, dqg_ref, dkg_ref, q_s, k_s, dq_s, dk_s, dv_s, c_s):
        b_i, p = pl.program_id(0), pl.program_id(1)
        qn, q_vjp = jax.vjp(_group_rms, x_ref[:, _SB_Q], qg_ref[...])
        kn, k_vjp = jax.vjp(_group_rms, x_ref[:, _SB_K], kg_ref[...])
        q_s[...], k_s[...] = qn, kn
        dk_s[...] = jnp.zeros_like(dk_s)
        dv_s[...] = jnp.zeros_like(dv_s)

        def qblock(i, _):
            ri = pl.ds(pl.multiple_of(i * SBQ, SBQ), SBQ)
            q = q_s[ri, :]
            dacc = _sb_stack(do_ref[ri, :])

            def carries(jj, carry):
                j = i - 1 - jj
                rj = pl.ds(pl.multiple_of(j * SBQ, SBQ), SBQ)
                c_s[j] = carry
                return carry + _sb_rowsum(q, k_s[rj, :], False)

            lax.fori_loop(0, i, carries, _sb_rowsum(q, k_s[ri, :], True))

            def kblock(j, c):
                rj = pl.ds(pl.multiple_of(j * SBQ, SBQ), SBQ)
                dq_j, dk_j, dv_j, dc = _sb_block_bwd(q, k_s[rj, :], x_ref[rj, _SB_V], c_s[j], False, dacc, c[1])
                dk_s[rj, :] += dk_j
                dv_s[rj, :] += dv_j
                return c[0] + dq_j, dc

            dq, dc = lax.fori_loop(0, i, kblock, (jnp.zeros((SBQ, LANES), F32), jnp.zeros((2 * SBQ, 1), F32)))
            dq_i, dk_i, dv_i, _dc = _sb_block_bwd(q, k_s[ri, :], x_ref[ri, _SB_V], jnp.zeros((2 * SBQ, 1), F32), True, dacc, dc)
            dk_s[ri, :] += dk_i
            dv_s[ri, :] += dv_i
            dq_s[ri, :] = dq + dq_i
            return 0

        lax.fori_loop(0, nblk, qblock, 0)
        dq_in, dqg = q_vjp(dq_s[...])
        dk_in, dkg = k_vjp(dk_s[...])
        dx_ref[:, _SB_Q], dx_ref[:, _SB_K], dx_ref[:, _SB_V] = dq_in, dk_in, dv_s[...]

        @pl.when(jnp.logical_and(b_i == 0, p == 0))
        def _():
            dqg_ref[...] = jnp.zeros_like(dqg_ref)
            dkg_ref[...] = jnp.zeros_like(dkg_ref)

        dqg_ref[...] += dqg + pltpu.roll(dqg, 64, 1)
        dkg_ref[...] += dkg + pltpu.roll(dkg, 64, 1)

    M = B * T
    vec = pl.BlockSpec((1, LANES), lambda b, p: (0, 0))
    blk = pl.BlockSpec((T, 3 * LANES), lambda b, p: (b, p))
    big = [pltpu.VMEM((T, LANES), F32)]
    return pl.pallas_call(
        body, name="sb_bwd", grid=(B, 2),
        in_specs=[blk, vec, vec, pl.BlockSpec((T, LANES), lambda b, p: (b, DN_W // LANES + p))],
        out_specs=[blk, vec, vec],
        out_shape=[jax.ShapeDtypeStruct((M, 3 * SB_W), F32)] + [jax.ShapeDtypeStruct((1, LANES), F32)] * 2,
        scratch_shapes=big * 5 + [pltpu.VMEM((nblk, 2 * SBQ, 1), F32)],
        compiler_params=_params(dimension_semantics=("arbitrary", "arbitrary")),
    )(proj_sb, q_gain, k_gain, dmix)


def _sg_chunk(u, v, gain, w_a, w_b, bias):
    n = 128
    row, col = _iota((n, n), 0), _iota((n, n), 1)
    first = _iota((1, LANES), 1) < 64
    vn = _group_rms(_gelu(v), gain)
    tril = col <= row
    mixed = jnp.where(first, _dot(jnp.where(tril, w_a, 0.0), vn), _dot(jnp.where(tril, w_b, 0.0), vn)) + bias
    return _gelu(u) * mixed


_SG_U, _SG_V = slice(0, LANES), slice(LANES, 2 * LANES)


def _sg_fwd_call(proj_sg, mix, gain, sg_w, bias, B, T):
    nchunk = T // 128

    def body(x_ref, g_ref, wa_ref, wb_ref, bias_ref, mix_ref, out_ref):
        del mix_ref

        def step(i, _):
            r = pl.ds(pl.multiple_of(i * 128, 128), 128)
            out_ref[r, :] = _sg_chunk(x_ref[r, _SG_U], x_ref[r, _SG_V], g_ref[...], wa_ref[0], wb_ref[0], bias_ref[...])
            return 0

        lax.fori_loop(0, nchunk, step, 0)

    return pl.pallas_call(
        body, name="sg_fwd", grid=(B, 2),
        in_specs=[pl.BlockSpec((T, 2 * LANES), lambda b, p: (b, p)), pl.BlockSpec((1, LANES), lambda b, p: (0, p)),
                  pl.BlockSpec((1, 128, 128), lambda b, p: (2 * p, 0, 0)), pl.BlockSpec((1, 128, 128), lambda b, p: (2 * p + 1, 0, 0)),
                  pl.BlockSpec((128, LANES), lambda b, p: (0, p)), pl.BlockSpec(memory_space=pl.ANY)],
        out_specs=pl.BlockSpec((T, LANES), lambda b, p: (b, (DN_W + SB_W) // LANES + p)),
        out_shape=jax.ShapeDtypeStruct((B * T, D), F32), input_output_aliases={5: 0},
        compiler_params=_params(dimension_semantics=("arbitrary", "arbitrary")),
    )(proj_sg, gain, sg_w, sg_w, bias, mix)


def _sg_bwd_call(proj_sg, dmix, gain, sg_w, bias, B, T):
    nchunk = T // 128

    def body(x_ref, g_ref, wa_ref, wb_ref, bias_ref, do_ref, dx_ref, dg_ref, dw_ref, db_ref):
        p, b_i = pl.program_id(0), pl.program_id(1)

        def step(i, c):
            r = pl.ds(pl.multiple_of(i * 128, 128), 128)
            _, vjp = jax.vjp(_sg_chunk, x_ref[r, _SG_U], x_ref[r, _SG_V], g_ref[...], wa_ref[0], wb_ref[0], bias_ref[...])
            du, dv, dg, dwa, dwb, dbias = vjp(do_ref[r, :])
            dx_ref[r, _SG_U], dx_ref[r, _SG_V] = du, dv
            return c[0] + dg, c[1] + dwa, c[2] + dwb, c[3] + dbias

        z = jnp.zeros((128, 128), F32)
        dg, dwa, dwb, dbias = lax.fori_loop(0, nchunk, step, (jnp.zeros((1, LANES), F32), z, z, z))
        lane = _iota((1, LANES), 1)
        first = lane < 64
        s_a = jnp.sum(jnp.where(first, dbias, 0.0), axis=-1, keepdims=True)
        s_b = jnp.sum(jnp.where(first, 0.0, dbias), axis=-1, keepdims=True)
        dbg = jnp.where(lane == 2 * p, s_a, 0.0) + jnp.where(lane == 2 * p + 1, s_b, 0.0)

        @pl.when(b_i == 0)
        def _():
            dg_ref[...] = jnp.zeros_like(dg_ref)
            dw_ref[...] = jnp.zeros_like(dw_ref)

        @pl.when(jnp.logical_and(b_i == 0, p == 0))
        def _():
            db_ref[...] = jnp.zeros_like(db_ref)

        dg_ref[...] += dg
        dw_ref[0] += dwa
        dw_ref[1] += dwb
        db_ref[...] += dbg

    M = B * T
    blk = pl.BlockSpec((T, 2 * LANES), lambda p, b: (b, p))
    return pl.pallas_call(
        body, name="sg_bwd", grid=(2, B),
        in_specs=[blk, pl.BlockSpec((1, LANES), lambda p, b: (0, p)),
                  pl.BlockSpec((1, 128, 128), lambda p, b: (2 * p, 0, 0)), pl.BlockSpec((1, 128, 128), lambda p, b: (2 * p + 1, 0, 0)),
                  pl.BlockSpec((128, LANES), lambda p, b: (0, p)),
                  pl.BlockSpec((T, LANES), lambda p, b: (b, (DN_W + SB_W) // LANES + p))],
        out_specs=[blk, pl.BlockSpec((1, LANES), lambda p, b: (0, p)), pl.BlockSpec((2, 128, 128), lambda p, b: (p, 0, 0)),
                   pl.BlockSpec((128, LANES), lambda p, b: (0, 0))],
        out_shape=[jax.ShapeDtypeStruct((M, 2 * SG_W), F32), jax.ShapeDtypeStruct((1, SG_W), F32),
                   jax.ShapeDtypeStruct((4, 128, 128), F32), jax.ShapeDtypeStruct((128, LANES), F32)],
        compiler_params=_params(dimension_semantics=("arbitrary", "arbitrary")),
    )(proj_sg, gain, sg_w, sg_w, bias, dmix)


def _row_tile(m):
    return min(m, 512)


def _in_proj_call(x, gain, wt):
    m = x.shape[0]
    tm = _row_tile(m)

    def body(x_ref, g_ref, wt_ref, *out_refs):
        h = _rms(x_ref[...], g_ref[...]).astype(BF16)
        out_refs[-1][...] = h
        for (off, width), out_ref in zip(SECTIONS, out_refs):
            out_ref[...] = lax.dot_general(h, wt_ref[off:off + width, :], (((1,), (1,)), ((), ())), preferred_element_type=F32)

    rows = lambda width: pl.BlockSpec((tm, width), lambda i: (i, 0))
    return pl.pallas_call(
        body, name="in_proj", grid=(m // tm,),
        in_specs=[rows(D), pl.BlockSpec((1, D), lambda i: (0, 0)),
                  pl.BlockSpec((NPACK, D), lambda i: (0, 0), pipeline_mode=pl.Buffered(1))],
        out_specs=[rows(w) for _, w in SECTIONS] + [rows(D)],
        out_shape=[jax.ShapeDtypeStruct((m, w), F32) for _, w in SECTIONS] + [jax.ShapeDtypeStruct((m, D), BF16)],
        compiler_params=_params(dimension_semantics=("arbitrary",)),
    )(x, gain, wt)


def _in_proj_bwd_call(dsections, wt, x, gain, dres):
    m = x.shape[0]
    tm = _row_tile(m)

    def body(*refs):
        ds_refs, (wt_ref, x_ref, g_ref, dres_ref, dx_ref, dg_ref) = refs[:len(SECTIONS)], refs[len(SECTIONS):]

        @pl.when(pl.program_id(0) == 0)
        def _():
            dg_ref[...] = jnp.zeros_like(dg_ref)

        dh = 0.0
        for (off, width), ds_ref in zip(SECTIONS, ds_refs):
            dh = dh + jnp.dot(ds_ref[...].astype(BF16), wt_ref[off:off + width, :], preferred_element_type=F32)
        _, vjp = jax.vjp(_rms, x_ref[...], g_ref[...])
        dx, dg = vjp(dh)
        dx_ref[...] = dres_ref[...] + dx
        dg_ref[...] += dg

    rows = lambda width: pl.BlockSpec((tm, width), lambda i: (i, 0))
    return pl.pallas_call(
        body, name="in_proj_bwd", grid=(m // tm,),
        in_specs=[rows(w) for _, w in SECTIONS] + [pl.BlockSpec((NPACK, D), lambda i: (0, 0), pipeline_mode=pl.Buffered(1)),
                                                   rows(D), pl.BlockSpec((1, D), lambda i: (0, 0)), rows(D)],
        out_specs=[rows(D), pl.BlockSpec((1, D), lambda i: (0, 0))],
        out_shape=[jax.ShapeDtypeStruct((m, D), F32), jax.ShapeDtypeStruct((1, D), F32)],
        compiler_params=_params(dimension_semantics=("arbitrary",)),
    )(*dsections, wt, x, gain, dres)


def _in_proj_grad_call(dsections, h):
    m = h.shape[0]
    tm = min(m, 256)

    def body(*refs):
        ds_refs, (h_ref, out_ref) = refs[:len(SECTIONS)], refs[len(SECTIONS):]

        @pl.when(pl.program_id(0) == 0)
        def _():
            out_ref[...] = jnp.zeros_like(out_ref)

        for (off, width), ds_ref in zip(SECTIONS, ds_refs):
            out_ref[off:off + width, :] += lax.dot_general(ds_ref[...].astype(BF16), h_ref[...], (((0,), (0,)), ((), ())),
                                                           preferred_element_type=F32)

    rows = lambda width: pl.BlockSpec((tm, width), lambda i: (i, 0))
    return pl.pallas_call(
        body, name="grad_w_in", grid=(m // tm,),
        in_specs=[rows(w) for _, w in SECTIONS] + [rows(D)],
        out_specs=pl.BlockSpec((NPACK, D), lambda i: (0, 0), pipeline_mode=pl.Buffered(1)),
        out_shape=jax.ShapeDtypeStruct((NPACK, D), F32),
        compiler_params=_params(dimension_semantics=("arbitrary",)),
    )(*dsections, h)


def _packed_column_of():
    t = np.full(NPACK, -1, np.int64)
    lanes = np.arange(LANES)
    for pair in range(2):
        for s in range(4):
            t[DN_OFF + pair * 1024 + s * 256 + np.arange(256)] = s * DN_W + pair * 256 + np.arange(256)
        for s in range(3):
            t[SB_OFF + pair * 384 + s * LANES + lanes] = 2056 + s * SB_W + pair * LANES + lanes
        for s in range(2):
            t[SG_OFF + pair * 256 + s * LANES + lanes] = 2056 + 3 * SB_W + s * SG_W + pair * LANES + lanes
    t[AB_OFF + np.arange(2 * NH)] = 4 * DN_W + np.arange(2 * NH)
    return t


def _row_tables():
    col = _packed_column_of()
    fwd = np.where(col >= 0, (col // IN_SHARD) * IN_SHARD_PAD + col % IN_SHARD, -1)
    packed_of = np.full(IN_DIM, -1, np.int64)
    packed_of[col[col >= 0]] = np.nonzero(col >= 0)[0]
    r = np.arange(NDEV * IN_SHARD_PAD)
    inside = r % IN_SHARD_PAD < IN_SHARD
    back = np.where(inside, packed_of[np.minimum((r // IN_SHARD_PAD) * IN_SHARD + r % IN_SHARD_PAD, IN_DIM - 1)], -1)
    return fwd, back


def _row_perm_call(src, table, name):
    n_out = table.shape[0]
    touched = [sorted(set((table[b * 128:(b + 1) * 128][table[b * 128:(b + 1) * 128] >= 0] // 128).tolist()))
               for b in range(n_out // 128)]

    def body(tbl_ref, src_ref, out_ref):
        lane = _iota((1, LANES), 1)
        for b, blocks in enumerate(touched):
            want = tbl_ref[b * 128:(b + 1) * 128, :]
            acc = jnp.zeros((128, D), F32)
            for sb in blocks:
                pick = jnp.where(want == sb * 128 + lane, 1.0, 0.0).astype(BF16)
                acc = acc + jnp.dot(pick, src_ref[sb * 128:(sb + 1) * 128, :].astype(BF16), preferred_element_type=F32)
            out_ref[b * 128:(b + 1) * 128, :] = acc.astype(BF16)

    return pl.pallas_call(
        body, name=name, out_shape=jax.ShapeDtypeStruct((n_out, D), BF16),
        in_specs=[pl.BlockSpec(memory_space=pltpu.VMEM)] * 2, out_specs=pl.BlockSpec(memory_space=pltpu.VMEM),
        compiler_params=_params(),
    )(jnp.asarray(table.reshape(-1, 1), jnp.int32), src)


def _out_proj_call(a, w, res):
    m, k = a.shape
    n = w.shape[1]
    tm = _row_tile(m)

    def body(a_ref, w_ref, res_ref, out_ref):
        out_ref[...] = res_ref[...] + jnp.dot(a_ref[...].astype(BF16), w_ref[...], preferred_element_type=F32)

    return pl.pallas_call(
        body, name="out_proj", grid=(m // tm,),
        in_specs=[pl.BlockSpec((tm, k), lambda i: (i, 0)), pl.BlockSpec((k, n), lambda i: (0, 0)),
                  pl.BlockSpec((tm, n), lambda i: (i, 0))],
        out_specs=pl.BlockSpec((tm, n), lambda i: (i, 0)),
        out_shape=jax.ShapeDtypeStruct((m, n), F32),
        compiler_params=_params(dimension_semantics=("arbitrary",)),
    )(a, w, res)


def _ffn_specs(tm):
    return [pl.BlockSpec((1, D, FF_SHARD), lambda i, j: (j, 0, 0)), pl.BlockSpec((FF_SHARD, D), lambda i, j: (j, 0))]


def _ffn_fwd_call(x, gain, w1, w2):
    m = x.shape[0]
    tm = _row_tile(m)

    def body(x_ref, g_ref, w1_ref, w2_ref, out_ref, h_s, acc_s):
        j = pl.program_id(1)

        @pl.when(j == 0)
        def _():
            h_s[...] = _rms(x_ref[...], g_ref[...]).astype(BF16)
            acc_s[...] = jnp.zeros_like(acc_s)

        a = jnp.maximum(jnp.dot(h_s[...], w1_ref[0], preferred_element_type=F32), 0.0)
        acc_s[...] += jnp.dot((a * a).astype(BF16), w2_ref[...], preferred_element_type=F32)

        @pl.when(j == NDEV - 1)
        def _():
            out_ref[...] = x_ref[...] + acc_s[...]

    return pl.pallas_call(
        body, name="ffn_fwd", grid=(m // tm, NDEV),
        in_specs=[pl.BlockSpec((tm, D), lambda i, j: (i, 0)), pl.BlockSpec((1, D), lambda i, j: (0, 0))] + _ffn_specs(tm),
        out_specs=pl.BlockSpec((tm, D), lambda i, j: (i, 0)),
        out_shape=jax.ShapeDtypeStruct((m, D), F32),
        scratch_shapes=[pltpu.VMEM((tm, D), BF16), pltpu.VMEM((tm, D), F32)],
        compiler_params=_params(dimension_semantics=("arbitrary", "arbitrary")),
    )(x, gain, w1, w2)


def _ffn_bwd_call(x, dy, gain, w1, w2):
    m = x.shape[0]
    tm = _row_tile(m)

    def body(x_ref, dy_ref, g_ref, w1_ref, w2_ref, dx_ref, da_ref, r_ref, h_ref, dg_ref, acc_s):
        i, j = pl.program_id(0), pl.program_id(1)

        @pl.when(j == 0)
        def _():
            h_ref[...] = _rms(x_ref[...], g_ref[...]).astype(BF16)
            acc_s[...] = jnp.zeros_like(acc_s)

        @pl.when(jnp.logical_and(i == 0, j == 0))
        def _():
            dg_ref[...] = jnp.zeros_like(dg_ref)

        a = jnp.maximum(jnp.dot(h_ref[...], w1_ref[0], preferred_element_type=F32), 0.0)
        r_ref[...] = (a * a).astype(BF16)
        dr = lax.dot_general(dy_ref[...].astype(BF16), w2_ref[...], (((1,), (1,)), ((), ())), preferred_element_type=F32)
        da = (dr * (2.0 * a)).astype(BF16)
        da_ref[...] = da
        acc_s[...] += lax.dot_general(da, w1_ref[0], (((1,), (1,)), ((), ())), preferred_element_type=F32)

        @pl.when(j == NDEV - 1)
        def _():
            _, vjp = jax.vjp(_rms, x_ref[...], g_ref[...])
            dx, dg = vjp(acc_s[...])
            dx_ref[...] = dy_ref[...] + dx
            dg_ref[...] += dg

    return pl.pallas_call(
        body, name="ffn_bwd", grid=(m // tm, NDEV),
        in_specs=[pl.BlockSpec((tm, D), lambda i, j: (i, 0)), pl.BlockSpec((tm, D), lambda i, j: (i, 0)),
                  pl.BlockSpec((1, D), lambda i, j: (0, 0))] + _ffn_specs(tm),
        out_specs=[pl.BlockSpec((tm, D), lambda i, j: (i, 0)), pl.BlockSpec((tm, FF_SHARD), lambda i, j: (i, j)),
                   pl.BlockSpec((tm, FF_SHARD), lambda i, j: (i, j)), pl.BlockSpec((tm, D), lambda i, j: (i, 0)),
                   pl.BlockSpec((1, D), lambda i, j: (0, 0))],
        out_shape=[jax.ShapeDtypeStruct((m, D), F32), jax.ShapeDtypeStruct((m, DFF), BF16), jax.ShapeDtypeStruct((m, DFF), BF16),
                   jax.ShapeDtypeStruct((m, D), BF16), jax.ShapeDtypeStruct((1, D), F32)],
        scratch_shapes=[pltpu.VMEM((tm, D), F32)],
        compiler_params=_params(dimension_semantics=("arbitrary", "arbitrary")),
    )(x, dy, gain, w1, w2)


def _mm_nt_call(a, b, name):
    m, k = a.shape
    n = b.shape[0]
    tm = _row_tile(m)

    def body(a_ref, b_ref, out_ref):
        out_ref[...] = lax.dot_general(a_ref[...].astype(BF16), b_ref[...].astype(BF16), (((1,), (1,)), ((), ())),
                                       preferred_element_type=F32)

    return pl.pallas_call(
        body, name=name, grid=(m // tm,),
        in_specs=[pl.BlockSpec((tm, k), lambda i: (i, 0)), pl.BlockSpec((n, k), lambda i: (0, 0))],
        out_specs=pl.BlockSpec((tm, n), lambda i: (i, 0)),
        out_shape=jax.ShapeDtypeStruct((m, n), F32),
        compiler_params=_params(dimension_semantics=("arbitrary",)),
    )(a, b)


def _mm_tn_call(a, b, name, col_shards=False):
    m, k = a.shape
    n = b.shape[1]
    tm, tk = _row_tile(m), min(k, 1024)
    tn = n // NDEV if col_shards else min(n, 1024)

    def body(a_ref, b_ref, out_ref, acc_s):
        s = pl.program_id(2)

        @pl.when(s == 0)
        def _():
            acc_s[...] = jnp.zeros_like(acc_s)

        acc_s[...] += lax.dot_general(a_ref[...].astype(BF16), b_ref[...].astype(BF16), (((0,), (0,)), ((), ())),
                                      preferred_element_type=F32)

        @pl.when(s == m // tm - 1)
        def _():
            out_ref[...] = acc_s[...].astype(BF16).reshape(out_ref.shape)

    if col_shards:
        out_spec, out_shape = pl.BlockSpec((1, tk, tn), lambda i, j, s: (j, i, 0)), (NDEV, k, tn)
    else:
        out_spec, out_shape = pl.BlockSpec((tk, tn), lambda i, j, s: (i, j)), (k, n)
    return pl.pallas_call(
        body, name=name, grid=(k // tk, n // tn, m // tm),
        in_specs=[pl.BlockSpec((tm, tk), lambda i, j, s: (s, i)), pl.BlockSpec((tm, tn), lambda i, j, s: (s, j))],
        out_specs=out_spec, out_shape=jax.ShapeDtypeStruct(out_shape, BF16),
        scratch_shapes=[pltpu.VMEM((tk, tn), F32)],
        compiler_params=_params(dimension_semantics=("arbitrary", "arbitrary", "arbitrary")),
    )(a, b)


def _loss_call(y, target):
    m = y.shape[0]
    tm = _row_tile(m)

    def body(y_ref, t_ref, loss_ref, dy_ref):
        @pl.when(pl.program_id(0) == 0)
        def _():
            loss_ref[...] = jnp.zeros_like(loss_ref)

        err = y_ref[...] - t_ref[...]
        dy_ref[...] = err * (1.0 / D)
        per_row = jnp.mean(err * err, axis=-1, keepdims=True)
        loss_ref[...] += jnp.broadcast_to(0.5 * jnp.sum(per_row, axis=0, keepdims=True), (1, LANES))

    return pl.pallas_call(
        body, name="loss", grid=(m // tm,),
        in_specs=[pl.BlockSpec((tm, D), lambda i: (i, 0))] * 2,
        out_specs=[pl.BlockSpec((1, LANES), lambda i: (0, 0)), pl.BlockSpec((tm, D), lambda i: (i, 0))],
        out_shape=[jax.ShapeDtypeStruct((1, LANES), F32), jax.ShapeDtypeStruct((m, D), F32)],
        compiler_params=_params(dimension_semantics=("arbitrary",)),
    )(y, target)


def _adamw_call(w, g, m, v, name):
    shape = w.shape
    cols = shape[-1] if w.ndim > 1 else w.size
    rows = w.size // cols
    tr = rows if (rows <= 512 or rows % 512) else 512
    c1, c2 = 1.0 - ADAM_B1 ** ADAM_STEP, 1.0 - ADAM_B2 ** ADAM_STEP

    def body(w_ref, g_ref, m_ref, v_ref, d_ref, nm_ref, nv_ref):
        g_ = g_ref[...]
        nm = ADAM_B1 * m_ref[...] + (1.0 - ADAM_B1) * g_
        nv = ADAM_B2 * v_ref[...] + (1.0 - ADAM_B2) * (g_ * g_)
        d_ref[...] = -ADAM_LR * ((nm / c1) / (jnp.sqrt(nv / c2) + ADAM_EPS) + ADAM_WD * w_ref[...])
        nm_ref[...], nv_ref[...] = nm, nv

    spec = pl.BlockSpec((tr, cols), lambda i: (i, 0))
    outs = pl.pallas_call(
        body, name=name, grid=(rows // tr,), in_specs=[spec] * 4, out_specs=[spec] * 3,
        out_shape=[jax.ShapeDtypeStruct((rows, cols), F32)] * 3,
        compiler_params=_params(dimension_semantics=("arbitrary",)),
    )(*(t.reshape(rows, cols) for t in (w, g, m, v)))
    return tuple(o.reshape(shape) for o in outs)


def _sum_tile(rows):
    for cand in (2048, 1024, 512, 256, 128):
        if rows > cand and rows % cand == 0:
            return cand
    return rows


def _pair_sum_call(g, got, core, name):
    rows, cols = g.shape[1:]
    tr = _sum_tile(rows)

    def body(core_ref, g_ref, got_ref, out_ref):
        del core_ref
        out_ref[...] = (g_ref[...].astype(F32) + got_ref[...].astype(F32)).astype(BF16)

    grid_spec = pltpu.PrefetchScalarGridSpec(
        num_scalar_prefetch=1, grid=(4, rows // tr),
        in_specs=[pl.BlockSpec((1, tr, cols), lambda ch, t, core_ref: (2 * ch + core_ref[0], t, 0)),
                  pl.BlockSpec((1, tr, cols), lambda ch, t, core_ref: (ch, t, 0))],
        out_specs=pl.BlockSpec((1, tr, cols), lambda ch, t, core_ref: (ch, t, 0)))
    return pl.pallas_call(
        body, name=name, grid_spec=grid_spec, out_shape=jax.ShapeDtypeStruct((4, rows, cols), BF16),
        compiler_params=_params(dimension_semantics=("arbitrary", "arbitrary")),
    )(jnp.asarray(core, jnp.int32).reshape(1), g, got)


def _sum_call(parts, out_dtype, name):
    rows, cols = parts[0][0].shape[1:]
    tr = _sum_tile(rows)
    index = jnp.stack([jnp.asarray(i, jnp.int32) for _, i in parts])

    def body(idx_ref, *refs):
        del idx_ref
        acc = refs[0][0].astype(F32)
        for r in refs[1:-1]:
            acc = acc + r[0].astype(F32)
        refs[-1][...] = acc.astype(out_dtype)

    grid_spec = pltpu.PrefetchScalarGridSpec(
        num_scalar_prefetch=1, grid=(rows // tr,),
        in_specs=[pl.BlockSpec((1, tr, cols), lambda t, idx, n=n: (idx[n], t, 0)) for n in range(len(parts))],
        out_specs=pl.BlockSpec((tr, cols), lambda t, idx: (t, 0)))
    return pl.pallas_call(
        body, name=name, grid_spec=grid_spec, out_shape=jax.ShapeDtypeStruct((rows, cols), out_dtype),
        compiler_params=_params(dimension_semantics=("arbitrary",)),
    )(index, *(a for a, _ in parts))


def _place():
    return lax.axis_index("x"), lax.axis_index("y"), lax.axis_index("c")


def _any_specs(n):
    return [pl.BlockSpec(memory_space=pl.ANY)] * n


def _all_gather_call(xs, name):
    n = len(xs)

    def body(*refs):
        for phase in _gather_phases(refs[:n], refs[n:2 * n], *refs[2 * n:]):
            phase()

    return pl.pallas_call(
        body, name=name, in_specs=_any_specs(n), out_specs=_any_specs(n),
        out_shape=_gather_shapes(xs), scratch_shapes=_gather_sems(n),
    )(*xs)


def _gather_shapes(xs):
    return [jax.ShapeDtypeStruct((NDEV,) + x.shape, x.dtype) for x in xs]


def _gather_sems(n):
    return [pltpu.SemaphoreType.DMA((7 * n,)), pltpu.SemaphoreType.DMA((7 * n,)), pltpu.SemaphoreType.DMA((n,))]


def _gather_phases(x_refs, out_refs, send_sems, recv_sems, local_sems):
    n = len(x_refs)
    ax, ay, ac = _place()
    me, sibling = (ax, ay, ac), (ax, ay, 1 - ac)
    chips = [(1 - ax, ay), (ax, 1 - ay), (1 - ax, 1 - ay)]

    def copy(a, k, block, to, src=None):
        slot = out_refs[a].at[4 * block[0] + 2 * block[1] + block[2]]
        return pltpu.make_async_remote_copy(
            src_ref=slot if src is None else src, dst_ref=slot,
            send_sem=send_sems.at[7 * a + k], recv_sem=recv_sems.at[7 * a + k], device_id=to, device_id_type=MESH)

    local = [pltpu.make_async_copy(x_refs[a], out_refs[a].at[4 * ax + 2 * ay + ac], local_sems.at[a]) for a in range(n)]
    first = []
    for a in range(n):
        first.append(copy(a, 0, me, sibling, src=x_refs[a]))
        first += [copy(a, 1 + j, me, (*chip, ac), src=x_refs[a]) for j, chip in enumerate(chips)]
    passed = [copy(a, 4 + j, (*chip, ac), sibling) for j, chip in enumerate(chips) for a in range(n)]

    def send():
        for cp in local + first:
            cp.start()

    def forward():
        for j, chip in enumerate(chips):
            for a in range(n):
                copy(a, 1 + j, (*chip, ac), me).wait_recv()
                passed[j * n + a].start()

    def finish():
        for a in range(n):
            copy(a, 0, sibling, me).wait_recv()
            for j, chip in enumerate(chips):
                copy(a, 4 + j, (*chip, 1 - ac), me).wait_recv()
        for cp in first + passed:
            cp.wait_send()
        for cp in local:
            cp.wait()

    return send, forward, finish


def _swap_sibling_call(xs, name):
    n = len(xs)

    def body(*refs):
        x_refs, out_refs, (send_sems, recv_sems) = refs[:n], refs[n:2 * n], refs[2 * n:]
        ax, ay, ac = _place()
        for a in range(n):
            for chip in range(4):
                pltpu.make_async_remote_copy(src_ref=x_refs[a].at[2 * chip + 1 - ac], dst_ref=out_refs[a].at[chip],
                                             send_sem=send_sems.at[a], recv_sem=recv_sems.at[a],
                                             device_id=(ax, ay, 1 - ac), device_id_type=MESH).start()
        for a in range(n):
            four = x_refs[a].at[pl.ds(0, 4)]
            pltpu.make_async_remote_copy(src_ref=four, dst_ref=out_refs[a], send_sem=send_sems.at[a], recv_sem=recv_sems.at[a],
                                         device_id=(ax, ay, 1 - ac), device_id_type=MESH).wait()

    return pl.pallas_call(
        body, name=name, in_specs=_any_specs(n), out_specs=_any_specs(n),
        out_shape=[jax.ShapeDtypeStruct((4,) + x.shape[1:], x.dtype) for x in xs],
        scratch_shapes=[pltpu.SemaphoreType.DMA((n,)), pltpu.SemaphoreType.DMA((n,))],
    )(*xs)


def _swap_chips_call(xs, name):
    n = len(xs)

    def body(*refs):
        for phase in _chip_swap_phases(refs[:n], refs[n:2 * n], *refs[2 * n:]):
            phase()

    return pl.pallas_call(
        body, name=name, in_specs=_any_specs(n), out_specs=_any_specs(n),
        out_shape=_chip_swap_shapes(xs), scratch_shapes=_chip_swap_sems(n),
    )(*xs)


def _chip_swap_shapes(xs):
    return [jax.ShapeDtypeStruct((3,) + x.shape[1:], x.dtype) for x in xs]


def _chip_swap_sems(n):
    return [pltpu.SemaphoreType.DMA((3 * n,)), pltpu.SemaphoreType.DMA((3 * n,))]


def _chip_swap_phases(x_refs, out_refs, send_sems, recv_sems):
    ax, ay, ac = _place()
    chips = [(1 - ax, ay), (ax, 1 - ay), (1 - ax, 1 - ay)]
    copies = [pltpu.make_async_remote_copy(src_ref=x_refs[a].at[2 * cx + cy], dst_ref=out_refs[a].at[j],
                                           send_sem=send_sems.at[3 * a + j], recv_sem=recv_sems.at[3 * a + j],
                                           device_id=(cx, cy, ac), device_id_type=MESH)
              for a in range(len(x_refs)) for j, (cx, cy) in enumerate(chips)]

    def send():
        for cp in copies:
            cp.start()

    def finish():
        for cp in copies:
            cp.wait()

    return send, finish


def _reduce_begin(gs, name):
    ac = lax.axis_index("c")
    got = _swap_sibling_call(gs, name + "_d2d")
    return got, [_pair_sum_call(g, t, ac, f"{name}_pair{a}") for a, (g, t) in enumerate(zip(gs, got))]


def _reduce_end(gs, got, from_chips, name):
    ax, ay, ac = _place()
    me, my_chip = 4 * ax + 2 * ay + ac, 2 * ax + ay
    return [_sum_call([(g, me), (t, my_chip), (f, 0), (f, 1), (f, 2)], F32, f"{name}_total{a}")
            for a, (g, t, f) in enumerate(zip(gs, got, from_chips))]


SMALL = ("norm1_g", "conv_w", "a_log", "dt_bias", "dn_out_g", "sb_q_g", "sb_k_g", "sg_v_g", "sg_w", "sg_b", "norm2_g")
WEIGHTS = ("norm1_g", "w_in", "conv_w", "a_log", "dt_bias", "dn_out_g", "sb_q_g", "sb_k_g", "sg_v_g", "sg_w", "sg_b",
           "w_out", "norm2_g", "w_ff1", "w_ff2")
SMALL_SHAPE = {"norm1_g": (D,), "conv_w": (4, 3 * DN_W), "a_log": (NH,), "dt_bias": (NH,), "dn_out_g": (128,), "sb_q_g": (64,),
               "sb_k_g": (64,), "sg_v_g": (SG_W,), "sg_w": (NH, 128, 128), "sg_b": (NH, 128), "norm2_g": (D,)}


def _size(shape):
    n = 1
    for s in shape:
        n *= s
    return n


def _to_rows(flat, multiple):
    pad = (-flat.shape[0]) % (LANES * multiple)
    return jnp.pad(flat, (0, pad)).reshape(-1, LANES)


def _conv_by_pair(conv):
    return conv.reshape(4, 3, 2, 256).transpose(0, 2, 1, 3).reshape(4, 3 * DN_W)


def kernel(x, norm1_g, w_in, conv_w, a_log, dt_bias, dn_out_g, sb_q_g, sb_k_g, sg_v_g, sg_w, sg_b, w_out, norm2_g, w_ff1, w_ff2, loss_target, m_norm1_g, m_w_in, m_conv_w, m_a_log, m_dt_bias, m_dn_out_g, m_sb_q_g, m_sb_k_g, m_sg_v_g, m_sg_w, m_sg_b, m_w_out, m_norm2_g, m_w_ff1, m_w_ff2, v_norm1_g, v_w_in, v_conv_w, v_a_log, v_dt_bias, v_dn_out_g, v_sb_q_g, v_sb_k_g, v_sg_v_g, v_sg_w, v_sg_b, v_w_out, v_norm2_g, v_w_ff1, v_w_ff2):
    given = dict(norm1_g=norm1_g, w_in=w_in, conv_w=conv_w, a_log=a_log, dt_bias=dt_bias, dn_out_g=dn_out_g, sb_q_g=sb_q_g,
                 sb_k_g=sb_k_g, sg_v_g=sg_v_g, sg_w=sg_w, sg_b=sg_b, w_out=w_out, norm2_g=norm2_g, w_ff1=w_ff1, w_ff2=w_ff2)
    mom = dict(norm1_g=m_norm1_g, w_in=m_w_in, conv_w=m_conv_w, a_log=m_a_log, dt_bias=m_dt_bias, dn_out_g=m_dn_out_g,
               sb_q_g=m_sb_q_g, sb_k_g=m_sb_k_g, sg_v_g=m_sg_v_g, sg_w=m_sg_w, sg_b=m_sg_b, w_out=m_w_out, norm2_g=m_norm2_g,
               w_ff1=m_w_ff1, w_ff2=m_w_ff2)
    var = dict(norm1_g=v_norm1_g, w_in=v_w_in, conv_w=v_conv_w, a_log=v_a_log, dt_bias=v_dt_bias, dn_out_g=v_dn_out_g,
               sb_q_g=v_sb_q_g, sb_k_g=v_sb_k_g, sg_v_g=v_sg_v_g, sg_w=v_sg_w, sg_b=v_sg_b, w_out=v_w_out, norm2_g=v_norm2_g,
               w_ff1=v_w_ff1, w_ff2=v_w_ff2)
    B, T, _ = x.shape
    M = B * T
    ax, ay, ac = _place()
    me = 4 * ax + 2 * ay + ac
    table_fwd, table_back = _row_tables()

    send = []
    for l in range(2):
        w_in_t = jnp.pad(w_in[l].T, ((0, IN_SHARD_PAD - IN_SHARD), (0, 0)))
        send.append([w_in_t.astype(BF16), w_out[l].astype(BF16), w_ff1[l].astype(BF16), w_ff2[l].astype(BF16)])
    first_in, conv_rows = _all_gather_call([send[0][0], _to_rows(conv_w.reshape(-1), 8)], "gather_first")
    conv_full = conv_rows.reshape(NDEV, -1)[:, :conv_w.size].reshape(NDEV, 2, 4, -1).transpose(1, 2, 0, 3).reshape(2, 4, 3 * DN_W)
    gathered = [[first_in, None, None, None], [None] * 4]

    pad_vec = lambda v: jnp.zeros((1, LANES), F32).at[0, :v.shape[0]].set(v)
    layer = []
    for l in range(2):
        layer.append(dict(
            g1=norm1_g[l].reshape(1, D), g2=norm2_g[l].reshape(1, D), conv=_conv_by_pair(conv_full[l]),
            a_log=pad_vec(a_log[l]), dt_bias=pad_vec(dt_bias[l]), dn_g=dn_out_g[l].reshape(1, LANES),
            sb_qg=jnp.tile(sb_q_g[l], 2).reshape(1, LANES), sb_kg=jnp.tile(sb_k_g[l], 2).reshape(1, LANES),
            sg_g=sg_v_g[l].reshape(1, SG_W), sg_w=sg_w[l], sg_bias=jnp.repeat(sg_b[l].T, 64, axis=1)))

    cur = x.reshape(M, D)
    saved = []
    for l, p in enumerate(layer):
        p["wt"] = _row_perm_call(gathered[l][0].reshape(NDEV * IN_SHARD_PAD, D), table_fwd, "pack_w_in")
        p_dn, p_sb, p_sg, p_ab, h = _in_proj_call(cur, p["g1"], p["wt"])
        mix, arrived = _dn_fwd_call(p_dn, p_ab, p["conv"], p["a_log"], p["dt_bias"], p["dn_g"], B, T,
                                    gather=send[0][1:] if l == 0 else [])
        if l == 0:
            gathered[0][1:] = list(arrived)
        p["w_out"], p["w1"], p["w2"] = gathered[l][1].reshape(D, D), gathered[l][2], gathered[l][3].reshape(DFF, D)
        mix, arrived = _sb_fwd_call(p_sb, mix, p["sb_qg"], p["sb_kg"], B, T, gather=send[1] if l == 0 else [])
        if l == 0:
            gathered[1] = list(arrived)
        mix = _sg_fwd_call(p_sg, mix, p["sg_g"], p["sg_w"], p["sg_bias"], B, T)
        x1 = _out_proj_call(mix, p["w_out"], cur)
        x2 = _ffn_fwd_call(x1, p["g2"], p["w1"], p["w2"])
        saved.append(dict(x0=cur, p_dn=p_dn, p_sb=p_sb, p_sg=p_sg, p_ab=p_ab, h=h, mix=mix, x1=x1))
        cur = x2
    loss_part, dy = _loss_call(cur, loss_target.reshape(M, D))
    loss = lax.psum(loss_part[0, 0], ("x", "y", "c"))

    big_grads = [[None] * 4, [None] * 4]
    small_grads = {n: [None, None] for n in SMALL}
    for l in (1, 0):
        p, s = layer[l], saved[l]
        dx1, da, r, h2, dg2 = _ffn_bwd_call(s["x1"], dy, p["g2"], p["w1"], p["w2"])
        big_grads[l][2] = _mm_tn_call(h2, da, "grad_w_ff1", col_shards=True)
        big_grads[l][3] = _mm_tn_call(r, dy, "grad_w_ff2").reshape(NDEV, FF_SHARD, D)
        dmix = _mm_nt_call(dx1, p["w_out"], "dmix")
        big_grads[l][1] = _mm_tn_call(s["mix"], dx1, "grad_w_out").reshape(NDEV, D // NDEV, D)
        if l == 0:
            early = big_grads[1] + big_grads[0][1:]
            early_got, early_sums = _reduce_begin(early, "reduce_early")
        (d_dn, d_ab, dcw, dalog, ddtb, ddn_g), early_from = _dn_bwd_call(
            s["p_dn"], s["p_ab"], dmix, p["conv"], p["a_log"], p["dt_bias"], p["dn_g"], B, T, swap=early_sums if l == 0 else ())
        d_sb, dqg, dkg = _sb_bwd_call(s["p_sb"], dmix, p["sb_qg"], p["sb_kg"], B, T)
        d_sg, dsg_g, dsg_w, dsg_b = _sg_bwd_call(s["p_sg"], dmix, p["sg_g"], p["sg_w"], p["sg_bias"], B, T)
        dsections = (d_dn, d_sb, d_sg, d_ab)
        dy, dg1 = _in_proj_bwd_call(dsections, p["wt"], s["x0"], p["g1"], dx1)
        dwt = _in_proj_grad_call(dsections, s["h"])
        big_grads[l][0] = _row_perm_call(dwt, table_back, "unpack_grad_w_in").reshape(NDEV, IN_SHARD_PAD, D)
        for n, val in (("norm1_g", dg1[0]), ("conv_w", dcw.transpose(1, 0, 2).reshape(4, 3 * DN_W)), ("a_log", dalog[0, :NH]),
                       ("dt_bias", ddtb[0, :NH]), ("dn_out_g", ddn_g[0]), ("sb_q_g", dqg[0, :64]), ("sb_k_g", dkg[0, :64]),
                       ("sg_v_g", dsg_g[0]), ("sg_w", dsg_w), ("sg_b", dsg_b[:, :NH].T), ("norm2_g", dg2[0])):
            small_grads[n][l] = val
    grad_x = dy.reshape(B, T, D)

    last = big_grads[0][:1]
    last_got, last_sums = _reduce_begin(last, "reduce_last")
    mine0 = _reduce_end(last, last_got, _swap_chips_call(last_sums, "reduce_last_ici"), "reduce_last")
    mine1 = _reduce_end(early, early_got, early_from, "reduce_early")
    grads = {"w_in": jnp.stack([mine0[0][:IN_SHARD].T, mine1[0][:IN_SHARD].T]), "w_out": jnp.stack([mine1[4], mine1[1]]),
             "w_ff1": jnp.stack([mine1[5], mine1[2]]), "w_ff2": jnp.stack([mine1[6], mine1[3]])}
    small_flat = jnp.concatenate([jnp.stack(small_grads[n]).reshape(-1) for n in SMALL])
    everyone, = _all_gather_call([_to_rows(small_flat, 8)], "gather_small_grads")
    small_sum = _sum_call([(everyone, k) for k in range(NDEV)], F32, "sum_small_grads").reshape(-1)
    off = 0
    for n in SMALL:
        sz = 2 * _size(SMALL_SHAPE[n])
        grads[n] = small_sum[off:off + sz].reshape((2,) + SMALL_SHAPE[n])
        off += sz
    cshard = conv_w.shape[-1]
    grads["conv_w"] = lax.dynamic_slice_in_dim(grads["conv_w"], me * cshard, cshard, axis=2)

    deltas, new_m, new_v = {}, {}, {}
    for n in WEIGHTS:
        deltas[n], new_m[n], new_v[n] = _adamw_call(given[n], grads[n], mom[n], var[n], "adamw_" + n)
    return (loss, grad_x, *[grads[n] for n in WEIGHTS], *[deltas[n] for n in WEIGHTS], *[new_m[n] for n in WEIGHTS],
            *[new_v[n] for n in WEIGHTS])
```

```python
import functools

import numpy as np

import jax
import jax.numpy as jnp
from jax import lax
from jax.experimental import pallas as pl
from jax.experimental.pallas import tpu as pltpu

F32, BF16 = jnp.float32, jnp.bfloat16
EPS = 1e-6
LANES = 128
D = 1024
DFF = 4096
NH = 4
DN_W, SB_W, SG_W = 512, 256, 256
IN_DIM = 3336
NDEV = 8
IN_SHARD = IN_DIM // NDEV
IN_SHARD_PAD = 432
FF_SHARD = DFF // NDEV
DN_OFF, SB_OFF, SG_OFF, AB_OFF, NPACK = 0, 2048, 2816, 3328, 3456
SECTIONS = ((DN_OFF, 2048), (SB_OFF, 768), (SG_OFF, 512), (AB_OFF, 128))
SB_SCALE = 64 ** -0.5
DN_SCALE = 128 ** -0.5
VMEM_LIMIT = 56 * 1024 * 1024
VMEM_LIMIT_MAX = 62 * 1024 * 1024
ADAM_LR, ADAM_B1, ADAM_B2, ADAM_EPS, ADAM_WD, ADAM_STEP = 0.001, 0.9, 0.999, 1e-08, 0.01, 10
MESH = pl.DeviceIdType.MESH


def _iota(shape, dim):
    return lax.broadcasted_iota(jnp.int32, shape, dim)


def _params(**kw):
    return pltpu.CompilerParams(vmem_limit_bytes=VMEM_LIMIT, **kw)


NN, NT, TN = ((1,), (0,)), ((1,), (1,)), ((0,), (0,))


def _mm(a, b, dims):
    return lax.dot_general(a.astype(BF16), b.astype(BF16), (dims, ((), ())), preferred_element_type=F32)


def _plain(a, b, dims):
    return (a.T if dims == TN else a), (b.T if dims == NT else b)


def _mmx(a, b, dims):
    return _mm(*_plain(a, b, dims), NN)


@jax.custom_vjp
def _dot(a, b):
    return _mmx(a, b, NN)


def _dot_fwd(a, b):
    return _dot(a, b), (a, b)


def _dot_bwd(res, g):
    a, b = res
    return _mmx(g, b, NT).astype(a.dtype), _mmx(a, g, TN).astype(b.dtype)


_dot.defvjp(_dot_fwd, _dot_bwd)


@jax.custom_vjp
def _dot_nt(a, b):
    return _mmx(a, b, NT)


def _dot_nt_fwd(a, b):
    return _dot_nt(a, b), (a, b)


def _dot_nt_bwd(res, g):
    a, b = res
    return _mmx(g, b, NN).astype(a.dtype), _mmx(g, a, TN).astype(b.dtype)


_dot_nt.defvjp(_dot_nt_fwd, _dot_nt_bwd)


@jax.custom_vjp
def _dot_tn(a, b):
    return _mmx(a, b, TN)


def _dot_tn_fwd(a, b):
    return _dot_tn(a, b), (a, b)


def _dot_tn_bwd(res, g):
    a, b = res
    return _mmx(b, g, NT).astype(a.dtype), _mmx(a, g, NN).astype(b.dtype)


_dot_tn.defvjp(_dot_tn_fwd, _dot_tn_bwd)


def _split(x):
    hi = x.astype(BF16)
    return hi, (x - hi.astype(F32)).astype(BF16)


def _mm3(a, b, dims):
    a, b = _plain(a, b, dims)
    (ah, al), (bh, bl) = _split(a), _split(b)
    mm = lambda x, y: jnp.dot(x, y, preferred_element_type=F32)
    return mm(ah, bh) + (mm(ah, bl) + mm(al, bh))


def _mm_ones(ones, x, ones_left):
    hi, lo = _split(x)
    mm = (lambda t: jnp.dot(ones, t, preferred_element_type=F32)) if ones_left else \
         (lambda t: jnp.dot(t, ones, preferred_element_type=F32))
    return mm(hi) + mm(lo)


def _pair_ones(kind, transposed):
    row, col = _iota((128, 128), 0), _iota((128, 128), 1)
    m = (row // 64) == (col // 64)
    if kind == "running":
        m = jnp.logical_and(m, (col >= row) if transposed else (col <= row))
    return jnp.where(m, 1.0, 0.0).astype(BF16)


@functools.partial(jax.custom_vjp, nondiff_argnums=(0,))
def _chunk_sum(kind, x):
    return _mm_ones(_pair_ones(kind, False), x, True)


def _chunk_sum_fwd(kind, x):
    return _chunk_sum(kind, x), None


def _chunk_sum_bwd(kind, _, g):
    return (_mm_ones(_pair_ones(kind, True), g, True),)


_chunk_sum.defvjp(_chunk_sum_fwd, _chunk_sum_bwd)


def _tri_ones(n, transposed):
    row, col = _iota((n, n), 0), _iota((n, n), 1)
    return jnp.where((row < col) if transposed else (row > col), 1.0, 0.0).astype(BF16)


@jax.custom_vjp
def _suffix_sum(x):
    return _mm_ones(_tri_ones(x.shape[1], False), x, False)


def _suffix_sum_fwd(x):
    return _suffix_sum(x), None


def _suffix_sum_bwd(_, g):
    return (_mm_ones(_tri_ones(g.shape[1], True), g, False),)


_suffix_sum.defvjp(_suffix_sum_fwd, _suffix_sum_bwd)


def _sigmoid(x):
    return jax.nn.sigmoid(x)


def _silu(x):
    return x * _sigmoid(x)


def _softplus(x):
    return jnp.maximum(x, 0.0) + jnp.log1p(jnp.exp(-jnp.abs(x)))


def _gelu(x):
    return 0.5 * x * (1.0 + jnp.tanh(0.7978845608028654 * (x + 0.044715 * (x * x * x))))


def _rms(x, gain):
    return x * lax.rsqrt(jnp.mean(x * x, axis=-1, keepdims=True) + EPS) * gain


def _shift_down_impl(x, k):
    return jnp.where(_iota(x.shape, 0) >= k, pltpu.roll(x, k, 0), 0.0)


def _shift_up_impl(x, k):
    n = x.shape[0]
    return jnp.where(_iota(x.shape, 0) < n - k, pltpu.roll(x, n - k, 0), 0.0)


@functools.partial(jax.custom_vjp, nondiff_argnums=(1,))
def _shift_down(x, k):
    return _shift_down_impl(x, k)


def _shift_down_fwd(x, k):
    return _shift_down_impl(x, k), None


def _shift_down_bwd(k, _, g):
    return (_shift_up_impl(g, k),)


_shift_down.defvjp(_shift_down_fwd, _shift_down_bwd)


def _lane_pick(x, idx):
    return jnp.sum(jnp.where(_iota(x.shape, 1) == idx, x, 0.0), axis=-1, keepdims=True)


def _dn_conv(x, w0, w1, w2, w3, l2_scale):
    y = _silu(w3 * x + w2 * _shift_down(x, 1) + w1 * _shift_down(x, 2) + w0 * _shift_down(x, 3))
    if l2_scale is None:
        return y
    return y * lax.rsqrt(jnp.sum(y * y, axis=-1, keepdims=True) + EPS) * l2_scale


def _dn_gate(a, b, a_log, dt_bias):
    return -jnp.exp(a_log) * _softplus(a + dt_bias), _sigmoid(b)


def _same_head(shape):
    return (_iota(shape, 0) < LANES) == (_iota(shape, 1) < LANES)


def _bd(r2):
    return jnp.where(_same_head((2 * LANES, 2 * LANES)), jnp.concatenate([r2, r2], axis=0), 0.0)


def _bd_t(y2):
    t = y2.T
    return jnp.where(_same_head((2 * LANES, 2 * LANES)), jnp.concatenate([t, t], axis=1), 0.0)


def _pair_prod(kind, a2, b2, mm):
    if kind == NN:
        return mm(a2, _bd(b2))
    if kind == NT:
        return mm(a2, _bd_t(b2))
    full = mm(a2.T, b2)
    return jnp.concatenate([full[:LANES, :LANES], full[LANES:, LANES:]], axis=1)


_MM1 = lambda x, y: _mm(x, y, NN)
_MM3 = lambda x, y: _mm3(x, y, NN)


def _pair_vjp_rule(kind, a2, b2, g, mm):
    if kind == NN:
        return _pair_prod(NT, g, b2, mm), _pair_prod(TN, a2, g, mm)
    if kind == NT:
        return _pair_prod(NN, g, b2, mm), _pair_prod(TN, g, a2, mm)
    return _pair_prod(NT, b2, g, mm), _pair_prod(NN, a2, g, mm)


@functools.partial(jax.custom_vjp, nondiff_argnums=(0,))
def _pdot(kind, a2, b2):
    return _pair_prod(kind, a2, b2, _MM1)


def _pdot_fwd(kind, a2, b2):
    return _pdot(kind, a2, b2), (a2, b2)


def _pdot_bwd(kind, res, g):
    return _pair_vjp_rule(kind, *res, g, _MM1)


_pdot.defvjp(_pdot_fwd, _pdot_bwd)


def _unit_lower_inverse(lower):
    n = lower.shape[0]
    nk = -lower
    inv = jnp.where(_iota(lower.shape, 0) == jnp.bitwise_and(_iota(lower.shape, 1), n - 1), 1.0, 0.0) + nk
    for _ in range(5):
        nk = _pair_prod(NN, nk, nk, _MM1)
        inv = inv + _pair_prod(NN, inv, nk, _MM1)
    return inv


@jax.custom_vjp
def _solve_with(lower, inv, rhs):
    return _pair_prod(NN, inv, rhs, _MM3)


def _solve_with_fwd(lower, inv, rhs):
    x = _pair_prod(NN, inv, rhs, _MM3)
    return x, (inv, x)


def _solve_with_bwd(res, g):
    inv, x = res
    d_rhs = _pair_prod(TN, inv, g, _MM3)
    return -_pair_prod(NT, d_rhs, x, _MM3), jnp.zeros_like(inv), d_rhs


_solve_with.defvjp(_solve_with_fwd, _solve_with_bwd)


def _dn_local(q, k, v, g, beta, inv=None):
    shape = (LANES, 2 * LANES)
    row, col = _iota(shape, 0), jnp.bitwise_and(_iota(shape, 1), LANES - 1)
    same = (row // 64) == (col // 64)
    tri_incl = jnp.logical_and(same, col <= row)
    tri_strict = jnp.logical_and(same, col < row)
    first = row < 64
    gc = _chunk_sum("running", g)
    gl = _chunk_sum("total", g)
    diff = gc - jnp.concatenate([gc[:, :LANES].T, gc[:, LANES:].T], axis=1)
    decay = jnp.where(tri_incl, jnp.exp(jnp.where(tri_incl, diff, 0.0)), 0.0)
    egc = jnp.exp(gc)
    lower = jnp.where(tri_strict, beta * _pdot(NT, k, k) * decay, 0.0)
    if inv is None:
        inv = _unit_lower_inverse(lower)
    u_val = _solve_with(lower, inv, v * beta)
    w_dec = _solve_with(lower, inv, k * (beta * egc))
    qk = jnp.where(tri_incl, _pdot(NT, q, k) * decay, 0.0)
    q_dec = q * egc
    k_dec = k * jnp.exp(gl - gc)
    cd1 = jnp.exp(jnp.sum(jnp.where(first, g, 0.0), axis=0, keepdims=True))
    cd2 = jnp.exp(jnp.sum(jnp.where(first, 0.0, g), axis=0, keepdims=True))
    return (u_val, w_dec, qk, q_dec, k_dec, cd1, cd2), inv


def _dn_state(u_val, w_dec, qk, q_dec, k_dec, cd1, cd2, s0):
    first = _iota((LANES, 2 * LANES), 0) < 64
    u1 = u_val - _pdot(NN, w_dec, s0)
    s1 = s0 * cd1 + _pdot(TN, jnp.where(first, k_dec, 0.0), u1)
    u2 = u_val - _pdot(NN, w_dec, s1)
    u_new = jnp.where(first, u1, u2)
    s2 = s1 * cd2 + _pdot(TN, jnp.where(first, 0.0, k_dec), u_new)
    o = jnp.where(first, _pdot(NN, q_dec, s0), _pdot(NN, q_dec, s1)) + _pdot(NN, qk, u_new)
    return o, s2


def _dn_post(o, z, gain):
    return _rms(o, gain) * _silu(z)


def _dn_gate_in(ab_ref, alog_ref, dtb_ref, h):
    ab = ab_ref[...]
    return _lane_pick(ab, h), _lane_pick(ab, h + NH), _lane_pick(alog_ref[...], h), _lane_pick(dtb_ref[...], h)


_DN_L2 = (DN_SCALE, 1.0, None)
DN_HPS = 2
DN_BLK = 4 * DN_HPS * LANES
_DN_COLS = tuple(slice(i * LANES, (i + 1) * LANES) for i in range(DN_HPS))


def _dn_in_cols(s, i):
    return slice((s * DN_HPS + i) * LANES, (s * DN_HPS + i + 1) * LANES)


def _dn_taps(cw_ref, s, i):
    return tuple(cw_ref[t:t + 1, _dn_in_cols(s, i)] for t in range(4))


def _dn_pack_gate(vals):
    lane = _iota((1, LANES), 1)
    out = 0.0
    for i, (g, beta) in enumerate(vals):
        out = out + jnp.where(lane == 2 * i, g, 0.0) + jnp.where(lane == 2 * i + 1, beta, 0.0)
    return out


def _pair_rows(n):
    return pl.ds(pl.multiple_of(n * 128, 128), 128)


def _dn_gate_rows(gate):
    head_a = _iota((1, DN_HPS * LANES), 1) < LANES
    return (jnp.where(head_a, _lane_pick(gate, 0), _lane_pick(gate, 2)),
            jnp.where(head_a, _lane_pick(gate, 1), _lane_pick(gate, 3)))


def _dn_gate_cols(dg, db):
    head_a = _iota((1, DN_HPS * LANES), 1) < LANES
    fold = lambda t: (jnp.sum(jnp.where(head_a, t, 0.0), axis=-1, keepdims=True),
                      jnp.sum(jnp.where(head_a, 0.0, t), axis=-1, keepdims=True))
    (dg_a, dg_b), (db_a, db_b) = fold(dg), fold(db)
    return [(dg_a, db_a), (dg_b, db_b)]


def _dn_in_specs(T):
    one = pl.Buffered(1)
    vec = pl.BlockSpec((1, LANES), lambda b, h: (0, 0))
    return [pl.BlockSpec((T, DN_BLK), lambda b, h: (b, h), pipeline_mode=one),
            pl.BlockSpec((T, LANES), lambda b, h: (b, 0), pipeline_mode=one),
            pl.BlockSpec((4, 3 * DN_HPS * LANES), lambda b, h: (0, h)), vec, vec, vec]


def _dn_fwd_call(proj_dn, proj_ab, conv_w, a_log, dt_bias, gain, B, T, gather=()):
    npair = T // 128
    ng = len(gather)
    nsteps = B * (NH // DN_HPS)

    def body(*refs):
        x_ref, ab_ref, cw_ref, alog_ref, dtb_ref, gain_ref = refs[:6]
        out_ref = refs[6 + ng]
        q_s, k_s, v_s, o_s, gate_s = refs[7 + 2 * ng:12 + 2 * ng]
        step_id = pl.program_id(0) * (NH // DN_HPS) + pl.program_id(1)
        if ng:
            send, forward, finish = _gather_phases(refs[6:6 + ng], refs[7 + ng:7 + 2 * ng], *refs[12 + 2 * ng:])
            pl.when(step_id == 0)(send)
            pl.when(step_id == nsteps // 2)(forward)
        hp = pl.program_id(1)
        gates = []
        for i, cs in enumerate(_DN_COLS):
            for s, (x_s, l2) in enumerate(zip((q_s, k_s, v_s), _DN_L2)):
                x_s[:, cs] = _dn_conv(x_ref[:, _dn_in_cols(s, i)], *_dn_taps(cw_ref, s, i), l2)
            gates.append(_dn_gate(*_dn_gate_in(ab_ref, alog_ref, dtb_ref, DN_HPS * hp + i)))
        gate_s[...] = _dn_pack_gate(gates)

        def local_of(pair):
            r = _pair_rows(pair)
            return _dn_local(q_s[r, :], k_s[r, :], v_s[r, :], *_dn_gate_rows(gate_s[r, :]))[0]

        def state_of(n, loc, state):
            o, s2 = _dn_state(*loc, state)
            o_s[_pair_rows(n), :] = o
            return s2

        def step(n, carry):
            loc, state = carry
            return local_of(n + 1), state_of(n, loc, state)

        loc, state = lax.fori_loop(0, npair - 1, step, (local_of(0), jnp.zeros((LANES, DN_HPS * LANES), F32)))
        state_of(npair - 1, loc, state)
        for i, cs in enumerate(_DN_COLS):
            out_ref[:, cs] = _dn_post(o_s[:, cs], x_ref[:, _dn_in_cols(3, i)], gain_ref[...])
        if ng:
            pl.when(step_id == nsteps - 1)(finish)

    wide = [pltpu.VMEM((T, DN_HPS * LANES), F32)]
    outs = pl.pallas_call(
        body, name="dn_fwd", grid=(B, NH // DN_HPS), in_specs=_dn_in_specs(T) + _any_specs(ng),
        out_specs=[pl.BlockSpec((T, DN_HPS * LANES), lambda b, h: (b, h), pipeline_mode=pl.Buffered(1))] + _any_specs(ng),
        out_shape=[jax.ShapeDtypeStruct((B * T, D), F32)] + _gather_shapes(gather),
        scratch_shapes=wide * 4 + [pltpu.VMEM((T, LANES), F32)] + (_gather_sems(ng) if ng else []),
        compiler_params=_params(dimension_semantics=("arbitrary", "arbitrary")),
    )(proj_dn, proj_ab, conv_w, a_log, dt_bias, gain, *gather)
    return outs[0], outs[1:]


def _dn_bwd_call(proj_dn, proj_ab, dmix, conv_w, a_log, dt_bias, gain, B, T, swap=()):
    npair = T // 128
    ns = len(swap)
    nsteps = B * (NH // DN_HPS)

    def body(*refs):
        x_ref, ab_ref, cw_ref, alog_ref, dtb_ref, gain_ref, do_ref = refs[:7]
        dx_ref, dab_ref, dcw_ref, dalog_ref, ddtb_ref, dgain_ref = refs[7 + ns:13 + ns]
        q_s, k_s, v_s, o_s, gate_s, dgate_s, st_s, inv_s, dcd_s = refs[13 + 2 * ns:22 + 2 * ns]
        dloc_s = refs[22 + 2 * ns:27 + 2 * ns]
        b_i, hp = pl.program_id(0), pl.program_id(1)
        step_id = b_i * (NH // DN_HPS) + hp
        if ns:
            send, finish = _chip_swap_phases(refs[7:7 + ns], refs[13 + ns:13 + 2 * ns], *refs[27 + 2 * ns:])
            pl.when(step_id == 0)(send)
        gates = []
        for i, cs in enumerate(_DN_COLS):
            for s, (x_s, l2) in enumerate(zip((q_s, k_s, v_s), _DN_L2)):
                x_s[:, cs] = _dn_conv(x_ref[:, _dn_in_cols(s, i)], *_dn_taps(cw_ref, s, i), l2)
            gates.append(_dn_gate(*_dn_gate_in(ab_ref, alog_ref, dtb_ref, DN_HPS * hp + i)))
        gate_s[...] = _dn_pack_gate(gates)

        def pair_in(r):
            return (q_s[r, :], k_s[r, :], v_s[r, :]) + _dn_gate_rows(gate_s[r, :])

        def local_of(pair, known_inverse=False):
            loc, inv = _dn_local(*pair_in(_pair_rows(pair)), inv_s[pair] if known_inverse else None)
            if not known_inverse:
                inv_s[pair] = inv
            return loc

        def state_of(n, loc, state):
            st_s[n] = state
            o, s2 = _dn_state(*loc, state)
            o_s[_pair_rows(n), :] = o
            return s2

        def step(n, carry):
            loc, state = carry
            return local_of(n + 1), state_of(n, loc, state)

        zero_state = jnp.zeros((LANES, DN_HPS * LANES), F32)
        loc, state = lax.fori_loop(0, npair - 1, step, (local_of(0), zero_state))
        state_of(npair - 1, loc, state)

        @pl.when(jnp.logical_and(b_i == 0, hp == 0))
        def _():
            dcw_ref[...] = jnp.zeros_like(dcw_ref)
            dalog_ref[...] = jnp.zeros_like(dalog_ref)
            ddtb_ref[...] = jnp.zeros_like(ddtb_ref)
            dgain_ref[...] = jnp.zeros_like(dgain_ref)

        for i, cs in enumerate(_DN_COLS):
            zc = _dn_in_cols(3, i)
            _, post_vjp = jax.vjp(_dn_post, o_s[:, cs], x_ref[:, zc], gain_ref[...])
            do, dz, dgain = post_vjp(do_ref[:, cs])
            dx_ref[:, zc] = dz
            o_s[:, cs] = do
            dgain_ref[...] += dgain

        def state_back(nn, dstate):
            n = npair - 1 - nn
            r = _pair_rows(n)
            _, state_vjp = jax.vjp(_dn_state, *local_of(n, known_inverse=True), st_s[n])
            *dloc, ds0 = state_vjp((o_s[r, :], dstate))
            for d_s, val in zip(dloc_s, dloc[:5]):
                d_s[r, :] = val
            dcd_s[n, 0:1, :], dcd_s[n, 1:2, :] = dloc[5], dloc[6]
            return ds0

        lax.fori_loop(0, npair, state_back, zero_state)

        wide_cols = lambda s: slice(s * DN_HPS * LANES, (s + 1) * DN_HPS * LANES)

        def local_back(m, _):
            for pair in (2 * m, 2 * m + 1):
                r = _pair_rows(pair)
                inv = inv_s[pair]
                local = lambda q, k, v, g, beta, inv=inv: _dn_local(q, k, v, g, beta, inv)[0]
                _, local_vjp = jax.vjp(local, *pair_in(r))
                dq, dk, dv, dg, db = local_vjp(tuple(d_s[r, :] for d_s in dloc_s) + (dcd_s[pair, 0:1, :], dcd_s[pair, 1:2, :]))
                dx_ref[r, wide_cols(0)], dx_ref[r, wide_cols(1)], dx_ref[r, wide_cols(2)] = dq, dk, dv
                dgate_s[r, :] = _dn_pack_gate(_dn_gate_cols(dg, db))
            return 0

        lax.fori_loop(0, npair // 2, local_back, 0)

        lane = _iota((1, LANES), 1)
        dab = 0.0
        for i, cs in enumerate(_DN_COLS):
            h = DN_HPS * hp + i
            for s, l2 in enumerate(_DN_L2):
                xc = _dn_in_cols(s, i)
                _, conv_vjp = jax.vjp(functools.partial(_dn_conv, l2_scale=l2), x_ref[:, xc], *_dn_taps(cw_ref, s, i))
                dx, *dw = conv_vjp(dx_ref[:, xc])
                dx_ref[:, xc] = dx
                for t in range(4):
                    dcw_ref[h + 4 * s, t:t + 1, :] += dw[t]
            _, gate_vjp = jax.vjp(_dn_gate, *_dn_gate_in(ab_ref, alog_ref, dtb_ref, h))
            dgate = dgate_s[...]
            da, db, dalog, ddtb = gate_vjp((_lane_pick(dgate, 2 * i), _lane_pick(dgate, 2 * i + 1)))
            dab = dab + jnp.where(lane == h, da, 0.0) + jnp.where(lane == h + NH, db, 0.0)
            dalog_ref[...] += jnp.where(lane == h, dalog, 0.0)
            ddtb_ref[...] += jnp.where(lane == h, ddtb, 0.0)

        @pl.when(hp == 0)
        def _():
            dab_ref[...] = jnp.zeros_like(dab_ref)

        dab_ref[...] += dab
        if ns:
            pl.when(step_id == nsteps - 1)(finish)

    M = B * T
    one = pl.Buffered(1)
    vec = pl.BlockSpec((1, LANES), lambda b, h: (0, 0))
    wide = [pltpu.VMEM((T, DN_HPS * LANES), F32)]
    narrow = [pltpu.VMEM((T, LANES), F32)]
    vec_shape = jax.ShapeDtypeStruct((1, LANES), F32)
    outs = pl.pallas_call(
        body, name="dn_bwd", grid=(B, NH // DN_HPS),
        in_specs=_dn_in_specs(T) + [pl.BlockSpec((T, DN_HPS * LANES), lambda b, h: (b, h), pipeline_mode=one)] + _any_specs(ns),
        out_specs=[pl.BlockSpec((T, DN_BLK), lambda b, h: (b, h), pipeline_mode=one), pl.BlockSpec((T, LANES), lambda b, h: (b, 0)),
                   pl.BlockSpec((12, 4, LANES), lambda b, h: (0, 0, 0)), vec, vec, vec] + _any_specs(ns),
        out_shape=[jax.ShapeDtypeStruct((M, 4 * DN_W), F32), jax.ShapeDtypeStruct((M, LANES), F32),
                   jax.ShapeDtypeStruct((12, 4, LANES), F32), vec_shape, vec_shape, vec_shape] + _chip_swap_shapes(swap),
        scratch_shapes=wide * 4 + narrow * 2 + [pltpu.VMEM((npair, LANES, DN_HPS * LANES), F32)] * 2
        + [pltpu.VMEM((npair, 8, DN_HPS * LANES), F32)] + wide * 5 + (_chip_swap_sems(ns) if ns else []),
        compiler_params=pltpu.CompilerParams(vmem_limit_bytes=VMEM_LIMIT_MAX, dimension_semantics=("arbitrary", "arbitrary")),
    )(proj_dn, proj_ab, conv_w, a_log, dt_bias, gain, dmix, *swap)
    return outs[:6], outs[6:]


SBQ = 256


def _group_rms(x, gain):
    first = _iota(x.shape, 1) < 64
    sq = x * x
    ss_a = jnp.sum(jnp.where(first, sq, 0.0), axis=-1, keepdims=True)
    ss_b = jnp.sum(jnp.where(first, 0.0, sq), axis=-1, keepdims=True)
    ms = jnp.where(first, ss_a, ss_b) * (1.0 / 64)
    return x * lax.rsqrt(ms + EPS) * gain


def _sb_stack(q):
    first = _iota((1, LANES), 1) < 64
    return jnp.concatenate([jnp.where(first, q, 0.0), jnp.where(first, 0.0, q)], axis=0)


def _sb_fold(acc):
    return jnp.where(_iota((1, LANES), 1) < 64, acc[:SBQ], acc[SBQ:])


def _sb_logs(q2, k, diag):
    n = SBQ
    z = _mm(q2, k, ((1,), (1,))) * SB_SCALE
    ls_pos = jnp.minimum(z, 0.0) - jnp.log1p(jnp.exp(-jnp.abs(z)))
    l1m = ls_pos - z
    if not diag:
        return ls_pos, l1m, None
    mask = _iota((2 * n, n), 1) < jnp.bitwise_and(_iota((2 * n, n), 0), n - 1)
    return ls_pos, jnp.where(mask, l1m, 0.0), mask


def _sb_weights(ls_pos, l1m, mask, carry):
    w = jnp.exp(ls_pos + (_mm_ones(_tri_ones(SBQ, False), l1m, False) + carry))
    return w if mask is None else jnp.where(mask, w, 0.0)


def _sb_block(q, k, v, carry, diag):
    ls_pos, l1m, mask = _sb_logs(_sb_stack(q), k, diag)
    w = _sb_weights(ls_pos, l1m, mask, carry)
    return _mm(w, v, ((1,), (0,))), carry + jnp.sum(l1m, axis=-1, keepdims=True)


def _sb_rowsum(q, k, diag):
    return jnp.sum(_sb_logs(_sb_stack(q), k, diag)[1], axis=-1, keepdims=True)


def _sb_block_bwd(q, k, v, carry, diag, dpv, dcarry):
    q2 = _sb_stack(q)
    ls_pos, l1m, mask = _sb_logs(q2, k, diag)
    w = _sb_weights(ls_pos, l1m, mask, carry)
    dv = _mm(w, dpv, ((0,), (0,)))
    de = _mm(dpv, v, ((1,), (1,))) * w
    dl1m = _mm_ones(_tri_ones(SBQ, True), de, False) + dcarry
    if mask is not None:
        dl1m = jnp.where(mask, dl1m, 0.0)
    sig = jnp.exp(ls_pos)
    dz = (de * (1.0 - sig) - dl1m * sig) * SB_SCALE
    dq = _sb_fold(_mm(dz, k, ((1,), (0,))))
    return dq, _mm(dz, q2, ((0,), (0,))), dv, dcarry + jnp.sum(de, axis=-1, keepdims=True)


_SB_Q, _SB_K, _SB_V = (slice(i * LANES, (i + 1) * LANES) for i in range(3))


def _sb_fwd_call(proj_sb, mix, q_gain, k_gain, B, T, gather=()):
    nblk = T // SBQ
    ng = len(gather)
    nsteps = 2 * B

    def body(*refs):
        x_ref, qg_ref, kg_ref = refs[:3]
        out_ref = refs[4 + ng]
        q_s, k_s = refs[5 + 2 * ng:7 + 2 * ng]
        step_id = 2 * pl.program_id(0) + pl.program_id(1)
        if ng:
            send, forward, finish = _gather_phases(refs[4:4 + ng], refs[5 + ng:5 + 2 * ng], *refs[7 + 2 * ng:])
            pl.when(step_id == 0)(send)
            pl.when(step_id == nsteps // 2)(forward)
        q_s[...] = _group_rms(x_ref[:, _SB_Q], qg_ref[...])
        k_s[...] = _group_rms(x_ref[:, _SB_K], kg_ref[...])

        def qblock(i, _):
            ri = pl.ds(pl.multiple_of(i * SBQ, SBQ), SBQ)
            q = q_s[ri, :]

            def kblock(jj, c):
                rj = pl.ds(pl.multiple_of((i - 1 - jj) * SBQ, SBQ), SBQ)
                pv, carry = _sb_block(q, k_s[rj, :], x_ref[rj, _SB_V], c[1], False)
                return c[0] + pv, carry

            on_diag = _sb_block(q, k_s[ri, :], x_ref[ri, _SB_V], jnp.zeros((2 * SBQ, 1), F32), True)
            acc, _c = lax.fori_loop(0, i, kblock, on_diag)
            out_ref[ri, :] = _sb_fold(acc)
            return 0

        lax.fori_loop(0, nblk, qblock, 0)
        if ng:
            pl.when(step_id == nsteps - 1)(finish)

    vec = pl.BlockSpec((1, LANES), lambda b, p: (0, 0))
    outs = pl.pallas_call(
        body, name="sb_fwd", grid=(B, 2),
        in_specs=[pl.BlockSpec((T, 3 * LANES), lambda b, p: (b, p)), vec, vec, pl.BlockSpec(memory_space=pl.ANY)] + _any_specs(ng),
        out_specs=[pl.BlockSpec((T, LANES), lambda b, p: (b, DN_W // LANES + p))] + _any_specs(ng),
        out_shape=[jax.ShapeDtypeStruct((B * T, D), F32)] + _gather_shapes(gather), input_output_aliases={3: 0},
        scratch_shapes=[pltpu.VMEM((T, LANES), F32)] * 2 + (_gather_sems(ng) if ng else []),
        compiler_params=_params(dimension_semantics=("arbitrary", "arbitrary")),
    )(proj_sb, q_gain, k_gain, mix, *gather)
    return outs[0], outs[1:]


def _sb_bwd_call(proj_sb, dmix, q_gain, k_gain, B, T):
    nblk = T // SBQ

    def body(x_ref, qg_ref, kg_ref, do_ref, dx_ref, dqg_ref, dkg_ref, q_s, k_s, dq_s, dk_s, dv_s, c_s):
        b_i, p = pl.program_id(0), pl.program_id(1)
        qn, q_vjp = jax.vjp(_group_rms, x_ref[:, _SB_Q], qg_ref[...])
        kn, k_vjp = jax.vjp(_group_rms, x_ref[:, _SB_K], kg_ref[...])
        q_s[...], k_s[...] = qn, kn
        dk_s[...] = jnp.zeros_like(dk_s)
        dv_s[...] = jnp.zeros_like(dv_s)

        def qblock(i, _):
            ri = pl.ds(pl.multiple_of(i * SBQ, SBQ), SBQ)
            q = q_s[ri, :]
            dacc = _sb_stack(do_ref[ri, :])

            def carries(jj, carry):
                j = i - 1 - jj
                rj = pl.ds(pl.multiple_of(j * SBQ, SBQ), SBQ)
                c_s[j] = carry
                return carry + _sb_rowsum(q, k_s[rj, :], False)

            lax.fori_loop(0, i, carries, _sb_rowsum(q, k_s[ri, :], True))

            def kblock(j, c):
                rj = pl.ds(pl.multiple_of(j * SBQ, SBQ), SBQ)
                dq_j, dk_j, dv_j, dc = _sb_block_bwd(q, k_s[rj, :], x_ref[rj, _SB_V], c_s[j], False, dacc, c[1])
                dk_s[rj, :] += dk_j
                dv_s[rj, :] += dv_j
                return c[0] + dq_j, dc

            dq, dc = lax.fori_loop(0, i, kblock, (jnp.zeros((SBQ, LANES), F32), jnp.zeros((2 * SBQ, 1), F32)))
            dq_i, dk_i, dv_i, _dc = _sb_block_bwd(q, k_s[ri, :], x_ref[ri, _SB_V], jnp.zeros((2 * SBQ, 1), F32), True, dacc, dc)
            dk_s[ri, :] += dk_i
            dv_s[ri, :] += dv_i
            dq_s[ri, :] = dq + dq_i
            return 0

        lax.fori_loop(0, nblk, qblock, 0)
        dq_in, dqg = q_vjp(dq_s[...])
        dk_in, dkg = k_vjp(dk_s[...])
        dx_ref[:, _SB_Q], dx_ref[:, _SB_K], dx_ref[:, _SB_V] = dq_in, dk_in, dv_s[...]

        @pl.when(jnp.logical_and(b_i == 0, p == 0))
        def _():
            dqg_ref[...] = jnp.zeros_like(dqg_ref)
            dkg_ref[...] = jnp.zeros_like(dkg_ref)

        dqg_ref[...] += dqg + pltpu.roll(dqg, 64, 1)
        dkg_ref[...] += dkg + pltpu.roll(dkg, 64, 1)

    M = B * T
    vec = pl.BlockSpec((1, LANES), lambda b, p: (0, 0))
    blk = pl.BlockSpec((T, 3 * LANES), lambda b, p: (b, p))
    big = [pltpu.VMEM((T, LANES), F32)]
    return pl.pallas_call(
        body, name="sb_bwd", grid=(B, 2),
        in_specs=[blk, vec, vec, pl.BlockSpec((T, LANES), lambda b, p: (b, DN_W // LANES + p))],
        out_specs=[blk, vec, vec],
        out_shape=[jax.ShapeDtypeStruct((M, 3 * SB_W), F32)] + [jax.ShapeDtypeStruct((1, LANES), F32)] * 2,
        scratch_shapes=big * 5 + [pltpu.VMEM((nblk, 2 * SBQ, 1), F32)],
        compiler_params=_params(dimension_semantics=("arbitrary", "arbitrary")),
    )(proj_sb, q_gain, k_gain, dmix)


def _sg_chunk(u, v, gain, w_a, w_b, bias):
    n = 128
    row, col = _iota((n, n), 0), _iota((n, n), 1)
    first = _iota((1, LANES), 1) < 64
    vn = _group_rms(_gelu(v), gain)
    tril = col <= row
    mixed = jnp.where(first, _dot(jnp.where(tril, w_a, 0.0), vn), _dot(jnp.where(tril, w_b, 0.0), vn)) + bias
    return _gelu(u) * mixed


_SG_U, _SG_V = slice(0, LANES), slice(LANES, 2 * LANES)


def _sg_fwd_call(proj_sg, mix, gain, sg_w, bias, B, T):
    nchunk = T // 128

    def body(x_ref, g_ref, wa_ref, wb_ref, bias_ref, mix_ref, out_ref):
        del mix_ref

        def step(i, _):
            r = pl.ds(pl.multiple_of(i * 128, 128), 128)
            out_ref[r, :] = _sg_chunk(x_ref[r, _SG_U], x_ref[r, _SG_V], g_ref[...], wa_ref[0], wb_ref[0], bias_ref[...])
            return 0

        lax.fori_loop(0, nchunk, step, 0)

    return pl.pallas_call(
        body, name="sg_fwd", grid=(B, 2),
        in_specs=[pl.BlockSpec((T, 2 * LANES), lambda b, p: (b, p)), pl.BlockSpec((1, LANES), lambda b, p: (0, p)),
                  pl.BlockSpec((1, 128, 128), lambda b, p: (2 * p, 0, 0)), pl.BlockSpec((1, 128, 128), lambda b, p: (2 * p + 1, 0, 0)),
                  pl.BlockSpec((128, LANES), lambda b, p: (0, p)), pl.BlockSpec(memory_space=pl.ANY)],
        out_specs=pl.BlockSpec((T, LANES), lambda b, p: (b, (DN_W + SB_W) // LANES + p)),
        out_shape=jax.ShapeDtypeStruct((B * T, D), F32), input_output_aliases={5: 0},
        compiler_params=_params(dimension_semantics=("arbitrary", "arbitrary")),
    )(proj_sg, gain, sg_w, sg_w, bias, mix)


def _sg_bwd_call(proj_sg, dmix, gain, sg_w, bias, B, T):
    nchunk = T // 128

    def body(x_ref, g_ref, wa_ref, wb_ref, bias_ref, do_ref, dx_ref, dg_ref, dw_ref, db_ref):
        p, b_i = pl.program_id(0), pl.program_id(1)

        def step(i, c):
            r = pl.ds(pl.multiple_of(i * 128, 128), 128)
            _, vjp = jax.vjp(_sg_chunk, x_ref[r, _SG_U], x_ref[r, _SG_V], g_ref[...], wa_ref[0], wb_ref[0], bias_ref[...])
            du, dv, dg, dwa, dwb, dbias = vjp(do_ref[r, :])
            dx_ref[r, _SG_U], dx_ref[r, _SG_V] = du, dv
            return c[0] + dg, c[1] + dwa, c[2] + dwb, c[3] + dbias

        z = jnp.zeros((128, 128), F32)
        dg, dwa, dwb, dbias = lax.fori_loop(0, nchunk, step, (jnp.zeros((1, LANES), F32), z, z, z))
        lane = _iota((1, LANES), 1)
        first = lane < 64
        s_a = jnp.sum(jnp.where(first, dbias, 0.0), axis=-1, keepdims=True)
        s_b = jnp.sum(jnp.where(first, 0.0, dbias), axis=-1, keepdims=True)
        dbg = jnp.where(lane == 2 * p, s_a, 0.0) + jnp.where(lane == 2 * p + 1, s_b, 0.0)

        @pl.when(b_i == 0)
        def _():
            dg_ref[...] = jnp.zeros_like(dg_ref)
            dw_ref[...] = jnp.zeros_like(dw_ref)

        @pl.when(jnp.logical_and(b_i == 0, p == 0))
        def _():
            db_ref[...] = jnp.zeros_like(db_ref)

        dg_ref[...] += dg
        dw_ref[0] += dwa
        dw_ref[1] += dwb
        db_ref[...] += dbg

    M = B * T
    blk = pl.BlockSpec((T, 2 * LANES), lambda p, b: (b, p))
    return pl.pallas_call(
        body, name="sg_bwd", grid=(2, B),
        in_specs=[blk, pl.BlockSpec((1, LANES), lambda p, b: (0, p)),
                  pl.BlockSpec((1, 128, 128), lambda p, b: (2 * p, 0, 0)), pl.BlockSpec((1, 128, 128), lambda p, b: (2 * p + 1, 0, 0)),
                  pl.BlockSpec((128, LANES), lambda p, b: (0, p)),
                  pl.BlockSpec((T, LANES), lambda p, b: (b, (DN_W + SB_W) // LANES + p))],
        out_specs=[blk, pl.BlockSpec((1, LANES), lambda p, b: (0, p)), pl.BlockSpec((2, 128, 128), lambda p, b: (p, 0, 0)),
                   pl.BlockSpec((128, LANES), lambda p, b: (0, 0))],
        out_shape=[jax.ShapeDtypeStruct((M, 2 * SG_W), F32), jax.ShapeDtypeStruct((1, SG_W), F32),
                   jax.ShapeDtypeStruct((4, 128, 128), F32), jax.ShapeDtypeStruct((128, LANES), F32)],
        compiler_params=_params(dimension_semantics=("arbitrary", "arbitrary")),
    )(proj_sg, gain, sg_w, sg_w, bias, dmix)


def _row_tile(m, most=512):
    return min(m, most)


def _in_proj_call(x, gain, wt):
    m = x.shape[0]
    tm = _row_tile(m)

    def body(x_ref, g_ref, wt_ref, *out_refs):
        h = _rms(x_ref[...], g_ref[...]).astype(BF16)
        out_refs[-1][...] = h
        for (off, width), out_ref in zip(SECTIONS, out_refs):
            out_ref[...] = lax.dot_general(h, wt_ref[off:off + width, :], (((1,), (1,)), ((), ())), preferred_element_type=F32)

    rows = lambda width: pl.BlockSpec((tm, width), lambda i: (i, 0))
    return pl.pallas_call(
        body, name="in_proj", grid=(m // tm,),
        in_specs=[rows(D), pl.BlockSpec((1, D), lambda i: (0, 0)),
                  pl.BlockSpec((NPACK, D), lambda i: (0, 0), pipeline_mode=pl.Buffered(1))],
        out_specs=[rows(w) for _, w in SECTIONS] + [rows(D)],
        out_shape=[jax.ShapeDtypeStruct((m, w), F32) for _, w in SECTIONS] + [jax.ShapeDtypeStruct((m, D), BF16)],
        compiler_params=_params(dimension_semantics=("arbitrary",)),
    )(x, gain, wt)


def _in_proj_bwd_call(dsections, wt, x, gain, dres):
    m = x.shape[0]
    tm = _row_tile(m)

    def body(*refs):
        ds_refs, (wt_ref, x_ref, g_ref, dres_ref, dx_ref, dg_ref) = refs[:len(SECTIONS)], refs[len(SECTIONS):]

        @pl.when(pl.program_id(0) == 0)
        def _():
            dg_ref[...] = jnp.zeros_like(dg_ref)

        dh = 0.0
        for (off, width), ds_ref in zip(SECTIONS, ds_refs):
            dh = dh + jnp.dot(ds_ref[...].astype(BF16), wt_ref[off:off + width, :], preferred_element_type=F32)
        _, vjp = jax.vjp(_rms, x_ref[...], g_ref[...])
        dx, dg = vjp(dh)
        dx_ref[...] = dres_ref[...] + dx
        dg_ref[...] += dg

    rows = lambda width: pl.BlockSpec((tm, width), lambda i: (i, 0))
    return pl.pallas_call(
        body, name="in_proj_bwd", grid=(m // tm,),
        in_specs=[rows(w) for _, w in SECTIONS] + [pl.BlockSpec((NPACK, D), lambda i: (0, 0), pipeline_mode=pl.Buffered(1)),
                                                   rows(D), pl.BlockSpec((1, D), lambda i: (0, 0)), rows(D)],
        out_specs=[rows(D), pl.BlockSpec((1, D), lambda i: (0, 0))],
        out_shape=[jax.ShapeDtypeStruct((m, D), F32), jax.ShapeDtypeStruct((1, D), F32)],
        compiler_params=_params(dimension_semantics=("arbitrary",)),
    )(*dsections, wt, x, gain, dres)


def _in_proj_grad_call(dsections, h):
    m = h.shape[0]
    tm = min(m, 256)

    def body(*refs):
        ds_refs, (h_ref, out_ref) = refs[:len(SECTIONS)], refs[len(SECTIONS):]

        @pl.when(pl.program_id(0) == 0)
        def _():
            out_ref[...] = jnp.zeros_like(out_ref)

        for (off, width), ds_ref in zip(SECTIONS, ds_refs):
            out_ref[off:off + width, :] += lax.dot_general(ds_ref[...].astype(BF16), h_ref[...], (((0,), (0,)), ((), ())),
                                                           preferred_element_type=F32)

    rows = lambda width: pl.BlockSpec((tm, width), lambda i: (i, 0))
    return pl.pallas_call(
        body, name="grad_w_in", grid=(m // tm,),
        in_specs=[rows(w) for _, w in SECTIONS] + [rows(D)],
        out_specs=pl.BlockSpec((NPACK, D), lambda i: (0, 0), pipeline_mode=pl.Buffered(1)),
        out_shape=jax.ShapeDtypeStruct((NPACK, D), F32),
        compiler_params=_params(dimension_semantics=("arbitrary",)),
    )(*dsections, h)


def _packed_column_of():
    t = np.full(NPACK, -1, np.int64)
    lanes = np.arange(LANES)
    for pair in range(2):
        for s in range(4):
            t[DN_OFF + pair * 1024 + s * 256 + np.arange(256)] = s * DN_W + pair * 256 + np.arange(256)
        for s in range(3):
            t[SB_OFF + pair * 384 + s * LANES + lanes] = 2056 + s * SB_W + pair * LANES + lanes
        for s in range(2):
            t[SG_OFF + pair * 256 + s * LANES + lanes] = 2056 + 3 * SB_W + s * SG_W + pair * LANES + lanes
    t[AB_OFF + np.arange(2 * NH)] = 4 * DN_W + np.arange(2 * NH)
    return t


def _row_tables():
    col = _packed_column_of()
    fwd = np.where(col >= 0, (col // IN_SHARD) * IN_SHARD_PAD + col % IN_SHARD, -1)
    packed_of = np.full(IN_DIM, -1, np.int64)
    packed_of[col[col >= 0]] = np.nonzero(col >= 0)[0]
    r = np.arange(NDEV * IN_SHARD_PAD)
    inside = r % IN_SHARD_PAD < IN_SHARD
    back = np.where(inside, packed_of[np.minimum((r // IN_SHARD_PAD) * IN_SHARD + r % IN_SHARD_PAD, IN_DIM - 1)], -1)
    return fwd, back


def _row_perm_call(src, table, name):
    n_out = table.shape[0]
    touched = [sorted(set((table[b * 128:(b + 1) * 128][table[b * 128:(b + 1) * 128] >= 0] // 128).tolist()))
               for b in range(n_out // 128)]

    def body(tbl_ref, src_ref, out_ref):
        lane = _iota((1, LANES), 1)
        for b, blocks in enumerate(touched):
            want = tbl_ref[b * 128:(b + 1) * 128, :]
            acc = jnp.zeros((128, D), F32)
            for sb in blocks:
                pick = jnp.where(want == sb * 128 + lane, 1.0, 0.0).astype(BF16)
                acc = acc + jnp.dot(pick, src_ref[sb * 128:(sb + 1) * 128, :].astype(BF16), preferred_element_type=F32)
            out_ref[b * 128:(b + 1) * 128, :] = acc.astype(BF16)

    return pl.pallas_call(
        body, name=name, out_shape=jax.ShapeDtypeStruct((n_out, D), BF16),
        in_specs=[pl.BlockSpec(memory_space=pltpu.VMEM)] * 2, out_specs=pl.BlockSpec(memory_space=pltpu.VMEM),
        compiler_params=_params(),
    )(jnp.asarray(table.reshape(-1, 1), jnp.int32), src)


def _out_proj_call(a, w, res):
    m, k = a.shape
    n = w.shape[1]
    tm = _row_tile(m)

    def body(a_ref, w_ref, res_ref, out_ref):
        out_ref[...] = res_ref[...] + jnp.dot(a_ref[...].astype(BF16), w_ref[...], preferred_element_type=F32)

    return pl.pallas_call(
        body, name="out_proj", grid=(m // tm,),
        in_specs=[pl.BlockSpec((tm, k), lambda i: (i, 0)), pl.BlockSpec((k, n), lambda i: (0, 0)),
                  pl.BlockSpec((tm, n), lambda i: (i, 0))],
        out_specs=pl.BlockSpec((tm, n), lambda i: (i, 0)),
        out_shape=jax.ShapeDtypeStruct((m, n), F32),
        compiler_params=_params(dimension_semantics=("arbitrary",)),
    )(a, w, res)


def _ffn_specs(tm):
    return [pl.BlockSpec((1, D, FF_SHARD), lambda i, j: (j, 0, 0)), pl.BlockSpec((FF_SHARD, D), lambda i, j: (j, 0))]


def _ffn_fwd_call(x, gain, w1, w2):
    m = x.shape[0]
    tm = _row_tile(m, 1024)

    def body(x_ref, g_ref, w1_ref, w2_ref, out_ref, h_s, acc_s):
        j = pl.program_id(1)

        @pl.when(j == 0)
        def _():
            h_s[...] = _rms(x_ref[...], g_ref[...]).astype(BF16)
            acc_s[...] = jnp.zeros_like(acc_s)

        a = jnp.maximum(jnp.dot(h_s[...], w1_ref[0], preferred_element_type=F32), 0.0)
        acc_s[...] += jnp.dot((a * a).astype(BF16), w2_ref[...], preferred_element_type=F32)

        @pl.when(j == NDEV - 1)
        def _():
            out_ref[...] = x_ref[...] + acc_s[...]

    return pl.pallas_call(
        body, name="ffn_fwd", grid=(m // tm, NDEV),
        in_specs=[pl.BlockSpec((tm, D), lambda i, j: (i, 0)), pl.BlockSpec((1, D), lambda i, j: (0, 0))] + _ffn_specs(tm),
        out_specs=pl.BlockSpec((tm, D), lambda i, j: (i, 0)),
        out_shape=jax.ShapeDtypeStruct((m, D), F32),
        scratch_shapes=[pltpu.VMEM((tm, D), BF16), pltpu.VMEM((tm, D), F32)],
        compiler_params=_params(dimension_semantics=("arbitrary", "arbitrary")),
    )(x, gain, w1, w2)


def _ffn_bwd_call(x, dy, gain, w1, w2, swap=()):
    m = x.shape[0]
    tm = _row_tile(m, 1024)
    ns = len(swap)

    def body(*refs):
        x_ref, dy_ref, g_ref, w1_ref, w2_ref = refs[:5]
        dx_ref, da_ref, r_ref, h_ref, dg_ref = refs[5 + ns:10 + ns]
        acc_s = refs[10 + 2 * ns]
        i, j = pl.program_id(0), pl.program_id(1)
        if ns:
            send, finish = _sibling_swap_phases(refs[5:5 + ns], refs[10 + ns:10 + 2 * ns], *refs[11 + 2 * ns:])
            pl.when(jnp.logical_and(i == 0, j == 0))(send)

        @pl.when(j == 0)
        def _():
            h_ref[...] = _rms(x_ref[...], g_ref[...]).astype(BF16)
            acc_s[...] = jnp.zeros_like(acc_s)

        @pl.when(jnp.logical_and(i == 0, j == 0))
        def _():
            dg_ref[...] = jnp.zeros_like(dg_ref)

        a = jnp.maximum(jnp.dot(h_ref[...], w1_ref[0], preferred_element_type=F32), 0.0)
        r_ref[...] = (a * a).astype(BF16)
        dr = lax.dot_general(dy_ref[...].astype(BF16), w2_ref[...], (((1,), (1,)), ((), ())), preferred_element_type=F32)
        da = (dr * (2.0 * a)).astype(BF16)
        da_ref[...] = da
        acc_s[...] += lax.dot_general(da, w1_ref[0], (((1,), (1,)), ((), ())), preferred_element_type=F32)

        @pl.when(j == NDEV - 1)
        def _():
            _, vjp = jax.vjp(_rms, x_ref[...], g_ref[...])
            dx, dg = vjp(acc_s[...])
            dx_ref[...] = dy_ref[...] + dx
            dg_ref[...] += dg

        if ns:
            pl.when(jnp.logical_and(i == m // tm - 1, j == NDEV - 1))(finish)

    outs = pl.pallas_call(
        body, name="ffn_bwd", grid=(m // tm, NDEV),
        in_specs=[pl.BlockSpec((tm, D), lambda i, j: (i, 0)), pl.BlockSpec((tm, D), lambda i, j: (i, 0)),
                  pl.BlockSpec((1, D), lambda i, j: (0, 0))] + _ffn_specs(tm) + _any_specs(ns),
        out_specs=[pl.BlockSpec((tm, D), lambda i, j: (i, 0)), pl.BlockSpec((tm, FF_SHARD), lambda i, j: (i, j)),
                   pl.BlockSpec((tm, FF_SHARD), lambda i, j: (i, j)), pl.BlockSpec((tm, D), lambda i, j: (i, 0)),
                   pl.BlockSpec((1, D), lambda i, j: (0, 0))] + _any_specs(ns),
        out_shape=[jax.ShapeDtypeStruct((m, D), F32), jax.ShapeDtypeStruct((m, DFF), BF16), jax.ShapeDtypeStruct((m, DFF), BF16),
                   jax.ShapeDtypeStruct((m, D), BF16), jax.ShapeDtypeStruct((1, D), F32)] + _sibling_swap_shapes(swap),
        scratch_shapes=[pltpu.VMEM((tm, D), F32)] + (_sibling_swap_sems(ns) if ns else []),
        compiler_params=_params(dimension_semantics=("arbitrary", "arbitrary")),
    )(x, dy, gain, w1, w2, *swap)
    return outs[:5], outs[5:]


def _mm_nt_call(a, b, name):
    m, k = a.shape
    n = b.shape[0]
    tm = _row_tile(m)

    def body(a_ref, b_ref, out_ref):
        out_ref[...] = lax.dot_general(a_ref[...].astype(BF16), b_ref[...].astype(BF16), (((1,), (1,)), ((), ())),
                                       preferred_element_type=F32)

    return pl.pallas_call(
        body, name=name, grid=(m // tm,),
        in_specs=[pl.BlockSpec((tm, k), lambda i: (i, 0)), pl.BlockSpec((n, k), lambda i: (0, 0))],
        out_specs=pl.BlockSpec((tm, n), lambda i: (i, 0)),
        out_shape=jax.ShapeDtypeStruct((m, n), F32),
        compiler_params=_params(dimension_semantics=("arbitrary",)),
    )(a, b)


def _mm_tn_call(a, b, name, col_shards=False):
    m, k = a.shape
    n = b.shape[1]
    tm, tk = _row_tile(m, 1024), min(k, 1024)
    tn = n // NDEV if col_shards else min(n, 1024)

    def body(a_ref, b_ref, out_ref, acc_s):
        s = pl.program_id(2)

        @pl.when(s == 0)
        def _():
            acc_s[...] = jnp.zeros_like(acc_s)

        acc_s[...] += lax.dot_general(a_ref[...].astype(BF16), b_ref[...].astype(BF16), (((0,), (0,)), ((), ())),
                                      preferred_element_type=F32)

        @pl.when(s == m // tm - 1)
        def _():
            out_ref[...] = acc_s[...].astype(BF16).reshape(out_ref.shape)

    if col_shards:
        out_spec, out_shape = pl.BlockSpec((1, tk, tn), lambda i, j, s: (j, i, 0)), (NDEV, k, tn)
    else:
        out_spec, out_shape = pl.BlockSpec((tk, tn), lambda i, j, s: (i, j)), (k, n)
    return pl.pallas_call(
        body, name=name, grid=(k // tk, n // tn, m // tm),
        in_specs=[pl.BlockSpec((tm, tk), lambda i, j, s: (s, i)), pl.BlockSpec((tm, tn), lambda i, j, s: (s, j))],
        out_specs=out_spec, out_shape=jax.ShapeDtypeStruct(out_shape, BF16),
        scratch_shapes=[pltpu.VMEM((tk, tn), F32)],
        compiler_params=_params(dimension_semantics=("arbitrary", "arbitrary", "arbitrary")),
    )(a, b)


def _loss_call(y, target):
    m = y.shape[0]
    tm = _row_tile(m)

    def body(y_ref, t_ref, loss_ref, dy_ref):
        @pl.when(pl.program_id(0) == 0)
        def _():
            loss_ref[...] = jnp.zeros_like(loss_ref)

        err = y_ref[...] - t_ref[...]
        dy_ref[...] = err * (1.0 / D)
        per_row = jnp.mean(err * err, axis=-1, keepdims=True)
        loss_ref[...] += jnp.broadcast_to(0.5 * jnp.sum(per_row, axis=0, keepdims=True), (1, LANES))

    return pl.pallas_call(
        body, name="loss", grid=(m // tm,),
        in_specs=[pl.BlockSpec((tm, D), lambda i: (i, 0))] * 2,
        out_specs=[pl.BlockSpec((1, LANES), lambda i: (0, 0)), pl.BlockSpec((tm, D), lambda i: (i, 0))],
        out_shape=[jax.ShapeDtypeStruct((1, LANES), F32), jax.ShapeDtypeStruct((m, D), F32)],
        compiler_params=_params(dimension_semantics=("arbitrary",)),
    )(y, target)


def _adamw_call(w, g, m, v, name):
    shape = w.shape
    cols = shape[-1] if w.ndim > 1 else w.size
    rows = w.size // cols
    tr = rows if (rows <= 512 or rows % 512) else 512
    c1, c2 = 1.0 - ADAM_B1 ** ADAM_STEP, 1.0 - ADAM_B2 ** ADAM_STEP

    def body(w_ref, g_ref, m_ref, v_ref, d_ref, nm_ref, nv_ref):
        g_ = g_ref[...]
        nm = ADAM_B1 * m_ref[...] + (1.0 - ADAM_B1) * g_
        nv = ADAM_B2 * v_ref[...] + (1.0 - ADAM_B2) * (g_ * g_)
        d_ref[...] = -ADAM_LR * ((nm / c1) / (jnp.sqrt(nv / c2) + ADAM_EPS) + ADAM_WD * w_ref[...])
        nm_ref[...], nv_ref[...] = nm, nv

    spec = pl.BlockSpec((tr, cols), lambda i: (i, 0))
    outs = pl.pallas_call(
        body, name=name, grid=(rows // tr,), in_specs=[spec] * 4, out_specs=[spec] * 3,
        out_shape=[jax.ShapeDtypeStruct((rows, cols), F32)] * 3,
        compiler_params=_params(dimension_semantics=("arbitrary",)),
    )(*(t.reshape(rows, cols) for t in (w, g, m, v)))
    return tuple(o.reshape(shape) for o in outs)


def _sum_tile(rows):
    for cand in (2048, 1024, 512, 256, 128):
        if rows > cand and rows % cand == 0:
            return cand
    return rows


def _pair_sum_call(g, got, core, name):
    rows, cols = g.shape[1:]
    tr = _sum_tile(rows)

    def body(core_ref, g_ref, got_ref, out_ref):
        del core_ref
        out_ref[...] = (g_ref[...].astype(F32) + got_ref[...].astype(F32)).astype(BF16)

    grid_spec = pltpu.PrefetchScalarGridSpec(
        num_scalar_prefetch=1, grid=(4, rows // tr),
        in_specs=[pl.BlockSpec((1, tr, cols), lambda ch, t, core_ref: (2 * ch + core_ref[0], t, 0)),
                  pl.BlockSpec((1, tr, cols), lambda ch, t, core_ref: (ch, t, 0))],
        out_specs=pl.BlockSpec((1, tr, cols), lambda ch, t, core_ref: (ch, t, 0)))
    return pl.pallas_call(
        body, name=name, grid_spec=grid_spec, out_shape=jax.ShapeDtypeStruct((4, rows, cols), BF16),
        compiler_params=_params(dimension_semantics=("arbitrary", "arbitrary")),
    )(jnp.asarray(core, jnp.int32).reshape(1), g, got)


def _sum_call(parts, out_dtype, name):
    rows, cols = parts[0][0].shape[1:]
    tr = _sum_tile(rows)
    index = jnp.stack([jnp.asarray(i, jnp.int32) for _, i in parts])

    def body(idx_ref, *refs):
        del idx_ref
        acc = refs[0][0].astype(F32)
        for r in refs[1:-1]:
            acc = acc + r[0].astype(F32)
        refs[-1][...] = acc.astype(out_dtype)

    grid_spec = pltpu.PrefetchScalarGridSpec(
        num_scalar_prefetch=1, grid=(rows // tr,),
        in_specs=[pl.BlockSpec((1, tr, cols), lambda t, idx, n=n: (idx[n], t, 0)) for n in range(len(parts))],
        out_specs=pl.BlockSpec((tr, cols), lambda t, idx: (t, 0)))
    return pl.pallas_call(
        body, name=name, grid_spec=grid_spec, out_shape=jax.ShapeDtypeStruct((rows, cols), out_dtype),
        compiler_params=_params(dimension_semantics=("arbitrary",)),
    )(index, *(a for a, _ in parts))


def _place():
    return lax.axis_index("x"), lax.axis_index("y"), lax.axis_index("c")


def _any_specs(n):
    return [pl.BlockSpec(memory_space=pl.ANY)] * n


def _all_gather_call(xs, name):
    n = len(xs)

    def body(*refs):
        for phase in _gather_phases(refs[:n], refs[n:2 * n], *refs[2 * n:]):
            phase()

    return pl.pallas_call(
        body, name=name, in_specs=_any_specs(n), out_specs=_any_specs(n),
        out_shape=_gather_shapes(xs), scratch_shapes=_gather_sems(n),
    )(*xs)


def _gather_shapes(xs):
    return [jax.ShapeDtypeStruct((NDEV,) + x.shape, x.dtype) for x in xs]


def _gather_sems(n):
    return [pltpu.SemaphoreType.DMA((7 * n,)), pltpu.SemaphoreType.DMA((7 * n,)), pltpu.SemaphoreType.DMA((n,))]


def _gather_phases(x_refs, out_refs, send_sems, recv_sems, local_sems):
    n = len(x_refs)
    ax, ay, ac = _place()
    me, sibling = (ax, ay, ac), (ax, ay, 1 - ac)
    chips = [(1 - ax, ay), (ax, 1 - ay), (1 - ax, 1 - ay)]

    def copy(a, k, block, to, src=None):
        slot = out_refs[a].at[4 * block[0] + 2 * block[1] + block[2]]
        return pltpu.make_async_remote_copy(
            src_ref=slot if src is None else src, dst_ref=slot,
            send_sem=send_sems.at[7 * a + k], recv_sem=recv_sems.at[7 * a + k], device_id=to, device_id_type=MESH)

    local = [pltpu.make_async_copy(x_refs[a], out_refs[a].at[4 * ax + 2 * ay + ac], local_sems.at[a]) for a in range(n)]
    first = []
    for a in range(n):
        first.append(copy(a, 0, me, sibling, src=x_refs[a]))
        first += [copy(a, 1 + j, me, (*chip, ac), src=x_refs[a]) for j, chip in enumerate(chips)]
    passed = [copy(a, 4 + j, (*chip, ac), sibling) for j, chip in enumerate(chips) for a in range(n)]

    def send():
        for cp in local + first:
            cp.start()

    def forward():
        for j, chip in enumerate(chips):
            for a in range(n):
                copy(a, 1 + j, (*chip, ac), me).wait_recv()
                passed[j * n + a].start()

    def finish():
        for a in range(n):
            copy(a, 0, sibling, me).wait_recv()
            for j, chip in enumerate(chips):
                copy(a, 4 + j, (*chip, 1 - ac), me).wait_recv()
        for cp in first + passed:
            cp.wait_send()
        for cp in local:
            cp.wait()

    return send, forward, finish


def _swap_sibling_call(xs, name):
    n = len(xs)

    def body(*refs):
        for phase in _sibling_swap_phases(refs[:n], refs[n:2 * n], *refs[2 * n:]):
            phase()

    return pl.pallas_call(
        body, name=name, in_specs=_any_specs(n), out_specs=_any_specs(n),
        out_shape=_sibling_swap_shapes(xs), scratch_shapes=_sibling_swap_sems(n),
    )(*xs)


def _sibling_swap_shapes(xs):
    return [jax.ShapeDtypeStruct((4,) + x.shape[1:], x.dtype) for x in xs]


def _sibling_swap_sems(n):
    return [pltpu.SemaphoreType.DMA((n,)), pltpu.SemaphoreType.DMA((n,))]


def _sibling_swap_phases(x_refs, out_refs, send_sems, recv_sems):
    ax, ay, ac = _place()
    sibling = (ax, ay, 1 - ac)

    def send():
        for a, (x_ref, out_ref) in enumerate(zip(x_refs, out_refs)):
            for chip in range(4):
                pltpu.make_async_remote_copy(src_ref=x_ref.at[2 * chip + 1 - ac], dst_ref=out_ref.at[chip],
                                             send_sem=send_sems.at[a], recv_sem=recv_sems.at[a],
                                             device_id=sibling, device_id_type=MESH).start()

    def finish():
        for a, (x_ref, out_ref) in enumerate(zip(x_refs, out_refs)):
            pltpu.make_async_remote_copy(src_ref=x_ref.at[pl.ds(0, 4)], dst_ref=out_ref, send_sem=send_sems.at[a],
                                         recv_sem=recv_sems.at[a], device_id=sibling, device_id_type=MESH).wait()

    return send, finish


def _swap_chips_call(xs, name):
    n = len(xs)

    def body(*refs):
        for phase in _chip_swap_phases(refs[:n], refs[n:2 * n], *refs[2 * n:]):
            phase()

    return pl.pallas_call(
        body, name=name, in_specs=_any_specs(n), out_specs=_any_specs(n),
        out_shape=_chip_swap_shapes(xs), scratch_shapes=_chip_swap_sems(n),
    )(*xs)


def _chip_swap_shapes(xs):
    return [jax.ShapeDtypeStruct((3,) + x.shape[1:], x.dtype) for x in xs]


def _chip_swap_sems(n):
    return [pltpu.SemaphoreType.DMA((3 * n,)), pltpu.SemaphoreType.DMA((3 * n,))]


def _chip_swap_phases(x_refs, out_refs, send_sems, recv_sems):
    ax, ay, ac = _place()
    chips = [(1 - ax, ay), (ax, 1 - ay), (1 - ax, 1 - ay)]
    copies = [pltpu.make_async_remote_copy(src_ref=x_refs[a].at[2 * cx + cy], dst_ref=out_refs[a].at[j],
                                           send_sem=send_sems.at[3 * a + j], recv_sem=recv_sems.at[3 * a + j],
                                           device_id=(cx, cy, ac), device_id_type=MESH)
              for a in range(len(x_refs)) for j, (cx, cy) in enumerate(chips)]

    def send():
        for cp in copies:
            cp.start()

    def finish():
        for cp in copies:
            cp.wait()

    return send, finish


def _reduce_begin(gs, name):
    ac = lax.axis_index("c")
    got = _swap_sibling_call(gs, name + "_d2d")
    return got, [_pair_sum_call(g, t, ac, f"{name}_pair{a}") for a, (g, t) in enumerate(zip(gs, got))]


def _reduce_end(gs, got, from_chips, name):
    ax, ay, ac = _place()
    me, my_chip = 4 * ax + 2 * ay + ac, 2 * ax + ay
    return [_sum_call([(g, me), (t, my_chip), (f, 0), (f, 1), (f, 2)], F32, f"{name}_total{a}")
            for a, (g, t, f) in enumerate(zip(gs, got, from_chips))]


SMALL = ("norm1_g", "conv_w", "a_log", "dt_bias", "dn_out_g", "sb_q_g", "sb_k_g", "sg_v_g", "sg_w", "sg_b", "norm2_g")
WEIGHTS = ("norm1_g", "w_in", "conv_w", "a_log", "dt_bias", "dn_out_g", "sb_q_g", "sb_k_g", "sg_v_g", "sg_w", "sg_b",
           "w_out", "norm2_g", "w_ff1", "w_ff2")
SMALL_SHAPE = {"norm1_g": (D,), "conv_w": (4, 3 * DN_W), "a_log": (NH,), "dt_bias": (NH,), "dn_out_g": (128,), "sb_q_g": (64,),
               "sb_k_g": (64,), "sg_v_g": (SG_W,), "sg_w": (NH, 128, 128), "sg_b": (NH, 128), "norm2_g": (D,)}


def _size(shape):
    n = 1
    for s in shape:
        n *= s
    return n


def _to_rows(flat, multiple):
    pad = (-flat.shape[0]) % (LANES * multiple)
    return jnp.pad(flat, (0, pad)).reshape(-1, LANES)


def _conv_by_pair(conv):
    return conv.reshape(4, 3, 2, 256).transpose(0, 2, 1, 3).reshape(4, 3 * DN_W)


def kernel(x, norm1_g, w_in, conv_w, a_log, dt_bias, dn_out_g, sb_q_g, sb_k_g, sg_v_g, sg_w, sg_b, w_out, norm2_g, w_ff1, w_ff2, loss_target, m_norm1_g, m_w_in, m_conv_w, m_a_log, m_dt_bias, m_dn_out_g, m_sb_q_g, m_sb_k_g, m_sg_v_g, m_sg_w, m_sg_b, m_w_out, m_norm2_g, m_w_ff1, m_w_ff2, v_norm1_g, v_w_in, v_conv_w, v_a_log, v_dt_bias, v_dn_out_g, v_sb_q_g, v_sb_k_g, v_sg_v_g, v_sg_w, v_sg_b, v_w_out, v_norm2_g, v_w_ff1, v_w_ff2):
    given = dict(norm1_g=norm1_g, w_in=w_in, conv_w=conv_w, a_log=a_log, dt_bias=dt_bias, dn_out_g=dn_out_g, sb_q_g=sb_q_g,
                 sb_k_g=sb_k_g, sg_v_g=sg_v_g, sg_w=sg_w, sg_b=sg_b, w_out=w_out, norm2_g=norm2_g, w_ff1=w_ff1, w_ff2=w_ff2)
    mom = dict(norm1_g=m_norm1_g, w_in=m_w_in, conv_w=m_conv_w, a_log=m_a_log, dt_bias=m_dt_bias, dn_out_g=m_dn_out_g,
               sb_q_g=m_sb_q_g, sb_k_g=m_sb_k_g, sg_v_g=m_sg_v_g, sg_w=m_sg_w, sg_b=m_sg_b, w_out=m_w_out, norm2_g=m_norm2_g,
               w_ff1=m_w_ff1, w_ff2=m_w_ff2)
    var = dict(norm1_g=v_norm1_g, w_in=v_w_in, conv_w=v_conv_w, a_log=v_a_log, dt_bias=v_dt_bias, dn_out_g=v_dn_out_g,
               sb_q_g=v_sb_q_g, sb_k_g=v_sb_k_g, sg_v_g=v_sg_v_g, sg_w=v_sg_w, sg_b=v_sg_b, w_out=v_w_out, norm2_g=v_norm2_g,
               w_ff1=v_w_ff1, w_ff2=v_w_ff2)
    B, T, _ = x.shape
    M = B * T
    ax, ay, ac = _place()
    me = 4 * ax + 2 * ay + ac
    table_fwd, table_back = _row_tables()

    send = []
    for l in range(2):
        w_in_t = jnp.pad(w_in[l].T, ((0, IN_SHARD_PAD - IN_SHARD), (0, 0)))
        send.append([w_in_t.astype(BF16), w_out[l].astype(BF16), w_ff1[l].astype(BF16), w_ff2[l].astype(BF16)])
    first_in, conv_rows = _all_gather_call([send[0][0], _to_rows(conv_w.reshape(-1), 8)], "gather_first")
    conv_full = conv_rows.reshape(NDEV, -1)[:, :conv_w.size].reshape(NDEV, 2, 4, -1).transpose(1, 2, 0, 3).reshape(2, 4, 3 * DN_W)
    gathered = [[first_in, None, None, None], [None] * 4]

    pad_vec = lambda v: jnp.zeros((1, LANES), F32).at[0, :v.shape[0]].set(v)
    layer = []
    for l in range(2):
        layer.append(dict(
            g1=norm1_g[l].reshape(1, D), g2=norm2_g[l].reshape(1, D), conv=_conv_by_pair(conv_full[l]),
            a_log=pad_vec(a_log[l]), dt_bias=pad_vec(dt_bias[l]), dn_g=dn_out_g[l].reshape(1, LANES),
            sb_qg=jnp.tile(sb_q_g[l], 2).reshape(1, LANES), sb_kg=jnp.tile(sb_k_g[l], 2).reshape(1, LANES),
            sg_g=sg_v_g[l].reshape(1, SG_W), sg_w=sg_w[l], sg_bias=jnp.repeat(sg_b[l].T, 64, axis=1)))

    cur = x.reshape(M, D)
    saved = []
    for l, p in enumerate(layer):
        p["wt"] = _row_perm_call(gathered[l][0].reshape(NDEV * IN_SHARD_PAD, D), table_fwd, "pack_w_in")
        p_dn, p_sb, p_sg, p_ab, h = _in_proj_call(cur, p["g1"], p["wt"])
        mix, arrived = _dn_fwd_call(p_dn, p_ab, p["conv"], p["a_log"], p["dt_bias"], p["dn_g"], B, T,
                                    gather=send[0][1:] + send[1][:1] if l == 0 else [])
        if l == 0:
            gathered[0][1:], gathered[1][0] = list(arrived[:3]), arrived[3]
        p["w_out"], p["w1"], p["w2"] = gathered[l][1].reshape(D, D), gathered[l][2], gathered[l][3].reshape(DFF, D)
        mix, arrived = _sb_fwd_call(p_sb, mix, p["sb_qg"], p["sb_kg"], B, T, gather=send[1][1:] if l == 0 else [])
        if l == 0:
            gathered[1][1:] = list(arrived)
        mix = _sg_fwd_call(p_sg, mix, p["sg_g"], p["sg_w"], p["sg_bias"], B, T)
        x1 = _out_proj_call(mix, p["w_out"], cur)
        x2 = _ffn_fwd_call(x1, p["g2"], p["w1"], p["w2"])
        saved.append(dict(x0=cur, p_dn=p_dn, p_sb=p_sb, p_sg=p_sg, p_ab=p_ab, h=h, mix=mix, x1=x1))
        cur = x2
    loss_part, dy = _loss_call(cur, loss_target.reshape(M, D))
    loss = lax.psum(loss_part[0, 0], ("x", "y", "c"))

    big_grads = [[None] * 4, [None] * 4]
    small_grads = {n: [None, None] for n in SMALL}
    for l in (1, 0):
        p, s = layer[l], saved[l]
        (dx1, da, r, h2, dg2), got1 = _ffn_bwd_call(s["x1"], dy, p["g2"], p["w1"], p["w2"], swap=big_grads[1] if l == 0 else ())
        big_grads[l][2] = _mm_tn_call(h2, da, "grad_w_ff1", col_shards=True)
        big_grads[l][3] = _mm_tn_call(r, dy, "grad_w_ff2").reshape(NDEV, FF_SHARD, D)
        dmix = _mm_nt_call(dx1, p["w_out"], "dmix")
        big_grads[l][1] = _mm_tn_call(s["mix"], dx1, "grad_w_out").reshape(NDEV, D // NDEV, D)
        if l == 0:
            got0, sums0 = _reduce_begin(big_grads[0][1:], "reduce_early0")
            early, early_got = big_grads[1] + big_grads[0][1:], list(got1) + list(got0)
            early_sums = [_pair_sum_call(g, t, ac, f"reduce_early1_pair{a}") for a, (g, t) in enumerate(zip(big_grads[1], got1))] + sums0
        (d_dn, d_ab, dcw, dalog, ddtb, ddn_g), early_from = _dn_bwd_call(
            s["p_dn"], s["p_ab"], dmix, p["conv"], p["a_log"], p["dt_bias"], p["dn_g"], B, T, swap=early_sums if l == 0 else ())
        d_sb, dqg, dkg = _sb_bwd_call(s["p_sb"], dmix, p["sb_qg"], p["sb_kg"], B, T)
        d_sg, dsg_g, dsg_w, dsg_b = _sg_bwd_call(s["p_sg"], dmix, p["sg_g"], p["sg_w"], p["sg_bias"], B, T)
        dsections = (d_dn, d_sb, d_sg, d_ab)
        dy, dg1 = _in_proj_bwd_call(dsections, p["wt"], s["x0"], p["g1"], dx1)
        dwt = _in_proj_grad_call(dsections, s["h"])
        big_grads[l][0] = _row_perm_call(dwt, table_back, "unpack_grad_w_in").reshape(NDEV, IN_SHARD_PAD, D)
        for n, val in (("norm1_g", dg1[0]), ("conv_w", dcw.transpose(1, 0, 2).reshape(4, 3 * DN_W)), ("a_log", dalog[0, :NH]),
                       ("dt_bias", ddtb[0, :NH]), ("dn_out_g", ddn_g[0]), ("sb_q_g", dqg[0, :64]), ("sb_k_g", dkg[0, :64]),
                       ("sg_v_g", dsg_g[0]), ("sg_w", dsg_w), ("sg_b", dsg_b[:, :NH].T), ("norm2_g", dg2[0])):
            small_grads[n][l] = val
    grad_x = dy.reshape(B, T, D)

    last = big_grads[0][:1]
    last_got, last_sums = _reduce_begin(last, "reduce_last")
    mine0 = _reduce_end(last, last_got, _swap_chips_call(last_sums, "reduce_last_ici"), "reduce_last")
    mine1 = _reduce_end(early, early_got, early_from, "reduce_early")
    grads = {"w_in": jnp.stack([mine0[0][:IN_SHARD].T, mine1[0][:IN_SHARD].T]), "w_out": jnp.stack([mine1[4], mine1[1]]),
             "w_ff1": jnp.stack([mine1[5], mine1[2]]), "w_ff2": jnp.stack([mine1[6], mine1[3]])}
    small_flat = jnp.concatenate([jnp.stack(small_grads[n]).reshape(-1) for n in SMALL])
    everyone, = _all_gather_call([_to_rows(small_flat, 8)], "gather_small_grads")
    small_sum = _sum_call([(everyone, k) for k in range(NDEV)], F32, "sum_small_grads").reshape(-1)
    off = 0
    for n in SMALL:
        sz = 2 * _size(SMALL_SHAPE[n])
        grads[n] = small_sum[off:off + sz].reshape((2,) + SMALL_SHAPE[n])
        off += sz
    cshard = conv_w.shape[-1]
    grads["conv_w"] = lax.dynamic_slice_in_dim(grads["conv_w"], me * cshard, cshard, axis=2)

    deltas, new_m, new_v = {}, {}, {}
    for n in WEIGHTS:
        deltas[n], new_m[n], new_v[n] = _adamw_call(given[n], grads[n], mom[n], var[n], "adamw_" + n)
    return (loss, grad_x, *[grads[n] for n in WEIGHTS], *[deltas[n] for n in WEIGHTS], *[new_m[n] for n in WEIGHTS],
            *[new_v[n] for n in WEIGHTS])
```

```python
import functools

import numpy as np

import jax
import jax.numpy as jnp
from jax import lax
from jax.experimental import pallas as pl
from jax.experimental.pallas import tpu as pltpu

F32, BF16 = jnp.float32, jnp.bfloat16
EPS = 1e-6
LANES = 128
D = 1024
DFF = 4096
NH = 4
DN_W, SB_W, SG_W = 512, 256, 256
IN_DIM = 3336
NDEV = 8
IN_SHARD = IN_DIM // NDEV
IN_SHARD_PAD = 432
FF_SHARD = DFF // NDEV
DN_OFF, SB_OFF, SG_OFF, AB_OFF, NPACK = 0, 2048, 2816, 3328, 3456
SECTIONS = ((DN_OFF, 2048), (SB_OFF, 768), (SG_OFF, 512), (AB_OFF, 128))
SB_SCALE = 64 ** -0.5
DN_SCALE = 128 ** -0.5
VMEM_LIMIT = 56 * 1024 * 1024
VMEM_LIMIT_MAX = 62 * 1024 * 1024
ADAM_LR, ADAM_B1, ADAM_B2, ADAM_EPS, ADAM_WD, ADAM_STEP = 0.001, 0.9, 0.999, 1e-08, 0.01, 10
MESH = pl.DeviceIdType.MESH


def _iota(shape, dim):
    return lax.broadcasted_iota(jnp.int32, shape, dim)


def _params(**kw):
    return pltpu.CompilerParams(vmem_limit_bytes=VMEM_LIMIT, **kw)


NN, NT, TN = ((1,), (0,)), ((1,), (1,)), ((0,), (0,))


def _mm(a, b, dims):
    return lax.dot_general(a.astype(BF16), b.astype(BF16), (dims, ((), ())), preferred_element_type=F32)


def _plain(a, b, dims):
    return (a.T if dims == TN else a), (b.T if dims == NT else b)


def _mmx(a, b, dims):
    return _mm(*_plain(a, b, dims), NN)


@jax.custom_vjp
def _dot(a, b):
    return _mmx(a, b, NN)


def _dot_fwd(a, b):
    return _dot(a, b), (a, b)


def _dot_bwd(res, g):
    a, b = res
    return _mmx(g, b, NT).astype(a.dtype), _mmx(a, g, TN).astype(b.dtype)


_dot.defvjp(_dot_fwd, _dot_bwd)


@jax.custom_vjp
def _dot_nt(a, b):
    return _mmx(a, b, NT)


def _dot_nt_fwd(a, b):
    return _dot_nt(a, b), (a, b)


def _dot_nt_bwd(res, g):
    a, b = res
    return _mmx(g, b, NN).astype(a.dtype), _mmx(g, a, TN).astype(b.dtype)


_dot_nt.defvjp(_dot_nt_fwd, _dot_nt_bwd)


@jax.custom_vjp
def _dot_tn(a, b):
    return _mmx(a, b, TN)


def _dot_tn_fwd(a, b):
    return _dot_tn(a, b), (a, b)


def _dot_tn_bwd(res, g):
    a, b = res
    return _mmx(b, g, NT).astype(a.dtype), _mmx(a, g, NN).astype(b.dtype)


_dot_tn.defvjp(_dot_tn_fwd, _dot_tn_bwd)


def _split(x):
    hi = x.astype(BF16)
    return hi, (x - hi.astype(F32)).astype(BF16)


def _mm3(a, b, dims):
    a, b = _plain(a, b, dims)
    (ah, al), (bh, bl) = _split(a), _split(b)
    mm = lambda x, y: jnp.dot(x, y, preferred_element_type=F32)
    return mm(ah, bh) + (mm(ah, bl) + mm(al, bh))


def _mm_ones(ones, x, ones_left):
    hi, lo = _split(x)
    mm = (lambda t: jnp.dot(ones, t, preferred_element_type=F32)) if ones_left else \
         (lambda t: jnp.dot(t, ones, preferred_element_type=F32))
    return mm(hi) + mm(lo)


def _pair_ones(kind, transposed):
    row, col = _iota((128, 128), 0), _iota((128, 128), 1)
    m = (row // 64) == (col // 64)
    if kind == "running":
        m = jnp.logical_and(m, (col >= row) if transposed else (col <= row))
    return jnp.where(m, 1.0, 0.0).astype(BF16)


@functools.partial(jax.custom_vjp, nondiff_argnums=(0,))
def _chunk_sum(kind, x):
    return _mm_ones(_pair_ones(kind, False), x, True)


def _chunk_sum_fwd(kind, x):
    return _chunk_sum(kind, x), None


def _chunk_sum_bwd(kind, _, g):
    return (_mm_ones(_pair_ones(kind, True), g, True),)


_chunk_sum.defvjp(_chunk_sum_fwd, _chunk_sum_bwd)


def _tri_ones(n, transposed):
    row, col = _iota((n, n), 0), _iota((n, n), 1)
    return jnp.where((row < col) if transposed else (row > col), 1.0, 0.0).astype(BF16)


@jax.custom_vjp
def _suffix_sum(x):
    return _mm_ones(_tri_ones(x.shape[1], False), x, False)


def _suffix_sum_fwd(x):
    return _suffix_sum(x), None


def _suffix_sum_bwd(_, g):
    return (_mm_ones(_tri_ones(g.shape[1], True), g, False),)


_suffix_sum.defvjp(_suffix_sum_fwd, _suffix_sum_bwd)


def _sigmoid(x):
    return jax.nn.sigmoid(x)


def _silu(x):
    return x * _sigmoid(x)


def _softplus(x):
    return jnp.maximum(x, 0.0) + jnp.log1p(jnp.exp(-jnp.abs(x)))


def _gelu(x):
    return 0.5 * x * (1.0 + jnp.tanh(0.7978845608028654 * (x + 0.044715 * (x * x * x))))


def _rms(x, gain):
    return x * lax.rsqrt(jnp.mean(x * x, axis=-1, keepdims=True) + EPS) * gain


def _shift_down_impl(x, k):
    return jnp.where(_iota(x.shape, 0) >= k, pltpu.roll(x, k, 0), 0.0)


def _shift_up_impl(x, k):
    n = x.shape[0]
    return jnp.where(_iota(x.shape, 0) < n - k, pltpu.roll(x, n - k, 0), 0.0)


@functools.partial(jax.custom_vjp, nondiff_argnums=(1,))
def _shift_down(x, k):
    return _shift_down_impl(x, k)


def _shift_down_fwd(x, k):
    return _shift_down_impl(x, k), None


def _shift_down_bwd(k, _, g):
    return (_shift_up_impl(g, k),)


_shift_down.defvjp(_shift_down_fwd, _shift_down_bwd)


def _lane_pick(x, idx):
    return jnp.sum(jnp.where(_iota(x.shape, 1) == idx, x, 0.0), axis=-1, keepdims=True)


def _dn_conv(x, w0, w1, w2, w3, l2_scale):
    y = _silu(w3 * x + w2 * _shift_down(x, 1) + w1 * _shift_down(x, 2) + w0 * _shift_down(x, 3))
    if l2_scale is None:
        return y
    return y * lax.rsqrt(jnp.sum(y * y, axis=-1, keepdims=True) + EPS) * l2_scale


def _dn_gate(a, b, a_log, dt_bias):
    return -jnp.exp(a_log) * _softplus(a + dt_bias), _sigmoid(b)


def _same_head(shape):
    return (_iota(shape, 0) < LANES) == (_iota(shape, 1) < LANES)


def _bd(r2):
    return jnp.where(_same_head((2 * LANES, 2 * LANES)), jnp.concatenate([r2, r2], axis=0), 0.0)


def _bd_t(y2):
    t = y2.T
    return jnp.where(_same_head((2 * LANES, 2 * LANES)), jnp.concatenate([t, t], axis=1), 0.0)


def _pair_prod(kind, a2, b2, mm):
    if kind == NN:
        return mm(a2, _bd(b2))
    if kind == NT:
        return mm(a2, _bd_t(b2))
    full = mm(a2.T, b2)
    return jnp.concatenate([full[:LANES, :LANES], full[LANES:, LANES:]], axis=1)


_MM1 = lambda x, y: _mm(x, y, NN)
_MM3 = lambda x, y: _mm3(x, y, NN)


def _pair_vjp_rule(kind, a2, b2, g, mm):
    if kind == NN:
        return _pair_prod(NT, g, b2, mm), _pair_prod(TN, a2, g, mm)
    if kind == NT:
        return _pair_prod(NN, g, b2, mm), _pair_prod(TN, g, a2, mm)
    return _pair_prod(NT, b2, g, mm), _pair_prod(NN, a2, g, mm)


@functools.partial(jax.custom_vjp, nondiff_argnums=(0,))
def _pdot(kind, a2, b2):
    return _pair_prod(kind, a2, b2, _MM1)


def _pdot_fwd(kind, a2, b2):
    return _pdot(kind, a2, b2), (a2, b2)


def _pdot_bwd(kind, res, g):
    return _pair_vjp_rule(kind, *res, g, _MM1)


_pdot.defvjp(_pdot_fwd, _pdot_bwd)


def _unit_lower_inverse(lower):
    n = lower.shape[0]
    nk = -lower
    inv = jnp.where(_iota(lower.shape, 0) == jnp.bitwise_and(_iota(lower.shape, 1), n - 1), 1.0, 0.0) + nk
    for _ in range(5):
        nk = _pair_prod(NN, nk, nk, _MM1)
        inv = inv + _pair_prod(NN, inv, nk, _MM1)
    return inv


@jax.custom_vjp
def _solve_with(lower, inv, rhs):
    return _pair_prod(NN, inv, rhs, _MM3)


def _solve_with_fwd(lower, inv, rhs):
    x = _pair_prod(NN, inv, rhs, _MM3)
    return x, (inv, x)


def _solve_with_bwd(res, g):
    inv, x = res
    d_rhs = _pair_prod(TN, inv, g, _MM3)
    return -_pair_prod(NT, d_rhs, x, _MM3), jnp.zeros_like(inv), d_rhs


_solve_with.defvjp(_solve_with_fwd, _solve_with_bwd)


def _dn_local(q, k, v, g, beta, inv=None):
    shape = (LANES, 2 * LANES)
    row, col = _iota(shape, 0), jnp.bitwise_and(_iota(shape, 1), LANES - 1)
    same = (row // 64) == (col // 64)
    tri_incl = jnp.logical_and(same, col <= row)
    tri_strict = jnp.logical_and(same, col < row)
    first = row < 64
    gc = _chunk_sum("running", g)
    gl = _chunk_sum("total", g)
    diff = gc - jnp.concatenate([gc[:, :LANES].T, gc[:, LANES:].T], axis=1)
    decay = jnp.where(tri_incl, jnp.exp(jnp.where(tri_incl, diff, 0.0)), 0.0)
    egc = jnp.exp(gc)
    lower = jnp.where(tri_strict, beta * _pdot(NT, k, k) * decay, 0.0)
    if inv is None:
        inv = _unit_lower_inverse(lower)
    u_val = _solve_with(lower, inv, v * beta)
    w_dec = _solve_with(lower, inv, k * (beta * egc))
    qk = jnp.where(tri_incl, _pdot(NT, q, k) * decay, 0.0)
    q_dec = q * egc
    k_dec = k * jnp.exp(gl - gc)
    cd1 = jnp.exp(jnp.sum(jnp.where(first, g, 0.0), axis=0, keepdims=True))
    cd2 = jnp.exp(jnp.sum(jnp.where(first, 0.0, g), axis=0, keepdims=True))
    return (u_val, w_dec, qk, q_dec, k_dec, cd1, cd2), inv


def _dn_state(u_val, w_dec, qk, q_dec, k_dec, cd1, cd2, s0):
    first = _iota((LANES, 2 * LANES), 0) < 64
    u1 = u_val - _pdot(NN, w_dec, s0)
    s1 = s0 * cd1 + _pdot(TN, jnp.where(first, k_dec, 0.0), u1)
    u2 = u_val - _pdot(NN, w_dec, s1)
    u_new = jnp.where(first, u1, u2)
    s2 = s1 * cd2 + _pdot(TN, jnp.where(first, 0.0, k_dec), u_new)
    o = jnp.where(first, _pdot(NN, q_dec, s0), _pdot(NN, q_dec, s1)) + _pdot(NN, qk, u_new)
    return o, s2


def _dn_post(o, z, gain):
    return _rms(o, gain) * _silu(z)


def _dn_gate_in(ab_ref, alog_ref, dtb_ref, h):
    ab = ab_ref[...]
    return _lane_pick(ab, h), _lane_pick(ab, h + NH), _lane_pick(alog_ref[...], h), _lane_pick(dtb_ref[...], h)


_DN_L2 = (DN_SCALE, 1.0, None)
DN_HPS = 2
DN_BLK = 4 * DN_HPS * LANES
_DN_COLS = tuple(slice(i * LANES, (i + 1) * LANES) for i in range(DN_HPS))


def _dn_in_cols(s, i):
    return slice((s * DN_HPS + i) * LANES, (s * DN_HPS + i + 1) * LANES)


def _dn_taps(cw_ref, s, i):
    return tuple(cw_ref[t:t + 1, _dn_in_cols(s, i)] for t in range(4))


def _dn_pack_gate(vals):
    lane = _iota((1, LANES), 1)
    out = 0.0
    for i, (g, beta) in enumerate(vals):
        out = out + jnp.where(lane == 2 * i, g, 0.0) + jnp.where(lane == 2 * i + 1, beta, 0.0)
    return out


def _pair_rows(n):
    return pl.ds(pl.multiple_of(n * 128, 128), 128)


def _dn_gate_rows(gate):
    head_a = _iota((1, DN_HPS * LANES), 1) < LANES
    return (jnp.where(head_a, _lane_pick(gate, 0), _lane_pick(gate, 2)),
            jnp.where(head_a, _lane_pick(gate, 1), _lane_pick(gate, 3)))


def _dn_gate_cols(dg, db):
    head_a = _iota((1, DN_HPS * LANES), 1) < LANES
    fold = lambda t: (jnp.sum(jnp.where(head_a, t, 0.0), axis=-1, keepdims=True),
                      jnp.sum(jnp.where(head_a, 0.0, t), axis=-1, keepdims=True))
    (dg_a, dg_b), (db_a, db_b) = fold(dg), fold(db)
    return [(dg_a, db_a), (dg_b, db_b)]


def _dn_in_specs(T):
    one = pl.Buffered(1)
    vec = pl.BlockSpec((1, LANES), lambda b, h: (0, 0))
    return [pl.BlockSpec((T, DN_BLK), lambda b, h: (b, h), pipeline_mode=one),
            pl.BlockSpec((T, LANES), lambda b, h: (b, 0), pipeline_mode=one),
            pl.BlockSpec((4, 3 * DN_HPS * LANES), lambda b, h: (0, h)), vec, vec, vec]


def _dn_fwd_call(proj_dn, proj_ab, conv_w, a_log, dt_bias, gain, B, T, gather=()):
    npair = T // 128
    ng = len(gather)
    nsteps = B * (NH // DN_HPS)

    def body(*refs):
        x_ref, ab_ref, cw_ref, alog_ref, dtb_ref, gain_ref = refs[:6]
        out_ref, q_s, k_s, v_s, o_s, gate_s, st_s, inv_s = refs[6 + ng:14 + ng]
        step_id = pl.program_id(0) * (NH // DN_HPS) + pl.program_id(1)
        if ng:
            send, forward, finish = _gather_phases(refs[6:6 + ng], refs[14 + ng:14 + 2 * ng], *refs[14 + 2 * ng:])
            pl.when(step_id == 0)(send)
            pl.when(step_id == nsteps - 1)(forward)
        hp = pl.program_id(1)
        gates = []
        for i, cs in enumerate(_DN_COLS):
            for s, (x_s, l2) in enumerate(zip((q_s, k_s, v_s), _DN_L2)):
                x_s[:, cs] = _dn_conv(x_ref[:, _dn_in_cols(s, i)], *_dn_taps(cw_ref, s, i), l2)
            gates.append(_dn_gate(*_dn_gate_in(ab_ref, alog_ref, dtb_ref, DN_HPS * hp + i)))
        gate_s[...] = _dn_pack_gate(gates)

        def local_of(pair):
            r = _pair_rows(pair)
            loc, inv = _dn_local(q_s[r, :], k_s[r, :], v_s[r, :], *_dn_gate_rows(gate_s[r, :]))
            inv_s[0, 0, pair] = inv
            return loc

        def state_of(n, loc, state):
            st_s[0, 0, n] = state
            o, s2 = _dn_state(*loc, state)
            o_s[_pair_rows(n), :] = o
            return s2

        def step(n, carry):
            loc, state = carry
            return local_of(n + 1), state_of(n, loc, state)

        loc, state = lax.fori_loop(0, npair - 1, step, (local_of(0), jnp.zeros((LANES, DN_HPS * LANES), F32)))
        state_of(npair - 1, loc, state)
        for i, cs in enumerate(_DN_COLS):
            out_ref[:, cs] = _dn_post(o_s[:, cs], x_ref[:, _dn_in_cols(3, i)], gain_ref[...])
        if ng:
            pl.when(step_id == nsteps - 1)(finish)

    kept_specs, kept_shapes = _dn_kept(B, T)
    outs = pl.pallas_call(
        body, name="dn_fwd", grid=(B, NH // DN_HPS), in_specs=_dn_in_specs(T) + _any_specs(ng),
        out_specs=[pl.BlockSpec((T, DN_HPS * LANES), lambda b, h: (b, h), pipeline_mode=pl.Buffered(1))] + kept_specs + _any_specs(ng),
        out_shape=[jax.ShapeDtypeStruct((B * T, D), F32)] + kept_shapes + _gather_shapes(gather),
        scratch_shapes=_gather_sems(ng) if ng else [],
        compiler_params=_params(dimension_semantics=("arbitrary", "arbitrary")),
    )(proj_dn, proj_ab, conv_w, a_log, dt_bias, gain, *gather)
    return outs[0], outs[1:8], outs[8:]


def _dn_kept(B, T):
    one = pl.Buffered(1)
    npair, pairs = T // 128, NH // DN_HPS
    wide = pl.BlockSpec((T, DN_HPS * LANES), lambda b, h: (b, h), pipeline_mode=one)
    per_pair = pl.BlockSpec((1, 1, npair, LANES, DN_HPS * LANES), lambda b, h: (b, h, 0, 0, 0), pipeline_mode=one)
    specs = [wide] * 4 + [pl.BlockSpec((T, LANES), lambda b, h: (b, h), pipeline_mode=one)] + [per_pair] * 2
    shapes = ([jax.ShapeDtypeStruct((B * T, DN_W), F32)] * 4 + [jax.ShapeDtypeStruct((B * T, pairs * LANES), F32)]
              + [jax.ShapeDtypeStruct((B, pairs, npair, LANES, DN_HPS * LANES), F32)] * 2)
    return specs, shapes


def _dn_bwd_call(proj_dn, proj_ab, dmix, kept, conv_w, a_log, dt_bias, gain, B, T, swap=()):
    npair = T // 128
    ns = len(swap)
    nsteps = B * (NH // DN_HPS)

    def body(*refs):
        x_ref, ab_ref, cw_ref, alog_ref, dtb_ref, gain_ref, do_ref, q_s, k_s, v_s, o_ref, gate_s, st_s, inv_s = refs[:14]
        dx_ref, dab_ref, dcw_ref, dalog_ref, ddtb_ref, dgain_ref = refs[14 + ns:20 + ns]
        dgate_s, dcd_s = refs[20 + 2 * ns:22 + 2 * ns]
        dloc_s = refs[22 + 2 * ns:27 + 2 * ns]
        do_s = dloc_s[0]
        b_i, hp = pl.program_id(0), pl.program_id(1)
        step_id = b_i * (NH // DN_HPS) + hp
        if ns:
            send, finish = _chip_swap_phases(refs[14:14 + ns], refs[20 + ns:20 + 2 * ns], *refs[27 + 2 * ns:])
            pl.when(step_id == 0)(send)

        def pair_in(r):
            return (q_s[r, :], k_s[r, :], v_s[r, :]) + _dn_gate_rows(gate_s[r, :])

        def local_of(pair):
            return _dn_local(*pair_in(_pair_rows(pair)), inv_s[0, 0, pair])[0]

        zero_state = jnp.zeros((LANES, DN_HPS * LANES), F32)

        @pl.when(jnp.logical_and(b_i == 0, hp == 0))
        def _():
            dcw_ref[...] = jnp.zeros_like(dcw_ref)
            dalog_ref[...] = jnp.zeros_like(dalog_ref)
            ddtb_ref[...] = jnp.zeros_like(ddtb_ref)
            dgain_ref[...] = jnp.zeros_like(dgain_ref)

        for i, cs in enumerate(_DN_COLS):
            zc = _dn_in_cols(3, i)
            _, post_vjp = jax.vjp(_dn_post, o_ref[:, cs], x_ref[:, zc], gain_ref[...])
            do, dz, dgain = post_vjp(do_ref[:, cs])
            dx_ref[:, zc] = dz
            do_s[:, cs] = do
            dgain_ref[...] += dgain

        def state_back(nn, dstate):
            n = npair - 1 - nn
            r = _pair_rows(n)
            _, state_vjp = jax.vjp(_dn_state, *local_of(n), st_s[0, 0, n])
            *dloc, ds0 = state_vjp((do_s[r, :], dstate))
            for d_s, val in zip(dloc_s, dloc[:5]):
                d_s[r, :] = val
            dcd_s[n, 0:1, :], dcd_s[n, 1:2, :] = dloc[5], dloc[6]
            return ds0

        lax.fori_loop(0, npair, state_back, zero_state)

        wide_cols = lambda s: slice(s * DN_HPS * LANES, (s + 1) * DN_HPS * LANES)

        def local_back(m, _):
            for pair in (2 * m, 2 * m + 1):
                r = _pair_rows(pair)
                inv = inv_s[0, 0, pair]
                local = lambda q, k, v, g, beta, inv=inv: _dn_local(q, k, v, g, beta, inv)[0]
                _, local_vjp = jax.vjp(local, *pair_in(r))
                dq, dk, dv, dg, db = local_vjp(tuple(d_s[r, :] for d_s in dloc_s) + (dcd_s[pair, 0:1, :], dcd_s[pair, 1:2, :]))
                dx_ref[r, wide_cols(0)], dx_ref[r, wide_cols(1)], dx_ref[r, wide_cols(2)] = dq, dk, dv
                dgate_s[r, :] = _dn_pack_gate(_dn_gate_cols(dg, db))
            return 0

        lax.fori_loop(0, npair // 2, local_back, 0)

        lane = _iota((1, LANES), 1)
        dab = 0.0
        for i, cs in enumerate(_DN_COLS):
            h = DN_HPS * hp + i
            for s, l2 in enumerate(_DN_L2):
                xc = _dn_in_cols(s, i)
                _, conv_vjp = jax.vjp(functools.partial(_dn_conv, l2_scale=l2), x_ref[:, xc], *_dn_taps(cw_ref, s, i))
                dx, *dw = conv_vjp(dx_ref[:, xc])
                dx_ref[:, xc] = dx
                for t in range(4):
                    dcw_ref[h + 4 * s, t:t + 1, :] += dw[t]
            _, gate_vjp = jax.vjp(_dn_gate, *_dn_gate_in(ab_ref, alog_ref, dtb_ref, h))
            dgate = dgate_s[...]
            da, db, dalog, ddtb = gate_vjp((_lane_pick(dgate, 2 * i), _lane_pick(dgate, 2 * i + 1)))
            dab = dab + jnp.where(lane == h, da, 0.0) + jnp.where(lane == h + NH, db, 0.0)
            dalog_ref[...] += jnp.where(lane == h, dalog, 0.0)
            ddtb_ref[...] += jnp.where(lane == h, ddtb, 0.0)

        @pl.when(hp == 0)
        def _():
            dab_ref[...] = jnp.zeros_like(dab_ref)

        dab_ref[...] += dab
        if ns:
            pl.when(step_id == nsteps - 1)(finish)

    M = B * T
    one = pl.Buffered(1)
    vec = pl.BlockSpec((1, LANES), lambda b, h: (0, 0))
    wide = [pltpu.VMEM((T, DN_HPS * LANES), F32)]
    vec_shape = jax.ShapeDtypeStruct((1, LANES), F32)
    outs = pl.pallas_call(
        body, name="dn_bwd", grid=(B, NH // DN_HPS),
        in_specs=_dn_in_specs(T) + [pl.BlockSpec((T, DN_HPS * LANES), lambda b, h: (b, h), pipeline_mode=one)] + _dn_kept(B, T)[0]
        + _any_specs(ns),
        out_specs=[pl.BlockSpec((T, DN_BLK), lambda b, h: (b, h), pipeline_mode=one), pl.BlockSpec((T, LANES), lambda b, h: (b, 0)),
                   pl.BlockSpec((12, 4, LANES), lambda b, h: (0, 0, 0)), vec, vec, vec] + _any_specs(ns),
        out_shape=[jax.ShapeDtypeStruct((M, 4 * DN_W), F32), jax.ShapeDtypeStruct((M, LANES), F32),
                   jax.ShapeDtypeStruct((12, 4, LANES), F32), vec_shape, vec_shape, vec_shape] + _chip_swap_shapes(swap),
        scratch_shapes=[pltpu.VMEM((T, LANES), F32), pltpu.VMEM((npair, 8, DN_HPS * LANES), F32)] + wide * 5
        + (_chip_swap_sems(ns) if ns else []),
        compiler_params=pltpu.CompilerParams(vmem_limit_bytes=VMEM_LIMIT_MAX, dimension_semantics=("arbitrary", "arbitrary")),
    )(proj_dn, proj_ab, conv_w, a_log, dt_bias, gain, dmix, *kept, *swap)
    return outs[:6], outs[6:]


SBQ = 256


def _group_rms(x, gain):
    first = _iota(x.shape, 1) < 64
    sq = x * x
    ss_a = jnp.sum(jnp.where(first, sq, 0.0), axis=-1, keepdims=True)
    ss_b = jnp.sum(jnp.where(first, 0.0, sq), axis=-1, keepdims=True)
    ms = jnp.where(first, ss_a, ss_b) * (1.0 / 64)
    return x * lax.rsqrt(ms + EPS) * gain


def _sb_stack(q):
    first = _iota((1, LANES), 1) < 64
    return jnp.concatenate([jnp.where(first, q, 0.0), jnp.where(first, 0.0, q)], axis=0)


def _sb_fold(acc):
    return jnp.where(_iota((1, LANES), 1) < 64, acc[:SBQ], acc[SBQ:])


def _sb_logs(q2, k, diag):
    n = SBQ
    z = _mm(q2, k, ((1,), (1,))) * SB_SCALE
    ls_pos = jnp.minimum(z, 0.0) - jnp.log1p(jnp.exp(-jnp.abs(z)))
    l1m = ls_pos - z
    if not diag:
        return ls_pos, l1m, None
    mask = _iota((2 * n, n), 1) < jnp.bitwise_and(_iota((2 * n, n), 0), n - 1)
    return ls_pos, jnp.where(mask, l1m, 0.0), mask


def _sb_weights(ls_pos, l1m, mask, carry):
    w = jnp.exp(ls_pos + (_mm_ones(_tri_ones(SBQ, False), l1m, False) + carry))
    return w if mask is None else jnp.where(mask, w, 0.0)


def _sb_block(q, k, v, carry, diag):
    ls_pos, l1m, mask = _sb_logs(_sb_stack(q), k, diag)
    w = _sb_weights(ls_pos, l1m, mask, carry)
    return _mm(w, v, ((1,), (0,))), carry + jnp.sum(l1m, axis=-1, keepdims=True)


def _sb_rowsum(q, k, diag):
    return jnp.sum(_sb_logs(_sb_stack(q), k, diag)[1], axis=-1, keepdims=True)


def _sb_block_bwd(q, k, v, carry, diag, dpv, dcarry):
    q2 = _sb_stack(q)
    ls_pos, l1m, mask = _sb_logs(q2, k, diag)
    w = _sb_weights(ls_pos, l1m, mask, carry)
    dv = _mm(w, dpv, ((0,), (0,)))
    de = _mm(dpv, v, ((1,), (1,))) * w
    dl1m = _mm_ones(_tri_ones(SBQ, True), de, False) + dcarry
    if mask is not None:
        dl1m = jnp.where(mask, dl1m, 0.0)
    sig = jnp.exp(ls_pos)
    dz = (de * (1.0 - sig) - dl1m * sig) * SB_SCALE
    dq = _sb_fold(_mm(dz, k, ((1,), (0,))))
    return dq, _mm(dz, q2, ((0,), (0,))), dv, dcarry + jnp.sum(de, axis=-1, keepdims=True)


_SB_Q, _SB_K, _SB_V = (slice(i * LANES, (i + 1) * LANES) for i in range(3))


def _sb_fwd_call(proj_sb, mix, q_gain, k_gain, B, T, gather=()):
    nblk = T // SBQ
    ng = len(gather)
    nsteps = 2 * B

    def body(*refs):
        x_ref, qg_ref, kg_ref = refs[:3]
        out_ref = refs[4 + ng]
        q_s, k_s = refs[5 + 2 * ng:7 + 2 * ng]
        step_id = 2 * pl.program_id(0) + pl.program_id(1)
        if ng:
            send, forward, finish = _gather_phases(refs[4:4 + ng], refs[5 + ng:5 + 2 * ng], *refs[7 + 2 * ng:])
            pl.when(step_id == 0)(send)
            pl.when(step_id == nsteps - 1)(forward)
        q_s[...] = _group_rms(x_ref[:, _SB_Q], qg_ref[...])
        k_s[...] = _group_rms(x_ref[:, _SB_K], kg_ref[...])

        def qblock(i, _):
            ri = pl.ds(pl.multiple_of(i * SBQ, SBQ), SBQ)
            q = q_s[ri, :]

            def kblock(jj, c):
                rj = pl.ds(pl.multiple_of((i - 1 - jj) * SBQ, SBQ), SBQ)
                pv, carry = _sb_block(q, k_s[rj, :], x_ref[rj, _SB_V], c[1], False)
                return c[0] + pv, carry

            on_diag = _sb_block(q, k_s[ri, :], x_ref[ri, _SB_V], jnp.zeros((2 * SBQ, 1), F32), True)
            acc, _c = lax.fori_loop(0, i, kblock, on_diag)
            out_ref[ri, :] = _sb_fold(acc)
            return 0

        lax.fori_loop(0, nblk, qblock, 0)
        if ng:
            pl.when(step_id == nsteps - 1)(finish)

    vec = pl.BlockSpec((1, LANES), lambda b, p: (0, 0))
    outs = pl.pallas_call(
        body, name="sb_fwd", grid=(B, 2),
        in_specs=[pl.BlockSpec((T, 3 * LANES), lambda b, p: (b, p)), vec, vec, pl.BlockSpec(memory_space=pl.ANY)] + _any_specs(ng),
        out_specs=[pl.BlockSpec((T, LANES), lambda b, p: (b, DN_W // LANES + p))] + _any_specs(ng),
        out_shape=[jax.ShapeDtypeStruct((B * T, D), F32)] + _gather_shapes(gather), input_output_aliases={3: 0},
        scratch_shapes=[pltpu.VMEM((T, LANES), F32)] * 2 + (_gather_sems(ng) if ng else []),
        compiler_params=_params(dimension_semantics=("arbitrary", "arbitrary")),
    )(proj_sb, q_gain, k_gain, mix, *gather)
    return outs[0], outs[1:]


def _sb_bwd_call(proj_sb, dmix, q_gain, k_gain, B, T):
    nblk = T // SBQ

    def body(x_ref, qg_ref, kg_ref, do_ref, dx_ref, dqg_ref, dkg_ref, q_s, k_s, dq_s, dk_s, dv_s, c_s):
        b_i, p = pl.program_id(0), pl.program_id(1)
        qn, q_vjp = jax.vjp(_group_rms, x_ref[:, _SB_Q], qg_ref[...])
        kn, k_vjp = jax.vjp(_group_rms, x_ref[:, _SB_K], kg_ref[...])
        q_s[...], k_s[...] = qn, kn
        dk_s[...] = jnp.zeros_like(dk_s)
        dv_s[...] = jnp.zeros_like(dv_s)

        def qblock(i, _):
            ri = pl.ds(pl.multiple_of(i * SBQ, SBQ), SBQ)
            q = q_s[ri, :]
            dacc = _sb_stack(do_ref[ri, :])

            def carries(jj, carry):
                j = i - 1 - jj
                rj = pl.ds(pl.multiple_of(j * SBQ, SBQ), SBQ)
                c_s[j] = carry
                return carry + _sb_rowsum(q, k_s[rj, :], False)

            lax.fori_loop(0, i, carries, _sb_rowsum(q, k_s[ri, :], True))

            def kblock(j, c):
                rj = pl.ds(pl.multiple_of(j * SBQ, SBQ), SBQ)
                dq_j, dk_j, dv_j, dc = _sb_block_bwd(q, k_s[rj, :], x_ref[rj, _SB_V], c_s[j], False, dacc, c[1])
                dk_s[rj, :] += dk_j
                dv_s[rj, :] += dv_j
                return c[0] + dq_j, dc

            dq, dc = lax.fori_loop(0, i, kblock, (jnp.zeros((SBQ, LANES), F32), jnp.zeros((2 * SBQ, 1), F32)))
            dq_i, dk_i, dv_i, _dc = _sb_block_bwd(q, k_s[ri, :], x_ref[ri, _SB_V], jnp.zeros((2 * SBQ, 1), F32), True, dacc, dc)
            dk_s[ri, :] += dk_i
            dv_s[ri, :] += dv_i
            dq_s[ri, :] = dq + dq_i
            return 0

        lax.fori_loop(0, nblk, qblock, 0)
        dq_in, dqg = q_vjp(dq_s[...])
        dk_in, dkg = k_vjp(dk_s[...])
        dx_ref[:, _SB_Q], dx_ref[:, _SB_K], dx_ref[:, _SB_V] = dq_in, dk_in, dv_s[...]

        @pl.when(jnp.logical_and(b_i == 0, p == 0))
        def _():
            dqg_ref[...] = jnp.zeros_like(dqg_ref)
            dkg_ref[...] = jnp.zeros_like(dkg_ref)

        dqg_ref[...] += dqg + pltpu.roll(dqg, 64, 1)
        dkg_ref[...] += dkg + pltpu.roll(dkg, 64, 1)

    M = B * T
    vec = pl.BlockSpec((1, LANES), lambda b, p: (0, 0))
    blk = pl.BlockSpec((T, 3 * LANES), lambda b, p: (b, p))
    big = [pltpu.VMEM((T, LANES), F32)]
    return pl.pallas_call(
        body, name="sb_bwd", grid=(B, 2),
        in_specs=[blk, vec, vec, pl.BlockSpec((T, LANES), lambda b, p: (b, DN_W // LANES + p))],
        out_specs=[blk, vec, vec],
        out_shape=[jax.ShapeDtypeStruct((M, 3 * SB_W), F32)] + [jax.ShapeDtypeStruct((1, LANES), F32)] * 2,
        scratch_shapes=big * 5 + [pltpu.VMEM((nblk, 2 * SBQ, 1), F32)],
        compiler_params=_params(dimension_semantics=("arbitrary", "arbitrary")),
    )(proj_sb, q_gain, k_gain, dmix)


def _sg_chunk(u, v, gain, w_a, w_b, bias):
    n = 128
    row, col = _iota((n, n), 0), _iota((n, n), 1)
    first = _iota((1, LANES), 1) < 64
    vn = _group_rms(_gelu(v), gain)
    tril = col <= row
    mixed = jnp.where(first, _dot(jnp.where(tril, w_a, 0.0), vn), _dot(jnp.where(tril, w_b, 0.0), vn)) + bias
    return _gelu(u) * mixed


_SG_U, _SG_V = slice(0, LANES), slice(LANES, 2 * LANES)


def _sg_fwd_call(proj_sg, mix, gain, sg_w, bias, B, T):
    nchunk = T // 128

    def body(x_ref, g_ref, wa_ref, wb_ref, bias_ref, mix_ref, out_ref):
        del mix_ref

        def step(i, _):
            r = pl.ds(pl.multiple_of(i * 128, 128), 128)
            out_ref[r, :] = _sg_chunk(x_ref[r, _SG_U], x_ref[r, _SG_V], g_ref[...], wa_ref[0], wb_ref[0], bias_ref[...])
            return 0

        lax.fori_loop(0, nchunk, step, 0)

    return pl.pallas_call(
        body, name="sg_fwd", grid=(B, 2),
        in_specs=[pl.BlockSpec((T, 2 * LANES), lambda b, p: (b, p)), pl.BlockSpec((1, LANES), lambda b, p: (0, p)),
                  pl.BlockSpec((1, 128, 128), lambda b, p: (2 * p, 0, 0)), pl.BlockSpec((1, 128, 128), lambda b, p: (2 * p + 1, 0, 0)),
                  pl.BlockSpec((128, LANES), lambda b, p: (0, p)), pl.BlockSpec(memory_space=pl.ANY)],
        out_specs=pl.BlockSpec((T, LANES), lambda b, p: (b, (DN_W + SB_W) // LANES + p)),
        out_shape=jax.ShapeDtypeStruct((B * T, D), F32), input_output_aliases={5: 0},
        compiler_params=_params(dimension_semantics=("arbitrary", "arbitrary")),
    )(proj_sg, gain, sg_w, sg_w, bias, mix)


def _sg_bwd_call(proj_sg, dmix, gain, sg_w, bias, B, T):
    nchunk = T // 128

    def body(x_ref, g_ref, wa_ref, wb_ref, bias_ref, do_ref, dx_ref, dg_ref, dw_ref, db_ref):
        p, b_i = pl.program_id(0), pl.program_id(1)

        def step(i, c):
            r = pl.ds(pl.multiple_of(i * 128, 128), 128)
            _, vjp = jax.vjp(_sg_chunk, x_ref[r, _SG_U], x_ref[r, _SG_V], g_ref[...], wa_ref[0], wb_ref[0], bias_ref[...])
            du, dv, dg, dwa, dwb, dbias = vjp(do_ref[r, :])
            dx_ref[r, _SG_U], dx_ref[r, _SG_V] = du, dv
            return c[0] + dg, c[1] + dwa, c[2] + dwb, c[3] + dbias

        z = jnp.zeros((128, 128), F32)
        dg, dwa, dwb, dbias = lax.fori_loop(0, nchunk, step, (jnp.zeros((1, LANES), F32), z, z, z))
        lane = _iota((1, LANES), 1)
        first = lane < 64
        s_a = jnp.sum(jnp.where(first, dbias, 0.0), axis=-1, keepdims=True)
        s_b = jnp.sum(jnp.where(first, 0.0, dbias), axis=-1, keepdims=True)
        dbg = jnp.where(lane == 2 * p, s_a, 0.0) + jnp.where(lane == 2 * p + 1, s_b, 0.0)

        @pl.when(b_i == 0)
        def _():
            dg_ref[...] = jnp.zeros_like(dg_ref)
            dw_ref[...] = jnp.zeros_like(dw_ref)

        @pl.when(jnp.logical_and(b_i == 0, p == 0))
        def _():
            db_ref[...] = jnp.zeros_like(db_ref)

        dg_ref[...] += dg
        dw_ref[0] += dwa
        dw_ref[1] += dwb
        db_ref[...] += dbg

    M = B * T
    blk = pl.BlockSpec((T, 2 * LANES), lambda p, b: (b, p))
    return pl.pallas_call(
        body, name="sg_bwd", grid=(2, B),
        in_specs=[blk, pl.BlockSpec((1, LANES), lambda p, b: (0, p)),
                  pl.BlockSpec((1, 128, 128), lambda p, b: (2 * p, 0, 0)), pl.BlockSpec((1, 128, 128), lambda p, b: (2 * p + 1, 0, 0)),
                  pl.BlockSpec((128, LANES), lambda p, b: (0, p)),
                  pl.BlockSpec((T, LANES), lambda p, b: (b, (DN_W + SB_W) // LANES + p))],
        out_specs=[blk, pl.BlockSpec((1, LANES), lambda p, b: (0, p)), pl.BlockSpec((2, 128, 128), lambda p, b: (p, 0, 0)),
                   pl.BlockSpec((128, LANES), lambda p, b: (0, 0))],
        out_shape=[jax.ShapeDtypeStruct((M, 2 * SG_W), F32), jax.ShapeDtypeStruct((1, SG_W), F32),
                   jax.ShapeDtypeStruct((4, 128, 128), F32), jax.ShapeDtypeStruct((128, LANES), F32)],
        compiler_params=_params(dimension_semantics=("arbitrary", "arbitrary")),
    )(proj_sg, gain, sg_w, sg_w, bias, dmix)


def _row_tile(m, most=512):
    return min(m, most)


def _in_proj_call(x, gain, wt):
    m = x.shape[0]
    tm = _row_tile(m)

    def body(x_ref, g_ref, wt_ref, *out_refs):
        h = _rms(x_ref[...], g_ref[...]).astype(BF16)
        out_refs[-1][...] = h
        for (off, width), out_ref in zip(SECTIONS, out_refs):
            out_ref[...] = lax.dot_general(h, wt_ref[off:off + width, :], (((1,), (1,)), ((), ())), preferred_element_type=F32)

    rows = lambda width: pl.BlockSpec((tm, width), lambda i: (i, 0))
    return pl.pallas_call(
        body, name="in_proj", grid=(m // tm,),
        in_specs=[rows(D), pl.BlockSpec((1, D), lambda i: (0, 0)),
                  pl.BlockSpec((NPACK, D), lambda i: (0, 0), pipeline_mode=pl.Buffered(1))],
        out_specs=[rows(w) for _, w in SECTIONS] + [rows(D)],
        out_shape=[jax.ShapeDtypeStruct((m, w), F32) for _, w in SECTIONS] + [jax.ShapeDtypeStruct((m, D), BF16)],
        compiler_params=_params(dimension_semantics=("arbitrary",)),
    )(x, gain, wt)


def _in_proj_bwd_call(dsections, wt, x, gain, dres):
    m = x.shape[0]
    tm = _row_tile(m)

    def body(*refs):
        ds_refs, (wt_ref, x_ref, g_ref, dres_ref, dx_ref, dg_ref) = refs[:len(SECTIONS)], refs[len(SECTIONS):]

        @pl.when(pl.program_id(0) == 0)
        def _():
            dg_ref[...] = jnp.zeros_like(dg_ref)

        dh = 0.0
        for (off, width), ds_ref in zip(SECTIONS, ds_refs):
            dh = dh + jnp.dot(ds_ref[...].astype(BF16), wt_ref[off:off + width, :], preferred_element_type=F32)
        _, vjp = jax.vjp(_rms, x_ref[...], g_ref[...])
        dx, dg = vjp(dh)
        dx_ref[...] = dres_ref[...] + dx
        dg_ref[...] += dg

    rows = lambda width: pl.BlockSpec((tm, width), lambda i: (i, 0))
    return pl.pallas_call(
        body, name="in_proj_bwd", grid=(m // tm,),
        in_specs=[rows(w) for _, w in SECTIONS] + [pl.BlockSpec((NPACK, D), lambda i: (0, 0), pipeline_mode=pl.Buffered(1)),
                                                   rows(D), pl.BlockSpec((1, D), lambda i: (0, 0)), rows(D)],
        out_specs=[rows(D), pl.BlockSpec((1, D), lambda i: (0, 0))],
        out_shape=[jax.ShapeDtypeStruct((m, D), F32), jax.ShapeDtypeStruct((1, D), F32)],
        compiler_params=_params(dimension_semantics=("arbitrary",)),
    )(*dsections, wt, x, gain, dres)


def _in_proj_grad_call(dsections, h):
    m = h.shape[0]
    tm = min(m, 256)

    def body(*refs):
        ds_refs, (h_ref, out_ref) = refs[:len(SECTIONS)], refs[len(SECTIONS):]

        @pl.when(pl.program_id(0) == 0)
        def _():
            out_ref[...] = jnp.zeros_like(out_ref)

        for (off, width), ds_ref in zip(SECTIONS, ds_refs):
            out_ref[off:off + width, :] += lax.dot_general(ds_ref[...].astype(BF16), h_ref[...], (((0,), (0,)), ((), ())),
                                                           preferred_element_type=F32)

    rows = lambda width: pl.BlockSpec((tm, width), lambda i: (i, 0))
    return pl.pallas_call(
        body, name="grad_w_in", grid=(m // tm,),
        in_specs=[rows(w) for _, w in SECTIONS] + [rows(D)],
        out_specs=pl.BlockSpec((NPACK, D), lambda i: (0, 0), pipeline_mode=pl.Buffered(1)),
        out_shape=jax.ShapeDtypeStruct((NPACK, D), F32),
        compiler_params=_params(dimension_semantics=("arbitrary",)),
    )(*dsections, h)


def _packed_column_of():
    t = np.full(NPACK, -1, np.int64)
    lanes = np.arange(LANES)
    for pair in range(2):
        for s in range(4):
            t[DN_OFF + pair * 1024 + s * 256 + np.arange(256)] = s * DN_W + pair * 256 + np.arange(256)
        for s in range(3):
            t[SB_OFF + pair * 384 + s * LANES + lanes] = 2056 + s * SB_W + pair * LANES + lanes
        for s in range(2):
            t[SG_OFF + pair * 256 + s * LANES + lanes] = 2056 + 3 * SB_W + s * SG_W + pair * LANES + lanes
    t[AB_OFF + np.arange(2 * NH)] = 4 * DN_W + np.arange(2 * NH)
    return t


def _row_tables():
    col = _packed_column_of()
    fwd = np.where(col >= 0, (col // IN_SHARD) * IN_SHARD_PAD + col % IN_SHARD, -1)
    packed_of = np.full(IN_DIM, -1, np.int64)
    packed_of[col[col >= 0]] = np.nonzero(col >= 0)[0]
    r = np.arange(NDEV * IN_SHARD_PAD)
    inside = r % IN_SHARD_PAD < IN_SHARD
    back = np.where(inside, packed_of[np.minimum((r // IN_SHARD_PAD) * IN_SHARD + r % IN_SHARD_PAD, IN_DIM - 1)], -1)
    return fwd, back


def _row_perm_call(src, table, name):
    n_out = table.shape[0]
    touched = [sorted(set((table[b * 128:(b + 1) * 128][table[b * 128:(b + 1) * 128] >= 0] // 128).tolist()))
               for b in range(n_out // 128)]

    def body(tbl_ref, src_ref, out_ref):
        lane = _iota((1, LANES), 1)
        for b, blocks in enumerate(touched):
            want = tbl_ref[b * 128:(b + 1) * 128, :]
            acc = jnp.zeros((128, D), F32)
            for sb in blocks:
                pick = jnp.where(want == sb * 128 + lane, 1.0, 0.0).astype(BF16)
                acc = acc + jnp.dot(pick, src_ref[sb * 128:(sb + 1) * 128, :].astype(BF16), preferred_element_type=F32)
            out_ref[b * 128:(b + 1) * 128, :] = acc.astype(BF16)

    return pl.pallas_call(
        body, name=name, out_shape=jax.ShapeDtypeStruct((n_out, D), BF16),
        in_specs=[pl.BlockSpec(memory_space=pltpu.VMEM)] * 2, out_specs=pl.BlockSpec(memory_space=pltpu.VMEM),
        compiler_params=_params(),
    )(jnp.asarray(table.reshape(-1, 1), jnp.int32), src)


def _out_proj_call(a, w, res):
    m, k = a.shape
    n = w.shape[1]
    tm = _row_tile(m)

    def body(a_ref, w_ref, res_ref, out_ref):
        out_ref[...] = res_ref[...] + jnp.dot(a_ref[...].astype(BF16), w_ref[...], preferred_element_type=F32)

    return pl.pallas_call(
        body, name="out_proj", grid=(m // tm,),
        in_specs=[pl.BlockSpec((tm, k), lambda i: (i, 0)), pl.BlockSpec((k, n), lambda i: (0, 0)),
                  pl.BlockSpec((tm, n), lambda i: (i, 0))],
        out_specs=pl.BlockSpec((tm, n), lambda i: (i, 0)),
        out_shape=jax.ShapeDtypeStruct((m, n), F32),
        compiler_params=_params(dimension_semantics=("arbitrary",)),
    )(a, w, res)


def _ffn_specs(tm):
    return [pl.BlockSpec((1, D, FF_SHARD), lambda i, j: (j, 0, 0)), pl.BlockSpec((FF_SHARD, D), lambda i, j: (j, 0))]


def _ffn_fwd_call(x, gain, w1, w2):
    m = x.shape[0]
    tm = _row_tile(m, 1024)

    def body(x_ref, g_ref, w1_ref, w2_ref, out_ref, h_s, acc_s):
        j = pl.program_id(1)

        @pl.when(j == 0)
        def _():
            h_s[...] = _rms(x_ref[...], g_ref[...]).astype(BF16)
            acc_s[...] = jnp.zeros_like(acc_s)

        a = jnp.maximum(jnp.dot(h_s[...], w1_ref[0], preferred_element_type=F32), 0.0)
        acc_s[...] += jnp.dot((a * a).astype(BF16), w2_ref[...], preferred_element_type=F32)

        @pl.when(j == NDEV - 1)
        def _():
            out_ref[...] = x_ref[...] + acc_s[...]

    return pl.pallas_call(
        body, name="ffn_fwd", grid=(m // tm, NDEV),
        in_specs=[pl.BlockSpec((tm, D), lambda i, j: (i, 0)), pl.BlockSpec((1, D), lambda i, j: (0, 0))] + _ffn_specs(tm),
        out_specs=pl.BlockSpec((tm, D), lambda i, j: (i, 0)),
        out_shape=jax.ShapeDtypeStruct((m, D), F32),
        scratch_shapes=[pltpu.VMEM((tm, D), BF16), pltpu.VMEM((tm, D), F32)],
        compiler_params=_params(dimension_semantics=("arbitrary", "arbitrary")),
    )(x, gain, w1, w2)


def _ffn_bwd_call(x, dy, gain, w1, w2, swap=()):
    m = x.shape[0]
    tm = _row_tile(m, 1024)
    ns = len(swap)

    def body(*refs):
        x_ref, dy_ref, g_ref, w1_ref, w2_ref = refs[:5]
        dx_ref, da_ref, r_ref, h_ref, dg_ref = refs[5 + ns:10 + ns]
        acc_s = refs[10 + 2 * ns]
        i, j = pl.program_id(0), pl.program_id(1)
        if ns:
            send, finish = _sibling_swap_phases(refs[5:5 + ns], refs[10 + ns:10 + 2 * ns], *refs[11 + 2 * ns:])
            pl.when(jnp.logical_and(i == 0, j == 0))(send)

        @pl.when(j == 0)
        def _():
            h_ref[...] = _rms(x_ref[...], g_ref[...]).astype(BF16)
            acc_s[...] = jnp.zeros_like(acc_s)

        @pl.when(jnp.logical_and(i == 0, j == 0))
        def _():
            dg_ref[...] = jnp.zeros_like(dg_ref)

        a = jnp.maximum(jnp.dot(h_ref[...], w1_ref[0], preferred_element_type=F32), 0.0)
        r_ref[...] = (a * a).astype(BF16)
        dr = lax.dot_general(dy_ref[...].astype(BF16), w2_ref[...], (((1,), (1,)), ((), ())), preferred_element_type=F32)
        da = (dr * (2.0 * a)).astype(BF16)
        da_ref[...] = da
        acc_s[...] += lax.dot_general(da, w1_ref[0], (((1,), (1,)), ((), ())), preferred_element_type=F32)

        @pl.when(j == NDEV - 1)
        def _():
            _, vjp = jax.vjp(_rms, x_ref[...], g_ref[...])
            dx, dg = vjp(acc_s[...])
            dx_ref[...] = dy_ref[...] + dx
            dg_ref[...] += dg

        if ns:
            pl.when(jnp.logical_and(i == m // tm - 1, j == NDEV - 1))(finish)

    outs = pl.pallas_call(
        body, name="ffn_bwd", grid=(m // tm, NDEV),
        in_specs=[pl.BlockSpec((tm, D), lambda i, j: (i, 0)), pl.BlockSpec((tm, D), lambda i, j: (i, 0)),
                  pl.BlockSpec((1, D), lambda i, j: (0, 0))] + _ffn_specs(tm) + _any_specs(ns),
        out_specs=[pl.BlockSpec((tm, D), lambda i, j: (i, 0)), pl.BlockSpec((tm, FF_SHARD), lambda i, j: (i, j)),
                   pl.BlockSpec((tm, FF_SHARD), lambda i, j: (i, j)), pl.BlockSpec((tm, D), lambda i, j: (i, 0)),
                   pl.BlockSpec((1, D), lambda i, j: (0, 0))] + _any_specs(ns),
        out_shape=[jax.ShapeDtypeStruct((m, D), F32), jax.ShapeDtypeStruct((m, DFF), BF16), jax.ShapeDtypeStruct((m, DFF), BF16),
                   jax.ShapeDtypeStruct((m, D), BF16), jax.ShapeDtypeStruct((1, D), F32)] + _sibling_swap_shapes(swap),
        scratch_shapes=[pltpu.VMEM((tm, D), F32)] + (_sibling_swap_sems(ns) if ns else []),
        compiler_params=_params(dimension_semantics=("arbitrary", "arbitrary")),
    )(x, dy, gain, w1, w2, *swap)
    return outs[:5], outs[5:]


def _mm_nt_call(a, b, name):
    m, k = a.shape
    n = b.shape[0]
    tm = _row_tile(m)

    def body(a_ref, b_ref, out_ref):
        out_ref[...] = lax.dot_general(a_ref[...].astype(BF16), b_ref[...].astype(BF16), (((1,), (1,)), ((), ())),
                                       preferred_element_type=F32)

    return pl.pallas_call(
        body, name=name, grid=(m // tm,),
        in_specs=[pl.BlockSpec((tm, k), lambda i: (i, 0)), pl.BlockSpec((n, k), lambda i: (0, 0))],
        out_specs=pl.BlockSpec((tm, n), lambda i: (i, 0)),
        out_shape=jax.ShapeDtypeStruct((m, n), F32),
        compiler_params=_params(dimension_semantics=("arbitrary",)),
    )(a, b)


def _mm_tn_call(a, b, name, col_shards=False):
    m, k = a.shape
    n = b.shape[1]
    tm, tk = _row_tile(m, 1024), min(k, 1024)
    tn = n // NDEV if col_shards else min(n, 1024)

    def body(a_ref, b_ref, out_ref, acc_s):
        s = pl.program_id(2)

        @pl.when(s == 0)
        def _():
            acc_s[...] = jnp.zeros_like(acc_s)

        acc_s[...] += lax.dot_general(a_ref[...].astype(BF16), b_ref[...].astype(BF16), (((0,), (0,)), ((), ())),
                                      preferred_element_type=F32)

        @pl.when(s == m // tm - 1)
        def _():
            out_ref[...] = acc_s[...].astype(BF16).reshape(out_ref.shape)

    if col_shards:
        out_spec, out_shape = pl.BlockSpec((1, tk, tn), lambda i, j, s: (j, i, 0)), (NDEV, k, tn)
    else:
        out_spec, out_shape = pl.BlockSpec((tk, tn), lambda i, j, s: (i, j)), (k, n)
    return pl.pallas_call(
        body, name=name, grid=(k // tk, n // tn, m // tm),
        in_specs=[pl.BlockSpec((tm, tk), lambda i, j, s: (s, i)), pl.BlockSpec((tm, tn), lambda i, j, s: (s, j))],
        out_specs=out_spec, out_shape=jax.ShapeDtypeStruct(out_shape, BF16),
        scratch_shapes=[pltpu.VMEM((tk, tn), F32)],
        compiler_params=_params(dimension_semantics=("arbitrary", "arbitrary", "arbitrary")),
    )(a, b)


def _loss_call(y, target):
    m = y.shape[0]
    tm = _row_tile(m)

    def body(y_ref, t_ref, loss_ref, dy_ref):
        @pl.when(pl.program_id(0) == 0)
        def _():
            loss_ref[...] = jnp.zeros_like(loss_ref)

        err = y_ref[...] - t_ref[...]
        dy_ref[...] = err * (1.0 / D)
        per_row = jnp.mean(err * err, axis=-1, keepdims=True)
        loss_ref[...] += jnp.broadcast_to(0.5 * jnp.sum(per_row, axis=0, keepdims=True), (1, LANES))

    return pl.pallas_call(
        body, name="loss", grid=(m // tm,),
        in_specs=[pl.BlockSpec((tm, D), lambda i: (i, 0))] * 2,
        out_specs=[pl.BlockSpec((1, LANES), lambda i: (0, 0)), pl.BlockSpec((tm, D), lambda i: (i, 0))],
        out_shape=[jax.ShapeDtypeStruct((1, LANES), F32), jax.ShapeDtypeStruct((m, D), F32)],
        compiler_params=_params(dimension_semantics=("arbitrary",)),
    )(y, target)


def _adamw_call(w, g, m, v, name):
    shape = w.shape
    cols = shape[-1] if w.ndim > 1 else w.size
    rows = w.size // cols
    tr = rows if (rows <= 512 or rows % 512) else 512
    c1, c2 = 1.0 - ADAM_B1 ** ADAM_STEP, 1.0 - ADAM_B2 ** ADAM_STEP

    def body(w_ref, g_ref, m_ref, v_ref, d_ref, nm_ref, nv_ref):
        g_ = g_ref[...]
        nm = ADAM_B1 * m_ref[...] + (1.0 - ADAM_B1) * g_
        nv = ADAM_B2 * v_ref[...] + (1.0 - ADAM_B2) * (g_ * g_)
        d_ref[...] = -ADAM_LR * ((nm / c1) / (jnp.sqrt(nv / c2) + ADAM_EPS) + ADAM_WD * w_ref[...])
        nm_ref[...], nv_ref[...] = nm, nv

    spec = pl.BlockSpec((tr, cols), lambda i: (i, 0))
    outs = pl.pallas_call(
        body, name=name, grid=(rows // tr,), in_specs=[spec] * 4, out_specs=[spec] * 3,
        out_shape=[jax.ShapeDtypeStruct((rows, cols), F32)] * 3,
        compiler_params=_params(dimension_semantics=("arbitrary",)),
    )(*(t.reshape(rows, cols) for t in (w, g, m, v)))
    return tuple(o.reshape(shape) for o in outs)


def _sum_tile(rows):
    for cand in (2048, 1024, 512, 256, 128):
        if rows > cand and rows % cand == 0:
            return cand
    return rows


def _pair_sum_call(g, got, core, name):
    rows, cols = g.shape[1:]
    tr = _sum_tile(rows)

    def body(core_ref, g_ref, got_ref, out_ref):
        del core_ref
        out_ref[...] = (g_ref[...].astype(F32) + got_ref[...].astype(F32)).astype(BF16)

    grid_spec = pltpu.PrefetchScalarGridSpec(
        num_scalar_prefetch=1, grid=(4, rows // tr),
        in_specs=[pl.BlockSpec((1, tr, cols), lambda ch, t, core_ref: (2 * ch + core_ref[0], t, 0)),
                  pl.BlockSpec((1, tr, cols), lambda ch, t, core_ref: (ch, t, 0))],
        out_specs=pl.BlockSpec((1, tr, cols), lambda ch, t, core_ref: (ch, t, 0)))
    return pl.pallas_call(
        body, name=name, grid_spec=grid_spec, out_shape=jax.ShapeDtypeStruct((4, rows, cols), BF16),
        compiler_params=_params(dimension_semantics=("arbitrary", "arbitrary")),
    )(jnp.asarray(core, jnp.int32).reshape(1), g, got)


def _sum_call(parts, out_dtype, name):
    rows, cols = parts[0][0].shape[1:]
    tr = _sum_tile(rows)
    index = jnp.stack([jnp.asarray(i, jnp.int32) for _, i in parts])

    def body(idx_ref, *refs):
        del idx_ref
        acc = refs[0][0].astype(F32)
        for r in refs[1:-1]:
            acc = acc + r[0].astype(F32)
        refs[-1][...] = acc.astype(out_dtype)

    grid_spec = pltpu.PrefetchScalarGridSpec(
        num_scalar_prefetch=1, grid=(rows // tr,),
        in_specs=[pl.BlockSpec((1, tr, cols), lambda t, idx, n=n: (idx[n], t, 0)) for n in range(len(parts))],
        out_specs=pl.BlockSpec((tr, cols), lambda t, idx: (t, 0)))
    return pl.pallas_call(
        body, name=name, grid_spec=grid_spec, out_shape=jax.ShapeDtypeStruct((rows, cols), out_dtype),
        compiler_params=_params(dimension_semantics=("arbitrary",)),
    )(index, *(a for a, _ in parts))


def _place():
    return lax.axis_index("x"), lax.axis_index("y"), lax.axis_index("c")


def _any_specs(n):
    return [pl.BlockSpec(memory_space=pl.ANY)] * n


def _all_gather_call(xs, name):
    n = len(xs)

    def body(*refs):
        for phase in _gather_phases(refs[:n], refs[n:2 * n], *refs[2 * n:]):
            phase()

    return pl.pallas_call(
        body, name=name, in_specs=_any_specs(n), out_specs=_any_specs(n),
        out_shape=_gather_shapes(xs), scratch_shapes=_gather_sems(n),
    )(*xs)


def _gather_shapes(xs):
    return [jax.ShapeDtypeStruct((NDEV,) + x.shape, x.dtype) for x in xs]


def _gather_sems(n):
    return [pltpu.SemaphoreType.DMA((7 * n,)), pltpu.SemaphoreType.DMA((7 * n,)), pltpu.SemaphoreType.DMA((n,))]


def _gather_phases(x_refs, out_refs, send_sems, recv_sems, local_sems):
    n = len(x_refs)
    ax, ay, ac = _place()
    me, sibling = (ax, ay, ac), (ax, ay, 1 - ac)
    chips = [(1 - ax, ay), (ax, 1 - ay), (1 - ax, 1 - ay)]

    def copy(a, k, block, to, src=None):
        slot = out_refs[a].at[4 * block[0] + 2 * block[1] + block[2]]
        return pltpu.make_async_remote_copy(
            src_ref=slot if src is None else src, dst_ref=slot,
            send_sem=send_sems.at[7 * a + k], recv_sem=recv_sems.at[7 * a + k], device_id=to, device_id_type=MESH)

    local = [pltpu.make_async_copy(x_refs[a], out_refs[a].at[4 * ax + 2 * ay + ac], local_sems.at[a]) for a in range(n)]
    first = []
    for a in range(n):
        first.append(copy(a, 0, me, sibling, src=x_refs[a]))
        first += [copy(a, 1 + j, me, (*chip, ac), src=x_refs[a]) for j, chip in enumerate(chips)]
    passed = [copy(a, 4 + j, (*chip, ac), sibling) for j, chip in enumerate(chips) for a in range(n)]

    def send():
        for cp in local + first:
            cp.start()

    def forward():
        for j, chip in enumerate(chips):
            for a in range(n):
                copy(a, 1 + j, (*chip, ac), me).wait_recv()
                passed[j * n + a].start()

    def finish():
        for a in range(n):
            copy(a, 0, sibling, me).wait_recv()
            for j, chip in enumerate(chips):
                copy(a, 4 + j, (*chip, 1 - ac), me).wait_recv()
        for cp in first + passed:
            cp.wait_send()
        for cp in local:
            cp.wait()

    return send, forward, finish


def _swap_sibling_call(xs, name):
    n = len(xs)

    def body(*refs):
        for phase in _sibling_swap_phases(refs[:n], refs[n:2 * n], *refs[2 * n:]):
            phase()

    return pl.pallas_call(
        body, name=name, in_specs=_any_specs(n), out_specs=_any_specs(n),
        out_shape=_sibling_swap_shapes(xs), scratch_shapes=_sibling_swap_sems(n),
    )(*xs)


def _sibling_swap_shapes(xs):
    return [jax.ShapeDtypeStruct((4,) + x.shape[1:], x.dtype) for x in xs]


def _sibling_swap_sems(n):
    return [pltpu.SemaphoreType.DMA((n,)), pltpu.SemaphoreType.DMA((n,))]


def _sibling_swap_phases(x_refs, out_refs, send_sems, recv_sems):
    ax, ay, ac = _place()
    sibling = (ax, ay, 1 - ac)

    def send():
        for a, (x_ref, out_ref) in enumerate(zip(x_refs, out_refs)):
            for chip in range(4):
                pltpu.make_async_remote_copy(src_ref=x_ref.at[2 * chip + 1 - ac], dst_ref=out_ref.at[chip],
                                             send_sem=send_sems.at[a], recv_sem=recv_sems.at[a],
                                             device_id=sibling, device_id_type=MESH).start()

    def finish():
        for a, (x_ref, out_ref) in enumerate(zip(x_refs, out_refs)):
            pltpu.make_async_remote_copy(src_ref=x_ref.at[pl.ds(0, 4)], dst_ref=out_ref, send_sem=send_sems.at[a],
                                         recv_sem=recv_sems.at[a], device_id=sibling, device_id_type=MESH).wait()

    return send, finish


def _swap_chips_call(xs, name):
    n = len(xs)

    def body(*refs):
        for phase in _chip_swap_phases(refs[:n], refs[n:2 * n], *refs[2 * n:]):
            phase()

    return pl.pallas_call(
        body, name=name, in_specs=_any_specs(n), out_specs=_any_specs(n),
        out_shape=_chip_swap_shapes(xs), scratch_shapes=_chip_swap_sems(n),
    )(*xs)


def _chip_swap_shapes(xs):
    return [jax.ShapeDtypeStruct((3,) + x.shape[1:], x.dtype) for x in xs]


def _chip_swap_sems(n):
    return [pltpu.SemaphoreType.DMA((3 * n,)), pltpu.SemaphoreType.DMA((3 * n,))]


def _chip_swap_phases(x_refs, out_refs, send_sems, recv_sems):
    ax, ay, ac = _place()
    chips = [(1 - ax, ay), (ax, 1 - ay), (1 - ax, 1 - ay)]
    copies = [pltpu.make_async_remote_copy(src_ref=x_refs[a].at[2 * cx + cy], dst_ref=out_refs[a].at[j],
                                           send_sem=send_sems.at[3 * a + j], recv_sem=recv_sems.at[3 * a + j],
                                           device_id=(cx, cy, ac), device_id_type=MESH)
              for a in range(len(x_refs)) for j, (cx, cy) in enumerate(chips)]

    def send():
        for cp in copies:
            cp.start()

    def finish():
        for cp in copies:
            cp.wait()

    return send, finish


def _reduce_begin(gs, name):
    ac = lax.axis_index("c")
    got = _swap_sibling_call(gs, name + "_d2d")
    return got, [_pair_sum_call(g, t, ac, f"{name}_pair{a}") for a, (g, t) in enumerate(zip(gs, got))]


def _reduce_end(gs, got, from_chips, name):
    ax, ay, ac = _place()
    me, my_chip = 4 * ax + 2 * ay + ac, 2 * ax + ay
    return [_sum_call([(g, me), (t, my_chip), (f, 0), (f, 1), (f, 2)], F32, f"{name}_total{a}")
            for a, (g, t, f) in enumerate(zip(gs, got, from_chips))]


SMALL = ("norm1_g", "conv_w", "a_log", "dt_bias", "dn_out_g", "sb_q_g", "sb_k_g", "sg_v_g", "sg_w", "sg_b", "norm2_g")
WEIGHTS = ("norm1_g", "w_in", "conv_w", "a_log", "dt_bias", "dn_out_g", "sb_q_g", "sb_k_g", "sg_v_g", "sg_w", "sg_b",
           "w_out", "norm2_g", "w_ff1", "w_ff2")
SMALL_SHAPE = {"norm1_g": (D,), "conv_w": (4, 3 * DN_W), "a_log": (NH,), "dt_bias": (NH,), "dn_out_g": (128,), "sb_q_g": (64,),
               "sb_k_g": (64,), "sg_v_g": (SG_W,), "sg_w": (NH, 128, 128), "sg_b": (NH, 128), "norm2_g": (D,)}


def _size(shape):
    n = 1
    for s in shape:
        n *= s
    return n


def _to_rows(flat, multiple):
    pad = (-flat.shape[0]) % (LANES * multiple)
    return jnp.pad(flat, (0, pad)).reshape(-1, LANES)


def _conv_by_pair(conv):
    return conv.reshape(4, 3, 2, 256).transpose(0, 2, 1, 3).reshape(4, 3 * DN_W)


def kernel(x, norm1_g, w_in, conv_w, a_log, dt_bias, dn_out_g, sb_q_g, sb_k_g, sg_v_g, sg_w, sg_b, w_out, norm2_g, w_ff1, w_ff2, loss_target, m_norm1_g, m_w_in, m_conv_w, m_a_log, m_dt_bias, m_dn_out_g, m_sb_q_g, m_sb_k_g, m_sg_v_g, m_sg_w, m_sg_b, m_w_out, m_norm2_g, m_w_ff1, m_w_ff2, v_norm1_g, v_w_in, v_conv_w, v_a_log, v_dt_bias, v_dn_out_g, v_sb_q_g, v_sb_k_g, v_sg_v_g, v_sg_w, v_sg_b, v_w_out, v_norm2_g, v_w_ff1, v_w_ff2):
    given = dict(norm1_g=norm1_g, w_in=w_in, conv_w=conv_w, a_log=a_log, dt_bias=dt_bias, dn_out_g=dn_out_g, sb_q_g=sb_q_g,
                 sb_k_g=sb_k_g, sg_v_g=sg_v_g, sg_w=sg_w, sg_b=sg_b, w_out=w_out, norm2_g=norm2_g, w_ff1=w_ff1, w_ff2=w_ff2)
    mom = dict(norm1_g=m_norm1_g, w_in=m_w_in, conv_w=m_conv_w, a_log=m_a_log, dt_bias=m_dt_bias, dn_out_g=m_dn_out_g,
               sb_q_g=m_sb_q_g, sb_k_g=m_sb_k_g, sg_v_g=m_sg_v_g, sg_w=m_sg_w, sg_b=m_sg_b, w_out=m_w_out, norm2_g=m_norm2_g,
               w_ff1=m_w_ff1, w_ff2=m_w_ff2)
    var = dict(norm1_g=v_norm1_g, w_in=v_w_in, conv_w=v_conv_w, a_log=v_a_log, dt_bias=v_dt_bias, dn_out_g=v_dn_out_g,
               sb_q_g=v_sb_q_g, sb_k_g=v_sb_k_g, sg_v_g=v_sg_v_g, sg_w=v_sg_w, sg_b=v_sg_b, w_out=v_w_out, norm2_g=v_norm2_g,
               w_ff1=v_w_ff1, w_ff2=v_w_ff2)
    B, T, _ = x.shape
    M = B * T
    ax, ay, ac = _place()
    me = 4 * ax + 2 * ay + ac
    table_fwd, table_back = _row_tables()

    send = []
    for l in range(2):
        w_in_t = jnp.pad(w_in[l].T, ((0, IN_SHARD_PAD - IN_SHARD), (0, 0)))
        send.append([w_in_t.astype(BF16), w_out[l].astype(BF16), w_ff1[l].astype(BF16), w_ff2[l].astype(BF16)])
    first_in, conv_rows = _all_gather_call([send[0][0], _to_rows(conv_w.reshape(-1), 8)], "gather_first")
    conv_full = conv_rows.reshape(NDEV, -1)[:, :conv_w.size].reshape(NDEV, 2, 4, -1).transpose(1, 2, 0, 3).reshape(2, 4, 3 * DN_W)
    gathered = [[first_in, None, None, None], [None] * 4]

    pad_vec = lambda v: jnp.zeros((1, LANES), F32).at[0, :v.shape[0]].set(v)
    layer = []
    for l in range(2):
        layer.append(dict(
            g1=norm1_g[l].reshape(1, D), g2=norm2_g[l].reshape(1, D), conv=_conv_by_pair(conv_full[l]),
            a_log=pad_vec(a_log[l]), dt_bias=pad_vec(dt_bias[l]), dn_g=dn_out_g[l].reshape(1, LANES),
            sb_qg=jnp.tile(sb_q_g[l], 2).reshape(1, LANES), sb_kg=jnp.tile(sb_k_g[l], 2).reshape(1, LANES),
            sg_g=sg_v_g[l].reshape(1, SG_W), sg_w=sg_w[l], sg_bias=jnp.repeat(sg_b[l].T, 64, axis=1)))

    cur = x.reshape(M, D)
    saved = []
    for l, p in enumerate(layer):
        p["wt"] = _row_perm_call(gathered[l][0].reshape(NDEV * IN_SHARD_PAD, D), table_fwd, "pack_w_in")
        p_dn, p_sb, p_sg, p_ab, h = _in_proj_call(cur, p["g1"], p["wt"])
        mix, dn_kept, arrived = _dn_fwd_call(p_dn, p_ab, p["conv"], p["a_log"], p["dt_bias"], p["dn_g"], B, T,
                                             gather=send[0][1:] + send[1][:1] if l == 0 else [])
        if l == 0:
            gathered[0][1:], gathered[1][0] = list(arrived[:3]), arrived[3]
        p["w_out"], p["w1"], p["w2"] = gathered[l][1].reshape(D, D), gathered[l][2], gathered[l][3].reshape(DFF, D)
        mix, arrived = _sb_fwd_call(p_sb, mix, p["sb_qg"], p["sb_kg"], B, T, gather=send[1][1:] if l == 0 else [])
        if l == 0:
            gathered[1][1:] = list(arrived)
        mix = _sg_fwd_call(p_sg, mix, p["sg_g"], p["sg_w"], p["sg_bias"], B, T)
        x1 = _out_proj_call(mix, p["w_out"], cur)
        x2 = _ffn_fwd_call(x1, p["g2"], p["w1"], p["w2"])
        saved.append(dict(x0=cur, p_dn=p_dn, p_sb=p_sb, p_sg=p_sg, p_ab=p_ab, h=h, mix=mix, x1=x1, dn_kept=dn_kept))
        cur = x2
    loss_part, dy = _loss_call(cur, loss_target.reshape(M, D))
    loss = lax.psum(loss_part[0, 0], ("x", "y", "c"))

    big_grads = [[None] * 4, [None] * 4]
    small_grads = {n: [None, None] for n in SMALL}
    for l in (1, 0):
        p, s = layer[l], saved[l]
        (dx1, da, r, h2, dg2), got1 = _ffn_bwd_call(s["x1"], dy, p["g2"], p["w1"], p["w2"], swap=big_grads[1] if l == 0 else ())
        big_grads[l][2] = _mm_tn_call(h2, da, "grad_w_ff1", col_shards=True)
        big_grads[l][3] = _mm_tn_call(r, dy, "grad_w_ff2").reshape(NDEV, FF_SHARD, D)
        dmix = _mm_nt_call(dx1, p["w_out"], "dmix")
        big_grads[l][1] = _mm_tn_call(s["mix"], dx1, "grad_w_out").reshape(NDEV, D // NDEV, D)
        if l == 0:
            got0, sums0 = _reduce_begin(big_grads[0][1:], "reduce_early0")
            early, early_got = big_grads[1] + big_grads[0][1:], list(got1) + list(got0)
            early_sums = [_pair_sum_call(g, t, ac, f"reduce_early1_pair{a}") for a, (g, t) in enumerate(zip(big_grads[1], got1))] + sums0
        (d_dn, d_ab, dcw, dalog, ddtb, ddn_g), early_from = _dn_bwd_call(
            s["p_dn"], s["p_ab"], dmix, s["dn_kept"], p["conv"], p["a_log"], p["dt_bias"], p["dn_g"], B, T,
            swap=early_sums if l == 0 else ())
        d_sb, dqg, dkg = _sb_bwd_call(s["p_sb"], dmix, p["sb_qg"], p["sb_kg"], B, T)
        d_sg, dsg_g, dsg_w, dsg_b = _sg_bwd_call(s["p_sg"], dmix, p["sg_g"], p["sg_w"], p["sg_bias"], B, T)
        dsections = (d_dn, d_sb, d_sg, d_ab)
        dy, dg1 = _in_proj_bwd_call(dsections, p["wt"], s["x0"], p["g1"], dx1)
        dwt = _in_proj_grad_call(dsections, s["h"])
        big_grads[l][0] = _row_perm_call(dwt, table_back, "unpack_grad_w_in").reshape(NDEV, IN_SHARD_PAD, D)
        for n, val in (("norm1_g", dg1[0]), ("conv_w", dcw.transpose(1, 0, 2).reshape(4, 3 * DN_W)), ("a_log", dalog[0, :NH]),
                       ("dt_bias", ddtb[0, :NH]), ("dn_out_g", ddn_g[0]), ("sb_q_g", dqg[0, :64]), ("sb_k_g", dkg[0, :64]),
                       ("sg_v_g", dsg_g[0]), ("sg_w", dsg_w), ("sg_b", dsg_b[:, :NH].T), ("norm2_g", dg2[0])):
            small_grads[n][l] = val
    grad_x = dy.reshape(B, T, D)

    last = big_grads[0][:1]
    last_got, last_sums = _reduce_begin(last, "reduce_last")
    mine0 = _reduce_end(last, last_got, _swap_chips_call(last_sums, "reduce_last_ici"), "reduce_last")
    mine1 = _reduce_end(early, early_got, early_from, "reduce_early")
    grads = {"w_in": jnp.stack([mine0[0][:IN_SHARD].T, mine1[0][:IN_SHARD].T]), "w_out": jnp.stack([mine1[4], mine1[1]]),
             "w_ff1": jnp.stack([mine1[5], mine1[2]]), "w_ff2": jnp.stack([mine1[6], mine1[3]])}
    small_flat = jnp.concatenate([jnp.stack(small_grads[n]).reshape(-1) for n in SMALL])
    everyone, = _all_gather_call([_to_rows(small_flat, 8)], "gather_small_grads")
    small_sum = _sum_call([(everyone, k) for k in range(NDEV)], F32, "sum_small_grads").reshape(-1)
    off = 0
    for n in SMALL:
        sz = 2 * _size(SMALL_SHAPE[n])
        grads[n] = small_sum[off:off + sz].reshape((2,) + SMALL_SHAPE[n])
        off += sz
    cshard = conv_w.shape[-1]
    grads["conv_w"] = lax.dynamic_slice_in_dim(grads["conv_w"], me * cshard, cshard, axis=2)

    deltas, new_m, new_v = {}, {}, {}
    for n in WEIGHTS:
        deltas[n], new_m[n], new_v[n] = _adamw_call(given[n], grads[n], mom[n], var[n], "adamw_" + n)
    return (loss, grad_x, *[grads[n] for n in WEIGHTS], *[deltas[n] for n in WEIGHTS], *[new_m[n] for n in WEIGHTS],
            *[new_v[n] for n in WEIGHTS])
```

```python
import functools

import numpy as np

import jax
import jax.numpy as jnp
from jax import lax
from jax.experimental import pallas as pl
from jax.experimental.pallas import tpu as pltpu

F32, BF16 = jnp.float32, jnp.bfloat16
EPS = 1e-6
LANES = 128
D = 1024
DFF = 4096
NH = 4
DN_W, SB_W, SG_W = 512, 256, 256
IN_DIM = 3336
NDEV = 8
IN_SHARD = IN_DIM // NDEV
IN_SHARD_PAD = 432
FF_SHARD = DFF // NDEV
DN_OFF, SB_OFF, SG_OFF, AB_OFF, NPACK = 0, 2048, 2816, 3328, 3456
SECTIONS = ((DN_OFF, 2048), (SB_OFF, 768), (SG_OFF, 512), (AB_OFF, 128))
SB_SCALE = 64 ** -0.5
DN_SCALE = 128 ** -0.5
VMEM_LIMIT = 56 * 1024 * 1024
VMEM_LIMIT_MAX = 62 * 1024 * 1024
ADAM_LR, ADAM_B1, ADAM_B2, ADAM_EPS, ADAM_WD, ADAM_STEP = 0.001, 0.9, 0.999, 1e-08, 0.01, 10
MESH = pl.DeviceIdType.MESH


def _iota(shape, dim):
    return lax.broadcasted_iota(jnp.int32, shape, dim)


def _params(**kw):
    return pltpu.CompilerParams(vmem_limit_bytes=VMEM_LIMIT, **kw)


NN, NT, TN = ((1,), (0,)), ((1,), (1,)), ((0,), (0,))


def _mm(a, b, dims):
    return lax.dot_general(a.astype(BF16), b.astype(BF16), (dims, ((), ())), preferred_element_type=F32)


def _plain(a, b, dims):
    return (a.T if dims == TN else a), (b.T if dims == NT else b)


def _mmx(a, b, dims):
    return _mm(*_plain(a, b, dims), NN)


@jax.custom_vjp
def _dot(a, b):
    return _mmx(a, b, NN)


def _dot_fwd(a, b):
    return _dot(a, b), (a, b)


def _dot_bwd(res, g):
    a, b = res
    return _mmx(g, b, NT).astype(a.dtype), _mmx(a, g, TN).astype(b.dtype)


_dot.defvjp(_dot_fwd, _dot_bwd)


@jax.custom_vjp
def _dot_nt(a, b):
    return _mmx(a, b, NT)


def _dot_nt_fwd(a, b):
    return _dot_nt(a, b), (a, b)


def _dot_nt_bwd(res, g):
    a, b = res
    return _mmx(g, b, NN).astype(a.dtype), _mmx(g, a, TN).astype(b.dtype)


_dot_nt.defvjp(_dot_nt_fwd, _dot_nt_bwd)


@jax.custom_vjp
def _dot_tn(a, b):
    return _mmx(a, b, TN)


def _dot_tn_fwd(a, b):
    return _dot_tn(a, b), (a, b)


def _dot_tn_bwd(res, g):
    a, b = res
    return _mmx(b, g, NT).astype(a.dtype), _mmx(a, g, NN).astype(b.dtype)


_dot_tn.defvjp(_dot_tn_fwd, _dot_tn_bwd)


def _split(x):
    hi = x.astype(BF16)
    return hi, (x - hi.astype(F32)).astype(BF16)


def _mm3(a, b, dims):
    a, b = _plain(a, b, dims)
    (ah, al), (bh, bl) = _split(a), _split(b)
    mm = lambda x, y: jnp.dot(x, y, preferred_element_type=F32)
    return mm(ah, bh) + (mm(ah, bl) + mm(al, bh))


def _mm_ones(ones, x, ones_left):
    hi, lo = _split(x)
    mm = (lambda t: jnp.dot(ones, t, preferred_element_type=F32)) if ones_left else \
         (lambda t: jnp.dot(t, ones, preferred_element_type=F32))
    return mm(hi) + mm(lo)


def _pair_ones(kind, transposed):
    row, col = _iota((128, 128), 0), _iota((128, 128), 1)
    m = (row // 64) == (col // 64)
    if kind == "running":
        m = jnp.logical_and(m, (col >= row) if transposed else (col <= row))
    return jnp.where(m, 1.0, 0.0).astype(BF16)


@functools.partial(jax.custom_vjp, nondiff_argnums=(0,))
def _chunk_sum(kind, x):
    return _mm_ones(_pair_ones(kind, False), x, True)


def _chunk_sum_fwd(kind, x):
    return _chunk_sum(kind, x), None


def _chunk_sum_bwd(kind, _, g):
    return (_mm_ones(_pair_ones(kind, True), g, True),)


_chunk_sum.defvjp(_chunk_sum_fwd, _chunk_sum_bwd)


def _tri_ones(n, transposed):
    row, col = _iota((n, n), 0), _iota((n, n), 1)
    return jnp.where((row < col) if transposed else (row > col), 1.0, 0.0).astype(BF16)


@jax.custom_vjp
def _suffix_sum(x):
    return _mm_ones(_tri_ones(x.shape[1], False), x, False)


def _suffix_sum_fwd(x):
    return _suffix_sum(x), None


def _suffix_sum_bwd(_, g):
    return (_mm_ones(_tri_ones(g.shape[1], True), g, False),)


_suffix_sum.defvjp(_suffix_sum_fwd, _suffix_sum_bwd)


def _sigmoid(x):
    return jax.nn.sigmoid(x)


def _silu(x):
    return x * _sigmoid(x)


def _softplus(x):
    return jnp.maximum(x, 0.0) + jnp.log1p(jnp.exp(-jnp.abs(x)))


def _gelu(x):
    return 0.5 * x * (1.0 + jnp.tanh(0.7978845608028654 * (x + 0.044715 * (x * x * x))))


def _rms(x, gain):
    return x * lax.rsqrt(jnp.mean(x * x, axis=-1, keepdims=True) + EPS) * gain


def _shift_down_impl(x, k):
    return jnp.where(_iota(x.shape, 0) >= k, pltpu.roll(x, k, 0), 0.0)


def _shift_up_impl(x, k):
    n = x.shape[0]
    return jnp.where(_iota(x.shape, 0) < n - k, pltpu.roll(x, n - k, 0), 0.0)


@functools.partial(jax.custom_vjp, nondiff_argnums=(1,))
def _shift_down(x, k):
    return _shift_down_impl(x, k)


def _shift_down_fwd(x, k):
    return _shift_down_impl(x, k), None


def _shift_down_bwd(k, _, g):
    return (_shift_up_impl(g, k),)


_shift_down.defvjp(_shift_down_fwd, _shift_down_bwd)


def _lane_pick(x, idx):
    return jnp.sum(jnp.where(_iota(x.shape, 1) == idx, x, 0.0), axis=-1, keepdims=True)


def _dn_conv(x, w0, w1, w2, w3, l2_scale):
    y = _silu(w3 * x + w2 * _shift_down(x, 1) + w1 * _shift_down(x, 2) + w0 * _shift_down(x, 3))
    if l2_scale is None:
        return y
    return y * lax.rsqrt(jnp.sum(y * y, axis=-1, keepdims=True) + EPS) * l2_scale


def _dn_gate(a, b, a_log, dt_bias):
    return -jnp.exp(a_log) * _softplus(a + dt_bias), _sigmoid(b)


def _same_head(shape):
    return (_iota(shape, 0) < LANES) == (_iota(shape, 1) < LANES)


def _bd(r2):
    return jnp.where(_same_head((2 * LANES, 2 * LANES)), jnp.concatenate([r2, r2], axis=0), 0.0)


def _bd_t(y2):
    t = y2.T
    return jnp.where(_same_head((2 * LANES, 2 * LANES)), jnp.concatenate([t, t], axis=1), 0.0)


def _pair_prod(kind, a2, b2, mm):
    if kind == NN:
        return mm(a2, _bd(b2))
    if kind == NT:
        return mm(a2, _bd_t(b2))
    full = mm(a2.T, b2)
    return jnp.concatenate([full[:LANES, :LANES], full[LANES:, LANES:]], axis=1)


_MM1 = lambda x, y: _mm(x, y, NN)
_MM3 = lambda x, y: _mm3(x, y, NN)


def _pair_vjp_rule(kind, a2, b2, g, mm):
    if kind == NN:
        return _pair_prod(NT, g, b2, mm), _pair_prod(TN, a2, g, mm)
    if kind == NT:
        return _pair_prod(NN, g, b2, mm), _pair_prod(TN, g, a2, mm)
    return _pair_prod(NT, b2, g, mm), _pair_prod(NN, a2, g, mm)


@functools.partial(jax.custom_vjp, nondiff_argnums=(0,))
def _pdot(kind, a2, b2):
    return _pair_prod(kind, a2, b2, _MM1)


def _pdot_fwd(kind, a2, b2):
    return _pdot(kind, a2, b2), (a2, b2)


def _pdot_bwd(kind, res, g):
    return _pair_vjp_rule(kind, *res, g, _MM1)


_pdot.defvjp(_pdot_fwd, _pdot_bwd)


def _unit_lower_inverse(lower):
    n = lower.shape[0]
    nk = -lower
    inv = jnp.where(_iota(lower.shape, 0) == jnp.bitwise_and(_iota(lower.shape, 1), n - 1), 1.0, 0.0) + nk
    for _ in range(5):
        nk = _pair_prod(NN, nk, nk, _MM1)
        inv = inv + _pair_prod(NN, inv, nk, _MM1)
    return inv


@jax.custom_vjp
def _solve_with(lower, inv, rhs):
    return _pair_prod(NN, inv, rhs, _MM3)


def _solve_with_fwd(lower, inv, rhs):
    x = _pair_prod(NN, inv, rhs, _MM3)
    return x, (inv, x)


def _solve_with_bwd(res, g):
    inv, x = res
    d_rhs = _pair_prod(TN, inv, g, _MM3)
    return -_pair_prod(NT, d_rhs, x, _MM3), jnp.zeros_like(inv), d_rhs


_solve_with.defvjp(_solve_with_fwd, _solve_with_bwd)


def _dn_local(q, k, v, g, beta, inv=None):
    shape = (LANES, 2 * LANES)
    row, col = _iota(shape, 0), jnp.bitwise_and(_iota(shape, 1), LANES - 1)
    same = (row // 64) == (col // 64)
    tri_incl = jnp.logical_and(same, col <= row)
    tri_strict = jnp.logical_and(same, col < row)
    first = row < 64
    gc = _chunk_sum("running", g)
    gl = _chunk_sum("total", g)
    diff = gc - jnp.concatenate([gc[:, :LANES].T, gc[:, LANES:].T], axis=1)
    decay = jnp.where(tri_incl, jnp.exp(jnp.where(tri_incl, diff, 0.0)), 0.0)
    egc = jnp.exp(gc)
    lower = jnp.where(tri_strict, beta * _pdot(NT, k, k) * decay, 0.0)
    if inv is None:
        inv = _unit_lower_inverse(lower)
    u_val = _solve_with(lower, inv, v * beta)
    w_dec = _solve_with(lower, inv, k * (beta * egc))
    qk = jnp.where(tri_incl, _pdot(NT, q, k) * decay, 0.0)
    q_dec = q * egc
    k_dec = k * jnp.exp(gl - gc)
    cd1 = jnp.exp(jnp.sum(jnp.where(first, g, 0.0), axis=0, keepdims=True))
    cd2 = jnp.exp(jnp.sum(jnp.where(first, 0.0, g), axis=0, keepdims=True))
    return (u_val, w_dec, qk, q_dec, k_dec, cd1, cd2), inv


def _dn_state(u_val, w_dec, qk, q_dec, k_dec, cd1, cd2, s0):
    first = _iota((LANES, 2 * LANES), 0) < 64
    u1 = u_val - _pdot(NN, w_dec, s0)
    s1 = s0 * cd1 + _pdot(TN, jnp.where(first, k_dec, 0.0), u1)
    u2 = u_val - _pdot(NN, w_dec, s1)
    u_new = jnp.where(first, u1, u2)
    s2 = s1 * cd2 + _pdot(TN, jnp.where(first, 0.0, k_dec), u_new)
    o = jnp.where(first, _pdot(NN, q_dec, s0), _pdot(NN, q_dec, s1)) + _pdot(NN, qk, u_new)
    return o, s2


def _dn_post(o, z, gain):
    return _rms(o, gain) * _silu(z)


def _dn_gate_in(ab_ref, alog_ref, dtb_ref, h):
    ab = ab_ref[...]
    return _lane_pick(ab, h), _lane_pick(ab, h + NH), _lane_pick(alog_ref[...], h), _lane_pick(dtb_ref[...], h)


_DN_L2 = (DN_SCALE, 1.0, None)
DN_HPS = 2
DN_BLK = 4 * DN_HPS * LANES
_DN_COLS = tuple(slice(i * LANES, (i + 1) * LANES) for i in range(DN_HPS))


def _dn_in_cols(s, i):
    return slice((s * DN_HPS + i) * LANES, (s * DN_HPS + i + 1) * LANES)


def _dn_taps(cw_ref, s, i):
    return tuple(cw_ref[t:t + 1, _dn_in_cols(s, i)] for t in range(4))


def _dn_pack_gate(vals):
    lane = _iota((1, LANES), 1)
    out = 0.0
    for i, (g, beta) in enumerate(vals):
        out = out + jnp.where(lane == 2 * i, g, 0.0) + jnp.where(lane == 2 * i + 1, beta, 0.0)
    return out


def _pair_rows(n):
    return pl.ds(pl.multiple_of(n * 128, 128), 128)


def _dn_gate_rows(gate):
    head_a = _iota((1, DN_HPS * LANES), 1) < LANES
    return (jnp.where(head_a, _lane_pick(gate, 0), _lane_pick(gate, 2)),
            jnp.where(head_a, _lane_pick(gate, 1), _lane_pick(gate, 3)))


def _dn_gate_cols(dg, db):
    head_a = _iota((1, DN_HPS * LANES), 1) < LANES
    fold = lambda t: (jnp.sum(jnp.where(head_a, t, 0.0), axis=-1, keepdims=True),
                      jnp.sum(jnp.where(head_a, 0.0, t), axis=-1, keepdims=True))
    (dg_a, dg_b), (db_a, db_b) = fold(dg), fold(db)
    return [(dg_a, db_a), (dg_b, db_b)]


def _dn_in_specs(T):
    one = pl.Buffered(1)
    vec = pl.BlockSpec((1, LANES), lambda b, h: (0, 0))
    return [pl.BlockSpec((T, DN_BLK), lambda b, h: (b, h), pipeline_mode=one),
            pl.BlockSpec((T, LANES), lambda b, h: (b, 0), pipeline_mode=one),
            pl.BlockSpec((4, 3 * DN_HPS * LANES), lambda b, h: (0, h)), vec, vec, vec]


def _dn_fwd_call(proj_dn, proj_ab, conv_w, a_log, dt_bias, gain, B, T, gather=()):
    npair = T // 128
    ng = len(gather)
    nsteps = B * (NH // DN_HPS)

    def body(*refs):
        x_ref, ab_ref, cw_ref, alog_ref, dtb_ref, gain_ref = refs[:6]
        out_ref, q_s, k_s, v_s, o_s, gate_s, st_s, inv_s = refs[6 + ng:14 + ng]
        step_id = pl.program_id(0) * (NH // DN_HPS) + pl.program_id(1)
        if ng:
            send, forward, finish = _gather_phases(refs[6:6 + ng], refs[14 + ng:14 + 2 * ng], *refs[14 + 2 * ng:])
            pl.when(step_id == 0)(send)
            pl.when(step_id == nsteps - 1)(forward)
        hp = pl.program_id(1)
        gates = []
        for i, cs in enumerate(_DN_COLS):
            for s, (x_s, l2) in enumerate(zip((q_s, k_s, v_s), _DN_L2)):
                x_s[:, cs] = _dn_conv(x_ref[:, _dn_in_cols(s, i)], *_dn_taps(cw_ref, s, i), l2)
            gates.append(_dn_gate(*_dn_gate_in(ab_ref, alog_ref, dtb_ref, DN_HPS * hp + i)))
        gate_s[...] = _dn_pack_gate(gates)

        def local_of(pair):
            r = _pair_rows(pair)
            loc, inv = _dn_local(q_s[r, :], k_s[r, :], v_s[r, :], *_dn_gate_rows(gate_s[r, :]))
            inv_s[0, 0, pair] = inv
            return loc

        def state_of(n, loc, state):
            st_s[0, 0, n] = state
            o, s2 = _dn_state(*loc, state)
            o_s[_pair_rows(n), :] = o
            return s2

        def step(n, carry):
            loc, state = carry
            return local_of(n + 1), state_of(n, loc, state)

        loc, state = lax.fori_loop(0, npair - 1, step, (local_of(0), jnp.zeros((LANES, DN_HPS * LANES), F32)))
        state_of(npair - 1, loc, state)
        for i, cs in enumerate(_DN_COLS):
            out_ref[:, cs] = _dn_post(o_s[:, cs], x_ref[:, _dn_in_cols(3, i)], gain_ref[...])
        if ng:
            pl.when(step_id == nsteps - 1)(finish)

    kept_specs, kept_shapes = _dn_kept(B, T)
    outs = pl.pallas_call(
        body, name="dn_fwd", grid=(B, NH // DN_HPS), in_specs=_dn_in_specs(T) + _any_specs(ng),
        out_specs=[pl.BlockSpec((T, DN_HPS * LANES), lambda b, h: (b, h), pipeline_mode=pl.Buffered(1))] + kept_specs + _any_specs(ng),
        out_shape=[jax.ShapeDtypeStruct((B * T, D), F32)] + kept_shapes + _gather_shapes(gather),
        scratch_shapes=_gather_sems(ng) if ng else [],
        compiler_params=_params(dimension_semantics=("arbitrary", "arbitrary")),
    )(proj_dn, proj_ab, conv_w, a_log, dt_bias, gain, *gather)
    return outs[0], outs[1:8], outs[8:]


def _dn_kept(B, T):
    one = pl.Buffered(1)
    npair, pairs = T // 128, NH // DN_HPS
    wide = pl.BlockSpec((T, DN_HPS * LANES), lambda b, h: (b, h), pipeline_mode=one)
    per_pair = pl.BlockSpec((1, 1, npair, LANES, DN_HPS * LANES), lambda b, h: (b, h, 0, 0, 0), pipeline_mode=one)
    specs = [wide] * 4 + [pl.BlockSpec((T, LANES), lambda b, h: (b, h), pipeline_mode=one)] + [per_pair] * 2
    shapes = ([jax.ShapeDtypeStruct((B * T, DN_W), F32)] * 4 + [jax.ShapeDtypeStruct((B * T, pairs * LANES), F32)]
              + [jax.ShapeDtypeStruct((B, pairs, npair, LANES, DN_HPS * LANES), F32)] * 2)
    return specs, shapes


def _dn_bwd_call(proj_dn, proj_ab, dmix, kept, conv_w, a_log, dt_bias, gain, B, T, swap=()):
    npair = T // 128
    ns = len(swap)
    nsteps = B * (NH // DN_HPS)

    def body(*refs):
        x_ref, ab_ref, cw_ref, alog_ref, dtb_ref, gain_ref, do_ref, q_s, k_s, v_s, o_ref, gate_s, st_s, inv_s = refs[:14]
        dx_ref, dab_ref, dcw_ref, dalog_ref, ddtb_ref, dgain_ref = refs[14 + ns:20 + ns]
        dgate_s, dcd_s = refs[20 + 2 * ns:22 + 2 * ns]
        dloc_s = refs[22 + 2 * ns:27 + 2 * ns]
        do_s = dloc_s[0]
        b_i, hp = pl.program_id(0), pl.program_id(1)
        step_id = b_i * (NH // DN_HPS) + hp
        if ns:
            send, finish = _chip_swap_phases(refs[14:14 + ns], refs[20 + ns:20 + 2 * ns], *refs[27 + 2 * ns:])
            pl.when(step_id == 0)(send)

        def pair_in(r):
            return (q_s[r, :], k_s[r, :], v_s[r, :]) + _dn_gate_rows(gate_s[r, :])

        def local_of(pair):
            return _dn_local(*pair_in(_pair_rows(pair)), inv_s[0, 0, pair])[0]

        zero_state = jnp.zeros((LANES, DN_HPS * LANES), F32)

        @pl.when(jnp.logical_and(b_i == 0, hp == 0))
        def _():
            dcw_ref[...] = jnp.zeros_like(dcw_ref)
            dalog_ref[...] = jnp.zeros_like(dalog_ref)
            ddtb_ref[...] = jnp.zeros_like(ddtb_ref)
            dgain_ref[...] = jnp.zeros_like(dgain_ref)

        for i, cs in enumerate(_DN_COLS):
            zc = _dn_in_cols(3, i)
            _, post_vjp = jax.vjp(_dn_post, o_ref[:, cs], x_ref[:, zc], gain_ref[...])
            do, dz, dgain = post_vjp(do_ref[:, cs])
            dx_ref[:, zc] = dz
            do_s[:, cs] = do
            dgain_ref[...] += dgain

        def state_back(nn, dstate):
            n = npair - 1 - nn
            r = _pair_rows(n)
            _, state_vjp = jax.vjp(_dn_state, *local_of(n), st_s[0, 0, n])
            *dloc, ds0 = state_vjp((do_s[r, :], dstate))
            for d_s, val in zip(dloc_s, dloc[:5]):
                d_s[r, :] = val
            dcd_s[n, 0:1, :], dcd_s[n, 1:2, :] = dloc[5], dloc[6]
            return ds0

        lax.fori_loop(0, npair, state_back, zero_state)

        wide_cols = lambda s: slice(s * DN_HPS * LANES, (s + 1) * DN_HPS * LANES)

        def local_back(m, _):
            for pair in (2 * m, 2 * m + 1):
                r = _pair_rows(pair)
                inv = inv_s[0, 0, pair]
                local = lambda q, k, v, g, beta, inv=inv: _dn_local(q, k, v, g, beta, inv)[0]
                _, local_vjp = jax.vjp(local, *pair_in(r))
                dq, dk, dv, dg, db = local_vjp(tuple(d_s[r, :] for d_s in dloc_s) + (dcd_s[pair, 0:1, :], dcd_s[pair, 1:2, :]))
                dx_ref[r, wide_cols(0)], dx_ref[r, wide_cols(1)], dx_ref[r, wide_cols(2)] = dq, dk, dv
                dgate_s[r, :] = _dn_pack_gate(_dn_gate_cols(dg, db))
            return 0

        lax.fori_loop(0, npair // 2, local_back, 0)

        lane = _iota((1, LANES), 1)
        dab = 0.0
        for i, cs in enumerate(_DN_COLS):
            h = DN_HPS * hp + i
            for s, l2 in enumerate(_DN_L2):
                xc = _dn_in_cols(s, i)
                _, conv_vjp = jax.vjp(functools.partial(_dn_conv, l2_scale=l2), x_ref[:, xc], *_dn_taps(cw_ref, s, i))
                dx, *dw = conv_vjp(dx_ref[:, xc])
                dx_ref[:, xc] = dx
                for t in range(4):
                    dcw_ref[h + 4 * s, t:t + 1, :] += dw[t]
            _, gate_vjp = jax.vjp(_dn_gate, *_dn_gate_in(ab_ref, alog_ref, dtb_ref, h))
            dgate = dgate_s[...]
            da, db, dalog, ddtb = gate_vjp((_lane_pick(dgate, 2 * i), _lane_pick(dgate, 2 * i + 1)))
            dab = dab + jnp.where(lane == h, da, 0.0) + jnp.where(lane == h + NH, db, 0.0)
            dalog_ref[...] += jnp.where(lane == h, dalog, 0.0)
            ddtb_ref[...] += jnp.where(lane == h, ddtb, 0.0)

        @pl.when(hp == 0)
        def _():
            dab_ref[...] = jnp.zeros_like(dab_ref)

        dab_ref[...] += dab
        if ns:
            pl.when(step_id == nsteps - 1)(finish)

    M = B * T
    one = pl.Buffered(1)
    vec = pl.BlockSpec((1, LANES), lambda b, h: (0, 0))
    wide = [pltpu.VMEM((T, DN_HPS * LANES), F32)]
    vec_shape = jax.ShapeDtypeStruct((1, LANES), F32)
    outs = pl.pallas_call(
        body, name="dn_bwd", grid=(B, NH // DN_HPS),
        in_specs=_dn_in_specs(T) + [pl.BlockSpec((T, DN_HPS * LANES), lambda b, h: (b, h), pipeline_mode=one)] + _dn_kept(B, T)[0]
        + _any_specs(ns),
        out_specs=[pl.BlockSpec((T, DN_BLK), lambda b, h: (b, h), pipeline_mode=one), pl.BlockSpec((T, LANES), lambda b, h: (b, 0)),
                   pl.BlockSpec((12, 4, LANES), lambda b, h: (0, 0, 0)), vec, vec, vec] + _any_specs(ns),
        out_shape=[jax.ShapeDtypeStruct((M, 4 * DN_W), F32), jax.ShapeDtypeStruct((M, LANES), F32),
                   jax.ShapeDtypeStruct((12, 4, LANES), F32), vec_shape, vec_shape, vec_shape] + _chip_swap_shapes(swap),
        scratch_shapes=[pltpu.VMEM((T, LANES), F32), pltpu.VMEM((npair, 8, DN_HPS * LANES), F32)] + wide * 5
        + (_chip_swap_sems(ns) if ns else []),
        compiler_params=pltpu.CompilerParams(vmem_limit_bytes=VMEM_LIMIT_MAX, dimension_semantics=("arbitrary", "arbitrary")),
    )(proj_dn, proj_ab, conv_w, a_log, dt_bias, gain, dmix, *kept, *swap)
    return outs[:6], outs[6:]


SBQ = 256


def _group_rms(x, gain):
    first = _iota(x.shape, 1) < 64
    sq = x * x
    ss_a = jnp.sum(jnp.where(first, sq, 0.0), axis=-1, keepdims=True)
    ss_b = jnp.sum(jnp.where(first, 0.0, sq), axis=-1, keepdims=True)
    ms = jnp.where(first, ss_a, ss_b) * (1.0 / 64)
    return x * lax.rsqrt(ms + EPS) * gain


def _sb_stack(q):
    first = _iota((1, LANES), 1) < 64
    return jnp.concatenate([jnp.where(first, q, 0.0), jnp.where(first, 0.0, q)], axis=0)


def _sb_fold(acc):
    return jnp.where(_iota((1, LANES), 1) < 64, acc[:SBQ], acc[SBQ:])


def _sb_logs(q2, k, diag):
    n = SBQ
    z = _mm(q2, k, ((1,), (1,))) * SB_SCALE
    ls_pos = jnp.minimum(z, 0.0) - jnp.log(1.0 + jnp.exp(-jnp.abs(z)))
    l1m = ls_pos - z
    if not diag:
        return ls_pos, l1m, None
    mask = _iota((2 * n, n), 1) < jnp.bitwise_and(_iota((2 * n, n), 0), n - 1)
    return ls_pos, jnp.where(mask, l1m, 0.0), mask


def _sb_weights(ls_pos, l1m, mask, carry):
    w = jnp.exp(ls_pos + (_mm_ones(_tri_ones(SBQ, False), l1m, False) + carry))
    return w if mask is None else jnp.where(mask, w, 0.0)


def _sb_block(q, k, v, carry, diag):
    ls_pos, l1m, mask = _sb_logs(_sb_stack(q), k, diag)
    w = _sb_weights(ls_pos, l1m, mask, carry)
    return _mm(w, v, ((1,), (0,))), carry + jnp.sum(l1m, axis=-1, keepdims=True)


def _sb_rowsum(q, k, diag):
    return jnp.sum(_sb_logs(_sb_stack(q), k, diag)[1], axis=-1, keepdims=True)


def _sb_block_bwd(q, k, v, carry, diag, dpv, dcarry):
    q2 = _sb_stack(q)
    ls_pos, l1m, mask = _sb_logs(q2, k, diag)
    w = _sb_weights(ls_pos, l1m, mask, carry)
    dv = _mm(w, dpv, ((0,), (0,)))
    de = _mm(dpv, v, ((1,), (1,))) * w
    dl1m = _mm_ones(_tri_ones(SBQ, True), de, False) + dcarry
    if mask is not None:
        dl1m = jnp.where(mask, dl1m, 0.0)
    sig = jnp.exp(ls_pos)
    dz = (de * (1.0 - sig) - dl1m * sig) * SB_SCALE
    dq = _sb_fold(_mm(dz, k, ((1,), (0,))))
    return dq, _mm(dz, q2, ((0,), (0,))), dv, dcarry + jnp.sum(de, axis=-1, keepdims=True)


_SB_Q, _SB_K, _SB_V = (slice(i * LANES, (i + 1) * LANES) for i in range(3))


def _sb_fwd_call(proj_sb, mix, q_gain, k_gain, B, T, gather=()):
    nblk = T // SBQ
    ng = len(gather)
    nsteps = 2 * B

    def body(*refs):
        x_ref, qg_ref, kg_ref = refs[:3]
        out_ref = refs[4 + ng]
        q_s, k_s = refs[5 + 2 * ng:7 + 2 * ng]
        step_id = 2 * pl.program_id(0) + pl.program_id(1)
        if ng:
            send, forward, finish = _gather_phases(refs[4:4 + ng], refs[5 + ng:5 + 2 * ng], *refs[7 + 2 * ng:])
            pl.when(step_id == 0)(send)
            pl.when(step_id == nsteps - 1)(forward)
        q_s[...] = _group_rms(x_ref[:, _SB_Q], qg_ref[...])
        k_s[...] = _group_rms(x_ref[:, _SB_K], kg_ref[...])

        def qblock(i, _):
            ri = pl.ds(pl.multiple_of(i * SBQ, SBQ), SBQ)
            q = q_s[ri, :]

            def kblock(jj, c):
                rj = pl.ds(pl.multiple_of((i - 1 - jj) * SBQ, SBQ), SBQ)
                pv, carry = _sb_block(q, k_s[rj, :], x_ref[rj, _SB_V], c[1], False)
                return c[0] + pv, carry

            on_diag = _sb_block(q, k_s[ri, :], x_ref[ri, _SB_V], jnp.zeros((2 * SBQ, 1), F32), True)
            acc, _c = lax.fori_loop(0, i, kblock, on_diag)
            out_ref[ri, :] = _sb_fold(acc)
            return 0

        lax.fori_loop(0, nblk, qblock, 0)
        if ng:
            pl.when(step_id == nsteps - 1)(finish)

    vec = pl.BlockSpec((1, LANES), lambda b, p: (0, 0))
    outs = pl.pallas_call(
        body, name="sb_fwd", grid=(B, 2),
        in_specs=[pl.BlockSpec((T, 3 * LANES), lambda b, p: (b, p)), vec, vec, pl.BlockSpec(memory_space=pl.ANY)] + _any_specs(ng),
        out_specs=[pl.BlockSpec((T, LANES), lambda b, p: (b, DN_W // LANES + p))] + _any_specs(ng),
        out_shape=[jax.ShapeDtypeStruct((B * T, D), F32)] + _gather_shapes(gather), input_output_aliases={3: 0},
        scratch_shapes=[pltpu.VMEM((T, LANES), F32)] * 2 + (_gather_sems(ng) if ng else []),
        compiler_params=_params(dimension_semantics=("arbitrary", "arbitrary")),
    )(proj_sb, q_gain, k_gain, mix, *gather)
    return outs[0], outs[1:]


def _sb_bwd_call(proj_sb, dmix, q_gain, k_gain, B, T):
    nblk = T // SBQ

    def body(x_ref, qg_ref, kg_ref, do_ref, dx_ref, dqg_ref, dkg_ref, q_s, k_s, dq_s, dk_s, dv_s, c_s):
        b_i, p = pl.program_id(0), pl.program_id(1)
        qn, q_vjp = jax.vjp(_group_rms, x_ref[:, _SB_Q], qg_ref[...])
        kn, k_vjp = jax.vjp(_group_rms, x_ref[:, _SB_K], kg_ref[...])
        q_s[...], k_s[...] = qn, kn
        dk_s[...] = jnp.zeros_like(dk_s)
        dv_s[...] = jnp.zeros_like(dv_s)

        def qblock(i, _):
            ri = pl.ds(pl.multiple_of(i * SBQ, SBQ), SBQ)
            q = q_s[ri, :]
            dacc = _sb_stack(do_ref[ri, :])

            def carries(jj, carry):
                j = i - 1 - jj
                rj = pl.ds(pl.multiple_of(j * SBQ, SBQ), SBQ)
                c_s[j] = carry
                return carry + _sb_rowsum(q, k_s[rj, :], False)

            lax.fori_loop(0, i, carries, _sb_rowsum(q, k_s[ri, :], True))

            def kblock(j, c):
                rj = pl.ds(pl.multiple_of(j * SBQ, SBQ), SBQ)
                dq_j, dk_j, dv_j, dc = _sb_block_bwd(q, k_s[rj, :], x_ref[rj, _SB_V], c_s[j], False, dacc, c[1])
                dk_s[rj, :] += dk_j
                dv_s[rj, :] += dv_j
                return c[0] + dq_j, dc

            dq, dc = lax.fori_loop(0, i, kblock, (jnp.zeros((SBQ, LANES), F32), jnp.zeros((2 * SBQ, 1), F32)))
            dq_i, dk_i, dv_i, _dc = _sb_block_bwd(q, k_s[ri, :], x_ref[ri, _SB_V], jnp.zeros((2 * SBQ, 1), F32), True, dacc, dc)
            dk_s[ri, :] += dk_i
            dv_s[ri, :] += dv_i
            dq_s[ri, :] = dq + dq_i
            return 0

        lax.fori_loop(0, nblk, qblock, 0)
        dq_in, dqg = q_vjp(dq_s[...])
        dk_in, dkg = k_vjp(dk_s[...])
        dx_ref[:, _SB_Q], dx_ref[:, _SB_K], dx_ref[:, _SB_V] = dq_in, dk_in, dv_s[...]

        @pl.when(jnp.logical_and(b_i == 0, p == 0))
        def _():
            dqg_ref[...] = jnp.zeros_like(dqg_ref)
            dkg_ref[...] = jnp.zeros_like(dkg_ref)

        dqg_ref[...] += dqg + pltpu.roll(dqg, 64, 1)
        dkg_ref[...] += dkg + pltpu.roll(dkg, 64, 1)

    M = B * T
    vec = pl.BlockSpec((1, LANES), lambda b, p: (0, 0))
    blk = pl.BlockSpec((T, 3 * LANES), lambda b, p: (b, p))
    big = [pltpu.VMEM((T, LANES), F32)]
    return pl.pallas_call(
        body, name="sb_bwd", grid=(B, 2),
        in_specs=[blk, vec, vec, pl.BlockSpec((T, LANES), lambda b, p: (b, DN_W // LANES + p))],
        out_specs=[blk, vec, vec],
        out_shape=[jax.ShapeDtypeStruct((M, 3 * SB_W), F32)] + [jax.ShapeDtypeStruct((1, LANES), F32)] * 2,
        scratch_shapes=big * 5 + [pltpu.VMEM((nblk, 2 * SBQ, 1), F32)],
        compiler_params=_params(dimension_semantics=("arbitrary", "arbitrary")),
    )(proj_sb, q_gain, k_gain, dmix)


def _sg_chunk(u, v, gain, w_a, w_b, bias):
    n = 128
    row, col = _iota((n, n), 0), _iota((n, n), 1)
    first = _iota((1, LANES), 1) < 64
    vn = _group_rms(_gelu(v), gain)
    tril = col <= row
    mixed = jnp.where(first, _dot(jnp.where(tril, w_a, 0.0), vn), _dot(jnp.where(tril, w_b, 0.0), vn)) + bias
    return _gelu(u) * mixed


_SG_U, _SG_V = slice(0, LANES), slice(LANES, 2 * LANES)


def _sg_fwd_call(proj_sg, mix, gain, sg_w, bias, B, T):
    nchunk = T // 128

    def body(x_ref, g_ref, wa_ref, wb_ref, bias_ref, mix_ref, out_ref):
        del mix_ref

        def step(i, _):
            r = pl.ds(pl.multiple_of(i * 128, 128), 128)
            out_ref[r, :] = _sg_chunk(x_ref[r, _SG_U], x_ref[r, _SG_V], g_ref[...], wa_ref[0], wb_ref[0], bias_ref[...])
            return 0

        lax.fori_loop(0, nchunk, step, 0)

    return pl.pallas_call(
        body, name="sg_fwd", grid=(B, 2),
        in_specs=[pl.BlockSpec((T, 2 * LANES), lambda b, p: (b, p)), pl.BlockSpec((1, LANES), lambda b, p: (0, p)),
                  pl.BlockSpec((1, 128, 128), lambda b, p: (2 * p, 0, 0)), pl.BlockSpec((1, 128, 128), lambda b, p: (2 * p + 1, 0, 0)),
                  pl.BlockSpec((128, LANES), lambda b, p: (0, p)), pl.BlockSpec(memory_space=pl.ANY)],
        out_specs=pl.BlockSpec((T, LANES), lambda b, p: (b, (DN_W + SB_W) // LANES + p)),
        out_shape=jax.ShapeDtypeStruct((B * T, D), F32), input_output_aliases={5: 0},
        compiler_params=_params(dimension_semantics=("arbitrary", "arbitrary")),
    )(proj_sg, gain, sg_w, sg_w, bias, mix)


def _sg_bwd_call(proj_sg, dmix, gain, sg_w, bias, B, T):
    nchunk = T // 128

    def body(x_ref, g_ref, wa_ref, wb_ref, bias_ref, do_ref, dx_ref, dg_ref, dw_ref, db_ref):
        p, b_i = pl.program_id(0), pl.program_id(1)

        def step(i, c):
            r = pl.ds(pl.multiple_of(i * 128, 128), 128)
            _, vjp = jax.vjp(_sg_chunk, x_ref[r, _SG_U], x_ref[r, _SG_V], g_ref[...], wa_ref[0], wb_ref[0], bias_ref[...])
            du, dv, dg, dwa, dwb, dbias = vjp(do_ref[r, :])
            dx_ref[r, _SG_U], dx_ref[r, _SG_V] = du, dv
            return c[0] + dg, c[1] + dwa, c[2] + dwb, c[3] + dbias

        z = jnp.zeros((128, 128), F32)
        dg, dwa, dwb, dbias = lax.fori_loop(0, nchunk, step, (jnp.zeros((1, LANES), F32), z, z, z))
        lane = _iota((1, LANES), 1)
        first = lane < 64
        s_a = jnp.sum(jnp.where(first, dbias, 0.0), axis=-1, keepdims=True)
        s_b = jnp.sum(jnp.where(first, 0.0, dbias), axis=-1, keepdims=True)
        dbg = jnp.where(lane == 2 * p, s_a, 0.0) + jnp.where(lane == 2 * p + 1, s_b, 0.0)

        @pl.when(b_i == 0)
        def _():
            dg_ref[...] = jnp.zeros_like(dg_ref)
            dw_ref[...] = jnp.zeros_like(dw_ref)

        @pl.when(jnp.logical_and(b_i == 0, p == 0))
        def _():
            db_ref[...] = jnp.zeros_like(db_ref)

        dg_ref[...] += dg
        dw_ref[0] += dwa
        dw_ref[1] += dwb
        db_ref[...] += dbg

    M = B * T
    blk = pl.BlockSpec((T, 2 * LANES), lambda p, b: (b, p))
    return pl.pallas_call(
        body, name="sg_bwd", grid=(2, B),
        in_specs=[blk, pl.BlockSpec((1, LANES), lambda p, b: (0, p)),
                  pl.BlockSpec((1, 128, 128), lambda p, b: (2 * p, 0, 0)), pl.BlockSpec((1, 128, 128), lambda p, b: (2 * p + 1, 0, 0)),
                  pl.BlockSpec((128, LANES), lambda p, b: (0, p)),
                  pl.BlockSpec((T, LANES), lambda p, b: (b, (DN_W + SB_W) // LANES + p))],
        out_specs=[blk, pl.BlockSpec((1, LANES), lambda p, b: (0, p)), pl.BlockSpec((2, 128, 128), lambda p, b: (p, 0, 0)),
                   pl.BlockSpec((128, LANES), lambda p, b: (0, 0))],
        out_shape=[jax.ShapeDtypeStruct((M, 2 * SG_W), F32), jax.ShapeDtypeStruct((1, SG_W), F32),
                   jax.ShapeDtypeStruct((4, 128, 128), F32), jax.ShapeDtypeStruct((128, LANES), F32)],
        compiler_params=_params(dimension_semantics=("arbitrary", "arbitrary")),
    )(proj_sg, gain, sg_w, sg_w, bias, dmix)


def _row_tile(m, most=512):
    return min(m, most)


def _in_proj_call(x, gain, wt):
    m = x.shape[0]
    tm = _row_tile(m)

    def body(x_ref, g_ref, wt_ref, *out_refs):
        h = _rms(x_ref[...], g_ref[...]).astype(BF16)
        out_refs[-1][...] = h
        for (off, width), out_ref in zip(SECTIONS, out_refs):
            out_ref[...] = lax.dot_general(h, wt_ref[off:off + width, :], (((1,), (1,)), ((), ())), preferred_element_type=F32)

    rows = lambda width: pl.BlockSpec((tm, width), lambda i: (i, 0))
    return pl.pallas_call(
        body, name="in_proj", grid=(m // tm,),
        in_specs=[rows(D), pl.BlockSpec((1, D), lambda i: (0, 0)),
                  pl.BlockSpec((NPACK, D), lambda i: (0, 0), pipeline_mode=pl.Buffered(1))],
        out_specs=[rows(w) for _, w in SECTIONS] + [rows(D)],
        out_shape=[jax.ShapeDtypeStruct((m, w), F32) for _, w in SECTIONS] + [jax.ShapeDtypeStruct((m, D), BF16)],
        compiler_params=_params(dimension_semantics=("arbitrary",)),
    )(x, gain, wt)


def _in_proj_bwd_call(dsections, wt, x, gain, dres):
    m = x.shape[0]
    tm = _row_tile(m)

    def body(*refs):
        ds_refs, (wt_ref, x_ref, g_ref, dres_ref, dx_ref, dg_ref) = refs[:len(SECTIONS)], refs[len(SECTIONS):]

        @pl.when(pl.program_id(0) == 0)
        def _():
            dg_ref[...] = jnp.zeros_like(dg_ref)

        dh = 0.0
        for (off, width), ds_ref in zip(SECTIONS, ds_refs):
            dh = dh + jnp.dot(ds_ref[...].astype(BF16), wt_ref[off:off + width, :], preferred_element_type=F32)
        _, vjp = jax.vjp(_rms, x_ref[...], g_ref[...])
        dx, dg = vjp(dh)
        dx_ref[...] = dres_ref[...] + dx
        dg_ref[...] += dg

    rows = lambda width: pl.BlockSpec((tm, width), lambda i: (i, 0))
    return pl.pallas_call(
        body, name="in_proj_bwd", grid=(m // tm,),
        in_specs=[rows(w) for _, w in SECTIONS] + [pl.BlockSpec((NPACK, D), lambda i: (0, 0), pipeline_mode=pl.Buffered(1)),
                                                   rows(D), pl.BlockSpec((1, D), lambda i: (0, 0)), rows(D)],
        out_specs=[rows(D), pl.BlockSpec((1, D), lambda i: (0, 0))],
        out_shape=[jax.ShapeDtypeStruct((m, D), F32), jax.ShapeDtypeStruct((1, D), F32)],
        compiler_params=_params(dimension_semantics=("arbitrary",)),
    )(*dsections, wt, x, gain, dres)


def _in_proj_grad_call(dsections, h):
    m = h.shape[0]
    tm = min(m, 256)

    def body(*refs):
        ds_refs, (h_ref, out_ref) = refs[:len(SECTIONS)], refs[len(SECTIONS):]

        @pl.when(pl.program_id(0) == 0)
        def _():
            out_ref[...] = jnp.zeros_like(out_ref)

        for (off, width), ds_ref in zip(SECTIONS, ds_refs):
            out_ref[off:off + width, :] += lax.dot_general(ds_ref[...].astype(BF16), h_ref[...], (((0,), (0,)), ((), ())),
                                                           preferred_element_type=F32)

    rows = lambda width: pl.BlockSpec((tm, width), lambda i: (i, 0))
    return pl.pallas_call(
        body, name="grad_w_in", grid=(m // tm,),
        in_specs=[rows(w) for _, w in SECTIONS] + [rows(D)],
        out_specs=pl.BlockSpec((NPACK, D), lambda i: (0, 0), pipeline_mode=pl.Buffered(1)),
        out_shape=jax.ShapeDtypeStruct((NPACK, D), F32),
        compiler_params=_params(dimension_semantics=("arbitrary",)),
    )(*dsections, h)


def _packed_column_of():
    t = np.full(NPACK, -1, np.int64)
    lanes = np.arange(LANES)
    for pair in range(2):
        for s in range(4):
            t[DN_OFF + pair * 1024 + s * 256 + np.arange(256)] = s * DN_W + pair * 256 + np.arange(256)
        for s in range(3):
            t[SB_OFF + pair * 384 + s * LANES + lanes] = 2056 + s * SB_W + pair * LANES + lanes
        for s in range(2):
            t[SG_OFF + pair * 256 + s * LANES + lanes] = 2056 + 3 * SB_W + s * SG_W + pair * LANES + lanes
    t[AB_OFF + np.arange(2 * NH)] = 4 * DN_W + np.arange(2 * NH)
    return t


def _row_tables():
    col = _packed_column_of()
    fwd = np.where(col >= 0, (col // IN_SHARD) * IN_SHARD_PAD + col % IN_SHARD, -1)
    packed_of = np.full(IN_DIM, -1, np.int64)
    packed_of[col[col >= 0]] = np.nonzero(col >= 0)[0]
    r = np.arange(NDEV * IN_SHARD_PAD)
    inside = r % IN_SHARD_PAD < IN_SHARD
    back = np.where(inside, packed_of[np.minimum((r // IN_SHARD_PAD) * IN_SHARD + r % IN_SHARD_PAD, IN_DIM - 1)], -1)
    return fwd, back


def _row_perm_call(src, table, name):
    n_out = table.shape[0]
    touched = [sorted(set((table[b * 128:(b + 1) * 128][table[b * 128:(b + 1) * 128] >= 0] // 128).tolist()))
               for b in range(n_out // 128)]

    def body(tbl_ref, src_ref, out_ref):
        lane = _iota((1, LANES), 1)
        for b, blocks in enumerate(touched):
            want = tbl_ref[b * 128:(b + 1) * 128, :]
            acc = jnp.zeros((128, D), F32)
            for sb in blocks:
                pick = jnp.where(want == sb * 128 + lane, 1.0, 0.0).astype(BF16)
                acc = acc + jnp.dot(pick, src_ref[sb * 128:(sb + 1) * 128, :].astype(BF16), preferred_element_type=F32)
            out_ref[b * 128:(b + 1) * 128, :] = acc.astype(BF16)

    return pl.pallas_call(
        body, name=name, out_shape=jax.ShapeDtypeStruct((n_out, D), BF16),
        in_specs=[pl.BlockSpec(memory_space=pltpu.VMEM)] * 2, out_specs=pl.BlockSpec(memory_space=pltpu.VMEM),
        compiler_params=_params(),
    )(jnp.asarray(table.reshape(-1, 1), jnp.int32), src)


def _out_proj_call(a, w, res):
    m, k = a.shape
    n = w.shape[1]
    tm = _row_tile(m)

    def body(a_ref, w_ref, res_ref, out_ref):
        out_ref[...] = res_ref[...] + jnp.dot(a_ref[...].astype(BF16), w_ref[...], preferred_element_type=F32)

    return pl.pallas_call(
        body, name="out_proj", grid=(m // tm,),
        in_specs=[pl.BlockSpec((tm, k), lambda i: (i, 0)), pl.BlockSpec((k, n), lambda i: (0, 0)),
                  pl.BlockSpec((tm, n), lambda i: (i, 0))],
        out_specs=pl.BlockSpec((tm, n), lambda i: (i, 0)),
        out_shape=jax.ShapeDtypeStruct((m, n), F32),
        compiler_params=_params(dimension_semantics=("arbitrary",)),
    )(a, w, res)


def _ffn_specs(tm):
    return [pl.BlockSpec((1, D, FF_SHARD), lambda i, j: (j, 0, 0)), pl.BlockSpec((FF_SHARD, D), lambda i, j: (j, 0))]


def _ffn_fwd_call(x, gain, w1, w2):
    m = x.shape[0]
    tm = _row_tile(m, 1024)

    def body(x_ref, g_ref, w1_ref, w2_ref, out_ref, h_s, acc_s):
        j = pl.program_id(1)

        @pl.when(j == 0)
        def _():
            h_s[...] = _rms(x_ref[...], g_ref[...]).astype(BF16)
            acc_s[...] = jnp.zeros_like(acc_s)

        a = jnp.maximum(jnp.dot(h_s[...], w1_ref[0], preferred_element_type=F32), 0.0)
        acc_s[...] += jnp.dot((a * a).astype(BF16), w2_ref[...], preferred_element_type=F32)

        @pl.when(j == NDEV - 1)
        def _():
            out_ref[...] = x_ref[...] + acc_s[...]

    return pl.pallas_call(
        body, name="ffn_fwd", grid=(m // tm, NDEV),
        in_specs=[pl.BlockSpec((tm, D), lambda i, j: (i, 0)), pl.BlockSpec((1, D), lambda i, j: (0, 0))] + _ffn_specs(tm),
        out_specs=pl.BlockSpec((tm, D), lambda i, j: (i, 0)),
        out_shape=jax.ShapeDtypeStruct((m, D), F32),
        scratch_shapes=[pltpu.VMEM((tm, D), BF16), pltpu.VMEM((tm, D), F32)],
        compiler_params=_params(dimension_semantics=("arbitrary", "arbitrary")),
    )(x, gain, w1, w2)


def _ffn_bwd_call(x, dy, gain, w1, w2, swap=()):
    m = x.shape[0]
    tm = _row_tile(m, 1024)
    ns = len(swap)

    def body(*refs):
        x_ref, dy_ref, g_ref, w1_ref, w2_ref = refs[:5]
        dx_ref, da_ref, r_ref, h_ref, dg_ref = refs[5 + ns:10 + ns]
        acc_s = refs[10 + 2 * ns]
        i, j = pl.program_id(0), pl.program_id(1)
        if ns:
            send, finish = _sibling_swap_phases(refs[5:5 + ns], refs[10 + ns:10 + 2 * ns], *refs[11 + 2 * ns:])
            pl.when(jnp.logical_and(i == 0, j == 0))(send)

        @pl.when(j == 0)
        def _():
            h_ref[...] = _rms(x_ref[...], g_ref[...]).astype(BF16)
            acc_s[...] = jnp.zeros_like(acc_s)

        @pl.when(jnp.logical_and(i == 0, j == 0))
        def _():
            dg_ref[...] = jnp.zeros_like(dg_ref)

        a = jnp.maximum(jnp.dot(h_ref[...], w1_ref[0], preferred_element_type=F32), 0.0)
        r_ref[...] = (a * a).astype(BF16)
        dr = lax.dot_general(dy_ref[...].astype(BF16), w2_ref[...], (((1,), (1,)), ((), ())), preferred_element_type=F32)
        da = (dr * (2.0 * a)).astype(BF16)
        da_ref[...] = da
        acc_s[...] += lax.dot_general(da, w1_ref[0], (((1,), (1,)), ((), ())), preferred_element_type=F32)

        @pl.when(j == NDEV - 1)
        def _():
            _, vjp = jax.vjp(_rms, x_ref[...], g_ref[...])
            dx, dg = vjp(acc_s[...])
            dx_ref[...] = dy_ref[...] + dx
            dg_ref[...] += dg

        if ns:
            pl.when(jnp.logical_and(i == m // tm - 1, j == NDEV - 1))(finish)

    outs = pl.pallas_call(
        body, name="ffn_bwd", grid=(m // tm, NDEV),
        in_specs=[pl.BlockSpec((tm, D), lambda i, j: (i, 0)), pl.BlockSpec((tm, D), lambda i, j: (i, 0)),
                  pl.BlockSpec((1, D), lambda i, j: (0, 0))] + _ffn_specs(tm) + _any_specs(ns),
        out_specs=[pl.BlockSpec((tm, D), lambda i, j: (i, 0)), pl.BlockSpec((tm, FF_SHARD), lambda i, j: (i, j)),
                   pl.BlockSpec((tm, FF_SHARD), lambda i, j: (i, j)), pl.BlockSpec((tm, D), lambda i, j: (i, 0)),
                   pl.BlockSpec((1, D), lambda i, j: (0, 0))] + _any_specs(ns),
        out_shape=[jax.ShapeDtypeStruct((m, D), F32), jax.ShapeDtypeStruct((m, DFF), BF16), jax.ShapeDtypeStruct((m, DFF), BF16),
                   jax.ShapeDtypeStruct((m, D), BF16), jax.ShapeDtypeStruct((1, D), F32)] + _sibling_swap_shapes(swap),
        scratch_shapes=[pltpu.VMEM((tm, D), F32)] + (_sibling_swap_sems(ns) if ns else []),
        compiler_params=_params(dimension_semantics=("arbitrary", "arbitrary")),
    )(x, dy, gain, w1, w2, *swap)
    return outs[:5], outs[5:]


def _mm_nt_call(a, b, name):
    m, k = a.shape
    n = b.shape[0]
    tm = _row_tile(m)

    def body(a_ref, b_ref, out_ref):
        out_ref[...] = lax.dot_general(a_ref[...].astype(BF16), b_ref[...].astype(BF16), (((1,), (1,)), ((), ())),
                                       preferred_element_type=F32)

    return pl.pallas_call(
        body, name=name, grid=(m // tm,),
        in_specs=[pl.BlockSpec((tm, k), lambda i: (i, 0)), pl.BlockSpec((n, k), lambda i: (0, 0))],
        out_specs=pl.BlockSpec((tm, n), lambda i: (i, 0)),
        out_shape=jax.ShapeDtypeStruct((m, n), F32),
        compiler_params=_params(dimension_semantics=("arbitrary",)),
    )(a, b)


def _mm_tn_call(a, b, name, col_shards=False):
    m, k = a.shape
    n = b.shape[1]
    tm, tk = _row_tile(m, 1024), min(k, 1024)
    tn = n // NDEV if col_shards else min(n, 1024)

    def body(a_ref, b_ref, out_ref, acc_s):
        s = pl.program_id(2)

        @pl.when(s == 0)
        def _():
            acc_s[...] = jnp.zeros_like(acc_s)

        acc_s[...] += lax.dot_general(a_ref[...].astype(BF16), b_ref[...].astype(BF16), (((0,), (0,)), ((), ())),
                                      preferred_element_type=F32)

        @pl.when(s == m // tm - 1)
        def _():
            out_ref[...] = acc_s[...].astype(BF16).reshape(out_ref.shape)

    if col_shards:
        out_spec, out_shape = pl.BlockSpec((1, tk, tn), lambda i, j, s: (j, i, 0)), (NDEV, k, tn)
    else:
        out_spec, out_shape = pl.BlockSpec((tk, tn), lambda i, j, s: (i, j)), (k, n)
    return pl.pallas_call(
        body, name=name, grid=(k // tk, n // tn, m // tm),
        in_specs=[pl.BlockSpec((tm, tk), lambda i, j, s: (s, i)), pl.BlockSpec((tm, tn), lambda i, j, s: (s, j))],
        out_specs=out_spec, out_shape=jax.ShapeDtypeStruct(out_shape, BF16),
        scratch_shapes=[pltpu.VMEM((tk, tn), F32)],
        compiler_params=_params(dimension_semantics=("arbitrary", "arbitrary", "arbitrary")),
    )(a, b)


def _loss_call(y, target):
    m = y.shape[0]
    tm = _row_tile(m)

    def body(y_ref, t_ref, loss_ref, dy_ref):
        @pl.when(pl.program_id(0) == 0)
        def _():
            loss_ref[...] = jnp.zeros_like(loss_ref)

        err = y_ref[...] - t_ref[...]
        dy_ref[...] = err * (1.0 / D)
        per_row = jnp.mean(err * err, axis=-1, keepdims=True)
        loss_ref[...] += jnp.broadcast_to(0.5 * jnp.sum(per_row, axis=0, keepdims=True), (1, LANES))

    return pl.pallas_call(
        body, name="loss", grid=(m // tm,),
        in_specs=[pl.BlockSpec((tm, D), lambda i: (i, 0))] * 2,
        out_specs=[pl.BlockSpec((1, LANES), lambda i: (0, 0)), pl.BlockSpec((tm, D), lambda i: (i, 0))],
        out_shape=[jax.ShapeDtypeStruct((1, LANES), F32), jax.ShapeDtypeStruct((m, D), F32)],
        compiler_params=_params(dimension_semantics=("arbitrary",)),
    )(y, target)


def _adamw_call(w, g, m, v, name):
    shape = w.shape
    cols = shape[-1] if w.ndim > 1 else w.size
    rows = w.size // cols
    tr = rows if (rows <= 512 or rows % 512) else 512
    c1, c2 = 1.0 - ADAM_B1 ** ADAM_STEP, 1.0 - ADAM_B2 ** ADAM_STEP

    def body(w_ref, g_ref, m_ref, v_ref, d_ref, nm_ref, nv_ref):
        g_ = g_ref[...]
        nm = ADAM_B1 * m_ref[...] + (1.0 - ADAM_B1) * g_
        nv = ADAM_B2 * v_ref[...] + (1.0 - ADAM_B2) * (g_ * g_)
        d_ref[...] = -ADAM_LR * ((nm / c1) / (jnp.sqrt(nv / c2) + ADAM_EPS) + ADAM_WD * w_ref[...])
        nm_ref[...], nv_ref[...] = nm, nv

    spec = pl.BlockSpec((tr, cols), lambda i: (i, 0))
    outs = pl.pallas_call(
        body, name=name, grid=(rows // tr,), in_specs=[spec] * 4, out_specs=[spec] * 3,
        out_shape=[jax.ShapeDtypeStruct((rows, cols), F32)] * 3,
        compiler_params=_params(dimension_semantics=("arbitrary",)),
    )(*(t.reshape(rows, cols) for t in (w, g, m, v)))
    return tuple(o.reshape(shape) for o in outs)


def _sum_tile(rows):
    for cand in (2048, 1024, 512, 256, 128):
        if rows > cand and rows % cand == 0:
            return cand
    return rows


def _pair_sum_call(g, got, core, name):
    rows, cols = g.shape[1:]
    tr = _sum_tile(rows)

    def body(core_ref, g_ref, got_ref, out_ref):
        del core_ref
        out_ref[...] = (g_ref[...].astype(F32) + got_ref[...].astype(F32)).astype(BF16)

    grid_spec = pltpu.PrefetchScalarGridSpec(
        num_scalar_prefetch=1, grid=(4, rows // tr),
        in_specs=[pl.BlockSpec((1, tr, cols), lambda ch, t, core_ref: (2 * ch + core_ref[0], t, 0)),
                  pl.BlockSpec((1, tr, cols), lambda ch, t, core_ref: (ch, t, 0))],
        out_specs=pl.BlockSpec((1, tr, cols), lambda ch, t, core_ref: (ch, t, 0)))
    return pl.pallas_call(
        body, name=name, grid_spec=grid_spec, out_shape=jax.ShapeDtypeStruct((4, rows, cols), BF16),
        compiler_params=_params(dimension_semantics=("arbitrary", "arbitrary")),
    )(jnp.asarray(core, jnp.int32).reshape(1), g, got)


def _sum_call(parts, out_dtype, name):
    rows, cols = parts[0][0].shape[1:]
    tr = _sum_tile(rows)
    index = jnp.stack([jnp.asarray(i, jnp.int32) for _, i in parts])

    def body(idx_ref, *refs):
        del idx_ref
        acc = refs[0][0].astype(F32)
        for r in refs[1:-1]:
            acc = acc + r[0].astype(F32)
        refs[-1][...] = acc.astype(out_dtype)

    grid_spec = pltpu.PrefetchScalarGridSpec(
        num_scalar_prefetch=1, grid=(rows // tr,),
        in_specs=[pl.BlockSpec((1, tr, cols), lambda t, idx, n=n: (idx[n], t, 0)) for n in range(len(parts))],
        out_specs=pl.BlockSpec((tr, cols), lambda t, idx: (t, 0)))
    return pl.pallas_call(
        body, name=name, grid_spec=grid_spec, out_shape=jax.ShapeDtypeStruct((rows, cols), out_dtype),
        compiler_params=_params(dimension_semantics=("arbitrary",)),
    )(index, *(a for a, _ in parts))


def _place():
    return lax.axis_index("x"), lax.axis_index("y"), lax.axis_index("c")


def _any_specs(n):
    return [pl.BlockSpec(memory_space=pl.ANY)] * n


def _all_gather_call(xs, name):
    n = len(xs)

    def body(*refs):
        for phase in _gather_phases(refs[:n], refs[n:2 * n], *refs[2 * n:]):
            phase()

    return pl.pallas_call(
        body, name=name, in_specs=_any_specs(n), out_specs=_any_specs(n),
        out_shape=_gather_shapes(xs), scratch_shapes=_gather_sems(n),
    )(*xs)


def _gather_shapes(xs):
    return [jax.ShapeDtypeStruct((NDEV,) + x.shape, x.dtype) for x in xs]


def _gather_sems(n):
    return [pltpu.SemaphoreType.DMA((7 * n,)), pltpu.SemaphoreType.DMA((7 * n,)), pltpu.SemaphoreType.DMA((n,))]


def _gather_phases(x_refs, out_refs, send_sems, recv_sems, local_sems):
    n = len(x_refs)
    ax, ay, ac = _place()
    me, sibling = (ax, ay, ac), (ax, ay, 1 - ac)
    chips = [(1 - ax, ay), (ax, 1 - ay), (1 - ax, 1 - ay)]

    def copy(a, k, block, to, src=None):
        slot = out_refs[a].at[4 * block[0] + 2 * block[1] + block[2]]
        return pltpu.make_async_remote_copy(
            src_ref=slot if src is None else src, dst_ref=slot,
            send_sem=send_sems.at[7 * a + k], recv_sem=recv_sems.at[7 * a + k], device_id=to, device_id_type=MESH)

    local = [pltpu.make_async_copy(x_refs[a], out_refs[a].at[4 * ax + 2 * ay + ac], local_sems.at[a]) for a in range(n)]
    first = []
    for a in range(n):
        first.append(copy(a, 0, me, sibling, src=x_refs[a]))
        first += [copy(a, 1 + j, me, (*chip, ac), src=x_refs[a]) for j, chip in enumerate(chips)]
    passed = [copy(a, 4 + j, (*chip, ac), sibling) for j, chip in enumerate(chips) for a in range(n)]

    def send():
        for cp in local + first:
            cp.start()

    def forward():
        for j, chip in enumerate(chips):
            for a in range(n):
                copy(a, 1 + j, (*chip, ac), me).wait_recv()
                passed[j * n + a].start()

    def finish():
        for a in range(n):
            copy(a, 0, sibling, me).wait_recv()
            for j, chip in enumerate(chips):
                copy(a, 4 + j, (*chip, 1 - ac), me).wait_recv()
        for cp in first + passed:
            cp.wait_send()
        for cp in local:
            cp.wait()

    return send, forward, finish


def _swap_sibling_call(xs, name):
    n = len(xs)

    def body(*refs):
        for phase in _sibling_swap_phases(refs[:n], refs[n:2 * n], *refs[2 * n:]):
            phase()

    return pl.pallas_call(
        body, name=name, in_specs=_any_specs(n), out_specs=_any_specs(n),
        out_shape=_sibling_swap_shapes(xs), scratch_shapes=_sibling_swap_sems(n),
    )(*xs)


def _sibling_swap_shapes(xs):
    return [jax.ShapeDtypeStruct((4,) + x.shape[1:], x.dtype) for x in xs]


def _sibling_swap_sems(n):
    return [pltpu.SemaphoreType.DMA((n,)), pltpu.SemaphoreType.DMA((n,))]


def _sibling_swap_phases(x_refs, out_refs, send_sems, recv_sems):
    ax, ay, ac = _place()
    sibling = (ax, ay, 1 - ac)

    def send():
        for a, (x_ref, out_ref) in enumerate(zip(x_refs, out_refs)):
            for chip in range(4):
                pltpu.make_async_remote_copy(src_ref=x_ref.at[2 * chip + 1 - ac], dst_ref=out_ref.at[chip],
                                             send_sem=send_sems.at[a], recv_sem=recv_sems.at[a],
                                             device_id=sibling, device_id_type=MESH).start()

    def finish():
        for a, (x_ref, out_ref) in enumerate(zip(x_refs, out_refs)):
            pltpu.make_async_remote_copy(src_ref=x_ref.at[pl.ds(0, 4)], dst_ref=out_ref, send_sem=send_sems.at[a],
                                         recv_sem=recv_sems.at[a], device_id=sibling, device_id_type=MESH).wait()

    return send, finish


def _swap_chips_call(xs, name):
    n = len(xs)

    def body(*refs):
        for phase in _chip_swap_phases(refs[:n], refs[n:2 * n], *refs[2 * n:]):
            phase()

    return pl.pallas_call(
        body, name=name, in_specs=_any_specs(n), out_specs=_any_specs(n),
        out_shape=_chip_swap_shapes(xs), scratch_shapes=_chip_swap_sems(n),
    )(*xs)


def _chip_swap_shapes(xs):
    return [jax.ShapeDtypeStruct((3,) + x.shape[1:], x.dtype) for x in xs]


def _chip_swap_sems(n):
    return [pltpu.SemaphoreType.DMA((3 * n,)), pltpu.SemaphoreType.DMA((3 * n,))]


def _chip_swap_phases(x_refs, out_refs, send_sems, recv_sems):
    ax, ay, ac = _place()
    chips = [(1 - ax, ay), (ax, 1 - ay), (1 - ax, 1 - ay)]
    copies = [pltpu.make_async_remote_copy(src_ref=x_refs[a].at[2 * cx + cy], dst_ref=out_refs[a].at[j],
                                           send_sem=send_sems.at[3 * a + j], recv_sem=recv_sems.at[3 * a + j],
                                           device_id=(cx, cy, ac), device_id_type=MESH)
              for a in range(len(x_refs)) for j, (cx, cy) in enumerate(chips)]

    def send():
        for cp in copies:
            cp.start()

    def finish():
        for cp in copies:
            cp.wait()

    return send, finish


def _reduce_begin(gs, name):
    ac = lax.axis_index("c")
    got = _swap_sibling_call(gs, name + "_d2d")
    return got, [_pair_sum_call(g, t, ac, f"{name}_pair{a}") for a, (g, t) in enumerate(zip(gs, got))]


def _reduce_end(gs, got, from_chips, name):
    ax, ay, ac = _place()
    me, my_chip = 4 * ax + 2 * ay + ac, 2 * ax + ay
    return [_sum_call([(g, me), (t, my_chip), (f, 0), (f, 1), (f, 2)], F32, f"{name}_total{a}")
            for a, (g, t, f) in enumerate(zip(gs, got, from_chips))]


SMALL = ("norm1_g", "conv_w", "a_log", "dt_bias", "dn_out_g", "sb_q_g", "sb_k_g", "sg_v_g", "sg_w", "sg_b", "norm2_g")
WEIGHTS = ("norm1_g", "w_in", "conv_w", "a_log", "dt_bias", "dn_out_g", "sb_q_g", "sb_k_g", "sg_v_g", "sg_w", "sg_b",
           "w_out", "norm2_g", "w_ff1", "w_ff2")
SMALL_SHAPE = {"norm1_g": (D,), "conv_w": (4, 3 * DN_W), "a_log": (NH,), "dt_bias": (NH,), "dn_out_g": (128,), "sb_q_g": (64,),
               "sb_k_g": (64,), "sg_v_g": (SG_W,), "sg_w": (NH, 128, 128), "sg_b": (NH, 128), "norm2_g": (D,)}


def _size(shape):
    n = 1
    for s in shape:
        n *= s
    return n


def _to_rows(flat, multiple):
    pad = (-flat.shape[0]) % (LANES * multiple)
    return jnp.pad(flat, (0, pad)).reshape(-1, LANES)


def _conv_by_pair(conv):
    return conv.reshape(4, 3, 2, 256).transpose(0, 2, 1, 3).reshape(4, 3 * DN_W)


def kernel(x, norm1_g, w_in, conv_w, a_log, dt_bias, dn_out_g, sb_q_g, sb_k_g, sg_v_g, sg_w, sg_b, w_out, norm2_g, w_ff1, w_ff2, loss_target, m_norm1_g, m_w_in, m_conv_w, m_a_log, m_dt_bias, m_dn_out_g, m_sb_q_g, m_sb_k_g, m_sg_v_g, m_sg_w, m_sg_b, m_w_out, m_norm2_g, m_w_ff1, m_w_ff2, v_norm1_g, v_w_in, v_conv_w, v_a_log, v_dt_bias, v_dn_out_g, v_sb_q_g, v_sb_k_g, v_sg_v_g, v_sg_w, v_sg_b, v_w_out, v_norm2_g, v_w_ff1, v_w_ff2):
    given = dict(norm1_g=norm1_g, w_in=w_in, conv_w=conv_w, a_log=a_log, dt_bias=dt_bias, dn_out_g=dn_out_g, sb_q_g=sb_q_g,
                 sb_k_g=sb_k_g, sg_v_g=sg_v_g, sg_w=sg_w, sg_b=sg_b, w_out=w_out, norm2_g=norm2_g, w_ff1=w_ff1, w_ff2=w_ff2)
    mom = dict(norm1_g=m_norm1_g, w_in=m_w_in, conv_w=m_conv_w, a_log=m_a_log, dt_bias=m_dt_bias, dn_out_g=m_dn_out_g,
               sb_q_g=m_sb_q_g, sb_k_g=m_sb_k_g, sg_v_g=m_sg_v_g, sg_w=m_sg_w, sg_b=m_sg_b, w_out=m_w_out, norm2_g=m_norm2_g,
               w_ff1=m_w_ff1, w_ff2=m_w_ff2)
    var = dict(norm1_g=v_norm1_g, w_in=v_w_in, conv_w=v_conv_w, a_log=v_a_log, dt_bias=v_dt_bias, dn_out_g=v_dn_out_g,
               sb_q_g=v_sb_q_g, sb_k_g=v_sb_k_g, sg_v_g=v_sg_v_g, sg_w=v_sg_w, sg_b=v_sg_b, w_out=v_w_out, norm2_g=v_norm2_g,
               w_ff1=v_w_ff1, w_ff2=v_w_ff2)
    B, T, _ = x.shape
    M = B * T
    ax, ay, ac = _place()
    me = 4 * ax + 2 * ay + ac
    table_fwd, table_back = _row_tables()

    send = []
    for l in range(2):
        w_in_t = jnp.pad(w_in[l].T, ((0, IN_SHARD_PAD - IN_SHARD), (0, 0)))
        send.append([w_in_t.astype(BF16), w_out[l].astype(BF16), w_ff1[l].astype(BF16), w_ff2[l].astype(BF16)])
    first_in, conv_rows = _all_gather_call([send[0][0], _to_rows(conv_w.reshape(-1), 8)], "gather_first")
    conv_full = conv_rows.reshape(NDEV, -1)[:, :conv_w.size].reshape(NDEV, 2, 4, -1).transpose(1, 2, 0, 3).reshape(2, 4, 3 * DN_W)
    gathered = [[first_in, None, None, None], [None] * 4]

    pad_vec = lambda v: jnp.zeros((1, LANES), F32).at[0, :v.shape[0]].set(v)
    layer = []
    for l in range(2):
        layer.append(dict(
            g1=norm1_g[l].reshape(1, D), g2=norm2_g[l].reshape(1, D), conv=_conv_by_pair(conv_full[l]),
            a_log=pad_vec(a_log[l]), dt_bias=pad_vec(dt_bias[l]), dn_g=dn_out_g[l].reshape(1, LANES),
            sb_qg=jnp.tile(sb_q_g[l], 2).reshape(1, LANES), sb_kg=jnp.tile(sb_k_g[l], 2).reshape(1, LANES),
            sg_g=sg_v_g[l].reshape(1, SG_W), sg_w=sg_w[l], sg_bias=jnp.repeat(sg_b[l].T, 64, axis=1)))

    cur = x.reshape(M, D)
    saved = []
    for l, p in enumerate(layer):
        p["wt"] = _row_perm_call(gathered[l][0].reshape(NDEV * IN_SHARD_PAD, D), table_fwd, "pack_w_in")
        p_dn, p_sb, p_sg, p_ab, h = _in_proj_call(cur, p["g1"], p["wt"])
        mix, dn_kept, arrived = _dn_fwd_call(p_dn, p_ab, p["conv"], p["a_log"], p["dt_bias"], p["dn_g"], B, T,
                                             gather=send[0][1:] + send[1][:1] if l == 0 else [])
        if l == 0:
            gathered[0][1:], gathered[1][0] = list(arrived[:3]), arrived[3]
        p["w_out"], p["w1"], p["w2"] = gathered[l][1].reshape(D, D), gathered[l][2], gathered[l][3].reshape(DFF, D)
        mix, arrived = _sb_fwd_call(p_sb, mix, p["sb_qg"], p["sb_kg"], B, T, gather=send[1][1:] if l == 0 else [])
        if l == 0:
            gathered[1][1:] = list(arrived)
        mix = _sg_fwd_call(p_sg, mix, p["sg_g"], p["sg_w"], p["sg_bias"], B, T)
        x1 = _out_proj_call(mix, p["w_out"], cur)
        x2 = _ffn_fwd_call(x1, p["g2"], p["w1"], p["w2"])
        saved.append(dict(x0=cur, p_dn=p_dn, p_sb=p_sb, p_sg=p_sg, p_ab=p_ab, h=h, mix=mix, x1=x1, dn_kept=dn_kept))
        cur = x2
    loss_part, dy = _loss_call(cur, loss_target.reshape(M, D))
    loss = lax.psum(loss_part[0, 0], ("x", "y", "c"))

    big_grads = [[None] * 4, [None] * 4]
    small_grads = {n: [None, None] for n in SMALL}
    for l in (1, 0):
        p, s = layer[l], saved[l]
        (dx1, da, r, h2, dg2), got1 = _ffn_bwd_call(s["x1"], dy, p["g2"], p["w1"], p["w2"], swap=big_grads[1] if l == 0 else ())
        big_grads[l][2] = _mm_tn_call(h2, da, "grad_w_ff1", col_shards=True)
        big_grads[l][3] = _mm_tn_call(r, dy, "grad_w_ff2").reshape(NDEV, FF_SHARD, D)
        dmix = _mm_nt_call(dx1, p["w_out"], "dmix")
        big_grads[l][1] = _mm_tn_call(s["mix"], dx1, "grad_w_out").reshape(NDEV, D // NDEV, D)
        if l == 0:
            got0, sums0 = _reduce_begin(big_grads[0][1:], "reduce_early0")
            early, early_got = big_grads[1] + big_grads[0][1:], list(got1) + list(got0)
            early_sums = [_pair_sum_call(g, t, ac, f"reduce_early1_pair{a}") for a, (g, t) in enumerate(zip(big_grads[1], got1))] + sums0
        (d_dn, d_ab, dcw, dalog, ddtb, ddn_g), early_from = _dn_bwd_call(
            s["p_dn"], s["p_ab"], dmix, s["dn_kept"], p["conv"], p["a_log"], p["dt_bias"], p["dn_g"], B, T,
            swap=early_sums if l == 0 else ())
        d_sb, dqg, dkg = _sb_bwd_call(s["p_sb"], dmix, p["sb_qg"], p["sb_kg"], B, T)
        d_sg, dsg_g, dsg_w, dsg_b = _sg_bwd_call(s["p_sg"], dmix, p["sg_g"], p["sg_w"], p["sg_bias"], B, T)
        dsections = (d_dn, d_sb, d_sg, d_ab)
        dy, dg1 = _in_proj_bwd_call(dsections, p["wt"], s["x0"], p["g1"], dx1)
        dwt = _in_proj_grad_call(dsections, s["h"])
        big_grads[l][0] = _row_perm_call(dwt, table_back, "unpack_grad_w_in").reshape(NDEV, IN_SHARD_PAD, D)
        for n, val in (("norm1_g", dg1[0]), ("conv_w", dcw.transpose(1, 0, 2).reshape(4, 3 * DN_W)), ("a_log", dalog[0, :NH]),
                       ("dt_bias", ddtb[0, :NH]), ("dn_out_g", ddn_g[0]), ("sb_q_g", dqg[0, :64]), ("sb_k_g", dkg[0, :64]),
                       ("sg_v_g", dsg_g[0]), ("sg_w", dsg_w), ("sg_b", dsg_b[:, :NH].T), ("norm2_g", dg2[0])):
            small_grads[n][l] = val
    grad_x = dy.reshape(B, T, D)

    last = big_grads[0][:1]
    last_got, last_sums = _reduce_begin(last, "reduce_last")
    mine0 = _reduce_end(last, last_got, _swap_chips_call(last_sums, "reduce_last_ici"), "reduce_last")
    mine1 = _reduce_end(early, early_got, early_from, "reduce_early")
    grads = {"w_in": jnp.stack([mine0[0][:IN_SHARD].T, mine1[0][:IN_SHARD].T]), "w_out": jnp.stack([mine1[4], mine1[1]]),
             "w_ff1": jnp.stack([mine1[5], mine1[2]]), "w_ff2": jnp.stack([mine1[6], mine1[3]])}
    small_flat = jnp.concatenate([jnp.stack(small_grads[n]).reshape(-1) for n in SMALL])
    everyone, = _all_gather_call([_to_rows(small_flat, 8)], "gather_small_grads")
    small_sum = _sum_call([(everyone, k) for k in range(NDEV)], F32, "sum_small_grads").reshape(-1)
    off = 0
    for n in SMALL:
        sz = 2 * _size(SMALL_SHAPE[n])
        grads[n] = small_sum[off:off + sz].reshape((2,) + SMALL_SHAPE[n])
        off += sz
    cshard = conv_w.shape[-1]
    grads["conv_w"] = lax.dynamic_slice_in_dim(grads["conv_w"], me * cshard, cshard, axis=2)

    deltas, new_m, new_v = {}, {}, {}
    for n in WEIGHTS:
        deltas[n], new_m[n], new_v[n] = _adamw_call(given[n], grads[n], mom[n], var[n], "adamw_" + n)
    return (loss, grad_x, *[grads[n] for n in WEIGHTS], *[deltas[n] for n in WEIGHTS], *[new_m[n] for n in WEIGHTS],
            *[new_v[n] for n in WEIGHTS])
```

```python
import functools

import numpy as np

import jax
import jax.numpy as jnp
from jax import lax
from jax.experimental import pallas as pl
from jax.experimental.pallas import tpu as pltpu

F32, BF16 = jnp.float32, jnp.bfloat16
EPS = 1e-6
LANES = 128
D = 1024
DFF = 4096
NH = 4
DN_W, SB_W, SG_W = 512, 256, 256
IN_DIM = 3336
NDEV = 8
IN_SHARD = IN_DIM // NDEV
IN_SHARD_PAD = 432
FF_SHARD = DFF // NDEV
DN_OFF, SB_OFF, SG_OFF, AB_OFF, NPACK = 0, 2048, 2816, 3328, 3456
SECTIONS = ((DN_OFF, 2048), (SB_OFF, 768), (SG_OFF, 512), (AB_OFF, 128))
SB_SCALE = 64 ** -0.5
DN_SCALE = 128 ** -0.5
VMEM_LIMIT = 56 * 1024 * 1024
VMEM_LIMIT_MAX = 62 * 1024 * 1024
ADAM_LR, ADAM_B1, ADAM_B2, ADAM_EPS, ADAM_WD, ADAM_STEP = 0.001, 0.9, 0.999, 1e-08, 0.01, 10
MESH = pl.DeviceIdType.MESH


def _iota(shape, dim):
    return lax.broadcasted_iota(jnp.int32, shape, dim)


def _params(**kw):
    return pltpu.CompilerParams(vmem_limit_bytes=VMEM_LIMIT, **kw)


NN, NT, TN = ((1,), (0,)), ((1,), (1,)), ((0,), (0,))


def _mm(a, b, dims):
    return lax.dot_general(a.astype(BF16), b.astype(BF16), (dims, ((), ())), preferred_element_type=F32)


def _plain(a, b, dims):
    return (a.T if dims == TN else a), (b.T if dims == NT else b)


def _mmx(a, b, dims):
    return _mm(*_plain(a, b, dims), NN)


@jax.custom_vjp
def _dot(a, b):
    return _mmx(a, b, NN)


def _dot_fwd(a, b):
    return _dot(a, b), (a, b)


def _dot_bwd(res, g):
    a, b = res
    return _mmx(g, b, NT).astype(a.dtype), _mmx(a, g, TN).astype(b.dtype)


_dot.defvjp(_dot_fwd, _dot_bwd)


@jax.custom_vjp
def _dot_nt(a, b):
    return _mmx(a, b, NT)


def _dot_nt_fwd(a, b):
    return _dot_nt(a, b), (a, b)


def _dot_nt_bwd(res, g):
    a, b = res
    return _mmx(g, b, NN).astype(a.dtype), _mmx(g, a, TN).astype(b.dtype)


_dot_nt.defvjp(_dot_nt_fwd, _dot_nt_bwd)


@jax.custom_vjp
def _dot_tn(a, b):
    return _mmx(a, b, TN)


def _dot_tn_fwd(a, b):
    return _dot_tn(a, b), (a, b)


def _dot_tn_bwd(res, g):
    a, b = res
    return _mmx(b, g, NT).astype(a.dtype), _mmx(a, g, NN).astype(b.dtype)


_dot_tn.defvjp(_dot_tn_fwd, _dot_tn_bwd)


def _split(x):
    hi = x.astype(BF16)
    return hi, (x - hi.astype(F32)).astype(BF16)


def _mm3(a, b, dims):
    a, b = _plain(a, b, dims)
    (ah, al), (bh, bl) = _split(a), _split(b)
    mm = lambda x, y: jnp.dot(x, y, preferred_element_type=F32)
    return mm(ah, bh) + (mm(ah, bl) + mm(al, bh))


def _mm_ones(ones, x, ones_left):
    hi, lo = _split(x)
    mm = (lambda t: jnp.dot(ones, t, preferred_element_type=F32)) if ones_left else \
         (lambda t: jnp.dot(t, ones, preferred_element_type=F32))
    return mm(hi) + mm(lo)


def _pair_ones(kind, transposed):
    row, col = _iota((128, 128), 0), _iota((128, 128), 1)
    m = (row // 64) == (col // 64)
    if kind == "running":
        m = jnp.logical_and(m, (col >= row) if transposed else (col <= row))
    return jnp.where(m, 1.0, 0.0).astype(BF16)


@functools.partial(jax.custom_vjp, nondiff_argnums=(0,))
def _chunk_sum(kind, x):
    return _mm_ones(_pair_ones(kind, False), x, True)


def _chunk_sum_fwd(kind, x):
    return _chunk_sum(kind, x), None


def _chunk_sum_bwd(kind, _, g):
    return (_mm_ones(_pair_ones(kind, True), g, True),)


_chunk_sum.defvjp(_chunk_sum_fwd, _chunk_sum_bwd)


def _tri_ones(n, transposed):
    row, col = _iota((n, n), 0), _iota((n, n), 1)
    return jnp.where((row < col) if transposed else (row > col), 1.0, 0.0).astype(BF16)


@jax.custom_vjp
def _suffix_sum(x):
    return _mm_ones(_tri_ones(x.shape[1], False), x, False)


def _suffix_sum_fwd(x):
    return _suffix_sum(x), None


def _suffix_sum_bwd(_, g):
    return (_mm_ones(_tri_ones(g.shape[1], True), g, False),)


_suffix_sum.defvjp(_suffix_sum_fwd, _suffix_sum_bwd)


def _sigmoid(x):
    return jax.nn.sigmoid(x)


def _silu(x):
    return x * _sigmoid(x)


def _softplus(x):
    return jnp.maximum(x, 0.0) + jnp.log1p(jnp.exp(-jnp.abs(x)))


def _gelu(x):
    return 0.5 * x * (1.0 + jnp.tanh(0.7978845608028654 * (x + 0.044715 * (x * x * x))))


def _rms(x, gain):
    return x * lax.rsqrt(jnp.mean(x * x, axis=-1, keepdims=True) + EPS) * gain


def _shift_down_impl(x, k):
    return jnp.where(_iota(x.shape, 0) >= k, pltpu.roll(x, k, 0), 0.0)


def _shift_up_impl(x, k):
    n = x.shape[0]
    return jnp.where(_iota(x.shape, 0) < n - k, pltpu.roll(x, n - k, 0), 0.0)


@functools.partial(jax.custom_vjp, nondiff_argnums=(1,))
def _shift_down(x, k):
    return _shift_down_impl(x, k)


def _shift_down_fwd(x, k):
    return _shift_down_impl(x, k), None


def _shift_down_bwd(k, _, g):
    return (_shift_up_impl(g, k),)


_shift_down.defvjp(_shift_down_fwd, _shift_down_bwd)


def _lane_pick(x, idx):
    return jnp.sum(jnp.where(_iota(x.shape, 1) == idx, x, 0.0), axis=-1, keepdims=True)


def _dn_conv(x, w0, w1, w2, w3, l2_scale):
    y = _silu(w3 * x + w2 * _shift_down(x, 1) + w1 * _shift_down(x, 2) + w0 * _shift_down(x, 3))
    if l2_scale is None:
        return y
    return y * lax.rsqrt(jnp.sum(y * y, axis=-1, keepdims=True) + EPS) * l2_scale


def _dn_gate(a, b, a_log, dt_bias):
    return -jnp.exp(a_log) * _softplus(a + dt_bias), _sigmoid(b)


def _same_head(shape):
    return (_iota(shape, 0) < LANES) == (_iota(shape, 1) < LANES)


def _bd(r2):
    return jnp.where(_same_head((2 * LANES, 2 * LANES)), jnp.concatenate([r2, r2], axis=0), 0.0)


def _bd_t(y2):
    t = y2.T
    return jnp.where(_same_head((2 * LANES, 2 * LANES)), jnp.concatenate([t, t], axis=1), 0.0)


def _pair_prod(kind, a2, b2, mm):
    if kind == NN:
        return mm(a2, _bd(b2))
    if kind == NT:
        return mm(a2, _bd_t(b2))
    full = mm(a2.T, b2)
    return jnp.concatenate([full[:LANES, :LANES], full[LANES:, LANES:]], axis=1)


_MM1 = lambda x, y: _mm(x, y, NN)
_MM3 = lambda x, y: _mm3(x, y, NN)


def _pair_vjp_rule(kind, a2, b2, g, mm):
    if kind == NN:
        return _pair_prod(NT, g, b2, mm), _pair_prod(TN, a2, g, mm)
    if kind == NT:
        return _pair_prod(NN, g, b2, mm), _pair_prod(TN, g, a2, mm)
    return _pair_prod(NT, b2, g, mm), _pair_prod(NN, a2, g, mm)


@functools.partial(jax.custom_vjp, nondiff_argnums=(0,))
def _pdot(kind, a2, b2):
    return _pair_prod(kind, a2, b2, _MM1)


def _pdot_fwd(kind, a2, b2):
    return _pdot(kind, a2, b2), (a2, b2)


def _pdot_bwd(kind, res, g):
    return _pair_vjp_rule(kind, *res, g, _MM1)


_pdot.defvjp(_pdot_fwd, _pdot_bwd)


def _unit_lower_inverse(lower):
    n = lower.shape[0]
    nk = -lower
    inv = jnp.where(_iota(lower.shape, 0) == jnp.bitwise_and(_iota(lower.shape, 1), n - 1), 1.0, 0.0) + nk
    for _ in range(5):
        nk = _pair_prod(NN, nk, nk, _MM1)
        inv = inv + _pair_prod(NN, inv, nk, _MM1)
    return inv


@jax.custom_vjp
def _solve_with(lower, inv, rhs):
    return _pair_prod(NN, inv, rhs, _MM3)


def _solve_with_fwd(lower, inv, rhs):
    x = _pair_prod(NN, inv, rhs, _MM3)
    return x, (inv, x)


def _solve_with_bwd(res, g):
    inv, x = res
    d_rhs = _pair_prod(TN, inv, g, _MM3)
    return -_pair_prod(NT, d_rhs, x, _MM3), jnp.zeros_like(inv), d_rhs


_solve_with.defvjp(_solve_with_fwd, _solve_with_bwd)


def _dn_local(q, k, v, g, beta, inv=None):
    shape = (LANES, 2 * LANES)
    row, col = _iota(shape, 0), jnp.bitwise_and(_iota(shape, 1), LANES - 1)
    same = (row // 64) == (col // 64)
    tri_incl = jnp.logical_and(same, col <= row)
    tri_strict = jnp.logical_and(same, col < row)
    first = row < 64
    gc = _chunk_sum("running", g)
    gl = _chunk_sum("total", g)
    diff = gc - jnp.concatenate([gc[:, :LANES].T, gc[:, LANES:].T], axis=1)
    decay = jnp.where(tri_incl, jnp.exp(jnp.where(tri_incl, diff, 0.0)), 0.0)
    egc = jnp.exp(gc)
    lower = jnp.where(tri_strict, beta * _pdot(NT, k, k) * decay, 0.0)
    if inv is None:
        inv = _unit_lower_inverse(lower)
    u_val = _solve_with(lower, inv, v * beta)
    w_dec = _solve_with(lower, inv, k * (beta * egc))
    qk = jnp.where(tri_incl, _pdot(NT, q, k) * decay, 0.0)
    q_dec = q * egc
    k_dec = k * jnp.exp(gl - gc)
    cd1 = jnp.exp(jnp.sum(jnp.where(first, g, 0.0), axis=0, keepdims=True))
    cd2 = jnp.exp(jnp.sum(jnp.where(first, 0.0, g), axis=0, keepdims=True))
    return (u_val, w_dec, qk, q_dec, k_dec, cd1, cd2), inv


def _dn_state(u_val, w_dec, qk, q_dec, k_dec, cd1, cd2, s0):
    first = _iota((LANES, 2 * LANES), 0) < 64
    u1 = u_val - _pdot(NN, w_dec, s0)
    s1 = s0 * cd1 + _pdot(TN, jnp.where(first, k_dec, 0.0), u1)
    u2 = u_val - _pdot(NN, w_dec, s1)
    u_new = jnp.where(first, u1, u2)
    s2 = s1 * cd2 + _pdot(TN, jnp.where(first, 0.0, k_dec), u_new)
    o = jnp.where(first, _pdot(NN, q_dec, s0), _pdot(NN, q_dec, s1)) + _pdot(NN, qk, u_new)
    return o, s2


def _dn_post(o, z, gain):
    return _rms(o, gain) * _silu(z)


def _dn_gate_in(ab_ref, alog_ref, dtb_ref, h):
    ab = ab_ref[...]
    return _lane_pick(ab, h), _lane_pick(ab, h + NH), _lane_pick(alog_ref[...], h), _lane_pick(dtb_ref[...], h)


_DN_L2 = (DN_SCALE, 1.0, None)
DN_HPS = 2
DN_BLK = 4 * DN_HPS * LANES
_DN_COLS = tuple(slice(i * LANES, (i + 1) * LANES) for i in range(DN_HPS))


def _dn_in_cols(s, i):
    return slice((s * DN_HPS + i) * LANES, (s * DN_HPS + i + 1) * LANES)


def _dn_taps(cw_ref, s, i):
    return tuple(cw_ref[t:t + 1, _dn_in_cols(s, i)] for t in range(4))


def _dn_pack_gate(vals):
    lane = _iota((1, LANES), 1)
    out = 0.0
    for i, (g, beta) in enumerate(vals):
        out = out + jnp.where(lane == 2 * i, g, 0.0) + jnp.where(lane == 2 * i + 1, beta, 0.0)
    return out


def _pair_rows(n):
    return pl.ds(pl.multiple_of(n * 128, 128), 128)


def _dn_gate_rows(gate):
    head_a = _iota((1, DN_HPS * LANES), 1) < LANES
    return (jnp.where(head_a, _lane_pick(gate, 0), _lane_pick(gate, 2)),
            jnp.where(head_a, _lane_pick(gate, 1), _lane_pick(gate, 3)))


def _dn_gate_cols(dg, db):
    head_a = _iota((1, DN_HPS * LANES), 1) < LANES
    fold = lambda t: (jnp.sum(jnp.where(head_a, t, 0.0), axis=-1, keepdims=True),
                      jnp.sum(jnp.where(head_a, 0.0, t), axis=-1, keepdims=True))
    (dg_a, dg_b), (db_a, db_b) = fold(dg), fold(db)
    return [(dg_a, db_a), (dg_b, db_b)]


def _dn_in_specs(T):
    one = pl.Buffered(1)
    vec = pl.BlockSpec((1, LANES), lambda b, h: (0, 0))
    return [pl.BlockSpec((T, DN_BLK), lambda b, h: (b, h), pipeline_mode=one),
            pl.BlockSpec((T, LANES), lambda b, h: (b, 0), pipeline_mode=one),
            pl.BlockSpec((4, 3 * DN_HPS * LANES), lambda b, h: (0, h)), vec, vec, vec]


def _dn_fwd_call(proj_dn, proj_ab, conv_w, a_log, dt_bias, gain, B, T, gather=()):
    npair = T // 128
    ng = len(gather)
    nsteps = B * (NH // DN_HPS)

    def body(*refs):
        x_ref, ab_ref, cw_ref, alog_ref, dtb_ref, gain_ref = refs[:6]
        out_ref, q_s, k_s, v_s, o_s, gate_s, st_s, inv_s = refs[6 + ng:14 + ng]
        step_id = pl.program_id(0) * (NH // DN_HPS) + pl.program_id(1)
        if ng:
            send, forward, finish = _gather_phases(refs[6:6 + ng], refs[14 + ng:14 + 2 * ng], *refs[14 + 2 * ng:])
            pl.when(step_id == 0)(send)
            pl.when(step_id == nsteps - 1)(forward)
        hp = pl.program_id(1)
        gates = []
        for i, cs in enumerate(_DN_COLS):
            for s, (x_s, l2) in enumerate(zip((q_s, k_s, v_s), _DN_L2)):
                x_s[:, cs] = _dn_conv(x_ref[:, _dn_in_cols(s, i)], *_dn_taps(cw_ref, s, i), l2)
            gates.append(_dn_gate(*_dn_gate_in(ab_ref, alog_ref, dtb_ref, DN_HPS * hp + i)))
        gate_s[...] = _dn_pack_gate(gates)

        def local_of(pair):
            r = _pair_rows(pair)
            loc, inv = _dn_local(q_s[r, :], k_s[r, :], v_s[r, :], *_dn_gate_rows(gate_s[r, :]))
            inv_s[0, 0, pair] = inv
            return loc

        def state_of(n, loc, state):
            st_s[0, 0, n] = state
            o, s2 = _dn_state(*loc, state)
            o_s[_pair_rows(n), :] = o
            return s2

        def step(n, carry):
            loc, state = carry
            return local_of(n + 1), state_of(n, loc, state)

        loc, state = lax.fori_loop(0, npair - 1, step, (local_of(0), jnp.zeros((LANES, DN_HPS * LANES), F32)))
        state_of(npair - 1, loc, state)
        for i, cs in enumerate(_DN_COLS):
            out_ref[:, cs] = _dn_post(o_s[:, cs], x_ref[:, _dn_in_cols(3, i)], gain_ref[...])
        if ng:
            pl.when(step_id == nsteps - 1)(finish)

    kept_specs, kept_shapes = _dn_kept(B, T)
    outs = pl.pallas_call(
        body, name="dn_fwd", grid=(B, NH // DN_HPS), in_specs=_dn_in_specs(T) + _any_specs(ng),
        out_specs=[pl.BlockSpec((T, DN_HPS * LANES), lambda b, h: (b, h), pipeline_mode=pl.Buffered(1))] + kept_specs + _any_specs(ng),
        out_shape=[jax.ShapeDtypeStruct((B * T, D), F32)] + kept_shapes + _gather_shapes(gather),
        scratch_shapes=_gather_sems(ng) if ng else [],
        compiler_params=_params(dimension_semantics=("arbitrary", "arbitrary")),
    )(proj_dn, proj_ab, conv_w, a_log, dt_bias, gain, *gather)
    return outs[0], outs[1:8], outs[8:]


def _dn_kept(B, T):
    one = pl.Buffered(1)
    npair, pairs = T // 128, NH // DN_HPS
    wide = pl.BlockSpec((T, DN_HPS * LANES), lambda b, h: (b, h), pipeline_mode=one)
    per_pair = pl.BlockSpec((1, 1, npair, LANES, DN_HPS * LANES), lambda b, h: (b, h, 0, 0, 0), pipeline_mode=one)
    specs = [wide] * 4 + [pl.BlockSpec((T, LANES), lambda b, h: (b, h), pipeline_mode=one)] + [per_pair] * 2
    shapes = ([jax.ShapeDtypeStruct((B * T, DN_W), F32)] * 4 + [jax.ShapeDtypeStruct((B * T, pairs * LANES), F32)]
              + [jax.ShapeDtypeStruct((B, pairs, npair, LANES, DN_HPS * LANES), F32)] * 2)
    return specs, shapes


def _dn_bwd_call(proj_dn, proj_ab, dmix, kept, conv_w, a_log, dt_bias, gain, B, T, swap=()):
    npair = T // 128
    ns = len(swap)
    nsteps = B * (NH // DN_HPS)

    def body(*refs):
        x_ref, ab_ref, cw_ref, alog_ref, dtb_ref, gain_ref, do_ref, q_s, k_s, v_s, o_ref, gate_s, st_s, inv_s = refs[:14]
        dx_ref, dab_ref, dcw_ref, dalog_ref, ddtb_ref, dgain_ref = refs[14 + ns:20 + ns]
        dgate_s, do_s = refs[20 + 2 * ns:22 + 2 * ns]
        b_i, hp = pl.program_id(0), pl.program_id(1)
        step_id = b_i * (NH // DN_HPS) + hp
        if ns:
            send, finish = _chip_swap_phases(refs[14:14 + ns], refs[20 + ns:20 + 2 * ns], *refs[22 + 2 * ns:])
            pl.when(step_id == 0)(send)

        def pair_in(r):
            return (q_s[r, :], k_s[r, :], v_s[r, :]) + _dn_gate_rows(gate_s[r, :])

        zero_state = jnp.zeros((LANES, DN_HPS * LANES), F32)

        @pl.when(jnp.logical_and(b_i == 0, hp == 0))
        def _():
            dcw_ref[...] = jnp.zeros_like(dcw_ref)
            dalog_ref[...] = jnp.zeros_like(dalog_ref)
            ddtb_ref[...] = jnp.zeros_like(ddtb_ref)
            dgain_ref[...] = jnp.zeros_like(dgain_ref)

        for i, cs in enumerate(_DN_COLS):
            zc = _dn_in_cols(3, i)
            _, post_vjp = jax.vjp(_dn_post, o_ref[:, cs], x_ref[:, zc], gain_ref[...])
            do, dz, dgain = post_vjp(do_ref[:, cs])
            dx_ref[:, zc] = dz
            do_s[:, cs] = do
            dgain_ref[...] += dgain

        wide_cols = lambda s: slice(s * DN_HPS * LANES, (s + 1) * DN_HPS * LANES)

        def back_step(nn, dstate):
            n = npair - 1 - nn
            r = _pair_rows(n)
            inv = inv_s[0, 0, n]
            local = lambda q, k, v, g, beta, inv=inv: _dn_local(q, k, v, g, beta, inv)[0]
            loc, local_vjp = jax.vjp(local, *pair_in(r))
            _, state_vjp = jax.vjp(_dn_state, *loc, st_s[0, 0, n])
            *dloc, ds0 = state_vjp((do_s[r, :], dstate))
            dq, dk, dv, dg, db = local_vjp(tuple(dloc))
            dx_ref[r, wide_cols(0)], dx_ref[r, wide_cols(1)], dx_ref[r, wide_cols(2)] = dq, dk, dv
            dgate_s[r, :] = _dn_pack_gate(_dn_gate_cols(dg, db))
            return ds0

        lax.fori_loop(0, npair, back_step, zero_state)

        lane = _iota((1, LANES), 1)
        dab = 0.0
        for i, cs in enumerate(_DN_COLS):
            h = DN_HPS * hp + i
            for s, l2 in enumerate(_DN_L2):
                xc = _dn_in_cols(s, i)
                _, conv_vjp = jax.vjp(functools.partial(_dn_conv, l2_scale=l2), x_ref[:, xc], *_dn_taps(cw_ref, s, i))
                dx, *dw = conv_vjp(dx_ref[:, xc])
                dx_ref[:, xc] = dx
                for t in range(4):
                    dcw_ref[h + 4 * s, t:t + 1, :] += dw[t]
            _, gate_vjp = jax.vjp(_dn_gate, *_dn_gate_in(ab_ref, alog_ref, dtb_ref, h))
            dgate = dgate_s[...]
            da, db, dalog, ddtb = gate_vjp((_lane_pick(dgate, 2 * i), _lane_pick(dgate, 2 * i + 1)))
            dab = dab + jnp.where(lane == h, da, 0.0) + jnp.where(lane == h + NH, db, 0.0)
            dalog_ref[...] += jnp.where(lane == h, dalog, 0.0)
            ddtb_ref[...] += jnp.where(lane == h, ddtb, 0.0)

        @pl.when(hp == 0)
        def _():
            dab_ref[...] = jnp.zeros_like(dab_ref)

        dab_ref[...] += dab
        if ns:
            pl.when(step_id == nsteps - 1)(finish)

    M = B * T
    one = pl.Buffered(1)
    vec = pl.BlockSpec((1, LANES), lambda b, h: (0, 0))
    wide = [pltpu.VMEM((T, DN_HPS * LANES), F32)]
    vec_shape = jax.ShapeDtypeStruct((1, LANES), F32)
    outs = pl.pallas_call(
        body, name="dn_bwd", grid=(B, NH // DN_HPS),
        in_specs=_dn_in_specs(T) + [pl.BlockSpec((T, DN_HPS * LANES), lambda b, h: (b, h), pipeline_mode=one)] + _dn_kept(B, T)[0]
        + _any_specs(ns),
        out_specs=[pl.BlockSpec((T, DN_BLK), lambda b, h: (b, h), pipeline_mode=one), pl.BlockSpec((T, LANES), lambda b, h: (b, 0)),
                   pl.BlockSpec((12, 4, LANES), lambda b, h: (0, 0, 0)), vec, vec, vec] + _any_specs(ns),
        out_shape=[jax.ShapeDtypeStruct((M, 4 * DN_W), F32), jax.ShapeDtypeStruct((M, LANES), F32),
                   jax.ShapeDtypeStruct((12, 4, LANES), F32), vec_shape, vec_shape, vec_shape] + _chip_swap_shapes(swap),
        scratch_shapes=[pltpu.VMEM((T, LANES), F32)] + wide + (_chip_swap_sems(ns) if ns else []),
        compiler_params=pltpu.CompilerParams(vmem_limit_bytes=VMEM_LIMIT_MAX, dimension_semantics=("arbitrary", "arbitrary")),
    )(proj_dn, proj_ab, conv_w, a_log, dt_bias, gain, dmix, *kept, *swap)
    return outs[:6], outs[6:]


SBQ = 256


def _group_rms(x, gain):
    first = _iota(x.shape, 1) < 64
    sq = x * x
    ss_a = jnp.sum(jnp.where(first, sq, 0.0), axis=-1, keepdims=True)
    ss_b = jnp.sum(jnp.where(first, 0.0, sq), axis=-1, keepdims=True)
    ms = jnp.where(first, ss_a, ss_b) * (1.0 / 64)
    return x * lax.rsqrt(ms + EPS) * gain


def _sb_stack(q):
    first = _iota((1, LANES), 1) < 64
    return jnp.concatenate([jnp.where(first, q, 0.0), jnp.where(first, 0.0, q)], axis=0)


def _sb_fold(acc):
    return jnp.where(_iota((1, LANES), 1) < 64, acc[:SBQ], acc[SBQ:])


def _sb_logs(q2, k, diag):
    n = SBQ
    z = _mm(q2, k, ((1,), (1,))) * SB_SCALE
    ls_pos = jnp.minimum(z, 0.0) - jnp.log(1.0 + jnp.exp(-jnp.abs(z)))
    l1m = ls_pos - z
    if not diag:
        return ls_pos, l1m, None
    mask = _iota((2 * n, n), 1) < jnp.bitwise_and(_iota((2 * n, n), 0), n - 1)
    return ls_pos, jnp.where(mask, l1m, 0.0), mask


def _sb_weights(ls_pos, l1m, mask, carry):
    w = jnp.exp(ls_pos + (_mm_ones(_tri_ones(SBQ, False), l1m, False) + carry))
    return w if mask is None else jnp.where(mask, w, 0.0)


def _sb_block(q, k, v, carry, diag):
    ls_pos, l1m, mask = _sb_logs(_sb_stack(q), k, diag)
    w = _sb_weights(ls_pos, l1m, mask, carry)
    return _mm(w, v, ((1,), (0,))), carry + jnp.sum(l1m, axis=-1, keepdims=True), _sb_sum_as_rows(l1m)


SB_ROWS = 16


def _sb_sum_as_rows(l1m):
    ones = jnp.ones((SB_ROWS, SBQ), BF16)
    hi, lo = _split(l1m)
    mm = lambda t: lax.dot_general(ones, t, (NT, ((), ())), preferred_element_type=F32)
    return mm(hi) + mm(lo)


def _sb_rows_as_column(rows):
    pick = jnp.where(_iota((SB_ROWS, SBQ), 0) == 0, 1.0, 0.0).astype(BF16)
    hi = rows.astype(BF16)
    rest = rows - hi.astype(F32)
    mid = rest.astype(BF16)
    lo = (rest - mid.astype(F32)).astype(BF16)
    mm = lambda t: lax.dot_general(t, pick, (TN, ((), ())), preferred_element_type=F32)
    return mm(hi) + (mm(mid) + mm(lo))


def _sb_block_bwd(q, k, v, carry, diag, dpv, dcarry):
    q2 = _sb_stack(q)
    ls_pos, l1m, mask = _sb_logs(q2, k, diag)
    w = _sb_weights(ls_pos, l1m, mask, carry)
    dv = _mm(w, dpv, ((0,), (0,)))
    de = _mm(dpv, v, ((1,), (1,))) * w
    dl1m = _mm_ones(_tri_ones(SBQ, True), de, False) + dcarry
    if mask is not None:
        dl1m = jnp.where(mask, dl1m, 0.0)
    sig = jnp.exp(ls_pos)
    dz = (de * (1.0 - sig) - dl1m * sig) * SB_SCALE
    dq = _sb_fold(_mm(dz, k, ((1,), (0,))))
    return dq, _mm(dz, q2, ((0,), (0,))), dv, dcarry + jnp.sum(de, axis=-1, keepdims=True)


_SB_Q, _SB_K, _SB_V = (slice(i * LANES, (i + 1) * LANES) for i in range(3))


def _sb_fwd_call(proj_sb, mix, q_gain, k_gain, B, T, gather=()):
    nblk = T // SBQ
    ng = len(gather)
    nsteps = 2 * B

    def body(*refs):
        x_ref, qg_ref, kg_ref = refs[:3]
        out_ref, carry_ref = refs[4 + ng:6 + ng]
        q_s, k_s = refs[6 + 2 * ng:8 + 2 * ng]
        step_id = 2 * pl.program_id(0) + pl.program_id(1)
        if ng:
            send, forward, finish = _gather_phases(refs[4:4 + ng], refs[6 + ng:6 + 2 * ng], *refs[8 + 2 * ng:])
            pl.when(step_id == 0)(send)
            pl.when(step_id == nsteps - 1)(forward)
        q_s[...] = _group_rms(x_ref[:, _SB_Q], qg_ref[...])
        k_s[...] = _group_rms(x_ref[:, _SB_K], kg_ref[...])

        def qblock(i, _):
            ri = pl.ds(pl.multiple_of(i * SBQ, SBQ), SBQ)
            q = q_s[ri, :]

            def kblock(jj, c):
                j = i - 1 - jj
                rj = pl.ds(pl.multiple_of(j * SBQ, SBQ), SBQ)
                carry_ref[0, 0, i, j] = c[2]
                pv, carry, rows = _sb_block(q, k_s[rj, :], x_ref[rj, _SB_V], c[1], False)
                return c[0] + pv, carry, c[2] + rows

            on_diag = _sb_block(q, k_s[ri, :], x_ref[ri, _SB_V], jnp.zeros((2 * SBQ, 1), F32), True)
            acc, _c, _r = lax.fori_loop(0, i, kblock, on_diag)
            out_ref[ri, :] = _sb_fold(acc)
            return 0

        lax.fori_loop(0, nblk, qblock, 0)
        if ng:
            pl.when(step_id == nsteps - 1)(finish)

    vec = pl.BlockSpec((1, LANES), lambda b, p: (0, 0))
    outs = pl.pallas_call(
        body, name="sb_fwd", grid=(B, 2),
        in_specs=[pl.BlockSpec((T, 3 * LANES), lambda b, p: (b, p)), vec, vec, pl.BlockSpec(memory_space=pl.ANY)] + _any_specs(ng),
        out_specs=[pl.BlockSpec((T, LANES), lambda b, p: (b, DN_W // LANES + p)), _sb_carry_spec(nblk)] + _any_specs(ng),
        out_shape=[jax.ShapeDtypeStruct((B * T, D), F32), jax.ShapeDtypeStruct((B, 2, nblk, nblk, SB_ROWS, 2 * SBQ), F32)]
        + _gather_shapes(gather), input_output_aliases={3: 0},
        scratch_shapes=[pltpu.VMEM((T, LANES), F32)] * 2 + (_gather_sems(ng) if ng else []),
        compiler_params=_params(dimension_semantics=("arbitrary", "arbitrary")),
    )(proj_sb, q_gain, k_gain, mix, *gather)
    return outs[0], outs[1], outs[2:]


def _sb_carry_spec(nblk):
    return pl.BlockSpec((1, 1, nblk, nblk, SB_ROWS, 2 * SBQ), lambda b, p: (b, p, 0, 0, 0, 0))


def _sb_bwd_call(proj_sb, dmix, carries, q_gain, k_gain, B, T):
    nblk = T // SBQ

    def body(x_ref, qg_ref, kg_ref, do_ref, carry_ref, dx_ref, dqg_ref, dkg_ref, q_s, k_s, dq_s, dk_s, dv_s):
        b_i, p = pl.program_id(0), pl.program_id(1)
        qn, q_vjp = jax.vjp(_group_rms, x_ref[:, _SB_Q], qg_ref[...])
        kn, k_vjp = jax.vjp(_group_rms, x_ref[:, _SB_K], kg_ref[...])
        q_s[...], k_s[...] = qn, kn
        dk_s[...] = jnp.zeros_like(dk_s)
        dv_s[...] = jnp.zeros_like(dv_s)

        def qblock(i, _):
            ri = pl.ds(pl.multiple_of(i * SBQ, SBQ), SBQ)
            q = q_s[ri, :]
            dacc = _sb_stack(do_ref[ri, :])

            def kblock(j, c):
                rj = pl.ds(pl.multiple_of(j * SBQ, SBQ), SBQ)
                carry = _sb_rows_as_column(carry_ref[0, 0, i, j])
                dq_j, dk_j, dv_j, dc = _sb_block_bwd(q, k_s[rj, :], x_ref[rj, _SB_V], carry, False, dacc, c[1])
                dk_s[rj, :] += dk_j
                dv_s[rj, :] += dv_j
                return c[0] + dq_j, dc

            dq, dc = lax.fori_loop(0, i, kblock, (jnp.zeros((SBQ, LANES), F32), jnp.zeros((2 * SBQ, 1), F32)))
            dq_i, dk_i, dv_i, _dc = _sb_block_bwd(q, k_s[ri, :], x_ref[ri, _SB_V], jnp.zeros((2 * SBQ, 1), F32), True, dacc, dc)
            dk_s[ri, :] += dk_i
            dv_s[ri, :] += dv_i
            dq_s[ri, :] = dq + dq_i
            return 0

        lax.fori_loop(0, nblk, qblock, 0)
        dq_in, dqg = q_vjp(dq_s[...])
        dk_in, dkg = k_vjp(dk_s[...])
        dx_ref[:, _SB_Q], dx_ref[:, _SB_K], dx_ref[:, _SB_V] = dq_in, dk_in, dv_s[...]

        @pl.when(jnp.logical_and(b_i == 0, p == 0))
        def _():
            dqg_ref[...] = jnp.zeros_like(dqg_ref)
            dkg_ref[...] = jnp.zeros_like(dkg_ref)

        dqg_ref[...] += dqg + pltpu.roll(dqg, 64, 1)
        dkg_ref[...] += dkg + pltpu.roll(dkg, 64, 1)

    M = B * T
    vec = pl.BlockSpec((1, LANES), lambda b, p: (0, 0))
    blk = pl.BlockSpec((T, 3 * LANES), lambda b, p: (b, p))
    big = [pltpu.VMEM((T, LANES), F32)]
    return pl.pallas_call(
        body, name="sb_bwd", grid=(B, 2),
        in_specs=[blk, vec, vec, pl.BlockSpec((T, LANES), lambda b, p: (b, DN_W // LANES + p)), _sb_carry_spec(nblk)],
        out_specs=[blk, vec, vec],
        out_shape=[jax.ShapeDtypeStruct((M, 3 * SB_W), F32)] + [jax.ShapeDtypeStruct((1, LANES), F32)] * 2,
        scratch_shapes=big * 5,
        compiler_params=_params(dimension_semantics=("arbitrary", "arbitrary")),
    )(proj_sb, q_gain, k_gain, dmix, carries)


def _sg_chunk(u, v, gain, w_a, w_b, bias):
    n = 128
    row, col = _iota((n, n), 0), _iota((n, n), 1)
    first = _iota((1, LANES), 1) < 64
    vn = _group_rms(_gelu(v), gain)
    tril = col <= row
    mixed = jnp.where(first, _dot(jnp.where(tril, w_a, 0.0), vn), _dot(jnp.where(tril, w_b, 0.0), vn)) + bias
    return _gelu(u) * mixed


_SG_U, _SG_V = slice(0, LANES), slice(LANES, 2 * LANES)


def _sg_fwd_call(proj_sg, mix, gain, sg_w, bias, B, T):
    nchunk = T // 128

    def body(x_ref, g_ref, wa_ref, wb_ref, bias_ref, mix_ref, out_ref):
        del mix_ref

        def step(i, _):
            r = pl.ds(pl.multiple_of(i * 128, 128), 128)
            out_ref[r, :] = _sg_chunk(x_ref[r, _SG_U], x_ref[r, _SG_V], g_ref[...], wa_ref[0], wb_ref[0], bias_ref[...])
            return 0

        lax.fori_loop(0, nchunk, step, 0)

    return pl.pallas_call(
        body, name="sg_fwd", grid=(B, 2),
        in_specs=[pl.BlockSpec((T, 2 * LANES), lambda b, p: (b, p)), pl.BlockSpec((1, LANES), lambda b, p: (0, p)),
                  pl.BlockSpec((1, 128, 128), lambda b, p: (2 * p, 0, 0)), pl.BlockSpec((1, 128, 128), lambda b, p: (2 * p + 1, 0, 0)),
                  pl.BlockSpec((128, LANES), lambda b, p: (0, p)), pl.BlockSpec(memory_space=pl.ANY)],
        out_specs=pl.BlockSpec((T, LANES), lambda b, p: (b, (DN_W + SB_W) // LANES + p)),
        out_shape=jax.ShapeDtypeStruct((B * T, D), F32), input_output_aliases={5: 0},
        compiler_params=_params(dimension_semantics=("arbitrary", "arbitrary")),
    )(proj_sg, gain, sg_w, sg_w, bias, mix)


def _sg_bwd_call(proj_sg, dmix, gain, sg_w, bias, B, T):
    nchunk = T // 128

    def body(x_ref, g_ref, wa_ref, wb_ref, bias_ref, do_ref, dx_ref, dg_ref, dw_ref, db_ref):
        p, b_i = pl.program_id(0), pl.program_id(1)

        def step(i, c):
            r = pl.ds(pl.multiple_of(i * 128, 128), 128)
            _, vjp = jax.vjp(_sg_chunk, x_ref[r, _SG_U], x_ref[r, _SG_V], g_ref[...], wa_ref[0], wb_ref[0], bias_ref[...])
            du, dv, dg, dwa, dwb, dbias = vjp(do_ref[r, :])
            dx_ref[r, _SG_U], dx_ref[r, _SG_V] = du, dv
            return c[0] + dg, c[1] + dwa, c[2] + dwb, c[3] + dbias

        z = jnp.zeros((128, 128), F32)
        dg, dwa, dwb, dbias = lax.fori_loop(0, nchunk, step, (jnp.zeros((1, LANES), F32), z, z, z))
        lane = _iota((1, LANES), 1)
        first = lane < 64
        s_a = jnp.sum(jnp.where(first, dbias, 0.0), axis=-1, keepdims=True)
        s_b = jnp.sum(jnp.where(first, 0.0, dbias), axis=-1, keepdims=True)
        dbg = jnp.where(lane == 2 * p, s_a, 0.0) + jnp.where(lane == 2 * p + 1, s_b, 0.0)

        @pl.when(b_i == 0)
        def _():
            dg_ref[...] = jnp.zeros_like(dg_ref)
            dw_ref[...] = jnp.zeros_like(dw_ref)

        @pl.when(jnp.logical_and(b_i == 0, p == 0))
        def _():
            db_ref[...] = jnp.zeros_like(db_ref)

        dg_ref[...] += dg
        dw_ref[0] += dwa
        dw_ref[1] += dwb
        db_ref[...] += dbg

    M = B * T
    blk = pl.BlockSpec((T, 2 * LANES), lambda p, b: (b, p))
    return pl.pallas_call(
        body, name="sg_bwd", grid=(2, B),
        in_specs=[blk, pl.BlockSpec((1, LANES), lambda p, b: (0, p)),
                  pl.BlockSpec((1, 128, 128), lambda p, b: (2 * p, 0, 0)), pl.BlockSpec((1, 128, 128), lambda p, b: (2 * p + 1, 0, 0)),
                  pl.BlockSpec((128, LANES), lambda p, b: (0, p)),
                  pl.BlockSpec((T, LANES), lambda p, b: (b, (DN_W + SB_W) // LANES + p))],
        out_specs=[blk, pl.BlockSpec((1, LANES), lambda p, b: (0, p)), pl.BlockSpec((2, 128, 128), lambda p, b: (p, 0, 0)),
                   pl.BlockSpec((128, LANES), lambda p, b: (0, 0))],
        out_shape=[jax.ShapeDtypeStruct((M, 2 * SG_W), F32), jax.ShapeDtypeStruct((1, SG_W), F32),
                   jax.ShapeDtypeStruct((4, 128, 128), F32), jax.ShapeDtypeStruct((128, LANES), F32)],
        compiler_params=_params(dimension_semantics=("arbitrary", "arbitrary")),
    )(proj_sg, gain, sg_w, sg_w, bias, dmix)


def _row_tile(m, most=512):
    return min(m, most)


def _in_proj_call(x, gain, wt):
    m = x.shape[0]
    tm = _row_tile(m)

    def body(x_ref, g_ref, wt_ref, *out_refs):
        h = _rms(x_ref[...], g_ref[...]).astype(BF16)
        out_refs[-1][...] = h
        for (off, width), out_ref in zip(SECTIONS, out_refs):
            out_ref[...] = lax.dot_general(h, wt_ref[off:off + width, :], (((1,), (1,)), ((), ())), preferred_element_type=F32)

    rows = lambda width: pl.BlockSpec((tm, width), lambda i: (i, 0))
    return pl.pallas_call(
        body, name="in_proj", grid=(m // tm,),
        in_specs=[rows(D), pl.BlockSpec((1, D), lambda i: (0, 0)),
                  pl.BlockSpec((NPACK, D), lambda i: (0, 0), pipeline_mode=pl.Buffered(1))],
        out_specs=[rows(w) for _, w in SECTIONS] + [rows(D)],
        out_shape=[jax.ShapeDtypeStruct((m, w), F32) for _, w in SECTIONS] + [jax.ShapeDtypeStruct((m, D), BF16)],
        compiler_params=_params(dimension_semantics=("arbitrary",)),
    )(x, gain, wt)


def _in_proj_bwd_call(dsections, wt, x, gain, dres):
    m = x.shape[0]
    tm = _row_tile(m)

    def body(*refs):
        ds_refs, (wt_ref, x_ref, g_ref, dres_ref, dx_ref, dg_ref) = refs[:len(SECTIONS)], refs[len(SECTIONS):]

        @pl.when(pl.program_id(0) == 0)
        def _():
            dg_ref[...] = jnp.zeros_like(dg_ref)

        dh = 0.0
        for (off, width), ds_ref in zip(SECTIONS, ds_refs):
            dh = dh + jnp.dot(ds_ref[...].astype(BF16), wt_ref[off:off + width, :], preferred_element_type=F32)
        _, vjp = jax.vjp(_rms, x_ref[...], g_ref[...])
        dx, dg = vjp(dh)
        dx_ref[...] = dres_ref[...] + dx
        dg_ref[...] += dg

    rows = lambda width: pl.BlockSpec((tm, width), lambda i: (i, 0))
    return pl.pallas_call(
        body, name="in_proj_bwd", grid=(m // tm,),
        in_specs=[rows(w) for _, w in SECTIONS] + [pl.BlockSpec((NPACK, D), lambda i: (0, 0), pipeline_mode=pl.Buffered(1)),
                                                   rows(D), pl.BlockSpec((1, D), lambda i: (0, 0)), rows(D)],
        out_specs=[rows(D), pl.BlockSpec((1, D), lambda i: (0, 0))],
        out_shape=[jax.ShapeDtypeStruct((m, D), F32), jax.ShapeDtypeStruct((1, D), F32)],
        compiler_params=_params(dimension_semantics=("arbitrary",)),
    )(*dsections, wt, x, gain, dres)


def _in_proj_grad_call(dsections, h):
    m = h.shape[0]
    tm = min(m, 256)

    def body(*refs):
        ds_refs, (h_ref, out_ref) = refs[:len(SECTIONS)], refs[len(SECTIONS):]

        @pl.when(pl.program_id(0) == 0)
        def _():
            out_ref[...] = jnp.zeros_like(out_ref)

        for (off, width), ds_ref in zip(SECTIONS, ds_refs):
            out_ref[off:off + width, :] += lax.dot_general(ds_ref[...].astype(BF16), h_ref[...], (((0,), (0,)), ((), ())),
                                                           preferred_element_type=F32)

    rows = lambda width: pl.BlockSpec((tm, width), lambda i: (i, 0))
    return pl.pallas_call(
        body, name="grad_w_in", grid=(m // tm,),
        in_specs=[rows(w) for _, w in SECTIONS] + [rows(D)],
        out_specs=pl.BlockSpec((NPACK, D), lambda i: (0, 0), pipeline_mode=pl.Buffered(1)),
        out_shape=jax.ShapeDtypeStruct((NPACK, D), F32),
        compiler_params=_params(dimension_semantics=("arbitrary",)),
    )(*dsections, h)


def _packed_column_of():
    t = np.full(NPACK, -1, np.int64)
    lanes = np.arange(LANES)
    for pair in range(2):
        for s in range(4):
            t[DN_OFF + pair * 1024 + s * 256 + np.arange(256)] = s * DN_W + pair * 256 + np.arange(256)
        for s in range(3):
            t[SB_OFF + pair * 384 + s * LANES + lanes] = 2056 + s * SB_W + pair * LANES + lanes
        for s in range(2):
            t[SG_OFF + pair * 256 + s * LANES + lanes] = 2056 + 3 * SB_W + s * SG_W + pair * LANES + lanes
    t[AB_OFF + np.arange(2 * NH)] = 4 * DN_W + np.arange(2 * NH)
    return t


def _row_tables():
    col = _packed_column_of()
    fwd = np.where(col >= 0, (col // IN_SHARD) * IN_SHARD_PAD + col % IN_SHARD, -1)
    packed_of = np.full(IN_DIM, -1, np.int64)
    packed_of[col[col >= 0]] = np.nonzero(col >= 0)[0]
    r = np.arange(NDEV * IN_SHARD_PAD)
    inside = r % IN_SHARD_PAD < IN_SHARD
    back = np.where(inside, packed_of[np.minimum((r // IN_SHARD_PAD) * IN_SHARD + r % IN_SHARD_PAD, IN_DIM - 1)], -1)
    return fwd, back


def _row_perm_call(src, table, name):
    n_out = table.shape[0]
    touched = [sorted(set((table[b * 128:(b + 1) * 128][table[b * 128:(b + 1) * 128] >= 0] // 128).tolist()))
               for b in range(n_out // 128)]

    def body(tbl_ref, src_ref, out_ref):
        lane = _iota((1, LANES), 1)
        for b, blocks in enumerate(touched):
            want = tbl_ref[b * 128:(b + 1) * 128, :]
            acc = jnp.zeros((128, D), F32)
            for sb in blocks:
                pick = jnp.where(want == sb * 128 + lane, 1.0, 0.0).astype(BF16)
                acc = acc + jnp.dot(pick, src_ref[sb * 128:(sb + 1) * 128, :].astype(BF16), preferred_element_type=F32)
            out_ref[b * 128:(b + 1) * 128, :] = acc.astype(BF16)

    return pl.pallas_call(
        body, name=name, out_shape=jax.ShapeDtypeStruct((n_out, D), BF16),
        in_specs=[pl.BlockSpec(memory_space=pltpu.VMEM)] * 2, out_specs=pl.BlockSpec(memory_space=pltpu.VMEM),
        compiler_params=_params(),
    )(jnp.asarray(table.reshape(-1, 1), jnp.int32), src)


def _out_proj_call(a, w, res):
    m, k = a.shape
    n = w.shape[1]
    tm = _row_tile(m)

    def body(a_ref, w_ref, res_ref, out_ref):
        out_ref[...] = res_ref[...] + jnp.dot(a_ref[...].astype(BF16), w_ref[...], preferred_element_type=F32)

    return pl.pallas_call(
        body, name="out_proj", grid=(m // tm,),
        in_specs=[pl.BlockSpec((tm, k), lambda i: (i, 0)), pl.BlockSpec((k, n), lambda i: (0, 0)),
                  pl.BlockSpec((tm, n), lambda i: (i, 0))],
        out_specs=pl.BlockSpec((tm, n), lambda i: (i, 0)),
        out_shape=jax.ShapeDtypeStruct((m, n), F32),
        compiler_params=_params(dimension_semantics=("arbitrary",)),
    )(a, w, res)


def _ffn_specs(tm):
    return [pl.BlockSpec((1, D, FF_SHARD), lambda i, j: (j, 0, 0)), pl.BlockSpec((FF_SHARD, D), lambda i, j: (j, 0))]


def _ffn_fwd_call(x, gain, w1, w2):
    m = x.shape[0]
    tm = _row_tile(m, 1024)

    def body(x_ref, g_ref, w1_ref, w2_ref, out_ref, h_s, acc_s):
        j = pl.program_id(1)

        @pl.when(j == 0)
        def _():
            h_s[...] = _rms(x_ref[...], g_ref[...]).astype(BF16)
            acc_s[...] = jnp.zeros_like(acc_s)

        a = jnp.maximum(jnp.dot(h_s[...], w1_ref[0], preferred_element_type=F32), 0.0)
        acc_s[...] += jnp.dot((a * a).astype(BF16), w2_ref[...], preferred_element_type=F32)

        @pl.when(j == NDEV - 1)
        def _():
            out_ref[...] = x_ref[...] + acc_s[...]

    return pl.pallas_call(
        body, name="ffn_fwd", grid=(m // tm, NDEV),
        in_specs=[pl.BlockSpec((tm, D), lambda i, j: (i, 0)), pl.BlockSpec((1, D), lambda i, j: (0, 0))] + _ffn_specs(tm),
        out_specs=pl.BlockSpec((tm, D), lambda i, j: (i, 0)),
        out_shape=jax.ShapeDtypeStruct((m, D), F32),
        scratch_shapes=[pltpu.VMEM((tm, D), BF16), pltpu.VMEM((tm, D), F32)],
        compiler_params=_params(dimension_semantics=("arbitrary", "arbitrary")),
    )(x, gain, w1, w2)


def _ffn_bwd_call(x, dy, gain, w1, w2, swap=()):
    m = x.shape[0]
    tm = _row_tile(m, 1024)
    ns = len(swap)

    def body(*refs):
        x_ref, dy_ref, g_ref, w1_ref, w2_ref = refs[:5]
        dx_ref, da_ref, r_ref, h_ref, dg_ref = refs[5 + ns:10 + ns]
        acc_s = refs[10 + 2 * ns]
        i, j = pl.program_id(0), pl.program_id(1)
        if ns:
            send, finish = _sibling_swap_phases(refs[5:5 + ns], refs[10 + ns:10 + 2 * ns], *refs[11 + 2 * ns:])
            pl.when(jnp.logical_and(i == 0, j == 0))(send)

        @pl.when(j == 0)
        def _():
            h_ref[...] = _rms(x_ref[...], g_ref[...]).astype(BF16)
            acc_s[...] = jnp.zeros_like(acc_s)

        @pl.when(jnp.logical_and(i == 0, j == 0))
        def _():
            dg_ref[...] = jnp.zeros_like(dg_ref)

        a = jnp.maximum(jnp.dot(h_ref[...], w1_ref[0], preferred_element_type=F32), 0.0)
        r_ref[...] = (a * a).astype(BF16)
        dr = lax.dot_general(dy_ref[...].astype(BF16), w2_ref[...], (((1,), (1,)), ((), ())), preferred_element_type=F32)
        da = (dr * (2.0 * a)).astype(BF16)
        da_ref[...] = da
        acc_s[...] += lax.dot_general(da, w1_ref[0], (((1,), (1,)), ((), ())), preferred_element_type=F32)

        @pl.when(j == NDEV - 1)
        def _():
            _, vjp = jax.vjp(_rms, x_ref[...], g_ref[...])
            dx, dg = vjp(acc_s[...])
            dx_ref[...] = dy_ref[...] + dx
            dg_ref[...] += dg

        if ns:
            pl.when(jnp.logical_and(i == m // tm - 1, j == NDEV - 1))(finish)

    outs = pl.pallas_call(
        body, name="ffn_bwd", grid=(m // tm, NDEV),
        in_specs=[pl.BlockSpec((tm, D), lambda i, j: (i, 0)), pl.BlockSpec((tm, D), lambda i, j: (i, 0)),
                  pl.BlockSpec((1, D), lambda i, j: (0, 0))] + _ffn_specs(tm) + _any_specs(ns),
        out_specs=[pl.BlockSpec((tm, D), lambda i, j: (i, 0)), pl.BlockSpec((tm, FF_SHARD), lambda i, j: (i, j)),
                   pl.BlockSpec((tm, FF_SHARD), lambda i, j: (i, j)), pl.BlockSpec((tm, D), lambda i, j: (i, 0)),
                   pl.BlockSpec((1, D), lambda i, j: (0, 0))] + _any_specs(ns),
        out_shape=[jax.ShapeDtypeStruct((m, D), F32), jax.ShapeDtypeStruct((m, DFF), BF16), jax.ShapeDtypeStruct((m, DFF), BF16),
                   jax.ShapeDtypeStruct((m, D), BF16), jax.ShapeDtypeStruct((1, D), F32)] + _sibling_swap_shapes(swap),
        scratch_shapes=[pltpu.VMEM((tm, D), F32)] + (_sibling_swap_sems(ns) if ns else []),
        compiler_params=_params(dimension_semantics=("arbitrary", "arbitrary")),
    )(x, dy, gain, w1, w2, *swap)
    return outs[:5], outs[5:]


def _mm_nt_call(a, b, name):
    m, k = a.shape
    n = b.shape[0]
    tm = _row_tile(m)

    def body(a_ref, b_ref, out_ref):
        out_ref[...] = lax.dot_general(a_ref[...].astype(BF16), b_ref[...].astype(BF16), (((1,), (1,)), ((), ())),
                                       preferred_element_type=F32)

    return pl.pallas_call(
        body, name=name, grid=(m // tm,),
        in_specs=[pl.BlockSpec((tm, k), lambda i: (i, 0)), pl.BlockSpec((n, k), lambda i: (0, 0))],
        out_specs=pl.BlockSpec((tm, n), lambda i: (i, 0)),
        out_shape=jax.ShapeDtypeStruct((m, n), F32),
        compiler_params=_params(dimension_semantics=("arbitrary",)),
    )(a, b)


def _mm_tn_call(a, b, name, col_shards=False):
    m, k = a.shape
    n = b.shape[1]
    tm, tk = _row_tile(m, 1024), min(k, 1024)
    tn = n // NDEV if col_shards else min(n, 1024)

    def body(a_ref, b_ref, out_ref, acc_s):
        s = pl.program_id(2)

        @pl.when(s == 0)
        def _():
            acc_s[...] = jnp.zeros_like(acc_s)

        acc_s[...] += lax.dot_general(a_ref[...].astype(BF16), b_ref[...].astype(BF16), (((0,), (0,)), ((), ())),
                                      preferred_element_type=F32)

        @pl.when(s == m // tm - 1)
        def _():
            out_ref[...] = acc_s[...].astype(BF16).reshape(out_ref.shape)

    if col_shards:
        out_spec, out_shape = pl.BlockSpec((1, tk, tn), lambda i, j, s: (j, i, 0)), (NDEV, k, tn)
    else:
        out_spec, out_shape = pl.BlockSpec((tk, tn), lambda i, j, s: (i, j)), (k, n)
    return pl.pallas_call(
        body, name=name, grid=(k // tk, n // tn, m // tm),
        in_specs=[pl.BlockSpec((tm, tk), lambda i, j, s: (s, i)), pl.BlockSpec((tm, tn), lambda i, j, s: (s, j))],
        out_specs=out_spec, out_shape=jax.ShapeDtypeStruct(out_shape, BF16),
        scratch_shapes=[pltpu.VMEM((tk, tn), F32)],
        compiler_params=_params(dimension_semantics=("arbitrary", "arbitrary", "arbitrary")),
    )(a, b)


def _loss_call(y, target):
    m = y.shape[0]
    tm = _row_tile(m)

    def body(y_ref, t_ref, loss_ref, dy_ref):
        @pl.when(pl.program_id(0) == 0)
        def _():
            loss_ref[...] = jnp.zeros_like(loss_ref)

        err = y_ref[...] - t_ref[...]
        dy_ref[...] = err * (1.0 / D)
        per_row = jnp.mean(err * err, axis=-1, keepdims=True)
        loss_ref[...] += jnp.broadcast_to(0.5 * jnp.sum(per_row, axis=0, keepdims=True), (1, LANES))

    return pl.pallas_call(
        body, name="loss", grid=(m // tm,),
        in_specs=[pl.BlockSpec((tm, D), lambda i: (i, 0))] * 2,
        out_specs=[pl.BlockSpec((1, LANES), lambda i: (0, 0)), pl.BlockSpec((tm, D), lambda i: (i, 0))],
        out_shape=[jax.ShapeDtypeStruct((1, LANES), F32), jax.ShapeDtypeStruct((m, D), F32)],
        compiler_params=_params(dimension_semantics=("arbitrary",)),
    )(y, target)


def _adamw_call(w, g, m, v, name):
    shape = w.shape
    cols = shape[-1] if w.ndim > 1 else w.size
    rows = w.size // cols
    tr = rows if (rows <= 512 or rows % 512) else 512
    c1, c2 = 1.0 - ADAM_B1 ** ADAM_STEP, 1.0 - ADAM_B2 ** ADAM_STEP

    def body(w_ref, g_ref, m_ref, v_ref, d_ref, nm_ref, nv_ref):
        g_ = g_ref[...]
        nm = ADAM_B1 * m_ref[...] + (1.0 - ADAM_B1) * g_
        nv = ADAM_B2 * v_ref[...] + (1.0 - ADAM_B2) * (g_ * g_)
        d_ref[...] = -ADAM_LR * ((nm / c1) / (jnp.sqrt(nv / c2) + ADAM_EPS) + ADAM_WD * w_ref[...])
        nm_ref[...], nv_ref[...] = nm, nv

    spec = pl.BlockSpec((tr, cols), lambda i: (i, 0))
    outs = pl.pallas_call(
        body, name=name, grid=(rows // tr,), in_specs=[spec] * 4, out_specs=[spec] * 3,
        out_shape=[jax.ShapeDtypeStruct((rows, cols), F32)] * 3,
        compiler_params=_params(dimension_semantics=("arbitrary",)),
    )(*(t.reshape(rows, cols) for t in (w, g, m, v)))
    return tuple(o.reshape(shape) for o in outs)


def _sum_tile(rows):
    for cand in (2048, 1024, 512, 256, 128):
        if rows > cand and rows % cand == 0:
            return cand
    return rows


def _pair_sum_call(g, got, core, name):
    rows, cols = g.shape[1:]
    tr = _sum_tile(rows)

    def body(core_ref, g_ref, got_ref, out_ref):
        del core_ref
        out_ref[...] = (g_ref[...].astype(F32) + got_ref[...].astype(F32)).astype(BF16)

    grid_spec = pltpu.PrefetchScalarGridSpec(
        num_scalar_prefetch=1, grid=(4, rows // tr),
        in_specs=[pl.BlockSpec((1, tr, cols), lambda ch, t, core_ref: (2 * ch + core_ref[0], t, 0)),
                  pl.BlockSpec((1, tr, cols), lambda ch, t, core_ref: (ch, t, 0))],
        out_specs=pl.BlockSpec((1, tr, cols), lambda ch, t, core_ref: (ch, t, 0)))
    return pl.pallas_call(
        body, name=name, grid_spec=grid_spec, out_shape=jax.ShapeDtypeStruct((4, rows, cols), BF16),
        compiler_params=_params(dimension_semantics=("arbitrary", "arbitrary")),
    )(jnp.asarray(core, jnp.int32).reshape(1), g, got)


def _sum_call(parts, out_dtype, name):
    rows, cols = parts[0][0].shape[1:]
    tr = _sum_tile(rows)
    index = jnp.stack([jnp.asarray(i, jnp.int32) for _, i in parts])

    def body(idx_ref, *refs):
        del idx_ref
        acc = refs[0][0].astype(F32)
        for r in refs[1:-1]:
            acc = acc + r[0].astype(F32)
        refs[-1][...] = acc.astype(out_dtype)

    grid_spec = pltpu.PrefetchScalarGridSpec(
        num_scalar_prefetch=1, grid=(rows // tr,),
        in_specs=[pl.BlockSpec((1, tr, cols), lambda t, idx, n=n: (idx[n], t, 0)) for n in range(len(parts))],
        out_specs=pl.BlockSpec((tr, cols), lambda t, idx: (t, 0)))
    return pl.pallas_call(
        body, name=name, grid_spec=grid_spec, out_shape=jax.ShapeDtypeStruct((rows, cols), out_dtype),
        compiler_params=_params(dimension_semantics=("arbitrary",)),
    )(index, *(a for a, _ in parts))


def _place():
    return lax.axis_index("x"), lax.axis_index("y"), lax.axis_index("c")


def _any_specs(n):
    return [pl.BlockSpec(memory_space=pl.ANY)] * n


def _all_gather_call(xs, name):
    n = len(xs)

    def body(*refs):
        for phase in _gather_phases(refs[:n], refs[n:2 * n], *refs[2 * n:]):
            phase()

    return pl.pallas_call(
        body, name=name, in_specs=_any_specs(n), out_specs=_any_specs(n),
        out_shape=_gather_shapes(xs), scratch_shapes=_gather_sems(n),
    )(*xs)


def _gather_shapes(xs):
    return [jax.ShapeDtypeStruct((NDEV,) + x.shape, x.dtype) for x in xs]


def _gather_sems(n):
    return [pltpu.SemaphoreType.DMA((7 * n,)), pltpu.SemaphoreType.DMA((7 * n,)), pltpu.SemaphoreType.DMA((n,))]


def _gather_phases(x_refs, out_refs, send_sems, recv_sems, local_sems):
    n = len(x_refs)
    ax, ay, ac = _place()
    me, sibling = (ax, ay, ac), (ax, ay, 1 - ac)
    chips = [(1 - ax, ay), (ax, 1 - ay), (1 - ax, 1 - ay)]

    def copy(a, k, block, to, src=None):
        slot = out_refs[a].at[4 * block[0] + 2 * block[1] + block[2]]
        return pltpu.make_async_remote_copy(
            src_ref=slot if src is None else src, dst_ref=slot,
            send_sem=send_sems.at[7 * a + k], recv_sem=recv_sems.at[7 * a + k], device_id=to, device_id_type=MESH)

    local = [pltpu.make_async_copy(x_refs[a], out_refs[a].at[4 * ax + 2 * ay + ac], local_sems.at[a]) for a in range(n)]
    first = []
    for a in range(n):
        first.append(copy(a, 0, me, sibling, src=x_refs[a]))
        first += [copy(a, 1 + j, me, (*chip, ac), src=x_refs[a]) for j, chip in enumerate(chips)]
    passed = [copy(a, 4 + j, (*chip, ac), sibling) for j, chip in enumerate(chips) for a in range(n)]

    def send():
        for cp in local + first:
            cp.start()

    def forward():
        for j, chip in enumerate(chips):
            for a in range(n):
                copy(a, 1 + j, (*chip, ac), me).wait_recv()
                passed[j * n + a].start()

    def finish():
        for a in range(n):
            copy(a, 0, sibling, me).wait_recv()
            for j, chip in enumerate(chips):
                copy(a, 4 + j, (*chip, 1 - ac), me).wait_recv()
        for cp in first + passed:
            cp.wait_send()
        for cp in local:
            cp.wait()

    return send, forward, finish


def _swap_sibling_call(xs, name):
    n = len(xs)

    def body(*refs):
        for phase in _sibling_swap_phases(refs[:n], refs[n:2 * n], *refs[2 * n:]):
            phase()

    return pl.pallas_call(
        body, name=name, in_specs=_any_specs(n), out_specs=_any_specs(n),
        out_shape=_sibling_swap_shapes(xs), scratch_shapes=_sibling_swap_sems(n),
    )(*xs)


def _sibling_swap_shapes(xs):
    return [jax.ShapeDtypeStruct((4,) + x.shape[1:], x.dtype) for x in xs]


def _sibling_swap_sems(n):
    return [pltpu.SemaphoreType.DMA((n,)), pltpu.SemaphoreType.DMA((n,))]


def _sibling_swap_phases(x_refs, out_refs, send_sems, recv_sems):
    ax, ay, ac = _place()
    sibling = (ax, ay, 1 - ac)

    def send():
        for a, (x_ref, out_ref) in enumerate(zip(x_refs, out_refs)):
            for chip in range(4):
                pltpu.make_async_remote_copy(src_ref=x_ref.at[2 * chip + 1 - ac], dst_ref=out_ref.at[chip],
                                             send_sem=send_sems.at[a], recv_sem=recv_sems.at[a],
                                             device_id=sibling, device_id_type=MESH).start()

    def finish():
        for a, (x_ref, out_ref) in enumerate(zip(x_refs, out_refs)):
            pltpu.make_async_remote_copy(src_ref=x_ref.at[pl.ds(0, 4)], dst_ref=out_ref, send_sem=send_sems.at[a],
                                         recv_sem=recv_sems.at[a], device_id=sibling, device_id_type=MESH).wait()

    return send, finish


def _swap_chips_call(xs, name):
    n = len(xs)

    def body(*refs):
        for phase in _chip_swap_phases(refs[:n], refs[n:2 * n], *refs[2 * n:]):
            phase()

    return pl.pallas_call(
        body, name=name, in_specs=_any_specs(n), out_specs=_any_specs(n),
        out_shape=_chip_swap_shapes(xs), scratch_shapes=_chip_swap_sems(n),
    )(*xs)


def _chip_swap_shapes(xs):
    return [jax.ShapeDtypeStruct((3,) + x.shape[1:], x.dtype) for x in xs]


def _chip_swap_sems(n):
    return [pltpu.SemaphoreType.DMA((3 * n,)), pltpu.SemaphoreType.DMA((3 * n,))]


def _chip_swap_phases(x_refs, out_refs, send_sems, recv_sems):
    ax, ay, ac = _place()
    chips = [(1 - ax, ay), (ax, 1 - ay), (1 - ax, 1 - ay)]
    copies = [pltpu.make_async_remote_copy(src_ref=x_refs[a].at[2 * cx + cy], dst_ref=out_refs[a].at[j],
                                           send_sem=send_sems.at[3 * a + j], recv_sem=recv_sems.at[3 * a + j],
                                           device_id=(cx, cy, ac), device_id_type=MESH)
              for a in range(len(x_refs)) for j, (cx, cy) in enumerate(chips)]

    def send():
        for cp in copies:
            cp.start()

    def finish():
        for cp in copies:
            cp.wait()

    return send, finish


def _reduce_begin(gs, name):
    ac = lax.axis_index("c")
    got = _swap_sibling_call(gs, name + "_d2d")
    return got, [_pair_sum_call(g, t, ac, f"{name}_pair{a}") for a, (g, t) in enumerate(zip(gs, got))]


def _reduce_end(gs, got, from_chips, name):
    ax, ay, ac = _place()
    me, my_chip = 4 * ax + 2 * ay + ac, 2 * ax + ay
    return [_sum_call([(g, me), (t, my_chip), (f, 0), (f, 1), (f, 2)], F32, f"{name}_total{a}")
            for a, (g, t, f) in enumerate(zip(gs, got, from_chips))]


SMALL = ("norm1_g", "conv_w", "a_log", "dt_bias", "dn_out_g", "sb_q_g", "sb_k_g", "sg_v_g", "sg_w", "sg_b", "norm2_g")
WEIGHTS = ("norm1_g", "w_in", "conv_w", "a_log", "dt_bias", "dn_out_g", "sb_q_g", "sb_k_g", "sg_v_g", "sg_w", "sg_b",
           "w_out", "norm2_g", "w_ff1", "w_ff2")
SMALL_SHAPE = {"norm1_g": (D,), "conv_w": (4, 3 * DN_W), "a_log": (NH,), "dt_bias": (NH,), "dn_out_g": (128,), "sb_q_g": (64,),
               "sb_k_g": (64,), "sg_v_g": (SG_W,), "sg_w": (NH, 128, 128), "sg_b": (NH, 128), "norm2_g": (D,)}


def _size(shape):
    n = 1
    for s in shape:
        n *= s
    return n


def _to_rows(flat, multiple):
    pad = (-flat.shape[0]) % (LANES * multiple)
    return jnp.pad(flat, (0, pad)).reshape(-1, LANES)


def _conv_by_pair(conv):
    return conv.reshape(4, 3, 2, 256).transpose(0, 2, 1, 3).reshape(4, 3 * DN_W)


def kernel(x, norm1_g, w_in, conv_w, a_log, dt_bias, dn_out_g, sb_q_g, sb_k_g, sg_v_g, sg_w, sg_b, w_out, norm2_g, w_ff1, w_ff2, loss_target, m_norm1_g, m_w_in, m_conv_w, m_a_log, m_dt_bias, m_dn_out_g, m_sb_q_g, m_sb_k_g, m_sg_v_g, m_sg_w, m_sg_b, m_w_out, m_norm2_g, m_w_ff1, m_w_ff2, v_norm1_g, v_w_in, v_conv_w, v_a_log, v_dt_bias, v_dn_out_g, v_sb_q_g, v_sb_k_g, v_sg_v_g, v_sg_w, v_sg_b, v_w_out, v_norm2_g, v_w_ff1, v_w_ff2):
    given = dict(norm1_g=norm1_g, w_in=w_in, conv_w=conv_w, a_log=a_log, dt_bias=dt_bias, dn_out_g=dn_out_g, sb_q_g=sb_q_g,
                 sb_k_g=sb_k_g, sg_v_g=sg_v_g, sg_w=sg_w, sg_b=sg_b, w_out=w_out, norm2_g=norm2_g, w_ff1=w_ff1, w_ff2=w_ff2)
    mom = dict(norm1_g=m_norm1_g, w_in=m_w_in, conv_w=m_conv_w, a_log=m_a_log, dt_bias=m_dt_bias, dn_out_g=m_dn_out_g,
               sb_q_g=m_sb_q_g, sb_k_g=m_sb_k_g, sg_v_g=m_sg_v_g, sg_w=m_sg_w, sg_b=m_sg_b, w_out=m_w_out, norm2_g=m_norm2_g,
               w_ff1=m_w_ff1, w_ff2=m_w_ff2)
    var = dict(norm1_g=v_norm1_g, w_in=v_w_in, conv_w=v_conv_w, a_log=v_a_log, dt_bias=v_dt_bias, dn_out_g=v_dn_out_g,
               sb_q_g=v_sb_q_g, sb_k_g=v_sb_k_g, sg_v_g=v_sg_v_g, sg_w=v_sg_w, sg_b=v_sg_b, w_out=v_w_out, norm2_g=v_norm2_g,
               w_ff1=v_w_ff1, w_ff2=v_w_ff2)
    B, T, _ = x.shape
    M = B * T
    ax, ay, ac = _place()
    me = 4 * ax + 2 * ay + ac
    table_fwd, table_back = _row_tables()

    send = []
    for l in range(2):
        w_in_t = jnp.pad(w_in[l].T, ((0, IN_SHARD_PAD - IN_SHARD), (0, 0)))
        send.append([w_in_t.astype(BF16), w_out[l].astype(BF16), w_ff1[l].astype(BF16), w_ff2[l].astype(BF16)])
    first_in, conv_rows = _all_gather_call([send[0][0], _to_rows(conv_w.reshape(-1), 8)], "gather_first")
    conv_full = conv_rows.reshape(NDEV, -1)[:, :conv_w.size].reshape(NDEV, 2, 4, -1).transpose(1, 2, 0, 3).reshape(2, 4, 3 * DN_W)
    gathered = [[first_in, None, None, None], [None] * 4]

    pad_vec = lambda v: jnp.zeros((1, LANES), F32).at[0, :v.shape[0]].set(v)
    layer = []
    for l in range(2):
        layer.append(dict(
            g1=norm1_g[l].reshape(1, D), g2=norm2_g[l].reshape(1, D), conv=_conv_by_pair(conv_full[l]),
            a_log=pad_vec(a_log[l]), dt_bias=pad_vec(dt_bias[l]), dn_g=dn_out_g[l].reshape(1, LANES),
            sb_qg=jnp.tile(sb_q_g[l], 2).reshape(1, LANES), sb_kg=jnp.tile(sb_k_g[l], 2).reshape(1, LANES),
            sg_g=sg_v_g[l].reshape(1, SG_W), sg_w=sg_w[l], sg_bias=jnp.repeat(sg_b[l].T, 64, axis=1)))

    cur = x.reshape(M, D)
    saved = []
    for l, p in enumerate(layer):
        p["wt"] = _row_perm_call(gathered[l][0].reshape(NDEV * IN_SHARD_PAD, D), table_fwd, "pack_w_in")
        p_dn, p_sb, p_sg, p_ab, h = _in_proj_call(cur, p["g1"], p["wt"])
        mix, dn_kept, arrived = _dn_fwd_call(p_dn, p_ab, p["conv"], p["a_log"], p["dt_bias"], p["dn_g"], B, T,
                                             gather=send[0][1:] + send[1][:1] if l == 0 else [])
        if l == 0:
            gathered[0][1:], gathered[1][0] = list(arrived[:3]), arrived[3]
        p["w_out"], p["w1"], p["w2"] = gathered[l][1].reshape(D, D), gathered[l][2], gathered[l][3].reshape(DFF, D)
        mix, sb_carries, arrived = _sb_fwd_call(p_sb, mix, p["sb_qg"], p["sb_kg"], B, T, gather=send[1][1:] if l == 0 else [])
        if l == 0:
            gathered[1][1:] = list(arrived)
        mix = _sg_fwd_call(p_sg, mix, p["sg_g"], p["sg_w"], p["sg_bias"], B, T)
        x1 = _out_proj_call(mix, p["w_out"], cur)
        x2 = _ffn_fwd_call(x1, p["g2"], p["w1"], p["w2"])
        saved.append(dict(x0=cur, p_dn=p_dn, p_sb=p_sb, p_sg=p_sg, p_ab=p_ab, h=h, mix=mix, x1=x1, dn_kept=dn_kept, sb_carries=sb_carries))
        cur = x2
    loss_part, dy = _loss_call(cur, loss_target.reshape(M, D))
    loss = lax.psum(loss_part[0, 0], ("x", "y", "c"))

    big_grads = [[None] * 4, [None] * 4]
    small_grads = {n: [None, None] for n in SMALL}
    for l in (1, 0):
        p, s = layer[l], saved[l]
        (dx1, da, r, h2, dg2), got1 = _ffn_bwd_call(s["x1"], dy, p["g2"], p["w1"], p["w2"], swap=big_grads[1] if l == 0 else ())
        big_grads[l][2] = _mm_tn_call(h2, da, "grad_w_ff1", col_shards=True)
        big_grads[l][3] = _mm_tn_call(r, dy, "grad_w_ff2").reshape(NDEV, FF_SHARD, D)
        dmix = _mm_nt_call(dx1, p["w_out"], "dmix")
        big_grads[l][1] = _mm_tn_call(s["mix"], dx1, "grad_w_out").reshape(NDEV, D // NDEV, D)
        if l == 0:
            got0, sums0 = _reduce_begin(big_grads[0][1:], "reduce_early0")
            early, early_got = big_grads[1] + big_grads[0][1:], list(got1) + list(got0)
            early_sums = [_pair_sum_call(g, t, ac, f"reduce_early1_pair{a}") for a, (g, t) in enumerate(zip(big_grads[1], got1))] + sums0
        (d_dn, d_ab, dcw, dalog, ddtb, ddn_g), early_from = _dn_bwd_call(
            s["p_dn"], s["p_ab"], dmix, s["dn_kept"], p["conv"], p["a_log"], p["dt_bias"], p["dn_g"], B, T,
            swap=early_sums if l == 0 else ())
        d_sb, dqg, dkg = _sb_bwd_call(s["p_sb"], dmix, s["sb_carries"], p["sb_qg"], p["sb_kg"], B, T)
        d_sg, dsg_g, dsg_w, dsg_b = _sg_bwd_call(s["p_sg"], dmix, p["sg_g"], p["sg_w"], p["sg_bias"], B, T)
        dsections = (d_dn, d_sb, d_sg, d_ab)
        dy, dg1 = _in_proj_bwd_call(dsections, p["wt"], s["x0"], p["g1"], dx1)
        dwt = _in_proj_grad_call(dsections, s["h"])
        big_grads[l][0] = _row_perm_call(dwt, table_back, "unpack_grad_w_in").reshape(NDEV, IN_SHARD_PAD, D)
        for n, val in (("norm1_g", dg1[0]), ("conv_w", dcw.transpose(1, 0, 2).reshape(4, 3 * DN_W)), ("a_log", dalog[0, :NH]),
                       ("dt_bias", ddtb[0, :NH]), ("dn_out_g", ddn_g[0]), ("sb_q_g", dqg[0, :64]), ("sb_k_g", dkg[0, :64]),
                       ("sg_v_g", dsg_g[0]), ("sg_w", dsg_w), ("sg_b", dsg_b[:, :NH].T), ("norm2_g", dg2[0])):
            small_grads[n][l] = val
    grad_x = dy.reshape(B, T, D)

    last = big_grads[0][:1]
    last_got, last_sums = _reduce_begin(last, "reduce_last")
    mine0 = _reduce_end(last, last_got, _swap_chips_call(last_sums, "reduce_last_ici"), "reduce_last")
    mine1 = _reduce_end(early, early_got, early_from, "reduce_early")
    grads = {"w_in": jnp.stack([mine0[0][:IN_SHARD].T, mine1[0][:IN_SHARD].T]), "w_out": jnp.stack([mine1[4], mine1[1]]),
             "w_ff1": jnp.stack([mine1[5], mine1[2]]), "w_ff2": jnp.stack([mine1[6], mine1[3]])}
    small_flat = jnp.concatenate([jnp.stack(small_grads[n]).reshape(-1) for n in SMALL])
    everyone, = _all_gather_call([_to_rows(small_flat, 8)], "gather_small_grads")
    small_sum = _sum_call([(everyone, k) for k in range(NDEV)], F32, "sum_small_grads").reshape(-1)
    off = 0
    for n in SMALL:
        sz = 2 * _size(SMALL_SHAPE[n])
        grads[n] = small_sum[off:off + sz].reshape((2,) + SMALL_SHAPE[n])
        off += sz
    cshard = conv_w.shape[-1]
    grads["conv_w"] = lax.dynamic_slice_in_dim(grads["conv_w"], me * cshard, cshard, axis=2)

    deltas, new_m, new_v = {}, {}, {}
    for n in WEIGHTS:
        deltas[n], new_m[n], new_v[n] = _adamw_call(given[n], grads[n], mom[n], var[n], "adamw_" + n)
    return (loss, grad_x, *[grads[n] for n in WEIGHTS], *[deltas[n] for n in WEIGHTS], *[new_m[n] for n in WEIGHTS],
            *[new_v[n] for n in WEIGHTS])
```

```python
import functools

import numpy as np

import jax
import jax.numpy as jnp
from jax import lax
from jax.experimental import pallas as pl
from jax.experimental.pallas import tpu as pltpu

F32, BF16 = jnp.float32, jnp.bfloat16
EPS = 1e-6
LANES = 128
D = 1024
DFF = 4096
NH = 4
DN_W, SB_W, SG_W = 512, 256, 256
IN_DIM = 3336
NDEV = 8
IN_SHARD = IN_DIM // NDEV
IN_SHARD_PAD = 432
FF_SHARD = DFF // NDEV
DN_OFF, SB_OFF, SG_OFF, AB_OFF, NPACK = 0, 2048, 2816, 3328, 3456
SECTIONS = ((DN_OFF, 2048), (SB_OFF, 768), (SG_OFF, 512), (AB_OFF, 128))
SB_SCALE = 64 ** -0.5
DN_SCALE = 128 ** -0.5
VMEM_LIMIT = 56 * 1024 * 1024
VMEM_LIMIT_MAX = 62 * 1024 * 1024
ADAM_LR, ADAM_B1, ADAM_B2, ADAM_EPS, ADAM_WD, ADAM_STEP = 0.001, 0.9, 0.999, 1e-08, 0.01, 10
MESH = pl.DeviceIdType.MESH


def _iota(shape, dim):
    return lax.broadcasted_iota(jnp.int32, shape, dim)


def _params(**kw):
    return pltpu.CompilerParams(vmem_limit_bytes=VMEM_LIMIT, **kw)


NN, NT, TN = ((1,), (0,)), ((1,), (1,)), ((0,), (0,))


def _mm(a, b, dims):
    return lax.dot_general(a.astype(BF16), b.astype(BF16), (dims, ((), ())), preferred_element_type=F32)


def _plain(a, b, dims):
    return (a.T if dims == TN else a), (b.T if dims == NT else b)


def _mmx(a, b, dims):
    return _mm(*_plain(a, b, dims), NN)


@jax.custom_vjp
def _dot(a, b):
    return _mmx(a, b, NN)


def _dot_fwd(a, b):
    return _dot(a, b), (a, b)


def _dot_bwd(res, g):
    a, b = res
    return _mmx(g, b, NT).astype(a.dtype), _mmx(a, g, TN).astype(b.dtype)


_dot.defvjp(_dot_fwd, _dot_bwd)


@jax.custom_vjp
def _dot_nt(a, b):
    return _mmx(a, b, NT)


def _dot_nt_fwd(a, b):
    return _dot_nt(a, b), (a, b)


def _dot_nt_bwd(res, g):
    a, b = res
    return _mmx(g, b, NN).astype(a.dtype), _mmx(g, a, TN).astype(b.dtype)


_dot_nt.defvjp(_dot_nt_fwd, _dot_nt_bwd)


@jax.custom_vjp
def _dot_tn(a, b):
    return _mmx(a, b, TN)


def _dot_tn_fwd(a, b):
    return _dot_tn(a, b), (a, b)


def _dot_tn_bwd(res, g):
    a, b = res
    return _mmx(b, g, NT).astype(a.dtype), _mmx(a, g, NN).astype(b.dtype)


_dot_tn.defvjp(_dot_tn_fwd, _dot_tn_bwd)


def _split(x):
    hi = x.astype(BF16)
    return hi, (x - hi.astype(F32)).astype(BF16)


def _mm3(a, b, dims):
    a, b = _plain(a, b, dims)
    (ah, al), (bh, bl) = _split(a), _split(b)
    mm = lambda x, y: jnp.dot(x, y, preferred_element_type=F32)
    return mm(ah, bh) + (mm(ah, bl) + mm(al, bh))


def _mm_ones(ones, x, ones_left):
    hi, lo = _split(x)
    mm = (lambda t: jnp.dot(ones, t, preferred_element_type=F32)) if ones_left else \
         (lambda t: jnp.dot(t, ones, preferred_element_type=F32))
    return mm(hi) + mm(lo)


def _pair_ones(kind, transposed):
    row, col = _iota((128, 128), 0), _iota((128, 128), 1)
    m = (row // 64) == (col // 64)
    if kind == "running":
        m = jnp.logical_and(m, (col >= row) if transposed else (col <= row))
    return jnp.where(m, 1.0, 0.0).astype(BF16)


@functools.partial(jax.custom_vjp, nondiff_argnums=(0,))
def _chunk_sum(kind, x):
    return _mm_ones(_pair_ones(kind, False), x, True)


def _chunk_sum_fwd(kind, x):
    return _chunk_sum(kind, x), None


def _chunk_sum_bwd(kind, _, g):
    return (_mm_ones(_pair_ones(kind, True), g, True),)


_chunk_sum.defvjp(_chunk_sum_fwd, _chunk_sum_bwd)


def _tri_ones(n, transposed):
    row, col = _iota((n, n), 0), _iota((n, n), 1)
    return jnp.where((row < col) if transposed else (row > col), 1.0, 0.0).astype(BF16)


@jax.custom_vjp
def _suffix_sum(x):
    return _mm_ones(_tri_ones(x.shape[1], False), x, False)


def _suffix_sum_fwd(x):
    return _suffix_sum(x), None


def _suffix_sum_bwd(_, g):
    return (_mm_ones(_tri_ones(g.shape[1], True), g, False),)


_suffix_sum.defvjp(_suffix_sum_fwd, _suffix_sum_bwd)


def _sigmoid(x):
    return jax.nn.sigmoid(x)


def _silu(x):
    return x * _sigmoid(x)


def _softplus(x):
    return jnp.maximum(x, 0.0) + jnp.log1p(jnp.exp(-jnp.abs(x)))


def _gelu(x):
    return 0.5 * x * (1.0 + jnp.tanh(0.7978845608028654 * (x + 0.044715 * (x * x * x))))


def _rms(x, gain):
    return x * lax.rsqrt(jnp.mean(x * x, axis=-1, keepdims=True) + EPS) * gain


def _shift_down_impl(x, k):
    return jnp.where(_iota(x.shape, 0) >= k, pltpu.roll(x, k, 0), 0.0)


def _shift_up_impl(x, k):
    n = x.shape[0]
    return jnp.where(_iota(x.shape, 0) < n - k, pltpu.roll(x, n - k, 0), 0.0)


@functools.partial(jax.custom_vjp, nondiff_argnums=(1,))
def _shift_down(x, k):
    return _shift_down_impl(x, k)


def _shift_down_fwd(x, k):
    return _shift_down_impl(x, k), None


def _shift_down_bwd(k, _, g):
    return (_shift_up_impl(g, k),)


_shift_down.defvjp(_shift_down_fwd, _shift_down_bwd)


def _lane_pick(x, idx):
    return jnp.sum(jnp.where(_iota(x.shape, 1) == idx, x, 0.0), axis=-1, keepdims=True)


def _dn_conv(x, w0, w1, w2, w3, l2_scale):
    y = _silu(w3 * x + w2 * _shift_down(x, 1) + w1 * _shift_down(x, 2) + w0 * _shift_down(x, 3))
    if l2_scale is None:
        return y
    return y * lax.rsqrt(jnp.sum(y * y, axis=-1, keepdims=True) + EPS) * l2_scale


def _dn_gates(ab, a_log, dt_bias):
    lane = _iota((1, LANES), 1)
    g = -jnp.exp(a_log) * _softplus(ab + dt_bias)
    return jnp.where(lane < NH, g, jnp.where(lane < 2 * NH, _sigmoid(ab), 0.0))


def _same_head(shape):
    return (_iota(shape, 0) < LANES) == (_iota(shape, 1) < LANES)


def _bd(r2):
    return jnp.where(_same_head((2 * LANES, 2 * LANES)), jnp.concatenate([r2, r2], axis=0), 0.0)


def _bd_t(y2):
    t = y2.T
    return jnp.where(_same_head((2 * LANES, 2 * LANES)), jnp.concatenate([t, t], axis=1), 0.0)


def _pair_prod(kind, a2, b2, mm):
    if kind == NN:
        return mm(a2, _bd(b2))
    if kind == NT:
        return mm(a2, _bd_t(b2))
    full = mm(a2.T, b2)
    return jnp.concatenate([full[:LANES, :LANES], full[LANES:, LANES:]], axis=1)


_MM1 = lambda x, y: _mm(x, y, NN)
_MM3 = lambda x, y: _mm3(x, y, NN)


def _pair_vjp_rule(kind, a2, b2, g, mm):
    if kind == NN:
        return _pair_prod(NT, g, b2, mm), _pair_prod(TN, a2, g, mm)
    if kind == NT:
        return _pair_prod(NN, g, b2, mm), _pair_prod(TN, g, a2, mm)
    return _pair_prod(NT, b2, g, mm), _pair_prod(NN, a2, g, mm)


@functools.partial(jax.custom_vjp, nondiff_argnums=(0,))
def _pdot(kind, a2, b2):
    return _pair_prod(kind, a2, b2, _MM1)


def _pdot_fwd(kind, a2, b2):
    return _pdot(kind, a2, b2), (a2, b2)


def _pdot_bwd(kind, res, g):
    return _pair_vjp_rule(kind, *res, g, _MM1)


_pdot.defvjp(_pdot_fwd, _pdot_bwd)


def _unit_lower_inverse(lower):
    n = lower.shape[0]
    nk = -lower
    inv = jnp.where(_iota(lower.shape, 0) == jnp.bitwise_and(_iota(lower.shape, 1), n - 1), 1.0, 0.0) + nk
    for _ in range(5):
        nk = _pair_prod(NN, nk, nk, _MM1)
        inv = inv + _pair_prod(NN, inv, nk, _MM1)
    return inv


@jax.custom_vjp
def _solve_with(lower, inv, rhs):
    return _pair_prod(NN, inv, rhs, _MM3)


def _solve_with_fwd(lower, inv, rhs):
    x = _pair_prod(NN, inv, rhs, _MM3)
    return x, (inv, x)


def _solve_with_bwd(res, g):
    inv, x = res
    d_rhs = _pair_prod(TN, inv, g, _MM3)
    return -_pair_prod(NT, d_rhs, x, _MM3), jnp.zeros_like(inv), d_rhs


_solve_with.defvjp(_solve_with_fwd, _solve_with_bwd)


def _dn_local(q, k, v, g, beta, inv=None):
    shape = (LANES, 2 * LANES)
    row, col = _iota(shape, 0), jnp.bitwise_and(_iota(shape, 1), LANES - 1)
    same = (row // 64) == (col // 64)
    tri_incl = jnp.logical_and(same, col <= row)
    tri_strict = jnp.logical_and(same, col < row)
    first = row < 64
    gc = _chunk_sum("running", g)
    gl = _chunk_sum("total", g)
    diff = gc - jnp.concatenate([gc[:, :LANES].T, gc[:, LANES:].T], axis=1)
    decay = jnp.where(tri_incl, jnp.exp(jnp.where(tri_incl, diff, 0.0)), 0.0)
    egc = jnp.exp(gc)
    lower = jnp.where(tri_strict, beta * _pdot(NT, k, k) * decay, 0.0)
    if inv is None:
        inv = _unit_lower_inverse(lower)
    u_val = _solve_with(lower, inv, v * beta)
    w_dec = _solve_with(lower, inv, k * (beta * egc))
    qk = jnp.where(tri_incl, _pdot(NT, q, k) * decay, 0.0)
    q_dec = q * egc
    k_dec = k * jnp.exp(gl - gc)
    cd1 = jnp.exp(jnp.sum(jnp.where(first, g, 0.0), axis=0, keepdims=True))
    cd2 = jnp.exp(jnp.sum(jnp.where(first, 0.0, g), axis=0, keepdims=True))
    return (u_val, w_dec, qk, q_dec, k_dec, cd1, cd2), inv


def _dn_state(u_val, w_dec, qk, q_dec, k_dec, cd1, cd2, s0):
    first = _iota((LANES, 2 * LANES), 0) < 64
    u1 = u_val - _pdot(NN, w_dec, s0)
    s1 = s0 * cd1 + _pdot(TN, jnp.where(first, k_dec, 0.0), u1)
    u2 = u_val - _pdot(NN, w_dec, s1)
    u_new = jnp.where(first, u1, u2)
    s2 = s1 * cd2 + _pdot(TN, jnp.where(first, 0.0, k_dec), u_new)
    o = jnp.where(first, _pdot(NN, q_dec, s0), _pdot(NN, q_dec, s1)) + _pdot(NN, qk, u_new)
    return o, s2


def _dn_post(o, z, gain):
    return _rms(o, gain) * _silu(z)


_DN_L2 = (DN_SCALE, 1.0, None)
DN_HPS = 2
DN_BLK = 4 * DN_HPS * LANES
_DN_COLS = tuple(slice(i * LANES, (i + 1) * LANES) for i in range(DN_HPS))


def _dn_in_cols(s, i):
    return slice((s * DN_HPS + i) * LANES, (s * DN_HPS + i + 1) * LANES)


def _dn_taps(cw_ref, s, i):
    return tuple(cw_ref[t:t + 1, _dn_in_cols(s, i)] for t in range(4))


def _pair_rows(n):
    return pl.ds(pl.multiple_of(n * 128, 128), 128)


def _dn_gate_rows(gate, hp):
    head_a = _iota((1, DN_HPS * LANES), 1) < LANES
    h = DN_HPS * hp
    return (jnp.where(head_a, _lane_pick(gate, h), _lane_pick(gate, h + 1)),
            jnp.where(head_a, _lane_pick(gate, NH + h), _lane_pick(gate, NH + h + 1)))


def _dn_gate_cols(dg, db, hp):
    head_a = _iota((1, DN_HPS * LANES), 1) < LANES
    lane = _iota((1, LANES), 1)
    h = DN_HPS * hp
    out = 0.0
    for t, first in ((dg, h), (db, NH + h)):
        out = out + jnp.where(lane == first, jnp.sum(jnp.where(head_a, t, 0.0), axis=-1, keepdims=True), 0.0)
        out = out + jnp.where(lane == first + 1, jnp.sum(jnp.where(head_a, 0.0, t), axis=-1, keepdims=True), 0.0)
    return out


def _dn_in_specs(T):
    one = pl.Buffered(1)
    vec = pl.BlockSpec((1, LANES), lambda b, h: (0, 0))
    return [pl.BlockSpec((T, DN_BLK), lambda b, h: (b, h), pipeline_mode=one),
            pl.BlockSpec((T, LANES), lambda b, h: (b, 0), pipeline_mode=one),
            pl.BlockSpec((4, 3 * DN_HPS * LANES), lambda b, h: (0, h)), vec, vec, vec]


def _dn_fwd_call(proj_dn, proj_ab, conv_w, a_log, dt_bias, gain, B, T, gather=()):
    npair = T // 128
    ng = len(gather)
    nsteps = B * (NH // DN_HPS)

    def body(*refs):
        x_ref, ab_ref, cw_ref, alog_ref, dtb_ref, gain_ref = refs[:6]
        out_ref, q_s, k_s, v_s, o_s, gate_s, st_s, inv_s = refs[6 + ng:14 + ng]
        step_id = pl.program_id(0) * (NH // DN_HPS) + pl.program_id(1)
        if ng:
            send, forward, finish = _gather_phases(refs[6:6 + ng], refs[14 + ng:14 + 2 * ng], *refs[14 + 2 * ng:])
            pl.when(step_id == 0)(send)
            pl.when(step_id == nsteps - 1)(forward)
        hp = pl.program_id(1)
        for i, cs in enumerate(_DN_COLS):
            for s, (x_s, l2) in enumerate(zip((q_s, k_s, v_s), _DN_L2)):
                x_s[:, cs] = _dn_conv(x_ref[:, _dn_in_cols(s, i)], *_dn_taps(cw_ref, s, i), l2)
        gate_s[...] = _dn_gates(ab_ref[...], alog_ref[...], dtb_ref[...])

        def local_of(pair):
            r = _pair_rows(pair)
            loc, inv = _dn_local(q_s[r, :], k_s[r, :], v_s[r, :], *_dn_gate_rows(gate_s[r, :], hp))
            inv_s[0, 0, pair] = inv
            return loc

        def state_of(n, loc, state):
            st_s[0, 0, n] = state
            o, s2 = _dn_state(*loc, state)
            o_s[_pair_rows(n), :] = o
            return s2

        def step(n, carry):
            loc, state = carry
            return local_of(n + 1), state_of(n, loc, state)

        loc, state = lax.fori_loop(0, npair - 1, step, (local_of(0), jnp.zeros((LANES, DN_HPS * LANES), F32)))
        state_of(npair - 1, loc, state)
        for i, cs in enumerate(_DN_COLS):
            out_ref[:, cs] = _dn_post(o_s[:, cs], x_ref[:, _dn_in_cols(3, i)], gain_ref[...])
        if ng:
            pl.when(step_id == nsteps - 1)(finish)

    kept_specs, kept_shapes = _dn_kept(B, T)
    outs = pl.pallas_call(
        body, name="dn_fwd", grid=(B, NH // DN_HPS), in_specs=_dn_in_specs(T) + _any_specs(ng),
        out_specs=[pl.BlockSpec((T, DN_HPS * LANES), lambda b, h: (b, h), pipeline_mode=pl.Buffered(1))] + kept_specs + _any_specs(ng),
        out_shape=[jax.ShapeDtypeStruct((B * T, D), F32)] + kept_shapes + _gather_shapes(gather),
        scratch_shapes=_gather_sems(ng) if ng else [],
        compiler_params=_params(dimension_semantics=("arbitrary", "arbitrary")),
    )(proj_dn, proj_ab, conv_w, a_log, dt_bias, gain, *gather)
    return outs[0], outs[1:8], outs[8:]


def _dn_kept(B, T):
    one = pl.Buffered(1)
    npair, pairs = T // 128, NH // DN_HPS
    wide = pl.BlockSpec((T, DN_HPS * LANES), lambda b, h: (b, h), pipeline_mode=one)
    per_pair = pl.BlockSpec((1, 1, npair, LANES, DN_HPS * LANES), lambda b, h: (b, h, 0, 0, 0), pipeline_mode=one)
    specs = [wide] * 4 + [pl.BlockSpec((T, LANES), lambda b, h: (b, h), pipeline_mode=one)] + [per_pair] * 2
    shapes = ([jax.ShapeDtypeStruct((B * T, DN_W), F32)] * 4 + [jax.ShapeDtypeStruct((B * T, pairs * LANES), F32)]
              + [jax.ShapeDtypeStruct((B, pairs, npair, LANES, DN_HPS * LANES), F32)] * 2)
    return specs, shapes


def _dn_bwd_call(proj_dn, proj_ab, dmix, kept, conv_w, a_log, dt_bias, gain, B, T, swap=()):
    npair = T // 128
    ns = len(swap)
    nsteps = B * (NH // DN_HPS)

    def body(*refs):
        x_ref, ab_ref, cw_ref, alog_ref, dtb_ref, gain_ref, do_ref, q_s, k_s, v_s, o_ref, gate_s, st_s, inv_s = refs[:14]
        dx_ref, dab_ref, dcw_ref, dalog_ref, ddtb_ref, dgain_ref = refs[14 + ns:20 + ns]
        dgate_s, do_s = refs[20 + 2 * ns:22 + 2 * ns]
        b_i, hp = pl.program_id(0), pl.program_id(1)
        step_id = b_i * (NH // DN_HPS) + hp
        if ns:
            send, finish = _chip_swap_phases(refs[14:14 + ns], refs[20 + ns:20 + 2 * ns], *refs[22 + 2 * ns:])
            pl.when(step_id == 0)(send)

        def pair_in(r):
            return (q_s[r, :], k_s[r, :], v_s[r, :]) + _dn_gate_rows(gate_s[r, :], hp)

        zero_state = jnp.zeros((LANES, DN_HPS * LANES), F32)

        @pl.when(jnp.logical_and(b_i == 0, hp == 0))
        def _():
            dcw_ref[...] = jnp.zeros_like(dcw_ref)
            dalog_ref[...] = jnp.zeros_like(dalog_ref)
            ddtb_ref[...] = jnp.zeros_like(ddtb_ref)
            dgain_ref[...] = jnp.zeros_like(dgain_ref)

        for i, cs in enumerate(_DN_COLS):
            zc = _dn_in_cols(3, i)
            _, post_vjp = jax.vjp(_dn_post, o_ref[:, cs], x_ref[:, zc], gain_ref[...])
            do, dz, dgain = post_vjp(do_ref[:, cs])
            dx_ref[:, zc] = dz
            do_s[:, cs] = do
            dgain_ref[...] += dgain

        wide_cols = lambda s: slice(s * DN_HPS * LANES, (s + 1) * DN_HPS * LANES)

        def back_step(nn, dstate):
            n = npair - 1 - nn
            r = _pair_rows(n)
            inv = inv_s[0, 0, n]
            local = lambda q, k, v, g, beta, inv=inv: _dn_local(q, k, v, g, beta, inv)[0]
            loc, local_vjp = jax.vjp(local, *pair_in(r))
            _, state_vjp = jax.vjp(_dn_state, *loc, st_s[0, 0, n])
            *dloc, ds0 = state_vjp((do_s[r, :], dstate))
            dq, dk, dv, dg, db = local_vjp(tuple(dloc))
            dx_ref[r, wide_cols(0)], dx_ref[r, wide_cols(1)], dx_ref[r, wide_cols(2)] = dq, dk, dv
            dgate_s[r, :] = _dn_gate_cols(dg, db, hp)
            return ds0

        lax.fori_loop(0, npair, back_step, zero_state)

        for i, cs in enumerate(_DN_COLS):
            h = DN_HPS * hp + i
            for s, l2 in enumerate(_DN_L2):
                xc = _dn_in_cols(s, i)
                _, conv_vjp = jax.vjp(functools.partial(_dn_conv, l2_scale=l2), x_ref[:, xc], *_dn_taps(cw_ref, s, i))
                dx, *dw = conv_vjp(dx_ref[:, xc])
                dx_ref[:, xc] = dx
                for t in range(4):
                    dcw_ref[h + 4 * s, t:t + 1, :] += dw[t]
        _, gate_vjp = jax.vjp(_dn_gates, ab_ref[...], alog_ref[...], dtb_ref[...])
        dab, dalog, ddtb = gate_vjp(dgate_s[...])
        dalog_ref[...] += dalog
        ddtb_ref[...] += ddtb

        @pl.when(hp == 0)
        def _():
            dab_ref[...] = jnp.zeros_like(dab_ref)

        dab_ref[...] += dab
        if ns:
            pl.when(step_id == nsteps - 1)(finish)

    M = B * T
    one = pl.Buffered(1)
    vec = pl.BlockSpec((1, LANES), lambda b, h: (0, 0))
    wide = [pltpu.VMEM((T, DN_HPS * LANES), F32)]
    vec_shape = jax.ShapeDtypeStruct((1, LANES), F32)
    outs = pl.pallas_call(
        body, name="dn_bwd", grid=(B, NH // DN_HPS),
        in_specs=_dn_in_specs(T) + [pl.BlockSpec((T, DN_HPS * LANES), lambda b, h: (b, h), pipeline_mode=one)] + _dn_kept(B, T)[0]
        + _any_specs(ns),
        out_specs=[pl.BlockSpec((T, DN_BLK), lambda b, h: (b, h), pipeline_mode=one), pl.BlockSpec((T, LANES), lambda b, h: (b, 0)),
                   pl.BlockSpec((12, 4, LANES), lambda b, h: (0, 0, 0)), vec, vec, vec] + _any_specs(ns),
        out_shape=[jax.ShapeDtypeStruct((M, 4 * DN_W), F32), jax.ShapeDtypeStruct((M, LANES), F32),
                   jax.ShapeDtypeStruct((12, 4, LANES), F32), vec_shape, vec_shape, vec_shape] + _chip_swap_shapes(swap),
        scratch_shapes=[pltpu.VMEM((T, LANES), F32)] + wide + (_chip_swap_sems(ns) if ns else []),
        compiler_params=pltpu.CompilerParams(vmem_limit_bytes=VMEM_LIMIT_MAX, dimension_semantics=("arbitrary", "arbitrary")),
    )(proj_dn, proj_ab, conv_w, a_log, dt_bias, gain, dmix, *kept, *swap)
    return outs[:6], outs[6:]


SBQ = 256


def _group_rms(x, gain):
    first = _iota(x.shape, 1) < 64
    sq = x * x
    ss_a = jnp.sum(jnp.where(first, sq, 0.0), axis=-1, keepdims=True)
    ss_b = jnp.sum(jnp.where(first, 0.0, sq), axis=-1, keepdims=True)
    ms = jnp.where(first, ss_a, ss_b) * (1.0 / 64)
    return x * lax.rsqrt(ms + EPS) * gain


def _sb_stack(q):
    first = _iota((1, LANES), 1) < 64
    return jnp.concatenate([jnp.where(first, q, 0.0), jnp.where(first, 0.0, q)], axis=0)


def _sb_fold(acc):
    return jnp.where(_iota((1, LANES), 1) < 64, acc[:SBQ], acc[SBQ:])


def _sb_logs(q2, k, diag):
    n = SBQ
    z = _mm(q2, k, ((1,), (1,))) * SB_SCALE
    ls_pos = jnp.minimum(z, 0.0) - jnp.log(1.0 + jnp.exp(-jnp.abs(z)))
    l1m = ls_pos - z
    if not diag:
        return ls_pos, l1m, None
    mask = _iota((2 * n, n), 1) < jnp.bitwise_and(_iota((2 * n, n), 0), n - 1)
    return ls_pos, jnp.where(mask, l1m, 0.0), mask


def _sb_weights(ls_pos, l1m, mask, carry):
    w = jnp.exp(ls_pos + (_mm_ones(_tri_ones(SBQ, False), l1m, False) + carry))
    return w if mask is None else jnp.where(mask, w, 0.0)


def _sb_block(q, k, v, carry, diag):
    ls_pos, l1m, mask = _sb_logs(_sb_stack(q), k, diag)
    w = _sb_weights(ls_pos, l1m, mask, carry)
    return _mm(w, v, ((1,), (0,))), carry + jnp.sum(l1m, axis=-1, keepdims=True), _sb_sum_as_rows(l1m)


SB_ROWS = 16


def _sb_sum_as_rows(l1m):
    ones = jnp.ones((SB_ROWS, SBQ), BF16)
    hi, lo = _split(l1m)
    mm = lambda t: lax.dot_general(ones, t, (NT, ((), ())), preferred_element_type=F32)
    return mm(hi) + mm(lo)


def _sb_rows_as_column(rows):
    pick = jnp.where(_iota((SB_ROWS, SBQ), 0) == 0, 1.0, 0.0).astype(BF16)
    hi = rows.astype(BF16)
    rest = rows - hi.astype(F32)
    mid = rest.astype(BF16)
    lo = (rest - mid.astype(F32)).astype(BF16)
    mm = lambda t: lax.dot_general(t, pick, (TN, ((), ())), preferred_element_type=F32)
    return mm(hi) + (mm(mid) + mm(lo))


def _sb_block_bwd(q, k, v, carry, diag, dpv, dcarry):
    q2 = _sb_stack(q)
    ls_pos, l1m, mask = _sb_logs(q2, k, diag)
    w = _sb_weights(ls_pos, l1m, mask, carry)
    dv = _mm(w, dpv, ((0,), (0,)))
    de = _mm(dpv, v, ((1,), (1,))) * w
    dl1m = _mm_ones(_tri_ones(SBQ, True), de, False) + dcarry
    if mask is not None:
        dl1m = jnp.where(mask, dl1m, 0.0)
    sig = jnp.exp(ls_pos)
    dz = (de * (1.0 - sig) - dl1m * sig) * SB_SCALE
    dq = _sb_fold(_mm(dz, k, ((1,), (0,))))
    return dq, _mm(dz, q2, ((0,), (0,))), dv, dcarry + jnp.sum(de, axis=-1, keepdims=True)


_SB_Q, _SB_K, _SB_V = (slice(i * LANES, (i + 1) * LANES) for i in range(3))


def _sb_fwd_call(proj_sb, mix, q_gain, k_gain, B, T, gather=()):
    nblk = T // SBQ
    ng = len(gather)
    nsteps = 2 * B

    def body(*refs):
        x_ref, qg_ref, kg_ref = refs[:3]
        out_ref, carry_ref = refs[4 + ng:6 + ng]
        q_s, k_s = refs[6 + 2 * ng:8 + 2 * ng]
        step_id = 2 * pl.program_id(0) + pl.program_id(1)
        if ng:
            send, forward, finish = _gather_phases(refs[4:4 + ng], refs[6 + ng:6 + 2 * ng], *refs[8 + 2 * ng:])
            pl.when(step_id == 0)(send)
            pl.when(step_id == nsteps - 1)(forward)
        q_s[...] = _group_rms(x_ref[:, _SB_Q], qg_ref[...])
        k_s[...] = _group_rms(x_ref[:, _SB_K], kg_ref[...])

        def qblock(i, _):
            ri = pl.ds(pl.multiple_of(i * SBQ, SBQ), SBQ)
            q = q_s[ri, :]

            def kblock(jj, c):
                j = i - 1 - jj
                rj = pl.ds(pl.multiple_of(j * SBQ, SBQ), SBQ)
                carry_ref[0, 0, i, j] = c[2]
                pv, carry, rows = _sb_block(q, k_s[rj, :], x_ref[rj, _SB_V], c[1], False)
                return c[0] + pv, carry, c[2] + rows

            on_diag = _sb_block(q, k_s[ri, :], x_ref[ri, _SB_V], jnp.zeros((2 * SBQ, 1), F32), True)
            acc, _c, _r = lax.fori_loop(0, i, kblock, on_diag)
            out_ref[ri, :] = _sb_fold(acc)
            return 0

        lax.fori_loop(0, nblk, qblock, 0)
        if ng:
            pl.when(step_id == nsteps - 1)(finish)

    vec = pl.BlockSpec((1, LANES), lambda b, p: (0, 0))
    outs = pl.pallas_call(
        body, name="sb_fwd", grid=(B, 2),
        in_specs=[pl.BlockSpec((T, 3 * LANES), lambda b, p: (b, p)), vec, vec, pl.BlockSpec(memory_space=pl.ANY)] + _any_specs(ng),
        out_specs=[pl.BlockSpec((T, LANES), lambda b, p: (b, DN_W // LANES + p)), _sb_carry_spec(nblk)] + _any_specs(ng),
        out_shape=[jax.ShapeDtypeStruct((B * T, D), F32), jax.ShapeDtypeStruct((B, 2, nblk, nblk, SB_ROWS, 2 * SBQ), F32)]
        + _gather_shapes(gather), input_output_aliases={3: 0},
        scratch_shapes=[pltpu.VMEM((T, LANES), F32)] * 2 + (_gather_sems(ng) if ng else []),
        compiler_params=_params(dimension_semantics=("arbitrary", "arbitrary")),
    )(proj_sb, q_gain, k_gain, mix, *gather)
    return outs[0], outs[1], outs[2:]


def _sb_carry_spec(nblk):
    return pl.BlockSpec((1, 1, nblk, nblk, SB_ROWS, 2 * SBQ), lambda b, p: (b, p, 0, 0, 0, 0))


def _sb_bwd_call(proj_sb, dmix, carries, q_gain, k_gain, B, T):
    nblk = T // SBQ

    def body(x_ref, qg_ref, kg_ref, do_ref, carry_ref, dx_ref, dqg_ref, dkg_ref, q_s, k_s, dq_s, dk_s, dv_s):
        b_i, p = pl.program_id(0), pl.program_id(1)
        qn, q_vjp = jax.vjp(_group_rms, x_ref[:, _SB_Q], qg_ref[...])
        kn, k_vjp = jax.vjp(_group_rms, x_ref[:, _SB_K], kg_ref[...])
        q_s[...], k_s[...] = qn, kn
        dk_s[...] = jnp.zeros_like(dk_s)
        dv_s[...] = jnp.zeros_like(dv_s)

        def qblock(i, _):
            ri = pl.ds(pl.multiple_of(i * SBQ, SBQ), SBQ)
            q = q_s[ri, :]
            dacc = _sb_stack(do_ref[ri, :])

            def kblock(j, c):
                rj = pl.ds(pl.multiple_of(j * SBQ, SBQ), SBQ)
                carry = _sb_rows_as_column(carry_ref[0, 0, i, j])
                dq_j, dk_j, dv_j, dc = _sb_block_bwd(q, k_s[rj, :], x_ref[rj, _SB_V], carry, False, dacc, c[1])
                dk_s[rj, :] += dk_j
                dv_s[rj, :] += dv_j
                return c[0] + dq_j, dc

            dq, dc = lax.fori_loop(0, i, kblock, (jnp.zeros((SBQ, LANES), F32), jnp.zeros((2 * SBQ, 1), F32)))
            dq_i, dk_i, dv_i, _dc = _sb_block_bwd(q, k_s[ri, :], x_ref[ri, _SB_V], jnp.zeros((2 * SBQ, 1), F32), True, dacc, dc)
            dk_s[ri, :] += dk_i
            dv_s[ri, :] += dv_i
            dq_s[ri, :] = dq + dq_i
            return 0

        lax.fori_loop(0, nblk, qblock, 0)
        dq_in, dqg = q_vjp(dq_s[...])
        dk_in, dkg = k_vjp(dk_s[...])
        dx_ref[:, _SB_Q], dx_ref[:, _SB_K], dx_ref[:, _SB_V] = dq_in, dk_in, dv_s[...]

        @pl.when(jnp.logical_and(b_i == 0, p == 0))
        def _():
            dqg_ref[...] = jnp.zeros_like(dqg_ref)
            dkg_ref[...] = jnp.zeros_like(dkg_ref)

        dqg_ref[...] += dqg + pltpu.roll(dqg, 64, 1)
        dkg_ref[...] += dkg + pltpu.roll(dkg, 64, 1)

    M = B * T
    vec = pl.BlockSpec((1, LANES), lambda b, p: (0, 0))
    blk = pl.BlockSpec((T, 3 * LANES), lambda b, p: (b, p))
    big = [pltpu.VMEM((T, LANES), F32)]
    return pl.pallas_call(
        body, name="sb_bwd", grid=(B, 2),
        in_specs=[blk, vec, vec, pl.BlockSpec((T, LANES), lambda b, p: (b, DN_W // LANES + p)), _sb_carry_spec(nblk)],
        out_specs=[blk, vec, vec],
        out_shape=[jax.ShapeDtypeStruct((M, 3 * SB_W), F32)] + [jax.ShapeDtypeStruct((1, LANES), F32)] * 2,
        scratch_shapes=big * 5,
        compiler_params=_params(dimension_semantics=("arbitrary", "arbitrary")),
    )(proj_sb, q_gain, k_gain, dmix, carries)


def _sg_chunk(u, v, gain, w_a, w_b, bias):
    n = 128
    row, col = _iota((n, n), 0), _iota((n, n), 1)
    first = _iota((1, LANES), 1) < 64
    vn = _group_rms(_gelu(v), gain)
    tril = col <= row
    mixed = jnp.where(first, _dot(jnp.where(tril, w_a, 0.0), vn), _dot(jnp.where(tril, w_b, 0.0), vn)) + bias
    return _gelu(u) * mixed


_SG_U, _SG_V = slice(0, LANES), slice(LANES, 2 * LANES)


def _sg_fwd_call(proj_sg, mix, gain, sg_w, bias, B, T):
    nchunk = T // 128

    def body(x_ref, g_ref, wa_ref, wb_ref, bias_ref, mix_ref, out_ref):
        del mix_ref

        def step(i, _):
            r = pl.ds(pl.multiple_of(i * 128, 128), 128)
            out_ref[r, :] = _sg_chunk(x_ref[r, _SG_U], x_ref[r, _SG_V], g_ref[...], wa_ref[0], wb_ref[0], bias_ref[...])
            return 0

        lax.fori_loop(0, nchunk, step, 0)

    return pl.pallas_call(
        body, name="sg_fwd", grid=(B, 2),
        in_specs=[pl.BlockSpec((T, 2 * LANES), lambda b, p: (b, p)), pl.BlockSpec((1, LANES), lambda b, p: (0, p)),
                  pl.BlockSpec((1, 128, 128), lambda b, p: (2 * p, 0, 0)), pl.BlockSpec((1, 128, 128), lambda b, p: (2 * p + 1, 0, 0)),
                  pl.BlockSpec((128, LANES), lambda b, p: (0, p)), pl.BlockSpec(memory_space=pl.ANY)],
        out_specs=pl.BlockSpec((T, LANES), lambda b, p: (b, (DN_W + SB_W) // LANES + p)),
        out_shape=jax.ShapeDtypeStruct((B * T, D), F32), input_output_aliases={5: 0},
        compiler_params=_params(dimension_semantics=("arbitrary", "arbitrary")),
    )(proj_sg, gain, sg_w, sg_w, bias, mix)


def _sg_bwd_call(proj_sg, dmix, gain, sg_w, bias, B, T):
    nchunk = T // 128

    def body(x_ref, g_ref, wa_ref, wb_ref, bias_ref, do_ref, dx_ref, dg_ref, dw_ref, db_ref):
        p, b_i = pl.program_id(0), pl.program_id(1)

        def step(i, c):
            r = pl.ds(pl.multiple_of(i * 128, 128), 128)
            _, vjp = jax.vjp(_sg_chunk, x_ref[r, _SG_U], x_ref[r, _SG_V], g_ref[...], wa_ref[0], wb_ref[0], bias_ref[...])
            du, dv, dg, dwa, dwb, dbias = vjp(do_ref[r, :])
            dx_ref[r, _SG_U], dx_ref[r, _SG_V] = du, dv
            return c[0] + dg, c[1] + dwa, c[2] + dwb, c[3] + dbias

        z = jnp.zeros((128, 128), F32)
        dg, dwa, dwb, dbias = lax.fori_loop(0, nchunk, step, (jnp.zeros((1, LANES), F32), z, z, z))
        lane = _iota((1, LANES), 1)
        first = lane < 64
        s_a = jnp.sum(jnp.where(first, dbias, 0.0), axis=-1, keepdims=True)
        s_b = jnp.sum(jnp.where(first, 0.0, dbias), axis=-1, keepdims=True)
        dbg = jnp.where(lane == 2 * p, s_a, 0.0) + jnp.where(lane == 2 * p + 1, s_b, 0.0)

        @pl.when(b_i == 0)
        def _():
            dg_ref[...] = jnp.zeros_like(dg_ref)
            dw_ref[...] = jnp.zeros_like(dw_ref)

        @pl.when(jnp.logical_and(b_i == 0, p == 0))
        def _():
            db_ref[...] = jnp.zeros_like(db_ref)

        dg_ref[...] += dg
        dw_ref[0] += dwa
        dw_ref[1] += dwb
        db_ref[...] += dbg

    M = B * T
    blk = pl.BlockSpec((T, 2 * LANES), lambda p, b: (b, p))
    return pl.pallas_call(
        body, name="sg_bwd", grid=(2, B),
        in_specs=[blk, pl.BlockSpec((1, LANES), lambda p, b: (0, p)),
                  pl.BlockSpec((1, 128, 128), lambda p, b: (2 * p, 0, 0)), pl.BlockSpec((1, 128, 128), lambda p, b: (2 * p + 1, 0, 0)),
                  pl.BlockSpec((128, LANES), lambda p, b: (0, p)),
                  pl.BlockSpec((T, LANES), lambda p, b: (b, (DN_W + SB_W) // LANES + p))],
        out_specs=[blk, pl.BlockSpec((1, LANES), lambda p, b: (0, p)), pl.BlockSpec((2, 128, 128), lambda p, b: (p, 0, 0)),
                   pl.BlockSpec((128, LANES), lambda p, b: (0, 0))],
        out_shape=[jax.ShapeDtypeStruct((M, 2 * SG_W), F32), jax.ShapeDtypeStruct((1, SG_W), F32),
                   jax.ShapeDtypeStruct((4, 128, 128), F32), jax.ShapeDtypeStruct((128, LANES), F32)],
        compiler_params=_params(dimension_semantics=("arbitrary", "arbitrary")),
    )(proj_sg, gain, sg_w, sg_w, bias, dmix)


def _row_tile(m, most=512):
    return min(m, most)


def _in_proj_call(x, gain, wt):
    m = x.shape[0]
    tm = _row_tile(m)

    def body(x_ref, g_ref, wt_ref, *out_refs):
        h = _rms(x_ref[...], g_ref[...]).astype(BF16)
        out_refs[-1][...] = h
        for (off, width), out_ref in zip(SECTIONS, out_refs):
            out_ref[...] = lax.dot_general(h, wt_ref[off:off + width, :], (((1,), (1,)), ((), ())), preferred_element_type=F32)

    rows = lambda width: pl.BlockSpec((tm, width), lambda i: (i, 0))
    return pl.pallas_call(
        body, name="in_proj", grid=(m // tm,),
        in_specs=[rows(D), pl.BlockSpec((1, D), lambda i: (0, 0)),
                  pl.BlockSpec((NPACK, D), lambda i: (0, 0), pipeline_mode=pl.Buffered(1))],
        out_specs=[rows(w) for _, w in SECTIONS] + [rows(D)],
        out_shape=[jax.ShapeDtypeStruct((m, w), F32) for _, w in SECTIONS] + [jax.ShapeDtypeStruct((m, D), BF16)],
        compiler_params=_params(dimension_semantics=("arbitrary",)),
    )(x, gain, wt)


def _in_proj_bwd_call(dsections, wt, x, gain, dres, swap=()):
    m = x.shape[0]
    tm = _row_tile(m)
    nsec, ns = len(SECTIONS), len(swap)

    def body(*refs):
        ds_refs = refs[:nsec]
        wt_ref, x_ref, g_ref, dres_ref = refs[nsec:nsec + 4]
        dx_ref, dg_ref = refs[nsec + 4 + ns:nsec + 6 + ns]
        step = pl.program_id(0)
        if ns:
            send, finish = _chip_swap_phases(refs[nsec + 4:nsec + 4 + ns], refs[nsec + 6 + ns:nsec + 6 + 2 * ns],
                                             *refs[nsec + 6 + 2 * ns:])
            pl.when(step == 0)(send)

        @pl.when(step == 0)
        def _():
            dg_ref[...] = jnp.zeros_like(dg_ref)

        dh = 0.0
        for (off, width), ds_ref in zip(SECTIONS, ds_refs):
            dh = dh + jnp.dot(ds_ref[...].astype(BF16), wt_ref[off:off + width, :], preferred_element_type=F32)
        _, vjp = jax.vjp(_rms, x_ref[...], g_ref[...])
        dx, dg = vjp(dh)
        dx_ref[...] = dres_ref[...] + dx
        dg_ref[...] += dg
        if ns:
            pl.when(step == m // tm - 1)(finish)

    rows = lambda width: pl.BlockSpec((tm, width), lambda i: (i, 0))
    outs = pl.pallas_call(
        body, name="in_proj_bwd", grid=(m // tm,),
        in_specs=[rows(w) for _, w in SECTIONS] + [pl.BlockSpec((NPACK, D), lambda i: (0, 0), pipeline_mode=pl.Buffered(1)),
                                                   rows(D), pl.BlockSpec((1, D), lambda i: (0, 0)), rows(D)] + _any_specs(ns),
        out_specs=[rows(D), pl.BlockSpec((1, D), lambda i: (0, 0))] + _any_specs(ns),
        out_shape=[jax.ShapeDtypeStruct((m, D), F32), jax.ShapeDtypeStruct((1, D), F32)] + _chip_swap_shapes(swap),
        scratch_shapes=_chip_swap_sems(ns) if ns else [],
        compiler_params=_params(dimension_semantics=("arbitrary",)),
    )(*dsections, wt, x, gain, dres, *swap)
    return outs[:2], outs[2:]


def _in_proj_grad_call(dsections, h):
    m = h.shape[0]
    tm = min(m, 256)

    def body(*refs):
        ds_refs, (h_ref, out_ref) = refs[:len(SECTIONS)], refs[len(SECTIONS):]

        @pl.when(pl.program_id(0) == 0)
        def _():
            out_ref[...] = jnp.zeros_like(out_ref)

        for (off, width), ds_ref in zip(SECTIONS, ds_refs):
            out_ref[off:off + width, :] += lax.dot_general(ds_ref[...].astype(BF16), h_ref[...], (((0,), (0,)), ((), ())),
                                                           preferred_element_type=F32)

    rows = lambda width: pl.BlockSpec((tm, width), lambda i: (i, 0))
    return pl.pallas_call(
        body, name="grad_w_in", grid=(m // tm,),
        in_specs=[rows(w) for _, w in SECTIONS] + [rows(D)],
        out_specs=pl.BlockSpec((NPACK, D), lambda i: (0, 0), pipeline_mode=pl.Buffered(1)),
        out_shape=jax.ShapeDtypeStruct((NPACK, D), F32),
        compiler_params=_params(dimension_semantics=("arbitrary",)),
    )(*dsections, h)


def _packed_column_of():
    t = np.full(NPACK, -1, np.int64)
    lanes = np.arange(LANES)
    for pair in range(2):
        for s in range(4):
            t[DN_OFF + pair * 1024 + s * 256 + np.arange(256)] = s * DN_W + pair * 256 + np.arange(256)
        for s in range(3):
            t[SB_OFF + pair * 384 + s * LANES + lanes] = 2056 + s * SB_W + pair * LANES + lanes
        for s in range(2):
            t[SG_OFF + pair * 256 + s * LANES + lanes] = 2056 + 3 * SB_W + s * SG_W + pair * LANES + lanes
    t[AB_OFF + np.arange(2 * NH)] = 4 * DN_W + np.arange(2 * NH)
    return t


def _row_tables():
    col = _packed_column_of()
    fwd = np.where(col >= 0, (col // IN_SHARD) * IN_SHARD_PAD + col % IN_SHARD, -1)
    packed_of = np.full(IN_DIM, -1, np.int64)
    packed_of[col[col >= 0]] = np.nonzero(col >= 0)[0]
    r = np.arange(NDEV * IN_SHARD_PAD)
    inside = r % IN_SHARD_PAD < IN_SHARD
    back = np.where(inside, packed_of[np.minimum((r // IN_SHARD_PAD) * IN_SHARD + r % IN_SHARD_PAD, IN_DIM - 1)], -1)
    return fwd, back


def _row_perm_call(src, table, name):
    n_out = table.shape[0]
    touched = [sorted(set((table[b * 128:(b + 1) * 128][table[b * 128:(b + 1) * 128] >= 0] // 128).tolist()))
               for b in range(n_out // 128)]

    def body(tbl_ref, src_ref, out_ref):
        lane = _iota((1, LANES), 1)
        for b, blocks in enumerate(touched):
            want = tbl_ref[b * 128:(b + 1) * 128, :]
            acc = jnp.zeros((128, D), F32)
            for sb in blocks:
                pick = jnp.where(want == sb * 128 + lane, 1.0, 0.0).astype(BF16)
                acc = acc + jnp.dot(pick, src_ref[sb * 128:(sb + 1) * 128, :].astype(BF16), preferred_element_type=F32)
            out_ref[b * 128:(b + 1) * 128, :] = acc.astype(BF16)

    return pl.pallas_call(
        body, name=name, out_shape=jax.ShapeDtypeStruct((n_out, D), BF16),
        in_specs=[pl.BlockSpec(memory_space=pltpu.VMEM)] * 2, out_specs=pl.BlockSpec(memory_space=pltpu.VMEM),
        compiler_params=_params(),
    )(jnp.asarray(table.reshape(-1, 1), jnp.int32), src)


def _out_proj_call(a, w, res):
    m, k = a.shape
    n = w.shape[1]
    tm = _row_tile(m)

    def body(a_ref, w_ref, res_ref, out_ref):
        out_ref[...] = res_ref[...] + jnp.dot(a_ref[...].astype(BF16), w_ref[...], preferred_element_type=F32)

    return pl.pallas_call(
        body, name="out_proj", grid=(m // tm,),
        in_specs=[pl.BlockSpec((tm, k), lambda i: (i, 0)), pl.BlockSpec((k, n), lambda i: (0, 0)),
                  pl.BlockSpec((tm, n), lambda i: (i, 0))],
        out_specs=pl.BlockSpec((tm, n), lambda i: (i, 0)),
        out_shape=jax.ShapeDtypeStruct((m, n), F32),
        compiler_params=_params(dimension_semantics=("arbitrary",)),
    )(a, w, res)


def _ffn_specs(tm):
    return [pl.BlockSpec((1, D, FF_SHARD), lambda i, j: (j, 0, 0)), pl.BlockSpec((FF_SHARD, D), lambda i, j: (j, 0))]


def _ffn_fwd_call(x, gain, w1, w2):
    m = x.shape[0]
    tm = _row_tile(m, 1024)

    def body(x_ref, g_ref, w1_ref, w2_ref, out_ref, h_s, acc_s):
        j = pl.program_id(1)

        @pl.when(j == 0)
        def _():
            h_s[...] = _rms(x_ref[...], g_ref[...]).astype(BF16)
            acc_s[...] = jnp.zeros_like(acc_s)

        a = jnp.maximum(jnp.dot(h_s[...], w1_ref[0], preferred_element_type=F32), 0.0)
        acc_s[...] += jnp.dot((a * a).astype(BF16), w2_ref[...], preferred_element_type=F32)

        @pl.when(j == NDEV - 1)
        def _():
            out_ref[...] = x_ref[...] + acc_s[...]

    return pl.pallas_call(
        body, name="ffn_fwd", grid=(m // tm, NDEV),
        in_specs=[pl.BlockSpec((tm, D), lambda i, j: (i, 0)), pl.BlockSpec((1, D), lambda i, j: (0, 0))] + _ffn_specs(tm),
        out_specs=pl.BlockSpec((tm, D), lambda i, j: (i, 0)),
        out_shape=jax.ShapeDtypeStruct((m, D), F32),
        scratch_shapes=[pltpu.VMEM((tm, D), BF16), pltpu.VMEM((tm, D), F32)],
        compiler_params=_params(dimension_semantics=("arbitrary", "arbitrary")),
    )(x, gain, w1, w2)


def _ffn_bwd_call(x, dy, gain, w1, w2, swap=()):
    m = x.shape[0]
    tm = _row_tile(m, 1024)
    ns = len(swap)

    def body(*refs):
        x_ref, dy_ref, g_ref, w1_ref, w2_ref = refs[:5]
        dx_ref, da_ref, r_ref, h_ref, dg_ref = refs[5 + ns:10 + ns]
        acc_s = refs[10 + 2 * ns]
        i, j = pl.program_id(0), pl.program_id(1)
        if ns:
            send, finish = _sibling_swap_phases(refs[5:5 + ns], refs[10 + ns:10 + 2 * ns], *refs[11 + 2 * ns:])
            pl.when(jnp.logical_and(i == 0, j == 0))(send)

        @pl.when(j == 0)
        def _():
            h_ref[...] = _rms(x_ref[...], g_ref[...]).astype(BF16)
            acc_s[...] = jnp.zeros_like(acc_s)

        @pl.when(jnp.logical_and(i == 0, j == 0))
        def _():
            dg_ref[...] = jnp.zeros_like(dg_ref)

        a = jnp.maximum(jnp.dot(h_ref[...], w1_ref[0], preferred_element_type=F32), 0.0)
        r_ref[...] = (a * a).astype(BF16)
        dr = lax.dot_general(dy_ref[...].astype(BF16), w2_ref[...], (((1,), (1,)), ((), ())), preferred_element_type=F32)
        da = (dr * (2.0 * a)).astype(BF16)
        da_ref[...] = da
        acc_s[...] += lax.dot_general(da, w1_ref[0], (((1,), (1,)), ((), ())), preferred_element_type=F32)

        @pl.when(j == NDEV - 1)
        def _():
            _, vjp = jax.vjp(_rms, x_ref[...], g_ref[...])
            dx, dg = vjp(acc_s[...])
            dx_ref[...] = dy_ref[...] + dx
            dg_ref[...] += dg

        if ns:
            pl.when(jnp.logical_and(i == m // tm - 1, j == NDEV - 1))(finish)

    outs = pl.pallas_call(
        body, name="ffn_bwd", grid=(m // tm, NDEV),
        in_specs=[pl.BlockSpec((tm, D), lambda i, j: (i, 0)), pl.BlockSpec((tm, D), lambda i, j: (i, 0)),
                  pl.BlockSpec((1, D), lambda i, j: (0, 0))] + _ffn_specs(tm) + _any_specs(ns),
        out_specs=[pl.BlockSpec((tm, D), lambda i, j: (i, 0)), pl.BlockSpec((tm, FF_SHARD), lambda i, j: (i, j)),
                   pl.BlockSpec((tm, FF_SHARD), lambda i, j: (i, j)), pl.BlockSpec((tm, D), lambda i, j: (i, 0)),
                   pl.BlockSpec((1, D), lambda i, j: (0, 0))] + _any_specs(ns),
        out_shape=[jax.ShapeDtypeStruct((m, D), F32), jax.ShapeDtypeStruct((m, DFF), BF16), jax.ShapeDtypeStruct((m, DFF), BF16),
                   jax.ShapeDtypeStruct((m, D), BF16), jax.ShapeDtypeStruct((1, D), F32)] + _sibling_swap_shapes(swap),
        scratch_shapes=[pltpu.VMEM((tm, D), F32)] + (_sibling_swap_sems(ns) if ns else []),
        compiler_params=_params(dimension_semantics=("arbitrary", "arbitrary")),
    )(x, dy, gain, w1, w2, *swap)
    return outs[:5], outs[5:]


def _mm_nt_call(a, b, name):
    m, k = a.shape
    n = b.shape[0]
    tm = _row_tile(m)

    def body(a_ref, b_ref, out_ref):
        out_ref[...] = lax.dot_general(a_ref[...].astype(BF16), b_ref[...].astype(BF16), (((1,), (1,)), ((), ())),
                                       preferred_element_type=F32)

    return pl.pallas_call(
        body, name=name, grid=(m // tm,),
        in_specs=[pl.BlockSpec((tm, k), lambda i: (i, 0)), pl.BlockSpec((n, k), lambda i: (0, 0))],
        out_specs=pl.BlockSpec((tm, n), lambda i: (i, 0)),
        out_shape=jax.ShapeDtypeStruct((m, n), F32),
        compiler_params=_params(dimension_semantics=("arbitrary",)),
    )(a, b)


def _mm_tn_call(a, b, name, col_shards=False):
    m, k = a.shape
    n = b.shape[1]
    tm, tk = _row_tile(m, 1024), min(k, 1024)
    tn = n // NDEV if col_shards else min(n, 1024)

    def body(a_ref, b_ref, out_ref, acc_s):
        s = pl.program_id(2)

        @pl.when(s == 0)
        def _():
            acc_s[...] = jnp.zeros_like(acc_s)

        acc_s[...] += lax.dot_general(a_ref[...].astype(BF16), b_ref[...].astype(BF16), (((0,), (0,)), ((), ())),
                                      preferred_element_type=F32)

        @pl.when(s == m // tm - 1)
        def _():
            out_ref[...] = acc_s[...].astype(BF16).reshape(out_ref.shape)

    if col_shards:
        out_spec, out_shape = pl.BlockSpec((1, tk, tn), lambda i, j, s: (j, i, 0)), (NDEV, k, tn)
    else:
        out_spec, out_shape = pl.BlockSpec((tk, tn), lambda i, j, s: (i, j)), (k, n)
    return pl.pallas_call(
        body, name=name, grid=(k // tk, n // tn, m // tm),
        in_specs=[pl.BlockSpec((tm, tk), lambda i, j, s: (s, i)), pl.BlockSpec((tm, tn), lambda i, j, s: (s, j))],
        out_specs=out_spec, out_shape=jax.ShapeDtypeStruct(out_shape, BF16),
        scratch_shapes=[pltpu.VMEM((tk, tn), F32)],
        compiler_params=_params(dimension_semantics=("arbitrary", "arbitrary", "arbitrary")),
    )(a, b)


def _loss_call(y, target):
    m = y.shape[0]
    tm = _row_tile(m)

    def body(y_ref, t_ref, loss_ref, dy_ref):
        @pl.when(pl.program_id(0) == 0)
        def _():
            loss_ref[...] = jnp.zeros_like(loss_ref)

        err = y_ref[...] - t_ref[...]
        dy_ref[...] = err * (1.0 / D)
        per_row = jnp.mean(err * err, axis=-1, keepdims=True)
        loss_ref[...] += jnp.broadcast_to(0.5 * jnp.sum(per_row, axis=0, keepdims=True), (1, LANES))

    return pl.pallas_call(
        body, name="loss", grid=(m // tm,),
        in_specs=[pl.BlockSpec((tm, D), lambda i: (i, 0))] * 2,
        out_specs=[pl.BlockSpec((1, LANES), lambda i: (0, 0)), pl.BlockSpec((tm, D), lambda i: (i, 0))],
        out_shape=[jax.ShapeDtypeStruct((1, LANES), F32), jax.ShapeDtypeStruct((m, D), F32)],
        compiler_params=_params(dimension_semantics=("arbitrary",)),
    )(y, target)


def _adamw_call(w, g, m, v, name):
    shape = w.shape
    cols = shape[-1] if w.ndim > 1 else w.size
    rows = w.size // cols
    tr = rows if (rows <= 512 or rows % 512) else 512
    c1, c2 = 1.0 - ADAM_B1 ** ADAM_STEP, 1.0 - ADAM_B2 ** ADAM_STEP

    def body(w_ref, g_ref, m_ref, v_ref, d_ref, nm_ref, nv_ref):
        g_ = g_ref[...]
        nm = ADAM_B1 * m_ref[...] + (1.0 - ADAM_B1) * g_
        nv = ADAM_B2 * v_ref[...] + (1.0 - ADAM_B2) * (g_ * g_)
        d_ref[...] = -ADAM_LR * ((nm / c1) / (jnp.sqrt(nv / c2) + ADAM_EPS) + ADAM_WD * w_ref[...])
        nm_ref[...], nv_ref[...] = nm, nv

    spec = pl.BlockSpec((tr, cols), lambda i: (i, 0))
    outs = pl.pallas_call(
        body, name=name, grid=(rows // tr,), in_specs=[spec] * 4, out_specs=[spec] * 3,
        out_shape=[jax.ShapeDtypeStruct((rows, cols), F32)] * 3,
        compiler_params=_params(dimension_semantics=("arbitrary",)),
    )(*(t.reshape(rows, cols) for t in (w, g, m, v)))
    return tuple(o.reshape(shape) for o in outs)


def _sum_tile(rows):
    for cand in (2048, 1024, 512, 256, 128):
        if rows > cand and rows % cand == 0:
            return cand
    return rows


def _pair_sum_call(g, got, core, name):
    rows, cols = g.shape[1:]
    tr = _sum_tile(rows)

    def body(core_ref, g_ref, got_ref, out_ref):
        del core_ref
        out_ref[...] = (g_ref[...].astype(F32) + got_ref[...].astype(F32)).astype(BF16)

    grid_spec = pltpu.PrefetchScalarGridSpec(
        num_scalar_prefetch=1, grid=(4, rows // tr),
        in_specs=[pl.BlockSpec((1, tr, cols), lambda ch, t, core_ref: (2 * ch + core_ref[0], t, 0)),
                  pl.BlockSpec((1, tr, cols), lambda ch, t, core_ref: (ch, t, 0))],
        out_specs=pl.BlockSpec((1, tr, cols), lambda ch, t, core_ref: (ch, t, 0)))
    return pl.pallas_call(
        body, name=name, grid_spec=grid_spec, out_shape=jax.ShapeDtypeStruct((4, rows, cols), BF16),
        compiler_params=_params(dimension_semantics=("arbitrary", "arbitrary")),
    )(jnp.asarray(core, jnp.int32).reshape(1), g, got)


def _sum_call(parts, out_dtype, name):
    rows, cols = parts[0][0].shape[1:]
    tr = _sum_tile(rows)
    index = jnp.stack([jnp.asarray(i, jnp.int32) for _, i in parts])

    def body(idx_ref, *refs):
        del idx_ref
        acc = refs[0][0].astype(F32)
        for r in refs[1:-1]:
            acc = acc + r[0].astype(F32)
        refs[-1][...] = acc.astype(out_dtype)

    grid_spec = pltpu.PrefetchScalarGridSpec(
        num_scalar_prefetch=1, grid=(rows // tr,),
        in_specs=[pl.BlockSpec((1, tr, cols), lambda t, idx, n=n: (idx[n], t, 0)) for n in range(len(parts))],
        out_specs=pl.BlockSpec((tr, cols), lambda t, idx: (t, 0)))
    return pl.pallas_call(
        body, name=name, grid_spec=grid_spec, out_shape=jax.ShapeDtypeStruct((rows, cols), out_dtype),
        compiler_params=_params(dimension_semantics=("arbitrary",)),
    )(index, *(a for a, _ in parts))


def _place():
    return lax.axis_index("x"), lax.axis_index("y"), lax.axis_index("c")


def _any_specs(n):
    return [pl.BlockSpec(memory_space=pl.ANY)] * n


def _all_gather_call(xs, name):
    n = len(xs)

    def body(*refs):
        for phase in _gather_phases(refs[:n], refs[n:2 * n], *refs[2 * n:]):
            phase()

    return pl.pallas_call(
        body, name=name, in_specs=_any_specs(n), out_specs=_any_specs(n),
        out_shape=_gather_shapes(xs), scratch_shapes=_gather_sems(n),
    )(*xs)


def _gather_shapes(xs):
    return [jax.ShapeDtypeStruct((NDEV,) + x.shape, x.dtype) for x in xs]


def _gather_sems(n):
    return [pltpu.SemaphoreType.DMA((7 * n,)), pltpu.SemaphoreType.DMA((7 * n,)), pltpu.SemaphoreType.DMA((n,))]


def _gather_phases(x_refs, out_refs, send_sems, recv_sems, local_sems):
    n = len(x_refs)
    ax, ay, ac = _place()
    me, sibling = (ax, ay, ac), (ax, ay, 1 - ac)
    chips = [(1 - ax, ay), (ax, 1 - ay), (1 - ax, 1 - ay)]

    def copy(a, k, block, to, src=None):
        slot = out_refs[a].at[4 * block[0] + 2 * block[1] + block[2]]
        return pltpu.make_async_remote_copy(
            src_ref=slot if src is None else src, dst_ref=slot,
            send_sem=send_sems.at[7 * a + k], recv_sem=recv_sems.at[7 * a + k], device_id=to, device_id_type=MESH)

    local = [pltpu.make_async_copy(x_refs[a], out_refs[a].at[4 * ax + 2 * ay + ac], local_sems.at[a]) for a in range(n)]
    first = []
    for a in range(n):
        first.append(copy(a, 0, me, sibling, src=x_refs[a]))
        first += [copy(a, 1 + j, me, (*chip, ac), src=x_refs[a]) for j, chip in enumerate(chips)]
    passed = [copy(a, 4 + j, (*chip, ac), sibling) for j, chip in enumerate(chips) for a in range(n)]

    def send():
        for cp in local + first:
            cp.start()

    def forward():
        for j, chip in enumerate(chips):
            for a in range(n):
                copy(a, 1 + j, (*chip, ac), me).wait_recv()
                passed[j * n + a].start()

    def finish():
        for a in range(n):
            copy(a, 0, sibling, me).wait_recv()
            for j, chip in enumerate(chips):
                copy(a, 4 + j, (*chip, 1 - ac), me).wait_recv()
        for cp in first + passed:
            cp.wait_send()
        for cp in local:
            cp.wait()

    return send, forward, finish


def _swap_sibling_call(xs, name):
    n = len(xs)

    def body(*refs):
        for phase in _sibling_swap_phases(refs[:n], refs[n:2 * n], *refs[2 * n:]):
            phase()

    return pl.pallas_call(
        body, name=name, in_specs=_any_specs(n), out_specs=_any_specs(n),
        out_shape=_sibling_swap_shapes(xs), scratch_shapes=_sibling_swap_sems(n),
    )(*xs)


def _sibling_swap_shapes(xs):
    return [jax.ShapeDtypeStruct((4,) + x.shape[1:], x.dtype) for x in xs]


def _sibling_swap_sems(n):
    return [pltpu.SemaphoreType.DMA((n,)), pltpu.SemaphoreType.DMA((n,))]


def _sibling_swap_phases(x_refs, out_refs, send_sems, recv_sems):
    ax, ay, ac = _place()
    sibling = (ax, ay, 1 - ac)

    def send():
        for a, (x_ref, out_ref) in enumerate(zip(x_refs, out_refs)):
            for chip in range(4):
                pltpu.make_async_remote_copy(src_ref=x_ref.at[2 * chip + 1 - ac], dst_ref=out_ref.at[chip],
                                             send_sem=send_sems.at[a], recv_sem=recv_sems.at[a],
                                             device_id=sibling, device_id_type=MESH).start()

    def finish():
        for a, (x_ref, out_ref) in enumerate(zip(x_refs, out_refs)):
            pltpu.make_async_remote_copy(src_ref=x_ref.at[pl.ds(0, 4)], dst_ref=out_ref, send_sem=send_sems.at[a],
                                         recv_sem=recv_sems.at[a], device_id=sibling, device_id_type=MESH).wait()

    return send, finish


def _swap_chips_call(xs, name):
    n = len(xs)

    def body(*refs):
        for phase in _chip_swap_phases(refs[:n], refs[n:2 * n], *refs[2 * n:]):
            phase()

    return pl.pallas_call(
        body, name=name, in_specs=_any_specs(n), out_specs=_any_specs(n),
        out_shape=_chip_swap_shapes(xs), scratch_shapes=_chip_swap_sems(n),
    )(*xs)


def _chip_swap_shapes(xs):
    return [jax.ShapeDtypeStruct((3,) + x.shape[1:], x.dtype) for x in xs]


def _chip_swap_sems(n):
    return [pltpu.SemaphoreType.DMA((3 * n,)), pltpu.SemaphoreType.DMA((3 * n,))]


def _chip_swap_phases(x_refs, out_refs, send_sems, recv_sems):
    ax, ay, ac = _place()
    chips = [(1 - ax, ay), (ax, 1 - ay), (1 - ax, 1 - ay)]
    copies = [pltpu.make_async_remote_copy(src_ref=x_refs[a].at[2 * cx + cy], dst_ref=out_refs[a].at[j],
                                           send_sem=send_sems.at[3 * a + j], recv_sem=recv_sems.at[3 * a + j],
                                           device_id=(cx, cy, ac), device_id_type=MESH)
              for a in range(len(x_refs)) for j, (cx, cy) in enumerate(chips)]

    def send():
        for cp in copies:
            cp.start()

    def finish():
        for cp in copies:
            cp.wait()

    return send, finish


def _reduce_begin(gs, name):
    ac = lax.axis_index("c")
    got = _swap_sibling_call(gs, name + "_d2d")
    return got, [_pair_sum_call(g, t, ac, f"{name}_pair{a}") for a, (g, t) in enumerate(zip(gs, got))]


def _reduce_end(gs, got, from_chips, name):
    ax, ay, ac = _place()
    me, my_chip = 4 * ax + 2 * ay + ac, 2 * ax + ay
    return [_sum_call([(g, me), (t, my_chip), (f, 0), (f, 1), (f, 2)], F32, f"{name}_total{a}")
            for a, (g, t, f) in enumerate(zip(gs, got, from_chips))]


SMALL = ("norm1_g", "conv_w", "a_log", "dt_bias", "dn_out_g", "sb_q_g", "sb_k_g", "sg_v_g", "sg_w", "sg_b", "norm2_g")
WEIGHTS = ("norm1_g", "w_in", "conv_w", "a_log", "dt_bias", "dn_out_g", "sb_q_g", "sb_k_g", "sg_v_g", "sg_w", "sg_b",
           "w_out", "norm2_g", "w_ff1", "w_ff2")
SMALL_SHAPE = {"norm1_g": (D,), "conv_w": (4, 3 * DN_W), "a_log": (NH,), "dt_bias": (NH,), "dn_out_g": (128,), "sb_q_g": (64,),
               "sb_k_g": (64,), "sg_v_g": (SG_W,), "sg_w": (NH, 128, 128), "sg_b": (NH, 128), "norm2_g": (D,)}


def _size(shape):
    n = 1
    for s in shape:
        n *= s
    return n


def _to_rows(flat, multiple):
    pad = (-flat.shape[0]) % (LANES * multiple)
    return jnp.pad(flat, (0, pad)).reshape(-1, LANES)


def _conv_by_pair(conv):
    return conv.reshape(4, 3, 2, 256).transpose(0, 2, 1, 3).reshape(4, 3 * DN_W)


def kernel(x, norm1_g, w_in, conv_w, a_log, dt_bias, dn_out_g, sb_q_g, sb_k_g, sg_v_g, sg_w, sg_b, w_out, norm2_g, w_ff1, w_ff2, loss_target, m_norm1_g, m_w_in, m_conv_w, m_a_log, m_dt_bias, m_dn_out_g, m_sb_q_g, m_sb_k_g, m_sg_v_g, m_sg_w, m_sg_b, m_w_out, m_norm2_g, m_w_ff1, m_w_ff2, v_norm1_g, v_w_in, v_conv_w, v_a_log, v_dt_bias, v_dn_out_g, v_sb_q_g, v_sb_k_g, v_sg_v_g, v_sg_w, v_sg_b, v_w_out, v_norm2_g, v_w_ff1, v_w_ff2):
    given = dict(norm1_g=norm1_g, w_in=w_in, conv_w=conv_w, a_log=a_log, dt_bias=dt_bias, dn_out_g=dn_out_g, sb_q_g=sb_q_g,
                 sb_k_g=sb_k_g, sg_v_g=sg_v_g, sg_w=sg_w, sg_b=sg_b, w_out=w_out, norm2_g=norm2_g, w_ff1=w_ff1, w_ff2=w_ff2)
    mom = dict(norm1_g=m_norm1_g, w_in=m_w_in, conv_w=m_conv_w, a_log=m_a_log, dt_bias=m_dt_bias, dn_out_g=m_dn_out_g,
               sb_q_g=m_sb_q_g, sb_k_g=m_sb_k_g, sg_v_g=m_sg_v_g, sg_w=m_sg_w, sg_b=m_sg_b, w_out=m_w_out, norm2_g=m_norm2_g,
               w_ff1=m_w_ff1, w_ff2=m_w_ff2)
    var = dict(norm1_g=v_norm1_g, w_in=v_w_in, conv_w=v_conv_w, a_log=v_a_log, dt_bias=v_dt_bias, dn_out_g=v_dn_out_g,
               sb_q_g=v_sb_q_g, sb_k_g=v_sb_k_g, sg_v_g=v_sg_v_g, sg_w=v_sg_w, sg_b=v_sg_b, w_out=v_w_out, norm2_g=v_norm2_g,
               w_ff1=v_w_ff1, w_ff2=v_w_ff2)
    B, T, _ = x.shape
    M = B * T
    ax, ay, ac = _place()
    me = 4 * ax + 2 * ay + ac
    table_fwd, table_back = _row_tables()

    send = []
    for l in range(2):
        w_in_t = jnp.pad(w_in[l].T, ((0, IN_SHARD_PAD - IN_SHARD), (0, 0)))
        send.append([w_in_t.astype(BF16), w_out[l].astype(BF16), w_ff1[l].astype(BF16), w_ff2[l].astype(BF16)])
    first_in, conv_rows = _all_gather_call([send[0][0], _to_rows(conv_w.reshape(-1), 8)], "gather_first")
    conv_full = conv_rows.reshape(NDEV, -1)[:, :conv_w.size].reshape(NDEV, 2, 4, -1).transpose(1, 2, 0, 3).reshape(2, 4, 3 * DN_W)
    gathered = [[first_in, None, None, None], [None] * 4]

    pad_vec = lambda v: jnp.zeros((1, LANES), F32).at[0, :v.shape[0]].set(v)
    layer = []
    for l in range(2):
        layer.append(dict(
            g1=norm1_g[l].reshape(1, D), g2=norm2_g[l].reshape(1, D), conv=_conv_by_pair(conv_full[l]),
            a_log=pad_vec(a_log[l]), dt_bias=pad_vec(dt_bias[l]), dn_g=dn_out_g[l].reshape(1, LANES),
            sb_qg=jnp.tile(sb_q_g[l], 2).reshape(1, LANES), sb_kg=jnp.tile(sb_k_g[l], 2).reshape(1, LANES),
            sg_g=sg_v_g[l].reshape(1, SG_W), sg_w=sg_w[l], sg_bias=jnp.repeat(sg_b[l].T, 64, axis=1)))

    cur = x.reshape(M, D)
    saved = []
    for l, p in enumerate(layer):
        p["wt"] = _row_perm_call(gathered[l][0].reshape(NDEV * IN_SHARD_PAD, D), table_fwd, "pack_w_in")
        p_dn, p_sb, p_sg, p_ab, h = _in_proj_call(cur, p["g1"], p["wt"])
        mix, dn_kept, arrived = _dn_fwd_call(p_dn, p_ab, p["conv"], p["a_log"], p["dt_bias"], p["dn_g"], B, T,
                                             gather=send[0][1:] + send[1][:1] if l == 0 else [])
        if l == 0:
            gathered[0][1:], gathered[1][0] = list(arrived[:3]), arrived[3]
        p["w_out"], p["w1"], p["w2"] = gathered[l][1].reshape(D, D), gathered[l][2], gathered[l][3].reshape(DFF, D)
        mix, sb_carries, arrived = _sb_fwd_call(p_sb, mix, p["sb_qg"], p["sb_kg"], B, T, gather=send[1][1:] if l == 0 else [])
        if l == 0:
            gathered[1][1:] = list(arrived)
        mix = _sg_fwd_call(p_sg, mix, p["sg_g"], p["sg_w"], p["sg_bias"], B, T)
        x1 = _out_proj_call(mix, p["w_out"], cur)
        x2 = _ffn_fwd_call(x1, p["g2"], p["w1"], p["w2"])
        saved.append(dict(x0=cur, p_dn=p_dn, p_sb=p_sb, p_sg=p_sg, p_ab=p_ab, h=h, mix=mix, x1=x1, dn_kept=dn_kept, sb_carries=sb_carries))
        cur = x2
    loss_part, dy = _loss_call(cur, loss_target.reshape(M, D))
    loss = lax.psum(loss_part[0, 0], ("x", "y", "c"))

    big_grads = [[None] * 4, [None] * 4]
    small_grads = {n: [None, None] for n in SMALL}
    for l in (1, 0):
        p, s = layer[l], saved[l]
        (dx1, da, r, h2, dg2), got1 = _ffn_bwd_call(s["x1"], dy, p["g2"], p["w1"], p["w2"], swap=big_grads[1] if l == 0 else ())
        big_grads[l][2] = _mm_tn_call(h2, da, "grad_w_ff1", col_shards=True)
        big_grads[l][3] = _mm_tn_call(r, dy, "grad_w_ff2").reshape(NDEV, FF_SHARD, D)
        dmix = _mm_nt_call(dx1, p["w_out"], "dmix")
        big_grads[l][1] = _mm_tn_call(s["mix"], dx1, "grad_w_out").reshape(NDEV, D // NDEV, D)
        if l == 0:
            got0, sums0 = _reduce_begin(big_grads[0][1:], "reduce_early0")
            early, early_got = big_grads[1] + big_grads[0][1:], list(got1) + list(got0)
            early_sums = [_pair_sum_call(g, t, ac, f"reduce_early1_pair{a}") for a, (g, t) in enumerate(zip(big_grads[1], got1))] + sums0
        (d_dn, d_ab, dcw, dalog, ddtb, ddn_g), early_from = _dn_bwd_call(
            s["p_dn"], s["p_ab"], dmix, s["dn_kept"], p["conv"], p["a_log"], p["dt_bias"], p["dn_g"], B, T,
            swap=early_sums if l == 0 else ())
        d_sb, dqg, dkg = _sb_bwd_call(s["p_sb"], dmix, s["sb_carries"], p["sb_qg"], p["sb_kg"], B, T)
        d_sg, dsg_g, dsg_w, dsg_b = _sg_bwd_call(s["p_sg"], dmix, p["sg_g"], p["sg_w"], p["sg_bias"], B, T)
        dsections = (d_dn, d_sb, d_sg, d_ab)
        dwt = _in_proj_grad_call(dsections, s["h"])
        big_grads[l][0] = _row_perm_call(dwt, table_back, "unpack_grad_w_in").reshape(NDEV, IN_SHARD_PAD, D)
        if l == 0:
            last = big_grads[0][:1]
            last_got, last_sums = _reduce_begin(last, "reduce_last")
        (dy, dg1), last_from = _in_proj_bwd_call(dsections, p["wt"], s["x0"], p["g1"], dx1, swap=last_sums if l == 0 else ())
        for n, val in (("norm1_g", dg1[0]), ("conv_w", dcw.transpose(1, 0, 2).reshape(4, 3 * DN_W)), ("a_log", dalog[0, :NH]),
                       ("dt_bias", ddtb[0, :NH]), ("dn_out_g", ddn_g[0]), ("sb_q_g", dqg[0, :64]), ("sb_k_g", dkg[0, :64]),
                       ("sg_v_g", dsg_g[0]), ("sg_w", dsg_w), ("sg_b", dsg_b[:, :NH].T), ("norm2_g", dg2[0])):
            small_grads[n][l] = val
    grad_x = dy.reshape(B, T, D)

    mine0 = _reduce_end(last, last_got, last_from, "reduce_last")
    mine1 = _reduce_end(early, early_got, early_from, "reduce_early")
    grads = {"w_in": jnp.stack([mine0[0][:IN_SHARD].T, mine1[0][:IN_SHARD].T]), "w_out": jnp.stack([mine1[4], mine1[1]]),
             "w_ff1": jnp.stack([mine1[5], mine1[2]]), "w_ff2": jnp.stack([mine1[6], mine1[3]])}
    small_flat = jnp.concatenate([jnp.stack(small_grads[n]).reshape(-1) for n in SMALL])
    everyone, = _all_gather_call([_to_rows(small_flat, 8)], "gather_small_grads")
    small_sum = _sum_call([(everyone, k) for k in range(NDEV)], F32, "sum_small_grads").reshape(-1)
    off = 0
    for n in SMALL:
        sz = 2 * _size(SMALL_SHAPE[n])
        grads[n] = small_sum[off:off + sz].reshape((2,) + SMALL_SHAPE[n])
        off += sz
    cshard = conv_w.shape[-1]
    grads["conv_w"] = lax.dynamic_slice_in_dim(grads["conv_w"], me * cshard, cshard, axis=2)

    deltas, new_m, new_v = {}, {}, {}
    for n in WEIGHTS:
        deltas[n], new_m[n], new_v[n] = _adamw_call(given[n], grads[n], mom[n], var[n], "adamw_" + n)
    return (loss, grad_x, *[grads[n] for n in WEIGHTS], *[deltas[n] for n in WEIGHTS], *[new_m[n] for n in WEIGHTS],
            *[new_v[n] for n in WEIGHTS])
```

```python
import functools

import numpy as np

import jax
import jax.numpy as jnp
from jax import lax
from jax.experimental import pallas as pl
from jax.experimental.pallas import tpu as pltpu

F32, BF16 = jnp.float32, jnp.bfloat16
EPS = 1e-6
LANES = 128
D = 1024
DFF = 4096
NH = 4
DN_W, SB_W, SG_W = 512, 256, 256
IN_DIM = 3336
NDEV = 8
IN_SHARD = IN_DIM // NDEV
IN_SHARD_PAD = 432
FF_SHARD = DFF // NDEV
DN_OFF, SB_OFF, SG_OFF, AB_OFF, NPACK = 0, 2048, 2816, 3328, 3456
SECTIONS = ((DN_OFF, 2048), (SB_OFF, 768), (SG_OFF, 512), (AB_OFF, 128))
SB_SCALE = 64 ** -0.5
DN_SCALE = 128 ** -0.5
VMEM_LIMIT = 56 * 1024 * 1024
VMEM_LIMIT_MAX = 62 * 1024 * 1024
ADAM_LR, ADAM_B1, ADAM_B2, ADAM_EPS, ADAM_WD, ADAM_STEP = 0.001, 0.9, 0.999, 1e-08, 0.01, 10
MESH = pl.DeviceIdType.MESH


def _iota(shape, dim):
    return lax.broadcasted_iota(jnp.int32, shape, dim)


def _params(**kw):
    return pltpu.CompilerParams(vmem_limit_bytes=VMEM_LIMIT, **kw)


NN, NT, TN = ((1,), (0,)), ((1,), (1,)), ((0,), (0,))


def _mm(a, b, dims):
    return lax.dot_general(a.astype(BF16), b.astype(BF16), (dims, ((), ())), preferred_element_type=F32)


def _plain(a, b, dims):
    return (a.T if dims == TN else a), (b.T if dims == NT else b)


def _mmx(a, b, dims):
    return _mm(*_plain(a, b, dims), NN)


@jax.custom_vjp
def _dot(a, b):
    return _mmx(a, b, NN)


def _dot_fwd(a, b):
    return _dot(a, b), (a, b)


def _dot_bwd(res, g):
    a, b = res
    return _mmx(g, b, NT).astype(a.dtype), _mmx(a, g, TN).astype(b.dtype)


_dot.defvjp(_dot_fwd, _dot_bwd)


def _split(x):
    hi = x.astype(BF16)
    return hi, (x - hi.astype(F32)).astype(BF16)


def _mm2(a, b):
    ah, al = _split(a)
    bh = b.astype(BF16)
    mm = lambda x, y: jnp.dot(x, y, preferred_element_type=F32)
    return mm(ah, bh) + mm(al, bh)


def _mm_ones(ones, x, ones_left):
    hi, lo = _split(x)
    mm = (lambda t: jnp.dot(ones, t, preferred_element_type=F32)) if ones_left else \
         (lambda t: jnp.dot(t, ones, preferred_element_type=F32))
    return mm(hi) + mm(lo)


def _pair_ones(kind, transposed):
    row, col = _iota((128, 128), 0), _iota((128, 128), 1)
    m = (row // 64) == (col // 64)
    if kind == "running":
        m = jnp.logical_and(m, (col >= row) if transposed else (col <= row))
    return jnp.where(m, 1.0, 0.0).astype(BF16)


@functools.partial(jax.custom_vjp, nondiff_argnums=(0,))
def _chunk_sum(kind, x):
    return _mm_ones(_pair_ones(kind, False), x, True)


def _chunk_sum_fwd(kind, x):
    return _chunk_sum(kind, x), None


def _chunk_sum_bwd(kind, _, g):
    return (_mm_ones(_pair_ones(kind, True), g, True),)


_chunk_sum.defvjp(_chunk_sum_fwd, _chunk_sum_bwd)


def _tri_ones(n, transposed):
    row, col = _iota((n, n), 0), _iota((n, n), 1)
    return jnp.where((row < col) if transposed else (row > col), 1.0, 0.0).astype(BF16)


def _sigmoid(x):
    return jax.nn.sigmoid(x)


def _silu(x):
    return x * _sigmoid(x)


def _softplus(x):
    return jnp.maximum(x, 0.0) + jnp.log1p(jnp.exp(-jnp.abs(x)))


def _gelu(x):
    return 0.5 * x * (1.0 + jnp.tanh(0.7978845608028654 * (x + 0.044715 * (x * x * x))))


def _rms(x, gain):
    return x * lax.rsqrt(jnp.mean(x * x, axis=-1, keepdims=True) + EPS) * gain


def _shift_down_impl(x, k):
    return jnp.where(_iota(x.shape, 0) >= k, pltpu.roll(x, k, 0), 0.0)


def _shift_up_impl(x, k):
    n = x.shape[0]
    return jnp.where(_iota(x.shape, 0) < n - k, pltpu.roll(x, n - k, 0), 0.0)


@functools.partial(jax.custom_vjp, nondiff_argnums=(1,))
def _shift_down(x, k):
    return _shift_down_impl(x, k)


def _shift_down_fwd(x, k):
    return _shift_down_impl(x, k), None


def _shift_down_bwd(k, _, g):
    return (_shift_up_impl(g, k),)


_shift_down.defvjp(_shift_down_fwd, _shift_down_bwd)


def _lane_pick(x, idx):
    return jnp.sum(jnp.where(_iota(x.shape, 1) == idx, x, 0.0), axis=-1, keepdims=True)


def _dn_conv(x, w0, w1, w2, w3, l2_scale):
    y = _silu(w3 * x + w2 * _shift_down(x, 1) + w1 * _shift_down(x, 2) + w0 * _shift_down(x, 3))
    if l2_scale is None:
        return y
    return y * lax.rsqrt(jnp.sum(y * y, axis=-1, keepdims=True) + EPS) * l2_scale


def _dn_gates(ab, a_log, dt_bias):
    lane = _iota((1, LANES), 1)
    g = -jnp.exp(a_log) * _softplus(ab + dt_bias)
    return jnp.where(lane < NH, g, jnp.where(lane < 2 * NH, _sigmoid(ab), 0.0))


def _same_head(shape):
    return (_iota(shape, 0) < LANES) == (_iota(shape, 1) < LANES)


def _bd(r2):
    return jnp.where(_same_head((2 * LANES, 2 * LANES)), jnp.concatenate([r2, r2], axis=0), 0.0)


def _bd_t(y2):
    t = y2.T
    return jnp.where(_same_head((2 * LANES, 2 * LANES)), jnp.concatenate([t, t], axis=1), 0.0)


def _pair_prod(kind, a2, b2, mm):
    if kind == NN:
        return mm(a2, _bd(b2))
    if kind == NT:
        return mm(a2, _bd_t(b2))
    full = mm(a2.T, b2)
    return jnp.concatenate([full[:LANES, :LANES], full[LANES:, LANES:]], axis=1)


_MM1 = lambda x, y: _mm(x, y, NN)


def _pair_vjp_rule(kind, a2, b2, g, mm):
    if kind == NN:
        return _pair_prod(NT, g, b2, mm), _pair_prod(TN, a2, g, mm)
    if kind == NT:
        return _pair_prod(NN, g, b2, mm), _pair_prod(TN, g, a2, mm)
    return _pair_prod(NT, b2, g, mm), _pair_prod(NN, a2, g, mm)


@functools.partial(jax.custom_vjp, nondiff_argnums=(0,))
def _pdot(kind, a2, b2):
    return _pair_prod(kind, a2, b2, _MM1)


def _pdot_fwd(kind, a2, b2):
    return _pdot(kind, a2, b2), (a2, b2)


def _pdot_bwd(kind, res, g):
    return _pair_vjp_rule(kind, *res, g, _MM1)


_pdot.defvjp(_pdot_fwd, _pdot_bwd)


def _unit_lower_inverse(lower):
    n = lower.shape[0]
    nk = -lower
    inv = jnp.where(_iota(lower.shape, 0) == jnp.bitwise_and(_iota(lower.shape, 1), n - 1), 1.0, 0.0) + nk
    for _ in range(5):
        nk = _pair_prod(NN, nk, nk, _MM1)
        inv = inv + _pair_prod(NN, inv, nk, _MM1)
    return inv


@jax.custom_vjp
def _solve_with(lower, inv, rhs):
    return _pair_prod(NN, inv, rhs, _mm2)


def _solve_with_fwd(lower, inv, rhs):
    x = _pair_prod(NN, inv, rhs, _mm2)
    return x, (inv, x)


def _solve_with_bwd(res, g):
    inv, x = res
    d_rhs = _pair_prod(TN, inv, g, _mm2)
    return -_pair_prod(NT, d_rhs, x, _MM1), jnp.zeros_like(inv), d_rhs


_solve_with.defvjp(_solve_with_fwd, _solve_with_bwd)


def _dn_local(q, k, v, g, beta, inv=None):
    shape = (LANES, 2 * LANES)
    row, col = _iota(shape, 0), jnp.bitwise_and(_iota(shape, 1), LANES - 1)
    same = (row // 64) == (col // 64)
    tri_incl = jnp.logical_and(same, col <= row)
    tri_strict = jnp.logical_and(same, col < row)
    first = row < 64
    gc = _chunk_sum("running", g)
    gl = _chunk_sum("total", g)
    diff = gc - jnp.concatenate([gc[:, :LANES].T, gc[:, LANES:].T], axis=1)
    decay = jnp.where(tri_incl, jnp.exp(jnp.where(tri_incl, diff, 0.0)), 0.0)
    egc = jnp.exp(gc)
    lower = jnp.where(tri_strict, beta * _pdot(NT, k, k) * decay, 0.0)
    if inv is None:
        inv = _unit_lower_inverse(lower)
    u_val = _solve_with(lower, inv, v * beta)
    w_dec = _solve_with(lower, inv, k * (beta * egc))
    qk = jnp.where(tri_incl, _pdot(NT, q, k) * decay, 0.0)
    q_dec = q * egc
    k_dec = k * jnp.exp(gl - gc)
    cd1 = jnp.exp(jnp.sum(jnp.where(first, g, 0.0), axis=0, keepdims=True))
    cd2 = jnp.exp(jnp.sum(jnp.where(first, 0.0, g), axis=0, keepdims=True))
    return (u_val, w_dec, qk, q_dec, k_dec, cd1, cd2), inv


def _dn_state(u_val, w_dec, qk, q_dec, k_dec, cd1, cd2, s0):
    first = _iota((LANES, 2 * LANES), 0) < 64
    u1 = u_val - _pdot(NN, w_dec, s0)
    s1 = s0 * cd1 + _pdot(TN, jnp.where(first, k_dec, 0.0), u1)
    u2 = u_val - _pdot(NN, w_dec, s1)
    u_new = jnp.where(first, u1, u2)
    s2 = s1 * cd2 + _pdot(TN, jnp.where(first, 0.0, k_dec), u_new)
    o = jnp.where(first, _pdot(NN, q_dec, s0), _pdot(NN, q_dec, s1)) + _pdot(NN, qk, u_new)
    return o, s2


def _dn_post(o, z, gain):
    return _rms(o, gain) * _silu(z)


_DN_L2 = (DN_SCALE, 1.0, None)
DN_HPS = 2
DN_BLK = 4 * DN_HPS * LANES
_DN_COLS = tuple(slice(i * LANES, (i + 1) * LANES) for i in range(DN_HPS))


def _dn_in_cols(s, i):
    return slice((s * DN_HPS + i) * LANES, (s * DN_HPS + i + 1) * LANES)


def _dn_taps(cw_ref, s, i):
    return tuple(cw_ref[t:t + 1, _dn_in_cols(s, i)] for t in range(4))


def _pair_rows(n):
    return pl.ds(pl.multiple_of(n * 128, 128), 128)


def _dn_gate_rows(gate, hp):
    head_a = _iota((1, DN_HPS * LANES), 1) < LANES
    h = DN_HPS * hp
    return (jnp.where(head_a, _lane_pick(gate, h), _lane_pick(gate, h + 1)),
            jnp.where(head_a, _lane_pick(gate, NH + h), _lane_pick(gate, NH + h + 1)))


def _dn_gate_cols(dg, db, hp):
    head_a = _iota((1, DN_HPS * LANES), 1) < LANES
    lane = _iota((1, LANES), 1)
    h = DN_HPS * hp
    out = 0.0
    for t, first in ((dg, h), (db, NH + h)):
        out = out + jnp.where(lane == first, jnp.sum(jnp.where(head_a, t, 0.0), axis=-1, keepdims=True), 0.0)
        out = out + jnp.where(lane == first + 1, jnp.sum(jnp.where(head_a, 0.0, t), axis=-1, keepdims=True), 0.0)
    return out


def _dn_in_specs(T):
    one = pl.Buffered(1)
    vec = pl.BlockSpec((1, LANES), lambda b, h: (0, 0))
    return [pl.BlockSpec((T, DN_BLK), lambda b, h: (b, h), pipeline_mode=one),
            pl.BlockSpec((T, LANES), lambda b, h: (b, 0), pipeline_mode=one),
            pl.BlockSpec((4, 3 * DN_HPS * LANES), lambda b, h: (0, h)), vec, vec, vec]


def _dn_fwd_call(proj_dn, proj_ab, conv_w, a_log, dt_bias, gain, B, T, gather=()):
    npair = T // 128
    ng = len(gather)
    nsteps = B * (NH // DN_HPS)

    def body(*refs):
        x_ref, ab_ref, cw_ref, alog_ref, dtb_ref, gain_ref = refs[:6]
        out_ref, q_s, k_s, v_s, o_s, gate_s, st_s, inv_s = refs[6 + ng:14 + ng]
        step_id = pl.program_id(0) * (NH // DN_HPS) + pl.program_id(1)
        if ng:
            send, forward, finish = _gather_phases(refs[6:6 + ng], refs[14 + ng:14 + 2 * ng], *refs[14 + 2 * ng:])
            pl.when(step_id == 0)(send)
            pl.when(step_id == nsteps - 1)(forward)
        hp = pl.program_id(1)
        for i, cs in enumerate(_DN_COLS):
            for s, (x_s, l2) in enumerate(zip((q_s, k_s, v_s), _DN_L2)):
                x_s[:, cs] = _dn_conv(x_ref[:, _dn_in_cols(s, i)], *_dn_taps(cw_ref, s, i), l2)
        gate_s[...] = _dn_gates(ab_ref[...], alog_ref[...], dtb_ref[...])

        def local_of(pair):
            r = _pair_rows(pair)
            loc, inv = _dn_local(q_s[r, :], k_s[r, :], v_s[r, :], *_dn_gate_rows(gate_s[r, :], hp))
            inv_s[0, 0, pair] = inv
            return loc

        def state_of(n, loc, state):
            st_s[0, 0, n] = state
            o, s2 = _dn_state(*loc, state)
            o_s[_pair_rows(n), :] = o
            return s2

        def step(n, carry):
            loc, state = carry
            return local_of(n + 1), state_of(n, loc, state)

        loc, state = lax.fori_loop(0, npair - 1, step, (local_of(0), jnp.zeros((LANES, DN_HPS * LANES), F32)))
        state_of(npair - 1, loc, state)
        for i, cs in enumerate(_DN_COLS):
            out_ref[:, cs] = _dn_post(o_s[:, cs], x_ref[:, _dn_in_cols(3, i)], gain_ref[...])
        if ng:
            pl.when(step_id == nsteps - 1)(finish)

    kept_specs, kept_shapes = _dn_kept(B, T)
    outs = pl.pallas_call(
        body, name="dn_fwd", grid=(B, NH // DN_HPS), in_specs=_dn_in_specs(T) + _any_specs(ng),
        out_specs=[pl.BlockSpec((T, DN_HPS * LANES), lambda b, h: (b, h), pipeline_mode=pl.Buffered(1))] + kept_specs + _any_specs(ng),
        out_shape=[jax.ShapeDtypeStruct((B * T, D), F32)] + kept_shapes + _gather_shapes(gather),
        scratch_shapes=_gather_sems(ng) if ng else [],
        compiler_params=_params(dimension_semantics=("arbitrary", "arbitrary")),
    )(proj_dn, proj_ab, conv_w, a_log, dt_bias, gain, *gather)
    return outs[0], outs[1:8], outs[8:]


def _dn_kept(B, T):
    one = pl.Buffered(1)
    npair, pairs = T // 128, NH // DN_HPS
    wide = pl.BlockSpec((T, DN_HPS * LANES), lambda b, h: (b, h), pipeline_mode=one)
    per_pair = pl.BlockSpec((1, 1, npair, LANES, DN_HPS * LANES), lambda b, h: (b, h, 0, 0, 0), pipeline_mode=one)
    specs = [wide] * 4 + [pl.BlockSpec((T, LANES), lambda b, h: (b, h), pipeline_mode=one)] + [per_pair] * 2
    shapes = ([jax.ShapeDtypeStruct((B * T, DN_W), F32)] * 4 + [jax.ShapeDtypeStruct((B * T, pairs * LANES), F32)]
              + [jax.ShapeDtypeStruct((B, pairs, npair, LANES, DN_HPS * LANES), F32)] * 2)
    return specs, shapes


def _dn_bwd_call(proj_dn, proj_ab, dmix, kept, conv_w, a_log, dt_bias, gain, B, T, swap=()):
    npair = T // 128
    ns = len(swap)
    nsteps = B * (NH // DN_HPS)

    def body(*refs):
        x_ref, ab_ref, cw_ref, alog_ref, dtb_ref, gain_ref, do_ref, q_s, k_s, v_s, o_ref, gate_s, st_s, inv_s = refs[:14]
        dx_ref, dab_ref, dcw_ref, dalog_ref, ddtb_ref, dgain_ref = refs[14 + ns:20 + ns]
        dgate_s, do_s = refs[20 + 2 * ns:22 + 2 * ns]
        b_i, hp = pl.program_id(0), pl.program_id(1)
        step_id = b_i * (NH // DN_HPS) + hp
        if ns:
            send, finish = _chip_swap_phases(refs[14:14 + ns], refs[20 + ns:20 + 2 * ns], *refs[22 + 2 * ns:])
            pl.when(step_id == 0)(send)

        def pair_in(r):
            return (q_s[r, :], k_s[r, :], v_s[r, :]) + _dn_gate_rows(gate_s[r, :], hp)

        zero_state = jnp.zeros((LANES, DN_HPS * LANES), F32)

        @pl.when(jnp.logical_and(b_i == 0, hp == 0))
        def _():
            dcw_ref[...] = jnp.zeros_like(dcw_ref)
            dalog_ref[...] = jnp.zeros_like(dalog_ref)
            ddtb_ref[...] = jnp.zeros_like(ddtb_ref)
            dgain_ref[...] = jnp.zeros_like(dgain_ref)

        for i, cs in enumerate(_DN_COLS):
            zc = _dn_in_cols(3, i)
            _, post_vjp = jax.vjp(_dn_post, o_ref[:, cs], x_ref[:, zc], gain_ref[...])
            do, dz, dgain = post_vjp(do_ref[:, cs])
            dx_ref[:, zc] = dz
            do_s[:, cs] = do
            dgain_ref[...] += dgain

        wide_cols = lambda s: slice(s * DN_HPS * LANES, (s + 1) * DN_HPS * LANES)

        def back_step(nn, dstate):
            n = npair - 1 - nn
            r = _pair_rows(n)
            inv = inv_s[0, 0, n]
            local = lambda q, k, v, g, beta, inv=inv: _dn_local(q, k, v, g, beta, inv)[0]
            loc, local_vjp = jax.vjp(local, *pair_in(r))
            _, state_vjp = jax.vjp(_dn_state, *loc, st_s[0, 0, n])
            *dloc, ds0 = state_vjp((do_s[r, :], dstate))
            dq, dk, dv, dg, db = local_vjp(tuple(dloc))
            dx_ref[r, wide_cols(0)], dx_ref[r, wide_cols(1)], dx_ref[r, wide_cols(2)] = dq, dk, dv
            dgate_s[r, :] = _dn_gate_cols(dg, db, hp)
            return ds0

        lax.fori_loop(0, npair, back_step, zero_state)

        for i, cs in enumerate(_DN_COLS):
            h = DN_HPS * hp + i
            for s, l2 in enumerate(_DN_L2):
                xc = _dn_in_cols(s, i)
                _, conv_vjp = jax.vjp(functools.partial(_dn_conv, l2_scale=l2), x_ref[:, xc], *_dn_taps(cw_ref, s, i))
                dx, *dw = conv_vjp(dx_ref[:, xc])
                dx_ref[:, xc] = dx
                for t in range(4):
                    dcw_ref[h + 4 * s, t:t + 1, :] += dw[t]
        _, gate_vjp = jax.vjp(_dn_gates, ab_ref[...], alog_ref[...], dtb_ref[...])
        dab, dalog, ddtb = gate_vjp(dgate_s[...])
        dalog_ref[...] += dalog
        ddtb_ref[...] += ddtb

        @pl.when(hp == 0)
        def _():
            dab_ref[...] = jnp.zeros_like(dab_ref)

        dab_ref[...] += dab
        if ns:
            pl.when(step_id == nsteps - 1)(finish)

    M = B * T
    one = pl.Buffered(1)
    vec = pl.BlockSpec((1, LANES), lambda b, h: (0, 0))
    wide = [pltpu.VMEM((T, DN_HPS * LANES), F32)]
    vec_shape = jax.ShapeDtypeStruct((1, LANES), F32)
    outs = pl.pallas_call(
        body, name="dn_bwd", grid=(B, NH // DN_HPS),
        in_specs=_dn_in_specs(T) + [pl.BlockSpec((T, DN_HPS * LANES), lambda b, h: (b, h), pipeline_mode=one)] + _dn_kept(B, T)[0]
        + _any_specs(ns),
        out_specs=[pl.BlockSpec((T, DN_BLK), lambda b, h: (b, h), pipeline_mode=one), pl.BlockSpec((T, LANES), lambda b, h: (b, 0)),
                   pl.BlockSpec((12, 4, LANES), lambda b, h: (0, 0, 0)), vec, vec, vec] + _any_specs(ns),
        out_shape=[jax.ShapeDtypeStruct((M, 4 * DN_W), F32), jax.ShapeDtypeStruct((M, LANES), F32),
                   jax.ShapeDtypeStruct((12, 4, LANES), F32), vec_shape, vec_shape, vec_shape] + _chip_swap_shapes(swap),
        scratch_shapes=[pltpu.VMEM((T, LANES), F32)] + wide + (_chip_swap_sems(ns) if ns else []),
        compiler_params=pltpu.CompilerParams(vmem_limit_bytes=VMEM_LIMIT_MAX, dimension_semantics=("arbitrary", "arbitrary")),
    )(proj_dn, proj_ab, conv_w, a_log, dt_bias, gain, dmix, *kept, *swap)
    return outs[:6], outs[6:]


SBQ = 256


def _group_rms(x, gain):
    first = _iota(x.shape, 1) < 64
    sq = x * x
    ss_a = jnp.sum(jnp.where(first, sq, 0.0), axis=-1, keepdims=True)
    ss_b = jnp.sum(jnp.where(first, 0.0, sq), axis=-1, keepdims=True)
    ms = jnp.where(first, ss_a, ss_b) * (1.0 / 64)
    return x * lax.rsqrt(ms + EPS) * gain


def _sb_stack(q):
    first = _iota((1, LANES), 1) < 64
    return jnp.concatenate([jnp.where(first, q, 0.0), jnp.where(first, 0.0, q)], axis=0)


def _sb_fold(acc):
    return jnp.where(_iota((1, LANES), 1) < 64, acc[:SBQ], acc[SBQ:])


def _sb_logs(q2, k, diag):
    n = SBQ
    z = _mm(q2, k, ((1,), (1,))) * SB_SCALE
    ls_pos = jnp.minimum(z, 0.0) - jnp.log(1.0 + jnp.exp(-jnp.abs(z)))
    l1m = ls_pos - z
    if not diag:
        return ls_pos, l1m, None
    mask = _iota((2 * n, n), 1) < jnp.bitwise_and(_iota((2 * n, n), 0), n - 1)
    return ls_pos, jnp.where(mask, l1m, 0.0), mask


def _sb_weights(ls_pos, l1m, mask, carry):
    w = jnp.exp(ls_pos + (_mm_ones(_tri_ones(SBQ, False), l1m, False) + carry))
    return w if mask is None else jnp.where(mask, w, 0.0)


def _sb_block(q, k, v, carry, diag):
    ls_pos, l1m, mask = _sb_logs(_sb_stack(q), k, diag)
    w = _sb_weights(ls_pos, l1m, mask, carry)
    return _mm(w, v, ((1,), (0,))), carry + jnp.sum(l1m, axis=-1, keepdims=True), _sb_sum_as_rows(l1m)


SB_ROWS = 16


def _sb_sum_as_rows(l1m):
    ones = jnp.ones((SB_ROWS, SBQ), BF16)
    hi, lo = _split(l1m)
    mm = lambda t: lax.dot_general(ones, t, (NT, ((), ())), preferred_element_type=F32)
    return mm(hi) + mm(lo)


def _sb_rows_as_column(rows):
    pick = jnp.where(_iota((SB_ROWS, SBQ), 0) == 0, 1.0, 0.0).astype(BF16)
    hi = rows.astype(BF16)
    rest = rows - hi.astype(F32)
    mid = rest.astype(BF16)
    lo = (rest - mid.astype(F32)).astype(BF16)
    mm = lambda t: lax.dot_general(t, pick, (TN, ((), ())), preferred_element_type=F32)
    return mm(hi) + (mm(mid) + mm(lo))


def _sb_block_bwd(q, k, v, carry, diag, dpv, dcarry):
    q2 = _sb_stack(q)
    ls_pos, l1m, mask = _sb_logs(q2, k, diag)
    w = _sb_weights(ls_pos, l1m, mask, carry)
    dv = _mm(w, dpv, ((0,), (0,)))
    de = _mm(dpv, v, ((1,), (1,))) * w
    dl1m = jnp.dot(de.astype(BF16), _tri_ones(SBQ, True), preferred_element_type=F32) + dcarry
    if mask is not None:
        dl1m = jnp.where(mask, dl1m, 0.0)
    sig = jnp.exp(ls_pos)
    dz = (de * (1.0 - sig) - dl1m * sig) * SB_SCALE
    dq = _sb_fold(_mm(dz, k, ((1,), (0,))))
    return dq, _mm(dz, q2, ((0,), (0,))), dv, dcarry + jnp.sum(de, axis=-1, keepdims=True)


_SB_Q, _SB_K, _SB_V = (slice(i * LANES, (i + 1) * LANES) for i in range(3))


def _sb_fwd_call(proj_sb, mix, q_gain, k_gain, B, T, gather=()):
    nblk = T // SBQ
    ng = len(gather)
    nsteps = 2 * B

    def body(*refs):
        x_ref, qg_ref, kg_ref = refs[:3]
        out_ref, carry_ref = refs[4 + ng:6 + ng]
        q_s, k_s = refs[6 + 2 * ng:8 + 2 * ng]
        step_id = 2 * pl.program_id(0) + pl.program_id(1)
        if ng:
            send, forward, finish = _gather_phases(refs[4:4 + ng], refs[6 + ng:6 + 2 * ng], *refs[8 + 2 * ng:])
            pl.when(step_id == 0)(send)
            pl.when(step_id == nsteps - 1)(forward)
        q_s[...] = _group_rms(x_ref[:, _SB_Q], qg_ref[...])
        k_s[...] = _group_rms(x_ref[:, _SB_K], kg_ref[...])

        def qblock(i, _):
            ri = pl.ds(pl.multiple_of(i * SBQ, SBQ), SBQ)
            q = q_s[ri, :]

            def kblock(jj, c):
                j = i - 1 - jj
                rj = pl.ds(pl.multiple_of(j * SBQ, SBQ), SBQ)
                carry_ref[0, 0, i, j] = c[2]
                pv, carry, rows = _sb_block(q, k_s[rj, :], x_ref[rj, _SB_V], c[1], False)
                return c[0] + pv, carry, c[2] + rows

            on_diag = _sb_block(q, k_s[ri, :], x_ref[ri, _SB_V], jnp.zeros((2 * SBQ, 1), F32), True)
            acc, _c, _r = lax.fori_loop(0, i, kblock, on_diag)
            out_ref[ri, :] = _sb_fold(acc)
            return 0

        lax.fori_loop(0, nblk, qblock, 0)
        if ng:
            pl.when(step_id == nsteps - 1)(finish)

    vec = pl.BlockSpec((1, LANES), lambda b, p: (0, 0))
    outs = pl.pallas_call(
        body, name="sb_fwd", grid=(B, 2),
        in_specs=[pl.BlockSpec((T, 3 * LANES), lambda b, p: (b, p)), vec, vec, pl.BlockSpec(memory_space=pl.ANY)] + _any_specs(ng),
        out_specs=[pl.BlockSpec((T, LANES), lambda b, p: (b, DN_W // LANES + p)), _sb_carry_spec(nblk)] + _any_specs(ng),
        out_shape=[jax.ShapeDtypeStruct((B * T, D), F32), jax.ShapeDtypeStruct((B, 2, nblk, nblk, SB_ROWS, 2 * SBQ), F32)]
        + _gather_shapes(gather), input_output_aliases={3: 0},
        scratch_shapes=[pltpu.VMEM((T, LANES), F32)] * 2 + (_gather_sems(ng) if ng else []),
        compiler_params=_params(dimension_semantics=("arbitrary", "arbitrary")),
    )(proj_sb, q_gain, k_gain, mix, *gather)
    return outs[0], outs[1], outs[2:]


def _sb_carry_spec(nblk):
    return pl.BlockSpec((1, 1, nblk, nblk, SB_ROWS, 2 * SBQ), lambda b, p: (b, p, 0, 0, 0, 0))


def _sb_bwd_call(proj_sb, dmix, carries, q_gain, k_gain, B, T):
    nblk = T // SBQ

    def body(x_ref, qg_ref, kg_ref, do_ref, carry_ref, dx_ref, dqg_ref, dkg_ref, q_s, k_s, dq_s, dk_s, dv_s):
        b_i, p = pl.program_id(0), pl.program_id(1)
        qn, q_vjp = jax.vjp(_group_rms, x_ref[:, _SB_Q], qg_ref[...])
        kn, k_vjp = jax.vjp(_group_rms, x_ref[:, _SB_K], kg_ref[...])
        q_s[...], k_s[...] = qn, kn
        dk_s[...] = jnp.zeros_like(dk_s)
        dv_s[...] = jnp.zeros_like(dv_s)

        def qblock(i, _):
            ri = pl.ds(pl.multiple_of(i * SBQ, SBQ), SBQ)
            q = q_s[ri, :]
            dacc = _sb_stack(do_ref[ri, :])

            def kblock(j, c):
                rj = pl.ds(pl.multiple_of(j * SBQ, SBQ), SBQ)
                carry = _sb_rows_as_column(carry_ref[0, 0, i, j])
                dq_j, dk_j, dv_j, dc = _sb_block_bwd(q, k_s[rj, :], x_ref[rj, _SB_V], carry, False, dacc, c[1])
                dk_s[rj, :] += dk_j
                dv_s[rj, :] += dv_j
                return c[0] + dq_j, dc

            dq, dc = lax.fori_loop(0, i, kblock, (jnp.zeros((SBQ, LANES), F32), jnp.zeros((2 * SBQ, 1), F32)))
            dq_i, dk_i, dv_i, _dc = _sb_block_bwd(q, k_s[ri, :], x_ref[ri, _SB_V], jnp.zeros((2 * SBQ, 1), F32), True, dacc, dc)
            dk_s[ri, :] += dk_i
            dv_s[ri, :] += dv_i
            dq_s[ri, :] = dq + dq_i
            return 0

        lax.fori_loop(0, nblk, qblock, 0)
        dq_in, dqg = q_vjp(dq_s[...])
        dk_in, dkg = k_vjp(dk_s[...])
        dx_ref[:, _SB_Q], dx_ref[:, _SB_K], dx_ref[:, _SB_V] = dq_in, dk_in, dv_s[...]

        @pl.when(jnp.logical_and(b_i == 0, p == 0))
        def _():
            dqg_ref[...] = jnp.zeros_like(dqg_ref)
            dkg_ref[...] = jnp.zeros_like(dkg_ref)

        dqg_ref[...] += dqg + pltpu.roll(dqg, 64, 1)
        dkg_ref[...] += dkg + pltpu.roll(dkg, 64, 1)

    M = B * T
    vec = pl.BlockSpec((1, LANES), lambda b, p: (0, 0))
    blk = pl.BlockSpec((T, 3 * LANES), lambda b, p: (b, p))
    big = [pltpu.VMEM((T, LANES), F32)]
    return pl.pallas_call(
        body, name="sb_bwd", grid=(B, 2),
        in_specs=[blk, vec, vec, pl.BlockSpec((T, LANES), lambda b, p: (b, DN_W // LANES + p)), _sb_carry_spec(nblk)],
        out_specs=[blk, vec, vec],
        out_shape=[jax.ShapeDtypeStruct((M, 3 * SB_W), F32)] + [jax.ShapeDtypeStruct((1, LANES), F32)] * 2,
        scratch_shapes=big * 5,
        compiler_params=_params(dimension_semantics=("arbitrary", "arbitrary")),
    )(proj_sb, q_gain, k_gain, dmix, carries)


def _sg_chunk(u, v, gain, w_a, w_b, bias):
    n = 128
    row, col = _iota((n, n), 0), _iota((n, n), 1)
    first = _iota((1, LANES), 1) < 64
    vn = _group_rms(_gelu(v), gain)
    tril = col <= row
    mixed = jnp.where(first, _dot(jnp.where(tril, w_a, 0.0), vn), _dot(jnp.where(tril, w_b, 0.0), vn)) + bias
    return _gelu(u) * mixed


_SG_U, _SG_V = slice(0, LANES), slice(LANES, 2 * LANES)


def _sg_fwd_call(proj_sg, mix, gain, sg_w, bias, B, T):
    nchunk = T // 128

    def body(x_ref, g_ref, wa_ref, wb_ref, bias_ref, mix_ref, out_ref):
        del mix_ref

        def step(i, _):
            r = pl.ds(pl.multiple_of(i * 128, 128), 128)
            out_ref[r, :] = _sg_chunk(x_ref[r, _SG_U], x_ref[r, _SG_V], g_ref[...], wa_ref[0], wb_ref[0], bias_ref[...])
            return 0

        lax.fori_loop(0, nchunk, step, 0)

    return pl.pallas_call(
        body, name="sg_fwd", grid=(B, 2),
        in_specs=[pl.BlockSpec((T, 2 * LANES), lambda b, p: (b, p)), pl.BlockSpec((1, LANES), lambda b, p: (0, p)),
                  pl.BlockSpec((1, 128, 128), lambda b, p: (2 * p, 0, 0)), pl.BlockSpec((1, 128, 128), lambda b, p: (2 * p + 1, 0, 0)),
                  pl.BlockSpec((128, LANES), lambda b, p: (0, p)), pl.BlockSpec(memory_space=pl.ANY)],
        out_specs=pl.BlockSpec((T, LANES), lambda b, p: (b, (DN_W + SB_W) // LANES + p)),
        out_shape=jax.ShapeDtypeStruct((B * T, D), F32), input_output_aliases={5: 0},
        compiler_params=_params(dimension_semantics=("arbitrary", "arbitrary")),
    )(proj_sg, gain, sg_w, sg_w, bias, mix)


def _sg_bwd_call(proj_sg, dmix, gain, sg_w, bias, B, T):
    nchunk = T // 128

    def body(x_ref, g_ref, wa_ref, wb_ref, bias_ref, do_ref, dx_ref, dg_ref, dw_ref, db_ref):
        p, b_i = pl.program_id(0), pl.program_id(1)

        def step(i, c):
            r = pl.ds(pl.multiple_of(i * 128, 128), 128)
            _, vjp = jax.vjp(_sg_chunk, x_ref[r, _SG_U], x_ref[r, _SG_V], g_ref[...], wa_ref[0], wb_ref[0], bias_ref[...])
            du, dv, dg, dwa, dwb, dbias = vjp(do_ref[r, :])
            dx_ref[r, _SG_U], dx_ref[r, _SG_V] = du, dv
            return c[0] + dg, c[1] + dwa, c[2] + dwb, c[3] + dbias

        z = jnp.zeros((128, 128), F32)
        dg, dwa, dwb, dbias = lax.fori_loop(0, nchunk, step, (jnp.zeros((1, LANES), F32), z, z, z))
        lane = _iota((1, LANES), 1)
        first = lane < 64
        s_a = jnp.sum(jnp.where(first, dbias, 0.0), axis=-1, keepdims=True)
        s_b = jnp.sum(jnp.where(first, 0.0, dbias), axis=-1, keepdims=True)
        dbg = jnp.where(lane == 2 * p, s_a, 0.0) + jnp.where(lane == 2 * p + 1, s_b, 0.0)

        @pl.when(b_i == 0)
        def _():
            dg_ref[...] = jnp.zeros_like(dg_ref)
            dw_ref[...] = jnp.zeros_like(dw_ref)

        @pl.when(jnp.logical_and(b_i == 0, p == 0))
        def _():
            db_ref[...] = jnp.zeros_like(db_ref)

        dg_ref[...] += dg
        dw_ref[0] += dwa
        dw_ref[1] += dwb
        db_ref[...] += dbg

    M = B * T
    blk = pl.BlockSpec((T, 2 * LANES), lambda p, b: (b, p))
    return pl.pallas_call(
        body, name="sg_bwd", grid=(2, B),
        in_specs=[blk, pl.BlockSpec((1, LANES), lambda p, b: (0, p)),
                  pl.BlockSpec((1, 128, 128), lambda p, b: (2 * p, 0, 0)), pl.BlockSpec((1, 128, 128), lambda p, b: (2 * p + 1, 0, 0)),
                  pl.BlockSpec((128, LANES), lambda p, b: (0, p)),
                  pl.BlockSpec((T, LANES), lambda p, b: (b, (DN_W + SB_W) // LANES + p))],
        out_specs=[blk, pl.BlockSpec((1, LANES), lambda p, b: (0, p)), pl.BlockSpec((2, 128, 128), lambda p, b: (p, 0, 0)),
                   pl.BlockSpec((128, LANES), lambda p, b: (0, 0))],
        out_shape=[jax.ShapeDtypeStruct((M, 2 * SG_W), F32), jax.ShapeDtypeStruct((1, SG_W), F32),
                   jax.ShapeDtypeStruct((4, 128, 128), F32), jax.ShapeDtypeStruct((128, LANES), F32)],
        compiler_params=_params(dimension_semantics=("arbitrary", "arbitrary")),
    )(proj_sg, gain, sg_w, sg_w, bias, dmix)


def _row_tile(m, most=512):
    return min(m, most)


def _in_proj_call(x, gain, wt):
    m = x.shape[0]
    tm = _row_tile(m)

    def body(x_ref, g_ref, wt_ref, *out_refs):
        h = _rms(x_ref[...], g_ref[...]).astype(BF16)
        out_refs[-1][...] = h
        for (off, width), out_ref in zip(SECTIONS, out_refs):
            out_ref[...] = lax.dot_general(h, wt_ref[off:off + width, :], (((1,), (1,)), ((), ())), preferred_element_type=F32)

    rows = lambda width: pl.BlockSpec((tm, width), lambda i: (i, 0))
    return pl.pallas_call(
        body, name="in_proj", grid=(m // tm,),
        in_specs=[rows(D), pl.BlockSpec((1, D), lambda i: (0, 0)),
                  pl.BlockSpec((NPACK, D), lambda i: (0, 0), pipeline_mode=pl.Buffered(1))],
        out_specs=[rows(w) for _, w in SECTIONS] + [rows(D)],
        out_shape=[jax.ShapeDtypeStruct((m, w), F32) for _, w in SECTIONS] + [jax.ShapeDtypeStruct((m, D), BF16)],
        compiler_params=_params(dimension_semantics=("arbitrary",)),
    )(x, gain, wt)


def _in_proj_bwd_call(dsections, wt, x, gain, dres, swap=()):
    m = x.shape[0]
    tm = _row_tile(m)
    nsec, ns = len(SECTIONS), len(swap)

    def body(*refs):
        ds_refs = refs[:nsec]
        wt_ref, x_ref, g_ref, dres_ref = refs[nsec:nsec + 4]
        dx_ref, dg_ref = refs[nsec + 4 + ns:nsec + 6 + ns]
        step = pl.program_id(0)
        if ns:
            send, finish = _chip_swap_phases(refs[nsec + 4:nsec + 4 + ns], refs[nsec + 6 + ns:nsec + 6 + 2 * ns],
                                             *refs[nsec + 6 + 2 * ns:])
            pl.when(step == 0)(send)

        @pl.when(step == 0)
        def _():
            dg_ref[...] = jnp.zeros_like(dg_ref)

        dh = 0.0
        for (off, width), ds_ref in zip(SECTIONS, ds_refs):
            dh = dh + jnp.dot(ds_ref[...].astype(BF16), wt_ref[off:off + width, :], preferred_element_type=F32)
        _, vjp = jax.vjp(_rms, x_ref[...], g_ref[...])
        dx, dg = vjp(dh)
        dx_ref[...] = dres_ref[...] + dx
        dg_ref[...] += dg
        if ns:
            pl.when(step == m // tm - 1)(finish)

    rows = lambda width: pl.BlockSpec((tm, width), lambda i: (i, 0))
    outs = pl.pallas_call(
        body, name="in_proj_bwd", grid=(m // tm,),
        in_specs=[rows(w) for _, w in SECTIONS] + [pl.BlockSpec((NPACK, D), lambda i: (0, 0), pipeline_mode=pl.Buffered(1)),
                                                   rows(D), pl.BlockSpec((1, D), lambda i: (0, 0)), rows(D)] + _any_specs(ns),
        out_specs=[rows(D), pl.BlockSpec((1, D), lambda i: (0, 0))] + _any_specs(ns),
        out_shape=[jax.ShapeDtypeStruct((m, D), F32), jax.ShapeDtypeStruct((1, D), F32)] + _chip_swap_shapes(swap),
        scratch_shapes=_chip_swap_sems(ns) if ns else [],
        compiler_params=_params(dimension_semantics=("arbitrary",)),
    )(*dsections, wt, x, gain, dres, *swap)
    return outs[:2], outs[2:]


def _in_proj_grad_call(dsections, h):
    m = h.shape[0]
    tm = min(m, 256)

    def body(*refs):
        ds_refs, (h_ref, out_ref) = refs[:len(SECTIONS)], refs[len(SECTIONS):]

        @pl.when(pl.program_id(0) == 0)
        def _():
            out_ref[...] = jnp.zeros_like(out_ref)

        for (off, width), ds_ref in zip(SECTIONS, ds_refs):
            out_ref[off:off + width, :] += lax.dot_general(ds_ref[...].astype(BF16), h_ref[...], (((0,), (0,)), ((), ())),
                                                           preferred_element_type=F32)

    rows = lambda width: pl.BlockSpec((tm, width), lambda i: (i, 0))
    return pl.pallas_call(
        body, name="grad_w_in", grid=(m // tm,),
        in_specs=[rows(w) for _, w in SECTIONS] + [rows(D)],
        out_specs=pl.BlockSpec((NPACK, D), lambda i: (0, 0), pipeline_mode=pl.Buffered(1)),
        out_shape=jax.ShapeDtypeStruct((NPACK, D), F32),
        compiler_params=_params(dimension_semantics=("arbitrary",)),
    )(*dsections, h)


def _packed_column_of():
    t = np.full(NPACK, -1, np.int64)
    lanes = np.arange(LANES)
    for pair in range(2):
        for s in range(4):
            t[DN_OFF + pair * 1024 + s * 256 + np.arange(256)] = s * DN_W + pair * 256 + np.arange(256)
        for s in range(3):
            t[SB_OFF + pair * 384 + s * LANES + lanes] = 2056 + s * SB_W + pair * LANES + lanes
        for s in range(2):
            t[SG_OFF + pair * 256 + s * LANES + lanes] = 2056 + 3 * SB_W + s * SG_W + pair * LANES + lanes
    t[AB_OFF + np.arange(2 * NH)] = 4 * DN_W + np.arange(2 * NH)
    return t


def _row_tables():
    col = _packed_column_of()
    fwd = np.where(col >= 0, (col // IN_SHARD) * IN_SHARD_PAD + col % IN_SHARD, -1)
    packed_of = np.full(IN_DIM, -1, np.int64)
    packed_of[col[col >= 0]] = np.nonzero(col >= 0)[0]
    r = np.arange(NDEV * IN_SHARD_PAD)
    inside = r % IN_SHARD_PAD < IN_SHARD
    back = np.where(inside, packed_of[np.minimum((r // IN_SHARD_PAD) * IN_SHARD + r % IN_SHARD_PAD, IN_DIM - 1)], -1)
    return fwd, back


def _row_perm_call(src, table, name):
    n_out = table.shape[0]
    touched = [sorted(set((table[b * 128:(b + 1) * 128][table[b * 128:(b + 1) * 128] >= 0] // 128).tolist()))
               for b in range(n_out // 128)]

    def body(tbl_ref, src_ref, out_ref):
        lane = _iota((1, LANES), 1)
        for b, blocks in enumerate(touched):
            want = tbl_ref[b * 128:(b + 1) * 128, :]
            acc = jnp.zeros((128, D), F32)
            for sb in blocks:
                pick = jnp.where(want == sb * 128 + lane, 1.0, 0.0).astype(BF16)
                acc = acc + jnp.dot(pick, src_ref[sb * 128:(sb + 1) * 128, :].astype(BF16), preferred_element_type=F32)
            out_ref[b * 128:(b + 1) * 128, :] = acc.astype(BF16)

    return pl.pallas_call(
        body, name=name, out_shape=jax.ShapeDtypeStruct((n_out, D), BF16),
        in_specs=[pl.BlockSpec(memory_space=pltpu.VMEM)] * 2, out_specs=pl.BlockSpec(memory_space=pltpu.VMEM),
        compiler_params=_params(),
    )(jnp.asarray(table.reshape(-1, 1), jnp.int32), src)


def _out_proj_call(a, w, res):
    m, k = a.shape
    n = w.shape[1]
    tm = _row_tile(m)

    def body(a_ref, w_ref, res_ref, out_ref):
        out_ref[...] = res_ref[...] + jnp.dot(a_ref[...].astype(BF16), w_ref[...], preferred_element_type=F32)

    return pl.pallas_call(
        body, name="out_proj", grid=(m // tm,),
        in_specs=[pl.BlockSpec((tm, k), lambda i: (i, 0)), pl.BlockSpec((k, n), lambda i: (0, 0)),
                  pl.BlockSpec((tm, n), lambda i: (i, 0))],
        out_specs=pl.BlockSpec((tm, n), lambda i: (i, 0)),
        out_shape=jax.ShapeDtypeStruct((m, n), F32),
        compiler_params=_params(dimension_semantics=("arbitrary",)),
    )(a, w, res)


def _ffn_specs(tm):
    return [pl.BlockSpec((1, D, FF_SHARD), lambda i, j: (j, 0, 0)), pl.BlockSpec((FF_SHARD, D), lambda i, j: (j, 0))]


def _ffn_fwd_call(x, gain, w1, w2):
    m = x.shape[0]
    tm = _row_tile(m, 1024)

    def body(x_ref, g_ref, w1_ref, w2_ref, out_ref, h_s, acc_s):
        j = pl.program_id(1)

        @pl.when(j == 0)
        def _():
            h_s[...] = _rms(x_ref[...], g_ref[...]).astype(BF16)
            acc_s[...] = jnp.zeros_like(acc_s)

        a = jnp.maximum(jnp.dot(h_s[...], w1_ref[0], preferred_element_type=F32), 0.0)
        acc_s[...] += jnp.dot((a * a).astype(BF16), w2_ref[...], preferred_element_type=F32)

        @pl.when(j == NDEV - 1)
        def _():
            out_ref[...] = x_ref[...] + acc_s[...]

    return pl.pallas_call(
        body, name="ffn_fwd", grid=(m // tm, NDEV),
        in_specs=[pl.BlockSpec((tm, D), lambda i, j: (i, 0)), pl.BlockSpec((1, D), lambda i, j: (0, 0))] + _ffn_specs(tm),
        out_specs=pl.BlockSpec((tm, D), lambda i, j: (i, 0)),
        out_shape=jax.ShapeDtypeStruct((m, D), F32),
        scratch_shapes=[pltpu.VMEM((tm, D), BF16), pltpu.VMEM((tm, D), F32)],
        compiler_params=_params(dimension_semantics=("arbitrary", "arbitrary")),
    )(x, gain, w1, w2)


def _ffn_bwd_call(x, dy, gain, w1, w2, swap=()):
    m = x.shape[0]
    tm = _row_tile(m, 1024)
    ns = len(swap)

    def body(*refs):
        x_ref, dy_ref, g_ref, w1_ref, w2_ref = refs[:5]
        dx_ref, da_ref, r_ref, h_ref, dg_ref = refs[5 + ns:10 + ns]
        acc_s = refs[10 + 2 * ns]
        i, j = pl.program_id(0), pl.program_id(1)
        if ns:
            send, finish = _sibling_swap_phases(refs[5:5 + ns], refs[10 + ns:10 + 2 * ns], *refs[11 + 2 * ns:])
            pl.when(jnp.logical_and(i == 0, j == 0))(send)

        @pl.when(j == 0)
        def _():
            h_ref[...] = _rms(x_ref[...], g_ref[...]).astype(BF16)
            acc_s[...] = jnp.zeros_like(acc_s)

        @pl.when(jnp.logical_and(i == 0, j == 0))
        def _():
            dg_ref[...] = jnp.zeros_like(dg_ref)

        a = jnp.maximum(jnp.dot(h_ref[...], w1_ref[0], preferred_element_type=F32), 0.0)
        r_ref[...] = (a * a).astype(BF16)
        dr = lax.dot_general(dy_ref[...].astype(BF16), w2_ref[...], (((1,), (1,)), ((), ())), preferred_element_type=F32)
        da = (dr * (2.0 * a)).astype(BF16)
        da_ref[...] = da
        acc_s[...] += lax.dot_general(da, w1_ref[0], (((1,), (1,)), ((), ())), preferred_element_type=F32)

        @pl.when(j == NDEV - 1)
        def _():
            _, vjp = jax.vjp(_rms, x_ref[...], g_ref[...])
            dx, dg = vjp(acc_s[...])
            dx_ref[...] = dy_ref[...] + dx
            dg_ref[...] += dg

        if ns:
            pl.when(jnp.logical_and(i == m // tm - 1, j == NDEV - 1))(finish)

    outs = pl.pallas_call(
        body, name="ffn_bwd", grid=(m // tm, NDEV),
        in_specs=[pl.BlockSpec((tm, D), lambda i, j: (i, 0)), pl.BlockSpec((tm, D), lambda i, j: (i, 0)),
                  pl.BlockSpec((1, D), lambda i, j: (0, 0))] + _ffn_specs(tm) + _any_specs(ns),
        out_specs=[pl.BlockSpec((tm, D), lambda i, j: (i, 0)), pl.BlockSpec((tm, FF_SHARD), lambda i, j: (i, j)),
                   pl.BlockSpec((tm, FF_SHARD), lambda i, j: (i, j)), pl.BlockSpec((tm, D), lambda i, j: (i, 0)),
                   pl.BlockSpec((1, D), lambda i, j: (0, 0))] + _any_specs(ns),
        out_shape=[jax.ShapeDtypeStruct((m, D), F32), jax.ShapeDtypeStruct((m, DFF), BF16), jax.ShapeDtypeStruct((m, DFF), BF16),
                   jax.ShapeDtypeStruct((m, D), BF16), jax.ShapeDtypeStruct((1, D), F32)] + _sibling_swap_shapes(swap),
        scratch_shapes=[pltpu.VMEM((tm, D), F32)] + (_sibling_swap_sems(ns) if ns else []),
        compiler_params=_params(dimension_semantics=("arbitrary", "arbitrary")),
    )(x, dy, gain, w1, w2, *swap)
    return outs[:5], outs[5:]


def _mm_nt_call(a, b, name):
    m, k = a.shape
    n = b.shape[0]
    tm = _row_tile(m)

    def body(a_ref, b_ref, out_ref):
        out_ref[...] = lax.dot_general(a_ref[...].astype(BF16), b_ref[...].astype(BF16), (((1,), (1,)), ((), ())),
                                       preferred_element_type=F32)

    return pl.pallas_call(
        body, name=name, grid=(m // tm,),
        in_specs=[pl.BlockSpec((tm, k), lambda i: (i, 0)), pl.BlockSpec((n, k), lambda i: (0, 0))],
        out_specs=pl.BlockSpec((tm, n), lambda i: (i, 0)),
        out_shape=jax.ShapeDtypeStruct((m, n), F32),
        compiler_params=_params(dimension_semantics=("arbitrary",)),
    )(a, b)


def _mm_tn_call(a, b, name, col_shards=False):
    m, k = a.shape
    n = b.shape[1]
    tm, tk = _row_tile(m, 1024), min(k, 1024)
    tn = n // NDEV if col_shards else min(n, 1024)

    def body(a_ref, b_ref, out_ref, acc_s):
        s = pl.program_id(2)

        @pl.when(s == 0)
        def _():
            acc_s[...] = jnp.zeros_like(acc_s)

        acc_s[...] += lax.dot_general(a_ref[...].astype(BF16), b_ref[...].astype(BF16), (((0,), (0,)), ((), ())),
                                      preferred_element_type=F32)

        @pl.when(s == m // tm - 1)
        def _():
            out_ref[...] = acc_s[...].astype(BF16).reshape(out_ref.shape)

    if col_shards:
        out_spec, out_shape = pl.BlockSpec((1, tk, tn), lambda i, j, s: (j, i, 0)), (NDEV, k, tn)
    else:
        out_spec, out_shape = pl.BlockSpec((tk, tn), lambda i, j, s: (i, j)), (k, n)
    return pl.pallas_call(
        body, name=name, grid=(k // tk, n // tn, m // tm),
        in_specs=[pl.BlockSpec((tm, tk), lambda i, j, s: (s, i)), pl.BlockSpec((tm, tn), lambda i, j, s: (s, j))],
        out_specs=out_spec, out_shape=jax.ShapeDtypeStruct(out_shape, BF16),
        scratch_shapes=[pltpu.VMEM((tk, tn), F32)],
        compiler_params=_params(dimension_semantics=("arbitrary", "arbitrary", "arbitrary")),
    )(a, b)


def _loss_call(y, target):
    m = y.shape[0]
    tm = _row_tile(m)

    def body(y_ref, t_ref, loss_ref, dy_ref):
        @pl.when(pl.program_id(0) == 0)
        def _():
            loss_ref[...] = jnp.zeros_like(loss_ref)

        err = y_ref[...] - t_ref[...]
        dy_ref[...] = err * (1.0 / D)
        per_row = jnp.mean(err * err, axis=-1, keepdims=True)
        loss_ref[...] += jnp.broadcast_to(0.5 * jnp.sum(per_row, axis=0, keepdims=True), (1, LANES))

    return pl.pallas_call(
        body, name="loss", grid=(m // tm,),
        in_specs=[pl.BlockSpec((tm, D), lambda i: (i, 0))] * 2,
        out_specs=[pl.BlockSpec((1, LANES), lambda i: (0, 0)), pl.BlockSpec((tm, D), lambda i: (i, 0))],
        out_shape=[jax.ShapeDtypeStruct((1, LANES), F32), jax.ShapeDtypeStruct((m, D), F32)],
        compiler_params=_params(dimension_semantics=("arbitrary",)),
    )(y, target)


def _adamw_call(w, g, m, v, name):
    shape = w.shape
    cols = shape[-1] if w.ndim > 1 else w.size
    rows = w.size // cols
    tr = rows if (rows <= 512 or rows % 512) else 512
    c1, c2 = 1.0 - ADAM_B1 ** ADAM_STEP, 1.0 - ADAM_B2 ** ADAM_STEP

    def body(w_ref, g_ref, m_ref, v_ref, d_ref, nm_ref, nv_ref):
        g_ = g_ref[...]
        nm = ADAM_B1 * m_ref[...] + (1.0 - ADAM_B1) * g_
        nv = ADAM_B2 * v_ref[...] + (1.0 - ADAM_B2) * (g_ * g_)
        d_ref[...] = -ADAM_LR * ((nm / c1) / (jnp.sqrt(nv / c2) + ADAM_EPS) + ADAM_WD * w_ref[...])
        nm_ref[...], nv_ref[...] = nm, nv

    spec = pl.BlockSpec((tr, cols), lambda i: (i, 0))
    outs = pl.pallas_call(
        body, name=name, grid=(rows // tr,), in_specs=[spec] * 4, out_specs=[spec] * 3,
        out_shape=[jax.ShapeDtypeStruct((rows, cols), F32)] * 3,
        compiler_params=_params(dimension_semantics=("arbitrary",)),
    )(*(t.reshape(rows, cols) for t in (w, g, m, v)))
    return tuple(o.reshape(shape) for o in outs)


def _sum_tile(rows):
    for cand in (2048, 1024, 512, 256, 128):
        if rows > cand and rows % cand == 0:
            return cand
    return rows


def _pair_sum_call(g, got, core, name):
    rows, cols = g.shape[1:]
    tr = _sum_tile(rows)

    def body(core_ref, g_ref, got_ref, out_ref):
        del core_ref
        out_ref[...] = (g_ref[...].astype(F32) + got_ref[...].astype(F32)).astype(BF16)

    grid_spec = pltpu.PrefetchScalarGridSpec(
        num_scalar_prefetch=1, grid=(4, rows // tr),
        in_specs=[pl.BlockSpec((1, tr, cols), lambda ch, t, core_ref: (2 * ch + core_ref[0], t, 0)),
                  pl.BlockSpec((1, tr, cols), lambda ch, t, core_ref: (ch, t, 0))],
        out_specs=pl.BlockSpec((1, tr, cols), lambda ch, t, core_ref: (ch, t, 0)))
    return pl.pallas_call(
        body, name=name, grid_spec=grid_spec, out_shape=jax.ShapeDtypeStruct((4, rows, cols), BF16),
        compiler_params=_params(dimension_semantics=("arbitrary", "arbitrary")),
    )(jnp.asarray(core, jnp.int32).reshape(1), g, got)


def _sum_call(parts, out_dtype, name):
    rows, cols = parts[0][0].shape[1:]
    tr = _sum_tile(rows)
    index = jnp.stack([jnp.asarray(i, jnp.int32) for _, i in parts])

    def body(idx_ref, *refs):
        del idx_ref
        acc = refs[0][0].astype(F32)
        for r in refs[1:-1]:
            acc = acc + r[0].astype(F32)
        refs[-1][...] = acc.astype(out_dtype)

    grid_spec = pltpu.PrefetchScalarGridSpec(
        num_scalar_prefetch=1, grid=(rows // tr,),
        in_specs=[pl.BlockSpec((1, tr, cols), lambda t, idx, n=n: (idx[n], t, 0)) for n in range(len(parts))],
        out_specs=pl.BlockSpec((tr, cols), lambda t, idx: (t, 0)))
    return pl.pallas_call(
        body, name=name, grid_spec=grid_spec, out_shape=jax.ShapeDtypeStruct((rows, cols), out_dtype),
        compiler_params=_params(dimension_semantics=("arbitrary",)),
    )(index, *(a for a, _ in parts))


def _place():
    return lax.axis_index("x"), lax.axis_index("y"), lax.axis_index("c")


def _any_specs(n):
    return [pl.BlockSpec(memory_space=pl.ANY)] * n


def _all_gather_call(xs, name):
    n = len(xs)

    def body(*refs):
        for phase in _gather_phases(refs[:n], refs[n:2 * n], *refs[2 * n:]):
            phase()

    return pl.pallas_call(
        body, name=name, in_specs=_any_specs(n), out_specs=_any_specs(n),
        out_shape=_gather_shapes(xs), scratch_shapes=_gather_sems(n),
    )(*xs)


def _gather_shapes(xs):
    return [jax.ShapeDtypeStruct((NDEV,) + x.shape, x.dtype) for x in xs]


def _gather_sems(n):
    return [pltpu.SemaphoreType.DMA((7 * n,)), pltpu.SemaphoreType.DMA((7 * n,)), pltpu.SemaphoreType.DMA((n,))]


def _gather_phases(x_refs, out_refs, send_sems, recv_sems, local_sems):
    n = len(x_refs)
    ax, ay, ac = _place()
    me, sibling = (ax, ay, ac), (ax, ay, 1 - ac)
    chips = [(1 - ax, ay), (ax, 1 - ay), (1 - ax, 1 - ay)]

    def copy(a, k, block, to, src=None):
        slot = out_refs[a].at[4 * block[0] + 2 * block[1] + block[2]]
        return pltpu.make_async_remote_copy(
            src_ref=slot if src is None else src, dst_ref=slot,
            send_sem=send_sems.at[7 * a + k], recv_sem=recv_sems.at[7 * a + k], device_id=to, device_id_type=MESH)

    local = [pltpu.make_async_copy(x_refs[a], out_refs[a].at[4 * ax + 2 * ay + ac], local_sems.at[a]) for a in range(n)]
    first = []
    for a in range(n):
        first.append(copy(a, 0, me, sibling, src=x_refs[a]))
        first += [copy(a, 1 + j, me, (*chip, ac), src=x_refs[a]) for j, chip in enumerate(chips)]
    passed = [copy(a, 4 + j, (*chip, ac), sibling) for j, chip in enumerate(chips) for a in range(n)]

    def send():
        for cp in local + first:
            cp.start()

    def forward():
        for j, chip in enumerate(chips):
            for a in range(n):
                copy(a, 1 + j, (*chip, ac), me).wait_recv()
                passed[j * n + a].start()

    def finish():
        for a in range(n):
            copy(a, 0, sibling, me).wait_recv()
            for j, chip in enumerate(chips):
                copy(a, 4 + j, (*chip, 1 - ac), me).wait_recv()
        for cp in first + passed:
            cp.wait_send()
        for cp in local:
            cp.wait()

    return send, forward, finish


def _swap_sibling_call(xs, name):
    n = len(xs)

    def body(*refs):
        for phase in _sibling_swap_phases(refs[:n], refs[n:2 * n], *refs[2 * n:]):
            phase()

    return pl.pallas_call(
        body, name=name, in_specs=_any_specs(n), out_specs=_any_specs(n),
        out_shape=_sibling_swap_shapes(xs), scratch_shapes=_sibling_swap_sems(n),
    )(*xs)


def _sibling_swap_shapes(xs):
    return [jax.ShapeDtypeStruct((4,) + x.shape[1:], x.dtype) for x in xs]


def _sibling_swap_sems(n):
    return [pltpu.SemaphoreType.DMA((n,)), pltpu.SemaphoreType.DMA((n,))]


def _sibling_swap_phases(x_refs, out_refs, send_sems, recv_sems):
    ax, ay, ac = _place()
    sibling = (ax, ay, 1 - ac)

    def send():
        for a, (x_ref, out_ref) in enumerate(zip(x_refs, out_refs)):
            for chip in range(4):
                pltpu.make_async_remote_copy(src_ref=x_ref.at[2 * chip + 1 - ac], dst_ref=out_ref.at[chip],
                                             send_sem=send_sems.at[a], recv_sem=recv_sems.at[a],
                                             device_id=sibling, device_id_type=MESH).start()

    def finish():
        for a, (x_ref, out_ref) in enumerate(zip(x_refs, out_refs)):
            pltpu.make_async_remote_copy(src_ref=x_ref.at[pl.ds(0, 4)], dst_ref=out_ref, send_sem=send_sems.at[a],
                                         recv_sem=recv_sems.at[a], device_id=sibling, device_id_type=MESH).wait()

    return send, finish


def _chip_swap_shapes(xs):
    return [jax.ShapeDtypeStruct((3,) + x.shape[1:], x.dtype) for x in xs]


def _chip_swap_sems(n):
    return [pltpu.SemaphoreType.DMA((3 * n,)), pltpu.SemaphoreType.DMA((3 * n,))]


def _chip_swap_phases(x_refs, out_refs, send_sems, recv_sems):
    ax, ay, ac = _place()
    chips = [(1 - ax, ay), (ax, 1 - ay), (1 - ax, 1 - ay)]
    copies = [pltpu.make_async_remote_copy(src_ref=x_refs[a].at[2 * cx + cy], dst_ref=out_refs[a].at[j],
                                           send_sem=send_sems.at[3 * a + j], recv_sem=recv_sems.at[3 * a + j],
                                           device_id=(cx, cy, ac), device_id_type=MESH)
              for a in range(len(x_refs)) for j, (cx, cy) in enumerate(chips)]

    def send():
        for cp in copies:
            cp.start()

    def finish():
        for cp in copies:
            cp.wait()

    return send, finish


def _reduce_begin(gs, name):
    ac = lax.axis_index("c")
    got = _swap_sibling_call(gs, name + "_d2d")
    return got, [_pair_sum_call(g, t, ac, f"{name}_pair{a}") for a, (g, t) in enumerate(zip(gs, got))]


def _reduce_end(gs, got, from_chips, name):
    ax, ay, ac = _place()
    me, my_chip = 4 * ax + 2 * ay + ac, 2 * ax + ay
    return [_sum_call([(g, me), (t, my_chip), (f, 0), (f, 1), (f, 2)], F32, f"{name}_total{a}")
            for a, (g, t, f) in enumerate(zip(gs, got, from_chips))]


SMALL = ("norm1_g", "conv_w", "a_log", "dt_bias", "dn_out_g", "sb_q_g", "sb_k_g", "sg_v_g", "sg_w", "sg_b", "norm2_g")
WEIGHTS = ("norm1_g", "w_in", "conv_w", "a_log", "dt_bias", "dn_out_g", "sb_q_g", "sb_k_g", "sg_v_g", "sg_w", "sg_b",
           "w_out", "norm2_g", "w_ff1", "w_ff2")
SMALL_SHAPE = {"norm1_g": (D,), "conv_w": (4, 3 * DN_W), "a_log": (NH,), "dt_bias": (NH,), "dn_out_g": (128,), "sb_q_g": (64,),
               "sb_k_g": (64,), "sg_v_g": (SG_W,), "sg_w": (NH, 128, 128), "sg_b": (NH, 128), "norm2_g": (D,)}


def _size(shape):
    n = 1
    for s in shape:
        n *= s
    return n


def _to_rows(flat, multiple):
    pad = (-flat.shape[0]) % (LANES * multiple)
    return jnp.pad(flat, (0, pad)).reshape(-1, LANES)


def _conv_by_pair(conv):
    return conv.reshape(4, 3, 2, 256).transpose(0, 2, 1, 3).reshape(4, 3 * DN_W)


def kernel(x, norm1_g, w_in, conv_w, a_log, dt_bias, dn_out_g, sb_q_g, sb_k_g, sg_v_g, sg_w, sg_b, w_out, norm2_g, w_ff1, w_ff2, loss_target, m_norm1_g, m_w_in, m_conv_w, m_a_log, m_dt_bias, m_dn_out_g, m_sb_q_g, m_sb_k_g, m_sg_v_g, m_sg_w, m_sg_b, m_w_out, m_norm2_g, m_w_ff1, m_w_ff2, v_norm1_g, v_w_in, v_conv_w, v_a_log, v_dt_bias, v_dn_out_g, v_sb_q_g, v_sb_k_g, v_sg_v_g, v_sg_w, v_sg_b, v_w_out, v_norm2_g, v_w_ff1, v_w_ff2):
    given = dict(norm1_g=norm1_g, w_in=w_in, conv_w=conv_w, a_log=a_log, dt_bias=dt_bias, dn_out_g=dn_out_g, sb_q_g=sb_q_g,
                 sb_k_g=sb_k_g, sg_v_g=sg_v_g, sg_w=sg_w, sg_b=sg_b, w_out=w_out, norm2_g=norm2_g, w_ff1=w_ff1, w_ff2=w_ff2)
    mom = dict(norm1_g=m_norm1_g, w_in=m_w_in, conv_w=m_conv_w, a_log=m_a_log, dt_bias=m_dt_bias, dn_out_g=m_dn_out_g,
               sb_q_g=m_sb_q_g, sb_k_g=m_sb_k_g, sg_v_g=m_sg_v_g, sg_w=m_sg_w, sg_b=m_sg_b, w_out=m_w_out, norm2_g=m_norm2_g,
               w_ff1=m_w_ff1, w_ff2=m_w_ff2)
    var = dict(norm1_g=v_norm1_g, w_in=v_w_in, conv_w=v_conv_w, a_log=v_a_log, dt_bias=v_dt_bias, dn_out_g=v_dn_out_g,
               sb_q_g=v_sb_q_g, sb_k_g=v_sb_k_g, sg_v_g=v_sg_v_g, sg_w=v_sg_w, sg_b=v_sg_b, w_out=v_w_out, norm2_g=v_norm2_g,
               w_ff1=v_w_ff1, w_ff2=v_w_ff2)
    B, T, _ = x.shape
    M = B * T
    ax, ay, ac = _place()
    me = 4 * ax + 2 * ay + ac
    table_fwd, table_back = _row_tables()

    send = []
    for l in range(2):
        w_in_t = jnp.pad(w_in[l].T, ((0, IN_SHARD_PAD - IN_SHARD), (0, 0)))
        send.append([w_in_t.astype(BF16), w_out[l].astype(BF16), w_ff1[l].astype(BF16), w_ff2[l].astype(BF16)])
    first_in, conv_rows = _all_gather_call([send[0][0], _to_rows(conv_w.reshape(-1), 8)], "gather_first")
    conv_full = conv_rows.reshape(NDEV, -1)[:, :conv_w.size].reshape(NDEV, 2, 4, -1).transpose(1, 2, 0, 3).reshape(2, 4, 3 * DN_W)
    gathered = [[first_in, None, None, None], [None] * 4]

    pad_vec = lambda v: jnp.zeros((1, LANES), F32).at[0, :v.shape[0]].set(v)
    layer = []
    for l in range(2):
        layer.append(dict(
            g1=norm1_g[l].reshape(1, D), g2=norm2_g[l].reshape(1, D), conv=_conv_by_pair(conv_full[l]),
            a_log=pad_vec(a_log[l]), dt_bias=pad_vec(dt_bias[l]), dn_g=dn_out_g[l].reshape(1, LANES),
            sb_qg=jnp.tile(sb_q_g[l], 2).reshape(1, LANES), sb_kg=jnp.tile(sb_k_g[l], 2).reshape(1, LANES),
            sg_g=sg_v_g[l].reshape(1, SG_W), sg_w=sg_w[l], sg_bias=jnp.repeat(sg_b[l].T, 64, axis=1)))

    cur = x.reshape(M, D)
    saved = []
    for l, p in enumerate(layer):
        p["wt"] = _row_perm_call(gathered[l][0].reshape(NDEV * IN_SHARD_PAD, D), table_fwd, "pack_w_in")
        p_dn, p_sb, p_sg, p_ab, h = _in_proj_call(cur, p["g1"], p["wt"])
        mix, dn_kept, arrived = _dn_fwd_call(p_dn, p_ab, p["conv"], p["a_log"], p["dt_bias"], p["dn_g"], B, T,
                                             gather=send[0][1:] + send[1][:1] if l == 0 else [])
        if l == 0:
            gathered[0][1:], gathered[1][0] = list(arrived[:3]), arrived[3]
        p["w_out"], p["w1"], p["w2"] = gathered[l][1].reshape(D, D), gathered[l][2], gathered[l][3].reshape(DFF, D)
        mix, sb_carries, arrived = _sb_fwd_call(p_sb, mix, p["sb_qg"], p["sb_kg"], B, T, gather=send[1][1:] if l == 0 else [])
        if l == 0:
            gathered[1][1:] = list(arrived)
        mix = _sg_fwd_call(p_sg, mix, p["sg_g"], p["sg_w"], p["sg_bias"], B, T)
        x1 = _out_proj_call(mix, p["w_out"], cur)
        x2 = _ffn_fwd_call(x1, p["g2"], p["w1"], p["w2"])
        saved.append(dict(x0=cur, p_dn=p_dn, p_sb=p_sb, p_sg=p_sg, p_ab=p_ab, h=h, mix=mix, x1=x1, dn_kept=dn_kept, sb_carries=sb_carries))
        cur = x2
    loss_part, dy = _loss_call(cur, loss_target.reshape(M, D))
    loss = lax.psum(loss_part[0, 0], ("x", "y", "c"))

    big_grads = [[None] * 4, [None] * 4]
    small_grads = {n: [None, None] for n in SMALL}
    for l in (1, 0):
        p, s = layer[l], saved[l]
        (dx1, da, r, h2, dg2), got1 = _ffn_bwd_call(s["x1"], dy, p["g2"], p["w1"], p["w2"], swap=big_grads[1] if l == 0 else ())
        big_grads[l][2] = _mm_tn_call(h2, da, "grad_w_ff1", col_shards=True)
        big_grads[l][3] = _mm_tn_call(r, dy, "grad_w_ff2").reshape(NDEV, FF_SHARD, D)
        dmix = _mm_nt_call(dx1, p["w_out"], "dmix")
        big_grads[l][1] = _mm_tn_call(s["mix"], dx1, "grad_w_out").reshape(NDEV, D // NDEV, D)
        if l == 0:
            got0, sums0 = _reduce_begin(big_grads[0][1:], "reduce_early0")
            early, early_got = big_grads[1] + big_grads[0][1:], list(got1) + list(got0)
            early_sums = [_pair_sum_call(g, t, ac, f"reduce_early1_pair{a}") for a, (g, t) in enumerate(zip(big_grads[1], got1))] + sums0
        (d_dn, d_ab, dcw, dalog, ddtb, ddn_g), early_from = _dn_bwd_call(
            s["p_dn"], s["p_ab"], dmix, s["dn_kept"], p["conv"], p["a_log"], p["dt_bias"], p["dn_g"], B, T,
            swap=early_sums if l == 0 else ())
        d_sb, dqg, dkg = _sb_bwd_call(s["p_sb"], dmix, s["sb_carries"], p["sb_qg"], p["sb_kg"], B, T)
        d_sg, dsg_g, dsg_w, dsg_b = _sg_bwd_call(s["p_sg"], dmix, p["sg_g"], p["sg_w"], p["sg_bias"], B, T)
        dsections = (d_dn, d_sb, d_sg, d_ab)
        dwt = _in_proj_grad_call(dsections, s["h"])
        big_grads[l][0] = _row_perm_call(dwt, table_back, "unpack_grad_w_in").reshape(NDEV, IN_SHARD_PAD, D)
        if l == 0:
            last = big_grads[0][:1]
            last_got, last_sums = _reduce_begin(last, "reduce_last")
        (dy, dg1), last_from = _in_proj_bwd_call(dsections, p["wt"], s["x0"], p["g1"], dx1, swap=last_sums if l == 0 else ())
        for n, val in (("norm1_g", dg1[0]), ("conv_w", dcw.transpose(1, 0, 2).reshape(4, 3 * DN_W)), ("a_log", dalog[0, :NH]),
                       ("dt_bias", ddtb[0, :NH]), ("dn_out_g", ddn_g[0]), ("sb_q_g", dqg[0, :64]), ("sb_k_g", dkg[0, :64]),
                       ("sg_v_g", dsg_g[0]), ("sg_w", dsg_w), ("sg_b", dsg_b[:, :NH].T), ("norm2_g", dg2[0])):
            small_grads[n][l] = val
    grad_x = dy.reshape(B, T, D)

    mine0 = _reduce_end(last, last_got, last_from, "reduce_last")
    mine1 = _reduce_end(early, early_got, early_from, "reduce_early")
    grads = {"w_in": jnp.stack([mine0[0][:IN_SHARD].T, mine1[0][:IN_SHARD].T]), "w_out": jnp.stack([mine1[4], mine1[1]]),
             "w_ff1": jnp.stack([mine1[5], mine1[2]]), "w_ff2": jnp.stack([mine1[6], mine1[3]])}
    small_flat = jnp.concatenate([jnp.stack(small_grads[n]).reshape(-1) for n in SMALL])
    everyone, = _all_gather_call([_to_rows(small_flat, 8)], "gather_small_grads")
    small_sum = _sum_call([(everyone, k) for k in range(NDEV)], F32, "sum_small_grads").reshape(-1)
    off = 0
    for n in SMALL:
        sz = 2 * _size(SMALL_SHAPE[n])
        grads[n] = small_sum[off:off + sz].reshape((2,) + SMALL_SHAPE[n])
        off += sz
    cshard = conv_w.shape[-1]
    grads["conv_w"] = lax.dynamic_slice_in_dim(grads["conv_w"], me * cshard, cshard, axis=2)

    deltas, new_m, new_v = {}, {}, {}
    for n in WEIGHTS:
        deltas[n], new_m[n], new_v[n] = _adamw_call(given[n], grads[n], mom[n], var[n], "adamw_" + n)
    return (loss, grad_x, *[grads[n] for n in WEIGHTS], *[deltas[n] for n in WEIGHTS], *[new_m[n] for n in WEIGHTS],
            *[new_v[n] for n in WEIGHTS])
```

```python
import functools

import numpy as np

import jax
import jax.numpy as jnp
from jax import lax
from jax.experimental import pallas as pl
from jax.experimental.pallas import tpu as pltpu

F32, BF16 = jnp.float32, jnp.bfloat16
EPS = 1e-6
LANES = 128
D = 1024
DFF = 4096
NH = 4
DN_W, SB_W, SG_W = 512, 256, 256
IN_DIM = 3336
NDEV = 8
IN_SHARD = IN_DIM // NDEV
IN_SHARD_PAD = 432
FF_SHARD = DFF // NDEV
DN_OFF, SB_OFF, SG_OFF, AB_OFF, NPACK = 0, 2048, 2816, 3328, 3456
SECTIONS = ((DN_OFF, 2048), (SB_OFF, 768), (SG_OFF, 512), (AB_OFF, 128))
SB_SCALE = 64 ** -0.5
DN_SCALE = 128 ** -0.5
VMEM_LIMIT = 56 * 1024 * 1024
VMEM_LIMIT_MAX = 62 * 1024 * 1024
ADAM_LR, ADAM_B1, ADAM_B2, ADAM_EPS, ADAM_WD, ADAM_STEP = 0.001, 0.9, 0.999, 1e-08, 0.01, 10
MESH = pl.DeviceIdType.MESH


def _iota(shape, dim):
    return lax.broadcasted_iota(jnp.int32, shape, dim)


def _params(**kw):
    return pltpu.CompilerParams(vmem_limit_bytes=VMEM_LIMIT, **kw)


NN, NT, TN = ((1,), (0,)), ((1,), (1,)), ((0,), (0,))


def _mm(a, b, dims):
    return lax.dot_general(a.astype(BF16), b.astype(BF16), (dims, ((), ())), preferred_element_type=F32)


def _plain(a, b, dims):
    return (a.T if dims == TN else a), (b.T if dims == NT else b)


def _mmx(a, b, dims):
    return _mm(*_plain(a, b, dims), NN)


@jax.custom_vjp
def _dot(a, b):
    return _mmx(a, b, NN)


def _dot_fwd(a, b):
    return _dot(a, b), (a, b)


def _dot_bwd(res, g):
    a, b = res
    return _mmx(g, b, NT).astype(a.dtype), _mmx(a, g, TN).astype(b.dtype)


_dot.defvjp(_dot_fwd, _dot_bwd)


def _split(x):
    hi = x.astype(BF16)
    return hi, (x - hi.astype(F32)).astype(BF16)


def _mm2(a, b):
    ah, al = _split(a)
    bh = b.astype(BF16)
    mm = lambda x, y: jnp.dot(x, y, preferred_element_type=F32)
    return mm(ah, bh) + mm(al, bh)


def _mm_ones(ones, x, ones_left):
    hi, lo = _split(x)
    mm = (lambda t: jnp.dot(ones, t, preferred_element_type=F32)) if ones_left else \
         (lambda t: jnp.dot(t, ones, preferred_element_type=F32))
    return mm(hi) + mm(lo)


def _pair_ones(kind, transposed):
    row, col = _iota((128, 128), 0), _iota((128, 128), 1)
    m = (row // 64) == (col // 64)
    if kind == "running":
        m = jnp.logical_and(m, (col >= row) if transposed else (col <= row))
    return jnp.where(m, 1.0, 0.0).astype(BF16)


@functools.partial(jax.custom_vjp, nondiff_argnums=(0,))
def _chunk_sum(kind, x):
    return _mm_ones(_pair_ones(kind, False), x, True)


def _chunk_sum_fwd(kind, x):
    return _chunk_sum(kind, x), None


def _chunk_sum_bwd(kind, _, g):
    return (_mm_ones(_pair_ones(kind, True), g, True),)


_chunk_sum.defvjp(_chunk_sum_fwd, _chunk_sum_bwd)


def _tri_ones(n, transposed):
    row, col = _iota((n, n), 0), _iota((n, n), 1)
    return jnp.where((row < col) if transposed else (row > col), 1.0, 0.0).astype(BF16)


def _sigmoid(x):
    return jax.nn.sigmoid(x)


def _silu(x):
    return x * _sigmoid(x)


def _softplus(x):
    return jnp.maximum(x, 0.0) + jnp.log1p(jnp.exp(-jnp.abs(x)))


def _gelu(x):
    return 0.5 * x * (1.0 + jnp.tanh(0.7978845608028654 * (x + 0.044715 * (x * x * x))))


def _rms(x, gain):
    return x * lax.rsqrt(jnp.mean(x * x, axis=-1, keepdims=True) + EPS) * gain


def _shift_down_impl(x, k):
    return jnp.where(_iota(x.shape, 0) >= k, pltpu.roll(x, k, 0), 0.0)


def _shift_up_impl(x, k):
    n = x.shape[0]
    return jnp.where(_iota(x.shape, 0) < n - k, pltpu.roll(x, n - k, 0), 0.0)


@functools.partial(jax.custom_vjp, nondiff_argnums=(1,))
def _shift_down(x, k):
    return _shift_down_impl(x, k)


def _shift_down_fwd(x, k):
    return _shift_down_impl(x, k), None


def _shift_down_bwd(k, _, g):
    return (_shift_up_impl(g, k),)


_shift_down.defvjp(_shift_down_fwd, _shift_down_bwd)


def _lane_pick(x, idx):
    return jnp.sum(jnp.where(_iota(x.shape, 1) == idx, x, 0.0), axis=-1, keepdims=True)


def _dn_conv(x, w0, w1, w2, w3, l2_scale):
    y = _silu(w3 * x + w2 * _shift_down(x, 1) + w1 * _shift_down(x, 2) + w0 * _shift_down(x, 3))
    if l2_scale is None:
        return y
    return y * lax.rsqrt(jnp.sum(y * y, axis=-1, keepdims=True) + EPS) * l2_scale


def _dn_gates(ab, a_log, dt_bias):
    lane = _iota((1, LANES), 1)
    g = -jnp.exp(a_log) * _softplus(ab + dt_bias)
    return jnp.where(lane < NH, g, jnp.where(lane < 2 * NH, _sigmoid(ab), 0.0))


def _same_head(shape):
    return (_iota(shape, 0) < LANES) == (_iota(shape, 1) < LANES)


def _bd(r2):
    return jnp.where(_same_head((2 * LANES, 2 * LANES)), jnp.concatenate([r2, r2], axis=0), 0.0)


def _bd_t(y2):
    t = y2.T
    return jnp.where(_same_head((2 * LANES, 2 * LANES)), jnp.concatenate([t, t], axis=1), 0.0)


def _pair_prod(kind, a2, b2, mm):
    if kind == NN:
        return mm(a2, _bd(b2))
    if kind == NT:
        return mm(a2, _bd_t(b2))
    full = mm(a2.T, b2)
    return jnp.concatenate([full[:LANES, :LANES], full[LANES:, LANES:]], axis=1)


_MM1 = lambda x, y: _mm(x, y, NN)


def _pair_vjp_rule(kind, a2, b2, g, mm):
    if kind == NN:
        return _pair_prod(NT, g, b2, mm), _pair_prod(TN, a2, g, mm)
    if kind == NT:
        return _pair_prod(NN, g, b2, mm), _pair_prod(TN, g, a2, mm)
    return _pair_prod(NT, b2, g, mm), _pair_prod(NN, a2, g, mm)


@functools.partial(jax.custom_vjp, nondiff_argnums=(0,))
def _pdot(kind, a2, b2):
    return _pair_prod(kind, a2, b2, _MM1)


def _pdot_fwd(kind, a2, b2):
    return _pdot(kind, a2, b2), (a2, b2)


def _pdot_bwd(kind, res, g):
    return _pair_vjp_rule(kind, *res, g, _MM1)


_pdot.defvjp(_pdot_fwd, _pdot_bwd)


def _unit_lower_inverse(lower):
    n = lower.shape[0]
    nk = -lower
    inv = jnp.where(_iota(lower.shape, 0) == jnp.bitwise_and(_iota(lower.shape, 1), n - 1), 1.0, 0.0) + nk
    for _ in range(5):
        nk = _pair_prod(NN, nk, nk, _MM1)
        inv = inv + _pair_prod(NN, inv, nk, _MM1)
    return inv


@jax.custom_vjp
def _solve_with(lower, inv, rhs):
    return _pair_prod(NN, inv, rhs, _mm2)


def _solve_with_fwd(lower, inv, rhs):
    x = _pair_prod(NN, inv, rhs, _mm2)
    return x, (inv, x)


def _solve_with_bwd(res, g):
    inv, x = res
    d_rhs = _pair_prod(TN, inv, g, _mm2)
    return -_pair_prod(NT, d_rhs, x, _MM1), jnp.zeros_like(inv), d_rhs


_solve_with.defvjp(_solve_with_fwd, _solve_with_bwd)


def _dn_local(q, k, v, g, beta, inv=None):
    shape = (LANES, 2 * LANES)
    row, col = _iota(shape, 0), jnp.bitwise_and(_iota(shape, 1), LANES - 1)
    same = (row // 64) == (col // 64)
    tri_incl = jnp.logical_and(same, col <= row)
    tri_strict = jnp.logical_and(same, col < row)
    first = row < 64
    gc = _chunk_sum("running", g)
    gl = _chunk_sum("total", g)
    diff = gc - jnp.concatenate([gc[:, :LANES].T, gc[:, LANES:].T], axis=1)
    decay = jnp.where(tri_incl, jnp.exp(jnp.where(tri_incl, diff, 0.0)), 0.0)
    egc = jnp.exp(gc)
    lower = jnp.where(tri_strict, beta * _pdot(NT, k, k) * decay, 0.0)
    if inv is None:
        inv = _unit_lower_inverse(lower)
    u_val = _solve_with(lower, inv, v * beta)
    w_dec = _solve_with(lower, inv, k * (beta * egc))
    qk = jnp.where(tri_incl, _pdot(NT, q, k) * decay, 0.0)
    q_dec = q * egc
    k_dec = k * jnp.exp(gl - gc)
    cd1 = jnp.exp(jnp.sum(jnp.where(first, g, 0.0), axis=0, keepdims=True))
    cd2 = jnp.exp(jnp.sum(jnp.where(first, 0.0, g), axis=0, keepdims=True))
    return (u_val, w_dec, qk, q_dec, k_dec, cd1, cd2), inv


def _dn_state(u_val, w_dec, qk, q_dec, k_dec, cd1, cd2, s0):
    first = _iota((LANES, 2 * LANES), 0) < 64
    u1 = u_val - _pdot(NN, w_dec, s0)
    s1 = s0 * cd1 + _pdot(TN, jnp.where(first, k_dec, 0.0), u1)
    u2 = u_val - _pdot(NN, w_dec, s1)
    u_new = jnp.where(first, u1, u2)
    s2 = s1 * cd2 + _pdot(TN, jnp.where(first, 0.0, k_dec), u_new)
    o = jnp.where(first, _pdot(NN, q_dec, s0), _pdot(NN, q_dec, s1)) + _pdot(NN, qk, u_new)
    return o, s2


def _dn_post(o, z, gain):
    return _rms(o, gain) * _silu(z)


_DN_L2 = (DN_SCALE, 1.0, None)
DN_HPS = 2
DN_BLK = 4 * DN_HPS * LANES
_DN_COLS = tuple(slice(i * LANES, (i + 1) * LANES) for i in range(DN_HPS))


def _dn_in_cols(s, i):
    return slice((s * DN_HPS + i) * LANES, (s * DN_HPS + i + 1) * LANES)


def _dn_taps(cw_ref, s, i):
    return tuple(cw_ref[t:t + 1, _dn_in_cols(s, i)] for t in range(4))


def _pair_rows(n):
    return pl.ds(pl.multiple_of(n * 128, 128), 128)


def _dn_gate_rows(gate, hp):
    head_a = _iota((1, DN_HPS * LANES), 1) < LANES
    h = DN_HPS * hp
    return (jnp.where(head_a, _lane_pick(gate, h), _lane_pick(gate, h + 1)),
            jnp.where(head_a, _lane_pick(gate, NH + h), _lane_pick(gate, NH + h + 1)))


def _dn_gate_cols(dg, db, hp):
    head_a = _iota((1, DN_HPS * LANES), 1) < LANES
    lane = _iota((1, LANES), 1)
    h = DN_HPS * hp
    out = 0.0
    for t, first in ((dg, h), (db, NH + h)):
        out = out + jnp.where(lane == first, jnp.sum(jnp.where(head_a, t, 0.0), axis=-1, keepdims=True), 0.0)
        out = out + jnp.where(lane == first + 1, jnp.sum(jnp.where(head_a, 0.0, t), axis=-1, keepdims=True), 0.0)
    return out


def _dn_in_specs(T):
    one = pl.Buffered(1)
    vec = pl.BlockSpec((1, LANES), lambda b, h: (0, 0))
    return [pl.BlockSpec((T, DN_BLK), lambda b, h: (b, h), pipeline_mode=one),
            pl.BlockSpec((T, LANES), lambda b, h: (b, 0), pipeline_mode=one),
            pl.BlockSpec((4, 3 * DN_HPS * LANES), lambda b, h: (0, h)), vec, vec, vec]


def _dn_fwd_call(proj_dn, proj_ab, conv_w, a_log, dt_bias, gain, B, T, gather=()):
    npair = T // 128
    ng = len(gather)
    nsteps = B * (NH // DN_HPS)

    def body(*refs):
        x_ref, ab_ref, cw_ref, alog_ref, dtb_ref, gain_ref = refs[:6]
        out_ref, q_s, k_s, v_s, o_s, gate_s, st_s, inv_s = refs[6 + ng:14 + ng]
        step_id = pl.program_id(0) * (NH // DN_HPS) + pl.program_id(1)
        if ng:
            send, forward, finish = _gather_phases(refs[6:6 + ng], refs[14 + ng:14 + 2 * ng], *refs[14 + 2 * ng:])
            pl.when(step_id == 0)(send)
            pl.when(step_id == nsteps - 1)(forward)
        hp = pl.program_id(1)
        for i, cs in enumerate(_DN_COLS):
            for s, (x_s, l2) in enumerate(zip((q_s, k_s, v_s), _DN_L2)):
                x_s[:, cs] = _dn_conv(x_ref[:, _dn_in_cols(s, i)], *_dn_taps(cw_ref, s, i), l2)
        gate_s[...] = _dn_gates(ab_ref[...], alog_ref[...], dtb_ref[...])

        def local_of(pair):
            r = _pair_rows(pair)
            loc, inv = _dn_local(q_s[r, :], k_s[r, :], v_s[r, :], *_dn_gate_rows(gate_s[r, :], hp))
            inv_s[0, 0, pair] = inv
            return loc

        def state_of(n, loc, state):
            st_s[0, 0, n] = state
            o, s2 = _dn_state(*loc, state)
            o_s[_pair_rows(n), :] = o
            return s2

        def step(n, carry):
            loc, state = carry
            return local_of(n + 1), state_of(n, loc, state)

        loc, state = lax.fori_loop(0, npair - 1, step, (local_of(0), jnp.zeros((LANES, DN_HPS * LANES), F32)))
        state_of(npair - 1, loc, state)
        for i, cs in enumerate(_DN_COLS):
            out_ref[:, cs] = _dn_post(o_s[:, cs], x_ref[:, _dn_in_cols(3, i)], gain_ref[...])
        if ng:
            pl.when(step_id == nsteps - 1)(finish)

    kept_specs, kept_shapes = _dn_kept(B, T)
    outs = pl.pallas_call(
        body, name="dn_fwd", grid=(B, NH // DN_HPS), in_specs=_dn_in_specs(T) + _any_specs(ng),
        out_specs=[pl.BlockSpec((T, DN_HPS * LANES), lambda b, h: (b, h), pipeline_mode=pl.Buffered(1))] + kept_specs + _any_specs(ng),
        out_shape=[jax.ShapeDtypeStruct((B * T, D), F32)] + kept_shapes + _gather_shapes(gather),
        scratch_shapes=_gather_sems(ng) if ng else [],
        compiler_params=_params(dimension_semantics=("arbitrary", "arbitrary")),
    )(proj_dn, proj_ab, conv_w, a_log, dt_bias, gain, *gather)
    return outs[0], outs[1:8], outs[8:]


def _dn_kept(B, T):
    one = pl.Buffered(1)
    npair, pairs = T // 128, NH // DN_HPS
    wide = pl.BlockSpec((T, DN_HPS * LANES), lambda b, h: (b, h), pipeline_mode=one)
    per_pair = pl.BlockSpec((1, 1, npair, LANES, DN_HPS * LANES), lambda b, h: (b, h, 0, 0, 0), pipeline_mode=one)
    specs = [wide] * 4 + [pl.BlockSpec((T, LANES), lambda b, h: (b, h), pipeline_mode=one)] + [per_pair] * 2
    shapes = ([jax.ShapeDtypeStruct((B * T, DN_W), F32)] * 4 + [jax.ShapeDtypeStruct((B * T, pairs * LANES), F32)]
              + [jax.ShapeDtypeStruct((B, pairs, npair, LANES, DN_HPS * LANES), F32)] * 2)
    return specs, shapes


def _dn_bwd_call(proj_dn, proj_ab, dmix, kept, conv_w, a_log, dt_bias, gain, B, T, swap=()):
    npair = T // 128
    ns = len(swap)
    nsteps = B * (NH // DN_HPS)

    def body(*refs):
        x_ref, ab_ref, cw_ref, alog_ref, dtb_ref, gain_ref, do_ref, q_s, k_s, v_s, o_ref, gate_s, st_s, inv_s = refs[:14]
        dx_ref, dab_ref, dcw_ref, dalog_ref, ddtb_ref, dgain_ref = refs[14 + ns:20 + ns]
        dgate_s, do_s = refs[20 + 2 * ns:22 + 2 * ns]
        b_i, hp = pl.program_id(0), pl.program_id(1)
        step_id = b_i * (NH // DN_HPS) + hp
        if ns:
            send, finish = _chip_swap_phases(refs[14:14 + ns], refs[20 + ns:20 + 2 * ns], *refs[22 + 2 * ns:])
            pl.when(step_id == 0)(send)

        def pair_in(r):
            return (q_s[r, :], k_s[r, :], v_s[r, :]) + _dn_gate_rows(gate_s[r, :], hp)

        zero_state = jnp.zeros((LANES, DN_HPS * LANES), F32)

        @pl.when(jnp.logical_and(b_i == 0, hp == 0))
        def _():
            dcw_ref[...] = jnp.zeros_like(dcw_ref)
            dalog_ref[...] = jnp.zeros_like(dalog_ref)
            ddtb_ref[...] = jnp.zeros_like(ddtb_ref)
            dgain_ref[...] = jnp.zeros_like(dgain_ref)

        for i, cs in enumerate(_DN_COLS):
            zc = _dn_in_cols(3, i)
            _, post_vjp = jax.vjp(_dn_post, o_ref[:, cs], x_ref[:, zc], gain_ref[...])
            do, dz, dgain = post_vjp(do_ref[:, cs])
            dx_ref[:, zc] = dz
            do_s[:, cs] = do
            dgain_ref[...] += dgain

        wide_cols = lambda s: slice(s * DN_HPS * LANES, (s + 1) * DN_HPS * LANES)

        def back_step(nn, dstate):
            n = npair - 1 - nn
            r = _pair_rows(n)
            inv = inv_s[0, 0, n]
            local = lambda q, k, v, g, beta, inv=inv: _dn_local(q, k, v, g, beta, inv)[0]
            loc, local_vjp = jax.vjp(local, *pair_in(r))
            _, state_vjp = jax.vjp(_dn_state, *loc, st_s[0, 0, n])
            *dloc, ds0 = state_vjp((do_s[r, :], dstate))
            dq, dk, dv, dg, db = local_vjp(tuple(dloc))
            dx_ref[r, wide_cols(0)], dx_ref[r, wide_cols(1)], dx_ref[r, wide_cols(2)] = dq, dk, dv
            dgate_s[r, :] = _dn_gate_cols(dg, db, hp)
            return ds0

        lax.fori_loop(0, npair, back_step, zero_state)

        for i, cs in enumerate(_DN_COLS):
            h = DN_HPS * hp + i
            for s, l2 in enumerate(_DN_L2):
                xc = _dn_in_cols(s, i)
                _, conv_vjp = jax.vjp(functools.partial(_dn_conv, l2_scale=l2), x_ref[:, xc], *_dn_taps(cw_ref, s, i))
                dx, *dw = conv_vjp(dx_ref[:, xc])
                dx_ref[:, xc] = dx
                for t in range(4):
                    dcw_ref[h + 4 * s, t:t + 1, :] += dw[t]
        _, gate_vjp = jax.vjp(_dn_gates, ab_ref[...], alog_ref[...], dtb_ref[...])
        dab, dalog, ddtb = gate_vjp(dgate_s[...])
        dalog_ref[...] += dalog
        ddtb_ref[...] += ddtb

        @pl.when(hp == 0)
        def _():
            dab_ref[...] = jnp.zeros_like(dab_ref)

        dab_ref[...] += dab
        if ns:
            pl.when(step_id == nsteps - 1)(finish)

    M = B * T
    one = pl.Buffered(1)
    vec = pl.BlockSpec((1, LANES), lambda b, h: (0, 0))
    wide = [pltpu.VMEM((T, DN_HPS * LANES), F32)]
    vec_shape = jax.ShapeDtypeStruct((1, LANES), F32)
    outs = pl.pallas_call(
        body, name="dn_bwd", grid=(B, NH // DN_HPS),
        in_specs=_dn_in_specs(T) + [pl.BlockSpec((T, DN_HPS * LANES), lambda b, h: (b, h), pipeline_mode=one)] + _dn_kept(B, T)[0]
        + _any_specs(ns),
        out_specs=[pl.BlockSpec((T, DN_BLK), lambda b, h: (b, h), pipeline_mode=one), pl.BlockSpec((T, LANES), lambda b, h: (b, 0)),
                   pl.BlockSpec((12, 4, LANES), lambda b, h: (0, 0, 0)), vec, vec, vec] + _any_specs(ns),
        out_shape=[jax.ShapeDtypeStruct((M, 4 * DN_W), F32), jax.ShapeDtypeStruct((M, LANES), F32),
                   jax.ShapeDtypeStruct((12, 4, LANES), F32), vec_shape, vec_shape, vec_shape] + _chip_swap_shapes(swap),
        scratch_shapes=[pltpu.VMEM((T, LANES), F32)] + wide + (_chip_swap_sems(ns) if ns else []),
        compiler_params=pltpu.CompilerParams(vmem_limit_bytes=VMEM_LIMIT_MAX, dimension_semantics=("arbitrary", "arbitrary")),
    )(proj_dn, proj_ab, conv_w, a_log, dt_bias, gain, dmix, *kept, *swap)
    return outs[:6], outs[6:]


SBQ = 256


def _group_rms(x, gain):
    first = _iota(x.shape, 1) < 64
    sq = x * x
    ss_a = jnp.sum(jnp.where(first, sq, 0.0), axis=-1, keepdims=True)
    ss_b = jnp.sum(jnp.where(first, 0.0, sq), axis=-1, keepdims=True)
    ms = jnp.where(first, ss_a, ss_b) * (1.0 / 64)
    return x * lax.rsqrt(ms + EPS) * gain


def _sb_stack(q):
    first = _iota((1, LANES), 1) < 64
    return jnp.concatenate([jnp.where(first, q, 0.0), jnp.where(first, 0.0, q)], axis=0)


def _sb_fold(acc):
    return jnp.where(_iota((1, LANES), 1) < 64, acc[:SBQ], acc[SBQ:])


def _sb_logs(q2, k, diag):
    n = SBQ
    z = _mm(q2, k, ((1,), (1,))) * SB_SCALE
    ls_pos = jnp.minimum(z, 0.0) - jnp.log(1.0 + jnp.exp(-jnp.abs(z)))
    l1m = ls_pos - z
    if not diag:
        return ls_pos, l1m, None
    mask = _iota((2 * n, n), 1) < jnp.bitwise_and(_iota((2 * n, n), 0), n - 1)
    return ls_pos, jnp.where(mask, l1m, 0.0), mask


def _sb_weights(ls_pos, l1m, mask, carry):
    w = jnp.exp(ls_pos + (_mm_ones(_tri_ones(SBQ, False), l1m, False) + carry))
    return w if mask is None else jnp.where(mask, w, 0.0)


def _sb_block(q, k, v, carry, diag):
    ls_pos, l1m, mask = _sb_logs(_sb_stack(q), k, diag)
    w = _sb_weights(ls_pos, l1m, mask, carry)
    return _mm(w, v, ((1,), (0,))), carry + jnp.sum(l1m, axis=-1, keepdims=True), _sb_sum_as_rows(l1m)


SB_ROWS = 16


def _sb_sum_as_rows(l1m):
    ones = jnp.ones((SB_ROWS, SBQ), BF16)
    hi, lo = _split(l1m)
    mm = lambda t: lax.dot_general(ones, t, (NT, ((), ())), preferred_element_type=F32)
    return mm(hi) + mm(lo)


def _sb_rows_as_column(rows):
    pick = jnp.where(_iota((SB_ROWS, SBQ), 0) == 0, 1.0, 0.0).astype(BF16)
    hi = rows.astype(BF16)
    rest = rows - hi.astype(F32)
    mid = rest.astype(BF16)
    lo = (rest - mid.astype(F32)).astype(BF16)
    mm = lambda t: lax.dot_general(t, pick, (TN, ((), ())), preferred_element_type=F32)
    return mm(hi) + (mm(mid) + mm(lo))


def _sb_block_bwd(q, k, v, carry, diag, dpv, dcarry):
    q2 = _sb_stack(q)
    ls_pos, l1m, mask = _sb_logs(q2, k, diag)
    w = _sb_weights(ls_pos, l1m, mask, carry)
    dv = _mm(w, dpv, ((0,), (0,)))
    de = _mm(dpv, v, ((1,), (1,))) * w
    dl1m = jnp.dot(de.astype(BF16), _tri_ones(SBQ, True), preferred_element_type=F32) + dcarry
    if mask is not None:
        dl1m = jnp.where(mask, dl1m, 0.0)
    sig = jnp.exp(ls_pos)
    dz = (de * (1.0 - sig) - dl1m * sig) * SB_SCALE
    dq = _sb_fold(_mm(dz, k, ((1,), (0,))))
    return dq, _mm(dz, q2, ((0,), (0,))), dv, dcarry + jnp.sum(de, axis=-1, keepdims=True)


_SB_Q, _SB_K, _SB_V = (slice(i * LANES, (i + 1) * LANES) for i in range(3))


def _sb_fwd_call(proj_sb, mix, q_gain, k_gain, B, T, gather=()):
    nblk = T // SBQ
    ng = len(gather)
    nsteps = 2 * B

    def body(*refs):
        x_ref, qg_ref, kg_ref = refs[:3]
        out_ref, carry_ref = refs[4 + ng:6 + ng]
        q_s, k_s = refs[6 + 2 * ng:8 + 2 * ng]
        step_id = 2 * pl.program_id(0) + pl.program_id(1)
        if ng:
            send, forward, finish = _gather_phases(refs[4:4 + ng], refs[6 + ng:6 + 2 * ng], *refs[8 + 2 * ng:])
            pl.when(step_id == 0)(send)
            pl.when(step_id == nsteps - 1)(forward)
        q_s[...] = _group_rms(x_ref[:, _SB_Q], qg_ref[...])
        k_s[...] = _group_rms(x_ref[:, _SB_K], kg_ref[...])

        def qblock(i, _):
            ri = pl.ds(pl.multiple_of(i * SBQ, SBQ), SBQ)
            q = q_s[ri, :]

            def kblock(jj, c):
                j = i - 1 - jj
                rj = pl.ds(pl.multiple_of(j * SBQ, SBQ), SBQ)
                carry_ref[0, 0, i, j] = c[2]
                pv, carry, rows = _sb_block(q, k_s[rj, :], x_ref[rj, _SB_V], c[1], False)
                return c[0] + pv, carry, c[2] + rows

            on_diag = _sb_block(q, k_s[ri, :], x_ref[ri, _SB_V], jnp.zeros((2 * SBQ, 1), F32), True)
            acc, _c, _r = lax.fori_loop(0, i, kblock, on_diag)
            out_ref[ri, :] = _sb_fold(acc)
            return 0

        lax.fori_loop(0, nblk, qblock, 0)
        if ng:
            pl.when(step_id == nsteps - 1)(finish)

    vec = pl.BlockSpec((1, LANES), lambda b, p: (0, 0))
    outs = pl.pallas_call(
        body, name="sb_fwd", grid=(B, 2),
        in_specs=[pl.BlockSpec((T, 3 * LANES), lambda b, p: (b, p)), vec, vec, pl.BlockSpec(memory_space=pl.ANY)] + _any_specs(ng),
        out_specs=[pl.BlockSpec((T, LANES), lambda b, p: (b, DN_W // LANES + p)), _sb_carry_spec(nblk)] + _any_specs(ng),
        out_shape=[jax.ShapeDtypeStruct((B * T, D), F32), jax.ShapeDtypeStruct((B, 2, nblk, nblk, SB_ROWS, 2 * SBQ), F32)]
        + _gather_shapes(gather), input_output_aliases={3: 0},
        scratch_shapes=[pltpu.VMEM((T, LANES), F32)] * 2 + (_gather_sems(ng) if ng else []),
        compiler_params=_params(dimension_semantics=("arbitrary", "arbitrary")),
    )(proj_sb, q_gain, k_gain, mix, *gather)
    return outs[0], outs[1], outs[2:]


def _sb_carry_spec(nblk):
    return pl.BlockSpec((1, 1, nblk, nblk, SB_ROWS, 2 * SBQ), lambda b, p: (b, p, 0, 0, 0, 0))


def _sb_bwd_call(proj_sb, dmix, carries, q_gain, k_gain, B, T):
    nblk = T // SBQ

    def body(x_ref, qg_ref, kg_ref, do_ref, carry_ref, dx_ref, dqg_ref, dkg_ref, q_s, k_s, dq_s, dk_s, dv_s):
        b_i, p = pl.program_id(0), pl.program_id(1)
        qn, q_vjp = jax.vjp(_group_rms, x_ref[:, _SB_Q], qg_ref[...])
        kn, k_vjp = jax.vjp(_group_rms, x_ref[:, _SB_K], kg_ref[...])
        q_s[...], k_s[...] = qn, kn
        dk_s[...] = jnp.zeros_like(dk_s)
        dv_s[...] = jnp.zeros_like(dv_s)

        def qblock(i, _):
            ri = pl.ds(pl.multiple_of(i * SBQ, SBQ), SBQ)
            q = q_s[ri, :]
            dacc = _sb_stack(do_ref[ri, :])

            def kblock(j, c):
                rj = pl.ds(pl.multiple_of(j * SBQ, SBQ), SBQ)
                carry = _sb_rows_as_column(carry_ref[0, 0, i, j])
                dq_j, dk_j, dv_j, dc = _sb_block_bwd(q, k_s[rj, :], x_ref[rj, _SB_V], carry, False, dacc, c[1])
                dk_s[rj, :] += dk_j
                dv_s[rj, :] += dv_j
                return c[0] + dq_j, dc

            dq, dc = lax.fori_loop(0, i, kblock, (jnp.zeros((SBQ, LANES), F32), jnp.zeros((2 * SBQ, 1), F32)))
            dq_i, dk_i, dv_i, _dc = _sb_block_bwd(q, k_s[ri, :], x_ref[ri, _SB_V], jnp.zeros((2 * SBQ, 1), F32), True, dacc, dc)
            dk_s[ri, :] += dk_i
            dv_s[ri, :] += dv_i
            dq_s[ri, :] = dq + dq_i
            return 0

        lax.fori_loop(0, nblk, qblock, 0)
        dq_in, dqg = q_vjp(dq_s[...])
        dk_in, dkg = k_vjp(dk_s[...])
        dx_ref[:, _SB_Q], dx_ref[:, _SB_K], dx_ref[:, _SB_V] = dq_in, dk_in, dv_s[...]

        @pl.when(jnp.logical_and(b_i == 0, p == 0))
        def _():
            dqg_ref[...] = jnp.zeros_like(dqg_ref)
            dkg_ref[...] = jnp.zeros_like(dkg_ref)

        dqg_ref[...] += dqg + pltpu.roll(dqg, 64, 1)
        dkg_ref[...] += dkg + pltpu.roll(dkg, 64, 1)

    M = B * T
    vec = pl.BlockSpec((1, LANES), lambda b, p: (0, 0))
    blk = pl.BlockSpec((T, 3 * LANES), lambda b, p: (b, p))
    big = [pltpu.VMEM((T, LANES), F32)]
    return pl.pallas_call(
        body, name="sb_bwd", grid=(B, 2),
        in_specs=[blk, vec, vec, pl.BlockSpec((T, LANES), lambda b, p: (b, DN_W // LANES + p)), _sb_carry_spec(nblk)],
        out_specs=[blk, vec, vec],
        out_shape=[jax.ShapeDtypeStruct((M, 3 * SB_W), F32)] + [jax.ShapeDtypeStruct((1, LANES), F32)] * 2,
        scratch_shapes=big * 5,
        compiler_params=_params(dimension_semantics=("arbitrary", "arbitrary")),
    )(proj_sb, q_gain, k_gain, dmix, carries)


def _sg_chunk(u, v, gain, w_a, w_b, bias):
    n = 128
    row, col = _iota((n, n), 0), _iota((n, n), 1)
    first = _iota((1, LANES), 1) < 64
    vn = _group_rms(_gelu(v), gain)
    tril = col <= row
    mixed = jnp.where(first, _dot(jnp.where(tril, w_a, 0.0), vn), _dot(jnp.where(tril, w_b, 0.0), vn)) + bias
    return _gelu(u) * mixed


_SG_U, _SG_V = slice(0, LANES), slice(LANES, 2 * LANES)


def _sg_fwd_call(proj_sg, mix, gain, sg_w, bias, B, T):
    nchunk = T // 128

    def body(x_ref, g_ref, wa_ref, wb_ref, bias_ref, mix_ref, out_ref):
        del mix_ref

        def step(i, _):
            r = pl.ds(pl.multiple_of(i * 128, 128), 128)
            out_ref[r, :] = _sg_chunk(x_ref[r, _SG_U], x_ref[r, _SG_V], g_ref[...], wa_ref[0], wb_ref[0], bias_ref[...])
            return 0

        lax.fori_loop(0, nchunk, step, 0)

    return pl.pallas_call(
        body, name="sg_fwd", grid=(B, 2),
        in_specs=[pl.BlockSpec((T, 2 * LANES), lambda b, p: (b, p)), pl.BlockSpec((1, LANES), lambda b, p: (0, p)),
                  pl.BlockSpec((1, 128, 128), lambda b, p: (2 * p, 0, 0)), pl.BlockSpec((1, 128, 128), lambda b, p: (2 * p + 1, 0, 0)),
                  pl.BlockSpec((128, LANES), lambda b, p: (0, p)), pl.BlockSpec(memory_space=pl.ANY)],
        out_specs=pl.BlockSpec((T, LANES), lambda b, p: (b, (DN_W + SB_W) // LANES + p)),
        out_shape=jax.ShapeDtypeStruct((B * T, D), F32), input_output_aliases={5: 0},
        compiler_params=_params(dimension_semantics=("arbitrary", "arbitrary")),
    )(proj_sg, gain, sg_w, sg_w, bias, mix)


def _sg_bwd_call(proj_sg, dmix, gain, sg_w, bias, B, T):
    nchunk = T // 128

    def body(x_ref, g_ref, wa_ref, wb_ref, bias_ref, do_ref, dx_ref, dg_ref, dw_ref, db_ref):
        p, b_i = pl.program_id(0), pl.program_id(1)

        def step(i, c):
            r = pl.ds(pl.multiple_of(i * 128, 128), 128)
            _, vjp = jax.vjp(_sg_chunk, x_ref[r, _SG_U], x_ref[r, _SG_V], g_ref[...], wa_ref[0], wb_ref[0], bias_ref[...])
            du, dv, dg, dwa, dwb, dbias = vjp(do_ref[r, :])
            dx_ref[r, _SG_U], dx_ref[r, _SG_V] = du, dv
            return c[0] + dg, c[1] + dwa, c[2] + dwb, c[3] + dbias

        z = jnp.zeros((128, 128), F32)
        dg, dwa, dwb, dbias = lax.fori_loop(0, nchunk, step, (jnp.zeros((1, LANES), F32), z, z, z))
        lane = _iota((1, LANES), 1)
        first = lane < 64
        s_a = jnp.sum(jnp.where(first, dbias, 0.0), axis=-1, keepdims=True)
        s_b = jnp.sum(jnp.where(first, 0.0, dbias), axis=-1, keepdims=True)
        dbg = jnp.where(lane == 2 * p, s_a, 0.0) + jnp.where(lane == 2 * p + 1, s_b, 0.0)

        @pl.when(b_i == 0)
        def _():
            dg_ref[...] = jnp.zeros_like(dg_ref)
            dw_ref[...] = jnp.zeros_like(dw_ref)

        @pl.when(jnp.logical_and(b_i == 0, p == 0))
        def _():
            db_ref[...] = jnp.zeros_like(db_ref)

        dg_ref[...] += dg
        dw_ref[0] += dwa
        dw_ref[1] += dwb
        db_ref[...] += dbg

    M = B * T
    blk = pl.BlockSpec((T, 2 * LANES), lambda p, b: (b, p))
    return pl.pallas_call(
        body, name="sg_bwd", grid=(2, B),
        in_specs=[blk, pl.BlockSpec((1, LANES), lambda p, b: (0, p)),
                  pl.BlockSpec((1, 128, 128), lambda p, b: (2 * p, 0, 0)), pl.BlockSpec((1, 128, 128), lambda p, b: (2 * p + 1, 0, 0)),
                  pl.BlockSpec((128, LANES), lambda p, b: (0, p)),
                  pl.BlockSpec((T, LANES), lambda p, b: (b, (DN_W + SB_W) // LANES + p))],
        out_specs=[blk, pl.BlockSpec((1, LANES), lambda p, b: (0, p)), pl.BlockSpec((2, 128, 128), lambda p, b: (p, 0, 0)),
                   pl.BlockSpec((128, LANES), lambda p, b: (0, 0))],
        out_shape=[jax.ShapeDtypeStruct((M, 2 * SG_W), F32), jax.ShapeDtypeStruct((1, SG_W), F32),
                   jax.ShapeDtypeStruct((4, 128, 128), F32), jax.ShapeDtypeStruct((128, LANES), F32)],
        compiler_params=_params(dimension_semantics=("arbitrary", "arbitrary")),
    )(proj_sg, gain, sg_w, sg_w, bias, dmix)


def _row_tile(m, most=512):
    return min(m, most)


def _in_proj_call(x, gain, wt):
    m = x.shape[0]
    tm = _row_tile(m)

    def body(x_ref, g_ref, wt_ref, *out_refs):
        h = _rms(x_ref[...], g_ref[...]).astype(BF16)
        out_refs[-1][...] = h
        for (off, width), out_ref in zip(SECTIONS, out_refs):
            out_ref[...] = lax.dot_general(h, wt_ref[off:off + width, :], (((1,), (1,)), ((), ())), preferred_element_type=F32)

    rows = lambda width: pl.BlockSpec((tm, width), lambda i: (i, 0))
    return pl.pallas_call(
        body, name="in_proj", grid=(m // tm,),
        in_specs=[rows(D), pl.BlockSpec((1, D), lambda i: (0, 0)),
                  pl.BlockSpec((NPACK, D), lambda i: (0, 0), pipeline_mode=pl.Buffered(1))],
        out_specs=[rows(w) for _, w in SECTIONS] + [rows(D)],
        out_shape=[jax.ShapeDtypeStruct((m, w), F32) for _, w in SECTIONS] + [jax.ShapeDtypeStruct((m, D), BF16)],
        compiler_params=_params(dimension_semantics=("arbitrary",)),
    )(x, gain, wt)


def _in_proj_bwd_call(dsections, wt, x, gain, dres, swap=()):
    m = x.shape[0]
    tm = _row_tile(m)
    nsec, ns = len(SECTIONS), len(swap)

    def body(*refs):
        ds_refs = refs[:nsec]
        wt_ref, x_ref, g_ref, dres_ref = refs[nsec:nsec + 4]
        dx_ref, dg_ref = refs[nsec + 4 + ns:nsec + 6 + ns]
        step = pl.program_id(0)
        if ns:
            send, finish = _chip_swap_phases(refs[nsec + 4:nsec + 4 + ns], refs[nsec + 6 + ns:nsec + 6 + 2 * ns],
                                             *refs[nsec + 6 + 2 * ns:])
            pl.when(step == 0)(send)

        @pl.when(step == 0)
        def _():
            dg_ref[...] = jnp.zeros_like(dg_ref)

        dh = 0.0
        for (off, width), ds_ref in zip(SECTIONS, ds_refs):
            dh = dh + jnp.dot(ds_ref[...].astype(BF16), wt_ref[off:off + width, :], preferred_element_type=F32)
        _, vjp = jax.vjp(_rms, x_ref[...], g_ref[...])
        dx, dg = vjp(dh)
        dx_ref[...] = dres_ref[...] + dx
        dg_ref[...] += dg
        if ns:
            pl.when(step == m // tm - 1)(finish)

    rows = lambda width: pl.BlockSpec((tm, width), lambda i: (i, 0))
    outs = pl.pallas_call(
        body, name="in_proj_bwd", grid=(m // tm,),
        in_specs=[rows(w) for _, w in SECTIONS] + [pl.BlockSpec((NPACK, D), lambda i: (0, 0), pipeline_mode=pl.Buffered(1)),
                                                   rows(D), pl.BlockSpec((1, D), lambda i: (0, 0)), rows(D)] + _any_specs(ns),
        out_specs=[rows(D), pl.BlockSpec((1, D), lambda i: (0, 0))] + _any_specs(ns),
        out_shape=[jax.ShapeDtypeStruct((m, D), F32), jax.ShapeDtypeStruct((1, D), F32)] + _chip_swap_shapes(swap),
        scratch_shapes=_chip_swap_sems(ns) if ns else [],
        compiler_params=_params(dimension_semantics=("arbitrary",)),
    )(*dsections, wt, x, gain, dres, *swap)
    return outs[:2], outs[2:]


def _in_proj_grad_call(dsections, h):
    m = h.shape[0]
    tm = min(m, 256)

    def body(*refs):
        ds_refs, (h_ref, out_ref) = refs[:len(SECTIONS)], refs[len(SECTIONS):]

        @pl.when(pl.program_id(0) == 0)
        def _():
            out_ref[...] = jnp.zeros_like(out_ref)

        for (off, width), ds_ref in zip(SECTIONS, ds_refs):
            out_ref[off:off + width, :] += lax.dot_general(ds_ref[...].astype(BF16), h_ref[...], (((0,), (0,)), ((), ())),
                                                           preferred_element_type=F32)

    rows = lambda width: pl.BlockSpec((tm, width), lambda i: (i, 0))
    return pl.pallas_call(
        body, name="grad_w_in", grid=(m // tm,),
        in_specs=[rows(w) for _, w in SECTIONS] + [rows(D)],
        out_specs=pl.BlockSpec((NPACK, D), lambda i: (0, 0), pipeline_mode=pl.Buffered(1)),
        out_shape=jax.ShapeDtypeStruct((NPACK, D), F32),
        compiler_params=_params(dimension_semantics=("arbitrary",)),
    )(*dsections, h)


def _packed_column_of():
    t = np.full(NPACK, -1, np.int64)
    lanes = np.arange(LANES)
    for pair in range(2):
        for s in range(4):
            t[DN_OFF + pair * 1024 + s * 256 + np.arange(256)] = s * DN_W + pair * 256 + np.arange(256)
        for s in range(3):
            t[SB_OFF + pair * 384 + s * LANES + lanes] = 2056 + s * SB_W + pair * LANES + lanes
        for s in range(2):
            t[SG_OFF + pair * 256 + s * LANES + lanes] = 2056 + 3 * SB_W + s * SG_W + pair * LANES + lanes
    t[AB_OFF + np.arange(2 * NH)] = 4 * DN_W + np.arange(2 * NH)
    return t


def _row_tables():
    col = _packed_column_of()
    fwd = np.where(col >= 0, (col // IN_SHARD) * IN_SHARD_PAD + col % IN_SHARD, -1)
    packed_of = np.full(IN_DIM, -1, np.int64)
    packed_of[col[col >= 0]] = np.nonzero(col >= 0)[0]
    r = np.arange(NDEV * IN_SHARD_PAD)
    inside = r % IN_SHARD_PAD < IN_SHARD
    back = np.where(inside, packed_of[np.minimum((r // IN_SHARD_PAD) * IN_SHARD + r % IN_SHARD_PAD, IN_DIM - 1)], -1)
    return fwd, back


def _row_perm_call(src, table, name):
    n_out = table.shape[0]
    touched = [sorted(set((table[b * 128:(b + 1) * 128][table[b * 128:(b + 1) * 128] >= 0] // 128).tolist()))
               for b in range(n_out // 128)]

    def body(tbl_ref, src_ref, out_ref):
        lane = _iota((1, LANES), 1)
        for b, blocks in enumerate(touched):
            want = tbl_ref[b * 128:(b + 1) * 128, :]
            acc = jnp.zeros((128, D), F32)
            for sb in blocks:
                pick = jnp.where(want == sb * 128 + lane, 1.0, 0.0).astype(BF16)
                acc = acc + jnp.dot(pick, src_ref[sb * 128:(sb + 1) * 128, :].astype(BF16), preferred_element_type=F32)
            out_ref[b * 128:(b + 1) * 128, :] = acc.astype(BF16)

    return pl.pallas_call(
        body, name=name, out_shape=jax.ShapeDtypeStruct((n_out, D), BF16),
        in_specs=[pl.BlockSpec(memory_space=pltpu.VMEM)] * 2, out_specs=pl.BlockSpec(memory_space=pltpu.VMEM),
        compiler_params=_params(),
    )(jnp.asarray(table.reshape(-1, 1), jnp.int32), src)


def _out_proj_call(a, w, res):
    m, k = a.shape
    n = w.shape[1]
    tm = _row_tile(m)

    def body(a_ref, w_ref, res_ref, out_ref):
        out_ref[...] = res_ref[...] + jnp.dot(a_ref[...].astype(BF16), w_ref[...], preferred_element_type=F32)

    return pl.pallas_call(
        body, name="out_proj", grid=(m // tm,),
        in_specs=[pl.BlockSpec((tm, k), lambda i: (i, 0)), pl.BlockSpec((k, n), lambda i: (0, 0)),
                  pl.BlockSpec((tm, n), lambda i: (i, 0))],
        out_specs=pl.BlockSpec((tm, n), lambda i: (i, 0)),
        out_shape=jax.ShapeDtypeStruct((m, n), F32),
        compiler_params=_params(dimension_semantics=("arbitrary",)),
    )(a, w, res)


def _ffn_specs(tm):
    return [pl.BlockSpec((1, D, FF_SHARD), lambda i, j: (j, 0, 0)), pl.BlockSpec((FF_SHARD, D), lambda i, j: (j, 0))]


def _ffn_fwd_call(x, gain, w1, w2):
    m = x.shape[0]
    tm = _row_tile(m, 1024)

    def body(x_ref, g_ref, w1_ref, w2_ref, out_ref, h_s, acc_s):
        j = pl.program_id(1)

        @pl.when(j == 0)
        def _():
            h_s[...] = _rms(x_ref[...], g_ref[...]).astype(BF16)
            acc_s[...] = jnp.zeros_like(acc_s)

        a = jnp.maximum(jnp.dot(h_s[...], w1_ref[0], preferred_element_type=F32), 0.0)
        acc_s[...] += jnp.dot((a * a).astype(BF16), w2_ref[...], preferred_element_type=F32)

        @pl.when(j == NDEV - 1)
        def _():
            out_ref[...] = x_ref[...] + acc_s[...]

    return pl.pallas_call(
        body, name="ffn_fwd", grid=(m // tm, NDEV),
        in_specs=[pl.BlockSpec((tm, D), lambda i, j: (i, 0)), pl.BlockSpec((1, D), lambda i, j: (0, 0))] + _ffn_specs(tm),
        out_specs=pl.BlockSpec((tm, D), lambda i, j: (i, 0)),
        out_shape=jax.ShapeDtypeStruct((m, D), F32),
        scratch_shapes=[pltpu.VMEM((tm, D), BF16), pltpu.VMEM((tm, D), F32)],
        compiler_params=_params(dimension_semantics=("arbitrary", "arbitrary")),
    )(x, gain, w1, w2)


def _ffn_bwd_call(x, dy, gain, w1, w2, swap=()):
    m = x.shape[0]
    tm = _row_tile(m, 1024)
    ns = len(swap)

    def body(*refs):
        x_ref, dy_ref, g_ref, w1_ref, w2_ref = refs[:5]
        dx_ref, da_ref, r_ref, h_ref, dg_ref = refs[5 + ns:10 + ns]
        acc_s = refs[10 + 2 * ns]
        i, j = pl.program_id(0), pl.program_id(1)
        if ns:
            send, finish = _sibling_swap_phases(refs[5:5 + ns], refs[10 + ns:10 + 2 * ns], *refs[11 + 2 * ns:])
            pl.when(jnp.logical_and(i == 0, j == 0))(send)

        @pl.when(j == 0)
        def _():
            h_ref[...] = _rms(x_ref[...], g_ref[...]).astype(BF16)
            acc_s[...] = jnp.zeros_like(acc_s)

        @pl.when(jnp.logical_and(i == 0, j == 0))
        def _():
            dg_ref[...] = jnp.zeros_like(dg_ref)

        a = jnp.maximum(jnp.dot(h_ref[...], w1_ref[0], preferred_element_type=F32), 0.0)
        r_ref[...] = (a * a).astype(BF16)
        dr = lax.dot_general(dy_ref[...].astype(BF16), w2_ref[...], (((1,), (1,)), ((), ())), preferred_element_type=F32)
        da = (dr * (2.0 * a)).astype(BF16)
        da_ref[...] = da
        acc_s[...] += lax.dot_general(da, w1_ref[0], (((1,), (1,)), ((), ())), preferred_element_type=F32)

        @pl.when(j == NDEV - 1)
        def _():
            _, vjp = jax.vjp(_rms, x_ref[...], g_ref[...])
            dx, dg = vjp(acc_s[...])
            dx_ref[...] = dy_ref[...] + dx
            dg_ref[...] += dg

        if ns:
            pl.when(jnp.logical_and(i == m // tm - 1, j == NDEV - 1))(finish)

    outs = pl.pallas_call(
        body, name="ffn_bwd", grid=(m // tm, NDEV),
        in_specs=[pl.BlockSpec((tm, D), lambda i, j: (i, 0)), pl.BlockSpec((tm, D), lambda i, j: (i, 0)),
                  pl.BlockSpec((1, D), lambda i, j: (0, 0))] + _ffn_specs(tm) + _any_specs(ns),
        out_specs=[pl.BlockSpec((tm, D), lambda i, j: (i, 0)), pl.BlockSpec((tm, FF_SHARD), lambda i, j: (i, j)),
                   pl.BlockSpec((tm, FF_SHARD), lambda i, j: (i, j)), pl.BlockSpec((tm, D), lambda i, j: (i, 0)),
                   pl.BlockSpec((1, D), lambda i, j: (0, 0))] + _any_specs(ns),
        out_shape=[jax.ShapeDtypeStruct((m, D), F32), jax.ShapeDtypeStruct((m, DFF), BF16), jax.ShapeDtypeStruct((m, DFF), BF16),
                   jax.ShapeDtypeStruct((m, D), BF16), jax.ShapeDtypeStruct((1, D), F32)] + _sibling_swap_shapes(swap),
        scratch_shapes=[pltpu.VMEM((tm, D), F32)] + (_sibling_swap_sems(ns) if ns else []),
        compiler_params=_params(dimension_semantics=("arbitrary", "arbitrary")),
    )(x, dy, gain, w1, w2, *swap)
    return outs[:5], outs[5:]


def _mm_nt_call(a, b, name):
    m, k = a.shape
    n = b.shape[0]
    tm = _row_tile(m)

    def body(a_ref, b_ref, out_ref):
        out_ref[...] = lax.dot_general(a_ref[...].astype(BF16), b_ref[...].astype(BF16), (((1,), (1,)), ((), ())),
                                       preferred_element_type=F32)

    return pl.pallas_call(
        body, name=name, grid=(m // tm,),
        in_specs=[pl.BlockSpec((tm, k), lambda i: (i, 0)), pl.BlockSpec((n, k), lambda i: (0, 0))],
        out_specs=pl.BlockSpec((tm, n), lambda i: (i, 0)),
        out_shape=jax.ShapeDtypeStruct((m, n), F32),
        compiler_params=_params(dimension_semantics=("arbitrary",)),
    )(a, b)


def _mm_tn_call(a, b, name, col_shards=False):
    m, k = a.shape
    n = b.shape[1]
    tm, tk = _row_tile(m, 1024), min(k, 1024)
    tn = n // NDEV if col_shards else min(n, 1024)

    def body(a_ref, b_ref, out_ref, acc_s):
        s = pl.program_id(2)

        @pl.when(s == 0)
        def _():
            acc_s[...] = jnp.zeros_like(acc_s)

        acc_s[...] += lax.dot_general(a_ref[...].astype(BF16), b_ref[...].astype(BF16), (((0,), (0,)), ((), ())),
                                      preferred_element_type=F32)

        @pl.when(s == m // tm - 1)
        def _():
            out_ref[...] = acc_s[...].astype(BF16).reshape(out_ref.shape)

    if col_shards:
        out_spec, out_shape = pl.BlockSpec((1, tk, tn), lambda i, j, s: (j, i, 0)), (NDEV, k, tn)
    else:
        out_spec, out_shape = pl.BlockSpec((tk, tn), lambda i, j, s: (i, j)), (k, n)
    return pl.pallas_call(
        body, name=name, grid=(k // tk, n // tn, m // tm),
        in_specs=[pl.BlockSpec((tm, tk), lambda i, j, s: (s, i)), pl.BlockSpec((tm, tn), lambda i, j, s: (s, j))],
        out_specs=out_spec, out_shape=jax.ShapeDtypeStruct(out_shape, BF16),
        scratch_shapes=[pltpu.VMEM((tk, tn), F32)],
        compiler_params=_params(dimension_semantics=("arbitrary", "arbitrary", "arbitrary")),
    )(a, b)


def _loss_call(y, target):
    m = y.shape[0]
    tm = _row_tile(m)

    def body(y_ref, t_ref, loss_ref, dy_ref):
        @pl.when(pl.program_id(0) == 0)
        def _():
            loss_ref[...] = jnp.zeros_like(loss_ref)

        err = y_ref[...] - t_ref[...]
        dy_ref[...] = err * (1.0 / D)
        per_row = jnp.mean(err * err, axis=-1, keepdims=True)
        loss_ref[...] += jnp.broadcast_to(0.5 * jnp.sum(per_row, axis=0, keepdims=True), (1, LANES))

    return pl.pallas_call(
        body, name="loss", grid=(m // tm,),
        in_specs=[pl.BlockSpec((tm, D), lambda i: (i, 0))] * 2,
        out_specs=[pl.BlockSpec((1, LANES), lambda i: (0, 0)), pl.BlockSpec((tm, D), lambda i: (i, 0))],
        out_shape=[jax.ShapeDtypeStruct((1, LANES), F32), jax.ShapeDtypeStruct((m, D), F32)],
        compiler_params=_params(dimension_semantics=("arbitrary",)),
    )(y, target)


def _adamw_call(w, g, m, v, name):
    shape = w.shape
    cols = shape[-1] if w.ndim > 1 else w.size
    rows = w.size // cols
    tr = rows if (rows <= 512 or rows % 512) else 512
    c1, c2 = 1.0 - ADAM_B1 ** ADAM_STEP, 1.0 - ADAM_B2 ** ADAM_STEP

    def body(w_ref, g_ref, m_ref, v_ref, d_ref, nm_ref, nv_ref):
        g_ = g_ref[...]
        nm = ADAM_B1 * m_ref[...] + (1.0 - ADAM_B1) * g_
        nv = ADAM_B2 * v_ref[...] + (1.0 - ADAM_B2) * (g_ * g_)
        d_ref[...] = -ADAM_LR * ((nm / c1) / (jnp.sqrt(nv / c2) + ADAM_EPS) + ADAM_WD * w_ref[...])
        nm_ref[...], nv_ref[...] = nm, nv

    spec = pl.BlockSpec((tr, cols), lambda i: (i, 0))
    outs = pl.pallas_call(
        body, name=name, grid=(rows // tr,), in_specs=[spec] * 4, out_specs=[spec] * 3,
        out_shape=[jax.ShapeDtypeStruct((rows, cols), F32)] * 3,
        compiler_params=_params(dimension_semantics=("arbitrary",)),
    )(*(t.reshape(rows, cols) for t in (w, g, m, v)))
    return tuple(o.reshape(shape) for o in outs)


def _sum_tile(rows):
    for cand in (2048, 1024, 512, 256, 128):
        if rows > cand and rows % cand == 0:
            return cand
    return rows


def _pair_sum_call(gs, gots, core, name):
    n = len(gs)

    def body(core_ref, *refs):
        del core_ref
        for g_ref, got_ref, out_ref in zip(refs[:n], refs[n:2 * n], refs[2 * n:]):
            out_ref[...] = (g_ref[...].astype(F32) + got_ref[...].astype(F32)).astype(BF16)

    block = lambda g: (1,) + g.shape[1:]
    grid_spec = pltpu.PrefetchScalarGridSpec(
        num_scalar_prefetch=1, grid=(4,),
        in_specs=[pl.BlockSpec(block(g), lambda ch, core_ref: (2 * ch + core_ref[0], 0, 0)) for g in gs]
        + [pl.BlockSpec(block(g), lambda ch, core_ref: (ch, 0, 0)) for g in gs],
        out_specs=[pl.BlockSpec(block(g), lambda ch, core_ref: (ch, 0, 0)) for g in gs])
    return pl.pallas_call(
        body, name=name, grid_spec=grid_spec, out_shape=[jax.ShapeDtypeStruct((4,) + g.shape[1:], BF16) for g in gs],
        compiler_params=_params(dimension_semantics=("arbitrary",)),
    )(jnp.asarray(core, jnp.int32).reshape(1), *gs, *gots)


def _total_sum_call(gs, gots, froms, me, my_chip, name):
    n = len(gs)

    def body(idx_ref, *refs):
        del idx_ref
        for a in range(n):
            g_ref, got_ref, f0, f1, f2 = (refs[k * n + a] for k in range(5))
            acc = g_ref[0].astype(F32) + got_ref[0].astype(F32)
            for f in (f0, f1, f2):
                acc = acc + f[0].astype(F32)
            refs[5 * n + a][...] = acc

    block = lambda g: (1,) + g.shape[1:]
    picked = lambda which: [pl.BlockSpec(block(g), lambda t, idx, which=which: (idx[which], 0, 0)) for g in gs]
    fixed = lambda j: [pl.BlockSpec(block(g), lambda t, idx, j=j: (j, 0, 0)) for g in gs]
    grid_spec = pltpu.PrefetchScalarGridSpec(
        num_scalar_prefetch=1, grid=(1,),
        in_specs=picked(0) + picked(1) + fixed(0) + fixed(1) + fixed(2),
        out_specs=[pl.BlockSpec(g.shape[1:], lambda t, idx: (0, 0)) for g in gs])
    return pl.pallas_call(
        body, name=name, grid_spec=grid_spec, out_shape=[jax.ShapeDtypeStruct(g.shape[1:], F32) for g in gs],
        compiler_params=_params(dimension_semantics=("arbitrary",)),
    )(jnp.stack([jnp.asarray(me, jnp.int32), jnp.asarray(my_chip, jnp.int32)]), *gs, *gots, *froms, *froms, *froms)


def _sum_call(parts, out_dtype, name):
    rows, cols = parts[0][0].shape[1:]
    tr = _sum_tile(rows)
    index = jnp.stack([jnp.asarray(i, jnp.int32) for _, i in parts])

    def body(idx_ref, *refs):
        del idx_ref
        acc = refs[0][0].astype(F32)
        for r in refs[1:-1]:
            acc = acc + r[0].astype(F32)
        refs[-1][...] = acc.astype(out_dtype)

    grid_spec = pltpu.PrefetchScalarGridSpec(
        num_scalar_prefetch=1, grid=(rows // tr,),
        in_specs=[pl.BlockSpec((1, tr, cols), lambda t, idx, n=n: (idx[n], t, 0)) for n in range(len(parts))],
        out_specs=pl.BlockSpec((tr, cols), lambda t, idx: (t, 0)))
    return pl.pallas_call(
        body, name=name, grid_spec=grid_spec, out_shape=jax.ShapeDtypeStruct((rows, cols), out_dtype),
        compiler_params=_params(dimension_semantics=("arbitrary",)),
    )(index, *(a for a, _ in parts))


def _place():
    return lax.axis_index("x"), lax.axis_index("y"), lax.axis_index("c")


def _any_specs(n):
    return [pl.BlockSpec(memory_space=pl.ANY)] * n


def _all_gather_call(xs, name):
    n = len(xs)

    def body(*refs):
        for phase in _gather_phases(refs[:n], refs[n:2 * n], *refs[2 * n:]):
            phase()

    return pl.pallas_call(
        body, name=name, in_specs=_any_specs(n), out_specs=_any_specs(n),
        out_shape=_gather_shapes(xs), scratch_shapes=_gather_sems(n),
    )(*xs)


def _gather_shapes(xs):
    return [jax.ShapeDtypeStruct((NDEV,) + x.shape, x.dtype) for x in xs]


def _gather_sems(n):
    return [pltpu.SemaphoreType.DMA((7 * n,)), pltpu.SemaphoreType.DMA((7 * n,)), pltpu.SemaphoreType.DMA((n,))]


def _gather_phases(x_refs, out_refs, send_sems, recv_sems, local_sems):
    n = len(x_refs)
    ax, ay, ac = _place()
    me, sibling = (ax, ay, ac), (ax, ay, 1 - ac)
    chips = [(1 - ax, ay), (ax, 1 - ay), (1 - ax, 1 - ay)]

    def copy(a, k, block, to, src=None):
        slot = out_refs[a].at[4 * block[0] + 2 * block[1] + block[2]]
        return pltpu.make_async_remote_copy(
            src_ref=slot if src is None else src, dst_ref=slot,
            send_sem=send_sems.at[7 * a + k], recv_sem=recv_sems.at[7 * a + k], device_id=to, device_id_type=MESH)

    local = [pltpu.make_async_copy(x_refs[a], out_refs[a].at[4 * ax + 2 * ay + ac], local_sems.at[a]) for a in range(n)]
    first = []
    for a in range(n):
        first.append(copy(a, 0, me, sibling, src=x_refs[a]))
        first += [copy(a, 1 + j, me, (*chip, ac), src=x_refs[a]) for j, chip in enumerate(chips)]
    passed = [copy(a, 4 + j, (*chip, ac), sibling) for j, chip in enumerate(chips) for a in range(n)]

    def send():
        for cp in local + first:
            cp.start()

    def forward():
        for j, chip in enumerate(chips):
            for a in range(n):
                copy(a, 1 + j, (*chip, ac), me).wait_recv()
                passed[j * n + a].start()

    def finish():
        for a in range(n):
            copy(a, 0, sibling, me).wait_recv()
            for j, chip in enumerate(chips):
                copy(a, 4 + j, (*chip, 1 - ac), me).wait_recv()
        for cp in first + passed:
            cp.wait_send()
        for cp in local:
            cp.wait()

    return send, forward, finish


def _swap_sibling_call(xs, name):
    n = len(xs)

    def body(*refs):
        for phase in _sibling_swap_phases(refs[:n], refs[n:2 * n], *refs[2 * n:]):
            phase()

    return pl.pallas_call(
        body, name=name, in_specs=_any_specs(n), out_specs=_any_specs(n),
        out_shape=_sibling_swap_shapes(xs), scratch_shapes=_sibling_swap_sems(n),
    )(*xs)


def _sibling_swap_shapes(xs):
    return [jax.ShapeDtypeStruct((4,) + x.shape[1:], x.dtype) for x in xs]


def _sibling_swap_sems(n):
    return [pltpu.SemaphoreType.DMA((n,)), pltpu.SemaphoreType.DMA((n,))]


def _sibling_swap_phases(x_refs, out_refs, send_sems, recv_sems):
    ax, ay, ac = _place()
    sibling = (ax, ay, 1 - ac)

    def send():
        for a, (x_ref, out_ref) in enumerate(zip(x_refs, out_refs)):
            for chip in range(4):
                pltpu.make_async_remote_copy(src_ref=x_ref.at[2 * chip + 1 - ac], dst_ref=out_ref.at[chip],
                                             send_sem=send_sems.at[a], recv_sem=recv_sems.at[a],
                                             device_id=sibling, device_id_type=MESH).start()

    def finish():
        for a, (x_ref, out_ref) in enumerate(zip(x_refs, out_refs)):
            pltpu.make_async_remote_copy(src_ref=x_ref.at[pl.ds(0, 4)], dst_ref=out_ref, send_sem=send_sems.at[a],
                                         recv_sem=recv_sems.at[a], device_id=sibling, device_id_type=MESH).wait()

    return send, finish


def _chip_swap_shapes(xs):
    return [jax.ShapeDtypeStruct((3,) + x.shape[1:], x.dtype) for x in xs]


def _chip_swap_sems(n):
    return [pltpu.SemaphoreType.DMA((3 * n,)), pltpu.SemaphoreType.DMA((3 * n,))]


def _chip_swap_phases(x_refs, out_refs, send_sems, recv_sems):
    ax, ay, ac = _place()
    chips = [(1 - ax, ay), (ax, 1 - ay), (1 - ax, 1 - ay)]
    copies = [pltpu.make_async_remote_copy(src_ref=x_refs[a].at[2 * cx + cy], dst_ref=out_refs[a].at[j],
                                           send_sem=send_sems.at[3 * a + j], recv_sem=recv_sems.at[3 * a + j],
                                           device_id=(cx, cy, ac), device_id_type=MESH)
              for a in range(len(x_refs)) for j, (cx, cy) in enumerate(chips)]

    def send():
        for cp in copies:
            cp.start()

    def finish():
        for cp in copies:
            cp.wait()

    return send, finish


def _reduce_begin(gs, name):
    got = _swap_sibling_call(gs, name + "_d2d")
    return got, _pair_sum_call(gs, got, lax.axis_index("c"), name + "_pair")


def _reduce_end(gs, got, from_chips, name):
    ax, ay, ac = _place()
    return _total_sum_call(gs, got, from_chips, 4 * ax + 2 * ay + ac, 2 * ax + ay, name + "_total")


SMALL = ("norm1_g", "conv_w", "a_log", "dt_bias", "dn_out_g", "sb_q_g", "sb_k_g", "sg_v_g", "sg_w", "sg_b", "norm2_g")
WEIGHTS = ("norm1_g", "w_in", "conv_w", "a_log", "dt_bias", "dn_out_g", "sb_q_g", "sb_k_g", "sg_v_g", "sg_w", "sg_b",
           "w_out", "norm2_g", "w_ff1", "w_ff2")
SMALL_SHAPE = {"norm1_g": (D,), "conv_w": (4, 3 * DN_W), "a_log": (NH,), "dt_bias": (NH,), "dn_out_g": (128,), "sb_q_g": (64,),
               "sb_k_g": (64,), "sg_v_g": (SG_W,), "sg_w": (NH, 128, 128), "sg_b": (NH, 128), "norm2_g": (D,)}


def _size(shape):
    n = 1
    for s in shape:
        n *= s
    return n


def _to_rows(flat, multiple):
    pad = (-flat.shape[0]) % (LANES * multiple)
    return jnp.pad(flat, (0, pad)).reshape(-1, LANES)


def _conv_by_pair(conv):
    return conv.reshape(4, 3, 2, 256).transpose(0, 2, 1, 3).reshape(4, 3 * DN_W)


def kernel(x, norm1_g, w_in, conv_w, a_log, dt_bias, dn_out_g, sb_q_g, sb_k_g, sg_v_g, sg_w, sg_b, w_out, norm2_g, w_ff1, w_ff2, loss_target, m_norm1_g, m_w_in, m_conv_w, m_a_log, m_dt_bias, m_dn_out_g, m_sb_q_g, m_sb_k_g, m_sg_v_g, m_sg_w, m_sg_b, m_w_out, m_norm2_g, m_w_ff1, m_w_ff2, v_norm1_g, v_w_in, v_conv_w, v_a_log, v_dt_bias, v_dn_out_g, v_sb_q_g, v_sb_k_g, v_sg_v_g, v_sg_w, v_sg_b, v_w_out, v_norm2_g, v_w_ff1, v_w_ff2):
    given = dict(norm1_g=norm1_g, w_in=w_in, conv_w=conv_w, a_log=a_log, dt_bias=dt_bias, dn_out_g=dn_out_g, sb_q_g=sb_q_g,
                 sb_k_g=sb_k_g, sg_v_g=sg_v_g, sg_w=sg_w, sg_b=sg_b, w_out=w_out, norm2_g=norm2_g, w_ff1=w_ff1, w_ff2=w_ff2)
    mom = dict(norm1_g=m_norm1_g, w_in=m_w_in, conv_w=m_conv_w, a_log=m_a_log, dt_bias=m_dt_bias, dn_out_g=m_dn_out_g,
               sb_q_g=m_sb_q_g, sb_k_g=m_sb_k_g, sg_v_g=m_sg_v_g, sg_w=m_sg_w, sg_b=m_sg_b, w_out=m_w_out, norm2_g=m_norm2_g,
               w_ff1=m_w_ff1, w_ff2=m_w_ff2)
    var = dict(norm1_g=v_norm1_g, w_in=v_w_in, conv_w=v_conv_w, a_log=v_a_log, dt_bias=v_dt_bias, dn_out_g=v_dn_out_g,
               sb_q_g=v_sb_q_g, sb_k_g=v_sb_k_g, sg_v_g=v_sg_v_g, sg_w=v_sg_w, sg_b=v_sg_b, w_out=v_w_out, norm2_g=v_norm2_g,
               w_ff1=v_w_ff1, w_ff2=v_w_ff2)
    B, T, _ = x.shape
    M = B * T
    ax, ay, ac = _place()
    me = 4 * ax + 2 * ay + ac
    table_fwd, table_back = _row_tables()

    send = []
    for l in range(2):
        w_in_t = jnp.pad(w_in[l].T, ((0, IN_SHARD_PAD - IN_SHARD), (0, 0)))
        send.append([w_in_t.astype(BF16), w_out[l].astype(BF16), w_ff1[l].astype(BF16), w_ff2[l].astype(BF16)])
    first_in, conv_rows = _all_gather_call([send[0][0], _to_rows(conv_w.reshape(-1), 8)], "gather_first")
    conv_full = conv_rows.reshape(NDEV, -1)[:, :conv_w.size].reshape(NDEV, 2, 4, -1).transpose(1, 2, 0, 3).reshape(2, 4, 3 * DN_W)
    gathered = [[first_in, None, None, None], [None] * 4]

    pad_vec = lambda v: jnp.zeros((1, LANES), F32).at[0, :v.shape[0]].set(v)
    layer = []
    for l in range(2):
        layer.append(dict(
            g1=norm1_g[l].reshape(1, D), g2=norm2_g[l].reshape(1, D), conv=_conv_by_pair(conv_full[l]),
            a_log=pad_vec(a_log[l]), dt_bias=pad_vec(dt_bias[l]), dn_g=dn_out_g[l].reshape(1, LANES),
            sb_qg=jnp.tile(sb_q_g[l], 2).reshape(1, LANES), sb_kg=jnp.tile(sb_k_g[l], 2).reshape(1, LANES),
            sg_g=sg_v_g[l].reshape(1, SG_W), sg_w=sg_w[l], sg_bias=jnp.repeat(sg_b[l].T, 64, axis=1)))

    cur = x.reshape(M, D)
    saved = []
    for l, p in enumerate(layer):
        p["wt"] = _row_perm_call(gathered[l][0].reshape(NDEV * IN_SHARD_PAD, D), table_fwd, "pack_w_in")
        p_dn, p_sb, p_sg, p_ab, h = _in_proj_call(cur, p["g1"], p["wt"])
        mix, dn_kept, arrived = _dn_fwd_call(p_dn, p_ab, p["conv"], p["a_log"], p["dt_bias"], p["dn_g"], B, T,
                                             gather=send[0][1:] + send[1][:1] if l == 0 else [])
        if l == 0:
            gathered[0][1:], gathered[1][0] = list(arrived[:3]), arrived[3]
        p["w_out"], p["w1"], p["w2"] = gathered[l][1].reshape(D, D), gathered[l][2], gathered[l][3].reshape(DFF, D)
        mix, sb_carries, arrived = _sb_fwd_call(p_sb, mix, p["sb_qg"], p["sb_kg"], B, T, gather=send[1][1:] if l == 0 else [])
        if l == 0:
            gathered[1][1:] = list(arrived)
        mix = _sg_fwd_call(p_sg, mix, p["sg_g"], p["sg_w"], p["sg_bias"], B, T)
        x1 = _out_proj_call(mix, p["w_out"], cur)
        x2 = _ffn_fwd_call(x1, p["g2"], p["w1"], p["w2"])
        saved.append(dict(x0=cur, p_dn=p_dn, p_sb=p_sb, p_sg=p_sg, p_ab=p_ab, h=h, mix=mix, x1=x1, dn_kept=dn_kept, sb_carries=sb_carries))
        cur = x2
    loss_part, dy = _loss_call(cur, loss_target.reshape(M, D))
    loss = lax.psum(loss_part[0, 0], ("x", "y", "c"))

    big_grads = [[None] * 4, [None] * 4]
    small_grads = {n: [None, None] for n in SMALL}
    for l in (1, 0):
        p, s = layer[l], saved[l]
        (dx1, da, r, h2, dg2), got1 = _ffn_bwd_call(s["x1"], dy, p["g2"], p["w1"], p["w2"], swap=big_grads[1] if l == 0 else ())
        big_grads[l][2] = _mm_tn_call(h2, da, "grad_w_ff1", col_shards=True)
        big_grads[l][3] = _mm_tn_call(r, dy, "grad_w_ff2").reshape(NDEV, FF_SHARD, D)
        dmix = _mm_nt_call(dx1, p["w_out"], "dmix")
        big_grads[l][1] = _mm_tn_call(s["mix"], dx1, "grad_w_out").reshape(NDEV, D // NDEV, D)
        if l == 0:
            got0, sums0 = _reduce_begin(big_grads[0][1:], "reduce_early0")
            early_sums = list(_pair_sum_call(big_grads[1], got1, ac, "reduce_early1_pair")) + list(sums0)
        (d_dn, d_ab, dcw, dalog, ddtb, ddn_g), early_from = _dn_bwd_call(
            s["p_dn"], s["p_ab"], dmix, s["dn_kept"], p["conv"], p["a_log"], p["dt_bias"], p["dn_g"], B, T,
            swap=early_sums if l == 0 else ())
        d_sb, dqg, dkg = _sb_bwd_call(s["p_sb"], dmix, s["sb_carries"], p["sb_qg"], p["sb_kg"], B, T)
        d_sg, dsg_g, dsg_w, dsg_b = _sg_bwd_call(s["p_sg"], dmix, p["sg_g"], p["sg_w"], p["sg_bias"], B, T)
        dsections = (d_dn, d_sb, d_sg, d_ab)
        dwt = _in_proj_grad_call(dsections, s["h"])
        big_grads[l][0] = _row_perm_call(dwt, table_back, "unpack_grad_w_in").reshape(NDEV, IN_SHARD_PAD, D)
        if l == 0:
            last = big_grads[0][:1]
            last_got, last_sums = _reduce_begin(last, "reduce_last")
        (dy, dg1), last_from = _in_proj_bwd_call(dsections, p["wt"], s["x0"], p["g1"], dx1, swap=last_sums if l == 0 else ())
        for n, val in (("norm1_g", dg1[0]), ("conv_w", dcw.transpose(1, 0, 2).reshape(4, 3 * DN_W)), ("a_log", dalog[0, :NH]),
                       ("dt_bias", ddtb[0, :NH]), ("dn_out_g", ddn_g[0]), ("sb_q_g", dqg[0, :64]), ("sb_k_g", dkg[0, :64]),
                       ("sg_v_g", dsg_g[0]), ("sg_w", dsg_w), ("sg_b", dsg_b[:, :NH].T), ("norm2_g", dg2[0])):
            small_grads[n][l] = val
    grad_x = dy.reshape(B, T, D)

    mine0 = _reduce_end(last, last_got, last_from, "reduce_last")
    mine1 = (_reduce_end(big_grads[1], got1, early_from[:4], "reduce_early1")
             + _reduce_end(big_grads[0][1:], got0, early_from[4:], "reduce_early0"))
    grads = {"w_in": jnp.stack([mine0[0][:IN_SHARD].T, mine1[0][:IN_SHARD].T]), "w_out": jnp.stack([mine1[4], mine1[1]]),
             "w_ff1": jnp.stack([mine1[5], mine1[2]]), "w_ff2": jnp.stack([mine1[6], mine1[3]])}
    small_flat = jnp.concatenate([jnp.stack(small_grads[n]).reshape(-1) for n in SMALL])
    everyone, = _all_gather_call([_to_rows(small_flat, 8)], "gather_small_grads")
    small_sum = _sum_call([(everyone, k) for k in range(NDEV)], F32, "sum_small_grads").reshape(-1)
    off = 0
    for n in SMALL:
        sz = 2 * _size(SMALL_SHAPE[n])
        grads[n] = small_sum[off:off + sz].reshape((2,) + SMALL_SHAPE[n])
        off += sz
    cshard = conv_w.shape[-1]
    grads["conv_w"] = lax.dynamic_slice_in_dim(grads["conv_w"], me * cshard, cshard, axis=2)

    deltas, new_m, new_v = {}, {}, {}
    for n in WEIGHTS:
        deltas[n], new_m[n], new_v[n] = _adamw_call(given[n], grads[n], mom[n], var[n], "adamw_" + n)
    return (loss, grad_x, *[grads[n] for n in WEIGHTS], *[deltas[n] for n in WEIGHTS], *[new_m[n] for n in WEIGHTS],
            *[new_v[n] for n in WEIGHTS])
```

```python
import functools

import numpy as np

import jax
import jax.numpy as jnp
from jax import lax
from jax.experimental import pallas as pl
from jax.experimental.pallas import tpu as pltpu

F32, BF16 = jnp.float32, jnp.bfloat16
EPS = 1e-6
LANES = 128
D = 1024
DFF = 4096
NH = 4
DN_W, SB_W, SG_W = 512, 256, 256
IN_DIM = 3336
NDEV = 8
IN_SHARD = IN_DIM // NDEV
IN_SHARD_PAD = 432
FF_SHARD = DFF // NDEV
DN_OFF, SB_OFF, SG_OFF, AB_OFF, NPACK = 0, 2048, 2816, 3328, 3456
SECTIONS = ((DN_OFF, 2048), (SB_OFF, 768), (SG_OFF, 512), (AB_OFF, 128))
SB_SCALE = 64 ** -0.5
DN_SCALE = 128 ** -0.5
VMEM_LIMIT = 56 * 1024 * 1024
VMEM_LIMIT_MAX = 62 * 1024 * 1024
ADAM_LR, ADAM_B1, ADAM_B2, ADAM_EPS, ADAM_WD, ADAM_STEP = 0.001, 0.9, 0.999, 1e-08, 0.01, 10
MESH = pl.DeviceIdType.MESH


def _iota(shape, dim):
    return lax.broadcasted_iota(jnp.int32, shape, dim)


def _params(**kw):
    return pltpu.CompilerParams(vmem_limit_bytes=VMEM_LIMIT, **kw)


NN, NT, TN = ((1,), (0,)), ((1,), (1,)), ((0,), (0,))


def _mm(a, b, dims):
    return lax.dot_general(a.astype(BF16), b.astype(BF16), (dims, ((), ())), preferred_element_type=F32)


def _plain(a, b, dims):
    return (a.T if dims == TN else a), (b.T if dims == NT else b)


def _mmx(a, b, dims):
    return _mm(*_plain(a, b, dims), NN)


@jax.custom_vjp
def _dot(a, b):
    return _mmx(a, b, NN)


def _dot_fwd(a, b):
    return _dot(a, b), (a, b)


def _dot_bwd(res, g):
    a, b = res
    return _mmx(g, b, NT).astype(a.dtype), _mmx(a, g, TN).astype(b.dtype)


_dot.defvjp(_dot_fwd, _dot_bwd)


def _split(x):
    hi = x.astype(BF16)
    return hi, (x - hi.astype(F32)).astype(BF16)


def _mm2(a, b):
    ah, al = _split(a)
    bh = b.astype(BF16)
    mm = lambda x, y: jnp.dot(x, y, preferred_element_type=F32)
    return mm(ah, bh) + mm(al, bh)


def _mm_ones(ones, x, ones_left):
    hi, lo = _split(x)
    mm = (lambda t: jnp.dot(ones, t, preferred_element_type=F32)) if ones_left else \
         (lambda t: jnp.dot(t, ones, preferred_element_type=F32))
    return mm(hi) + mm(lo)


def _pair_ones(kind, transposed):
    row, col = _iota((128, 128), 0), _iota((128, 128), 1)
    m = (row // 64) == (col // 64)
    if kind == "running":
        m = jnp.logical_and(m, (col >= row) if transposed else (col <= row))
    return jnp.where(m, 1.0, 0.0).astype(BF16)


@functools.partial(jax.custom_vjp, nondiff_argnums=(0,))
def _chunk_sum(kind, x):
    return _mm_ones(_pair_ones(kind, False), x, True)


def _chunk_sum_fwd(kind, x):
    return _chunk_sum(kind, x), None


def _chunk_sum_bwd(kind, _, g):
    return (_mm_ones(_pair_ones(kind, True), g, True),)


_chunk_sum.defvjp(_chunk_sum_fwd, _chunk_sum_bwd)


def _tri_ones(n, transposed):
    row, col = _iota((n, n), 0), _iota((n, n), 1)
    return jnp.where((row < col) if transposed else (row > col), 1.0, 0.0).astype(BF16)


def _sigmoid(x):
    return jax.nn.sigmoid(x)


def _silu(x):
    return x * _sigmoid(x)


def _softplus(x):
    return jnp.maximum(x, 0.0) + jnp.log1p(jnp.exp(-jnp.abs(x)))


def _gelu(x):
    return 0.5 * x * (1.0 + jnp.tanh(0.7978845608028654 * (x + 0.044715 * (x * x * x))))


def _rms(x, gain):
    return x * lax.rsqrt(jnp.mean(x * x, axis=-1, keepdims=True) + EPS) * gain


SUBLANES = 8


def _shift_down_impl(x, k):
    y = pltpu.roll(x, k, 0)
    top = jnp.where(_iota((SUBLANES, x.shape[1]), 0) >= k, y[:SUBLANES], 0.0)
    return jnp.concatenate([top, y[SUBLANES:]], axis=0)


def _shift_up_impl(x, k):
    n = x.shape[0]
    y = pltpu.roll(x, n - k, 0)
    bottom = jnp.where(_iota((SUBLANES, x.shape[1]), 0) < SUBLANES - k, y[n - SUBLANES:], 0.0)
    return jnp.concatenate([y[:n - SUBLANES], bottom], axis=0)


@functools.partial(jax.custom_vjp, nondiff_argnums=(1,))
def _shift_down(x, k):
    return _shift_down_impl(x, k)


def _shift_down_fwd(x, k):
    return _shift_down_impl(x, k), None


def _shift_down_bwd(k, _, g):
    return (_shift_up_impl(g, k),)


_shift_down.defvjp(_shift_down_fwd, _shift_down_bwd)


def _lane_pick(x, idx):
    return jnp.sum(jnp.where(_iota(x.shape, 1) == idx, x, 0.0), axis=-1, keepdims=True)


def _dn_conv(x, w0, w1, w2, w3, l2_scale):
    y = _silu(w3 * x + w2 * _shift_down(x, 1) + w1 * _shift_down(x, 2) + w0 * _shift_down(x, 3))
    if l2_scale is None:
        return y
    return y * lax.rsqrt(jnp.sum(y * y, axis=-1, keepdims=True) + EPS) * l2_scale


def _dn_gates(ab, a_log, dt_bias):
    lane = _iota((1, LANES), 1)
    g = -jnp.exp(a_log) * _softplus(ab + dt_bias)
    return jnp.where(lane < NH, g, jnp.where(lane < 2 * NH, _sigmoid(ab), 0.0))


def _same_head(shape):
    return (_iota(shape, 0) < LANES) == (_iota(shape, 1) < LANES)


def _bd(r2):
    return jnp.where(_same_head((2 * LANES, 2 * LANES)), jnp.concatenate([r2, r2], axis=0), 0.0)


def _bd_t(y2):
    t = y2.T
    return jnp.where(_same_head((2 * LANES, 2 * LANES)), jnp.concatenate([t, t], axis=1), 0.0)


def _pair_prod(kind, a2, b2, mm):
    if kind == NN:
        return mm(a2, _bd(b2))
    if kind == NT:
        return mm(a2, _bd_t(b2))
    full = mm(a2.T, b2)
    return jnp.concatenate([full[:LANES, :LANES], full[LANES:, LANES:]], axis=1)


_MM1 = lambda x, y: _mm(x, y, NN)


def _pair_vjp_rule(kind, a2, b2, g, mm):
    if kind == NN:
        return _pair_prod(NT, g, b2, mm), _pair_prod(TN, a2, g, mm)
    if kind == NT:
        return _pair_prod(NN, g, b2, mm), _pair_prod(TN, g, a2, mm)
    return _pair_prod(NT, b2, g, mm), _pair_prod(NN, a2, g, mm)


@functools.partial(jax.custom_vjp, nondiff_argnums=(0,))
def _pdot(kind, a2, b2):
    return _pair_prod(kind, a2, b2, _MM1)


def _pdot_fwd(kind, a2, b2):
    return _pdot(kind, a2, b2), (a2, b2)


def _pdot_bwd(kind, res, g):
    return _pair_vjp_rule(kind, *res, g, _MM1)


_pdot.defvjp(_pdot_fwd, _pdot_bwd)


def _unit_lower_inverse(lower):
    n = lower.shape[0]
    nk = -lower
    inv = jnp.where(_iota(lower.shape, 0) == jnp.bitwise_and(_iota(lower.shape, 1), n - 1), 1.0, 0.0) + nk
    for _ in range(5):
        nk = _pair_prod(NN, nk, nk, _MM1)
        inv = inv + _pair_prod(NN, inv, nk, _MM1)
    return inv


@jax.custom_vjp
def _solve_with(lower, inv, rhs):
    return _pair_prod(NN, inv, rhs, _mm2)


def _solve_with_fwd(lower, inv, rhs):
    x = _pair_prod(NN, inv, rhs, _mm2)
    return x, (inv, x)


def _solve_with_bwd(res, g):
    inv, x = res
    d_rhs = _pair_prod(TN, inv, g, _mm2)
    return -_pair_prod(NT, d_rhs, x, _MM1), jnp.zeros_like(inv), d_rhs


_solve_with.defvjp(_solve_with_fwd, _solve_with_bwd)


def _dn_local(q, k, v, g, beta, inv=None):
    shape = (LANES, 2 * LANES)
    row, col = _iota(shape, 0), jnp.bitwise_and(_iota(shape, 1), LANES - 1)
    same = (row // 64) == (col // 64)
    tri_incl = jnp.logical_and(same, col <= row)
    tri_strict = jnp.logical_and(same, col < row)
    first = row < 64
    gc = _chunk_sum("running", g)
    gl = _chunk_sum("total", g)
    diff = gc - jnp.concatenate([gc[:, :LANES].T, gc[:, LANES:].T], axis=1)
    decay = jnp.where(tri_incl, jnp.exp(jnp.where(tri_incl, diff, 0.0)), 0.0)
    egc = jnp.exp(gc)
    lower = jnp.where(tri_strict, beta * _pdot(NT, k, k) * decay, 0.0)
    if inv is None:
        inv = _unit_lower_inverse(lower)
    u_val = _solve_with(lower, inv, v * beta)
    w_dec = _solve_with(lower, inv, k * (beta * egc))
    qk = jnp.where(tri_incl, _pdot(NT, q, k) * decay, 0.0)
    q_dec = q * egc
    k_dec = k * jnp.exp(gl - gc)
    cd1 = jnp.exp(jnp.sum(jnp.where(first, g, 0.0), axis=0, keepdims=True))
    cd2 = jnp.exp(jnp.sum(jnp.where(first, 0.0, g), axis=0, keepdims=True))
    return (u_val, w_dec, qk, q_dec, k_dec, cd1, cd2), inv


def _dn_state(u_val, w_dec, qk, q_dec, k_dec, cd1, cd2, s0):
    first = _iota((LANES, 2 * LANES), 0) < 64
    u1 = u_val - _pdot(NN, w_dec, s0)
    s1 = s0 * cd1 + _pdot(TN, jnp.where(first, k_dec, 0.0), u1)
    u2 = u_val - _pdot(NN, w_dec, s1)
    u_new = jnp.where(first, u1, u2)
    s2 = s1 * cd2 + _pdot(TN, jnp.where(first, 0.0, k_dec), u_new)
    o = jnp.where(first, _pdot(NN, q_dec, s0), _pdot(NN, q_dec, s1)) + _pdot(NN, qk, u_new)
    return o, s2


def _dn_post(o, z, gain):
    return _rms(o, gain) * _silu(z)


_DN_L2 = (DN_SCALE, 1.0, None)
DN_HPS = 2
DN_BLK = 4 * DN_HPS * LANES
_DN_COLS = tuple(slice(i * LANES, (i + 1) * LANES) for i in range(DN_HPS))


def _dn_in_cols(s, i):
    return slice((s * DN_HPS + i) * LANES, (s * DN_HPS + i + 1) * LANES)


def _dn_taps(cw_ref, s, i):
    return tuple(cw_ref[t:t + 1, _dn_in_cols(s, i)] for t in range(4))


def _pair_rows(n):
    return pl.ds(pl.multiple_of(n * 128, 128), 128)


def _dn_gate_rows(gate, hp):
    head_a = _iota((1, DN_HPS * LANES), 1) < LANES
    h = DN_HPS * hp
    return (jnp.where(head_a, _lane_pick(gate, h), _lane_pick(gate, h + 1)),
            jnp.where(head_a, _lane_pick(gate, NH + h), _lane_pick(gate, NH + h + 1)))


def _dn_gate_cols(dg, db, hp):
    head_a = _iota((1, DN_HPS * LANES), 1) < LANES
    lane = _iota((1, LANES), 1)
    h = DN_HPS * hp
    out = 0.0
    for t, first in ((dg, h), (db, NH + h)):
        out = out + jnp.where(lane == first, jnp.sum(jnp.where(head_a, t, 0.0), axis=-1, keepdims=True), 0.0)
        out = out + jnp.where(lane == first + 1, jnp.sum(jnp.where(head_a, 0.0, t), axis=-1, keepdims=True), 0.0)
    return out


def _dn_in_specs(T, buffers=1):
    mode = pl.Buffered(buffers)
    vec = pl.BlockSpec((1, LANES), lambda b, h: (0, 0))
    return [pl.BlockSpec((T, DN_BLK), lambda b, h: (b, h), pipeline_mode=mode),
            pl.BlockSpec((T, LANES), lambda b, h: (b, 0), pipeline_mode=mode),
            pl.BlockSpec((4, 3 * DN_HPS * LANES), lambda b, h: (0, h)), vec, vec, vec]


def _dn_fwd_call(proj_dn, proj_ab, conv_w, a_log, dt_bias, gain, B, T, gather=()):
    npair = T // 128
    ng = len(gather)
    nsteps = B * (NH // DN_HPS)

    def body(*refs):
        x_ref, ab_ref, cw_ref, alog_ref, dtb_ref, gain_ref = refs[:6]
        out_ref, q_s, k_s, v_s, o_s, gate_s, st_s, inv_s = refs[6 + ng:14 + ng]
        step_id = pl.program_id(0) * (NH // DN_HPS) + pl.program_id(1)
        if ng:
            send, forward, finish = _gather_phases(refs[6:6 + ng], refs[14 + ng:14 + 2 * ng], *refs[14 + 2 * ng:])
            pl.when(step_id == 0)(send)
            pl.when(step_id == nsteps - 1)(forward)
        hp = pl.program_id(1)
        for i, cs in enumerate(_DN_COLS):
            for s, (x_s, l2) in enumerate(zip((q_s, k_s, v_s), _DN_L2)):
                x_s[:, cs] = _dn_conv(x_ref[:, _dn_in_cols(s, i)], *_dn_taps(cw_ref, s, i), l2)
        gate_s[...] = _dn_gates(ab_ref[...], alog_ref[...], dtb_ref[...])

        def local_of(pair):
            r = _pair_rows(pair)
            loc, inv = _dn_local(q_s[r, :], k_s[r, :], v_s[r, :], *_dn_gate_rows(gate_s[r, :], hp))
            inv_s[0, 0, pair] = inv
            return loc

        def state_of(n, loc, state):
            st_s[0, 0, n] = state
            o, s2 = _dn_state(*loc, state)
            o_s[_pair_rows(n), :] = o
            return s2

        def step(n, carry):
            loc, state = carry
            return local_of(n + 1), state_of(n, loc, state)

        loc, state = lax.fori_loop(0, npair - 1, step, (local_of(0), jnp.zeros((LANES, DN_HPS * LANES), F32)))
        state_of(npair - 1, loc, state)
        for i, cs in enumerate(_DN_COLS):
            out_ref[:, cs] = _dn_post(o_s[:, cs], x_ref[:, _dn_in_cols(3, i)], gain_ref[...])
        if ng:
            pl.when(step_id == nsteps - 1)(finish)

    kept_specs, kept_shapes = _dn_kept(B, T)
    outs = pl.pallas_call(
        body, name="dn_fwd", grid=(B, NH // DN_HPS), in_specs=_dn_in_specs(T, 2) + _any_specs(ng),
        out_specs=[pl.BlockSpec((T, DN_HPS * LANES), lambda b, h: (b, h), pipeline_mode=pl.Buffered(1))] + kept_specs + _any_specs(ng),
        out_shape=[jax.ShapeDtypeStruct((B * T, D), F32)] + kept_shapes + _gather_shapes(gather),
        scratch_shapes=_gather_sems(ng) if ng else [],
        compiler_params=_params(dimension_semantics=("arbitrary", "arbitrary")),
    )(proj_dn, proj_ab, conv_w, a_log, dt_bias, gain, *gather)
    return outs[0], outs[1:8], outs[8:]


def _dn_kept(B, T):
    one = pl.Buffered(1)
    npair, pairs = T // 128, NH // DN_HPS
    wide = pl.BlockSpec((T, DN_HPS * LANES), lambda b, h: (b, h), pipeline_mode=one)
    per_pair = pl.BlockSpec((1, 1, npair, LANES, DN_HPS * LANES), lambda b, h: (b, h, 0, 0, 0), pipeline_mode=one)
    specs = [wide] * 4 + [pl.BlockSpec((T, LANES), lambda b, h: (b, h), pipeline_mode=one)] + [per_pair] * 2
    shapes = ([jax.ShapeDtypeStruct((B * T, DN_W), F32)] * 4 + [jax.ShapeDtypeStruct((B * T, pairs * LANES), F32)]
              + [jax.ShapeDtypeStruct((B, pairs, npair, LANES, DN_HPS * LANES), F32)] * 2)
    return specs, shapes


def _dn_bwd_call(proj_dn, proj_ab, dmix, kept, conv_w, a_log, dt_bias, gain, B, T, swap=()):
    npair = T // 128
    ns = len(swap)
    nsteps = B * (NH // DN_HPS)

    def body(*refs):
        x_ref, ab_ref, cw_ref, alog_ref, dtb_ref, gain_ref, do_ref, q_s, k_s, v_s, o_ref, gate_s, st_s, inv_s = refs[:14]
        dx_ref, dab_ref, dcw_ref, dalog_ref, ddtb_ref, dgain_ref = refs[14 + ns:20 + ns]
        dgate_s, do_s = refs[20 + 2 * ns:22 + 2 * ns]
        b_i, hp = pl.program_id(0), pl.program_id(1)
        step_id = b_i * (NH // DN_HPS) + hp
        if ns:
            send, finish = _chip_swap_phases(refs[14:14 + ns], refs[20 + ns:20 + 2 * ns], *refs[22 + 2 * ns:])
            pl.when(step_id == 0)(send)

        def pair_in(r):
            return (q_s[r, :], k_s[r, :], v_s[r, :]) + _dn_gate_rows(gate_s[r, :], hp)

        zero_state = jnp.zeros((LANES, DN_HPS * LANES), F32)

        @pl.when(jnp.logical_and(b_i == 0, hp == 0))
        def _():
            dcw_ref[...] = jnp.zeros_like(dcw_ref)
            dalog_ref[...] = jnp.zeros_like(dalog_ref)
            ddtb_ref[...] = jnp.zeros_like(ddtb_ref)
            dgain_ref[...] = jnp.zeros_like(dgain_ref)

        for i, cs in enumerate(_DN_COLS):
            zc = _dn_in_cols(3, i)
            _, post_vjp = jax.vjp(_dn_post, o_ref[:, cs], x_ref[:, zc], gain_ref[...])
            do, dz, dgain = post_vjp(do_ref[:, cs])
            dx_ref[:, zc] = dz
            do_s[:, cs] = do
            dgain_ref[...] += dgain

        wide_cols = lambda s: slice(s * DN_HPS * LANES, (s + 1) * DN_HPS * LANES)

        def back_step(nn, dstate):
            n = npair - 1 - nn
            r = _pair_rows(n)
            inv = inv_s[0, 0, n]
            local = lambda q, k, v, g, beta, inv=inv: _dn_local(q, k, v, g, beta, inv)[0]
            loc, local_vjp = jax.vjp(local, *pair_in(r))
            _, state_vjp = jax.vjp(_dn_state, *loc, st_s[0, 0, n])
            *dloc, ds0 = state_vjp((do_s[r, :], dstate))
            dq, dk, dv, dg, db = local_vjp(tuple(dloc))
            dx_ref[r, wide_cols(0)], dx_ref[r, wide_cols(1)], dx_ref[r, wide_cols(2)] = dq, dk, dv
            dgate_s[r, :] = _dn_gate_cols(dg, db, hp)
            return ds0

        lax.fori_loop(0, npair, back_step, zero_state)

        for i, cs in enumerate(_DN_COLS):
            h = DN_HPS * hp + i
            for s, l2 in enumerate(_DN_L2):
                xc = _dn_in_cols(s, i)
                _, conv_vjp = jax.vjp(functools.partial(_dn_conv, l2_scale=l2), x_ref[:, xc], *_dn_taps(cw_ref, s, i))
                dx, *dw = conv_vjp(dx_ref[:, xc])
                dx_ref[:, xc] = dx
                for t in range(4):
                    dcw_ref[h + 4 * s, t:t + 1, :] += dw[t]
        _, gate_vjp = jax.vjp(_dn_gates, ab_ref[...], alog_ref[...], dtb_ref[...])
        dab, dalog, ddtb = gate_vjp(dgate_s[...])
        dalog_ref[...] += dalog
        ddtb_ref[...] += ddtb

        @pl.when(hp == 0)
        def _():
            dab_ref[...] = jnp.zeros_like(dab_ref)

        dab_ref[...] += dab
        if ns:
            pl.when(step_id == nsteps - 1)(finish)

    M = B * T
    one = pl.Buffered(1)
    vec = pl.BlockSpec((1, LANES), lambda b, h: (0, 0))
    wide = [pltpu.VMEM((T, DN_HPS * LANES), F32)]
    vec_shape = jax.ShapeDtypeStruct((1, LANES), F32)
    outs = pl.pallas_call(
        body, name="dn_bwd", grid=(B, NH // DN_HPS),
        in_specs=_dn_in_specs(T) + [pl.BlockSpec((T, DN_HPS * LANES), lambda b, h: (b, h), pipeline_mode=one)] + _dn_kept(B, T)[0]
        + _any_specs(ns),
        out_specs=[pl.BlockSpec((T, DN_BLK), lambda b, h: (b, h), pipeline_mode=one), pl.BlockSpec((T, LANES), lambda b, h: (b, 0)),
                   pl.BlockSpec((12, 4, LANES), lambda b, h: (0, 0, 0)), vec, vec, vec] + _any_specs(ns),
        out_shape=[jax.ShapeDtypeStruct((M, 4 * DN_W), F32), jax.ShapeDtypeStruct((M, LANES), F32),
                   jax.ShapeDtypeStruct((12, 4, LANES), F32), vec_shape, vec_shape, vec_shape] + _chip_swap_shapes(swap),
        scratch_shapes=[pltpu.VMEM((T, LANES), F32)] + wide + (_chip_swap_sems(ns) if ns else []),
        compiler_params=pltpu.CompilerParams(vmem_limit_bytes=VMEM_LIMIT_MAX, dimension_semantics=("arbitrary", "arbitrary")),
    )(proj_dn, proj_ab, conv_w, a_log, dt_bias, gain, dmix, *kept, *swap)
    return outs[:6], outs[6:]


SBQ = 256


def _group_rms(x, gain):
    first = _iota(x.shape, 1) < 64
    sq = x * x
    ss_a = jnp.sum(jnp.where(first, sq, 0.0), axis=-1, keepdims=True)
    ss_b = jnp.sum(jnp.where(first, 0.0, sq), axis=-1, keepdims=True)
    ms = jnp.where(first, ss_a, ss_b) * (1.0 / 64)
    return x * lax.rsqrt(ms + EPS) * gain


def _sb_stack(q):
    first = _iota((1, LANES), 1) < 64
    return jnp.concatenate([jnp.where(first, q, 0.0), jnp.where(first, 0.0, q)], axis=0)


def _sb_fold(acc):
    return jnp.where(_iota((1, LANES), 1) < 64, acc[:SBQ], acc[SBQ:])


def _sb_logs(q2, k, diag):
    n = SBQ
    z = _mm(q2, k, ((1,), (1,))) * SB_SCALE
    ls_pos = jnp.minimum(z, 0.0) - jnp.log(1.0 + jnp.exp(-jnp.abs(z)))
    l1m = ls_pos - z
    if not diag:
        return ls_pos, l1m, None
    mask = _iota((2 * n, n), 1) < jnp.bitwise_and(_iota((2 * n, n), 0), n - 1)
    return ls_pos, jnp.where(mask, l1m, 0.0), mask


def _sb_weights(ls_pos, l1m, mask, carry):
    w = jnp.exp(ls_pos + (_mm_ones(_tri_ones(SBQ, False), l1m, False) + carry))
    return w if mask is None else jnp.where(mask, w, 0.0)


def _sb_block(q, k, v, carry, diag):
    ls_pos, l1m, mask = _sb_logs(_sb_stack(q), k, diag)
    w = _sb_weights(ls_pos, l1m, mask, carry)
    return _mm(w, v, ((1,), (0,))), carry + jnp.sum(l1m, axis=-1, keepdims=True), _sb_sum_as_rows(l1m)


SB_ROWS = 16


def _sb_sum_as_rows(l1m):
    ones = jnp.ones((SB_ROWS, SBQ), BF16)
    hi, lo = _split(l1m)
    mm = lambda t: lax.dot_general(ones, t, (NT, ((), ())), preferred_element_type=F32)
    return mm(hi) + mm(lo)


def _sb_rows_as_column(rows):
    pick = jnp.where(_iota((SB_ROWS, SBQ), 0) == 0, 1.0, 0.0).astype(BF16)
    hi = rows.astype(BF16)
    rest = rows - hi.astype(F32)
    mid = rest.astype(BF16)
    lo = (rest - mid.astype(F32)).astype(BF16)
    mm = lambda t: lax.dot_general(t, pick, (TN, ((), ())), preferred_element_type=F32)
    return mm(hi) + (mm(mid) + mm(lo))


def _sb_block_bwd(q, k, v, carry, diag, dpv, dcarry):
    q2 = _sb_stack(q)
    ls_pos, l1m, mask = _sb_logs(q2, k, diag)
    w = _sb_weights(ls_pos, l1m, mask, carry)
    dv = _mm(w, dpv, ((0,), (0,)))
    de = _mm(dpv, v, ((1,), (1,))) * w
    dl1m = jnp.dot(de.astype(BF16), _tri_ones(SBQ, True), preferred_element_type=F32) + dcarry
    if mask is not None:
        dl1m = jnp.where(mask, dl1m, 0.0)
    sig = jnp.exp(ls_pos)
    dz = (de * (1.0 - sig) - dl1m * sig) * SB_SCALE
    dq = _sb_fold(_mm(dz, k, ((1,), (0,))))
    return dq, _mm(dz, q2, ((0,), (0,))), dv, dcarry + jnp.sum(de, axis=-1, keepdims=True)


_SB_Q, _SB_K, _SB_V = (slice(i * LANES, (i + 1) * LANES) for i in range(3))


def _sb_fwd_call(proj_sb, mix, q_gain, k_gain, B, T, gather=()):
    nblk = T // SBQ
    ng = len(gather)
    nsteps = 2 * B

    def body(*refs):
        x_ref, qg_ref, kg_ref = refs[:3]
        out_ref, carry_ref = refs[4 + ng:6 + ng]
        q_s, k_s = refs[6 + 2 * ng:8 + 2 * ng]
        step_id = 2 * pl.program_id(0) + pl.program_id(1)
        if ng:
            send, forward, finish = _gather_phases(refs[4:4 + ng], refs[6 + ng:6 + 2 * ng], *refs[8 + 2 * ng:])
            pl.when(step_id == 0)(send)
            pl.when(step_id == nsteps - 1)(forward)
        q_s[...] = _group_rms(x_ref[:, _SB_Q], qg_ref[...])
        k_s[...] = _group_rms(x_ref[:, _SB_K], kg_ref[...])

        def qblock(i, _):
            ri = pl.ds(pl.multiple_of(i * SBQ, SBQ), SBQ)
            q = q_s[ri, :]

            def kblock(jj, c):
                j = i - 1 - jj
                rj = pl.ds(pl.multiple_of(j * SBQ, SBQ), SBQ)
                carry_ref[0, 0, i, j] = c[2]
                pv, carry, rows = _sb_block(q, k_s[rj, :], x_ref[rj, _SB_V], c[1], False)
                return c[0] + pv, carry, c[2] + rows

            on_diag = _sb_block(q, k_s[ri, :], x_ref[ri, _SB_V], jnp.zeros((2 * SBQ, 1), F32), True)
            acc, _c, _r = lax.fori_loop(0, i, kblock, on_diag)
            out_ref[ri, :] = _sb_fold(acc)
            return 0

        lax.fori_loop(0, nblk, qblock, 0)
        if ng:
            pl.when(step_id == nsteps - 1)(finish)

    vec = pl.BlockSpec((1, LANES), lambda b, p: (0, 0))
    outs = pl.pallas_call(
        body, name="sb_fwd", grid=(B, 2),
        in_specs=[pl.BlockSpec((T, 3 * LANES), lambda b, p: (b, p)), vec, vec, pl.BlockSpec(memory_space=pl.ANY)] + _any_specs(ng),
        out_specs=[pl.BlockSpec((T, LANES), lambda b, p: (b, DN_W // LANES + p)), _sb_carry_spec(nblk)] + _any_specs(ng),
        out_shape=[jax.ShapeDtypeStruct((B * T, D), F32), jax.ShapeDtypeStruct((B, 2, nblk, nblk, SB_ROWS, 2 * SBQ), F32)]
        + _gather_shapes(gather), input_output_aliases={3: 0},
        scratch_shapes=[pltpu.VMEM((T, LANES), F32)] * 2 + (_gather_sems(ng) if ng else []),
        compiler_params=_params(dimension_semantics=("arbitrary", "arbitrary")),
    )(proj_sb, q_gain, k_gain, mix, *gather)
    return outs[0], outs[1], outs[2:]


def _sb_carry_spec(nblk):
    return pl.BlockSpec((1, 1, nblk, nblk, SB_ROWS, 2 * SBQ), lambda b, p: (b, p, 0, 0, 0, 0))


def _sb_bwd_call(proj_sb, dmix, carries, q_gain, k_gain, B, T):
    nblk = T // SBQ

    def body(x_ref, qg_ref, kg_ref, do_ref, carry_ref, dx_ref, dqg_ref, dkg_ref, q_s, k_s, dq_s, dk_s, dv_s):
        b_i, p = pl.program_id(0), pl.program_id(1)
        qn, q_vjp = jax.vjp(_group_rms, x_ref[:, _SB_Q], qg_ref[...])
        kn, k_vjp = jax.vjp(_group_rms, x_ref[:, _SB_K], kg_ref[...])
        q_s[...], k_s[...] = qn, kn
        dk_s[...] = jnp.zeros_like(dk_s)
        dv_s[...] = jnp.zeros_like(dv_s)

        def qblock(i, _):
            ri = pl.ds(pl.multiple_of(i * SBQ, SBQ), SBQ)
            q = q_s[ri, :]
            dacc = _sb_stack(do_ref[ri, :])

            def kblock(j, c):
                rj = pl.ds(pl.multiple_of(j * SBQ, SBQ), SBQ)
                carry = _sb_rows_as_column(carry_ref[0, 0, i, j])
                dq_j, dk_j, dv_j, dc = _sb_block_bwd(q, k_s[rj, :], x_ref[rj, _SB_V], carry, False, dacc, c[1])
                dk_s[rj, :] += dk_j
                dv_s[rj, :] += dv_j
                return c[0] + dq_j, dc

            dq, dc = lax.fori_loop(0, i, kblock, (jnp.zeros((SBQ, LANES), F32), jnp.zeros((2 * SBQ, 1), F32)))
            dq_i, dk_i, dv_i, _dc = _sb_block_bwd(q, k_s[ri, :], x_ref[ri, _SB_V], jnp.zeros((2 * SBQ, 1), F32), True, dacc, dc)
            dk_s[ri, :] += dk_i
            dv_s[ri, :] += dv_i
            dq_s[ri, :] = dq + dq_i
            return 0

        lax.fori_loop(0, nblk, qblock, 0)
        dq_in, dqg = q_vjp(dq_s[...])
        dk_in, dkg = k_vjp(dk_s[...])
        dx_ref[:, _SB_Q], dx_ref[:, _SB_K], dx_ref[:, _SB_V] = dq_in, dk_in, dv_s[...]

        @pl.when(jnp.logical_and(b_i == 0, p == 0))
        def _():
            dqg_ref[...] = jnp.zeros_like(dqg_ref)
            dkg_ref[...] = jnp.zeros_like(dkg_ref)

        dqg_ref[...] += dqg + pltpu.roll(dqg, 64, 1)
        dkg_ref[...] += dkg + pltpu.roll(dkg, 64, 1)

    M = B * T
    vec = pl.BlockSpec((1, LANES), lambda b, p: (0, 0))
    blk = pl.BlockSpec((T, 3 * LANES), lambda b, p: (b, p))
    big = [pltpu.VMEM((T, LANES), F32)]
    return pl.pallas_call(
        body, name="sb_bwd", grid=(B, 2),
        in_specs=[blk, vec, vec, pl.BlockSpec((T, LANES), lambda b, p: (b, DN_W // LANES + p)), _sb_carry_spec(nblk)],
        out_specs=[blk, vec, vec],
        out_shape=[jax.ShapeDtypeStruct((M, 3 * SB_W), F32)] + [jax.ShapeDtypeStruct((1, LANES), F32)] * 2,
        scratch_shapes=big * 5,
        compiler_params=_params(dimension_semantics=("arbitrary", "arbitrary")),
    )(proj_sb, q_gain, k_gain, dmix, carries)


def _sg_chunk(u, v, gain, w_a, w_b, bias):
    n = 128
    row, col = _iota((n, n), 0), _iota((n, n), 1)
    first = _iota((1, LANES), 1) < 64
    vn = _group_rms(_gelu(v), gain)
    tril = col <= row
    mixed = jnp.where(first, _dot(jnp.where(tril, w_a, 0.0), vn), _dot(jnp.where(tril, w_b, 0.0), vn)) + bias
    return _gelu(u) * mixed


_SG_U, _SG_V = slice(0, LANES), slice(LANES, 2 * LANES)


def _sg_fwd_call(proj_sg, mix, gain, sg_w, bias, B, T):
    nchunk = T // 128

    def body(x_ref, g_ref, wa_ref, wb_ref, bias_ref, mix_ref, out_ref):
        del mix_ref

        def step(i, _):
            r = pl.ds(pl.multiple_of(i * 128, 128), 128)
            out_ref[r, :] = _sg_chunk(x_ref[r, _SG_U], x_ref[r, _SG_V], g_ref[...], wa_ref[0], wb_ref[0], bias_ref[...])
            return 0

        lax.fori_loop(0, nchunk, step, 0)

    return pl.pallas_call(
        body, name="sg_fwd", grid=(B, 2),
        in_specs=[pl.BlockSpec((T, 2 * LANES), lambda b, p: (b, p)), pl.BlockSpec((1, LANES), lambda b, p: (0, p)),
                  pl.BlockSpec((1, 128, 128), lambda b, p: (2 * p, 0, 0)), pl.BlockSpec((1, 128, 128), lambda b, p: (2 * p + 1, 0, 0)),
                  pl.BlockSpec((128, LANES), lambda b, p: (0, p)), pl.BlockSpec(memory_space=pl.ANY)],
        out_specs=pl.BlockSpec((T, LANES), lambda b, p: (b, (DN_W + SB_W) // LANES + p)),
        out_shape=jax.ShapeDtypeStruct((B * T, D), F32), input_output_aliases={5: 0},
        compiler_params=_params(dimension_semantics=("arbitrary", "arbitrary")),
    )(proj_sg, gain, sg_w, sg_w, bias, mix)


def _sg_bwd_call(proj_sg, dmix, gain, sg_w, bias, B, T):
    nchunk = T // 128

    def body(x_ref, g_ref, wa_ref, wb_ref, bias_ref, do_ref, dx_ref, dg_ref, dw_ref, db_ref):
        p, b_i = pl.program_id(0), pl.program_id(1)

        def step(i, c):
            r = pl.ds(pl.multiple_of(i * 128, 128), 128)
            _, vjp = jax.vjp(_sg_chunk, x_ref[r, _SG_U], x_ref[r, _SG_V], g_ref[...], wa_ref[0], wb_ref[0], bias_ref[...])
            du, dv, dg, dwa, dwb, dbias = vjp(do_ref[r, :])
            dx_ref[r, _SG_U], dx_ref[r, _SG_V] = du, dv
            return c[0] + dg, c[1] + dwa, c[2] + dwb, c[3] + dbias

        z = jnp.zeros((128, 128), F32)
        dg, dwa, dwb, dbias = lax.fori_loop(0, nchunk, step, (jnp.zeros((1, LANES), F32), z, z, z))
        lane = _iota((1, LANES), 1)
        first = lane < 64
        s_a = jnp.sum(jnp.where(first, dbias, 0.0), axis=-1, keepdims=True)
        s_b = jnp.sum(jnp.where(first, 0.0, dbias), axis=-1, keepdims=True)
        dbg = jnp.where(lane == 2 * p, s_a, 0.0) + jnp.where(lane == 2 * p + 1, s_b, 0.0)

        @pl.when(b_i == 0)
        def _():
            dg_ref[...] = jnp.zeros_like(dg_ref)
            dw_ref[...] = jnp.zeros_like(dw_ref)

        @pl.when(jnp.logical_and(b_i == 0, p == 0))
        def _():
            db_ref[...] = jnp.zeros_like(db_ref)

        dg_ref[...] += dg
        dw_ref[0] += dwa
        dw_ref[1] += dwb
        db_ref[...] += dbg

    M = B * T
    blk = pl.BlockSpec((T, 2 * LANES), lambda p, b: (b, p))
    return pl.pallas_call(
        body, name="sg_bwd", grid=(2, B),
        in_specs=[blk, pl.BlockSpec((1, LANES), lambda p, b: (0, p)),
                  pl.BlockSpec((1, 128, 128), lambda p, b: (2 * p, 0, 0)), pl.BlockSpec((1, 128, 128), lambda p, b: (2 * p + 1, 0, 0)),
                  pl.BlockSpec((128, LANES), lambda p, b: (0, p)),
                  pl.BlockSpec((T, LANES), lambda p, b: (b, (DN_W + SB_W) // LANES + p))],
        out_specs=[blk, pl.BlockSpec((1, LANES), lambda p, b: (0, p)), pl.BlockSpec((2, 128, 128), lambda p, b: (p, 0, 0)),
                   pl.BlockSpec((128, LANES), lambda p, b: (0, 0))],
        out_shape=[jax.ShapeDtypeStruct((M, 2 * SG_W), F32), jax.ShapeDtypeStruct((1, SG_W), F32),
                   jax.ShapeDtypeStruct((4, 128, 128), F32), jax.ShapeDtypeStruct((128, LANES), F32)],
        compiler_params=_params(dimension_semantics=("arbitrary", "arbitrary")),
    )(proj_sg, gain, sg_w, sg_w, bias, dmix)


def _row_tile(m, most=512):
    return min(m, most)


def _in_proj_call(x, gain, wt):
    m = x.shape[0]
    tm = _row_tile(m)

    def body(x_ref, g_ref, wt_ref, *out_refs):
        h = _rms(x_ref[...], g_ref[...]).astype(BF16)
        out_refs[-1][...] = h
        for (off, width), out_ref in zip(SECTIONS, out_refs):
            out_ref[...] = lax.dot_general(h, wt_ref[off:off + width, :], (((1,), (1,)), ((), ())), preferred_element_type=F32)

    rows = lambda width: pl.BlockSpec((tm, width), lambda i: (i, 0))
    return pl.pallas_call(
        body, name="in_proj", grid=(m // tm,),
        in_specs=[rows(D), pl.BlockSpec((1, D), lambda i: (0, 0)),
                  pl.BlockSpec((NPACK, D), lambda i: (0, 0), pipeline_mode=pl.Buffered(1))],
        out_specs=[rows(w) for _, w in SECTIONS] + [rows(D)],
        out_shape=[jax.ShapeDtypeStruct((m, w), F32) for _, w in SECTIONS] + [jax.ShapeDtypeStruct((m, D), BF16)],
        compiler_params=_params(dimension_semantics=("arbitrary",)),
    )(x, gain, wt)


def _in_proj_bwd_call(dsections, wt, x, gain, dres, swap=()):
    m = x.shape[0]
    tm = _row_tile(m)
    nsec, ns = len(SECTIONS), len(swap)

    def body(*refs):
        ds_refs = refs[:nsec]
        wt_ref, x_ref, g_ref, dres_ref = refs[nsec:nsec + 4]
        dx_ref, dg_ref = refs[nsec + 4 + ns:nsec + 6 + ns]
        step = pl.program_id(0)
        if ns:
            send, finish = _chip_swap_phases(refs[nsec + 4:nsec + 4 + ns], refs[nsec + 6 + ns:nsec + 6 + 2 * ns],
                                             *refs[nsec + 6 + 2 * ns:])
            pl.when(step == 0)(send)

        @pl.when(step == 0)
        def _():
            dg_ref[...] = jnp.zeros_like(dg_ref)

        dh = 0.0
        for (off, width), ds_ref in zip(SECTIONS, ds_refs):
            dh = dh + jnp.dot(ds_ref[...].astype(BF16), wt_ref[off:off + width, :], preferred_element_type=F32)
        _, vjp = jax.vjp(_rms, x_ref[...], g_ref[...])
        dx, dg = vjp(dh)
        dx_ref[...] = dres_ref[...] + dx
        dg_ref[...] += dg
        if ns:
            pl.when(step == m // tm - 1)(finish)

    rows = lambda width: pl.BlockSpec((tm, width), lambda i: (i, 0))
    outs = pl.pallas_call(
        body, name="in_proj_bwd", grid=(m // tm,),
        in_specs=[rows(w) for _, w in SECTIONS] + [pl.BlockSpec((NPACK, D), lambda i: (0, 0), pipeline_mode=pl.Buffered(1)),
                                                   rows(D), pl.BlockSpec((1, D), lambda i: (0, 0)), rows(D)] + _any_specs(ns),
        out_specs=[rows(D), pl.BlockSpec((1, D), lambda i: (0, 0))] + _any_specs(ns),
        out_shape=[jax.ShapeDtypeStruct((m, D), F32), jax.ShapeDtypeStruct((1, D), F32)] + _chip_swap_shapes(swap),
        scratch_shapes=_chip_swap_sems(ns) if ns else [],
        compiler_params=_params(dimension_semantics=("arbitrary",)),
    )(*dsections, wt, x, gain, dres, *swap)
    return outs[:2], outs[2:]


def _in_proj_grad_call(dsections, h):
    m = h.shape[0]
    tm = min(m, 256)

    def body(*refs):
        ds_refs, (h_ref, out_ref) = refs[:len(SECTIONS)], refs[len(SECTIONS):]

        @pl.when(pl.program_id(0) == 0)
        def _():
            out_ref[...] = jnp.zeros_like(out_ref)

        for (off, width), ds_ref in zip(SECTIONS, ds_refs):
            out_ref[off:off + width, :] += lax.dot_general(ds_ref[...].astype(BF16), h_ref[...], (((0,), (0,)), ((), ())),
                                                           preferred_element_type=F32)

    rows = lambda width: pl.BlockSpec((tm, width), lambda i: (i, 0))
    return pl.pallas_call(
        body, name="grad_w_in", grid=(m // tm,),
        in_specs=[rows(w) for _, w in SECTIONS] + [rows(D)],
        out_specs=pl.BlockSpec((NPACK, D), lambda i: (0, 0), pipeline_mode=pl.Buffered(1)),
        out_shape=jax.ShapeDtypeStruct((NPACK, D), F32),
        compiler_params=_params(dimension_semantics=("arbitrary",)),
    )(*dsections, h)


def _packed_column_of():
    t = np.full(NPACK, -1, np.int64)
    lanes = np.arange(LANES)
    for pair in range(2):
        for s in range(4):
            t[DN_OFF + pair * 1024 + s * 256 + np.arange(256)] = s * DN_W + pair * 256 + np.arange(256)
        for s in range(3):
            t[SB_OFF + pair * 384 + s * LANES + lanes] = 2056 + s * SB_W + pair * LANES + lanes
        for s in range(2):
            t[SG_OFF + pair * 256 + s * LANES + lanes] = 2056 + 3 * SB_W + s * SG_W + pair * LANES + lanes
    t[AB_OFF + np.arange(2 * NH)] = 4 * DN_W + np.arange(2 * NH)
    return t


def _row_tables():
    col = _packed_column_of()
    fwd = np.where(col >= 0, (col // IN_SHARD) * IN_SHARD_PAD + col % IN_SHARD, -1)
    packed_of = np.full(IN_DIM, -1, np.int64)
    packed_of[col[col >= 0]] = np.nonzero(col >= 0)[0]
    r = np.arange(NDEV * IN_SHARD_PAD)
    inside = r % IN_SHARD_PAD < IN_SHARD
    back = np.where(inside, packed_of[np.minimum((r // IN_SHARD_PAD) * IN_SHARD + r % IN_SHARD_PAD, IN_DIM - 1)], -1)
    return fwd, back


def _row_perm_call(src, table, name):
    n_out = table.shape[0]
    touched = [sorted(set((table[b * 128:(b + 1) * 128][table[b * 128:(b + 1) * 128] >= 0] // 128).tolist()))
               for b in range(n_out // 128)]

    def body(tbl_ref, src_ref, out_ref):
        lane = _iota((1, LANES), 1)
        for b, blocks in enumerate(touched):
            want = tbl_ref[b * 128:(b + 1) * 128, :]
            acc = jnp.zeros((128, D), F32)
            for sb in blocks:
                pick = jnp.where(want == sb * 128 + lane, 1.0, 0.0).astype(BF16)
                acc = acc + jnp.dot(pick, src_ref[sb * 128:(sb + 1) * 128, :].astype(BF16), preferred_element_type=F32)
            out_ref[b * 128:(b + 1) * 128, :] = acc.astype(BF16)

    return pl.pallas_call(
        body, name=name, out_shape=jax.ShapeDtypeStruct((n_out, D), BF16),
        in_specs=[pl.BlockSpec(memory_space=pltpu.VMEM)] * 2, out_specs=pl.BlockSpec(memory_space=pltpu.VMEM),
        compiler_params=_params(),
    )(jnp.asarray(table.reshape(-1, 1), jnp.int32), src)


def _out_proj_call(a, w, res):
    m, k = a.shape
    n = w.shape[1]
    tm = _row_tile(m)

    def body(a_ref, w_ref, res_ref, out_ref):
        out_ref[...] = res_ref[...] + jnp.dot(a_ref[...].astype(BF16), w_ref[...], preferred_element_type=F32)

    return pl.pallas_call(
        body, name="out_proj", grid=(m // tm,),
        in_specs=[pl.BlockSpec((tm, k), lambda i: (i, 0)), pl.BlockSpec((k, n), lambda i: (0, 0)),
                  pl.BlockSpec((tm, n), lambda i: (i, 0))],
        out_specs=pl.BlockSpec((tm, n), lambda i: (i, 0)),
        out_shape=jax.ShapeDtypeStruct((m, n), F32),
        compiler_params=_params(dimension_semantics=("arbitrary",)),
    )(a, w, res)


def _ffn_specs(tm):
    return [pl.BlockSpec((1, D, FF_SHARD), lambda i, j: (j, 0, 0)), pl.BlockSpec((FF_SHARD, D), lambda i, j: (j, 0))]


def _ffn_fwd_call(x, gain, w1, w2):
    m = x.shape[0]
    tm = _row_tile(m, 1024)

    def body(x_ref, g_ref, w1_ref, w2_ref, out_ref, h_s, acc_s):
        j = pl.program_id(1)

        @pl.when(j == 0)
        def _():
            h_s[...] = _rms(x_ref[...], g_ref[...]).astype(BF16)
            acc_s[...] = jnp.zeros_like(acc_s)

        a = jnp.maximum(jnp.dot(h_s[...], w1_ref[0], preferred_element_type=F32), 0.0)
        acc_s[...] += jnp.dot((a * a).astype(BF16), w2_ref[...], preferred_element_type=F32)

        @pl.when(j == NDEV - 1)
        def _():
            out_ref[...] = x_ref[...] + acc_s[...]

    return pl.pallas_call(
        body, name="ffn_fwd", grid=(m // tm, NDEV),
        in_specs=[pl.BlockSpec((tm, D), lambda i, j: (i, 0)), pl.BlockSpec((1, D), lambda i, j: (0, 0))] + _ffn_specs(tm),
        out_specs=pl.BlockSpec((tm, D), lambda i, j: (i, 0)),
        out_shape=jax.ShapeDtypeStruct((m, D), F32),
        scratch_shapes=[pltpu.VMEM((tm, D), BF16), pltpu.VMEM((tm, D), F32)],
        compiler_params=_params(dimension_semantics=("arbitrary", "arbitrary")),
    )(x, gain, w1, w2)


def _ffn_bwd_call(x, dy, gain, w1, w2, swap=()):
    m = x.shape[0]
    tm = _row_tile(m, 1024)
    ns = len(swap)

    def body(*refs):
        x_ref, dy_ref, g_ref, w1_ref, w2_ref = refs[:5]
        dx_ref, da_ref, r_ref, h_ref, dg_ref = refs[5 + ns:10 + ns]
        acc_s = refs[10 + 2 * ns]
        i, j = pl.program_id(0), pl.program_id(1)
        if ns:
            send, finish = _sibling_swap_phases(refs[5:5 + ns], refs[10 + ns:10 + 2 * ns], *refs[11 + 2 * ns:])
            pl.when(jnp.logical_and(i == 0, j == 0))(send)

        @pl.when(j == 0)
        def _():
            h_ref[...] = _rms(x_ref[...], g_ref[...]).astype(BF16)
            acc_s[...] = jnp.zeros_like(acc_s)

        @pl.when(jnp.logical_and(i == 0, j == 0))
        def _():
            dg_ref[...] = jnp.zeros_like(dg_ref)

        a = jnp.maximum(jnp.dot(h_ref[...], w1_ref[0], preferred_element_type=F32), 0.0)
        r_ref[...] = (a * a).astype(BF16)
        dr = lax.dot_general(dy_ref[...].astype(BF16), w2_ref[...], (((1,), (1,)), ((), ())), preferred_element_type=F32)
        da = (dr * (2.0 * a)).astype(BF16)
        da_ref[...] = da
        acc_s[...] += lax.dot_general(da, w1_ref[0], (((1,), (1,)), ((), ())), preferred_element_type=F32)

        @pl.when(j == NDEV - 1)
        def _():
            _, vjp = jax.vjp(_rms, x_ref[...], g_ref[...])
            dx, dg = vjp(acc_s[...])
            dx_ref[...] = dy_ref[...] + dx
            dg_ref[...] += dg

        if ns:
            pl.when(jnp.logical_and(i == m // tm - 1, j == NDEV - 1))(finish)

    outs = pl.pallas_call(
        body, name="ffn_bwd", grid=(m // tm, NDEV),
        in_specs=[pl.BlockSpec((tm, D), lambda i, j: (i, 0)), pl.BlockSpec((tm, D), lambda i, j: (i, 0)),
                  pl.BlockSpec((1, D), lambda i, j: (0, 0))] + _ffn_specs(tm) + _any_specs(ns),
        out_specs=[pl.BlockSpec((tm, D), lambda i, j: (i, 0)), pl.BlockSpec((tm, FF_SHARD), lambda i, j: (i, j)),
                   pl.BlockSpec((tm, FF_SHARD), lambda i, j: (i, j)), pl.BlockSpec((tm, D), lambda i, j: (i, 0)),
                   pl.BlockSpec((1, D), lambda i, j: (0, 0))] + _any_specs(ns),
        out_shape=[jax.ShapeDtypeStruct((m, D), F32), jax.ShapeDtypeStruct((m, DFF), BF16), jax.ShapeDtypeStruct((m, DFF), BF16),
                   jax.ShapeDtypeStruct((m, D), BF16), jax.ShapeDtypeStruct((1, D), F32)] + _sibling_swap_shapes(swap),
        scratch_shapes=[pltpu.VMEM((tm, D), F32)] + (_sibling_swap_sems(ns) if ns else []),
        compiler_params=_params(dimension_semantics=("arbitrary", "arbitrary")),
    )(x, dy, gain, w1, w2, *swap)
    return outs[:5], outs[5:]


def _mm_nt_call(a, b, name):
    m, k = a.shape
    n = b.shape[0]
    tm = _row_tile(m)

    def body(a_ref, b_ref, out_ref):
        out_ref[...] = lax.dot_general(a_ref[...].astype(BF16), b_ref[...].astype(BF16), (((1,), (1,)), ((), ())),
                                       preferred_element_type=F32)

    return pl.pallas_call(
        body, name=name, grid=(m // tm,),
        in_specs=[pl.BlockSpec((tm, k), lambda i: (i, 0)), pl.BlockSpec((n, k), lambda i: (0, 0))],
        out_specs=pl.BlockSpec((tm, n), lambda i: (i, 0)),
        out_shape=jax.ShapeDtypeStruct((m, n), F32),
        compiler_params=_params(dimension_semantics=("arbitrary",)),
    )(a, b)


def _mm_tn_call(a, b, name, col_shards=False):
    m, k = a.shape
    n = b.shape[1]
    tm, tk = _row_tile(m, 1024), min(k, 1024)
    tn = n // NDEV if col_shards else min(n, 1024)

    def body(a_ref, b_ref, out_ref, acc_s):
        s = pl.program_id(2)

        @pl.when(s == 0)
        def _():
            acc_s[...] = jnp.zeros_like(acc_s)

        acc_s[...] += lax.dot_general(a_ref[...].astype(BF16), b_ref[...].astype(BF16), (((0,), (0,)), ((), ())),
                                      preferred_element_type=F32)

        @pl.when(s == m // tm - 1)
        def _():
            out_ref[...] = acc_s[...].astype(BF16).reshape(out_ref.shape)

    if col_shards:
        out_spec, out_shape = pl.BlockSpec((1, tk, tn), lambda i, j, s: (j, i, 0)), (NDEV, k, tn)
    else:
        out_spec, out_shape = pl.BlockSpec((tk, tn), lambda i, j, s: (i, j)), (k, n)
    return pl.pallas_call(
        body, name=name, grid=(k // tk, n // tn, m // tm),
        in_specs=[pl.BlockSpec((tm, tk), lambda i, j, s: (s, i)), pl.BlockSpec((tm, tn), lambda i, j, s: (s, j))],
        out_specs=out_spec, out_shape=jax.ShapeDtypeStruct(out_shape, BF16),
        scratch_shapes=[pltpu.VMEM((tk, tn), F32)],
        compiler_params=_params(dimension_semantics=("arbitrary", "arbitrary", "arbitrary")),
    )(a, b)


def _loss_call(y, target):
    m = y.shape[0]
    tm = _row_tile(m)

    def body(y_ref, t_ref, loss_ref, dy_ref):
        @pl.when(pl.program_id(0) == 0)
        def _():
            loss_ref[...] = jnp.zeros_like(loss_ref)

        err = y_ref[...] - t_ref[...]
        dy_ref[...] = err * (1.0 / D)
        per_row = jnp.mean(err * err, axis=-1, keepdims=True)
        loss_ref[...] += jnp.broadcast_to(0.5 * jnp.sum(per_row, axis=0, keepdims=True), (1, LANES))

    return pl.pallas_call(
        body, name="loss", grid=(m // tm,),
        in_specs=[pl.BlockSpec((tm, D), lambda i: (i, 0))] * 2,
        out_specs=[pl.BlockSpec((1, LANES), lambda i: (0, 0)), pl.BlockSpec((tm, D), lambda i: (i, 0))],
        out_shape=[jax.ShapeDtypeStruct((1, LANES), F32), jax.ShapeDtypeStruct((m, D), F32)],
        compiler_params=_params(dimension_semantics=("arbitrary",)),
    )(y, target)


def _adamw_call(w, g, m, v, name):
    shape = w.shape
    cols = shape[-1] if w.ndim > 1 else w.size
    rows = w.size // cols
    tr = rows if (rows <= 512 or rows % 512) else 512
    c1, c2 = 1.0 - ADAM_B1 ** ADAM_STEP, 1.0 - ADAM_B2 ** ADAM_STEP

    def body(w_ref, g_ref, m_ref, v_ref, d_ref, nm_ref, nv_ref):
        g_ = g_ref[...]
        nm = ADAM_B1 * m_ref[...] + (1.0 - ADAM_B1) * g_
        nv = ADAM_B2 * v_ref[...] + (1.0 - ADAM_B2) * (g_ * g_)
        d_ref[...] = -ADAM_LR * ((nm / c1) / (jnp.sqrt(nv / c2) + ADAM_EPS) + ADAM_WD * w_ref[...])
        nm_ref[...], nv_ref[...] = nm, nv

    spec = pl.BlockSpec((tr, cols), lambda i: (i, 0))
    outs = pl.pallas_call(
        body, name=name, grid=(rows // tr,), in_specs=[spec] * 4, out_specs=[spec] * 3,
        out_shape=[jax.ShapeDtypeStruct((rows, cols), F32)] * 3,
        compiler_params=_params(dimension_semantics=("arbitrary",)),
    )(*(t.reshape(rows, cols) for t in (w, g, m, v)))
    return tuple(o.reshape(shape) for o in outs)


def _sum_tile(rows):
    for cand in (2048, 1024, 512, 256, 128):
        if rows > cand and rows % cand == 0:
            return cand
    return rows


def _pair_sum_call(gs, gots, core, name):
    n = len(gs)

    def body(core_ref, *refs):
        del core_ref
        for g_ref, got_ref, out_ref in zip(refs[:n], refs[n:2 * n], refs[2 * n:]):
            out_ref[...] = (g_ref[...].astype(F32) + got_ref[...].astype(F32)).astype(BF16)

    block = lambda g: (1,) + g.shape[1:]
    grid_spec = pltpu.PrefetchScalarGridSpec(
        num_scalar_prefetch=1, grid=(4,),
        in_specs=[pl.BlockSpec(block(g), lambda ch, core_ref: (2 * ch + core_ref[0], 0, 0)) for g in gs]
        + [pl.BlockSpec(block(g), lambda ch, core_ref: (ch, 0, 0)) for g in gs],
        out_specs=[pl.BlockSpec(block(g), lambda ch, core_ref: (ch, 0, 0)) for g in gs])
    return pl.pallas_call(
        body, name=name, grid_spec=grid_spec, out_shape=[jax.ShapeDtypeStruct((4,) + g.shape[1:], BF16) for g in gs],
        compiler_params=_params(dimension_semantics=("arbitrary",)),
    )(jnp.asarray(core, jnp.int32).reshape(1), *gs, *gots)


def _total_sum_call(gs, gots, froms, me, my_chip, name):
    n = len(gs)

    def body(idx_ref, *refs):
        del idx_ref
        for a in range(n):
            g_ref, got_ref, f0, f1, f2 = (refs[k * n + a] for k in range(5))
            acc = g_ref[0].astype(F32) + got_ref[0].astype(F32)
            for f in (f0, f1, f2):
                acc = acc + f[0].astype(F32)
            refs[5 * n + a][...] = acc

    block = lambda g: (1,) + g.shape[1:]
    picked = lambda which: [pl.BlockSpec(block(g), lambda t, idx, which=which: (idx[which], 0, 0)) for g in gs]
    fixed = lambda j: [pl.BlockSpec(block(g), lambda t, idx, j=j: (j, 0, 0)) for g in gs]
    grid_spec = pltpu.PrefetchScalarGridSpec(
        num_scalar_prefetch=1, grid=(1,),
        in_specs=picked(0) + picked(1) + fixed(0) + fixed(1) + fixed(2),
        out_specs=[pl.BlockSpec(g.shape[1:], lambda t, idx: (0, 0)) for g in gs])
    return pl.pallas_call(
        body, name=name, grid_spec=grid_spec, out_shape=[jax.ShapeDtypeStruct(g.shape[1:], F32) for g in gs],
        compiler_params=_params(dimension_semantics=("arbitrary",)),
    )(jnp.stack([jnp.asarray(me, jnp.int32), jnp.asarray(my_chip, jnp.int32)]), *gs, *gots, *froms, *froms, *froms)


def _sum_call(parts, out_dtype, name):
    rows, cols = parts[0][0].shape[1:]
    tr = _sum_tile(rows)
    index = jnp.stack([jnp.asarray(i, jnp.int32) for _, i in parts])

    def body(idx_ref, *refs):
        del idx_ref
        acc = refs[0][0].astype(F32)
        for r in refs[1:-1]:
            acc = acc + r[0].astype(F32)
        refs[-1][...] = acc.astype(out_dtype)

    grid_spec = pltpu.PrefetchScalarGridSpec(
        num_scalar_prefetch=1, grid=(rows // tr,),
        in_specs=[pl.BlockSpec((1, tr, cols), lambda t, idx, n=n: (idx[n], t, 0)) for n in range(len(parts))],
        out_specs=pl.BlockSpec((tr, cols), lambda t, idx: (t, 0)))
    return pl.pallas_call(
        body, name=name, grid_spec=grid_spec, out_shape=jax.ShapeDtypeStruct((rows, cols), out_dtype),
        compiler_params=_params(dimension_semantics=("arbitrary",)),
    )(index, *(a for a, _ in parts))


def _place():
    return lax.axis_index("x"), lax.axis_index("y"), lax.axis_index("c")


def _any_specs(n):
    return [pl.BlockSpec(memory_space=pl.ANY)] * n


def _all_gather_call(xs, name):
    n = len(xs)

    def body(*refs):
        for phase in _gather_phases(refs[:n], refs[n:2 * n], *refs[2 * n:]):
            phase()

    return pl.pallas_call(
        body, name=name, in_specs=_any_specs(n), out_specs=_any_specs(n),
        out_shape=_gather_shapes(xs), scratch_shapes=_gather_sems(n),
    )(*xs)


def _gather_shapes(xs):
    return [jax.ShapeDtypeStruct((NDEV,) + x.shape, x.dtype) for x in xs]


def _gather_sems(n):
    return [pltpu.SemaphoreType.DMA((7 * n,)), pltpu.SemaphoreType.DMA((7 * n,)), pltpu.SemaphoreType.DMA((n,))]


def _gather_phases(x_refs, out_refs, send_sems, recv_sems, local_sems):
    n = len(x_refs)
    ax, ay, ac = _place()
    me, sibling = (ax, ay, ac), (ax, ay, 1 - ac)
    chips = [(1 - ax, ay), (ax, 1 - ay), (1 - ax, 1 - ay)]

    def copy(a, k, block, to, src=None):
        slot = out_refs[a].at[4 * block[0] + 2 * block[1] + block[2]]
        return pltpu.make_async_remote_copy(
            src_ref=slot if src is None else src, dst_ref=slot,
            send_sem=send_sems.at[7 * a + k], recv_sem=recv_sems.at[7 * a + k], device_id=to, device_id_type=MESH)

    local = [pltpu.make_async_copy(x_refs[a], out_refs[a].at[4 * ax + 2 * ay + ac], local_sems.at[a]) for a in range(n)]
    first = []
    for a in range(n):
        first.append(copy(a, 0, me, sibling, src=x_refs[a]))
        first += [copy(a, 1 + j, me, (*chip, ac), src=x_refs[a]) for j, chip in enumerate(chips)]
    passed = [copy(a, 4 + j, (*chip, ac), sibling) for j, chip in enumerate(chips) for a in range(n)]

    def send():
        for cp in local + first:
            cp.start()

    def forward():
        for j, chip in enumerate(chips):
            for a in range(n):
                copy(a, 1 + j, (*chip, ac), me).wait_recv()
                passed[j * n + a].start()

    def finish():
        for a in range(n):
            copy(a, 0, sibling, me).wait_recv()
            for j, chip in enumerate(chips):
                copy(a, 4 + j, (*chip, 1 - ac), me).wait_recv()
        for cp in first + passed:
            cp.wait_send()
        for cp in local:
            cp.wait()

    return send, forward, finish


def _swap_sibling_call(xs, name):
    n = len(xs)

    def body(*refs):
        for phase in _sibling_swap_phases(refs[:n], refs[n:2 * n], *refs[2 * n:]):
            phase()

    return pl.pallas_call(
        body, name=name, in_specs=_any_specs(n), out_specs=_any_specs(n),
        out_shape=_sibling_swap_shapes(xs), scratch_shapes=_sibling_swap_sems(n),
    )(*xs)


def _sibling_swap_shapes(xs):
    return [jax.ShapeDtypeStruct((4,) + x.shape[1:], x.dtype) for x in xs]


def _sibling_swap_sems(n):
    return [pltpu.SemaphoreType.DMA((n,)), pltpu.SemaphoreType.DMA((n,))]


def _sibling_swap_phases(x_refs, out_refs, send_sems, recv_sems):
    ax, ay, ac = _place()
    sibling = (ax, ay, 1 - ac)

    def send():
        for a, (x_ref, out_ref) in enumerate(zip(x_refs, out_refs)):
            for chip in range(4):
                pltpu.make_async_remote_copy(src_ref=x_ref.at[2 * chip + 1 - ac], dst_ref=out_ref.at[chip],
                                             send_sem=send_sems.at[a], recv_sem=recv_sems.at[a],
                                             device_id=sibling, device_id_type=MESH).start()

    def finish():
        for a, (x_ref, out_ref) in enumerate(zip(x_refs, out_refs)):
            pltpu.make_async_remote_copy(src_ref=x_ref.at[pl.ds(0, 4)], dst_ref=out_ref, send_sem=send_sems.at[a],
                                         recv_sem=recv_sems.at[a], device_id=sibling, device_id_type=MESH).wait()

    return send, finish


def _chip_swap_shapes(xs):
    return [jax.ShapeDtypeStruct((3,) + x.shape[1:], x.dtype) for x in xs]


def _chip_swap_sems(n):
    return [pltpu.SemaphoreType.DMA((3 * n,)), pltpu.SemaphoreType.DMA((3 * n,))]


def _chip_swap_phases(x_refs, out_refs, send_sems, recv_sems):
    ax, ay, ac = _place()
    chips = [(1 - ax, ay), (ax, 1 - ay), (1 - ax, 1 - ay)]
    copies = [pltpu.make_async_remote_copy(src_ref=x_refs[a].at[2 * cx + cy], dst_ref=out_refs[a].at[j],
                                           send_sem=send_sems.at[3 * a + j], recv_sem=recv_sems.at[3 * a + j],
                                           device_id=(cx, cy, ac), device_id_type=MESH)
              for a in range(len(x_refs)) for j, (cx, cy) in enumerate(chips)]

    def send():
        for cp in copies:
            cp.start()

    def finish():
        for cp in copies:
            cp.wait()

    return send, finish


def _reduce_begin(gs, name):
    got = _swap_sibling_call(gs, name + "_d2d")
    return got, _pair_sum_call(gs, got, lax.axis_index("c"), name + "_pair")


def _reduce_end(gs, got, from_chips, name):
    ax, ay, ac = _place()
    return _total_sum_call(gs, got, from_chips, 4 * ax + 2 * ay + ac, 2 * ax + ay, name + "_total")


SMALL = ("norm1_g", "conv_w", "a_log", "dt_bias", "dn_out_g", "sb_q_g", "sb_k_g", "sg_v_g", "sg_w", "sg_b", "norm2_g")
WEIGHTS = ("norm1_g", "w_in", "conv_w", "a_log", "dt_bias", "dn_out_g", "sb_q_g", "sb_k_g", "sg_v_g", "sg_w", "sg_b",
           "w_out", "norm2_g", "w_ff1", "w_ff2")
SMALL_SHAPE = {"norm1_g": (D,), "conv_w": (4, 3 * DN_W), "a_log": (NH,), "dt_bias": (NH,), "dn_out_g": (128,), "sb_q_g": (64,),
               "sb_k_g": (64,), "sg_v_g": (SG_W,), "sg_w": (NH, 128, 128), "sg_b": (NH, 128), "norm2_g": (D,)}


def _size(shape):
    n = 1
    for s in shape:
        n *= s
    return n


def _to_rows(flat, multiple):
    pad = (-flat.shape[0]) % (LANES * multiple)
    return jnp.pad(flat, (0, pad)).reshape(-1, LANES)


def _conv_by_pair(conv):
    return conv.reshape(4, 3, 2, 256).transpose(0, 2, 1, 3).reshape(4, 3 * DN_W)


def kernel(x, norm1_g, w_in, conv_w, a_log, dt_bias, dn_out_g, sb_q_g, sb_k_g, sg_v_g, sg_w, sg_b, w_out, norm2_g, w_ff1, w_ff2, loss_target, m_norm1_g, m_w_in, m_conv_w, m_a_log, m_dt_bias, m_dn_out_g, m_sb_q_g, m_sb_k_g, m_sg_v_g, m_sg_w, m_sg_b, m_w_out, m_norm2_g, m_w_ff1, m_w_ff2, v_norm1_g, v_w_in, v_conv_w, v_a_log, v_dt_bias, v_dn_out_g, v_sb_q_g, v_sb_k_g, v_sg_v_g, v_sg_w, v_sg_b, v_w_out, v_norm2_g, v_w_ff1, v_w_ff2):
    given = dict(norm1_g=norm1_g, w_in=w_in, conv_w=conv_w, a_log=a_log, dt_bias=dt_bias, dn_out_g=dn_out_g, sb_q_g=sb_q_g,
                 sb_k_g=sb_k_g, sg_v_g=sg_v_g, sg_w=sg_w, sg_b=sg_b, w_out=w_out, norm2_g=norm2_g, w_ff1=w_ff1, w_ff2=w_ff2)
    mom = dict(norm1_g=m_norm1_g, w_in=m_w_in, conv_w=m_conv_w, a_log=m_a_log, dt_bias=m_dt_bias, dn_out_g=m_dn_out_g,
               sb_q_g=m_sb_q_g, sb_k_g=m_sb_k_g, sg_v_g=m_sg_v_g, sg_w=m_sg_w, sg_b=m_sg_b, w_out=m_w_out, norm2_g=m_norm2_g,
               w_ff1=m_w_ff1, w_ff2=m_w_ff2)
    var = dict(norm1_g=v_norm1_g, w_in=v_w_in, conv_w=v_conv_w, a_log=v_a_log, dt_bias=v_dt_bias, dn_out_g=v_dn_out_g,
               sb_q_g=v_sb_q_g, sb_k_g=v_sb_k_g, sg_v_g=v_sg_v_g, sg_w=v_sg_w, sg_b=v_sg_b, w_out=v_w_out, norm2_g=v_norm2_g,
               w_ff1=v_w_ff1, w_ff2=v_w_ff2)
    B, T, _ = x.shape
    M = B * T
    ax, ay, ac = _place()
    me = 4 * ax + 2 * ay + ac
    table_fwd, table_back = _row_tables()

    send = []
    for l in range(2):
        w_in_t = jnp.pad(w_in[l].T, ((0, IN_SHARD_PAD - IN_SHARD), (0, 0)))
        send.append([w_in_t.astype(BF16), w_out[l].astype(BF16), w_ff1[l].astype(BF16), w_ff2[l].astype(BF16)])
    first_in, conv_rows = _all_gather_call([send[0][0], _to_rows(conv_w.reshape(-1), 8)], "gather_first")
    conv_full = conv_rows.reshape(NDEV, -1)[:, :conv_w.size].reshape(NDEV, 2, 4, -1).transpose(1, 2, 0, 3).reshape(2, 4, 3 * DN_W)
    gathered = [[first_in, None, None, None], [None] * 4]

    pad_vec = lambda v: jnp.zeros((1, LANES), F32).at[0, :v.shape[0]].set(v)
    layer = []
    for l in range(2):
        layer.append(dict(
            g1=norm1_g[l].reshape(1, D), g2=norm2_g[l].reshape(1, D), conv=_conv_by_pair(conv_full[l]),
            a_log=pad_vec(a_log[l]), dt_bias=pad_vec(dt_bias[l]), dn_g=dn_out_g[l].reshape(1, LANES),
            sb_qg=jnp.tile(sb_q_g[l], 2).reshape(1, LANES), sb_kg=jnp.tile(sb_k_g[l], 2).reshape(1, LANES),
            sg_g=sg_v_g[l].reshape(1, SG_W), sg_w=sg_w[l], sg_bias=jnp.repeat(sg_b[l].T, 64, axis=1)))

    cur = x.reshape(M, D)
    saved = []
    for l, p in enumerate(layer):
        p["wt"] = _row_perm_call(gathered[l][0].reshape(NDEV * IN_SHARD_PAD, D), table_fwd, "pack_w_in")
        p_dn, p_sb, p_sg, p_ab, h = _in_proj_call(cur, p["g1"], p["wt"])
        mix, dn_kept, arrived = _dn_fwd_call(p_dn, p_ab, p["conv"], p["a_log"], p["dt_bias"], p["dn_g"], B, T,
                                             gather=send[0][1:] + send[1][:1] if l == 0 else [])
        if l == 0:
            gathered[0][1:], gathered[1][0] = list(arrived[:3]), arrived[3]
        p["w_out"], p["w1"], p["w2"] = gathered[l][1].reshape(D, D), gathered[l][2], gathered[l][3].reshape(DFF, D)
        mix, sb_carries, arrived = _sb_fwd_call(p_sb, mix, p["sb_qg"], p["sb_kg"], B, T, gather=send[1][1:] if l == 0 else [])
        if l == 0:
            gathered[1][1:] = list(arrived)
        mix = _sg_fwd_call(p_sg, mix, p["sg_g"], p["sg_w"], p["sg_bias"], B, T)
        x1 = _out_proj_call(mix, p["w_out"], cur)
        x2 = _ffn_fwd_call(x1, p["g2"], p["w1"], p["w2"])
        saved.append(dict(x0=cur, p_dn=p_dn, p_sb=p_sb, p_sg=p_sg, p_ab=p_ab, h=h, mix=mix, x1=x1, dn_kept=dn_kept, sb_carries=sb_carries))
        cur = x2
    loss_part, dy = _loss_call(cur, loss_target.reshape(M, D))
    loss = lax.psum(loss_part[0, 0], ("x", "y", "c"))

    big_grads = [[None] * 4, [None] * 4]
    small_grads = {n: [None, None] for n in SMALL}
    for l in (1, 0):
        p, s = layer[l], saved[l]
        (dx1, da, r, h2, dg2), got1 = _ffn_bwd_call(s["x1"], dy, p["g2"], p["w1"], p["w2"], swap=big_grads[1] if l == 0 else ())
        big_grads[l][2] = _mm_tn_call(h2, da, "grad_w_ff1", col_shards=True)
        big_grads[l][3] = _mm_tn_call(r, dy, "grad_w_ff2").reshape(NDEV, FF_SHARD, D)
        dmix = _mm_nt_call(dx1, p["w_out"], "dmix")
        big_grads[l][1] = _mm_tn_call(s["mix"], dx1, "grad_w_out").reshape(NDEV, D // NDEV, D)
        if l == 0:
            got0, sums0 = _reduce_begin(big_grads[0][1:], "reduce_early0")
            early_sums = list(_pair_sum_call(big_grads[1], got1, ac, "reduce_early1_pair")) + list(sums0)
        (d_dn, d_ab, dcw, dalog, ddtb, ddn_g), early_from = _dn_bwd_call(
            s["p_dn"], s["p_ab"], dmix, s["dn_kept"], p["conv"], p["a_log"], p["dt_bias"], p["dn_g"], B, T,
            swap=early_sums if l == 0 else ())
        d_sb, dqg, dkg = _sb_bwd_call(s["p_sb"], dmix, s["sb_carries"], p["sb_qg"], p["sb_kg"], B, T)
        d_sg, dsg_g, dsg_w, dsg_b = _sg_bwd_call(s["p_sg"], dmix, p["sg_g"], p["sg_w"], p["sg_bias"], B, T)
        dsections = (d_dn, d_sb, d_sg, d_ab)
        dwt = _in_proj_grad_call(dsections, s["h"])
        big_grads[l][0] = _row_perm_call(dwt, table_back, "unpack_grad_w_in").reshape(NDEV, IN_SHARD_PAD, D)
        if l == 0:
            last = big_grads[0][:1]
            last_got, last_sums = _reduce_begin(last, "reduce_last")
        (dy, dg1), last_from = _in_proj_bwd_call(dsections, p["wt"], s["x0"], p["g1"], dx1, swap=last_sums if l == 0 else ())
        for n, val in (("norm1_g", dg1[0]), ("conv_w", dcw.transpose(1, 0, 2).reshape(4, 3 * DN_W)), ("a_log", dalog[0, :NH]),
                       ("dt_bias", ddtb[0, :NH]), ("dn_out_g", ddn_g[0]), ("sb_q_g", dqg[0, :64]), ("sb_k_g", dkg[0, :64]),
                       ("sg_v_g", dsg_g[0]), ("sg_w", dsg_w), ("sg_b", dsg_b[:, :NH].T), ("norm2_g", dg2[0])):
            small_grads[n][l] = val
    grad_x = dy.reshape(B, T, D)

    mine0 = _reduce_end(last, last_got, last_from, "reduce_last")
    mine1 = (_reduce_end(big_grads[1], got1, early_from[:4], "reduce_early1")
             + _reduce_end(big_grads[0][1:], got0, early_from[4:], "reduce_early0"))
    grads = {"w_in": jnp.stack([mine0[0][:IN_SHARD].T, mine1[0][:IN_SHARD].T]), "w_out": jnp.stack([mine1[4], mine1[1]]),
             "w_ff1": jnp.stack([mine1[5], mine1[2]]), "w_ff2": jnp.stack([mine1[6], mine1[3]])}
    small_flat = jnp.concatenate([jnp.stack(small_grads[n]).reshape(-1) for n in SMALL])
    everyone, = _all_gather_call([_to_rows(small_flat, 8)], "gather_small_grads")
    small_sum = _sum_call([(everyone, k) for k in range(NDEV)], F32, "sum_small_grads").reshape(-1)
    off = 0
    for n in SMALL:
        sz = 2 * _size(SMALL_SHAPE[n])
        grads[n] = small_sum[off:off + sz].reshape((2,) + SMALL_SHAPE[n])
        off += sz
    cshard = conv_w.shape[-1]
    grads["conv_w"] = lax.dynamic_slice_in_dim(grads["conv_w"], me * cshard, cshard, axis=2)

    deltas, new_m, new_v = {}, {}, {}
    for n in WEIGHTS:
        deltas[n], new_m[n], new_v[n] = _adamw_call(given[n], grads[n], mom[n], var[n], "adamw_" + n)
    return (loss, grad_x, *[grads[n] for n in WEIGHTS], *[deltas[n] for n in WEIGHTS], *[new_m[n] for n in WEIGHTS],
            *[new_v[n] for n in WEIGHTS])
```

```python
import functools

import numpy as np

import jax
import jax.numpy as jnp
from jax import lax
from jax.experimental import pallas as pl
from jax.experimental.pallas import tpu as pltpu

F32, BF16 = jnp.float32, jnp.bfloat16
EPS = 1e-6
LANES = 128
D = 1024
DFF = 4096
NH = 4
DN_W, SB_W, SG_W = 512, 256, 256
IN_DIM = 3336
NDEV = 8
IN_SHARD = IN_DIM // NDEV
IN_SHARD_PAD = 432
FF_SHARD = DFF // NDEV
DN_OFF, SB_OFF, SG_OFF, AB_OFF, NPACK = 0, 2048, 2816, 3328, 3456
SECTIONS = ((DN_OFF, 2048), (SB_OFF, 768), (SG_OFF, 512), (AB_OFF, 128))
SB_SCALE = 64 ** -0.5
DN_SCALE = 128 ** -0.5
VMEM_LIMIT = 56 * 1024 * 1024
VMEM_LIMIT_MAX = 62 * 1024 * 1024
ADAM_LR, ADAM_B1, ADAM_B2, ADAM_EPS, ADAM_WD, ADAM_STEP = 0.001, 0.9, 0.999, 1e-08, 0.01, 10
MESH = pl.DeviceIdType.MESH


def _iota(shape, dim):
    return lax.broadcasted_iota(jnp.int32, shape, dim)


def _params(**kw):
    return pltpu.CompilerParams(vmem_limit_bytes=VMEM_LIMIT, **kw)


NN, NT, TN = ((1,), (0,)), ((1,), (1,)), ((0,), (0,))


def _mm(a, b, dims):
    return lax.dot_general(a.astype(BF16), b.astype(BF16), (dims, ((), ())), preferred_element_type=F32)


def _plain(a, b, dims):
    return (a.T if dims == TN else a), (b.T if dims == NT else b)


def _mmx(a, b, dims):
    return _mm(*_plain(a, b, dims), NN)


@jax.custom_vjp
def _dot(a, b):
    return _mmx(a, b, NN)


def _dot_fwd(a, b):
    return _dot(a, b), (a, b)


def _dot_bwd(res, g):
    a, b = res
    return _mmx(g, b, NT).astype(a.dtype), _mmx(a, g, TN).astype(b.dtype)


_dot.defvjp(_dot_fwd, _dot_bwd)


def _split(x):
    hi = x.astype(BF16)
    return hi, (x - hi.astype(F32)).astype(BF16)


def _mm2(a, b):
    ah, al = _split(a)
    bh = b.astype(BF16)
    mm = lambda x, y: jnp.dot(x, y, preferred_element_type=F32)
    return mm(ah, bh) + mm(al, bh)


def _mm_ones(ones, x, ones_left):
    hi, lo = _split(x)
    mm = (lambda t: jnp.dot(ones, t, preferred_element_type=F32)) if ones_left else \
         (lambda t: jnp.dot(t, ones, preferred_element_type=F32))
    return mm(hi) + mm(lo)


def _pair_ones(kind, transposed):
    row, col = _iota((128, 128), 0), _iota((128, 128), 1)
    m = (row // 64) == (col // 64)
    if kind == "running":
        m = jnp.logical_and(m, (col >= row) if transposed else (col <= row))
    return jnp.where(m, 1.0, 0.0).astype(BF16)


@functools.partial(jax.custom_vjp, nondiff_argnums=(0,))
def _chunk_sum(kind, x):
    return _mm_ones(_pair_ones(kind, False), x, True)


def _chunk_sum_fwd(kind, x):
    return _chunk_sum(kind, x), None


def _chunk_sum_bwd(kind, _, g):
    return (_mm_ones(_pair_ones(kind, True), g, True),)


_chunk_sum.defvjp(_chunk_sum_fwd, _chunk_sum_bwd)


def _tri_ones(n, transposed):
    row, col = _iota((n, n), 0), _iota((n, n), 1)
    return jnp.where((row < col) if transposed else (row > col), 1.0, 0.0).astype(BF16)


def _sigmoid(x):
    return jax.nn.sigmoid(x)


def _silu(x):
    return x * _sigmoid(x)


def _softplus(x):
    return jnp.maximum(x, 0.0) + jnp.log1p(jnp.exp(-jnp.abs(x)))


def _gelu(x):
    return 0.5 * x * (1.0 + jnp.tanh(0.7978845608028654 * (x + 0.044715 * (x * x * x))))


def _rms(x, gain):
    return x * lax.rsqrt(jnp.mean(x * x, axis=-1, keepdims=True) + EPS) * gain


SUBLANES = 8


def _shift_down_impl(x, k):
    y = pltpu.roll(x, k, 0)
    top = jnp.where(_iota((SUBLANES, x.shape[1]), 0) >= k, y[:SUBLANES], 0.0)
    return jnp.concatenate([top, y[SUBLANES:]], axis=0)


def _shift_up_impl(x, k):
    n = x.shape[0]
    y = pltpu.roll(x, n - k, 0)
    bottom = jnp.where(_iota((SUBLANES, x.shape[1]), 0) < SUBLANES - k, y[n - SUBLANES:], 0.0)
    return jnp.concatenate([y[:n - SUBLANES], bottom], axis=0)


@functools.partial(jax.custom_vjp, nondiff_argnums=(1,))
def _shift_down(x, k):
    return _shift_down_impl(x, k)


def _shift_down_fwd(x, k):
    return _shift_down_impl(x, k), None


def _shift_down_bwd(k, _, g):
    return (_shift_up_impl(g, k),)


_shift_down.defvjp(_shift_down_fwd, _shift_down_bwd)


def _lane_pick(x, idx):
    return jnp.sum(jnp.where(_iota(x.shape, 1) == idx, x, 0.0), axis=-1, keepdims=True)


def _dn_conv(x, w0, w1, w2, w3, l2_scale):
    y = _silu(w3 * x + w2 * _shift_down(x, 1) + w1 * _shift_down(x, 2) + w0 * _shift_down(x, 3))
    if l2_scale is None:
        return y
    return y * lax.rsqrt(jnp.sum(y * y, axis=-1, keepdims=True) + EPS) * l2_scale


def _dn_gates(ab, a_log, dt_bias):
    lane = _iota((1, LANES), 1)
    g = -jnp.exp(a_log) * _softplus(ab + dt_bias)
    return jnp.where(lane < NH, g, jnp.where(lane < 2 * NH, _sigmoid(ab), 0.0))


def _same_head(shape):
    return (_iota(shape, 0) < LANES) == (_iota(shape, 1) < LANES)


def _bd(r2):
    return jnp.where(_same_head((2 * LANES, 2 * LANES)), jnp.concatenate([r2, r2], axis=0), 0.0)


def _bd_t(y2):
    t = y2.T
    return jnp.where(_same_head((2 * LANES, 2 * LANES)), jnp.concatenate([t, t], axis=1), 0.0)


def _pair_prod(kind, a2, b2, mm):
    if kind == NN:
        return mm(a2, _bd(b2))
    if kind == NT:
        return mm(a2, _bd_t(b2))
    full = mm(a2.T, b2)
    return jnp.concatenate([full[:LANES, :LANES], full[LANES:, LANES:]], axis=1)


_MM1 = lambda x, y: _mm(x, y, NN)


def _pair_vjp_rule(kind, a2, b2, g, mm):
    if kind == NN:
        return _pair_prod(NT, g, b2, mm), _pair_prod(TN, a2, g, mm)
    if kind == NT:
        return _pair_prod(NN, g, b2, mm), _pair_prod(TN, g, a2, mm)
    return _pair_prod(NT, b2, g, mm), _pair_prod(NN, a2, g, mm)


@functools.partial(jax.custom_vjp, nondiff_argnums=(0,))
def _pdot(kind, a2, b2):
    return _pair_prod(kind, a2, b2, _MM1)


def _pdot_fwd(kind, a2, b2):
    return _pdot(kind, a2, b2), (a2, b2)


def _pdot_bwd(kind, res, g):
    return _pair_vjp_rule(kind, *res, g, _MM1)


_pdot.defvjp(_pdot_fwd, _pdot_bwd)


def _unit_lower_inverse(lower):
    n = lower.shape[0]
    nk = -lower
    inv = jnp.where(_iota(lower.shape, 0) == jnp.bitwise_and(_iota(lower.shape, 1), n - 1), 1.0, 0.0) + nk
    for _ in range(5):
        nk = _pair_prod(NN, nk, nk, _MM1)
        inv = inv + _pair_prod(NN, inv, nk, _MM1)
    return inv


@jax.custom_vjp
def _solve_with(lower, inv, rhs):
    return _pair_prod(NN, inv, rhs, _mm2)


def _solve_with_fwd(lower, inv, rhs):
    x = _pair_prod(NN, inv, rhs, _mm2)
    return x, (inv, x)


def _solve_with_bwd(res, g):
    inv, x = res
    d_rhs = _pair_prod(TN, inv, g, _mm2)
    return -_pair_prod(NT, d_rhs, x, _MM1), jnp.zeros_like(inv), d_rhs


_solve_with.defvjp(_solve_with_fwd, _solve_with_bwd)


def _dn_local(q, k, v, g, beta, inv=None):
    shape = (LANES, 2 * LANES)
    row, col = _iota(shape, 0), jnp.bitwise_and(_iota(shape, 1), LANES - 1)
    same = (row // 64) == (col // 64)
    tri_incl = jnp.logical_and(same, col <= row)
    tri_strict = jnp.logical_and(same, col < row)
    first = row < 64
    gc = _chunk_sum("running", g)
    gl = _chunk_sum("total", g)
    diff = gc - jnp.concatenate([gc[:, :LANES].T, gc[:, LANES:].T], axis=1)
    decay = jnp.where(tri_incl, jnp.exp(jnp.where(tri_incl, diff, 0.0)), 0.0)
    egc = jnp.exp(gc)
    lower = jnp.where(tri_strict, beta * _pdot(NT, k, k) * decay, 0.0)
    if inv is None:
        inv = _unit_lower_inverse(lower)
    u_val = _solve_with(lower, inv, v * beta)
    w_dec = _solve_with(lower, inv, k * (beta * egc))
    qk = jnp.where(tri_incl, _pdot(NT, q, k) * decay, 0.0)
    q_dec = q * egc
    k_dec = k * jnp.exp(gl - gc)
    cd1 = jnp.exp(jnp.sum(jnp.where(first, g, 0.0), axis=0, keepdims=True))
    cd2 = jnp.exp(jnp.sum(jnp.where(first, 0.0, g), axis=0, keepdims=True))
    return (u_val, w_dec, qk, q_dec, k_dec, cd1, cd2), inv


def _dn_state(u_val, w_dec, qk, q_dec, k_dec, cd1, cd2, s0):
    first = _iota((LANES, 2 * LANES), 0) < 64
    u1 = u_val - _pdot(NN, w_dec, s0)
    s1 = s0 * cd1 + _pdot(TN, jnp.where(first, k_dec, 0.0), u1)
    u2 = u_val - _pdot(NN, w_dec, s1)
    u_new = jnp.where(first, u1, u2)
    s2 = s1 * cd2 + _pdot(TN, jnp.where(first, 0.0, k_dec), u_new)
    o = jnp.where(first, _pdot(NN, q_dec, s0), _pdot(NN, q_dec, s1)) + _pdot(NN, qk, u_new)
    return o, s2


def _dn_post(o, z, gain):
    return _rms(o, gain) * _silu(z)


_DN_L2 = (DN_SCALE, 1.0, None)
DN_HPS = 2
DN_BLK = 4 * DN_HPS * LANES
_DN_COLS = tuple(slice(i * LANES, (i + 1) * LANES) for i in range(DN_HPS))


def _dn_in_cols(s, i):
    return slice((s * DN_HPS + i) * LANES, (s * DN_HPS + i + 1) * LANES)


def _dn_taps(cw_ref, s, i):
    return tuple(cw_ref[t:t + 1, _dn_in_cols(s, i)] for t in range(4))


def _pair_rows(n):
    return pl.ds(pl.multiple_of(n * 128, 128), 128)


def _dn_gate_rows(gate, hp):
    head_a = _iota((1, DN_HPS * LANES), 1) < LANES
    h = DN_HPS * hp
    return (jnp.where(head_a, _lane_pick(gate, h), _lane_pick(gate, h + 1)),
            jnp.where(head_a, _lane_pick(gate, NH + h), _lane_pick(gate, NH + h + 1)))


def _dn_gate_cols(dg, db, hp):
    head_a = _iota((1, DN_HPS * LANES), 1) < LANES
    lane = _iota((1, LANES), 1)
    h = DN_HPS * hp
    out = 0.0
    for t, first in ((dg, h), (db, NH + h)):
        out = out + jnp.where(lane == first, jnp.sum(jnp.where(head_a, t, 0.0), axis=-1, keepdims=True), 0.0)
        out = out + jnp.where(lane == first + 1, jnp.sum(jnp.where(head_a, 0.0, t), axis=-1, keepdims=True), 0.0)
    return out


def _dn_in_specs(T):
    vec = pl.BlockSpec((1, LANES), lambda b, h: (0, 0))
    return [pl.BlockSpec((T, DN_BLK), lambda b, h: (b, h)), pl.BlockSpec((T, LANES), lambda b, h: (b, 0)),
            pl.BlockSpec((4, 3 * DN_HPS * LANES), lambda b, h: (0, h)), vec, vec, vec]


def _dn_fwd_call(proj_dn, proj_ab, conv_w, a_log, dt_bias, gain, B, T, gather=()):
    npair = T // 128
    ng = len(gather)
    nsteps = B * (NH // DN_HPS)

    def body(*refs):
        x_ref, ab_ref, cw_ref, alog_ref, dtb_ref, gain_ref = refs[:6]
        out_ref, q_s, k_s, v_s, o_s, gate_s, st_s, inv_s = refs[6 + ng:14 + ng]
        step_id = pl.program_id(0) * (NH // DN_HPS) + pl.program_id(1)
        if ng:
            send, forward, finish = _gather_phases(refs[6:6 + ng], refs[14 + ng:14 + 2 * ng], *refs[14 + 2 * ng:])
            pl.when(step_id == 0)(send)
            pl.when(step_id == nsteps - 1)(forward)
        hp = pl.program_id(1)
        for i, cs in enumerate(_DN_COLS):
            for s, (x_s, l2) in enumerate(zip((q_s, k_s, v_s), _DN_L2)):
                x_s[:, cs] = _dn_conv(x_ref[:, _dn_in_cols(s, i)], *_dn_taps(cw_ref, s, i), l2)
        gate_s[...] = _dn_gates(ab_ref[...], alog_ref[...], dtb_ref[...])

        def local_of(pair):
            r = _pair_rows(pair)
            loc, inv = _dn_local(q_s[r, :], k_s[r, :], v_s[r, :], *_dn_gate_rows(gate_s[r, :], hp))
            inv_s[0, 0, pair] = inv
            return loc

        def state_of(n, loc, state):
            st_s[0, 0, n] = state
            o, s2 = _dn_state(*loc, state)
            o_s[_pair_rows(n), :] = o
            return s2

        def step(n, carry):
            loc, state = carry
            return local_of(n + 1), state_of(n, loc, state)

        loc, state = lax.fori_loop(0, npair - 1, step, (local_of(0), jnp.zeros((LANES, DN_HPS * LANES), F32)))
        state_of(npair - 1, loc, state)
        for i, cs in enumerate(_DN_COLS):
            out_ref[:, cs] = _dn_post(o_s[:, cs], x_ref[:, _dn_in_cols(3, i)], gain_ref[...])
        if ng:
            pl.when(step_id == nsteps - 1)(finish)

    kept_specs, kept_shapes = _dn_kept(B, T)
    outs = pl.pallas_call(
        body, name="dn_fwd", grid=(B, NH // DN_HPS), in_specs=_dn_in_specs(T) + _any_specs(ng),
        out_specs=[pl.BlockSpec((T, DN_HPS * LANES), lambda b, h: (b, h), pipeline_mode=pl.Buffered(1))] + kept_specs + _any_specs(ng),
        out_shape=[jax.ShapeDtypeStruct((B * T, D), F32)] + kept_shapes + _gather_shapes(gather),
        scratch_shapes=_gather_sems(ng) if ng else [],
        compiler_params=_params(dimension_semantics=("arbitrary", "arbitrary")),
    )(proj_dn, proj_ab, conv_w, a_log, dt_bias, gain, *gather)
    return outs[0], outs[1:8], outs[8:]


def _dn_kept(B, T):
    one = pl.Buffered(1)
    npair, pairs = T // 128, NH // DN_HPS
    wide = pl.BlockSpec((T, DN_HPS * LANES), lambda b, h: (b, h), pipeline_mode=one)
    per_pair = pl.BlockSpec((1, 1, npair, LANES, DN_HPS * LANES), lambda b, h: (b, h, 0, 0, 0), pipeline_mode=one)
    specs = [wide] * 4 + [pl.BlockSpec((T, LANES), lambda b, h: (b, h), pipeline_mode=one)] + [per_pair] * 2
    shapes = ([jax.ShapeDtypeStruct((B * T, DN_W), F32)] * 4 + [jax.ShapeDtypeStruct((B * T, pairs * LANES), F32)]
              + [jax.ShapeDtypeStruct((B, pairs, npair, LANES, DN_HPS * LANES), F32)] * 2)
    return specs, shapes


def _dn_bwd_call(proj_dn, proj_ab, dmix, kept, conv_w, a_log, dt_bias, gain, B, T, swap=()):
    npair = T // 128
    ns = len(swap)
    nsteps = B * (NH // DN_HPS)

    def body(*refs):
        x_ref, ab_ref, cw_ref, alog_ref, dtb_ref, gain_ref, do_ref, q_s, k_s, v_s, o_ref, gate_s, st_s, inv_s = refs[:14]
        dx_ref, dab_ref, dcw_ref, dalog_ref, ddtb_ref, dgain_ref = refs[14 + ns:20 + ns]
        dgate_s, do_s = refs[20 + 2 * ns:22 + 2 * ns]
        b_i, hp = pl.program_id(0), pl.program_id(1)
        step_id = b_i * (NH // DN_HPS) + hp
        if ns:
            send, finish = _chip_swap_phases(refs[14:14 + ns], refs[20 + ns:20 + 2 * ns], *refs[22 + 2 * ns:])
            pl.when(step_id == 0)(send)

        def pair_in(r):
            return (q_s[r, :], k_s[r, :], v_s[r, :]) + _dn_gate_rows(gate_s[r, :], hp)

        zero_state = jnp.zeros((LANES, DN_HPS * LANES), F32)

        @pl.when(jnp.logical_and(b_i == 0, hp == 0))
        def _():
            dcw_ref[...] = jnp.zeros_like(dcw_ref)
            dalog_ref[...] = jnp.zeros_like(dalog_ref)
            ddtb_ref[...] = jnp.zeros_like(ddtb_ref)
            dgain_ref[...] = jnp.zeros_like(dgain_ref)

        for i, cs in enumerate(_DN_COLS):
            zc = _dn_in_cols(3, i)
            _, post_vjp = jax.vjp(_dn_post, o_ref[:, cs], x_ref[:, zc], gain_ref[...])
            do, dz, dgain = post_vjp(do_ref[:, cs])
            dx_ref[:, zc] = dz
            do_s[:, cs] = do
            dgain_ref[...] += dgain

        wide_cols = lambda s: slice(s * DN_HPS * LANES, (s + 1) * DN_HPS * LANES)

        def back_step(nn, dstate):
            n = npair - 1 - nn
            r = _pair_rows(n)
            inv = inv_s[0, 0, n]
            local = lambda q, k, v, g, beta, inv=inv: _dn_local(q, k, v, g, beta, inv)[0]
            loc, local_vjp = jax.vjp(local, *pair_in(r))
            _, state_vjp = jax.vjp(_dn_state, *loc, st_s[0, 0, n])
            *dloc, ds0 = state_vjp((do_s[r, :], dstate))
            dq, dk, dv, dg, db = local_vjp(tuple(dloc))
            dx_ref[r, wide_cols(0)], dx_ref[r, wide_cols(1)], dx_ref[r, wide_cols(2)] = dq, dk, dv
            dgate_s[r, :] = _dn_gate_cols(dg, db, hp)
            return ds0

        lax.fori_loop(0, npair, back_step, zero_state)

        for i, cs in enumerate(_DN_COLS):
            h = DN_HPS * hp + i
            for s, l2 in enumerate(_DN_L2):
                xc = _dn_in_cols(s, i)
                _, conv_vjp = jax.vjp(functools.partial(_dn_conv, l2_scale=l2), x_ref[:, xc], *_dn_taps(cw_ref, s, i))
                dx, *dw = conv_vjp(dx_ref[:, xc])
                dx_ref[:, xc] = dx
                for t in range(4):
                    dcw_ref[h + 4 * s, t:t + 1, :] += dw[t]
        _, gate_vjp = jax.vjp(_dn_gates, ab_ref[...], alog_ref[...], dtb_ref[...])
        dab, dalog, ddtb = gate_vjp(dgate_s[...])
        dalog_ref[...] += dalog
        ddtb_ref[...] += ddtb

        @pl.when(hp == 0)
        def _():
            dab_ref[...] = jnp.zeros_like(dab_ref)

        dab_ref[...] += dab
        if ns:
            pl.when(step_id == nsteps - 1)(finish)

    M = B * T
    one = pl.Buffered(1)
    vec = pl.BlockSpec((1, LANES), lambda b, h: (0, 0))
    wide = [pltpu.VMEM((T, DN_HPS * LANES), F32)]
    vec_shape = jax.ShapeDtypeStruct((1, LANES), F32)
    outs = pl.pallas_call(
        body, name="dn_bwd", grid=(B, NH // DN_HPS),
        in_specs=_dn_in_specs(T) + [pl.BlockSpec((T, DN_HPS * LANES), lambda b, h: (b, h))] + _dn_kept(B, T)[0]
        + _any_specs(ns),
        out_specs=[pl.BlockSpec((T, DN_BLK), lambda b, h: (b, h), pipeline_mode=one), pl.BlockSpec((T, LANES), lambda b, h: (b, 0)),
                   pl.BlockSpec((12, 4, LANES), lambda b, h: (0, 0, 0)), vec, vec, vec] + _any_specs(ns),
        out_shape=[jax.ShapeDtypeStruct((M, 4 * DN_W), F32), jax.ShapeDtypeStruct((M, LANES), F32),
                   jax.ShapeDtypeStruct((12, 4, LANES), F32), vec_shape, vec_shape, vec_shape] + _chip_swap_shapes(swap),
        scratch_shapes=[pltpu.VMEM((T, LANES), F32)] + wide + (_chip_swap_sems(ns) if ns else []),
        compiler_params=pltpu.CompilerParams(vmem_limit_bytes=VMEM_LIMIT_MAX, dimension_semantics=("arbitrary", "arbitrary")),
    )(proj_dn, proj_ab, conv_w, a_log, dt_bias, gain, dmix, *kept, *swap)
    return outs[:6], outs[6:]


SBQ = 256


def _group_rms(x, gain):
    first = _iota(x.shape, 1) < 64
    sq = x * x
    ss_a = jnp.sum(jnp.where(first, sq, 0.0), axis=-1, keepdims=True)
    ss_b = jnp.sum(jnp.where(first, 0.0, sq), axis=-1, keepdims=True)
    ms = jnp.where(first, ss_a, ss_b) * (1.0 / 64)
    return x * lax.rsqrt(ms + EPS) * gain


def _sb_stack(q):
    first = _iota((1, LANES), 1) < 64
    return jnp.concatenate([jnp.where(first, q, 0.0), jnp.where(first, 0.0, q)], axis=0)


def _sb_fold(acc):
    return jnp.where(_iota((1, LANES), 1) < 64, acc[:SBQ], acc[SBQ:])


def _sb_logs(q2, k, diag):
    n = SBQ
    z = _mm(q2, k, ((1,), (1,))) * SB_SCALE
    ls_pos = jnp.minimum(z, 0.0) - jnp.log(1.0 + jnp.exp(-jnp.abs(z)))
    l1m = ls_pos - z
    if not diag:
        return ls_pos, l1m, None
    mask = _iota((2 * n, n), 1) < jnp.bitwise_and(_iota((2 * n, n), 0), n - 1)
    return ls_pos, jnp.where(mask, l1m, 0.0), mask


def _sb_weights(ls_pos, l1m, mask, carry):
    w = jnp.exp(ls_pos + (_mm_ones(_tri_ones(SBQ, False), l1m, False) + carry))
    return w if mask is None else jnp.where(mask, w, 0.0)


def _sb_block(q, k, v, carry, diag):
    ls_pos, l1m, mask = _sb_logs(_sb_stack(q), k, diag)
    w = _sb_weights(ls_pos, l1m, mask, carry)
    return _mm(w, v, ((1,), (0,))), carry + jnp.sum(l1m, axis=-1, keepdims=True), _sb_sum_as_rows(l1m)


SB_ROWS = 16


def _sb_sum_as_rows(l1m):
    ones = jnp.ones((SB_ROWS, SBQ), BF16)
    hi, lo = _split(l1m)
    mm = lambda t: lax.dot_general(ones, t, (NT, ((), ())), preferred_element_type=F32)
    return mm(hi) + mm(lo)


def _sb_rows_as_column(rows):
    pick = jnp.where(_iota((SB_ROWS, SBQ), 0) == 0, 1.0, 0.0).astype(BF16)
    hi = rows.astype(BF16)
    rest = rows - hi.astype(F32)
    mid = rest.astype(BF16)
    lo = (rest - mid.astype(F32)).astype(BF16)
    mm = lambda t: lax.dot_general(t, pick, (TN, ((), ())), preferred_element_type=F32)
    return mm(hi) + (mm(mid) + mm(lo))


def _sb_block_bwd(q, k, v, carry, diag, dpv, dcarry):
    q2 = _sb_stack(q)
    ls_pos, l1m, mask = _sb_logs(q2, k, diag)
    w = _sb_weights(ls_pos, l1m, mask, carry)
    dv = _mm(w, dpv, ((0,), (0,)))
    de = _mm(dpv, v, ((1,), (1,))) * w
    dl1m = jnp.dot(de.astype(BF16), _tri_ones(SBQ, True), preferred_element_type=F32) + dcarry
    if mask is not None:
        dl1m = jnp.where(mask, dl1m, 0.0)
    sig = jnp.exp(ls_pos)
    dz = (de * (1.0 - sig) - dl1m * sig) * SB_SCALE
    dq = _sb_fold(_mm(dz, k, ((1,), (0,))))
    return dq, _mm(dz, q2, ((0,), (0,))), dv, dcarry + jnp.sum(de, axis=-1, keepdims=True)


_SB_Q, _SB_K, _SB_V = (slice(i * LANES, (i + 1) * LANES) for i in range(3))


def _sb_fwd_call(proj_sb, mix, q_gain, k_gain, B, T, gather=()):
    nblk = T // SBQ
    ng = len(gather)
    nsteps = 2 * B

    def body(*refs):
        x_ref, qg_ref, kg_ref = refs[:3]
        out_ref, carry_ref = refs[4 + ng:6 + ng]
        q_s, k_s = refs[6 + 2 * ng:8 + 2 * ng]
        step_id = 2 * pl.program_id(0) + pl.program_id(1)
        if ng:
            send, forward, finish = _gather_phases(refs[4:4 + ng], refs[6 + ng:6 + 2 * ng], *refs[8 + 2 * ng:])
            pl.when(step_id == 0)(send)
            pl.when(step_id == nsteps - 1)(forward)
        q_s[...] = _group_rms(x_ref[:, _SB_Q], qg_ref[...])
        k_s[...] = _group_rms(x_ref[:, _SB_K], kg_ref[...])

        def qblock(i, _):
            ri = pl.ds(pl.multiple_of(i * SBQ, SBQ), SBQ)
            q = q_s[ri, :]

            def kblock(jj, c):
                j = i - 1 - jj
                rj = pl.ds(pl.multiple_of(j * SBQ, SBQ), SBQ)
                carry_ref[0, 0, i, j] = c[2]
                pv, carry, rows = _sb_block(q, k_s[rj, :], x_ref[rj, _SB_V], c[1], False)
                return c[0] + pv, carry, c[2] + rows

            on_diag = _sb_block(q, k_s[ri, :], x_ref[ri, _SB_V], jnp.zeros((2 * SBQ, 1), F32), True)
            acc, _c, _r = lax.fori_loop(0, i, kblock, on_diag)
            out_ref[ri, :] = _sb_fold(acc)
            return 0

        lax.fori_loop(0, nblk, qblock, 0)
        if ng:
            pl.when(step_id == nsteps - 1)(finish)

    vec = pl.BlockSpec((1, LANES), lambda b, p: (0, 0))
    outs = pl.pallas_call(
        body, name="sb_fwd", grid=(B, 2),
        in_specs=[pl.BlockSpec((T, 3 * LANES), lambda b, p: (b, p)), vec, vec, pl.BlockSpec(memory_space=pl.ANY)] + _any_specs(ng),
        out_specs=[pl.BlockSpec((T, LANES), lambda b, p: (b, DN_W // LANES + p)), _sb_carry_spec(nblk)] + _any_specs(ng),
        out_shape=[jax.ShapeDtypeStruct((B * T, D), F32), jax.ShapeDtypeStruct((B, 2, nblk, nblk, SB_ROWS, 2 * SBQ), F32)]
        + _gather_shapes(gather), input_output_aliases={3: 0},
        scratch_shapes=[pltpu.VMEM((T, LANES), F32)] * 2 + (_gather_sems(ng) if ng else []),
        compiler_params=_params(dimension_semantics=("arbitrary", "arbitrary")),
    )(proj_sb, q_gain, k_gain, mix, *gather)
    return outs[0], outs[1], outs[2:]


def _sb_carry_spec(nblk):
    return pl.BlockSpec((1, 1, nblk, nblk, SB_ROWS, 2 * SBQ), lambda b, p: (b, p, 0, 0, 0, 0))


def _sb_bwd_call(proj_sb, dmix, carries, q_gain, k_gain, B, T):
    nblk = T // SBQ

    def body(x_ref, qg_ref, kg_ref, do_ref, carry_ref, dx_ref, dqg_ref, dkg_ref, q_s, k_s, dq_s, dk_s, dv_s):
        b_i, p = pl.program_id(0), pl.program_id(1)
        qn, q_vjp = jax.vjp(_group_rms, x_ref[:, _SB_Q], qg_ref[...])
        kn, k_vjp = jax.vjp(_group_rms, x_ref[:, _SB_K], kg_ref[...])
        q_s[...], k_s[...] = qn, kn
        dk_s[...] = jnp.zeros_like(dk_s)
        dv_s[...] = jnp.zeros_like(dv_s)

        def qblock(i, _):
            ri = pl.ds(pl.multiple_of(i * SBQ, SBQ), SBQ)
            q = q_s[ri, :]
            dacc = _sb_stack(do_ref[ri, :])

            def kblock(j, c):
                rj = pl.ds(pl.multiple_of(j * SBQ, SBQ), SBQ)
                carry = _sb_rows_as_column(carry_ref[0, 0, i, j])
                dq_j, dk_j, dv_j, dc = _sb_block_bwd(q, k_s[rj, :], x_ref[rj, _SB_V], carry, False, dacc, c[1])
                dk_s[rj, :] += dk_j
                dv_s[rj, :] += dv_j
                return c[0] + dq_j, dc

            dq, dc = lax.fori_loop(0, i, kblock, (jnp.zeros((SBQ, LANES), F32), jnp.zeros((2 * SBQ, 1), F32)))
            dq_i, dk_i, dv_i, _dc = _sb_block_bwd(q, k_s[ri, :], x_ref[ri, _SB_V], jnp.zeros((2 * SBQ, 1), F32), True, dacc, dc)
            dk_s[ri, :] += dk_i
            dv_s[ri, :] += dv_i
            dq_s[ri, :] = dq + dq_i
            return 0

        lax.fori_loop(0, nblk, qblock, 0)
        dq_in, dqg = q_vjp(dq_s[...])
        dk_in, dkg = k_vjp(dk_s[...])
        dx_ref[:, _SB_Q], dx_ref[:, _SB_K], dx_ref[:, _SB_V] = dq_in, dk_in, dv_s[...]

        @pl.when(jnp.logical_and(b_i == 0, p == 0))
        def _():
            dqg_ref[...] = jnp.zeros_like(dqg_ref)
            dkg_ref[...] = jnp.zeros_like(dkg_ref)

        dqg_ref[...] += dqg + pltpu.roll(dqg, 64, 1)
        dkg_ref[...] += dkg + pltpu.roll(dkg, 64, 1)

    M = B * T
    vec = pl.BlockSpec((1, LANES), lambda b, p: (0, 0))
    blk = pl.BlockSpec((T, 3 * LANES), lambda b, p: (b, p))
    big = [pltpu.VMEM((T, LANES), F32)]
    return pl.pallas_call(
        body, name="sb_bwd", grid=(B, 2),
        in_specs=[blk, vec, vec, pl.BlockSpec((T, LANES), lambda b, p: (b, DN_W // LANES + p)), _sb_carry_spec(nblk)],
        out_specs=[blk, vec, vec],
        out_shape=[jax.ShapeDtypeStruct((M, 3 * SB_W), F32)] + [jax.ShapeDtypeStruct((1, LANES), F32)] * 2,
        scratch_shapes=big * 5,
        compiler_params=_params(dimension_semantics=("arbitrary", "arbitrary")),
    )(proj_sb, q_gain, k_gain, dmix, carries)


def _sg_chunk(u, v, gain, w_a, w_b, bias):
    n = 128
    row, col = _iota((n, n), 0), _iota((n, n), 1)
    first = _iota((1, LANES), 1) < 64
    vn = _group_rms(_gelu(v), gain)
    tril = col <= row
    mixed = jnp.where(first, _dot(jnp.where(tril, w_a, 0.0), vn), _dot(jnp.where(tril, w_b, 0.0), vn)) + bias
    return _gelu(u) * mixed


_SG_U, _SG_V = slice(0, LANES), slice(LANES, 2 * LANES)


def _sg_fwd_call(proj_sg, mix, gain, sg_w, bias, B, T):
    nchunk = T // 128

    def body(x_ref, g_ref, wa_ref, wb_ref, bias_ref, mix_ref, out_ref):
        del mix_ref

        def step(i, _):
            r = pl.ds(pl.multiple_of(i * 128, 128), 128)
            out_ref[r, :] = _sg_chunk(x_ref[r, _SG_U], x_ref[r, _SG_V], g_ref[...], wa_ref[0], wb_ref[0], bias_ref[...])
            return 0

        lax.fori_loop(0, nchunk, step, 0)

    return pl.pallas_call(
        body, name="sg_fwd", grid=(B, 2),
        in_specs=[pl.BlockSpec((T, 2 * LANES), lambda b, p: (b, p)), pl.BlockSpec((1, LANES), lambda b, p: (0, p)),
                  pl.BlockSpec((1, 128, 128), lambda b, p: (2 * p, 0, 0)), pl.BlockSpec((1, 128, 128), lambda b, p: (2 * p + 1, 0, 0)),
                  pl.BlockSpec((128, LANES), lambda b, p: (0, p)), pl.BlockSpec(memory_space=pl.ANY)],
        out_specs=pl.BlockSpec((T, LANES), lambda b, p: (b, (DN_W + SB_W) // LANES + p)),
        out_shape=jax.ShapeDtypeStruct((B * T, D), F32), input_output_aliases={5: 0},
        compiler_params=_params(dimension_semantics=("arbitrary", "arbitrary")),
    )(proj_sg, gain, sg_w, sg_w, bias, mix)


def _sg_bwd_call(proj_sg, dmix, gain, sg_w, bias, B, T):
    nchunk = T // 128

    def body(x_ref, g_ref, wa_ref, wb_ref, bias_ref, do_ref, dx_ref, dg_ref, dw_ref, db_ref):
        p, b_i = pl.program_id(0), pl.program_id(1)

        def step(i, c):
            r = pl.ds(pl.multiple_of(i * 128, 128), 128)
            _, vjp = jax.vjp(_sg_chunk, x_ref[r, _SG_U], x_ref[r, _SG_V], g_ref[...], wa_ref[0], wb_ref[0], bias_ref[...])
            du, dv, dg, dwa, dwb, dbias = vjp(do_ref[r, :])
            dx_ref[r, _SG_U], dx_ref[r, _SG_V] = du, dv
            return c[0] + dg, c[1] + dwa, c[2] + dwb, c[3] + dbias

        z = jnp.zeros((128, 128), F32)
        dg, dwa, dwb, dbias = lax.fori_loop(0, nchunk, step, (jnp.zeros((1, LANES), F32), z, z, z))
        lane = _iota((1, LANES), 1)
        first = lane < 64
        s_a = jnp.sum(jnp.where(first, dbias, 0.0), axis=-1, keepdims=True)
        s_b = jnp.sum(jnp.where(first, 0.0, dbias), axis=-1, keepdims=True)
        dbg = jnp.where(lane == 2 * p, s_a, 0.0) + jnp.where(lane == 2 * p + 1, s_b, 0.0)

        @pl.when(b_i == 0)
        def _():
            dg_ref[...] = jnp.zeros_like(dg_ref)
            dw_ref[...] = jnp.zeros_like(dw_ref)

        @pl.when(jnp.logical_and(b_i == 0, p == 0))
        def _():
            db_ref[...] = jnp.zeros_like(db_ref)

        dg_ref[...] += dg
        dw_ref[0] += dwa
        dw_ref[1] += dwb
        db_ref[...] += dbg

    M = B * T
    blk = pl.BlockSpec((T, 2 * LANES), lambda p, b: (b, p))
    return pl.pallas_call(
        body, name="sg_bwd", grid=(2, B),
        in_specs=[blk, pl.BlockSpec((1, LANES), lambda p, b: (0, p)),
                  pl.BlockSpec((1, 128, 128), lambda p, b: (2 * p, 0, 0)), pl.BlockSpec((1, 128, 128), lambda p, b: (2 * p + 1, 0, 0)),
                  pl.BlockSpec((128, LANES), lambda p, b: (0, p)),
                  pl.BlockSpec((T, LANES), lambda p, b: (b, (DN_W + SB_W) // LANES + p))],
        out_specs=[blk, pl.BlockSpec((1, LANES), lambda p, b: (0, p)), pl.BlockSpec((2, 128, 128), lambda p, b: (p, 0, 0)),
                   pl.BlockSpec((128, LANES), lambda p, b: (0, 0))],
        out_shape=[jax.ShapeDtypeStruct((M, 2 * SG_W), F32), jax.ShapeDtypeStruct((1, SG_W), F32),
                   jax.ShapeDtypeStruct((4, 128, 128), F32), jax.ShapeDtypeStruct((128, LANES), F32)],
        compiler_params=_params(dimension_semantics=("arbitrary", "arbitrary")),
    )(proj_sg, gain, sg_w, sg_w, bias, dmix)


def _row_tile(m, most=512):
    return min(m, most)


def _in_proj_call(x, gain, wt):
    m = x.shape[0]
    tm = _row_tile(m)

    def body(x_ref, g_ref, wt_ref, *out_refs):
        h = _rms(x_ref[...], g_ref[...]).astype(BF16)
        out_refs[-1][...] = h
        for (off, width), out_ref in zip(SECTIONS, out_refs):
            out_ref[...] = lax.dot_general(h, wt_ref[off:off + width, :], (((1,), (1,)), ((), ())), preferred_element_type=F32)

    rows = lambda width: pl.BlockSpec((tm, width), lambda i: (i, 0))
    return pl.pallas_call(
        body, name="in_proj", grid=(m // tm,),
        in_specs=[rows(D), pl.BlockSpec((1, D), lambda i: (0, 0)),
                  pl.BlockSpec((NPACK, D), lambda i: (0, 0), pipeline_mode=pl.Buffered(1))],
        out_specs=[rows(w) for _, w in SECTIONS] + [rows(D)],
        out_shape=[jax.ShapeDtypeStruct((m, w), F32) for _, w in SECTIONS] + [jax.ShapeDtypeStruct((m, D), BF16)],
        compiler_params=_params(dimension_semantics=("arbitrary",)),
    )(x, gain, wt)


def _in_proj_bwd_call(dsections, wt, x, gain, dres, swap=()):
    m = x.shape[0]
    tm = _row_tile(m)
    nsec, ns = len(SECTIONS), len(swap)

    def body(*refs):
        ds_refs = refs[:nsec]
        wt_ref, x_ref, g_ref, dres_ref = refs[nsec:nsec + 4]
        dx_ref, dg_ref = refs[nsec + 4 + ns:nsec + 6 + ns]
        step = pl.program_id(0)
        if ns:
            send, finish = _chip_swap_phases(refs[nsec + 4:nsec + 4 + ns], refs[nsec + 6 + ns:nsec + 6 + 2 * ns],
                                             *refs[nsec + 6 + 2 * ns:])
            pl.when(step == 0)(send)

        @pl.when(step == 0)
        def _():
            dg_ref[...] = jnp.zeros_like(dg_ref)

        dh = 0.0
        for (off, width), ds_ref in zip(SECTIONS, ds_refs):
            dh = dh + jnp.dot(ds_ref[...].astype(BF16), wt_ref[off:off + width, :], preferred_element_type=F32)
        _, vjp = jax.vjp(_rms, x_ref[...], g_ref[...])
        dx, dg = vjp(dh)
        dx_ref[...] = dres_ref[...] + dx
        dg_ref[...] += dg
        if ns:
            pl.when(step == m // tm - 1)(finish)

    rows = lambda width: pl.BlockSpec((tm, width), lambda i: (i, 0))
    outs = pl.pallas_call(
        body, name="in_proj_bwd", grid=(m // tm,),
        in_specs=[rows(w) for _, w in SECTIONS] + [pl.BlockSpec((NPACK, D), lambda i: (0, 0), pipeline_mode=pl.Buffered(1)),
                                                   rows(D), pl.BlockSpec((1, D), lambda i: (0, 0)), rows(D)] + _any_specs(ns),
        out_specs=[rows(D), pl.BlockSpec((1, D), lambda i: (0, 0))] + _any_specs(ns),
        out_shape=[jax.ShapeDtypeStruct((m, D), F32), jax.ShapeDtypeStruct((1, D), F32)] + _chip_swap_shapes(swap),
        scratch_shapes=_chip_swap_sems(ns) if ns else [],
        compiler_params=_params(dimension_semantics=("arbitrary",)),
    )(*dsections, wt, x, gain, dres, *swap)
    return outs[:2], outs[2:]


def _in_proj_grad_call(dsections, h):
    m = h.shape[0]
    tm = min(m, 256)

    def body(*refs):
        ds_refs, (h_ref, out_ref) = refs[:len(SECTIONS)], refs[len(SECTIONS):]

        @pl.when(pl.program_id(0) == 0)
        def _():
            out_ref[...] = jnp.zeros_like(out_ref)

        for (off, width), ds_ref in zip(SECTIONS, ds_refs):
            out_ref[off:off + width, :] += lax.dot_general(ds_ref[...].astype(BF16), h_ref[...], (((0,), (0,)), ((), ())),
                                                           preferred_element_type=F32)

    rows = lambda width: pl.BlockSpec((tm, width), lambda i: (i, 0))
    return pl.pallas_call(
        body, name="grad_w_in", grid=(m // tm,),
        in_specs=[rows(w) for _, w in SECTIONS] + [rows(D)],
        out_specs=pl.BlockSpec((NPACK, D), lambda i: (0, 0), pipeline_mode=pl.Buffered(1)),
        out_shape=jax.ShapeDtypeStruct((NPACK, D), F32),
        compiler_params=_params(dimension_semantics=("arbitrary",)),
    )(*dsections, h)


def _packed_column_of():
    t = np.full(NPACK, -1, np.int64)
    lanes = np.arange(LANES)
    for pair in range(2):
        for s in range(4):
            t[DN_OFF + pair * 1024 + s * 256 + np.arange(256)] = s * DN_W + pair * 256 + np.arange(256)
        for s in range(3):
            t[SB_OFF + pair * 384 + s * LANES + lanes] = 2056 + s * SB_W + pair * LANES + lanes
        for s in range(2):
            t[SG_OFF + pair * 256 + s * LANES + lanes] = 2056 + 3 * SB_W + s * SG_W + pair * LANES + lanes
    t[AB_OFF + np.arange(2 * NH)] = 4 * DN_W + np.arange(2 * NH)
    return t


def _row_tables():
    col = _packed_column_of()
    fwd = np.where(col >= 0, (col // IN_SHARD) * IN_SHARD_PAD + col % IN_SHARD, -1)
    packed_of = np.full(IN_DIM, -1, np.int64)
    packed_of[col[col >= 0]] = np.nonzero(col >= 0)[0]
    r = np.arange(NDEV * IN_SHARD_PAD)
    inside = r % IN_SHARD_PAD < IN_SHARD
    back = np.where(inside, packed_of[np.minimum((r // IN_SHARD_PAD) * IN_SHARD + r % IN_SHARD_PAD, IN_DIM - 1)], -1)
    return fwd, back


def _row_perm_call(src, table, name):
    n_out = table.shape[0]
    touched = [sorted(set((table[b * 128:(b + 1) * 128][table[b * 128:(b + 1) * 128] >= 0] // 128).tolist()))
               for b in range(n_out // 128)]

    def body(tbl_ref, src_ref, out_ref):
        lane = _iota((1, LANES), 1)
        for b, blocks in enumerate(touched):
            want = tbl_ref[b * 128:(b + 1) * 128, :]
            acc = jnp.zeros((128, D), F32)
            for sb in blocks:
                pick = jnp.where(want == sb * 128 + lane, 1.0, 0.0).astype(BF16)
                acc = acc + jnp.dot(pick, src_ref[sb * 128:(sb + 1) * 128, :].astype(BF16), preferred_element_type=F32)
            out_ref[b * 128:(b + 1) * 128, :] = acc.astype(BF16)

    return pl.pallas_call(
        body, name=name, out_shape=jax.ShapeDtypeStruct((n_out, D), BF16),
        in_specs=[pl.BlockSpec(memory_space=pltpu.VMEM)] * 2, out_specs=pl.BlockSpec(memory_space=pltpu.VMEM),
        compiler_params=_params(),
    )(jnp.asarray(table.reshape(-1, 1), jnp.int32), src)


def _out_proj_call(a, w, res):
    m, k = a.shape
    n = w.shape[1]
    tm = _row_tile(m)

    def body(a_ref, w_ref, res_ref, out_ref):
        out_ref[...] = res_ref[...] + jnp.dot(a_ref[...].astype(BF16), w_ref[...], preferred_element_type=F32)

    return pl.pallas_call(
        body, name="out_proj", grid=(m // tm,),
        in_specs=[pl.BlockSpec((tm, k), lambda i: (i, 0)), pl.BlockSpec((k, n), lambda i: (0, 0)),
                  pl.BlockSpec((tm, n), lambda i: (i, 0))],
        out_specs=pl.BlockSpec((tm, n), lambda i: (i, 0)),
        out_shape=jax.ShapeDtypeStruct((m, n), F32),
        compiler_params=_params(dimension_semantics=("arbitrary",)),
    )(a, w, res)


def _ffn_specs(tm):
    return [pl.BlockSpec((1, D, FF_SHARD), lambda i, j: (j, 0, 0)), pl.BlockSpec((FF_SHARD, D), lambda i, j: (j, 0))]


def _ffn_fwd_call(x, gain, w1, w2):
    m = x.shape[0]
    tm = _row_tile(m, 1024)

    def body(x_ref, g_ref, w1_ref, w2_ref, out_ref, h_s, acc_s):
        j = pl.program_id(1)

        @pl.when(j == 0)
        def _():
            h_s[...] = _rms(x_ref[...], g_ref[...]).astype(BF16)
            acc_s[...] = jnp.zeros_like(acc_s)

        a = jnp.maximum(jnp.dot(h_s[...], w1_ref[0], preferred_element_type=F32), 0.0)
        acc_s[...] += jnp.dot((a * a).astype(BF16), w2_ref[...], preferred_element_type=F32)

        @pl.when(j == NDEV - 1)
        def _():
            out_ref[...] = x_ref[...] + acc_s[...]

    return pl.pallas_call(
        body, name="ffn_fwd", grid=(m // tm, NDEV),
        in_specs=[pl.BlockSpec((tm, D), lambda i, j: (i, 0)), pl.BlockSpec((1, D), lambda i, j: (0, 0))] + _ffn_specs(tm),
        out_specs=pl.BlockSpec((tm, D), lambda i, j: (i, 0)),
        out_shape=jax.ShapeDtypeStruct((m, D), F32),
        scratch_shapes=[pltpu.VMEM((tm, D), BF16), pltpu.VMEM((tm, D), F32)],
        compiler_params=_params(dimension_semantics=("arbitrary", "arbitrary")),
    )(x, gain, w1, w2)


def _ffn_bwd_call(x, dy, gain, w1, w2, swap=()):
    m = x.shape[0]
    tm = _row_tile(m, 1024)
    ns = len(swap)

    def body(*refs):
        x_ref, dy_ref, g_ref, w1_ref, w2_ref = refs[:5]
        dx_ref, da_ref, r_ref, h_ref, dg_ref = refs[5 + ns:10 + ns]
        acc_s = refs[10 + 2 * ns]
        i, j = pl.program_id(0), pl.program_id(1)
        if ns:
            send, finish = _sibling_swap_phases(refs[5:5 + ns], refs[10 + ns:10 + 2 * ns], *refs[11 + 2 * ns:])
            pl.when(jnp.logical_and(i == 0, j == 0))(send)

        @pl.when(j == 0)
        def _():
            h_ref[...] = _rms(x_ref[...], g_ref[...]).astype(BF16)
            acc_s[...] = jnp.zeros_like(acc_s)

        @pl.when(jnp.logical_and(i == 0, j == 0))
        def _():
            dg_ref[...] = jnp.zeros_like(dg_ref)

        a = jnp.maximum(jnp.dot(h_ref[...], w1_ref[0], preferred_element_type=F32), 0.0)
        r_ref[...] = (a * a).astype(BF16)
        dr = lax.dot_general(dy_ref[...].astype(BF16), w2_ref[...], (((1,), (1,)), ((), ())), preferred_element_type=F32)
        da = (dr * (2.0 * a)).astype(BF16)
        da_ref[...] = da
        acc_s[...] += lax.dot_general(da, w1_ref[0], (((1,), (1,)), ((), ())), preferred_element_type=F32)

        @pl.when(j == NDEV - 1)
        def _():
            _, vjp = jax.vjp(_rms, x_ref[...], g_ref[...])
            dx, dg = vjp(acc_s[...])
            dx_ref[...] = dy_ref[...] + dx
            dg_ref[...] += dg

        if ns:
            pl.when(jnp.logical_and(i == m // tm - 1, j == NDEV - 1))(finish)

    outs = pl.pallas_call(
        body, name="ffn_bwd", grid=(m // tm, NDEV),
        in_specs=[pl.BlockSpec((tm, D), lambda i, j: (i, 0)), pl.BlockSpec((tm, D), lambda i, j: (i, 0)),
                  pl.BlockSpec((1, D), lambda i, j: (0, 0))] + _ffn_specs(tm) + _any_specs(ns),
        out_specs=[pl.BlockSpec((tm, D), lambda i, j: (i, 0)), pl.BlockSpec((tm, FF_SHARD), lambda i, j: (i, j)),
                   pl.BlockSpec((tm, FF_SHARD), lambda i, j: (i, j)), pl.BlockSpec((tm, D), lambda i, j: (i, 0)),
                   pl.BlockSpec((1, D), lambda i, j: (0, 0))] + _any_specs(ns),
        out_shape=[jax.ShapeDtypeStruct((m, D), F32), jax.ShapeDtypeStruct((m, DFF), BF16), jax.ShapeDtypeStruct((m, DFF), BF16),
                   jax.ShapeDtypeStruct((m, D), BF16), jax.ShapeDtypeStruct((1, D), F32)] + _sibling_swap_shapes(swap),
        scratch_shapes=[pltpu.VMEM((tm, D), F32)] + (_sibling_swap_sems(ns) if ns else []),
        compiler_params=_params(dimension_semantics=("arbitrary", "arbitrary")),
    )(x, dy, gain, w1, w2, *swap)
    return outs[:5], outs[5:]


def _mm_nt_call(a, b, name):
    m, k = a.shape
    n = b.shape[0]
    tm = _row_tile(m)

    def body(a_ref, b_ref, out_ref):
        out_ref[...] = lax.dot_general(a_ref[...].astype(BF16), b_ref[...].astype(BF16), (((1,), (1,)), ((), ())),
                                       preferred_element_type=F32)

    return pl.pallas_call(
        body, name=name, grid=(m // tm,),
        in_specs=[pl.BlockSpec((tm, k), lambda i: (i, 0)), pl.BlockSpec((n, k), lambda i: (0, 0))],
        out_specs=pl.BlockSpec((tm, n), lambda i: (i, 0)),
        out_shape=jax.ShapeDtypeStruct((m, n), F32),
        compiler_params=_params(dimension_semantics=("arbitrary",)),
    )(a, b)


def _mm_tn_call(a, b, name, col_shards=False):
    m, k = a.shape
    n = b.shape[1]
    tm, tk = _row_tile(m, 1024), min(k, 1024)
    tn = n // NDEV if col_shards else min(n, 1024)

    def body(a_ref, b_ref, out_ref, acc_s):
        s = pl.program_id(2)

        @pl.when(s == 0)
        def _():
            acc_s[...] = jnp.zeros_like(acc_s)

        acc_s[...] += lax.dot_general(a_ref[...].astype(BF16), b_ref[...].astype(BF16), (((0,), (0,)), ((), ())),
                                      preferred_element_type=F32)

        @pl.when(s == m // tm - 1)
        def _():
            out_ref[...] = acc_s[...].astype(BF16).reshape(out_ref.shape)

    if col_shards:
        out_spec, out_shape = pl.BlockSpec((1, tk, tn), lambda i, j, s: (j, i, 0)), (NDEV, k, tn)
    else:
        out_spec, out_shape = pl.BlockSpec((tk, tn), lambda i, j, s: (i, j)), (k, n)
    return pl.pallas_call(
        body, name=name, grid=(k // tk, n // tn, m // tm),
        in_specs=[pl.BlockSpec((tm, tk), lambda i, j, s: (s, i)), pl.BlockSpec((tm, tn), lambda i, j, s: (s, j))],
        out_specs=out_spec, out_shape=jax.ShapeDtypeStruct(out_shape, BF16),
        scratch_shapes=[pltpu.VMEM((tk, tn), F32)],
        compiler_params=_params(dimension_semantics=("arbitrary", "arbitrary", "arbitrary")),
    )(a, b)


def _loss_call(y, target):
    m = y.shape[0]
    tm = _row_tile(m)

    def body(y_ref, t_ref, loss_ref, dy_ref):
        @pl.when(pl.program_id(0) == 0)
        def _():
            loss_ref[...] = jnp.zeros_like(loss_ref)

        err = y_ref[...] - t_ref[...]
        dy_ref[...] = err * (1.0 / D)
        per_row = jnp.mean(err * err, axis=-1, keepdims=True)
        loss_ref[...] += jnp.broadcast_to(0.5 * jnp.sum(per_row, axis=0, keepdims=True), (1, LANES))

    return pl.pallas_call(
        body, name="loss", grid=(m // tm,),
        in_specs=[pl.BlockSpec((tm, D), lambda i: (i, 0))] * 2,
        out_specs=[pl.BlockSpec((1, LANES), lambda i: (0, 0)), pl.BlockSpec((tm, D), lambda i: (i, 0))],
        out_shape=[jax.ShapeDtypeStruct((1, LANES), F32), jax.ShapeDtypeStruct((m, D), F32)],
        compiler_params=_params(dimension_semantics=("arbitrary",)),
    )(y, target)


def _adamw_call(w, g, m, v, name):
    shape = w.shape
    cols = shape[-1] if w.ndim > 1 else w.size
    rows = w.size // cols
    tr = rows if (rows <= 512 or rows % 512) else 512
    c1, c2 = 1.0 - ADAM_B1 ** ADAM_STEP, 1.0 - ADAM_B2 ** ADAM_STEP

    def body(w_ref, g_ref, m_ref, v_ref, d_ref, nm_ref, nv_ref):
        g_ = g_ref[...]
        nm = ADAM_B1 * m_ref[...] + (1.0 - ADAM_B1) * g_
        nv = ADAM_B2 * v_ref[...] + (1.0 - ADAM_B2) * (g_ * g_)
        d_ref[...] = -ADAM_LR * ((nm / c1) / (jnp.sqrt(nv / c2) + ADAM_EPS) + ADAM_WD * w_ref[...])
        nm_ref[...], nv_ref[...] = nm, nv

    spec = pl.BlockSpec((tr, cols), lambda i: (i, 0))
    outs = pl.pallas_call(
        body, name=name, grid=(rows // tr,), in_specs=[spec] * 4, out_specs=[spec] * 3,
        out_shape=[jax.ShapeDtypeStruct((rows, cols), F32)] * 3,
        compiler_params=_params(dimension_semantics=("arbitrary",)),
    )(*(t.reshape(rows, cols) for t in (w, g, m, v)))
    return tuple(o.reshape(shape) for o in outs)


def _sum_tile(rows):
    for cand in (2048, 1024, 512, 256, 128):
        if rows > cand and rows % cand == 0:
            return cand
    return rows


def _pair_sum_call(gs, gots, core, name):
    n = len(gs)

    def body(core_ref, *refs):
        del core_ref
        for g_ref, got_ref, out_ref in zip(refs[:n], refs[n:2 * n], refs[2 * n:]):
            out_ref[...] = (g_ref[...].astype(F32) + got_ref[...].astype(F32)).astype(BF16)

    block = lambda g: (1,) + g.shape[1:]
    grid_spec = pltpu.PrefetchScalarGridSpec(
        num_scalar_prefetch=1, grid=(4,),
        in_specs=[pl.BlockSpec(block(g), lambda ch, core_ref: (2 * ch + core_ref[0], 0, 0)) for g in gs]
        + [pl.BlockSpec(block(g), lambda ch, core_ref: (ch, 0, 0)) for g in gs],
        out_specs=[pl.BlockSpec(block(g), lambda ch, core_ref: (ch, 0, 0)) for g in gs])
    return pl.pallas_call(
        body, name=name, grid_spec=grid_spec, out_shape=[jax.ShapeDtypeStruct((4,) + g.shape[1:], BF16) for g in gs],
        compiler_params=_params(dimension_semantics=("arbitrary",)),
    )(jnp.asarray(core, jnp.int32).reshape(1), *gs, *gots)


def _total_sum_call(gs, gots, froms, me, my_chip, name):
    n = len(gs)

    def body(idx_ref, *refs):
        del idx_ref
        for a in range(n):
            g_ref, got_ref, f0, f1, f2 = (refs[k * n + a] for k in range(5))
            acc = g_ref[0].astype(F32) + got_ref[0].astype(F32)
            for f in (f0, f1, f2):
                acc = acc + f[0].astype(F32)
            refs[5 * n + a][...] = acc

    block = lambda g: (1,) + g.shape[1:]
    picked = lambda which: [pl.BlockSpec(block(g), lambda t, idx, which=which: (idx[which], 0, 0)) for g in gs]
    fixed = lambda j: [pl.BlockSpec(block(g), lambda t, idx, j=j: (j, 0, 0)) for g in gs]
    grid_spec = pltpu.PrefetchScalarGridSpec(
        num_scalar_prefetch=1, grid=(1,),
        in_specs=picked(0) + picked(1) + fixed(0) + fixed(1) + fixed(2),
        out_specs=[pl.BlockSpec(g.shape[1:], lambda t, idx: (0, 0)) for g in gs])
    return pl.pallas_call(
        body, name=name, grid_spec=grid_spec, out_shape=[jax.ShapeDtypeStruct(g.shape[1:], F32) for g in gs],
        compiler_params=_params(dimension_semantics=("arbitrary",)),
    )(jnp.stack([jnp.asarray(me, jnp.int32), jnp.asarray(my_chip, jnp.int32)]), *gs, *gots, *froms, *froms, *froms)


def _sum_call(parts, out_dtype, name):
    rows, cols = parts[0][0].shape[1:]
    tr = _sum_tile(rows)
    index = jnp.stack([jnp.asarray(i, jnp.int32) for _, i in parts])

    def body(idx_ref, *refs):
        del idx_ref
        acc = refs[0][0].astype(F32)
        for r in refs[1:-1]:
            acc = acc + r[0].astype(F32)
        refs[-1][...] = acc.astype(out_dtype)

    grid_spec = pltpu.PrefetchScalarGridSpec(
        num_scalar_prefetch=1, grid=(rows // tr,),
        in_specs=[pl.BlockSpec((1, tr, cols), lambda t, idx, n=n: (idx[n], t, 0)) for n in range(len(parts))],
        out_specs=pl.BlockSpec((tr, cols), lambda t, idx: (t, 0)))
    return pl.pallas_call(
        body, name=name, grid_spec=grid_spec, out_shape=jax.ShapeDtypeStruct((rows, cols), out_dtype),
        compiler_params=_params(dimension_semantics=("arbitrary",)),
    )(index, *(a for a, _ in parts))


def _place():
    return lax.axis_index("x"), lax.axis_index("y"), lax.axis_index("c")


def _any_specs(n):
    return [pl.BlockSpec(memory_space=pl.ANY)] * n


def _all_gather_call(xs, name):
    n = len(xs)

    def body(*refs):
        for phase in _gather_phases(refs[:n], refs[n:2 * n], *refs[2 * n:]):
            phase()

    return pl.pallas_call(
        body, name=name, in_specs=_any_specs(n), out_specs=_any_specs(n),
        out_shape=_gather_shapes(xs), scratch_shapes=_gather_sems(n),
    )(*xs)


def _gather_shapes(xs):
    return [jax.ShapeDtypeStruct((NDEV,) + x.shape, x.dtype) for x in xs]


def _gather_sems(n):
    return [pltpu.SemaphoreType.DMA((7 * n,)), pltpu.SemaphoreType.DMA((7 * n,)), pltpu.SemaphoreType.DMA((n,))]


def _gather_phases(x_refs, out_refs, send_sems, recv_sems, local_sems):
    n = len(x_refs)
    ax, ay, ac = _place()
    me, sibling = (ax, ay, ac), (ax, ay, 1 - ac)
    chips = [(1 - ax, ay), (ax, 1 - ay), (1 - ax, 1 - ay)]

    def copy(a, k, block, to, src=None):
        slot = out_refs[a].at[4 * block[0] + 2 * block[1] + block[2]]
        return pltpu.make_async_remote_copy(
            src_ref=slot if src is None else src, dst_ref=slot,
            send_sem=send_sems.at[7 * a + k], recv_sem=recv_sems.at[7 * a + k], device_id=to, device_id_type=MESH)

    local = [pltpu.make_async_copy(x_refs[a], out_refs[a].at[4 * ax + 2 * ay + ac], local_sems.at[a]) for a in range(n)]
    first = []
    for a in range(n):
        first.append(copy(a, 0, me, sibling, src=x_refs[a]))
        first += [copy(a, 1 + j, me, (*chip, ac), src=x_refs[a]) for j, chip in enumerate(chips)]
    passed = [copy(a, 4 + j, (*chip, ac), sibling) for j, chip in enumerate(chips) for a in range(n)]

    def send():
        for cp in local + first:
            cp.start()

    def forward():
        for j, chip in enumerate(chips):
            for a in range(n):
                copy(a, 1 + j, (*chip, ac), me).wait_recv()
                passed[j * n + a].start()

    def finish():
        for a in range(n):
            copy(a, 0, sibling, me).wait_recv()
            for j, chip in enumerate(chips):
                copy(a, 4 + j, (*chip, 1 - ac), me).wait_recv()
        for cp in first + passed:
            cp.wait_send()
        for cp in local:
            cp.wait()

    return send, forward, finish


def _swap_sibling_call(xs, name):
    n = len(xs)

    def body(*refs):
        for phase in _sibling_swap_phases(refs[:n], refs[n:2 * n], *refs[2 * n:]):
            phase()

    return pl.pallas_call(
        body, name=name, in_specs=_any_specs(n), out_specs=_any_specs(n),
        out_shape=_sibling_swap_shapes(xs), scratch_shapes=_sibling_swap_sems(n),
    )(*xs)


def _sibling_swap_shapes(xs):
    return [jax.ShapeDtypeStruct((4,) + x.shape[1:], x.dtype) for x in xs]


def _sibling_swap_sems(n):
    return [pltpu.SemaphoreType.DMA((n,)), pltpu.SemaphoreType.DMA((n,))]


def _sibling_swap_phases(x_refs, out_refs, send_sems, recv_sems):
    ax, ay, ac = _place()
    sibling = (ax, ay, 1 - ac)

    def send():
        for a, (x_ref, out_ref) in enumerate(zip(x_refs, out_refs)):
            for chip in range(4):
                pltpu.make_async_remote_copy(src_ref=x_ref.at[2 * chip + 1 - ac], dst_ref=out_ref.at[chip],
                                             send_sem=send_sems.at[a], recv_sem=recv_sems.at[a],
                                             device_id=sibling, device_id_type=MESH).start()

    def finish():
        for a, (x_ref, out_ref) in enumerate(zip(x_refs, out_refs)):
            pltpu.make_async_remote_copy(src_ref=x_ref.at[pl.ds(0, 4)], dst_ref=out_ref, send_sem=send_sems.at[a],
                                         recv_sem=recv_sems.at[a], device_id=sibling, device_id_type=MESH).wait()

    return send, finish


def _chip_swap_shapes(xs):
    return [jax.ShapeDtypeStruct((3,) + x.shape[1:], x.dtype) for x in xs]


def _chip_swap_sems(n):
    return [pltpu.SemaphoreType.DMA((3 * n,)), pltpu.SemaphoreType.DMA((3 * n,))]


def _chip_swap_phases(x_refs, out_refs, send_sems, recv_sems):
    ax, ay, ac = _place()
    chips = [(1 - ax, ay), (ax, 1 - ay), (1 - ax, 1 - ay)]
    copies = [pltpu.make_async_remote_copy(src_ref=x_refs[a].at[2 * cx + cy], dst_ref=out_refs[a].at[j],
                                           send_sem=send_sems.at[3 * a + j], recv_sem=recv_sems.at[3 * a + j],
                                           device_id=(cx, cy, ac), device_id_type=MESH)
              for a in range(len(x_refs)) for j, (cx, cy) in enumerate(chips)]

    def send():
        for cp in copies:
            cp.start()

    def finish():
        for cp in copies:
            cp.wait()

    return send, finish


def _reduce_begin(gs, name):
    got = _swap_sibling_call(gs, name + "_d2d")
    return got, _pair_sum_call(gs, got, lax.axis_index("c"), name + "_pair")


def _reduce_end(gs, got, from_chips, name):
    ax, ay, ac = _place()
    return _total_sum_call(gs, got, from_chips, 4 * ax + 2 * ay + ac, 2 * ax + ay, name + "_total")


SMALL = ("norm1_g", "conv_w", "a_log", "dt_bias", "dn_out_g", "sb_q_g", "sb_k_g", "sg_v_g", "sg_w", "sg_b", "norm2_g")
WEIGHTS = ("norm1_g", "w_in", "conv_w", "a_log", "dt_bias", "dn_out_g", "sb_q_g", "sb_k_g", "sg_v_g", "sg_w", "sg_b",
           "w_out", "norm2_g", "w_ff1", "w_ff2")
SMALL_SHAPE = {"norm1_g": (D,), "conv_w": (4, 3 * DN_W), "a_log": (NH,), "dt_bias": (NH,), "dn_out_g": (128,), "sb_q_g": (64,),
               "sb_k_g": (64,), "sg_v_g": (SG_W,), "sg_w": (NH, 128, 128), "sg_b": (NH, 128), "norm2_g": (D,)}


def _size(shape):
    n = 1
    for s in shape:
        n *= s
    return n


def _to_rows(flat, multiple):
    pad = (-flat.shape[0]) % (LANES * multiple)
    return jnp.pad(flat, (0, pad)).reshape(-1, LANES)


def _conv_by_pair(conv):
    return conv.reshape(4, 3, 2, 256).transpose(0, 2, 1, 3).reshape(4, 3 * DN_W)


def kernel(x, norm1_g, w_in, conv_w, a_log, dt_bias, dn_out_g, sb_q_g, sb_k_g, sg_v_g, sg_w, sg_b, w_out, norm2_g, w_ff1, w_ff2, loss_target, m_norm1_g, m_w_in, m_conv_w, m_a_log, m_dt_bias, m_dn_out_g, m_sb_q_g, m_sb_k_g, m_sg_v_g, m_sg_w, m_sg_b, m_w_out, m_norm2_g, m_w_ff1, m_w_ff2, v_norm1_g, v_w_in, v_conv_w, v_a_log, v_dt_bias, v_dn_out_g, v_sb_q_g, v_sb_k_g, v_sg_v_g, v_sg_w, v_sg_b, v_w_out, v_norm2_g, v_w_ff1, v_w_ff2):
    given = dict(norm1_g=norm1_g, w_in=w_in, conv_w=conv_w, a_log=a_log, dt_bias=dt_bias, dn_out_g=dn_out_g, sb_q_g=sb_q_g,
                 sb_k_g=sb_k_g, sg_v_g=sg_v_g, sg_w=sg_w, sg_b=sg_b, w_out=w_out, norm2_g=norm2_g, w_ff1=w_ff1, w_ff2=w_ff2)
    mom = dict(norm1_g=m_norm1_g, w_in=m_w_in, conv_w=m_conv_w, a_log=m_a_log, dt_bias=m_dt_bias, dn_out_g=m_dn_out_g,
               sb_q_g=m_sb_q_g, sb_k_g=m_sb_k_g, sg_v_g=m_sg_v_g, sg_w=m_sg_w, sg_b=m_sg_b, w_out=m_w_out, norm2_g=m_norm2_g,
               w_ff1=m_w_ff1, w_ff2=m_w_ff2)
    var = dict(norm1_g=v_norm1_g, w_in=v_w_in, conv_w=v_conv_w, a_log=v_a_log, dt_bias=v_dt_bias, dn_out_g=v_dn_out_g,
               sb_q_g=v_sb_q_g, sb_k_g=v_sb_k_g, sg_v_g=v_sg_v_g, sg_w=v_sg_w, sg_b=v_sg_b, w_out=v_w_out, norm2_g=v_norm2_g,
               w_ff1=v_w_ff1, w_ff2=v_w_ff2)
    B, T, _ = x.shape
    M = B * T
    ax, ay, ac = _place()
    me = 4 * ax + 2 * ay + ac
    table_fwd, table_back = _row_tables()

    send = []
    for l in range(2):
        w_in_t = jnp.pad(w_in[l].T, ((0, IN_SHARD_PAD - IN_SHARD), (0, 0)))
        send.append([w_in_t.astype(BF16), w_out[l].astype(BF16), w_ff1[l].astype(BF16), w_ff2[l].astype(BF16)])
    first_in, conv_rows = _all_gather_call([send[0][0], _to_rows(conv_w.reshape(-1), 8)], "gather_first")
    conv_full = conv_rows.reshape(NDEV, -1)[:, :conv_w.size].reshape(NDEV, 2, 4, -1).transpose(1, 2, 0, 3).reshape(2, 4, 3 * DN_W)
    gathered = [[first_in, None, None, None], [None] * 4]

    pad_vec = lambda v: jnp.zeros((1, LANES), F32).at[0, :v.shape[0]].set(v)
    layer = []
    for l in range(2):
        layer.append(dict(
            g1=norm1_g[l].reshape(1, D), g2=norm2_g[l].reshape(1, D), conv=_conv_by_pair(conv_full[l]),
            a_log=pad_vec(a_log[l]), dt_bias=pad_vec(dt_bias[l]), dn_g=dn_out_g[l].reshape(1, LANES),
            sb_qg=jnp.tile(sb_q_g[l], 2).reshape(1, LANES), sb_kg=jnp.tile(sb_k_g[l], 2).reshape(1, LANES),
            sg_g=sg_v_g[l].reshape(1, SG_W), sg_w=sg_w[l], sg_bias=jnp.repeat(sg_b[l].T, 64, axis=1)))

    cur = x.reshape(M, D)
    saved = []
    for l, p in enumerate(layer):
        p["wt"] = _row_perm_call(gathered[l][0].reshape(NDEV * IN_SHARD_PAD, D), table_fwd, "pack_w_in")
        p_dn, p_sb, p_sg, p_ab, h = _in_proj_call(cur, p["g1"], p["wt"])
        mix, dn_kept, arrived = _dn_fwd_call(p_dn, p_ab, p["conv"], p["a_log"], p["dt_bias"], p["dn_g"], B, T,
                                             gather=send[0][1:] + send[1][:1] if l == 0 else [])
        if l == 0:
            gathered[0][1:], gathered[1][0] = list(arrived[:3]), arrived[3]
        p["w_out"], p["w1"], p["w2"] = gathered[l][1].reshape(D, D), gathered[l][2], gathered[l][3].reshape(DFF, D)
        mix, sb_carries, arrived = _sb_fwd_call(p_sb, mix, p["sb_qg"], p["sb_kg"], B, T, gather=send[1][1:] if l == 0 else [])
        if l == 0:
            gathered[1][1:] = list(arrived)
        mix = _sg_fwd_call(p_sg, mix, p["sg_g"], p["sg_w"], p["sg_bias"], B, T)
        x1 = _out_proj_call(mix, p["w_out"], cur)
        x2 = _ffn_fwd_call(x1, p["g2"], p["w1"], p["w2"])
        saved.append(dict(x0=cur, p_dn=p_dn, p_sb=p_sb, p_sg=p_sg, p_ab=p_ab, h=h, mix=mix, x1=x1, dn_kept=dn_kept, sb_carries=sb_carries))
        cur = x2
    loss_part, dy = _loss_call(cur, loss_target.reshape(M, D))
    loss = lax.psum(loss_part[0, 0], ("x", "y", "c"))

    big_grads = [[None] * 4, [None] * 4]
    small_grads = {n: [None, None] for n in SMALL}
    for l in (1, 0):
        p, s = layer[l], saved[l]
        (dx1, da, r, h2, dg2), got1 = _ffn_bwd_call(s["x1"], dy, p["g2"], p["w1"], p["w2"], swap=big_grads[1] if l == 0 else ())
        big_grads[l][2] = _mm_tn_call(h2, da, "grad_w_ff1", col_shards=True)
        big_grads[l][3] = _mm_tn_call(r, dy, "grad_w_ff2").reshape(NDEV, FF_SHARD, D)
        dmix = _mm_nt_call(dx1, p["w_out"], "dmix")
        big_grads[l][1] = _mm_tn_call(s["mix"], dx1, "grad_w_out").reshape(NDEV, D // NDEV, D)
        if l == 0:
            got0, sums0 = _reduce_begin(big_grads[0][1:], "reduce_early0")
            early_sums = list(_pair_sum_call(big_grads[1], got1, ac, "reduce_early1_pair")) + list(sums0)
        (d_dn, d_ab, dcw, dalog, ddtb, ddn_g), early_from = _dn_bwd_call(
            s["p_dn"], s["p_ab"], dmix, s["dn_kept"], p["conv"], p["a_log"], p["dt_bias"], p["dn_g"], B, T,
            swap=early_sums if l == 0 else ())
        d_sb, dqg, dkg = _sb_bwd_call(s["p_sb"], dmix, s["sb_carries"], p["sb_qg"], p["sb_kg"], B, T)
        d_sg, dsg_g, dsg_w, dsg_b = _sg_bwd_call(s["p_sg"], dmix, p["sg_g"], p["sg_w"], p["sg_bias"], B, T)
        dsections = (d_dn, d_sb, d_sg, d_ab)
        dwt = _in_proj_grad_call(dsections, s["h"])
        big_grads[l][0] = _row_perm_call(dwt, table_back, "unpack_grad_w_in").reshape(NDEV, IN_SHARD_PAD, D)
        if l == 0:
            last = big_grads[0][:1]
            last_got, last_sums = _reduce_begin(last, "reduce_last")
        (dy, dg1), last_from = _in_proj_bwd_call(dsections, p["wt"], s["x0"], p["g1"], dx1, swap=last_sums if l == 0 else ())
        for n, val in (("norm1_g", dg1[0]), ("conv_w", dcw.transpose(1, 0, 2).reshape(4, 3 * DN_W)), ("a_log", dalog[0, :NH]),
                       ("dt_bias", ddtb[0, :NH]), ("dn_out_g", ddn_g[0]), ("sb_q_g", dqg[0, :64]), ("sb_k_g", dkg[0, :64]),
                       ("sg_v_g", dsg_g[0]), ("sg_w", dsg_w), ("sg_b", dsg_b[:, :NH].T), ("norm2_g", dg2[0])):
            small_grads[n][l] = val
    grad_x = dy.reshape(B, T, D)

    mine0 = _reduce_end(last, last_got, last_from, "reduce_last")
    mine1 = (_reduce_end(big_grads[1], got1, early_from[:4], "reduce_early1")
             + _reduce_end(big_grads[0][1:], got0, early_from[4:], "reduce_early0"))
    grads = {"w_in": jnp.stack([mine0[0][:IN_SHARD].T, mine1[0][:IN_SHARD].T]), "w_out": jnp.stack([mine1[4], mine1[1]]),
             "w_ff1": jnp.stack([mine1[5], mine1[2]]), "w_ff2": jnp.stack([mine1[6], mine1[3]])}
    small_flat = jnp.concatenate([jnp.stack(small_grads[n]).reshape(-1) for n in SMALL])
    everyone, = _all_gather_call([_to_rows(small_flat, 8)], "gather_small_grads")
    small_sum = _sum_call([(everyone, k) for k in range(NDEV)], F32, "sum_small_grads").reshape(-1)
    off = 0
    for n in SMALL:
        sz = 2 * _size(SMALL_SHAPE[n])
        grads[n] = small_sum[off:off + sz].reshape((2,) + SMALL_SHAPE[n])
        off += sz
    cshard = conv_w.shape[-1]
    grads["conv_w"] = lax.dynamic_slice_in_dim(grads["conv_w"], me * cshard, cshard, axis=2)

    deltas, new_m, new_v = {}, {}, {}
    for n in WEIGHTS:
        deltas[n], new_m[n], new_v[n] = _adamw_call(given[n], grads[n], mom[n], var[n], "adamw_" + n)
    return (loss, grad_x, *[grads[n] for n in WEIGHTS], *[deltas[n] for n in WEIGHTS], *[new_m[n] for n in WEIGHTS],
            *[new_v[n] for n in WEIGHTS])
```

```python
import functools

import numpy as np

import jax
import jax.numpy as jnp
from jax import lax
from jax.experimental import pallas as pl
from jax.experimental.pallas import tpu as pltpu

F32, BF16 = jnp.float32, jnp.bfloat16
EPS = 1e-6
LANES = 128
D = 1024
DFF = 4096
NH = 4
DN_W, SB_W, SG_W = 512, 256, 256
IN_DIM = 3336
NDEV = 8
IN_SHARD = IN_DIM // NDEV
IN_SHARD_PAD = 432
FF_SHARD = DFF // NDEV
DN_OFF, SB_OFF, SG_OFF, AB_OFF, NPACK = 0, 2048, 2816, 3328, 3456
SECTIONS = ((DN_OFF, 2048), (SB_OFF, 768), (SG_OFF, 512), (AB_OFF, 128))
SB_SCALE = 64 ** -0.5
DN_SCALE = 128 ** -0.5
VMEM_LIMIT = 56 * 1024 * 1024
VMEM_LIMIT_MAX = 62 * 1024 * 1024
ADAM_LR, ADAM_B1, ADAM_B2, ADAM_EPS, ADAM_WD, ADAM_STEP = 0.001, 0.9, 0.999, 1e-08, 0.01, 10
MESH = pl.DeviceIdType.MESH


def _iota(shape, dim):
    return lax.broadcasted_iota(jnp.int32, shape, dim)


def _params(**kw):
    return pltpu.CompilerParams(vmem_limit_bytes=VMEM_LIMIT, **kw)


NN, NT, TN = ((1,), (0,)), ((1,), (1,)), ((0,), (0,))


def _mm(a, b, dims):
    return lax.dot_general(a.astype(BF16), b.astype(BF16), (dims, ((), ())), preferred_element_type=F32)


def _plain(a, b, dims):
    return (a.T if dims == TN else a), (b.T if dims == NT else b)


def _mmx(a, b, dims):
    return _mm(*_plain(a, b, dims), NN)


@jax.custom_vjp
def _dot(a, b):
    return _mmx(a, b, NN)


def _dot_fwd(a, b):
    return _dot(a, b), (a, b)


def _dot_bwd(res, g):
    a, b = res
    return _mmx(g, b, NT).astype(a.dtype), _mmx(a, g, TN).astype(b.dtype)


_dot.defvjp(_dot_fwd, _dot_bwd)


def _split(x):
    hi = x.astype(BF16)
    return hi, (x - hi.astype(F32)).astype(BF16)


def _mm2(a, b):
    ah, al = _split(a)
    bh = b.astype(BF16)
    mm = lambda x, y: jnp.dot(x, y, preferred_element_type=F32)
    return mm(ah, bh) + mm(al, bh)


def _mm_ones(ones, x, ones_left):
    hi, lo = _split(x)
    mm = (lambda t: jnp.dot(ones, t, preferred_element_type=F32)) if ones_left else \
         (lambda t: jnp.dot(t, ones, preferred_element_type=F32))
    return mm(hi) + mm(lo)


def _pair_ones(kind, transposed):
    row, col = _iota((128, 128), 0), _iota((128, 128), 1)
    m = (row // 64) == (col // 64)
    if kind == "running":
        m = jnp.logical_and(m, (col >= row) if transposed else (col <= row))
    return jnp.where(m, 1.0, 0.0).astype(BF16)


@functools.partial(jax.custom_vjp, nondiff_argnums=(0,))
def _chunk_sum(kind, x):
    return _mm_ones(_pair_ones(kind, False), x, True)


def _chunk_sum_fwd(kind, x):
    return _chunk_sum(kind, x), None


def _chunk_sum_bwd(kind, _, g):
    return (_mm_ones(_pair_ones(kind, True), g, True),)


_chunk_sum.defvjp(_chunk_sum_fwd, _chunk_sum_bwd)


def _tri_ones(n, transposed):
    row, col = _iota((n, n), 0), _iota((n, n), 1)
    return jnp.where((row < col) if transposed else (row > col), 1.0, 0.0).astype(BF16)


def _sigmoid(x):
    return jax.nn.sigmoid(x)


def _silu(x):
    return x * _sigmoid(x)


def _softplus(x):
    return jnp.maximum(x, 0.0) + jnp.log1p(jnp.exp(-jnp.abs(x)))


def _gelu(x):
    return 0.5 * x * (1.0 + jnp.tanh(0.7978845608028654 * (x + 0.044715 * (x * x * x))))


def _rms(x, gain):
    return x * lax.rsqrt(jnp.mean(x * x, axis=-1, keepdims=True) + EPS) * gain


SUBLANES = 8


def _shift_down_impl(x, k):
    y = pltpu.roll(x, k, 0)
    top = jnp.where(_iota((SUBLANES, x.shape[1]), 0) >= k, y[:SUBLANES], 0.0)
    return jnp.concatenate([top, y[SUBLANES:]], axis=0)


def _shift_up_impl(x, k):
    n = x.shape[0]
    y = pltpu.roll(x, n - k, 0)
    bottom = jnp.where(_iota((SUBLANES, x.shape[1]), 0) < SUBLANES - k, y[n - SUBLANES:], 0.0)
    return jnp.concatenate([y[:n - SUBLANES], bottom], axis=0)


@functools.partial(jax.custom_vjp, nondiff_argnums=(1,))
def _shift_down(x, k):
    return _shift_down_impl(x, k)


def _shift_down_fwd(x, k):
    return _shift_down_impl(x, k), None


def _shift_down_bwd(k, _, g):
    return (_shift_up_impl(g, k),)


_shift_down.defvjp(_shift_down_fwd, _shift_down_bwd)


def _lane_pick(x, idx):
    return jnp.sum(jnp.where(_iota(x.shape, 1) == idx, x, 0.0), axis=-1, keepdims=True)


def _dn_conv(x, w0, w1, w2, w3, l2_scale):
    y = _silu(w3 * x + w2 * _shift_down(x, 1) + w1 * _shift_down(x, 2) + w0 * _shift_down(x, 3))
    if l2_scale is None:
        return y
    return y * lax.rsqrt(jnp.sum(y * y, axis=-1, keepdims=True) + EPS) * l2_scale


def _dn_gates(ab, a_log, dt_bias):
    lane = _iota((1, LANES), 1)
    g = -jnp.exp(a_log) * _softplus(ab + dt_bias)
    return jnp.where(lane < NH, g, jnp.where(lane < 2 * NH, _sigmoid(ab), 0.0))


def _same_head(shape):
    return (_iota(shape, 0) < LANES) == (_iota(shape, 1) < LANES)


def _bd(r2):
    return jnp.where(_same_head((2 * LANES, 2 * LANES)), jnp.concatenate([r2, r2], axis=0), 0.0)


def _bd_t(y2):
    t = y2.T
    return jnp.where(_same_head((2 * LANES, 2 * LANES)), jnp.concatenate([t, t], axis=1), 0.0)


def _pair_prod(kind, a2, b2, mm):
    if kind == NN:
        return mm(a2, _bd(b2))
    if kind == NT:
        return mm(a2, _bd_t(b2))
    full = mm(a2.T, b2)
    return jnp.concatenate([full[:LANES, :LANES], full[LANES:, LANES:]], axis=1)


_MM1 = lambda x, y: _mm(x, y, NN)


def _pair_vjp_rule(kind, a2, b2, g, mm):
    if kind == NN:
        return _pair_prod(NT, g, b2, mm), _pair_prod(TN, a2, g, mm)
    if kind == NT:
        return _pair_prod(NN, g, b2, mm), _pair_prod(TN, g, a2, mm)
    return _pair_prod(NT, b2, g, mm), _pair_prod(NN, a2, g, mm)


@functools.partial(jax.custom_vjp, nondiff_argnums=(0,))
def _pdot(kind, a2, b2):
    return _pair_prod(kind, a2, b2, _MM1)


def _pdot_fwd(kind, a2, b2):
    return _pdot(kind, a2, b2), (a2, b2)


def _pdot_bwd(kind, res, g):
    return _pair_vjp_rule(kind, *res, g, _MM1)


_pdot.defvjp(_pdot_fwd, _pdot_bwd)


def _unit_lower_inverse(lower):
    n = lower.shape[0]
    nk = -lower
    inv = jnp.where(_iota(lower.shape, 0) == jnp.bitwise_and(_iota(lower.shape, 1), n - 1), 1.0, 0.0) + nk
    for _ in range(5):
        nk = _pair_prod(NN, nk, nk, _MM1)
        inv = inv + _pair_prod(NN, inv, nk, _MM1)
    return inv


@jax.custom_vjp
def _solve_with(lower, inv, rhs):
    return _pair_prod(NN, inv, rhs, _mm2)


def _solve_with_fwd(lower, inv, rhs):
    x = _pair_prod(NN, inv, rhs, _mm2)
    return x, (inv, x)


def _solve_with_bwd(res, g):
    inv, x = res
    d_rhs = _pair_prod(TN, inv, g, _mm2)
    return -_pair_prod(NT, d_rhs, x, _MM1), jnp.zeros_like(inv), d_rhs


_solve_with.defvjp(_solve_with_fwd, _solve_with_bwd)


def _dn_local(q, k, v, g, beta, inv=None):
    shape = (LANES, 2 * LANES)
    row, col = _iota(shape, 0), jnp.bitwise_and(_iota(shape, 1), LANES - 1)
    same = (row // 64) == (col // 64)
    tri_incl = jnp.logical_and(same, col <= row)
    tri_strict = jnp.logical_and(same, col < row)
    first = row < 64
    gc = _chunk_sum("running", g)
    gl = _chunk_sum("total", g)
    diff = gc - jnp.concatenate([gc[:, :LANES].T, gc[:, LANES:].T], axis=1)
    decay = jnp.where(tri_incl, jnp.exp(jnp.where(tri_incl, diff, 0.0)), 0.0)
    egc = jnp.exp(gc)
    lower = jnp.where(tri_strict, beta * _pdot(NT, k, k) * decay, 0.0)
    if inv is None:
        inv = _unit_lower_inverse(lower)
    u_val = _solve_with(lower, inv, v * beta)
    w_dec = _solve_with(lower, inv, k * (beta * egc))
    qk = jnp.where(tri_incl, _pdot(NT, q, k) * decay, 0.0)
    q_dec = q * egc
    k_dec = k * jnp.exp(gl - gc)
    cd1 = jnp.exp(jnp.sum(jnp.where(first, g, 0.0), axis=0, keepdims=True))
    cd2 = jnp.exp(jnp.sum(jnp.where(first, 0.0, g), axis=0, keepdims=True))
    return (u_val, w_dec, qk, q_dec, k_dec, cd1, cd2), inv


def _dn_state(u_val, w_dec, qk, q_dec, k_dec, cd1, cd2, s0):
    first = _iota((LANES, 2 * LANES), 0) < 64
    u1 = u_val - _pdot(NN, w_dec, s0)
    s1 = s0 * cd1 + _pdot(TN, jnp.where(first, k_dec, 0.0), u1)
    u2 = u_val - _pdot(NN, w_dec, s1)
    u_new = jnp.where(first, u1, u2)
    s2 = s1 * cd2 + _pdot(TN, jnp.where(first, 0.0, k_dec), u_new)
    o = jnp.where(first, _pdot(NN, q_dec, s0), _pdot(NN, q_dec, s1)) + _pdot(NN, qk, u_new)
    return o, s2


def _dn_post(o, z, gain):
    return _rms(o, gain) * _silu(z)


_DN_L2 = (DN_SCALE, 1.0, None)
DN_HPS = 2
DN_BLK = 4 * DN_HPS * LANES
_DN_COLS = tuple(slice(i * LANES, (i + 1) * LANES) for i in range(DN_HPS))


def _dn_in_cols(s, i):
    return slice((s * DN_HPS + i) * LANES, (s * DN_HPS + i + 1) * LANES)


def _dn_taps(cw_ref, s, i):
    return tuple(cw_ref[t:t + 1, _dn_in_cols(s, i)] for t in range(4))


def _pair_rows(n):
    return pl.ds(pl.multiple_of(n * 128, 128), 128)


def _dn_gate_rows(gate, hp):
    head_a = _iota((1, DN_HPS * LANES), 1) < LANES
    h = DN_HPS * hp
    return (jnp.where(head_a, _lane_pick(gate, h), _lane_pick(gate, h + 1)),
            jnp.where(head_a, _lane_pick(gate, NH + h), _lane_pick(gate, NH + h + 1)))


def _dn_gate_cols(dg, db, hp):
    head_a = _iota((1, DN_HPS * LANES), 1) < LANES
    lane = _iota((1, LANES), 1)
    h = DN_HPS * hp
    out = 0.0
    for t, first in ((dg, h), (db, NH + h)):
        out = out + jnp.where(lane == first, jnp.sum(jnp.where(head_a, t, 0.0), axis=-1, keepdims=True), 0.0)
        out = out + jnp.where(lane == first + 1, jnp.sum(jnp.where(head_a, 0.0, t), axis=-1, keepdims=True), 0.0)
    return out


def _dn_in_specs(T):
    vec = pl.BlockSpec((1, LANES), lambda b, h: (0, 0))
    return [pl.BlockSpec((T, DN_BLK), lambda b, h: (b, h)), pl.BlockSpec((T, LANES), lambda b, h: (b, 0)),
            pl.BlockSpec((4, 3 * DN_HPS * LANES), lambda b, h: (0, h)), vec, vec, vec]


def _dn_fwd_call(proj_dn, proj_ab, conv_w, a_log, dt_bias, gain, B, T, gather=()):
    npair = T // 128
    ng = len(gather)
    nsteps = B * (NH // DN_HPS)

    def body(*refs):
        x_ref, ab_ref, cw_ref, alog_ref, dtb_ref, gain_ref = refs[:6]
        out_ref, q_s, k_s, v_s, o_s, gate_s, st_s, inv_s = refs[6 + ng:14 + ng]
        step_id = pl.program_id(0) * (NH // DN_HPS) + pl.program_id(1)
        if ng:
            send, forward, finish = _gather_phases(refs[6:6 + ng], refs[14 + ng:14 + 2 * ng], *refs[14 + 2 * ng:])
            pl.when(step_id == 0)(send)
            pl.when(step_id == nsteps - 1)(forward)
        hp = pl.program_id(1)
        for i, cs in enumerate(_DN_COLS):
            for s, (x_s, l2) in enumerate(zip((q_s, k_s, v_s), _DN_L2)):
                x_s[:, cs] = _dn_conv(x_ref[:, _dn_in_cols(s, i)], *_dn_taps(cw_ref, s, i), l2)
        gate_s[...] = _dn_gates(ab_ref[...], alog_ref[...], dtb_ref[...])

        def local_of(pair):
            r = _pair_rows(pair)
            loc, inv = _dn_local(q_s[r, :], k_s[r, :], v_s[r, :], *_dn_gate_rows(gate_s[r, :], hp))
            inv_s[0, 0, pair] = inv
            return loc

        def state_of(n, loc, state):
            st_s[0, 0, n] = state
            o, s2 = _dn_state(*loc, state)
            o_s[_pair_rows(n), :] = o
            return s2

        def step(n, carry):
            loc, state = carry
            return local_of(n + 1), state_of(n, loc, state)

        loc, state = lax.fori_loop(0, npair - 1, step, (local_of(0), jnp.zeros((LANES, DN_HPS * LANES), F32)))
        state_of(npair - 1, loc, state)
        for i, cs in enumerate(_DN_COLS):
            out_ref[:, cs] = _dn_post(o_s[:, cs], x_ref[:, _dn_in_cols(3, i)], gain_ref[...])
        if ng:
            pl.when(step_id == nsteps - 1)(finish)

    kept_specs, kept_shapes = _dn_kept(B, T)
    outs = pl.pallas_call(
        body, name="dn_fwd", grid=(B, NH // DN_HPS), in_specs=_dn_in_specs(T) + _any_specs(ng),
        out_specs=[pl.BlockSpec((T, DN_HPS * LANES), lambda b, h: (b, h), pipeline_mode=pl.Buffered(1))] + kept_specs + _any_specs(ng),
        out_shape=[jax.ShapeDtypeStruct((B * T, D), F32)] + kept_shapes + _gather_shapes(gather),
        scratch_shapes=_gather_sems(ng) if ng else [],
        compiler_params=_params(dimension_semantics=("arbitrary", "arbitrary")),
    )(proj_dn, proj_ab, conv_w, a_log, dt_bias, gain, *gather)
    return outs[0], outs[1:8], outs[8:]


def _dn_kept(B, T):
    one = pl.Buffered(1)
    npair, pairs = T // 128, NH // DN_HPS
    wide = pl.BlockSpec((T, DN_HPS * LANES), lambda b, h: (b, h), pipeline_mode=one)
    per_pair = pl.BlockSpec((1, 1, npair, LANES, DN_HPS * LANES), lambda b, h: (b, h, 0, 0, 0), pipeline_mode=one)
    specs = [wide] * 4 + [pl.BlockSpec((T, LANES), lambda b, h: (b, h), pipeline_mode=one)] + [per_pair] * 2
    shapes = ([jax.ShapeDtypeStruct((B * T, DN_W), F32)] * 4 + [jax.ShapeDtypeStruct((B * T, pairs * LANES), F32)]
              + [jax.ShapeDtypeStruct((B, pairs, npair, LANES, DN_HPS * LANES), F32)] * 2)
    return specs, shapes


def _dn_bwd_call(proj_dn, proj_ab, dmix, kept, conv_w, a_log, dt_bias, gain, B, T, swap=()):
    npair = T // 128
    ns = len(swap)
    nsteps = B * (NH // DN_HPS)

    def body(*refs):
        x_ref, ab_ref, cw_ref, alog_ref, dtb_ref, gain_ref, do_ref, q_s, k_s, v_s, o_ref, gate_s, st_s, inv_s = refs[:14]
        dx_ref, dab_ref, dcw_ref, dalog_ref, ddtb_ref, dgain_ref = refs[14 + ns:20 + ns]
        dgate_s, do_s = refs[20 + 2 * ns:22 + 2 * ns]
        b_i, hp = pl.program_id(0), pl.program_id(1)
        step_id = b_i * (NH // DN_HPS) + hp
        if ns:
            send, finish = _chip_swap_phases(refs[14:14 + ns], refs[20 + ns:20 + 2 * ns], *refs[22 + 2 * ns:])
            pl.when(step_id == 0)(send)

        def pair_in(r):
            return (q_s[r, :], k_s[r, :], v_s[r, :]) + _dn_gate_rows(gate_s[r, :], hp)

        zero_state = jnp.zeros((LANES, DN_HPS * LANES), F32)

        @pl.when(jnp.logical_and(b_i == 0, hp == 0))
        def _():
            dcw_ref[...] = jnp.zeros_like(dcw_ref)
            dalog_ref[...] = jnp.zeros_like(dalog_ref)
            ddtb_ref[...] = jnp.zeros_like(ddtb_ref)
            dgain_ref[...] = jnp.zeros_like(dgain_ref)

        for i, cs in enumerate(_DN_COLS):
            zc = _dn_in_cols(3, i)
            _, post_vjp = jax.vjp(_dn_post, o_ref[:, cs], x_ref[:, zc], gain_ref[...])
            do, dz, dgain = post_vjp(do_ref[:, cs])
            dx_ref[:, zc] = dz
            do_s[:, cs] = do
            dgain_ref[...] += dgain

        wide_cols = lambda s: slice(s * DN_HPS * LANES, (s + 1) * DN_HPS * LANES)

        def back_step(nn, dstate):
            n = npair - 1 - nn
            r = _pair_rows(n)
            inv = inv_s[0, 0, n]
            local = lambda q, k, v, g, beta, inv=inv: _dn_local(q, k, v, g, beta, inv)[0]
            loc, local_vjp = jax.vjp(local, *pair_in(r))
            _, state_vjp = jax.vjp(_dn_state, *loc, st_s[0, 0, n])
            *dloc, ds0 = state_vjp((do_s[r, :], dstate))
            dq, dk, dv, dg, db = local_vjp(tuple(dloc))
            dx_ref[r, wide_cols(0)], dx_ref[r, wide_cols(1)], dx_ref[r, wide_cols(2)] = dq, dk, dv
            dgate_s[r, :] = _dn_gate_cols(dg, db, hp)
            return ds0

        lax.fori_loop(0, npair, back_step, zero_state)

        for i, cs in enumerate(_DN_COLS):
            h = DN_HPS * hp + i
            for s, l2 in enumerate(_DN_L2):
                xc = _dn_in_cols(s, i)
                _, conv_vjp = jax.vjp(functools.partial(_dn_conv, l2_scale=l2), x_ref[:, xc], *_dn_taps(cw_ref, s, i))
                dx, *dw = conv_vjp(dx_ref[:, xc])
                dx_ref[:, xc] = dx
                for t in range(4):
                    dcw_ref[h + 4 * s, t:t + 1, :] += dw[t]
        _, gate_vjp = jax.vjp(_dn_gates, ab_ref[...], alog_ref[...], dtb_ref[...])
        dab, dalog, ddtb = gate_vjp(dgate_s[...])
        dalog_ref[...] += dalog
        ddtb_ref[...] += ddtb

        @pl.when(hp == 0)
        def _():
            dab_ref[...] = jnp.zeros_like(dab_ref)

        dab_ref[...] += dab
        if ns:
            pl.when(step_id == nsteps - 1)(finish)

    M = B * T
    one = pl.Buffered(1)
    vec = pl.BlockSpec((1, LANES), lambda b, h: (0, 0))
    wide = [pltpu.VMEM((T, DN_HPS * LANES), F32)]
    vec_shape = jax.ShapeDtypeStruct((1, LANES), F32)
    outs = pl.pallas_call(
        body, name="dn_bwd", grid=(B, NH // DN_HPS),
        in_specs=_dn_in_specs(T) + [pl.BlockSpec((T, DN_HPS * LANES), lambda b, h: (b, h))] + _dn_kept(B, T)[0]
        + _any_specs(ns),
        out_specs=[pl.BlockSpec((T, DN_BLK), lambda b, h: (b, h), pipeline_mode=one), pl.BlockSpec((T, LANES), lambda b, h: (b, 0)),
                   pl.BlockSpec((12, 4, LANES), lambda b, h: (0, 0, 0)), vec, vec, vec] + _any_specs(ns),
        out_shape=[jax.ShapeDtypeStruct((M, 4 * DN_W), F32), jax.ShapeDtypeStruct((M, LANES), F32),
                   jax.ShapeDtypeStruct((12, 4, LANES), F32), vec_shape, vec_shape, vec_shape] + _chip_swap_shapes(swap),
        scratch_shapes=[pltpu.VMEM((T, LANES), F32)] + wide + (_chip_swap_sems(ns) if ns else []),
        compiler_params=pltpu.CompilerParams(vmem_limit_bytes=VMEM_LIMIT_MAX, dimension_semantics=("arbitrary", "arbitrary")),
    )(proj_dn, proj_ab, conv_w, a_log, dt_bias, gain, dmix, *kept, *swap)
    return outs[:6], outs[6:]


SBQ = 256


def _group_rms(x, gain):
    first = _iota(x.shape, 1) < 64
    sq = x * x
    ss_a = jnp.sum(jnp.where(first, sq, 0.0), axis=-1, keepdims=True)
    ss_b = jnp.sum(jnp.where(first, 0.0, sq), axis=-1, keepdims=True)
    ms = jnp.where(first, ss_a, ss_b) * (1.0 / 64)
    return x * lax.rsqrt(ms + EPS) * gain


def _sb_stack(q):
    first = _iota((1, LANES), 1) < 64
    return jnp.concatenate([jnp.where(first, q, 0.0), jnp.where(first, 0.0, q)], axis=0)


def _sb_fold(acc):
    return jnp.where(_iota((1, LANES), 1) < 64, acc[:SBQ], acc[SBQ:])


def _sb_logs(q2, k, diag):
    n = SBQ
    z = _mm(q2, k, ((1,), (1,))) * SB_SCALE
    ls_pos = jnp.minimum(z, 0.0) - jnp.log(1.0 + jnp.exp(-jnp.abs(z)))
    l1m = ls_pos - z
    if not diag:
        return ls_pos, l1m, None
    mask = _iota((2 * n, n), 1) < jnp.bitwise_and(_iota((2 * n, n), 0), n - 1)
    return ls_pos, jnp.where(mask, l1m, 0.0), mask


def _sb_weights(ls_pos, l1m, mask, carry):
    w = jnp.exp(ls_pos + (_mm_ones(_tri_ones(SBQ, False), l1m, False) + carry))
    return w if mask is None else jnp.where(mask, w, 0.0)


def _sb_block(q, k, v, carry, diag):
    ls_pos, l1m, mask = _sb_logs(_sb_stack(q), k, diag)
    w = _sb_weights(ls_pos, l1m, mask, carry)
    return _mm(w, v, ((1,), (0,))), carry + jnp.sum(l1m, axis=-1, keepdims=True), _sb_sum_as_rows(l1m)


SB_ROWS = 16


def _sb_sum_as_rows(l1m):
    ones = jnp.ones((SB_ROWS, SBQ), BF16)
    hi, lo = _split(l1m)
    mm = lambda t: lax.dot_general(ones, t, (NT, ((), ())), preferred_element_type=F32)
    return mm(hi) + mm(lo)


def _sb_rows_as_column(rows):
    pick = jnp.where(_iota((SB_ROWS, SBQ), 0) == 0, 1.0, 0.0).astype(BF16)
    hi = rows.astype(BF16)
    rest = rows - hi.astype(F32)
    mid = rest.astype(BF16)
    lo = (rest - mid.astype(F32)).astype(BF16)
    mm = lambda t: lax.dot_general(t, pick, (TN, ((), ())), preferred_element_type=F32)
    return mm(hi) + (mm(mid) + mm(lo))


def _sb_block_bwd(q, k, v, carry, diag, dpv, dcarry):
    q2 = _sb_stack(q)
    ls_pos, l1m, mask = _sb_logs(q2, k, diag)
    w = _sb_weights(ls_pos, l1m, mask, carry)
    dv = _mm(w, dpv, ((0,), (0,)))
    de = _mm(dpv, v, ((1,), (1,))) * w
    dl1m = jnp.dot(de.astype(BF16), _tri_ones(SBQ, True), preferred_element_type=F32) + dcarry
    if mask is not None:
        dl1m = jnp.where(mask, dl1m, 0.0)
    sig = jnp.exp(ls_pos)
    dz = (de * (1.0 - sig) - dl1m * sig) * SB_SCALE
    dq = _sb_fold(_mm(dz, k, ((1,), (0,))))
    return dq, _mm(dz, q2, ((0,), (0,))), dv, dcarry + jnp.sum(de, axis=-1, keepdims=True)


_SB_Q, _SB_K, _SB_V = (slice(i * LANES, (i + 1) * LANES) for i in range(3))


def _sb_fwd_call(proj_sb, mix, q_gain, k_gain, B, T, gather=()):
    nblk = T // SBQ
    ng = len(gather)
    nsteps = 2 * B

    def body(*refs):
        x_ref, qg_ref, kg_ref = refs[:3]
        out_ref, carry_ref = refs[4 + ng:6 + ng]
        q_s, k_s = refs[6 + 2 * ng:8 + 2 * ng]
        step_id = 2 * pl.program_id(0) + pl.program_id(1)
        if ng:
            send, forward, finish = _gather_phases(refs[4:4 + ng], refs[6 + ng:6 + 2 * ng], *refs[8 + 2 * ng:])
            pl.when(step_id == 0)(send)
            pl.when(step_id == nsteps - 1)(forward)
        q_s[...] = _group_rms(x_ref[:, _SB_Q], qg_ref[...])
        k_s[...] = _group_rms(x_ref[:, _SB_K], kg_ref[...])

        def qblock(i, _):
            ri = pl.ds(pl.multiple_of(i * SBQ, SBQ), SBQ)
            q = q_s[ri, :]

            def kblock(jj, c):
                j = i - 1 - jj
                rj = pl.ds(pl.multiple_of(j * SBQ, SBQ), SBQ)
                carry_ref[0, 0, i, j] = c[2]
                pv, carry, rows = _sb_block(q, k_s[rj, :], x_ref[rj, _SB_V], c[1], False)
                return c[0] + pv, carry, c[2] + rows

            on_diag = _sb_block(q, k_s[ri, :], x_ref[ri, _SB_V], jnp.zeros((2 * SBQ, 1), F32), True)
            acc, _c, _r = lax.fori_loop(0, i, kblock, on_diag)
            out_ref[ri, :] = _sb_fold(acc)
            return 0

        lax.fori_loop(0, nblk, qblock, 0)
        if ng:
            pl.when(step_id == nsteps - 1)(finish)

    vec = pl.BlockSpec((1, LANES), lambda b, p: (0, 0))
    outs = pl.pallas_call(
        body, name="sb_fwd", grid=(B, 2),
        in_specs=[pl.BlockSpec((T, 3 * LANES), lambda b, p: (b, p)), vec, vec, pl.BlockSpec(memory_space=pl.ANY)] + _any_specs(ng),
        out_specs=[pl.BlockSpec((T, LANES), lambda b, p: (b, DN_W // LANES + p)), _sb_carry_spec(nblk)] + _any_specs(ng),
        out_shape=[jax.ShapeDtypeStruct((B * T, D), F32), jax.ShapeDtypeStruct((B, 2, nblk, nblk, SB_ROWS, 2 * SBQ), F32)]
        + _gather_shapes(gather), input_output_aliases={3: 0},
        scratch_shapes=[pltpu.VMEM((T, LANES), F32)] * 2 + (_gather_sems(ng) if ng else []),
        compiler_params=_params(dimension_semantics=("arbitrary", "arbitrary")),
    )(proj_sb, q_gain, k_gain, mix, *gather)
    return outs[0], outs[1], outs[2:]


def _sb_carry_spec(nblk):
    return pl.BlockSpec((1, 1, nblk, nblk, SB_ROWS, 2 * SBQ), lambda b, p: (b, p, 0, 0, 0, 0))


def _sb_bwd_call(proj_sb, dmix, carries, q_gain, k_gain, B, T):
    nblk = T // SBQ

    def body(x_ref, qg_ref, kg_ref, do_ref, carry_ref, dx_ref, dqg_ref, dkg_ref, q_s, k_s, dq_s, dk_s, dv_s):
        b_i, p = pl.program_id(0), pl.program_id(1)
        qn, q_vjp = jax.vjp(_group_rms, x_ref[:, _SB_Q], qg_ref[...])
        kn, k_vjp = jax.vjp(_group_rms, x_ref[:, _SB_K], kg_ref[...])
        q_s[...], k_s[...] = qn, kn
        dk_s[...] = jnp.zeros_like(dk_s)
        dv_s[...] = jnp.zeros_like(dv_s)

        def qblock(i, _):
            ri = pl.ds(pl.multiple_of(i * SBQ, SBQ), SBQ)
            q = q_s[ri, :]
            dacc = _sb_stack(do_ref[ri, :])

            def kblock(j, c):
                rj = pl.ds(pl.multiple_of(j * SBQ, SBQ), SBQ)
                carry = _sb_rows_as_column(carry_ref[0, 0, i, j])
                dq_j, dk_j, dv_j, dc = _sb_block_bwd(q, k_s[rj, :], x_ref[rj, _SB_V], carry, False, dacc, c[1])
                dk_s[rj, :] += dk_j
                dv_s[rj, :] += dv_j
                return c[0] + dq_j, dc

            dq, dc = lax.fori_loop(0, i, kblock, (jnp.zeros((SBQ, LANES), F32), jnp.zeros((2 * SBQ, 1), F32)))
            dq_i, dk_i, dv_i, _dc = _sb_block_bwd(q, k_s[ri, :], x_ref[ri, _SB_V], jnp.zeros((2 * SBQ, 1), F32), True, dacc, dc)
            dk_s[ri, :] += dk_i
            dv_s[ri, :] += dv_i
            dq_s[ri, :] = dq + dq_i
            return 0

        lax.fori_loop(0, nblk, qblock, 0)
        dq_in, dqg = q_vjp(dq_s[...])
        dk_in, dkg = k_vjp(dk_s[...])
        dx_ref[:, _SB_Q], dx_ref[:, _SB_K], dx_ref[:, _SB_V] = dq_in, dk_in, dv_s[...]

        @pl.when(jnp.logical_and(b_i == 0, p == 0))
        def _():
            dqg_ref[...] = jnp.zeros_like(dqg_ref)
            dkg_ref[...] = jnp.zeros_like(dkg_ref)

        dqg_ref[...] += dqg + pltpu.roll(dqg, 64, 1)
        dkg_ref[...] += dkg + pltpu.roll(dkg, 64, 1)

    M = B * T
    vec = pl.BlockSpec((1, LANES), lambda b, p: (0, 0))
    blk = pl.BlockSpec((T, 3 * LANES), lambda b, p: (b, p))
    big = [pltpu.VMEM((T, LANES), F32)]
    return pl.pallas_call(
        body, name="sb_bwd", grid=(B, 2),
        in_specs=[blk, vec, vec, pl.BlockSpec((T, LANES), lambda b, p: (b, DN_W // LANES + p)), _sb_carry_spec(nblk)],
        out_specs=[blk, vec, vec],
        out_shape=[jax.ShapeDtypeStruct((M, 3 * SB_W), F32)] + [jax.ShapeDtypeStruct((1, LANES), F32)] * 2,
        scratch_shapes=big * 5,
        compiler_params=_params(dimension_semantics=("arbitrary", "arbitrary")),
    )(proj_sb, q_gain, k_gain, dmix, carries)


def _sg_chunk(u, v, gain, w_a, w_b, bias):
    n = 128
    row, col = _iota((n, n), 0), _iota((n, n), 1)
    first = _iota((1, LANES), 1) < 64
    vn = _group_rms(_gelu(v), gain)
    tril = col <= row
    mixed = jnp.where(first, _dot(jnp.where(tril, w_a, 0.0), vn), _dot(jnp.where(tril, w_b, 0.0), vn)) + bias
    return _gelu(u) * mixed


_SG_U, _SG_V = slice(0, LANES), slice(LANES, 2 * LANES)


def _sg_fwd_call(proj_sg, mix, gain, sg_w, bias, B, T):
    nchunk = T // 128

    def body(x_ref, g_ref, wa_ref, wb_ref, bias_ref, mix_ref, out_ref):
        del mix_ref

        def step(i, _):
            r = pl.ds(pl.multiple_of(i * 128, 128), 128)
            out_ref[r, :] = _sg_chunk(x_ref[r, _SG_U], x_ref[r, _SG_V], g_ref[...], wa_ref[0], wb_ref[0], bias_ref[...])
            return 0

        lax.fori_loop(0, nchunk, step, 0)

    return pl.pallas_call(
        body, name="sg_fwd", grid=(B, 2),
        in_specs=[pl.BlockSpec((T, 2 * LANES), lambda b, p: (b, p)), pl.BlockSpec((1, LANES), lambda b, p: (0, p)),
                  pl.BlockSpec((1, 128, 128), lambda b, p: (2 * p, 0, 0)), pl.BlockSpec((1, 128, 128), lambda b, p: (2 * p + 1, 0, 0)),
                  pl.BlockSpec((128, LANES), lambda b, p: (0, p)), pl.BlockSpec(memory_space=pl.ANY)],
        out_specs=pl.BlockSpec((T, LANES), lambda b, p: (b, (DN_W + SB_W) // LANES + p)),
        out_shape=jax.ShapeDtypeStruct((B * T, D), F32), input_output_aliases={5: 0},
        compiler_params=_params(dimension_semantics=("arbitrary", "arbitrary")),
    )(proj_sg, gain, sg_w, sg_w, bias, mix)


def _sg_bwd_call(proj_sg, dmix, gain, sg_w, bias, B, T):
    nchunk = T // 128

    def body(x_ref, g_ref, wa_ref, wb_ref, bias_ref, do_ref, dx_ref, dg_ref, dw_ref, db_ref):
        p, b_i = pl.program_id(0), pl.program_id(1)

        def step(i, c):
            r = pl.ds(pl.multiple_of(i * 128, 128), 128)
            _, vjp = jax.vjp(_sg_chunk, x_ref[r, _SG_U], x_ref[r, _SG_V], g_ref[...], wa_ref[0], wb_ref[0], bias_ref[...])
            du, dv, dg, dwa, dwb, dbias = vjp(do_ref[r, :])
            dx_ref[r, _SG_U], dx_ref[r, _SG_V] = du, dv
            return c[0] + dg, c[1] + dwa, c[2] + dwb, c[3] + dbias

        z = jnp.zeros((128, 128), F32)
        dg, dwa, dwb, dbias = lax.fori_loop(0, nchunk, step, (jnp.zeros((1, LANES), F32), z, z, z))
        lane = _iota((1, LANES), 1)
        first = lane < 64
        s_a = jnp.sum(jnp.where(first, dbias, 0.0), axis=-1, keepdims=True)
        s_b = jnp.sum(jnp.where(first, 0.0, dbias), axis=-1, keepdims=True)
        dbg = jnp.where(lane == 2 * p, s_a, 0.0) + jnp.where(lane == 2 * p + 1, s_b, 0.0)

        @pl.when(b_i == 0)
        def _():
            dg_ref[...] = jnp.zeros_like(dg_ref)
            dw_ref[...] = jnp.zeros_like(dw_ref)

        @pl.when(jnp.logical_and(b_i == 0, p == 0))
        def _():
            db_ref[...] = jnp.zeros_like(db_ref)

        dg_ref[...] += dg
        dw_ref[0] += dwa
        dw_ref[1] += dwb
        db_ref[...] += dbg

    M = B * T
    blk = pl.BlockSpec((T, 2 * LANES), lambda p, b: (b, p))
    return pl.pallas_call(
        body, name="sg_bwd", grid=(2, B),
        in_specs=[blk, pl.BlockSpec((1, LANES), lambda p, b: (0, p)),
                  pl.BlockSpec((1, 128, 128), lambda p, b: (2 * p, 0, 0)), pl.BlockSpec((1, 128, 128), lambda p, b: (2 * p + 1, 0, 0)),
                  pl.BlockSpec((128, LANES), lambda p, b: (0, p)),
                  pl.BlockSpec((T, LANES), lambda p, b: (b, (DN_W + SB_W) // LANES + p))],
        out_specs=[blk, pl.BlockSpec((1, LANES), lambda p, b: (0, p)), pl.BlockSpec((2, 128, 128), lambda p, b: (p, 0, 0)),
                   pl.BlockSpec((128, LANES), lambda p, b: (0, 0))],
        out_shape=[jax.ShapeDtypeStruct((M, 2 * SG_W), F32), jax.ShapeDtypeStruct((1, SG_W), F32),
                   jax.ShapeDtypeStruct((4, 128, 128), F32), jax.ShapeDtypeStruct((128, LANES), F32)],
        compiler_params=_params(dimension_semantics=("arbitrary", "arbitrary")),
    )(proj_sg, gain, sg_w, sg_w, bias, dmix)


def _row_tile(m, most=512):
    return min(m, most)


def _in_proj_call(x, gain, wt):
    m = x.shape[0]
    tm = _row_tile(m, 1024)

    def body(x_ref, g_ref, wt_ref, *out_refs):
        h = _rms(x_ref[...], g_ref[...]).astype(BF16)
        out_refs[-1][...] = h
        for (off, width), out_ref in zip(SECTIONS, out_refs):
            out_ref[...] = lax.dot_general(h, wt_ref[off:off + width, :], (((1,), (1,)), ((), ())), preferred_element_type=F32)

    rows = lambda width: pl.BlockSpec((tm, width), lambda i: (i, 0))
    return pl.pallas_call(
        body, name="in_proj", grid=(m // tm,),
        in_specs=[rows(D), pl.BlockSpec((1, D), lambda i: (0, 0)),
                  pl.BlockSpec((NPACK, D), lambda i: (0, 0), pipeline_mode=pl.Buffered(1))],
        out_specs=[rows(w) for _, w in SECTIONS] + [rows(D)],
        out_shape=[jax.ShapeDtypeStruct((m, w), F32) for _, w in SECTIONS] + [jax.ShapeDtypeStruct((m, D), BF16)],
        compiler_params=_params(dimension_semantics=("arbitrary",)),
    )(x, gain, wt)


def _in_proj_bwd_call(dsections, wt, x, gain, dres, swap=()):
    m = x.shape[0]
    tm = _row_tile(m)
    nsec, ns = len(SECTIONS), len(swap)

    def body(*refs):
        ds_refs = refs[:nsec]
        wt_ref, x_ref, g_ref, dres_ref = refs[nsec:nsec + 4]
        dx_ref, dg_ref = refs[nsec + 4 + ns:nsec + 6 + ns]
        step = pl.program_id(0)
        if ns:
            send, finish = _chip_swap_phases(refs[nsec + 4:nsec + 4 + ns], refs[nsec + 6 + ns:nsec + 6 + 2 * ns],
                                             *refs[nsec + 6 + 2 * ns:])
            pl.when(step == 0)(send)

        @pl.when(step == 0)
        def _():
            dg_ref[...] = jnp.zeros_like(dg_ref)

        dh = 0.0
        for (off, width), ds_ref in zip(SECTIONS, ds_refs):
            dh = dh + jnp.dot(ds_ref[...].astype(BF16), wt_ref[off:off + width, :], preferred_element_type=F32)
        _, vjp = jax.vjp(_rms, x_ref[...], g_ref[...])
        dx, dg = vjp(dh)
        dx_ref[...] = dres_ref[...] + dx
        dg_ref[...] += dg
        if ns:
            pl.when(step == m // tm - 1)(finish)

    rows = lambda width: pl.BlockSpec((tm, width), lambda i: (i, 0))
    outs = pl.pallas_call(
        body, name="in_proj_bwd", grid=(m // tm,),
        in_specs=[rows(w) for _, w in SECTIONS] + [pl.BlockSpec((NPACK, D), lambda i: (0, 0), pipeline_mode=pl.Buffered(1)),
                                                   rows(D), pl.BlockSpec((1, D), lambda i: (0, 0)), rows(D)] + _any_specs(ns),
        out_specs=[rows(D), pl.BlockSpec((1, D), lambda i: (0, 0))] + _any_specs(ns),
        out_shape=[jax.ShapeDtypeStruct((m, D), F32), jax.ShapeDtypeStruct((1, D), F32)] + _chip_swap_shapes(swap),
        scratch_shapes=_chip_swap_sems(ns) if ns else [],
        compiler_params=_params(dimension_semantics=("arbitrary",)),
    )(*dsections, wt, x, gain, dres, *swap)
    return outs[:2], outs[2:]


def _in_proj_grad_call(dsections, h):
    m = h.shape[0]
    tm = _row_tile(m)

    def body(*refs):
        ds_refs, (h_ref, out_ref) = refs[:len(SECTIONS)], refs[len(SECTIONS):]

        @pl.when(pl.program_id(0) == 0)
        def _():
            out_ref[...] = jnp.zeros_like(out_ref)

        for (off, width), ds_ref in zip(SECTIONS, ds_refs):
            out_ref[off:off + width, :] += lax.dot_general(ds_ref[...].astype(BF16), h_ref[...], (((0,), (0,)), ((), ())),
                                                           preferred_element_type=F32)

    rows = lambda width: pl.BlockSpec((tm, width), lambda i: (i, 0))
    return pl.pallas_call(
        body, name="grad_w_in", grid=(m // tm,),
        in_specs=[rows(w) for _, w in SECTIONS] + [rows(D)],
        out_specs=pl.BlockSpec((NPACK, D), lambda i: (0, 0), pipeline_mode=pl.Buffered(1)),
        out_shape=jax.ShapeDtypeStruct((NPACK, D), F32),
        compiler_params=_params(dimension_semantics=("arbitrary",)),
    )(*dsections, h)


def _packed_column_of():
    t = np.full(NPACK, -1, np.int64)
    lanes = np.arange(LANES)
    for pair in range(2):
        for s in range(4):
            t[DN_OFF + pair * 1024 + s * 256 + np.arange(256)] = s * DN_W + pair * 256 + np.arange(256)
        for s in range(3):
            t[SB_OFF + pair * 384 + s * LANES + lanes] = 2056 + s * SB_W + pair * LANES + lanes
        for s in range(2):
            t[SG_OFF + pair * 256 + s * LANES + lanes] = 2056 + 3 * SB_W + s * SG_W + pair * LANES + lanes
    t[AB_OFF + np.arange(2 * NH)] = 4 * DN_W + np.arange(2 * NH)
    return t


def _row_tables():
    col = _packed_column_of()
    fwd = np.where(col >= 0, (col // IN_SHARD) * IN_SHARD_PAD + col % IN_SHARD, -1)
    packed_of = np.full(IN_DIM, -1, np.int64)
    packed_of[col[col >= 0]] = np.nonzero(col >= 0)[0]
    r = np.arange(NDEV * IN_SHARD_PAD)
    inside = r % IN_SHARD_PAD < IN_SHARD
    back = np.where(inside, packed_of[np.minimum((r // IN_SHARD_PAD) * IN_SHARD + r % IN_SHARD_PAD, IN_DIM - 1)], -1)
    return fwd, back


def _row_perm_call(src, table, name):
    n_out = table.shape[0]
    touched = [sorted(set((table[b * 128:(b + 1) * 128][table[b * 128:(b + 1) * 128] >= 0] // 128).tolist()))
               for b in range(n_out // 128)]

    def body(tbl_ref, src_ref, out_ref):
        lane = _iota((1, LANES), 1)
        for b, blocks in enumerate(touched):
            want = tbl_ref[b * 128:(b + 1) * 128, :]
            acc = jnp.zeros((128, D), F32)
            for sb in blocks:
                pick = jnp.where(want == sb * 128 + lane, 1.0, 0.0).astype(BF16)
                acc = acc + jnp.dot(pick, src_ref[sb * 128:(sb + 1) * 128, :].astype(BF16), preferred_element_type=F32)
            out_ref[b * 128:(b + 1) * 128, :] = acc.astype(BF16)

    return pl.pallas_call(
        body, name=name, out_shape=jax.ShapeDtypeStruct((n_out, D), BF16),
        in_specs=[pl.BlockSpec(memory_space=pltpu.VMEM)] * 2, out_specs=pl.BlockSpec(memory_space=pltpu.VMEM),
        compiler_params=_params(),
    )(jnp.asarray(table.reshape(-1, 1), jnp.int32), src)


def _out_proj_call(a, w, res):
    m, k = a.shape
    n = w.shape[1]
    tm = _row_tile(m)

    def body(a_ref, w_ref, res_ref, out_ref):
        out_ref[...] = res_ref[...] + jnp.dot(a_ref[...].astype(BF16), w_ref[...], preferred_element_type=F32)

    return pl.pallas_call(
        body, name="out_proj", grid=(m // tm,),
        in_specs=[pl.BlockSpec((tm, k), lambda i: (i, 0)), pl.BlockSpec((k, n), lambda i: (0, 0)),
                  pl.BlockSpec((tm, n), lambda i: (i, 0))],
        out_specs=pl.BlockSpec((tm, n), lambda i: (i, 0)),
        out_shape=jax.ShapeDtypeStruct((m, n), F32),
        compiler_params=_params(dimension_semantics=("arbitrary",)),
    )(a, w, res)


def _ffn_specs(tm):
    return [pl.BlockSpec((1, D, FF_SHARD), lambda i, j: (j, 0, 0)), pl.BlockSpec((FF_SHARD, D), lambda i, j: (j, 0))]


def _ffn_fwd_call(x, gain, w1, w2):
    m = x.shape[0]
    tm = _row_tile(m, 1024)

    def body(x_ref, g_ref, w1_ref, w2_ref, out_ref, h_s, acc_s):
        j = pl.program_id(1)

        @pl.when(j == 0)
        def _():
            h_s[...] = _rms(x_ref[...], g_ref[...]).astype(BF16)
            acc_s[...] = jnp.zeros_like(acc_s)

        a = jnp.maximum(jnp.dot(h_s[...], w1_ref[0], preferred_element_type=F32), 0.0)
        acc_s[...] += jnp.dot((a * a).astype(BF16), w2_ref[...], preferred_element_type=F32)

        @pl.when(j == NDEV - 1)
        def _():
            out_ref[...] = x_ref[...] + acc_s[...]

    return pl.pallas_call(
        body, name="ffn_fwd", grid=(m // tm, NDEV),
        in_specs=[pl.BlockSpec((tm, D), lambda i, j: (i, 0)), pl.BlockSpec((1, D), lambda i, j: (0, 0))] + _ffn_specs(tm),
        out_specs=pl.BlockSpec((tm, D), lambda i, j: (i, 0)),
        out_shape=jax.ShapeDtypeStruct((m, D), F32),
        scratch_shapes=[pltpu.VMEM((tm, D), BF16), pltpu.VMEM((tm, D), F32)],
        compiler_params=_params(dimension_semantics=("arbitrary", "arbitrary")),
    )(x, gain, w1, w2)


def _ffn_bwd_call(x, dy, gain, w1, w2, swap=()):
    m = x.shape[0]
    tm = _row_tile(m, 1024)
    ns = len(swap)

    def body(*refs):
        x_ref, dy_ref, g_ref, w1_ref, w2_ref = refs[:5]
        dx_ref, da_ref, r_ref, h_ref, dg_ref = refs[5 + ns:10 + ns]
        acc_s = refs[10 + 2 * ns]
        i, j = pl.program_id(0), pl.program_id(1)
        if ns:
            send, finish = _sibling_swap_phases(refs[5:5 + ns], refs[10 + ns:10 + 2 * ns], *refs[11 + 2 * ns:])
            pl.when(jnp.logical_and(i == 0, j == 0))(send)

        @pl.when(j == 0)
        def _():
            h_ref[...] = _rms(x_ref[...], g_ref[...]).astype(BF16)
            acc_s[...] = jnp.zeros_like(acc_s)

        @pl.when(jnp.logical_and(i == 0, j == 0))
        def _():
            dg_ref[...] = jnp.zeros_like(dg_ref)

        a = jnp.maximum(jnp.dot(h_ref[...], w1_ref[0], preferred_element_type=F32), 0.0)
        r_ref[...] = (a * a).astype(BF16)
        dr = lax.dot_general(dy_ref[...].astype(BF16), w2_ref[...], (((1,), (1,)), ((), ())), preferred_element_type=F32)
        da = (dr * (2.0 * a)).astype(BF16)
        da_ref[...] = da
        acc_s[...] += lax.dot_general(da, w1_ref[0], (((1,), (1,)), ((), ())), preferred_element_type=F32)

        @pl.when(j == NDEV - 1)
        def _():
            _, vjp = jax.vjp(_rms, x_ref[...], g_ref[...])
            dx, dg = vjp(acc_s[...])
            dx_ref[...] = dy_ref[...] + dx
            dg_ref[...] += dg

        if ns:
            pl.when(jnp.logical_and(i == m // tm - 1, j == NDEV - 1))(finish)

    outs = pl.pallas_call(
        body, name="ffn_bwd", grid=(m // tm, NDEV),
        in_specs=[pl.BlockSpec((tm, D), lambda i, j: (i, 0)), pl.BlockSpec((tm, D), lambda i, j: (i, 0)),
                  pl.BlockSpec((1, D), lambda i, j: (0, 0))] + _ffn_specs(tm) + _any_specs(ns),
        out_specs=[pl.BlockSpec((tm, D), lambda i, j: (i, 0)), pl.BlockSpec((tm, FF_SHARD), lambda i, j: (i, j)),
                   pl.BlockSpec((tm, FF_SHARD), lambda i, j: (i, j)), pl.BlockSpec((tm, D), lambda i, j: (i, 0)),
                   pl.BlockSpec((1, D), lambda i, j: (0, 0))] + _any_specs(ns),
        out_shape=[jax.ShapeDtypeStruct((m, D), F32), jax.ShapeDtypeStruct((m, DFF), BF16), jax.ShapeDtypeStruct((m, DFF), BF16),
                   jax.ShapeDtypeStruct((m, D), BF16), jax.ShapeDtypeStruct((1, D), F32)] + _sibling_swap_shapes(swap),
        scratch_shapes=[pltpu.VMEM((tm, D), F32)] + (_sibling_swap_sems(ns) if ns else []),
        compiler_params=_params(dimension_semantics=("arbitrary", "arbitrary")),
    )(x, dy, gain, w1, w2, *swap)
    return outs[:5], outs[5:]


def _mm_nt_call(a, b, name):
    m, k = a.shape
    n = b.shape[0]
    tm = _row_tile(m)

    def body(a_ref, b_ref, out_ref):
        out_ref[...] = lax.dot_general(a_ref[...].astype(BF16), b_ref[...].astype(BF16), (((1,), (1,)), ((), ())),
                                       preferred_element_type=F32)

    return pl.pallas_call(
        body, name=name, grid=(m // tm,),
        in_specs=[pl.BlockSpec((tm, k), lambda i: (i, 0)), pl.BlockSpec((n, k), lambda i: (0, 0))],
        out_specs=pl.BlockSpec((tm, n), lambda i: (i, 0)),
        out_shape=jax.ShapeDtypeStruct((m, n), F32),
        compiler_params=_params(dimension_semantics=("arbitrary",)),
    )(a, b)


def _mm_tn_call(a, b, name, col_shards=False):
    m, k = a.shape
    n = b.shape[1]
    tm, tk = _row_tile(m, 1024), min(k, 1024)
    tn = n // NDEV if col_shards else min(n, 1024)

    def body(a_ref, b_ref, out_ref, acc_s):
        s = pl.program_id(2)

        @pl.when(s == 0)
        def _():
            acc_s[...] = jnp.zeros_like(acc_s)

        acc_s[...] += lax.dot_general(a_ref[...].astype(BF16), b_ref[...].astype(BF16), (((0,), (0,)), ((), ())),
                                      preferred_element_type=F32)

        @pl.when(s == m // tm - 1)
        def _():
            out_ref[...] = acc_s[...].astype(BF16).reshape(out_ref.shape)

    if col_shards:
        out_spec, out_shape = pl.BlockSpec((1, tk, tn), lambda i, j, s: (j, i, 0)), (NDEV, k, tn)
    else:
        out_spec, out_shape = pl.BlockSpec((tk, tn), lambda i, j, s: (i, j)), (k, n)
    return pl.pallas_call(
        body, name=name, grid=(k // tk, n // tn, m // tm),
        in_specs=[pl.BlockSpec((tm, tk), lambda i, j, s: (s, i)), pl.BlockSpec((tm, tn), lambda i, j, s: (s, j))],
        out_specs=out_spec, out_shape=jax.ShapeDtypeStruct(out_shape, BF16),
        scratch_shapes=[pltpu.VMEM((tk, tn), F32)],
        compiler_params=_params(dimension_semantics=("arbitrary", "arbitrary", "arbitrary")),
    )(a, b)


def _loss_call(y, target):
    m = y.shape[0]
    tm = _row_tile(m)

    def body(y_ref, t_ref, loss_ref, dy_ref):
        @pl.when(pl.program_id(0) == 0)
        def _():
            loss_ref[...] = jnp.zeros_like(loss_ref)

        err = y_ref[...] - t_ref[...]
        dy_ref[...] = err * (1.0 / D)
        per_row = jnp.mean(err * err, axis=-1, keepdims=True)
        loss_ref[...] += jnp.broadcast_to(0.5 * jnp.sum(per_row, axis=0, keepdims=True), (1, LANES))

    return pl.pallas_call(
        body, name="loss", grid=(m // tm,),
        in_specs=[pl.BlockSpec((tm, D), lambda i: (i, 0))] * 2,
        out_specs=[pl.BlockSpec((1, LANES), lambda i: (0, 0)), pl.BlockSpec((tm, D), lambda i: (i, 0))],
        out_shape=[jax.ShapeDtypeStruct((1, LANES), F32), jax.ShapeDtypeStruct((m, D), F32)],
        compiler_params=_params(dimension_semantics=("arbitrary",)),
    )(y, target)


def _adamw_call(w, g, m, v, name):
    shape = w.shape
    cols = shape[-1] if w.ndim > 1 else w.size
    rows = w.size // cols
    tr = rows if (rows <= 512 or rows % 512) else 512
    c1, c2 = 1.0 - ADAM_B1 ** ADAM_STEP, 1.0 - ADAM_B2 ** ADAM_STEP

    def body(w_ref, g_ref, m_ref, v_ref, d_ref, nm_ref, nv_ref):
        g_ = g_ref[...]
        nm = ADAM_B1 * m_ref[...] + (1.0 - ADAM_B1) * g_
        nv = ADAM_B2 * v_ref[...] + (1.0 - ADAM_B2) * (g_ * g_)
        d_ref[...] = -ADAM_LR * ((nm / c1) / (jnp.sqrt(nv / c2) + ADAM_EPS) + ADAM_WD * w_ref[...])
        nm_ref[...], nv_ref[...] = nm, nv

    spec = pl.BlockSpec((tr, cols), lambda i: (i, 0))
    outs = pl.pallas_call(
        body, name=name, grid=(rows // tr,), in_specs=[spec] * 4, out_specs=[spec] * 3,
        out_shape=[jax.ShapeDtypeStruct((rows, cols), F32)] * 3,
        compiler_params=_params(dimension_semantics=("arbitrary",)),
    )(*(t.reshape(rows, cols) for t in (w, g, m, v)))
    return tuple(o.reshape(shape) for o in outs)


def _sum_tile(rows):
    for cand in (2048, 1024, 512, 256, 128):
        if rows > cand and rows % cand == 0:
            return cand
    return rows


def _pair_sum_call(gs, gots, core, name):
    n = len(gs)

    def body(core_ref, *refs):
        del core_ref
        for g_ref, got_ref, out_ref in zip(refs[:n], refs[n:2 * n], refs[2 * n:]):
            out_ref[...] = (g_ref[...].astype(F32) + got_ref[...].astype(F32)).astype(BF16)

    block = lambda g: (1,) + g.shape[1:]
    grid_spec = pltpu.PrefetchScalarGridSpec(
        num_scalar_prefetch=1, grid=(4,),
        in_specs=[pl.BlockSpec(block(g), lambda ch, core_ref: (2 * ch + core_ref[0], 0, 0)) for g in gs]
        + [pl.BlockSpec(block(g), lambda ch, core_ref: (ch, 0, 0)) for g in gs],
        out_specs=[pl.BlockSpec(block(g), lambda ch, core_ref: (ch, 0, 0)) for g in gs])
    return pl.pallas_call(
        body, name=name, grid_spec=grid_spec, out_shape=[jax.ShapeDtypeStruct((4,) + g.shape[1:], BF16) for g in gs],
        compiler_params=_params(dimension_semantics=("arbitrary",)),
    )(jnp.asarray(core, jnp.int32).reshape(1), *gs, *gots)


def _total_sum_call(gs, gots, froms, me, my_chip, name):
    n = len(gs)

    def body(idx_ref, *refs):
        del idx_ref
        for a in range(n):
            g_ref, got_ref, f0, f1, f2 = (refs[k * n + a] for k in range(5))
            acc = g_ref[0].astype(F32) + got_ref[0].astype(F32)
            for f in (f0, f1, f2):
                acc = acc + f[0].astype(F32)
            refs[5 * n + a][...] = acc

    block = lambda g: (1,) + g.shape[1:]
    picked = lambda which: [pl.BlockSpec(block(g), lambda t, idx, which=which: (idx[which], 0, 0)) for g in gs]
    fixed = lambda j: [pl.BlockSpec(block(g), lambda t, idx, j=j: (j, 0, 0)) for g in gs]
    grid_spec = pltpu.PrefetchScalarGridSpec(
        num_scalar_prefetch=1, grid=(1,),
        in_specs=picked(0) + picked(1) + fixed(0) + fixed(1) + fixed(2),
        out_specs=[pl.BlockSpec(g.shape[1:], lambda t, idx: (0, 0)) for g in gs])
    return pl.pallas_call(
        body, name=name, grid_spec=grid_spec, out_shape=[jax.ShapeDtypeStruct(g.shape[1:], F32) for g in gs],
        compiler_params=_params(dimension_semantics=("arbitrary",)),
    )(jnp.stack([jnp.asarray(me, jnp.int32), jnp.asarray(my_chip, jnp.int32)]), *gs, *gots, *froms, *froms, *froms)


def _sum_call(parts, out_dtype, name):
    rows, cols = parts[0][0].shape[1:]
    tr = _sum_tile(rows)
    index = jnp.stack([jnp.asarray(i, jnp.int32) for _, i in parts])

    def body(idx_ref, *refs):
        del idx_ref
        acc = refs[0][0].astype(F32)
        for r in refs[1:-1]:
            acc = acc + r[0].astype(F32)
        refs[-1][...] = acc.astype(out_dtype)

    grid_spec = pltpu.PrefetchScalarGridSpec(
        num_scalar_prefetch=1, grid=(rows // tr,),
        in_specs=[pl.BlockSpec((1, tr, cols), lambda t, idx, n=n: (idx[n], t, 0)) for n in range(len(parts))],
        out_specs=pl.BlockSpec((tr, cols), lambda t, idx: (t, 0)))
    return pl.pallas_call(
        body, name=name, grid_spec=grid_spec, out_shape=jax.ShapeDtypeStruct((rows, cols), out_dtype),
        compiler_params=_params(dimension_semantics=("arbitrary",)),
    )(index, *(a for a, _ in parts))


def _place():
    return lax.axis_index("x"), lax.axis_index("y"), lax.axis_index("c")


def _any_specs(n):
    return [pl.BlockSpec(memory_space=pl.ANY)] * n


def _all_gather_call(xs, name):
    n = len(xs)

    def body(*refs):
        for phase in _gather_phases(refs[:n], refs[n:2 * n], *refs[2 * n:]):
            phase()

    return pl.pallas_call(
        body, name=name, in_specs=_any_specs(n), out_specs=_any_specs(n),
        out_shape=_gather_shapes(xs), scratch_shapes=_gather_sems(n),
    )(*xs)


def _gather_shapes(xs):
    return [jax.ShapeDtypeStruct((NDEV,) + x.shape, x.dtype) for x in xs]


def _gather_sems(n):
    return [pltpu.SemaphoreType.DMA((7 * n,)), pltpu.SemaphoreType.DMA((7 * n,)), pltpu.SemaphoreType.DMA((n,))]


def _gather_phases(x_refs, out_refs, send_sems, recv_sems, local_sems):
    n = len(x_refs)
    ax, ay, ac = _place()
    me, sibling = (ax, ay, ac), (ax, ay, 1 - ac)
    chips = [(1 - ax, ay), (ax, 1 - ay), (1 - ax, 1 - ay)]

    def copy(a, k, block, to, src=None):
        slot = out_refs[a].at[4 * block[0] + 2 * block[1] + block[2]]
        return pltpu.make_async_remote_copy(
            src_ref=slot if src is None else src, dst_ref=slot,
            send_sem=send_sems.at[7 * a + k], recv_sem=recv_sems.at[7 * a + k], device_id=to, device_id_type=MESH)

    local = [pltpu.make_async_copy(x_refs[a], out_refs[a].at[4 * ax + 2 * ay + ac], local_sems.at[a]) for a in range(n)]
    first = []
    for a in range(n):
        first.append(copy(a, 0, me, sibling, src=x_refs[a]))
        first += [copy(a, 1 + j, me, (*chip, ac), src=x_refs[a]) for j, chip in enumerate(chips)]
    passed = [copy(a, 4 + j, (*chip, ac), sibling) for j, chip in enumerate(chips) for a in range(n)]

    def send():
        for cp in local + first:
            cp.start()

    def forward():
        for j, chip in enumerate(chips):
            for a in range(n):
                copy(a, 1 + j, (*chip, ac), me).wait_recv()
                passed[j * n + a].start()

    def finish():
        for a in range(n):
            copy(a, 0, sibling, me).wait_recv()
            for j, chip in enumerate(chips):
                copy(a, 4 + j, (*chip, 1 - ac), me).wait_recv()
        for cp in first + passed:
            cp.wait_send()
        for cp in local:
            cp.wait()

    return send, forward, finish


def _swap_sibling_call(xs, name):
    n = len(xs)

    def body(*refs):
        for phase in _sibling_swap_phases(refs[:n], refs[n:2 * n], *refs[2 * n:]):
            phase()

    return pl.pallas_call(
        body, name=name, in_specs=_any_specs(n), out_specs=_any_specs(n),
        out_shape=_sibling_swap_shapes(xs), scratch_shapes=_sibling_swap_sems(n),
    )(*xs)


def _sibling_swap_shapes(xs):
    return [jax.ShapeDtypeStruct((4,) + x.shape[1:], x.dtype) for x in xs]


def _sibling_swap_sems(n):
    return [pltpu.SemaphoreType.DMA((n,)), pltpu.SemaphoreType.DMA((n,))]


def _sibling_swap_phases(x_refs, out_refs, send_sems, recv_sems):
    ax, ay, ac = _place()
    sibling = (ax, ay, 1 - ac)

    def send():
        for a, (x_ref, out_ref) in enumerate(zip(x_refs, out_refs)):
            for chip in range(4):
                pltpu.make_async_remote_copy(src_ref=x_ref.at[2 * chip + 1 - ac], dst_ref=out_ref.at[chip],
                                             send_sem=send_sems.at[a], recv_sem=recv_sems.at[a],
                                             device_id=sibling, device_id_type=MESH).start()

    def finish():
        for a, (x_ref, out_ref) in enumerate(zip(x_refs, out_refs)):
            pltpu.make_async_remote_copy(src_ref=x_ref.at[pl.ds(0, 4)], dst_ref=out_ref, send_sem=send_sems.at[a],
                                         recv_sem=recv_sems.at[a], device_id=sibling, device_id_type=MESH).wait()

    return send, finish


def _chip_swap_shapes(xs):
    return [jax.ShapeDtypeStruct((3,) + x.shape[1:], x.dtype) for x in xs]


def _chip_swap_sems(n):
    return [pltpu.SemaphoreType.DMA((3 * n,)), pltpu.SemaphoreType.DMA((3 * n,))]


def _chip_swap_phases(x_refs, out_refs, send_sems, recv_sems):
    ax, ay, ac = _place()
    chips = [(1 - ax, ay), (ax, 1 - ay), (1 - ax, 1 - ay)]
    copies = [pltpu.make_async_remote_copy(src_ref=x_refs[a].at[2 * cx + cy], dst_ref=out_refs[a].at[j],
                                           send_sem=send_sems.at[3 * a + j], recv_sem=recv_sems.at[3 * a + j],
                                           device_id=(cx, cy, ac), device_id_type=MESH)
              for a in range(len(x_refs)) for j, (cx, cy) in enumerate(chips)]

    def send():
        for cp in copies:
            cp.start()

    def finish():
        for cp in copies:
            cp.wait()

    return send, finish


def _reduce_begin(gs, name):
    got = _swap_sibling_call(gs, name + "_d2d")
    return got, _pair_sum_call(gs, got, lax.axis_index("c"), name + "_pair")


def _reduce_end(gs, got, from_chips, name):
    ax, ay, ac = _place()
    return _total_sum_call(gs, got, from_chips, 4 * ax + 2 * ay + ac, 2 * ax + ay, name + "_total")


SMALL = ("norm1_g", "conv_w", "a_log", "dt_bias", "dn_out_g", "sb_q_g", "sb_k_g", "sg_v_g", "sg_w", "sg_b", "norm2_g")
WEIGHTS = ("norm1_g", "w_in", "conv_w", "a_log", "dt_bias", "dn_out_g", "sb_q_g", "sb_k_g", "sg_v_g", "sg_w", "sg_b",
           "w_out", "norm2_g", "w_ff1", "w_ff2")
SMALL_SHAPE = {"norm1_g": (D,), "conv_w": (4, 3 * DN_W), "a_log": (NH,), "dt_bias": (NH,), "dn_out_g": (128,), "sb_q_g": (64,),
               "sb_k_g": (64,), "sg_v_g": (SG_W,), "sg_w": (NH, 128, 128), "sg_b": (NH, 128), "norm2_g": (D,)}


def _size(shape):
    n = 1
    for s in shape:
        n *= s
    return n


def _to_rows(flat, multiple):
    pad = (-flat.shape[0]) % (LANES * multiple)
    return jnp.pad(flat, (0, pad)).reshape(-1, LANES)


def _conv_by_pair(conv):
    return conv.reshape(4, 3, 2, 256).transpose(0, 2, 1, 3).reshape(4, 3 * DN_W)


def kernel(x, norm1_g, w_in, conv_w, a_log, dt_bias, dn_out_g, sb_q_g, sb_k_g, sg_v_g, sg_w, sg_b, w_out, norm2_g, w_ff1, w_ff2, loss_target, m_norm1_g, m_w_in, m_conv_w, m_a_log, m_dt_bias, m_dn_out_g, m_sb_q_g, m_sb_k_g, m_sg_v_g, m_sg_w, m_sg_b, m_w_out, m_norm2_g, m_w_ff1, m_w_ff2, v_norm1_g, v_w_in, v_conv_w, v_a_log, v_dt_bias, v_dn_out_g, v_sb_q_g, v_sb_k_g, v_sg_v_g, v_sg_w, v_sg_b, v_w_out, v_norm2_g, v_w_ff1, v_w_ff2):
    given = dict(norm1_g=norm1_g, w_in=w_in, conv_w=conv_w, a_log=a_log, dt_bias=dt_bias, dn_out_g=dn_out_g, sb_q_g=sb_q_g,
                 sb_k_g=sb_k_g, sg_v_g=sg_v_g, sg_w=sg_w, sg_b=sg_b, w_out=w_out, norm2_g=norm2_g, w_ff1=w_ff1, w_ff2=w_ff2)
    mom = dict(norm1_g=m_norm1_g, w_in=m_w_in, conv_w=m_conv_w, a_log=m_a_log, dt_bias=m_dt_bias, dn_out_g=m_dn_out_g,
               sb_q_g=m_sb_q_g, sb_k_g=m_sb_k_g, sg_v_g=m_sg_v_g, sg_w=m_sg_w, sg_b=m_sg_b, w_out=m_w_out, norm2_g=m_norm2_g,
               w_ff1=m_w_ff1, w_ff2=m_w_ff2)
    var = dict(norm1_g=v_norm1_g, w_in=v_w_in, conv_w=v_conv_w, a_log=v_a_log, dt_bias=v_dt_bias, dn_out_g=v_dn_out_g,
               sb_q_g=v_sb_q_g, sb_k_g=v_sb_k_g, sg_v_g=v_sg_v_g, sg_w=v_sg_w, sg_b=v_sg_b, w_out=v_w_out, norm2_g=v_norm2_g,
               w_ff1=v_w_ff1, w_ff2=v_w_ff2)
    B, T, _ = x.shape
    M = B * T
    ax, ay, ac = _place()
    me = 4 * ax + 2 * ay + ac
    table_fwd, table_back = _row_tables()

    send = []
    for l in range(2):
        w_in_t = jnp.pad(w_in[l].T, ((0, IN_SHARD_PAD - IN_SHARD), (0, 0)))
        send.append([w_in_t.astype(BF16), w_out[l].astype(BF16), w_ff1[l].astype(BF16), w_ff2[l].astype(BF16)])
    first_in, conv_rows = _all_gather_call([send[0][0], _to_rows(conv_w.reshape(-1), 8)], "gather_first")
    conv_full = conv_rows.reshape(NDEV, -1)[:, :conv_w.size].reshape(NDEV, 2, 4, -1).transpose(1, 2, 0, 3).reshape(2, 4, 3 * DN_W)
    gathered = [[first_in, None, None, None], [None] * 4]

    pad_vec = lambda v: jnp.zeros((1, LANES), F32).at[0, :v.shape[0]].set(v)
    layer = []
    for l in range(2):
        layer.append(dict(
            g1=norm1_g[l].reshape(1, D), g2=norm2_g[l].reshape(1, D), conv=_conv_by_pair(conv_full[l]),
            a_log=pad_vec(a_log[l]), dt_bias=pad_vec(dt_bias[l]), dn_g=dn_out_g[l].reshape(1, LANES),
            sb_qg=jnp.tile(sb_q_g[l], 2).reshape(1, LANES), sb_kg=jnp.tile(sb_k_g[l], 2).reshape(1, LANES),
            sg_g=sg_v_g[l].reshape(1, SG_W), sg_w=sg_w[l], sg_bias=jnp.repeat(sg_b[l].T, 64, axis=1)))

    cur = x.reshape(M, D)
    saved = []
    for l, p in enumerate(layer):
        p["wt"] = _row_perm_call(gathered[l][0].reshape(NDEV * IN_SHARD_PAD, D), table_fwd, "pack_w_in")
        p_dn, p_sb, p_sg, p_ab, h = _in_proj_call(cur, p["g1"], p["wt"])
        mix, dn_kept, arrived = _dn_fwd_call(p_dn, p_ab, p["conv"], p["a_log"], p["dt_bias"], p["dn_g"], B, T,
                                             gather=send[0][1:] + send[1][:1] if l == 0 else [])
        if l == 0:
            gathered[0][1:], gathered[1][0] = list(arrived[:3]), arrived[3]
        p["w_out"], p["w1"], p["w2"] = gathered[l][1].reshape(D, D), gathered[l][2], gathered[l][3].reshape(DFF, D)
        mix, sb_carries, arrived = _sb_fwd_call(p_sb, mix, p["sb_qg"], p["sb_kg"], B, T, gather=send[1][1:] if l == 0 else [])
        if l == 0:
            gathered[1][1:] = list(arrived)
        mix = _sg_fwd_call(p_sg, mix, p["sg_g"], p["sg_w"], p["sg_bias"], B, T)
        x1 = _out_proj_call(mix, p["w_out"], cur)
        x2 = _ffn_fwd_call(x1, p["g2"], p["w1"], p["w2"])
        saved.append(dict(x0=cur, p_dn=p_dn, p_sb=p_sb, p_sg=p_sg, p_ab=p_ab, h=h, mix=mix, x1=x1, dn_kept=dn_kept, sb_carries=sb_carries))
        cur = x2
    loss_part, dy = _loss_call(cur, loss_target.reshape(M, D))
    loss = lax.psum(loss_part[0, 0], ("x", "y", "c"))

    big_grads = [[None] * 4, [None] * 4]
    small_grads = {n: [None, None] for n in SMALL}
    for l in (1, 0):
        p, s = layer[l], saved[l]
        (dx1, da, r, h2, dg2), got1 = _ffn_bwd_call(s["x1"], dy, p["g2"], p["w1"], p["w2"], swap=big_grads[1] if l == 0 else ())
        big_grads[l][2] = _mm_tn_call(h2, da, "grad_w_ff1", col_shards=True)
        big_grads[l][3] = _mm_tn_call(r, dy, "grad_w_ff2").reshape(NDEV, FF_SHARD, D)
        dmix = _mm_nt_call(dx1, p["w_out"], "dmix")
        big_grads[l][1] = _mm_tn_call(s["mix"], dx1, "grad_w_out").reshape(NDEV, D // NDEV, D)
        if l == 0:
            got0, sums0 = _reduce_begin(big_grads[0][1:], "reduce_early0")
            early_sums = list(_pair_sum_call(big_grads[1], got1, ac, "reduce_early1_pair")) + list(sums0)
        (d_dn, d_ab, dcw, dalog, ddtb, ddn_g), early_from = _dn_bwd_call(
            s["p_dn"], s["p_ab"], dmix, s["dn_kept"], p["conv"], p["a_log"], p["dt_bias"], p["dn_g"], B, T,
            swap=early_sums if l == 0 else ())
        d_sb, dqg, dkg = _sb_bwd_call(s["p_sb"], dmix, s["sb_carries"], p["sb_qg"], p["sb_kg"], B, T)
        d_sg, dsg_g, dsg_w, dsg_b = _sg_bwd_call(s["p_sg"], dmix, p["sg_g"], p["sg_w"], p["sg_bias"], B, T)
        dsections = (d_dn, d_sb, d_sg, d_ab)
        dwt = _in_proj_grad_call(dsections, s["h"])
        big_grads[l][0] = _row_perm_call(dwt, table_back, "unpack_grad_w_in").reshape(NDEV, IN_SHARD_PAD, D)
        if l == 0:
            last = big_grads[0][:1]
            last_got, last_sums = _reduce_begin(last, "reduce_last")
        (dy, dg1), last_from = _in_proj_bwd_call(dsections, p["wt"], s["x0"], p["g1"], dx1, swap=last_sums if l == 0 else ())
        for n, val in (("norm1_g", dg1[0]), ("conv_w", dcw.transpose(1, 0, 2).reshape(4, 3 * DN_W)), ("a_log", dalog[0, :NH]),
                       ("dt_bias", ddtb[0, :NH]), ("dn_out_g", ddn_g[0]), ("sb_q_g", dqg[0, :64]), ("sb_k_g", dkg[0, :64]),
                       ("sg_v_g", dsg_g[0]), ("sg_w", dsg_w), ("sg_b", dsg_b[:, :NH].T), ("norm2_g", dg2[0])):
            small_grads[n][l] = val
    grad_x = dy.reshape(B, T, D)

    mine0 = _reduce_end(last, last_got, last_from, "reduce_last")
    mine1 = (_reduce_end(big_grads[1], got1, early_from[:4], "reduce_early1")
             + _reduce_end(big_grads[0][1:], got0, early_from[4:], "reduce_early0"))
    grads = {"w_in": jnp.stack([mine0[0][:IN_SHARD].T, mine1[0][:IN_SHARD].T]), "w_out": jnp.stack([mine1[4], mine1[1]]),
             "w_ff1": jnp.stack([mine1[5], mine1[2]]), "w_ff2": jnp.stack([mine1[6], mine1[3]])}
    small_flat = jnp.concatenate([jnp.stack(small_grads[n]).reshape(-1) for n in SMALL])
    everyone, = _all_gather_call([_to_rows(small_flat, 8)], "gather_small_grads")
    small_sum = _sum_call([(everyone, k) for k in range(NDEV)], F32, "sum_small_grads").reshape(-1)
    off = 0
    for n in SMALL:
        sz = 2 * _size(SMALL_SHAPE[n])
        grads[n] = small_sum[off:off + sz].reshape((2,) + SMALL_SHAPE[n])
        off += sz
    cshard = conv_w.shape[-1]
    grads["conv_w"] = lax.dynamic_slice_in_dim(grads["conv_w"], me * cshard, cshard, axis=2)

    deltas, new_m, new_v = {}, {}, {}
    for n in WEIGHTS:
        deltas[n], new_m[n], new_v[n] = _adamw_call(given[n], grads[n], mom[n], var[n], "adamw_" + n)
    return (loss, grad_x, *[grads[n] for n in WEIGHTS], *[deltas[n] for n in WEIGHTS], *[new_m[n] for n in WEIGHTS],
            *[new_v[n] for n in WEIGHTS])
```

```python
import functools

import numpy as np

import jax
import jax.numpy as jnp
from jax import lax
from jax.experimental import pallas as pl
from jax.experimental.pallas import tpu as pltpu

F32, BF16 = jnp.float32, jnp.bfloat16
EPS = 1e-6
LANES = 128
D = 1024
DFF = 4096
NH = 4
DN_W, SB_W, SG_W = 512, 256, 256
IN_DIM = 3336
NDEV = 8
IN_SHARD = IN_DIM // NDEV
IN_SHARD_PAD = 432
FF_SHARD = DFF // NDEV
DN_OFF, SB_OFF, SG_OFF, AB_OFF, NPACK = 0, 2048, 2816, 3328, 3456
SECTIONS = ((DN_OFF, 2048), (SB_OFF, 768), (SG_OFF, 512), (AB_OFF, 128))
SB_SCALE = 64 ** -0.5
DN_SCALE = 128 ** -0.5
VMEM_LIMIT = 56 * 1024 * 1024
VMEM_LIMIT_MAX = 62 * 1024 * 1024
ADAM_LR, ADAM_B1, ADAM_B2, ADAM_EPS, ADAM_WD, ADAM_STEP = 0.001, 0.9, 0.999, 1e-08, 0.01, 10
MESH = pl.DeviceIdType.MESH


def _iota(shape, dim):
    return lax.broadcasted_iota(jnp.int32, shape, dim)


def _params(**kw):
    return pltpu.CompilerParams(vmem_limit_bytes=VMEM_LIMIT, **kw)


NN, NT, TN = ((1,), (0,)), ((1,), (1,)), ((0,), (0,))


def _mm(a, b, dims):
    return lax.dot_general(a.astype(BF16), b.astype(BF16), (dims, ((), ())), preferred_element_type=F32)


def _plain(a, b, dims):
    return (a.T if dims == TN else a), (b.T if dims == NT else b)


def _mmx(a, b, dims):
    return _mm(*_plain(a, b, dims), NN)


@jax.custom_vjp
def _dot(a, b):
    return _mmx(a, b, NN)


def _dot_fwd(a, b):
    return _dot(a, b), (a, b)


def _dot_bwd(res, g):
    a, b = res
    return _mmx(g, b, NT).astype(a.dtype), _mmx(a, g, TN).astype(b.dtype)


_dot.defvjp(_dot_fwd, _dot_bwd)


def _split(x):
    hi = x.astype(BF16)
    return hi, (x - hi.astype(F32)).astype(BF16)


def _mm2(a, b):
    ah, al = _split(a)
    bh = b.astype(BF16)
    mm = lambda x, y: jnp.dot(x, y, preferred_element_type=F32)
    return mm(ah, bh) + mm(al, bh)


def _mm_ones(ones, x, ones_left):
    hi, lo = _split(x)
    mm = (lambda t: jnp.dot(ones, t, preferred_element_type=F32)) if ones_left else \
         (lambda t: jnp.dot(t, ones, preferred_element_type=F32))
    return mm(hi) + mm(lo)


def _pair_ones(kind, transposed):
    row, col = _iota((128, 128), 0), _iota((128, 128), 1)
    m = (row // 64) == (col // 64)
    if kind == "running":
        m = jnp.logical_and(m, (col >= row) if transposed else (col <= row))
    return jnp.where(m, 1.0, 0.0).astype(BF16)


@functools.partial(jax.custom_vjp, nondiff_argnums=(0,))
def _chunk_sum(kind, x):
    return _mm_ones(_pair_ones(kind, False), x, True)


def _chunk_sum_fwd(kind, x):
    return _chunk_sum(kind, x), None


def _chunk_sum_bwd(kind, _, g):
    return (_mm_ones(_pair_ones(kind, True), g, True),)


_chunk_sum.defvjp(_chunk_sum_fwd, _chunk_sum_bwd)


def _tri_ones(n, transposed):
    row, col = _iota((n, n), 0), _iota((n, n), 1)
    return jnp.where((row < col) if transposed else (row > col), 1.0, 0.0).astype(BF16)


def _sigmoid(x):
    return jax.nn.sigmoid(x)


def _silu(x):
    return x * _sigmoid(x)


def _softplus(x):
    return jnp.maximum(x, 0.0) + jnp.log1p(jnp.exp(-jnp.abs(x)))


def _gelu(x):
    return 0.5 * x * (1.0 + jnp.tanh(0.7978845608028654 * (x + 0.044715 * (x * x * x))))


def _rms(x, gain):
    return x * lax.rsqrt(jnp.mean(x * x, axis=-1, keepdims=True) + EPS) * gain


SUBLANES = 8


def _shift_down_impl(x, k):
    y = pltpu.roll(x, k, 0)
    top = jnp.where(_iota((SUBLANES, x.shape[1]), 0) >= k, y[:SUBLANES], 0.0)
    return jnp.concatenate([top, y[SUBLANES:]], axis=0)


def _shift_up_impl(x, k):
    n = x.shape[0]
    y = pltpu.roll(x, n - k, 0)
    bottom = jnp.where(_iota((SUBLANES, x.shape[1]), 0) < SUBLANES - k, y[n - SUBLANES:], 0.0)
    return jnp.concatenate([y[:n - SUBLANES], bottom], axis=0)


@functools.partial(jax.custom_vjp, nondiff_argnums=(1,))
def _shift_down(x, k):
    return _shift_down_impl(x, k)


def _shift_down_fwd(x, k):
    return _shift_down_impl(x, k), None


def _shift_down_bwd(k, _, g):
    return (_shift_up_impl(g, k),)


_shift_down.defvjp(_shift_down_fwd, _shift_down_bwd)


def _lane_pick(x, idx):
    return jnp.sum(jnp.where(_iota(x.shape, 1) == idx, x, 0.0), axis=-1, keepdims=True)


def _dn_conv(x, w0, w1, w2, w3, l2_scale):
    y = _silu(w3 * x + w2 * _shift_down(x, 1) + w1 * _shift_down(x, 2) + w0 * _shift_down(x, 3))
    if l2_scale is None:
        return y
    return y * lax.rsqrt(jnp.sum(y * y, axis=-1, keepdims=True) + EPS) * l2_scale


def _dn_gates(ab, a_log, dt_bias):
    lane = _iota((1, LANES), 1)
    g = -jnp.exp(a_log) * _softplus(ab + dt_bias)
    return jnp.where(lane < NH, g, jnp.where(lane < 2 * NH, _sigmoid(ab), 0.0))


def _same_head(shape):
    return (_iota(shape, 0) < LANES) == (_iota(shape, 1) < LANES)


def _bd(r2):
    return jnp.where(_same_head((2 * LANES, 2 * LANES)), jnp.concatenate([r2, r2], axis=0), 0.0)


def _bd_t(y2):
    t = y2.T
    return jnp.where(_same_head((2 * LANES, 2 * LANES)), jnp.concatenate([t, t], axis=1), 0.0)


def _pair_prod(kind, a2, b2, mm):
    if kind == NN:
        return mm(a2, _bd(b2))
    if kind == NT:
        return mm(a2, _bd_t(b2))
    full = mm(a2.T, b2)
    return jnp.concatenate([full[:LANES, :LANES], full[LANES:, LANES:]], axis=1)


_MM1 = lambda x, y: _mm(x, y, NN)


def _pair_vjp_rule(kind, a2, b2, g, mm):
    if kind == NN:
        return _pair_prod(NT, g, b2, mm), _pair_prod(TN, a2, g, mm)
    if kind == NT:
        return _pair_prod(NN, g, b2, mm), _pair_prod(TN, g, a2, mm)
    return _pair_prod(NT, b2, g, mm), _pair_prod(NN, a2, g, mm)


@functools.partial(jax.custom_vjp, nondiff_argnums=(0,))
def _pdot(kind, a2, b2):
    return _pair_prod(kind, a2, b2, _MM1)


def _pdot_fwd(kind, a2, b2):
    return _pdot(kind, a2, b2), (a2, b2)


def _pdot_bwd(kind, res, g):
    return _pair_vjp_rule(kind, *res, g, _MM1)


_pdot.defvjp(_pdot_fwd, _pdot_bwd)


def _unit_lower_inverse(lower):
    n = lower.shape[0]
    nk = -lower
    inv = jnp.where(_iota(lower.shape, 0) == jnp.bitwise_and(_iota(lower.shape, 1), n - 1), 1.0, 0.0) + nk
    for _ in range(5):
        nk = _pair_prod(NN, nk, nk, _MM1)
        inv = inv + _pair_prod(NN, inv, nk, _MM1)
    return inv


@jax.custom_vjp
def _solve_with(lower, inv, rhs):
    return _pair_prod(NN, inv, rhs, _mm2)


def _solve_with_fwd(lower, inv, rhs):
    x = _pair_prod(NN, inv, rhs, _mm2)
    return x, (inv, x)


def _solve_with_bwd(res, g):
    inv, x = res
    d_rhs = _pair_prod(TN, inv, g, _mm2)
    return -_pair_prod(NT, d_rhs, x, _MM1), jnp.zeros_like(inv), d_rhs


_solve_with.defvjp(_solve_with_fwd, _solve_with_bwd)


def _dn_local(q, k, v, g, beta, inv=None):
    shape = (LANES, 2 * LANES)
    row, col = _iota(shape, 0), jnp.bitwise_and(_iota(shape, 1), LANES - 1)
    same = (row // 64) == (col // 64)
    tri_incl = jnp.logical_and(same, col <= row)
    tri_strict = jnp.logical_and(same, col < row)
    first = row < 64
    gc = _chunk_sum("running", g)
    gl = _chunk_sum("total", g)
    diff = gc - jnp.concatenate([gc[:, :LANES].T, gc[:, LANES:].T], axis=1)
    decay = jnp.where(tri_incl, jnp.exp(jnp.where(tri_incl, diff, 0.0)), 0.0)
    egc = jnp.exp(gc)
    lower = jnp.where(tri_strict, beta * _pdot(NT, k, k) * decay, 0.0)
    if inv is None:
        inv = _unit_lower_inverse(lower)
    u_val = _solve_with(lower, inv, v * beta)
    w_dec = _solve_with(lower, inv, k * (beta * egc))
    qk = jnp.where(tri_incl, _pdot(NT, q, k) * decay, 0.0)
    q_dec = q * egc
    k_dec = k * jnp.exp(gl - gc)
    cd1 = jnp.exp(jnp.sum(jnp.where(first, g, 0.0), axis=0, keepdims=True))
    cd2 = jnp.exp(jnp.sum(jnp.where(first, 0.0, g), axis=0, keepdims=True))
    return (u_val, w_dec, qk, q_dec, k_dec, cd1, cd2), inv


def _dn_state(u_val, w_dec, qk, q_dec, k_dec, cd1, cd2, s0):
    first = _iota((LANES, 2 * LANES), 0) < 64
    u1 = u_val - _pdot(NN, w_dec, s0)
    s1 = s0 * cd1 + _pdot(TN, jnp.where(first, k_dec, 0.0), u1)
    u2 = u_val - _pdot(NN, w_dec, s1)
    u_new = jnp.where(first, u1, u2)
    s2 = s1 * cd2 + _pdot(TN, jnp.where(first, 0.0, k_dec), u_new)
    o = jnp.where(first, _pdot(NN, q_dec, s0), _pdot(NN, q_dec, s1)) + _pdot(NN, qk, u_new)
    return o, s2


def _dn_post(o, z, gain):
    return _rms(o, gain) * _silu(z)


_DN_L2 = (DN_SCALE, 1.0, None)
DN_HPS = 2
DN_BLK = 4 * DN_HPS * LANES
_DN_COLS = tuple(slice(i * LANES, (i + 1) * LANES) for i in range(DN_HPS))


def _dn_in_cols(s, i):
    return slice((s * DN_HPS + i) * LANES, (s * DN_HPS + i + 1) * LANES)


def _dn_taps(cw_ref, s, i):
    return tuple(cw_ref[t:t + 1, _dn_in_cols(s, i)] for t in range(4))


def _pair_rows(n):
    return pl.ds(pl.multiple_of(n * 128, 128), 128)


def _dn_gate_rows(gate, hp):
    head_a = _iota((1, DN_HPS * LANES), 1) < LANES
    h = DN_HPS * hp
    return (jnp.where(head_a, _lane_pick(gate, h), _lane_pick(gate, h + 1)),
            jnp.where(head_a, _lane_pick(gate, NH + h), _lane_pick(gate, NH + h + 1)))


def _dn_gate_cols(dg, db, hp):
    head_a = _iota((1, DN_HPS * LANES), 1) < LANES
    lane = _iota((1, LANES), 1)
    h = DN_HPS * hp
    out = 0.0
    for t, first in ((dg, h), (db, NH + h)):
        out = out + jnp.where(lane == first, jnp.sum(jnp.where(head_a, t, 0.0), axis=-1, keepdims=True), 0.0)
        out = out + jnp.where(lane == first + 1, jnp.sum(jnp.where(head_a, 0.0, t), axis=-1, keepdims=True), 0.0)
    return out


def _dn_in_specs(T):
    vec = pl.BlockSpec((1, LANES), lambda b, h: (0, 0))
    return [pl.BlockSpec((T, DN_BLK), lambda b, h: (b, h)), pl.BlockSpec((T, LANES), lambda b, h: (b, 0)),
            pl.BlockSpec((4, 3 * DN_HPS * LANES), lambda b, h: (0, h)), vec, vec, vec]


def _dn_fwd_call(proj_dn, proj_ab, conv_w, a_log, dt_bias, gain, B, T, gather=()):
    npair = T // 128
    ng = len(gather)
    nsteps = B * (NH // DN_HPS)

    def body(*refs):
        x_ref, ab_ref, cw_ref, alog_ref, dtb_ref, gain_ref = refs[:6]
        out_ref, q_s, k_s, v_s, o_s, gate_s, st_s, inv_s = refs[6 + ng:14 + ng]
        step_id = pl.program_id(0) * (NH // DN_HPS) + pl.program_id(1)
        if ng:
            send, forward, finish = _gather_phases(refs[6:6 + ng], refs[14 + ng:14 + 2 * ng], *refs[14 + 2 * ng:])
            pl.when(step_id == 0)(send)
            pl.when(step_id == nsteps - 1)(forward)
        hp = pl.program_id(1)
        for i, cs in enumerate(_DN_COLS):
            for s, (x_s, l2) in enumerate(zip((q_s, k_s, v_s), _DN_L2)):
                x_s[:, cs] = _dn_conv(x_ref[:, _dn_in_cols(s, i)], *_dn_taps(cw_ref, s, i), l2)
        gate_s[...] = _dn_gates(ab_ref[...], alog_ref[...], dtb_ref[...])

        def local_of(pair):
            r = _pair_rows(pair)
            loc, inv = _dn_local(q_s[r, :], k_s[r, :], v_s[r, :], *_dn_gate_rows(gate_s[r, :], hp))
            inv_s[0, 0, pair] = inv
            return loc

        def state_of(n, loc, state):
            st_s[0, 0, n] = state
            o, s2 = _dn_state(*loc, state)
            o_s[_pair_rows(n), :] = o
            return s2

        def step(n, carry):
            loc, state = carry
            return local_of(n + 1), state_of(n, loc, state)

        loc, state = lax.fori_loop(0, npair - 1, step, (local_of(0), jnp.zeros((LANES, DN_HPS * LANES), F32)))
        state_of(npair - 1, loc, state)
        for i, cs in enumerate(_DN_COLS):
            out_ref[:, cs] = _dn_post(o_s[:, cs], x_ref[:, _dn_in_cols(3, i)], gain_ref[...])
        if ng:
            pl.when(step_id == nsteps - 1)(finish)

    kept_specs, kept_shapes = _dn_kept(B, T)
    outs = pl.pallas_call(
        body, name="dn_fwd", grid=(B, NH // DN_HPS), in_specs=_dn_in_specs(T) + _any_specs(ng),
        out_specs=[pl.BlockSpec((T, DN_HPS * LANES), lambda b, h: (b, h), pipeline_mode=pl.Buffered(1))] + kept_specs + _any_specs(ng),
        out_shape=[jax.ShapeDtypeStruct((B * T, D), F32)] + kept_shapes + _gather_shapes(gather),
        scratch_shapes=_gather_sems(ng) if ng else [],
        compiler_params=_params(dimension_semantics=("arbitrary", "arbitrary")),
    )(proj_dn, proj_ab, conv_w, a_log, dt_bias, gain, *gather)
    return outs[0], outs[1:8], outs[8:]


def _dn_kept(B, T):
    one = pl.Buffered(1)
    npair, pairs = T // 128, NH // DN_HPS
    wide = pl.BlockSpec((T, DN_HPS * LANES), lambda b, h: (b, h), pipeline_mode=one)
    per_pair = pl.BlockSpec((1, 1, npair, LANES, DN_HPS * LANES), lambda b, h: (b, h, 0, 0, 0), pipeline_mode=one)
    specs = [wide] * 4 + [pl.BlockSpec((T, LANES), lambda b, h: (b, h), pipeline_mode=one)] + [per_pair] * 2
    shapes = ([jax.ShapeDtypeStruct((B * T, DN_W), F32)] * 4 + [jax.ShapeDtypeStruct((B * T, pairs * LANES), F32)]
              + [jax.ShapeDtypeStruct((B, pairs, npair, LANES, DN_HPS * LANES), F32)] * 2)
    return specs, shapes


def _dn_bwd_call(proj_dn, proj_ab, dmix, kept, conv_w, a_log, dt_bias, gain, B, T, swap=()):
    npair = T // 128
    ns = len(swap)
    nsteps = B * (NH // DN_HPS)

    def body(*refs):
        x_ref, ab_ref, cw_ref, alog_ref, dtb_ref, gain_ref, do_ref, q_s, k_s, v_s, o_ref, gate_s, st_s, inv_s = refs[:14]
        dx_ref, dab_ref, dcw_ref, dalog_ref, ddtb_ref, dgain_ref = refs[14 + ns:20 + ns]
        dgate_s, do_s = refs[20 + 2 * ns:22 + 2 * ns]
        b_i, hp = pl.program_id(0), pl.program_id(1)
        step_id = b_i * (NH // DN_HPS) + hp
        if ns:
            send, finish = _chip_swap_phases(refs[14:14 + ns], refs[20 + ns:20 + 2 * ns], *refs[22 + 2 * ns:])
            pl.when(step_id == 0)(send)

        def pair_in(r):
            return (q_s[r, :], k_s[r, :], v_s[r, :]) + _dn_gate_rows(gate_s[r, :], hp)

        zero_state = jnp.zeros((LANES, DN_HPS * LANES), F32)

        @pl.when(jnp.logical_and(b_i == 0, hp == 0))
        def _():
            dcw_ref[...] = jnp.zeros_like(dcw_ref)
            dalog_ref[...] = jnp.zeros_like(dalog_ref)
            ddtb_ref[...] = jnp.zeros_like(ddtb_ref)
            dgain_ref[...] = jnp.zeros_like(dgain_ref)

        for i, cs in enumerate(_DN_COLS):
            zc = _dn_in_cols(3, i)
            _, post_vjp = jax.vjp(_dn_post, o_ref[:, cs], x_ref[:, zc], gain_ref[...])
            do, dz, dgain = post_vjp(do_ref[:, cs])
            dx_ref[:, zc] = dz
            do_s[:, cs] = do
            dgain_ref[...] += dgain

        wide_cols = lambda s: slice(s * DN_HPS * LANES, (s + 1) * DN_HPS * LANES)

        def back_step(nn, dstate):
            n = npair - 1 - nn
            r = _pair_rows(n)
            inv = inv_s[0, 0, n]
            local = lambda q, k, v, g, beta, inv=inv: _dn_local(q, k, v, g, beta, inv)[0]
            loc, local_vjp = jax.vjp(local, *pair_in(r))
            _, state_vjp = jax.vjp(_dn_state, *loc, st_s[0, 0, n])
            *dloc, ds0 = state_vjp((do_s[r, :], dstate))
            dq, dk, dv, dg, db = local_vjp(tuple(dloc))
            dx_ref[r, wide_cols(0)], dx_ref[r, wide_cols(1)], dx_ref[r, wide_cols(2)] = dq, dk, dv
            dgate_s[r, :] = _dn_gate_cols(dg, db, hp)
            return ds0

        lax.fori_loop(0, npair, back_step, zero_state)

        for i, cs in enumerate(_DN_COLS):
            h = DN_HPS * hp + i
            for s, l2 in enumerate(_DN_L2):
                xc = _dn_in_cols(s, i)
                _, conv_vjp = jax.vjp(functools.partial(_dn_conv, l2_scale=l2), x_ref[:, xc], *_dn_taps(cw_ref, s, i))
                dx, *dw = conv_vjp(dx_ref[:, xc])
                dx_ref[:, xc] = dx
                for t in range(4):
                    dcw_ref[h + 4 * s, t:t + 1, :] += dw[t]
        _, gate_vjp = jax.vjp(_dn_gates, ab_ref[...], alog_ref[...], dtb_ref[...])
        dab, dalog, ddtb = gate_vjp(dgate_s[...])
        dalog_ref[...] += dalog
        ddtb_ref[...] += ddtb

        @pl.when(hp == 0)
        def _():
            dab_ref[...] = jnp.zeros_like(dab_ref)

        dab_ref[...] += dab
        if ns:
            pl.when(step_id == nsteps - 1)(finish)

    M = B * T
    one = pl.Buffered(1)
    vec = pl.BlockSpec((1, LANES), lambda b, h: (0, 0))
    wide = [pltpu.VMEM((T, DN_HPS * LANES), F32)]
    vec_shape = jax.ShapeDtypeStruct((1, LANES), F32)
    outs = pl.pallas_call(
        body, name="dn_bwd", grid=(B, NH // DN_HPS),
        in_specs=_dn_in_specs(T) + [pl.BlockSpec((T, DN_HPS * LANES), lambda b, h: (b, h))] + _dn_kept(B, T)[0]
        + _any_specs(ns),
        out_specs=[pl.BlockSpec((T, DN_BLK), lambda b, h: (b, h), pipeline_mode=one), pl.BlockSpec((T, LANES), lambda b, h: (b, 0)),
                   pl.BlockSpec((12, 4, LANES), lambda b, h: (0, 0, 0)), vec, vec, vec] + _any_specs(ns),
        out_shape=[jax.ShapeDtypeStruct((M, 4 * DN_W), F32), jax.ShapeDtypeStruct((M, LANES), F32),
                   jax.ShapeDtypeStruct((12, 4, LANES), F32), vec_shape, vec_shape, vec_shape] + _chip_swap_shapes(swap),
        scratch_shapes=[pltpu.VMEM((T, LANES), F32)] + wide + (_chip_swap_sems(ns) if ns else []),
        compiler_params=pltpu.CompilerParams(vmem_limit_bytes=VMEM_LIMIT_MAX, dimension_semantics=("arbitrary", "arbitrary")),
    )(proj_dn, proj_ab, conv_w, a_log, dt_bias, gain, dmix, *kept, *swap)
    return outs[:6], outs[6:]


SBQ = 256


def _group_rms(x, gain):
    first = _iota(x.shape, 1) < 64
    sq = x * x
    ss_a = jnp.sum(jnp.where(first, sq, 0.0), axis=-1, keepdims=True)
    ss_b = jnp.sum(jnp.where(first, 0.0, sq), axis=-1, keepdims=True)
    ms = jnp.where(first, ss_a, ss_b) * (1.0 / 64)
    return x * lax.rsqrt(ms + EPS) * gain


def _sb_stack(q):
    first = _iota((1, LANES), 1) < 64
    return jnp.concatenate([jnp.where(first, q, 0.0), jnp.where(first, 0.0, q)], axis=0)


def _sb_fold(acc):
    return jnp.where(_iota((1, LANES), 1) < 64, acc[:SBQ], acc[SBQ:])


def _sb_logs(q2, k, diag):
    n = SBQ
    z = _mm(q2, k, ((1,), (1,))) * SB_SCALE
    ls_pos = jnp.minimum(z, 0.0) - jnp.log(1.0 + jnp.exp(-jnp.abs(z)))
    l1m = ls_pos - z
    if not diag:
        return ls_pos, l1m, None
    mask = _iota((2 * n, n), 1) < jnp.bitwise_and(_iota((2 * n, n), 0), n - 1)
    return ls_pos, jnp.where(mask, l1m, 0.0), mask


def _sb_weights(ls_pos, l1m, mask, carry):
    w = jnp.exp(ls_pos + (_mm_ones(_tri_ones(SBQ, False), l1m, False) + carry))
    return w if mask is None else jnp.where(mask, w, 0.0)


def _sb_block(q, k, v, carry, diag):
    ls_pos, l1m, mask = _sb_logs(_sb_stack(q), k, diag)
    w = _sb_weights(ls_pos, l1m, mask, carry)
    return _mm(w, v, ((1,), (0,))), carry + jnp.sum(l1m, axis=-1, keepdims=True), _sb_sum_as_rows(l1m)


SB_ROWS = 16


def _sb_sum_as_rows(l1m):
    ones = jnp.ones((SB_ROWS, SBQ), BF16)
    hi, lo = _split(l1m)
    mm = lambda t: lax.dot_general(ones, t, (NT, ((), ())), preferred_element_type=F32)
    return mm(hi) + mm(lo)


def _sb_rows_as_column(rows):
    pick = jnp.where(_iota((SB_ROWS, SBQ), 0) == 0, 1.0, 0.0).astype(BF16)
    hi = rows.astype(BF16)
    rest = rows - hi.astype(F32)
    mid = rest.astype(BF16)
    lo = (rest - mid.astype(F32)).astype(BF16)
    mm = lambda t: lax.dot_general(t, pick, (TN, ((), ())), preferred_element_type=F32)
    return mm(hi) + (mm(mid) + mm(lo))


def _sb_block_bwd(q, k, v, carry, diag, dpv, dcarry):
    q2 = _sb_stack(q)
    ls_pos, l1m, mask = _sb_logs(q2, k, diag)
    w = _sb_weights(ls_pos, l1m, mask, carry)
    dv = _mm(w, dpv, ((0,), (0,)))
    de = _mm(dpv, v, ((1,), (1,))) * w
    dl1m = jnp.dot(de.astype(BF16), _tri_ones(SBQ, True), preferred_element_type=F32) + dcarry
    if mask is not None:
        dl1m = jnp.where(mask, dl1m, 0.0)
    sig = jnp.exp(ls_pos)
    dz = (de * (1.0 - sig) - dl1m * sig) * SB_SCALE
    dq = _sb_fold(_mm(dz, k, ((1,), (0,))))
    return dq, _mm(dz, q2, ((0,), (0,))), dv, dcarry + jnp.sum(de, axis=-1, keepdims=True)


_SB_Q, _SB_K, _SB_V = (slice(i * LANES, (i + 1) * LANES) for i in range(3))


def _sb_fwd_call(proj_sb, mix, q_gain, k_gain, B, T, gather=()):
    nblk = T // SBQ
    ng = len(gather)
    nsteps = 2 * B

    def body(*refs):
        x_ref, qg_ref, kg_ref = refs[:3]
        out_ref, carry_ref = refs[4 + ng:6 + ng]
        q_s, k_s = refs[6 + 2 * ng:8 + 2 * ng]
        step_id = 2 * pl.program_id(0) + pl.program_id(1)
        if ng:
            send, forward, finish = _gather_phases(refs[4:4 + ng], refs[6 + ng:6 + 2 * ng], *refs[8 + 2 * ng:])
            pl.when(step_id == 0)(send)
            pl.when(step_id == nsteps - 1)(forward)
        q_s[...] = _group_rms(x_ref[:, _SB_Q], qg_ref[...])
        k_s[...] = _group_rms(x_ref[:, _SB_K], kg_ref[...])

        def qblock(i, _):
            ri = pl.ds(pl.multiple_of(i * SBQ, SBQ), SBQ)
            q = q_s[ri, :]

            def kblock(jj, c):
                j = i - 1 - jj
                rj = pl.ds(pl.multiple_of(j * SBQ, SBQ), SBQ)
                carry_ref[0, 0, i, j] = c[2]
                pv, carry, rows = _sb_block(q, k_s[rj, :], x_ref[rj, _SB_V], c[1], False)
                return c[0] + pv, carry, c[2] + rows

            on_diag = _sb_block(q, k_s[ri, :], x_ref[ri, _SB_V], jnp.zeros((2 * SBQ, 1), F32), True)
            acc, _c, _r = lax.fori_loop(0, i, kblock, on_diag)
            out_ref[ri, :] = _sb_fold(acc)
            return 0

        lax.fori_loop(0, nblk, qblock, 0)
        if ng:
            pl.when(step_id == nsteps - 1)(finish)

    vec = pl.BlockSpec((1, LANES), lambda b, p: (0, 0))
    outs = pl.pallas_call(
        body, name="sb_fwd", grid=(B, 2),
        in_specs=[pl.BlockSpec((T, 3 * LANES), lambda b, p: (b, p)), vec, vec, pl.BlockSpec(memory_space=pl.ANY)] + _any_specs(ng),
        out_specs=[pl.BlockSpec((T, LANES), lambda b, p: (b, DN_W // LANES + p)), _sb_carry_spec(nblk)] + _any_specs(ng),
        out_shape=[jax.ShapeDtypeStruct((B * T, D), F32), jax.ShapeDtypeStruct((B, 2, nblk, nblk, SB_ROWS, 2 * SBQ), F32)]
        + _gather_shapes(gather), input_output_aliases={3: 0},
        scratch_shapes=[pltpu.VMEM((T, LANES), F32)] * 2 + (_gather_sems(ng) if ng else []),
        compiler_params=_params(dimension_semantics=("arbitrary", "arbitrary")),
    )(proj_sb, q_gain, k_gain, mix, *gather)
    return outs[0], outs[1], outs[2:]


def _sb_carry_spec(nblk):
    return pl.BlockSpec((1, 1, nblk, nblk, SB_ROWS, 2 * SBQ), lambda b, p: (b, p, 0, 0, 0, 0))


def _sb_bwd_call(proj_sb, dmix, carries, q_gain, k_gain, B, T):
    nblk = T // SBQ

    def body(x_ref, qg_ref, kg_ref, do_ref, carry_ref, dx_ref, dqg_ref, dkg_ref, q_s, k_s, dq_s, dk_s, dv_s):
        b_i, p = pl.program_id(0), pl.program_id(1)
        qn, q_vjp = jax.vjp(_group_rms, x_ref[:, _SB_Q], qg_ref[...])
        kn, k_vjp = jax.vjp(_group_rms, x_ref[:, _SB_K], kg_ref[...])
        q_s[...], k_s[...] = qn, kn
        dk_s[...] = jnp.zeros_like(dk_s)
        dv_s[...] = jnp.zeros_like(dv_s)

        def qblock(i, _):
            ri = pl.ds(pl.multiple_of(i * SBQ, SBQ), SBQ)
            q = q_s[ri, :]
            dacc = _sb_stack(do_ref[ri, :])

            def kblock(j, c):
                rj = pl.ds(pl.multiple_of(j * SBQ, SBQ), SBQ)
                carry = _sb_rows_as_column(carry_ref[0, 0, i, j])
                dq_j, dk_j, dv_j, dc = _sb_block_bwd(q, k_s[rj, :], x_ref[rj, _SB_V], carry, False, dacc, c[1])
                dk_s[rj, :] += dk_j
                dv_s[rj, :] += dv_j
                return c[0] + dq_j, dc

            dq, dc = lax.fori_loop(0, i, kblock, (jnp.zeros((SBQ, LANES), F32), jnp.zeros((2 * SBQ, 1), F32)))
            dq_i, dk_i, dv_i, _dc = _sb_block_bwd(q, k_s[ri, :], x_ref[ri, _SB_V], jnp.zeros((2 * SBQ, 1), F32), True, dacc, dc)
            dk_s[ri, :] += dk_i
            dv_s[ri, :] += dv_i
            dq_s[ri, :] = dq + dq_i
            return 0

        lax.fori_loop(0, nblk, qblock, 0)
        dq_in, dqg = q_vjp(dq_s[...])
        dk_in, dkg = k_vjp(dk_s[...])
        dx_ref[:, _SB_Q], dx_ref[:, _SB_K], dx_ref[:, _SB_V] = dq_in, dk_in, dv_s[...]

        @pl.when(jnp.logical_and(b_i == 0, p == 0))
        def _():
            dqg_ref[...] = jnp.zeros_like(dqg_ref)
            dkg_ref[...] = jnp.zeros_like(dkg_ref)

        dqg_ref[...] += dqg + pltpu.roll(dqg, 64, 1)
        dkg_ref[...] += dkg + pltpu.roll(dkg, 64, 1)

    M = B * T
    vec = pl.BlockSpec((1, LANES), lambda b, p: (0, 0))
    blk = pl.BlockSpec((T, 3 * LANES), lambda b, p: (b, p))
    big = [pltpu.VMEM((T, LANES), F32)]
    return pl.pallas_call(
        body, name="sb_bwd", grid=(B, 2),
        in_specs=[blk, vec, vec, pl.BlockSpec((T, LANES), lambda b, p: (b, DN_W // LANES + p)), _sb_carry_spec(nblk)],
        out_specs=[blk, vec, vec],
        out_shape=[jax.ShapeDtypeStruct((M, 3 * SB_W), F32)] + [jax.ShapeDtypeStruct((1, LANES), F32)] * 2,
        scratch_shapes=big * 5,
        compiler_params=_params(dimension_semantics=("arbitrary", "arbitrary")),
    )(proj_sb, q_gain, k_gain, dmix, carries)


def _sg_chunk(u, v, gain, w_a, w_b, bias):
    n = 128
    row, col = _iota((n, n), 0), _iota((n, n), 1)
    first = _iota((1, LANES), 1) < 64
    vn = _group_rms(_gelu(v), gain)
    tril = col <= row
    mixed = jnp.where(first, _dot(jnp.where(tril, w_a, 0.0), vn), _dot(jnp.where(tril, w_b, 0.0), vn)) + bias
    return _gelu(u) * mixed


_SG_U, _SG_V = slice(0, LANES), slice(LANES, 2 * LANES)


def _sg_fwd_call(proj_sg, mix, gain, sg_w, bias, B, T):
    nchunk = T // 128

    def body(x_ref, g_ref, wa_ref, wb_ref, bias_ref, mix_ref, out_ref):
        del mix_ref

        def step(i, _):
            r = pl.ds(pl.multiple_of(i * 128, 128), 128)
            out_ref[r, :] = _sg_chunk(x_ref[r, _SG_U], x_ref[r, _SG_V], g_ref[...], wa_ref[0], wb_ref[0], bias_ref[...])
            return 0

        lax.fori_loop(0, nchunk, step, 0)

    return pl.pallas_call(
        body, name="sg_fwd", grid=(B, 2),
        in_specs=[pl.BlockSpec((T, 2 * LANES), lambda b, p: (b, p)), pl.BlockSpec((1, LANES), lambda b, p: (0, p)),
                  pl.BlockSpec((1, 128, 128), lambda b, p: (2 * p, 0, 0)), pl.BlockSpec((1, 128, 128), lambda b, p: (2 * p + 1, 0, 0)),
                  pl.BlockSpec((128, LANES), lambda b, p: (0, p)), pl.BlockSpec(memory_space=pl.ANY)],
        out_specs=pl.BlockSpec((T, LANES), lambda b, p: (b, (DN_W + SB_W) // LANES + p)),
        out_shape=jax.ShapeDtypeStruct((B * T, D), F32), input_output_aliases={5: 0},
        compiler_params=_params(dimension_semantics=("arbitrary", "arbitrary")),
    )(proj_sg, gain, sg_w, sg_w, bias, mix)


def _sg_bwd_call(proj_sg, dmix, gain, sg_w, bias, B, T):
    nchunk = T // 128

    def body(x_ref, g_ref, wa_ref, wb_ref, bias_ref, do_ref, dx_ref, dg_ref, dw_ref, db_ref):
        p, b_i = pl.program_id(0), pl.program_id(1)

        def step(i, c):
            r = pl.ds(pl.multiple_of(i * 128, 128), 128)
            _, vjp = jax.vjp(_sg_chunk, x_ref[r, _SG_U], x_ref[r, _SG_V], g_ref[...], wa_ref[0], wb_ref[0], bias_ref[...])
            du, dv, dg, dwa, dwb, dbias = vjp(do_ref[r, :])
            dx_ref[r, _SG_U], dx_ref[r, _SG_V] = du, dv
            return c[0] + dg, c[1] + dwa, c[2] + dwb, c[3] + dbias

        z = jnp.zeros((128, 128), F32)
        dg, dwa, dwb, dbias = lax.fori_loop(0, nchunk, step, (jnp.zeros((1, LANES), F32), z, z, z))
        lane = _iota((1, LANES), 1)
        first = lane < 64
        s_a = jnp.sum(jnp.where(first, dbias, 0.0), axis=-1, keepdims=True)
        s_b = jnp.sum(jnp.where(first, 0.0, dbias), axis=-1, keepdims=True)
        dbg = jnp.where(lane == 2 * p, s_a, 0.0) + jnp.where(lane == 2 * p + 1, s_b, 0.0)

        @pl.when(b_i == 0)
        def _():
            dg_ref[...] = jnp.zeros_like(dg_ref)
            dw_ref[...] = jnp.zeros_like(dw_ref)

        @pl.when(jnp.logical_and(b_i == 0, p == 0))
        def _():
            db_ref[...] = jnp.zeros_like(db_ref)

        dg_ref[...] += dg
        dw_ref[0] += dwa
        dw_ref[1] += dwb
        db_ref[...] += dbg

    M = B * T
    blk = pl.BlockSpec((T, 2 * LANES), lambda p, b: (b, p))
    return pl.pallas_call(
        body, name="sg_bwd", grid=(2, B),
        in_specs=[blk, pl.BlockSpec((1, LANES), lambda p, b: (0, p)),
                  pl.BlockSpec((1, 128, 128), lambda p, b: (2 * p, 0, 0)), pl.BlockSpec((1, 128, 128), lambda p, b: (2 * p + 1, 0, 0)),
                  pl.BlockSpec((128, LANES), lambda p, b: (0, p)),
                  pl.BlockSpec((T, LANES), lambda p, b: (b, (DN_W + SB_W) // LANES + p))],
        out_specs=[blk, pl.BlockSpec((1, LANES), lambda p, b: (0, p)), pl.BlockSpec((2, 128, 128), lambda p, b: (p, 0, 0)),
                   pl.BlockSpec((128, LANES), lambda p, b: (0, 0))],
        out_shape=[jax.ShapeDtypeStruct((M, 2 * SG_W), F32), jax.ShapeDtypeStruct((1, SG_W), F32),
                   jax.ShapeDtypeStruct((4, 128, 128), F32), jax.ShapeDtypeStruct((128, LANES), F32)],
        compiler_params=_params(dimension_semantics=("arbitrary", "arbitrary")),
    )(proj_sg, gain, sg_w, sg_w, bias, dmix)


def _row_tile(m, most=512):
    return min(m, most)


def _in_proj_call(x, gain, wt):
    m = x.shape[0]
    tm = _row_tile(m, 1024)

    def body(x_ref, g_ref, wt_ref, *out_refs):
        h = _rms(x_ref[...], g_ref[...]).astype(BF16)
        out_refs[-1][...] = h
        for (off, width), out_ref in zip(SECTIONS, out_refs):
            out_ref[...] = lax.dot_general(h, wt_ref[off:off + width, :], (((1,), (1,)), ((), ())), preferred_element_type=F32)

    rows = lambda width: pl.BlockSpec((tm, width), lambda i: (i, 0))
    return pl.pallas_call(
        body, name="in_proj", grid=(m // tm,),
        in_specs=[rows(D), pl.BlockSpec((1, D), lambda i: (0, 0)),
                  pl.BlockSpec((NPACK, D), lambda i: (0, 0), pipeline_mode=pl.Buffered(1))],
        out_specs=[rows(w) for _, w in SECTIONS] + [rows(D)],
        out_shape=[jax.ShapeDtypeStruct((m, w), F32) for _, w in SECTIONS] + [jax.ShapeDtypeStruct((m, D), BF16)],
        compiler_params=_params(dimension_semantics=("arbitrary",)),
    )(x, gain, wt)


def _in_proj_bwd_call(dsections, wt, x, gain, dres, swap=()):
    m = x.shape[0]
    tm = _row_tile(m)
    nsec, ns = len(SECTIONS), len(swap)

    def body(*refs):
        ds_refs = refs[:nsec]
        wt_ref, x_ref, g_ref, dres_ref = refs[nsec:nsec + 4]
        dx_ref, dg_ref = refs[nsec + 4 + ns:nsec + 6 + ns]
        step = pl.program_id(0)
        if ns:
            send, finish = _chip_swap_phases(refs[nsec + 4:nsec + 4 + ns], refs[nsec + 6 + ns:nsec + 6 + 2 * ns],
                                             *refs[nsec + 6 + 2 * ns:])
            pl.when(step == 0)(send)

        @pl.when(step == 0)
        def _():
            dg_ref[...] = jnp.zeros_like(dg_ref)

        dh = 0.0
        for (off, width), ds_ref in zip(SECTIONS, ds_refs):
            dh = dh + jnp.dot(ds_ref[...].astype(BF16), wt_ref[off:off + width, :], preferred_element_type=F32)
        _, vjp = jax.vjp(_rms, x_ref[...], g_ref[...])
        dx, dg = vjp(dh)
        dx_ref[...] = dres_ref[...] + dx
        dg_ref[...] += dg
        if ns:
            pl.when(step == m // tm - 1)(finish)

    rows = lambda width: pl.BlockSpec((tm, width), lambda i: (i, 0))
    outs = pl.pallas_call(
        body, name="in_proj_bwd", grid=(m // tm,),
        in_specs=[rows(w) for _, w in SECTIONS] + [pl.BlockSpec((NPACK, D), lambda i: (0, 0), pipeline_mode=pl.Buffered(1)),
                                                   rows(D), pl.BlockSpec((1, D), lambda i: (0, 0)), rows(D)] + _any_specs(ns),
        out_specs=[rows(D), pl.BlockSpec((1, D), lambda i: (0, 0))] + _any_specs(ns),
        out_shape=[jax.ShapeDtypeStruct((m, D), F32), jax.ShapeDtypeStruct((1, D), F32)] + _chip_swap_shapes(swap),
        scratch_shapes=_chip_swap_sems(ns) if ns else [],
        compiler_params=_params(dimension_semantics=("arbitrary",)),
    )(*dsections, wt, x, gain, dres, *swap)
    return outs[:2], outs[2:]


def _in_proj_grad_call(dsections, h):
    m = h.shape[0]
    tm = _row_tile(m)

    def body(*refs):
        ds_refs, (h_ref, out_ref) = refs[:len(SECTIONS)], refs[len(SECTIONS):]

        @pl.when(pl.program_id(0) == 0)
        def _():
            out_ref[...] = jnp.zeros_like(out_ref)

        for (off, width), ds_ref in zip(SECTIONS, ds_refs):
            out_ref[off:off + width, :] += lax.dot_general(ds_ref[...].astype(BF16), h_ref[...], (((0,), (0,)), ((), ())),
                                                           preferred_element_type=F32)

    rows = lambda width: pl.BlockSpec((tm, width), lambda i: (i, 0))
    return pl.pallas_call(
        body, name="grad_w_in", grid=(m // tm,),
        in_specs=[rows(w) for _, w in SECTIONS] + [rows(D)],
        out_specs=pl.BlockSpec((NPACK, D), lambda i: (0, 0), pipeline_mode=pl.Buffered(1)),
        out_shape=jax.ShapeDtypeStruct((NPACK, D), F32),
        compiler_params=_params(dimension_semantics=("arbitrary",)),
    )(*dsections, h)


def _packed_column_of():
    t = np.full(NPACK, -1, np.int64)
    lanes = np.arange(LANES)
    for pair in range(2):
        for s in range(4):
            t[DN_OFF + pair * 1024 + s * 256 + np.arange(256)] = s * DN_W + pair * 256 + np.arange(256)
        for s in range(3):
            t[SB_OFF + pair * 384 + s * LANES + lanes] = 2056 + s * SB_W + pair * LANES + lanes
        for s in range(2):
            t[SG_OFF + pair * 256 + s * LANES + lanes] = 2056 + 3 * SB_W + s * SG_W + pair * LANES + lanes
    t[AB_OFF + np.arange(2 * NH)] = 4 * DN_W + np.arange(2 * NH)
    return t


def _row_tables():
    col = _packed_column_of()
    fwd = np.where(col >= 0, (col // IN_SHARD) * IN_SHARD_PAD + col % IN_SHARD, -1)
    packed_of = np.full(IN_DIM, -1, np.int64)
    packed_of[col[col >= 0]] = np.nonzero(col >= 0)[0]
    r = np.arange(NDEV * IN_SHARD_PAD)
    inside = r % IN_SHARD_PAD < IN_SHARD
    back = np.where(inside, packed_of[np.minimum((r // IN_SHARD_PAD) * IN_SHARD + r % IN_SHARD_PAD, IN_DIM - 1)], -1)
    return fwd, back


def _row_perm_call(src, table, name):
    n_out = table.shape[0]
    touched = [sorted(set((table[b * 128:(b + 1) * 128][table[b * 128:(b + 1) * 128] >= 0] // 128).tolist()))
               for b in range(n_out // 128)]

    def body(tbl_ref, src_ref, out_ref):
        lane = _iota((1, LANES), 1)
        for b, blocks in enumerate(touched):
            want = tbl_ref[b * 128:(b + 1) * 128, :]
            acc = jnp.zeros((128, D), F32)
            for sb in blocks:
                pick = jnp.where(want == sb * 128 + lane, 1.0, 0.0).astype(BF16)
                acc = acc + jnp.dot(pick, src_ref[sb * 128:(sb + 1) * 128, :].astype(BF16), preferred_element_type=F32)
            out_ref[b * 128:(b + 1) * 128, :] = acc.astype(BF16)

    return pl.pallas_call(
        body, name=name, out_shape=jax.ShapeDtypeStruct((n_out, D), BF16),
        in_specs=[pl.BlockSpec(memory_space=pltpu.VMEM)] * 2, out_specs=pl.BlockSpec(memory_space=pltpu.VMEM),
        compiler_params=_params(),
    )(jnp.asarray(table.reshape(-1, 1), jnp.int32), src)


def _out_proj_call(a, w, res):
    m, k = a.shape
    n = w.shape[1]
    tm = _row_tile(m, 1024)

    def body(a_ref, w_ref, res_ref, out_ref):
        out_ref[...] = res_ref[...] + jnp.dot(a_ref[...].astype(BF16), w_ref[...], preferred_element_type=F32)

    return pl.pallas_call(
        body, name="out_proj", grid=(m // tm,),
        in_specs=[pl.BlockSpec((tm, k), lambda i: (i, 0)), pl.BlockSpec((k, n), lambda i: (0, 0)),
                  pl.BlockSpec((tm, n), lambda i: (i, 0))],
        out_specs=pl.BlockSpec((tm, n), lambda i: (i, 0)),
        out_shape=jax.ShapeDtypeStruct((m, n), F32),
        compiler_params=_params(dimension_semantics=("arbitrary",)),
    )(a, w, res)


def _ffn_specs(tm):
    return [pl.BlockSpec((1, D, FF_SHARD), lambda i, j: (j, 0, 0)), pl.BlockSpec((FF_SHARD, D), lambda i, j: (j, 0))]


def _ffn_fwd_call(x, gain, w1, w2):
    m = x.shape[0]
    tm = _row_tile(m, 1024)

    def body(x_ref, g_ref, w1_ref, w2_ref, out_ref, h_s, acc_s):
        j = pl.program_id(1)

        @pl.when(j == 0)
        def _():
            h_s[...] = _rms(x_ref[...], g_ref[...]).astype(BF16)
            acc_s[...] = jnp.zeros_like(acc_s)

        a = jnp.maximum(jnp.dot(h_s[...], w1_ref[0], preferred_element_type=F32), 0.0)
        acc_s[...] += jnp.dot((a * a).astype(BF16), w2_ref[...], preferred_element_type=F32)

        @pl.when(j == NDEV - 1)
        def _():
            out_ref[...] = x_ref[...] + acc_s[...]

    return pl.pallas_call(
        body, name="ffn_fwd", grid=(m // tm, NDEV),
        in_specs=[pl.BlockSpec((tm, D), lambda i, j: (i, 0)), pl.BlockSpec((1, D), lambda i, j: (0, 0))] + _ffn_specs(tm),
        out_specs=pl.BlockSpec((tm, D), lambda i, j: (i, 0)),
        out_shape=jax.ShapeDtypeStruct((m, D), F32),
        scratch_shapes=[pltpu.VMEM((tm, D), BF16), pltpu.VMEM((tm, D), F32)],
        compiler_params=_params(dimension_semantics=("arbitrary", "arbitrary")),
    )(x, gain, w1, w2)


def _ffn_bwd_call(x, dy, gain, w1, w2, swap=()):
    m = x.shape[0]
    tm = _row_tile(m, 1024)
    ns = len(swap)

    def body(*refs):
        x_ref, dy_ref, g_ref, w1_ref, w2_ref = refs[:5]
        dx_ref, da_ref, r_ref, h_ref, dg_ref = refs[5 + ns:10 + ns]
        acc_s = refs[10 + 2 * ns]
        i, j = pl.program_id(0), pl.program_id(1)
        if ns:
            send, finish = _sibling_swap_phases(refs[5:5 + ns], refs[10 + ns:10 + 2 * ns], *refs[11 + 2 * ns:])
            pl.when(jnp.logical_and(i == 0, j == 0))(send)

        @pl.when(j == 0)
        def _():
            h_ref[...] = _rms(x_ref[...], g_ref[...]).astype(BF16)
            acc_s[...] = jnp.zeros_like(acc_s)

        @pl.when(jnp.logical_and(i == 0, j == 0))
        def _():
            dg_ref[...] = jnp.zeros_like(dg_ref)

        a = jnp.maximum(jnp.dot(h_ref[...], w1_ref[0], preferred_element_type=F32), 0.0)
        r_ref[...] = (a * a).astype(BF16)
        dr = lax.dot_general(dy_ref[...].astype(BF16), w2_ref[...], (((1,), (1,)), ((), ())), preferred_element_type=F32)
        da = (dr * (2.0 * a)).astype(BF16)
        da_ref[...] = da
        acc_s[...] += lax.dot_general(da, w1_ref[0], (((1,), (1,)), ((), ())), preferred_element_type=F32)

        @pl.when(j == NDEV - 1)
        def _():
            _, vjp = jax.vjp(_rms, x_ref[...], g_ref[...])
            dx, dg = vjp(acc_s[...])
            dx_ref[...] = dy_ref[...] + dx
            dg_ref[...] += dg

        if ns:
            pl.when(jnp.logical_and(i == m // tm - 1, j == NDEV - 1))(finish)

    outs = pl.pallas_call(
        body, name="ffn_bwd", grid=(m // tm, NDEV),
        in_specs=[pl.BlockSpec((tm, D), lambda i, j: (i, 0)), pl.BlockSpec((tm, D), lambda i, j: (i, 0)),
                  pl.BlockSpec((1, D), lambda i, j: (0, 0))] + _ffn_specs(tm) + _any_specs(ns),
        out_specs=[pl.BlockSpec((tm, D), lambda i, j: (i, 0)), pl.BlockSpec((tm, FF_SHARD), lambda i, j: (i, j)),
                   pl.BlockSpec((tm, FF_SHARD), lambda i, j: (i, j)), pl.BlockSpec((tm, D), lambda i, j: (i, 0)),
                   pl.BlockSpec((1, D), lambda i, j: (0, 0))] + _any_specs(ns),
        out_shape=[jax.ShapeDtypeStruct((m, D), F32), jax.ShapeDtypeStruct((m, DFF), BF16), jax.ShapeDtypeStruct((m, DFF), BF16),
                   jax.ShapeDtypeStruct((m, D), BF16), jax.ShapeDtypeStruct((1, D), F32)] + _sibling_swap_shapes(swap),
        scratch_shapes=[pltpu.VMEM((tm, D), F32)] + (_sibling_swap_sems(ns) if ns else []),
        compiler_params=_params(dimension_semantics=("arbitrary", "arbitrary")),
    )(x, dy, gain, w1, w2, *swap)
    return outs[:5], outs[5:]


def _mm_nt_call(a, b, name):
    m, k = a.shape
    n = b.shape[0]
    tm = _row_tile(m, 1024)

    def body(a_ref, b_ref, out_ref):
        out_ref[...] = lax.dot_general(a_ref[...].astype(BF16), b_ref[...].astype(BF16), (((1,), (1,)), ((), ())),
                                       preferred_element_type=F32)

    return pl.pallas_call(
        body, name=name, grid=(m // tm,),
        in_specs=[pl.BlockSpec((tm, k), lambda i: (i, 0)), pl.BlockSpec((n, k), lambda i: (0, 0))],
        out_specs=pl.BlockSpec((tm, n), lambda i: (i, 0)),
        out_shape=jax.ShapeDtypeStruct((m, n), F32),
        compiler_params=_params(dimension_semantics=("arbitrary",)),
    )(a, b)


def _mm_tn_call(a, b, name, col_shards=False):
    m, k = a.shape
    n = b.shape[1]
    tm, tk = _row_tile(m, 1024), min(k, 1024)
    tn = n // NDEV if col_shards else min(n, 1024)

    def body(a_ref, b_ref, out_ref, acc_s):
        s = pl.program_id(2)

        @pl.when(s == 0)
        def _():
            acc_s[...] = jnp.zeros_like(acc_s)

        acc_s[...] += lax.dot_general(a_ref[...].astype(BF16), b_ref[...].astype(BF16), (((0,), (0,)), ((), ())),
                                      preferred_element_type=F32)

        @pl.when(s == m // tm - 1)
        def _():
            out_ref[...] = acc_s[...].astype(BF16).reshape(out_ref.shape)

    if col_shards:
        out_spec, out_shape = pl.BlockSpec((1, tk, tn), lambda i, j, s: (j, i, 0)), (NDEV, k, tn)
    else:
        out_spec, out_shape = pl.BlockSpec((tk, tn), lambda i, j, s: (i, j)), (k, n)
    return pl.pallas_call(
        body, name=name, grid=(k // tk, n // tn, m // tm),
        in_specs=[pl.BlockSpec((tm, tk), lambda i, j, s: (s, i)), pl.BlockSpec((tm, tn), lambda i, j, s: (s, j))],
        out_specs=out_spec, out_shape=jax.ShapeDtypeStruct(out_shape, BF16),
        scratch_shapes=[pltpu.VMEM((tk, tn), F32)],
        compiler_params=_params(dimension_semantics=("arbitrary", "arbitrary", "arbitrary")),
    )(a, b)


def _loss_call(y, target):
    m = y.shape[0]
    tm = _row_tile(m, 1024)

    def body(y_ref, t_ref, loss_ref, dy_ref):
        @pl.when(pl.program_id(0) == 0)
        def _():
            loss_ref[...] = jnp.zeros_like(loss_ref)

        err = y_ref[...] - t_ref[...]
        dy_ref[...] = err * (1.0 / D)
        per_row = jnp.mean(err * err, axis=-1, keepdims=True)
        loss_ref[...] += jnp.broadcast_to(0.5 * jnp.sum(per_row, axis=0, keepdims=True), (1, LANES))

    return pl.pallas_call(
        body, name="loss", grid=(m // tm,),
        in_specs=[pl.BlockSpec((tm, D), lambda i: (i, 0))] * 2,
        out_specs=[pl.BlockSpec((1, LANES), lambda i: (0, 0)), pl.BlockSpec((tm, D), lambda i: (i, 0))],
        out_shape=[jax.ShapeDtypeStruct((1, LANES), F32), jax.ShapeDtypeStruct((m, D), F32)],
        compiler_params=_params(dimension_semantics=("arbitrary",)),
    )(y, target)


def _adamw_call(w, g, m, v, name):
    shape = w.shape
    cols = shape[-1] if w.ndim > 1 else w.size
    rows = w.size // cols
    tr = rows if (rows <= 512 or rows % 512) else 512
    c1, c2 = 1.0 - ADAM_B1 ** ADAM_STEP, 1.0 - ADAM_B2 ** ADAM_STEP

    def body(w_ref, g_ref, m_ref, v_ref, d_ref, nm_ref, nv_ref):
        g_ = g_ref[...]
        nm = ADAM_B1 * m_ref[...] + (1.0 - ADAM_B1) * g_
        nv = ADAM_B2 * v_ref[...] + (1.0 - ADAM_B2) * (g_ * g_)
        d_ref[...] = -ADAM_LR * ((nm / c1) / (jnp.sqrt(nv / c2) + ADAM_EPS) + ADAM_WD * w_ref[...])
        nm_ref[...], nv_ref[...] = nm, nv

    spec = pl.BlockSpec((tr, cols), lambda i: (i, 0))
    outs = pl.pallas_call(
        body, name=name, grid=(rows // tr,), in_specs=[spec] * 4, out_specs=[spec] * 3,
        out_shape=[jax.ShapeDtypeStruct((rows, cols), F32)] * 3,
        compiler_params=_params(dimension_semantics=("arbitrary",)),
    )(*(t.reshape(rows, cols) for t in (w, g, m, v)))
    return tuple(o.reshape(shape) for o in outs)


def _sum_tile(rows):
    for cand in (2048, 1024, 512, 256, 128):
        if rows > cand and rows % cand == 0:
            return cand
    return rows


def _pair_sum_call(gs, gots, core, name):
    n = len(gs)

    def body(core_ref, *refs):
        del core_ref
        for g_ref, got_ref, out_ref in zip(refs[:n], refs[n:2 * n], refs[2 * n:]):
            out_ref[...] = (g_ref[...].astype(F32) + got_ref[...].astype(F32)).astype(BF16)

    block = lambda g: (1,) + g.shape[1:]
    grid_spec = pltpu.PrefetchScalarGridSpec(
        num_scalar_prefetch=1, grid=(4,),
        in_specs=[pl.BlockSpec(block(g), lambda ch, core_ref: (2 * ch + core_ref[0], 0, 0)) for g in gs]
        + [pl.BlockSpec(block(g), lambda ch, core_ref: (ch, 0, 0)) for g in gs],
        out_specs=[pl.BlockSpec(block(g), lambda ch, core_ref: (ch, 0, 0)) for g in gs])
    return pl.pallas_call(
        body, name=name, grid_spec=grid_spec, out_shape=[jax.ShapeDtypeStruct((4,) + g.shape[1:], BF16) for g in gs],
        compiler_params=_params(dimension_semantics=("arbitrary",)),
    )(jnp.asarray(core, jnp.int32).reshape(1), *gs, *gots)


def _total_sum_call(gs, gots, froms, me, my_chip, name):
    n = len(gs)

    def body(idx_ref, *refs):
        del idx_ref
        for a in range(n):
            g_ref, got_ref, f0, f1, f2 = (refs[k * n + a] for k in range(5))
            acc = g_ref[0].astype(F32) + got_ref[0].astype(F32)
            for f in (f0, f1, f2):
                acc = acc + f[0].astype(F32)
            refs[5 * n + a][...] = acc

    block = lambda g: (1,) + g.shape[1:]
    picked = lambda which: [pl.BlockSpec(block(g), lambda t, idx, which=which: (idx[which], 0, 0)) for g in gs]
    fixed = lambda j: [pl.BlockSpec(block(g), lambda t, idx, j=j: (j, 0, 0)) for g in gs]
    grid_spec = pltpu.PrefetchScalarGridSpec(
        num_scalar_prefetch=1, grid=(1,),
        in_specs=picked(0) + picked(1) + fixed(0) + fixed(1) + fixed(2),
        out_specs=[pl.BlockSpec(g.shape[1:], lambda t, idx: (0, 0)) for g in gs])
    return pl.pallas_call(
        body, name=name, grid_spec=grid_spec, out_shape=[jax.ShapeDtypeStruct(g.shape[1:], F32) for g in gs],
        compiler_params=_params(dimension_semantics=("arbitrary",)),
    )(jnp.stack([jnp.asarray(me, jnp.int32), jnp.asarray(my_chip, jnp.int32)]), *gs, *gots, *froms, *froms, *froms)


def _sum_call(parts, out_dtype, name):
    rows, cols = parts[0][0].shape[1:]
    tr = _sum_tile(rows)
    index = jnp.stack([jnp.asarray(i, jnp.int32) for _, i in parts])

    def body(idx_ref, *refs):
        del idx_ref
        acc = refs[0][0].astype(F32)
        for r in refs[1:-1]:
            acc = acc + r[0].astype(F32)
        refs[-1][...] = acc.astype(out_dtype)

    grid_spec = pltpu.PrefetchScalarGridSpec(
        num_scalar_prefetch=1, grid=(rows // tr,),
        in_specs=[pl.BlockSpec((1, tr, cols), lambda t, idx, n=n: (idx[n], t, 0)) for n in range(len(parts))],
        out_specs=pl.BlockSpec((tr, cols), lambda t, idx: (t, 0)))
    return pl.pallas_call(
        body, name=name, grid_spec=grid_spec, out_shape=jax.ShapeDtypeStruct((rows, cols), out_dtype),
        compiler_params=_params(dimension_semantics=("arbitrary",)),
    )(index, *(a for a, _ in parts))


def _place():
    return lax.axis_index("x"), lax.axis_index("y"), lax.axis_index("c")


def _any_specs(n):
    return [pl.BlockSpec(memory_space=pl.ANY)] * n


def _all_gather_call(xs, name):
    n = len(xs)

    def body(*refs):
        for phase in _gather_phases(refs[:n], refs[n:2 * n], *refs[2 * n:]):
            phase()

    return pl.pallas_call(
        body, name=name, in_specs=_any_specs(n), out_specs=_any_specs(n),
        out_shape=_gather_shapes(xs), scratch_shapes=_gather_sems(n),
    )(*xs)


def _gather_shapes(xs):
    return [jax.ShapeDtypeStruct((NDEV,) + x.shape, x.dtype) for x in xs]


def _gather_sems(n):
    return [pltpu.SemaphoreType.DMA((7 * n,)), pltpu.SemaphoreType.DMA((7 * n,)), pltpu.SemaphoreType.DMA((n,))]


def _gather_phases(x_refs, out_refs, send_sems, recv_sems, local_sems):
    n = len(x_refs)
    ax, ay, ac = _place()
    me, sibling = (ax, ay, ac), (ax, ay, 1 - ac)
    chips = [(1 - ax, ay), (ax, 1 - ay), (1 - ax, 1 - ay)]

    def copy(a, k, block, to, src=None):
        slot = out_refs[a].at[4 * block[0] + 2 * block[1] + block[2]]
        return pltpu.make_async_remote_copy(
            src_ref=slot if src is None else src, dst_ref=slot,
            send_sem=send_sems.at[7 * a + k], recv_sem=recv_sems.at[7 * a + k], device_id=to, device_id_type=MESH)

    local = [pltpu.make_async_copy(x_refs[a], out_refs[a].at[4 * ax + 2 * ay + ac], local_sems.at[a]) for a in range(n)]
    first = []
    for a in range(n):
        first.append(copy(a, 0, me, sibling, src=x_refs[a]))
        first += [copy(a, 1 + j, me, (*chip, ac), src=x_refs[a]) for j, chip in enumerate(chips)]
    passed = [copy(a, 4 + j, (*chip, ac), sibling) for j, chip in enumerate(chips) for a in range(n)]

    def send():
        for cp in local + first:
            cp.start()

    def forward():
        for j, chip in enumerate(chips):
            for a in range(n):
                copy(a, 1 + j, (*chip, ac), me).wait_recv()
                passed[j * n + a].start()

    def finish():
        for a in range(n):
            copy(a, 0, sibling, me).wait_recv()
            for j, chip in enumerate(chips):
                copy(a, 4 + j, (*chip, 1 - ac), me).wait_recv()
        for cp in first + passed:
            cp.wait_send()
        for cp in local:
            cp.wait()

    return send, forward, finish


def _swap_sibling_call(xs, name):
    n = len(xs)

    def body(*refs):
        for phase in _sibling_swap_phases(refs[:n], refs[n:2 * n], *refs[2 * n:]):
            phase()

    return pl.pallas_call(
        body, name=name, in_specs=_any_specs(n), out_specs=_any_specs(n),
        out_shape=_sibling_swap_shapes(xs), scratch_shapes=_sibling_swap_sems(n),
    )(*xs)


def _sibling_swap_shapes(xs):
    return [jax.ShapeDtypeStruct((4,) + x.shape[1:], x.dtype) for x in xs]


def _sibling_swap_sems(n):
    return [pltpu.SemaphoreType.DMA((n,)), pltpu.SemaphoreType.DMA((n,))]


def _sibling_swap_phases(x_refs, out_refs, send_sems, recv_sems):
    ax, ay, ac = _place()
    sibling = (ax, ay, 1 - ac)

    def send():
        for a, (x_ref, out_ref) in enumerate(zip(x_refs, out_refs)):
            for chip in range(4):
                pltpu.make_async_remote_copy(src_ref=x_ref.at[2 * chip + 1 - ac], dst_ref=out_ref.at[chip],
                                             send_sem=send_sems.at[a], recv_sem=recv_sems.at[a],
                                             device_id=sibling, device_id_type=MESH).start()

    def finish():
        for a, (x_ref, out_ref) in enumerate(zip(x_refs, out_refs)):
            pltpu.make_async_remote_copy(src_ref=x_ref.at[pl.ds(0, 4)], dst_ref=out_ref, send_sem=send_sems.at[a],
                                         recv_sem=recv_sems.at[a], device_id=sibling, device_id_type=MESH).wait()

    return send, finish


def _chip_swap_shapes(xs):
    return [jax.ShapeDtypeStruct((3,) + x.shape[1:], x.dtype) for x in xs]


def _chip_swap_sems(n):
    return [pltpu.SemaphoreType.DMA((3 * n,)), pltpu.SemaphoreType.DMA((3 * n,))]


def _chip_swap_phases(x_refs, out_refs, send_sems, recv_sems):
    ax, ay, ac = _place()
    chips = [(1 - ax, ay), (ax, 1 - ay), (1 - ax, 1 - ay)]
    copies = [pltpu.make_async_remote_copy(src_ref=x_refs[a].at[2 * cx + cy], dst_ref=out_refs[a].at[j],
                                           send_sem=send_sems.at[3 * a + j], recv_sem=recv_sems.at[3 * a + j],
                                           device_id=(cx, cy, ac), device_id_type=MESH)
              for a in range(len(x_refs)) for j, (cx, cy) in enumerate(chips)]

    def send():
        for cp in copies:
            cp.start()

    def finish():
        for cp in copies:
            cp.wait()

    return send, finish


def _reduce_begin(gs, name):
    got = _swap_sibling_call(gs, name + "_d2d")
    return got, _pair_sum_call(gs, got, lax.axis_index("c"), name + "_pair")


def _reduce_end(gs, got, from_chips, name):
    ax, ay, ac = _place()
    return _total_sum_call(gs, got, from_chips, 4 * ax + 2 * ay + ac, 2 * ax + ay, name + "_total")


SMALL = ("norm1_g", "conv_w", "a_log", "dt_bias", "dn_out_g", "sb_q_g", "sb_k_g", "sg_v_g", "sg_w", "sg_b", "norm2_g")
WEIGHTS = ("norm1_g", "w_in", "conv_w", "a_log", "dt_bias", "dn_out_g", "sb_q_g", "sb_k_g", "sg_v_g", "sg_w", "sg_b",
           "w_out", "norm2_g", "w_ff1", "w_ff2")
SMALL_SHAPE = {"norm1_g": (D,), "conv_w": (4, 3 * DN_W), "a_log": (NH,), "dt_bias": (NH,), "dn_out_g": (128,), "sb_q_g": (64,),
               "sb_k_g": (64,), "sg_v_g": (SG_W,), "sg_w": (NH, 128, 128), "sg_b": (NH, 128), "norm2_g": (D,)}


def _size(shape):
    n = 1
    for s in shape:
        n *= s
    return n


def _to_rows(flat, multiple):
    pad = (-flat.shape[0]) % (LANES * multiple)
    return jnp.pad(flat, (0, pad)).reshape(-1, LANES)


def _conv_by_pair(conv):
    return conv.reshape(4, 3, 2, 256).transpose(0, 2, 1, 3).reshape(4, 3 * DN_W)


def kernel(x, norm1_g, w_in, conv_w, a_log, dt_bias, dn_out_g, sb_q_g, sb_k_g, sg_v_g, sg_w, sg_b, w_out, norm2_g, w_ff1, w_ff2, loss_target, m_norm1_g, m_w_in, m_conv_w, m_a_log, m_dt_bias, m_dn_out_g, m_sb_q_g, m_sb_k_g, m_sg_v_g, m_sg_w, m_sg_b, m_w_out, m_norm2_g, m_w_ff1, m_w_ff2, v_norm1_g, v_w_in, v_conv_w, v_a_log, v_dt_bias, v_dn_out_g, v_sb_q_g, v_sb_k_g, v_sg_v_g, v_sg_w, v_sg_b, v_w_out, v_norm2_g, v_w_ff1, v_w_ff2):
    given = dict(norm1_g=norm1_g, w_in=w_in, conv_w=conv_w, a_log=a_log, dt_bias=dt_bias, dn_out_g=dn_out_g, sb_q_g=sb_q_g,
                 sb_k_g=sb_k_g, sg_v_g=sg_v_g, sg_w=sg_w, sg_b=sg_b, w_out=w_out, norm2_g=norm2_g, w_ff1=w_ff1, w_ff2=w_ff2)
    mom = dict(norm1_g=m_norm1_g, w_in=m_w_in, conv_w=m_conv_w, a_log=m_a_log, dt_bias=m_dt_bias, dn_out_g=m_dn_out_g,
               sb_q_g=m_sb_q_g, sb_k_g=m_sb_k_g, sg_v_g=m_sg_v_g, sg_w=m_sg_w, sg_b=m_sg_b, w_out=m_w_out, norm2_g=m_norm2_g,
               w_ff1=m_w_ff1, w_ff2=m_w_ff2)
    var = dict(norm1_g=v_norm1_g, w_in=v_w_in, conv_w=v_conv_w, a_log=v_a_log, dt_bias=v_dt_bias, dn_out_g=v_dn_out_g,
               sb_q_g=v_sb_q_g, sb_k_g=v_sb_k_g, sg_v_g=v_sg_v_g, sg_w=v_sg_w, sg_b=v_sg_b, w_out=v_w_out, norm2_g=v_norm2_g,
               w_ff1=v_w_ff1, w_ff2=v_w_ff2)
    B, T, _ = x.shape
    M = B * T
    ax, ay, ac = _place()
    me = 4 * ax + 2 * ay + ac
    table_fwd, table_back = _row_tables()

    send = []
    for l in range(2):
        w_in_t = jnp.pad(w_in[l].T, ((0, IN_SHARD_PAD - IN_SHARD), (0, 0)))
        send.append([w_in_t.astype(BF16), w_out[l].astype(BF16), w_ff1[l].astype(BF16), w_ff2[l].astype(BF16)])
    first_in, conv_rows = _all_gather_call([send[0][0], _to_rows(conv_w.reshape(-1), 8)], "gather_first")
    conv_full = conv_rows.reshape(NDEV, -1)[:, :conv_w.size].reshape(NDEV, 2, 4, -1).transpose(1, 2, 0, 3).reshape(2, 4, 3 * DN_W)
    gathered = [[first_in, None, None, None], [None] * 4]

    pad_vec = lambda v: jnp.zeros((1, LANES), F32).at[0, :v.shape[0]].set(v)
    layer = []
    for l in range(2):
        layer.append(dict(
            g1=norm1_g[l].reshape(1, D), g2=norm2_g[l].reshape(1, D), conv=_conv_by_pair(conv_full[l]),
            a_log=pad_vec(a_log[l]), dt_bias=pad_vec(dt_bias[l]), dn_g=dn_out_g[l].reshape(1, LANES),
            sb_qg=jnp.tile(sb_q_g[l], 2).reshape(1, LANES), sb_kg=jnp.tile(sb_k_g[l], 2).reshape(1, LANES),
            sg_g=sg_v_g[l].reshape(1, SG_W), sg_w=sg_w[l], sg_bias=jnp.repeat(sg_b[l].T, 64, axis=1)))

    cur = x.reshape(M, D)
    saved = []
    for l, p in enumerate(layer):
        p["wt"] = _row_perm_call(gathered[l][0].reshape(NDEV * IN_SHARD_PAD, D), table_fwd, "pack_w_in")
        p_dn, p_sb, p_sg, p_ab, h = _in_proj_call(cur, p["g1"], p["wt"])
        mix, dn_kept, arrived = _dn_fwd_call(p_dn, p_ab, p["conv"], p["a_log"], p["dt_bias"], p["dn_g"], B, T,
                                             gather=send[0][1:] + send[1][:1] if l == 0 else [])
        if l == 0:
            gathered[0][1:], gathered[1][0] = list(arrived[:3]), arrived[3]
        p["w_out"], p["w1"], p["w2"] = gathered[l][1].reshape(D, D), gathered[l][2], gathered[l][3].reshape(DFF, D)
        mix, sb_carries, arrived = _sb_fwd_call(p_sb, mix, p["sb_qg"], p["sb_kg"], B, T, gather=send[1][1:] if l == 0 else [])
        if l == 0:
            gathered[1][1:] = list(arrived)
        mix = _sg_fwd_call(p_sg, mix, p["sg_g"], p["sg_w"], p["sg_bias"], B, T)
        x1 = _out_proj_call(mix, p["w_out"], cur)
        x2 = _ffn_fwd_call(x1, p["g2"], p["w1"], p["w2"])
        saved.append(dict(x0=cur, p_dn=p_dn, p_sb=p_sb, p_sg=p_sg, p_ab=p_ab, h=h, mix=mix, x1=x1, dn_kept=dn_kept, sb_carries=sb_carries))
        cur = x2
    loss_part, dy = _loss_call(cur, loss_target.reshape(M, D))
    loss = lax.psum(loss_part[0, 0], ("x", "y", "c"))

    big_grads = [[None] * 4, [None] * 4]
    small_grads = {n: [None, None] for n in SMALL}
    for l in (1, 0):
        p, s = layer[l], saved[l]
        (dx1, da, r, h2, dg2), got1 = _ffn_bwd_call(s["x1"], dy, p["g2"], p["w1"], p["w2"], swap=big_grads[1] if l == 0 else ())
        big_grads[l][2] = _mm_tn_call(h2, da, "grad_w_ff1", col_shards=True)
        big_grads[l][3] = _mm_tn_call(r, dy, "grad_w_ff2").reshape(NDEV, FF_SHARD, D)
        dmix = _mm_nt_call(dx1, p["w_out"], "dmix")
        big_grads[l][1] = _mm_tn_call(s["mix"], dx1, "grad_w_out").reshape(NDEV, D // NDEV, D)
        if l == 0:
            got0, sums0 = _reduce_begin(big_grads[0][1:], "reduce_early0")
            early_sums = list(_pair_sum_call(big_grads[1], got1, ac, "reduce_early1_pair")) + list(sums0)
        (d_dn, d_ab, dcw, dalog, ddtb, ddn_g), early_from = _dn_bwd_call(
            s["p_dn"], s["p_ab"], dmix, s["dn_kept"], p["conv"], p["a_log"], p["dt_bias"], p["dn_g"], B, T,
            swap=early_sums if l == 0 else ())
        d_sb, dqg, dkg = _sb_bwd_call(s["p_sb"], dmix, s["sb_carries"], p["sb_qg"], p["sb_kg"], B, T)
        d_sg, dsg_g, dsg_w, dsg_b = _sg_bwd_call(s["p_sg"], dmix, p["sg_g"], p["sg_w"], p["sg_bias"], B, T)
        dsections = (d_dn, d_sb, d_sg, d_ab)
        dwt = _in_proj_grad_call(dsections, s["h"])
        big_grads[l][0] = _row_perm_call(dwt, table_back, "unpack_grad_w_in").reshape(NDEV, IN_SHARD_PAD, D)
        if l == 0:
            last = big_grads[0][:1]
            last_got, last_sums = _reduce_begin(last, "reduce_last")
        (dy, dg1), last_from = _in_proj_bwd_call(dsections, p["wt"], s["x0"], p["g1"], dx1, swap=last_sums if l == 0 else ())
        for n, val in (("norm1_g", dg1[0]), ("conv_w", dcw.transpose(1, 0, 2).reshape(4, 3 * DN_W)), ("a_log", dalog[0, :NH]),
                       ("dt_bias", ddtb[0, :NH]), ("dn_out_g", ddn_g[0]), ("sb_q_g", dqg[0, :64]), ("sb_k_g", dkg[0, :64]),
                       ("sg_v_g", dsg_g[0]), ("sg_w", dsg_w), ("sg_b", dsg_b[:, :NH].T), ("norm2_g", dg2[0])):
            small_grads[n][l] = val
    grad_x = dy.reshape(B, T, D)

    mine0 = _reduce_end(last, last_got, last_from, "reduce_last")
    mine1 = (_reduce_end(big_grads[1], got1, early_from[:4], "reduce_early1")
             + _reduce_end(big_grads[0][1:], got0, early_from[4:], "reduce_early0"))
    grads = {"w_in": jnp.stack([mine0[0][:IN_SHARD].T, mine1[0][:IN_SHARD].T]), "w_out": jnp.stack([mine1[4], mine1[1]]),
             "w_ff1": jnp.stack([mine1[5], mine1[2]]), "w_ff2": jnp.stack([mine1[6], mine1[3]])}
    small_flat = jnp.concatenate([jnp.stack(small_grads[n]).reshape(-1) for n in SMALL])
    everyone, = _all_gather_call([_to_rows(small_flat, 8)], "gather_small_grads")
    small_sum = _sum_call([(everyone, k) for k in range(NDEV)], F32, "sum_small_grads").reshape(-1)
    off = 0
    for n in SMALL:
        sz = 2 * _size(SMALL_SHAPE[n])
        grads[n] = small_sum[off:off + sz].reshape((2,) + SMALL_SHAPE[n])
        off += sz
    cshard = conv_w.shape[-1]
    grads["conv_w"] = lax.dynamic_slice_in_dim(grads["conv_w"], me * cshard, cshard, axis=2)

    deltas, new_m, new_v = {}, {}, {}
    for n in WEIGHTS:
        deltas[n], new_m[n], new_v[n] = _adamw_call(given[n], grads[n], mom[n], var[n], "adamw_" + n)
    return (loss, grad_x, *[grads[n] for n in WEIGHTS], *[deltas[n] for n in WEIGHTS], *[new_m[n] for n in WEIGHTS],
            *[new_v[n] for n in WEIGHTS])
```

```python
import functools

import numpy as np

import jax
import jax.numpy as jnp
from jax import lax
from jax.experimental import pallas as pl
from jax.experimental.pallas import tpu as pltpu

F32, BF16 = jnp.float32, jnp.bfloat16
EPS = 1e-6
LANES = 128
D = 1024
DFF = 4096
NH = 4
DN_W, SB_W, SG_W = 512, 256, 256
IN_DIM = 3336
NDEV = 8
IN_SHARD = IN_DIM // NDEV
IN_SHARD_PAD = 432
FF_SHARD = DFF // NDEV
DN_OFF, SB_OFF, SG_OFF, AB_OFF, NPACK = 0, 2048, 2816, 3328, 3456
SECTIONS = ((DN_OFF, 2048), (SB_OFF, 768), (SG_OFF, 512), (AB_OFF, 128))
SB_SCALE = 64 ** -0.5
DN_SCALE = 128 ** -0.5
VMEM_LIMIT = 56 * 1024 * 1024
VMEM_LIMIT_MAX = 62 * 1024 * 1024
ADAM_LR, ADAM_B1, ADAM_B2, ADAM_EPS, ADAM_WD, ADAM_STEP = 0.001, 0.9, 0.999, 1e-08, 0.01, 10
MESH = pl.DeviceIdType.MESH


def _iota(shape, dim):
    return lax.broadcasted_iota(jnp.int32, shape, dim)


def _params(**kw):
    return pltpu.CompilerParams(vmem_limit_bytes=VMEM_LIMIT, **kw)


NN, NT, TN = ((1,), (0,)), ((1,), (1,)), ((0,), (0,))


def _mm(a, b, dims):
    return lax.dot_general(a.astype(BF16), b.astype(BF16), (dims, ((), ())), preferred_element_type=F32)


def _plain(a, b, dims):
    return (a.T if dims == TN else a), (b.T if dims == NT else b)


def _mmx(a, b, dims):
    return _mm(*_plain(a, b, dims), NN)


@jax.custom_vjp
def _dot(a, b):
    return _mmx(a, b, NN)


def _dot_fwd(a, b):
    return _dot(a, b), (a, b)


def _dot_bwd(res, g):
    a, b = res
    return _mmx(g, b, NT).astype(a.dtype), _mmx(a, g, TN).astype(b.dtype)


_dot.defvjp(_dot_fwd, _dot_bwd)


def _split(x):
    hi = x.astype(BF16)
    return hi, (x - hi.astype(F32)).astype(BF16)


def _mm2(a, b):
    ah, al = _split(a)
    bh = b.astype(BF16)
    mm = lambda x, y: jnp.dot(x, y, preferred_element_type=F32)
    return mm(ah, bh) + mm(al, bh)


def _mm_ones(ones, x, ones_left):
    hi, lo = _split(x)
    mm = (lambda t: jnp.dot(ones, t, preferred_element_type=F32)) if ones_left else \
         (lambda t: jnp.dot(t, ones, preferred_element_type=F32))
    return mm(hi) + mm(lo)


def _pair_ones(kind, transposed):
    row, col = _iota((128, 128), 0), _iota((128, 128), 1)
    m = (row // 64) == (col // 64)
    if kind == "running":
        m = jnp.logical_and(m, (col >= row) if transposed else (col <= row))
    return jnp.where(m, 1.0, 0.0).astype(BF16)


@functools.partial(jax.custom_vjp, nondiff_argnums=(0,))
def _chunk_sum(kind, x):
    return _mm_ones(_pair_ones(kind, False), x, True)


def _chunk_sum_fwd(kind, x):
    return _chunk_sum(kind, x), None


def _chunk_sum_bwd(kind, _, g):
    return (_mm_ones(_pair_ones(kind, True), g, True),)


_chunk_sum.defvjp(_chunk_sum_fwd, _chunk_sum_bwd)


def _tri_ones(n, transposed):
    row, col = _iota((n, n), 0), _iota((n, n), 1)
    return jnp.where((row < col) if transposed else (row > col), 1.0, 0.0).astype(BF16)


def _sigmoid(x):
    return jax.nn.sigmoid(x)


def _silu(x):
    return x * _sigmoid(x)


def _softplus(x):
    return jnp.maximum(x, 0.0) + jnp.log1p(jnp.exp(-jnp.abs(x)))


def _gelu(x):
    return 0.5 * x * (1.0 + jnp.tanh(0.7978845608028654 * (x + 0.044715 * (x * x * x))))


def _rms(x, gain):
    return x * lax.rsqrt(jnp.mean(x * x, axis=-1, keepdims=True) + EPS) * gain


SUBLANES = 8


def _shift_down_impl(x, k):
    y = pltpu.roll(x, k, 0)
    top = jnp.where(_iota((SUBLANES, x.shape[1]), 0) >= k, y[:SUBLANES], 0.0)
    return jnp.concatenate([top, y[SUBLANES:]], axis=0)


def _shift_up_impl(x, k):
    n = x.shape[0]
    y = pltpu.roll(x, n - k, 0)
    bottom = jnp.where(_iota((SUBLANES, x.shape[1]), 0) < SUBLANES - k, y[n - SUBLANES:], 0.0)
    return jnp.concatenate([y[:n - SUBLANES], bottom], axis=0)


@functools.partial(jax.custom_vjp, nondiff_argnums=(1,))
def _shift_down(x, k):
    return _shift_down_impl(x, k)


def _shift_down_fwd(x, k):
    return _shift_down_impl(x, k), None


def _shift_down_bwd(k, _, g):
    return (_shift_up_impl(g, k),)


_shift_down.defvjp(_shift_down_fwd, _shift_down_bwd)


def _lane_pick(x, idx):
    return jnp.sum(jnp.where(_iota(x.shape, 1) == idx, x, 0.0), axis=-1, keepdims=True)


def _dn_conv(x, w0, w1, w2, w3, l2_scale):
    y = _silu(w3 * x + w2 * _shift_down(x, 1) + w1 * _shift_down(x, 2) + w0 * _shift_down(x, 3))
    if l2_scale is None:
        return y
    return y * lax.rsqrt(jnp.sum(y * y, axis=-1, keepdims=True) + EPS) * l2_scale


def _dn_gates(ab, a_log, dt_bias):
    lane = _iota((1, LANES), 1)
    g = -jnp.exp(a_log) * _softplus(ab + dt_bias)
    return jnp.where(lane < NH, g, jnp.where(lane < 2 * NH, _sigmoid(ab), 0.0))


def _same_head(shape):
    return (_iota(shape, 0) < LANES) == (_iota(shape, 1) < LANES)


def _bd(r2):
    return jnp.where(_same_head((2 * LANES, 2 * LANES)), jnp.concatenate([r2, r2], axis=0), 0.0)


def _bd_t(y2):
    t = y2.T
    return jnp.where(_same_head((2 * LANES, 2 * LANES)), jnp.concatenate([t, t], axis=1), 0.0)


def _pair_prod(kind, a2, b2, mm):
    if kind == NN:
        return mm(a2, _bd(b2))
    if kind == NT:
        return mm(a2, _bd_t(b2))
    full = mm(a2.T, b2)
    return jnp.concatenate([full[:LANES, :LANES], full[LANES:, LANES:]], axis=1)


_MM1 = lambda x, y: _mm(x, y, NN)


def _pair_vjp_rule(kind, a2, b2, g, mm):
    if kind == NN:
        return _pair_prod(NT, g, b2, mm), _pair_prod(TN, a2, g, mm)
    if kind == NT:
        return _pair_prod(NN, g, b2, mm), _pair_prod(TN, g, a2, mm)
    return _pair_prod(NT, b2, g, mm), _pair_prod(NN, a2, g, mm)


@functools.partial(jax.custom_vjp, nondiff_argnums=(0,))
def _pdot(kind, a2, b2):
    return _pair_prod(kind, a2, b2, _MM1)


def _pdot_fwd(kind, a2, b2):
    return _pdot(kind, a2, b2), (a2, b2)


def _pdot_bwd(kind, res, g):
    return _pair_vjp_rule(kind, *res, g, _MM1)


_pdot.defvjp(_pdot_fwd, _pdot_bwd)


def _unit_lower_inverse(lower):
    n = lower.shape[0]
    nk = -lower
    inv = jnp.where(_iota(lower.shape, 0) == jnp.bitwise_and(_iota(lower.shape, 1), n - 1), 1.0, 0.0) + nk
    for _ in range(5):
        nk = _pair_prod(NN, nk, nk, _MM1)
        inv = inv + _pair_prod(NN, inv, nk, _MM1)
    return inv


@jax.custom_vjp
def _solve_with(lower, inv, rhs):
    return _pair_prod(NN, inv, rhs, _mm2)


def _solve_with_fwd(lower, inv, rhs):
    x = _pair_prod(NN, inv, rhs, _mm2)
    return x, (inv, x)


def _solve_with_bwd(res, g):
    inv, x = res
    d_rhs = _pair_prod(TN, inv, g, _mm2)
    return -_pair_prod(NT, d_rhs, x, _MM1), jnp.zeros_like(inv), d_rhs


_solve_with.defvjp(_solve_with_fwd, _solve_with_bwd)


def _dn_local(q, k, v, g, beta, inv=None):
    shape = (LANES, 2 * LANES)
    row, col = _iota(shape, 0), jnp.bitwise_and(_iota(shape, 1), LANES - 1)
    same = (row // 64) == (col // 64)
    tri_incl = jnp.logical_and(same, col <= row)
    tri_strict = jnp.logical_and(same, col < row)
    first = row < 64
    gc = _chunk_sum("running", g)
    gl = _chunk_sum("total", g)
    diff = gc - jnp.concatenate([gc[:, :LANES].T, gc[:, LANES:].T], axis=1)
    decay = jnp.where(tri_incl, jnp.exp(jnp.where(tri_incl, diff, 0.0)), 0.0)
    egc = jnp.exp(gc)
    lower = jnp.where(tri_strict, beta * _pdot(NT, k, k) * decay, 0.0)
    if inv is None:
        inv = _unit_lower_inverse(lower)
    u_val = _solve_with(lower, inv, v * beta)
    w_dec = _solve_with(lower, inv, k * (beta * egc))
    qk = jnp.where(tri_incl, _pdot(NT, q, k) * decay, 0.0)
    q_dec = q * egc
    k_dec = k * jnp.exp(gl - gc)
    cd1 = jnp.exp(jnp.sum(jnp.where(first, g, 0.0), axis=0, keepdims=True))
    cd2 = jnp.exp(jnp.sum(jnp.where(first, 0.0, g), axis=0, keepdims=True))
    return (u_val, w_dec, qk, q_dec, k_dec, cd1, cd2), inv


def _dn_state(u_val, w_dec, qk, q_dec, k_dec, cd1, cd2, s0):
    first = _iota((LANES, 2 * LANES), 0) < 64
    u1 = u_val - _pdot(NN, w_dec, s0)
    s1 = s0 * cd1 + _pdot(TN, jnp.where(first, k_dec, 0.0), u1)
    u2 = u_val - _pdot(NN, w_dec, s1)
    u_new = jnp.where(first, u1, u2)
    s2 = s1 * cd2 + _pdot(TN, jnp.where(first, 0.0, k_dec), u_new)
    o = jnp.where(first, _pdot(NN, q_dec, s0), _pdot(NN, q_dec, s1)) + _pdot(NN, qk, u_new)
    return o, s2


DN_POST_ROWS = 128


def _dn_post(o, z, gain):
    return _rms(o, gain) * _silu(z)


_DN_L2 = (DN_SCALE, 1.0, None)
DN_HPS = 2
DN_BLK = 4 * DN_HPS * LANES
_DN_COLS = tuple(slice(i * LANES, (i + 1) * LANES) for i in range(DN_HPS))


def _dn_in_cols(s, i):
    return slice((s * DN_HPS + i) * LANES, (s * DN_HPS + i + 1) * LANES)


def _dn_taps(cw_ref, s, i):
    return tuple(cw_ref[t:t + 1, _dn_in_cols(s, i)] for t in range(4))


def _pair_rows(n):
    return pl.ds(pl.multiple_of(n * 128, 128), 128)


def _dn_gate_rows(gate, hp):
    head_a = _iota((1, DN_HPS * LANES), 1) < LANES
    h = DN_HPS * hp
    return (jnp.where(head_a, _lane_pick(gate, h), _lane_pick(gate, h + 1)),
            jnp.where(head_a, _lane_pick(gate, NH + h), _lane_pick(gate, NH + h + 1)))


def _dn_gate_cols(dg, db, hp):
    head_a = _iota((1, DN_HPS * LANES), 1) < LANES
    lane = _iota((1, LANES), 1)
    h = DN_HPS * hp
    out = 0.0
    for t, first in ((dg, h), (db, NH + h)):
        out = out + jnp.where(lane == first, jnp.sum(jnp.where(head_a, t, 0.0), axis=-1, keepdims=True), 0.0)
        out = out + jnp.where(lane == first + 1, jnp.sum(jnp.where(head_a, 0.0, t), axis=-1, keepdims=True), 0.0)
    return out


def _dn_in_specs(T):
    vec = pl.BlockSpec((1, LANES), lambda b, h: (0, 0))
    return [pl.BlockSpec((T, DN_BLK), lambda b, h: (b, h)), pl.BlockSpec((T, LANES), lambda b, h: (b, 0)),
            pl.BlockSpec((4, 3 * DN_HPS * LANES), lambda b, h: (0, h)), vec, vec, vec]


def _dn_fwd_call(proj_dn, proj_ab, conv_w, a_log, dt_bias, gain, B, T, gather=()):
    npair = T // 128
    ng = len(gather)
    nsteps = B * (NH // DN_HPS)

    def body(*refs):
        x_ref, ab_ref, cw_ref, alog_ref, dtb_ref, gain_ref = refs[:6]
        out_ref, q_s, k_s, v_s, o_s, gate_s, st_s, inv_s = refs[6 + ng:14 + ng]
        step_id = pl.program_id(0) * (NH // DN_HPS) + pl.program_id(1)
        if ng:
            send, forward, finish = _gather_phases(refs[6:6 + ng], refs[14 + ng:14 + 2 * ng], *refs[14 + 2 * ng:])
            pl.when(step_id == 0)(send)
            pl.when(step_id == nsteps - 1)(forward)
        hp = pl.program_id(1)
        for i, cs in enumerate(_DN_COLS):
            for s, (x_s, l2) in enumerate(zip((q_s, k_s, v_s), _DN_L2)):
                x_s[:, cs] = _dn_conv(x_ref[:, _dn_in_cols(s, i)], *_dn_taps(cw_ref, s, i), l2)
        gate_s[...] = _dn_gates(ab_ref[...], alog_ref[...], dtb_ref[...])

        def local_of(pair):
            r = _pair_rows(pair)
            loc, inv = _dn_local(q_s[r, :], k_s[r, :], v_s[r, :], *_dn_gate_rows(gate_s[r, :], hp))
            inv_s[0, 0, pair] = inv
            return loc

        def state_of(n, loc, state):
            st_s[0, 0, n] = state
            o, s2 = _dn_state(*loc, state)
            o_s[_pair_rows(n), :] = o
            return s2

        def step(n, carry):
            loc, state = carry
            return local_of(n + 1), state_of(n, loc, state)

        loc, state = lax.fori_loop(0, npair - 1, step, (local_of(0), jnp.zeros((LANES, DN_HPS * LANES), F32)))
        state_of(npair - 1, loc, state)
        def post_chunk(c, _):
            r = pl.ds(pl.multiple_of(c * DN_POST_ROWS, DN_POST_ROWS), DN_POST_ROWS)
            for i, cs in enumerate(_DN_COLS):
                out_ref[r, cs] = _dn_post(o_s[r, cs], x_ref[r, _dn_in_cols(3, i)], gain_ref[...])
            return 0

        lax.fori_loop(0, T // DN_POST_ROWS, post_chunk, 0)
        if ng:
            pl.when(step_id == nsteps - 1)(finish)

    kept_specs, kept_shapes = _dn_kept(B, T)
    outs = pl.pallas_call(
        body, name="dn_fwd", grid=(B, NH // DN_HPS), in_specs=_dn_in_specs(T) + _any_specs(ng),
        out_specs=[pl.BlockSpec((T, DN_HPS * LANES), lambda b, h: (b, h), pipeline_mode=pl.Buffered(1))] + kept_specs + _any_specs(ng),
        out_shape=[jax.ShapeDtypeStruct((B * T, D), F32)] + kept_shapes + _gather_shapes(gather),
        scratch_shapes=_gather_sems(ng) if ng else [],
        compiler_params=_params(dimension_semantics=("arbitrary", "arbitrary")),
    )(proj_dn, proj_ab, conv_w, a_log, dt_bias, gain, *gather)
    return outs[0], outs[1:8], outs[8:]


def _dn_kept(B, T):
    one = pl.Buffered(1)
    npair, pairs = T // 128, NH // DN_HPS
    wide = pl.BlockSpec((T, DN_HPS * LANES), lambda b, h: (b, h), pipeline_mode=one)
    per_pair = pl.BlockSpec((1, 1, npair, LANES, DN_HPS * LANES), lambda b, h: (b, h, 0, 0, 0), pipeline_mode=one)
    specs = [wide] * 4 + [pl.BlockSpec((T, LANES), lambda b, h: (b, h), pipeline_mode=one)] + [per_pair] * 2
    shapes = ([jax.ShapeDtypeStruct((B * T, DN_W), F32)] * 4 + [jax.ShapeDtypeStruct((B * T, pairs * LANES), F32)]
              + [jax.ShapeDtypeStruct((B, pairs, npair, LANES, DN_HPS * LANES), F32)] * 2)
    return specs, shapes


def _dn_bwd_call(proj_dn, proj_ab, dmix, kept, conv_w, a_log, dt_bias, gain, B, T, swap=()):
    npair = T // 128
    ns = len(swap)
    nsteps = B * (NH // DN_HPS)

    def body(*refs):
        x_ref, ab_ref, cw_ref, alog_ref, dtb_ref, gain_ref, do_ref, q_s, k_s, v_s, o_ref, gate_s, st_s, inv_s = refs[:14]
        dx_ref, dab_ref, dcw_ref, dalog_ref, ddtb_ref, dgain_ref = refs[14 + ns:20 + ns]
        dgate_s, do_s = refs[20 + 2 * ns:22 + 2 * ns]
        b_i, hp = pl.program_id(0), pl.program_id(1)
        step_id = b_i * (NH // DN_HPS) + hp
        if ns:
            send, finish = _chip_swap_phases(refs[14:14 + ns], refs[20 + ns:20 + 2 * ns], *refs[22 + 2 * ns:])
            pl.when(step_id == 0)(send)

        def pair_in(r):
            return (q_s[r, :], k_s[r, :], v_s[r, :]) + _dn_gate_rows(gate_s[r, :], hp)

        zero_state = jnp.zeros((LANES, DN_HPS * LANES), F32)

        @pl.when(jnp.logical_and(b_i == 0, hp == 0))
        def _():
            dcw_ref[...] = jnp.zeros_like(dcw_ref)
            dalog_ref[...] = jnp.zeros_like(dalog_ref)
            ddtb_ref[...] = jnp.zeros_like(ddtb_ref)
            dgain_ref[...] = jnp.zeros_like(dgain_ref)

        def post_chunk(c, dgain):
            r = pl.ds(pl.multiple_of(c * DN_POST_ROWS, DN_POST_ROWS), DN_POST_ROWS)
            for i, cs in enumerate(_DN_COLS):
                zc = _dn_in_cols(3, i)
                _, post_vjp = jax.vjp(_dn_post, o_ref[r, cs], x_ref[r, zc], gain_ref[...])
                do, dz, dg = post_vjp(do_ref[r, cs])
                dx_ref[r, zc] = dz
                do_s[r, cs] = do
                dgain = dgain + dg
            return dgain

        dgain_ref[...] += lax.fori_loop(0, T // DN_POST_ROWS, post_chunk, jnp.zeros((1, LANES), F32))

        wide_cols = lambda s: slice(s * DN_HPS * LANES, (s + 1) * DN_HPS * LANES)

        def back_step(nn, dstate):
            n = npair - 1 - nn
            r = _pair_rows(n)
            inv = inv_s[0, 0, n]
            local = lambda q, k, v, g, beta, inv=inv: _dn_local(q, k, v, g, beta, inv)[0]
            loc, local_vjp = jax.vjp(local, *pair_in(r))
            _, state_vjp = jax.vjp(_dn_state, *loc, st_s[0, 0, n])
            *dloc, ds0 = state_vjp((do_s[r, :], dstate))
            dq, dk, dv, dg, db = local_vjp(tuple(dloc))
            dx_ref[r, wide_cols(0)], dx_ref[r, wide_cols(1)], dx_ref[r, wide_cols(2)] = dq, dk, dv
            dgate_s[r, :] = _dn_gate_cols(dg, db, hp)
            return ds0

        lax.fori_loop(0, npair, back_step, zero_state)

        for i, cs in enumerate(_DN_COLS):
            h = DN_HPS * hp + i
            for s, l2 in enumerate(_DN_L2):
                xc = _dn_in_cols(s, i)
                _, conv_vjp = jax.vjp(functools.partial(_dn_conv, l2_scale=l2), x_ref[:, xc], *_dn_taps(cw_ref, s, i))
                dx, *dw = conv_vjp(dx_ref[:, xc])
                dx_ref[:, xc] = dx
                for t in range(4):
                    dcw_ref[h + 4 * s, t:t + 1, :] += dw[t]
        _, gate_vjp = jax.vjp(_dn_gates, ab_ref[...], alog_ref[...], dtb_ref[...])
        dab, dalog, ddtb = gate_vjp(dgate_s[...])
        dalog_ref[...] += dalog
        ddtb_ref[...] += ddtb

        @pl.when(hp == 0)
        def _():
            dab_ref[...] = jnp.zeros_like(dab_ref)

        dab_ref[...] += dab
        if ns:
            pl.when(step_id == nsteps - 1)(finish)

    M = B * T
    one = pl.Buffered(1)
    vec = pl.BlockSpec((1, LANES), lambda b, h: (0, 0))
    wide = [pltpu.VMEM((T, DN_HPS * LANES), F32)]
    vec_shape = jax.ShapeDtypeStruct((1, LANES), F32)
    outs = pl.pallas_call(
        body, name="dn_bwd", grid=(B, NH // DN_HPS),
        in_specs=_dn_in_specs(T) + [pl.BlockSpec((T, DN_HPS * LANES), lambda b, h: (b, h))] + _dn_kept(B, T)[0]
        + _any_specs(ns),
        out_specs=[pl.BlockSpec((T, DN_BLK), lambda b, h: (b, h), pipeline_mode=one), pl.BlockSpec((T, LANES), lambda b, h: (b, 0)),
                   pl.BlockSpec((12, 4, LANES), lambda b, h: (0, 0, 0)), vec, vec, vec] + _any_specs(ns),
        out_shape=[jax.ShapeDtypeStruct((M, 4 * DN_W), F32), jax.ShapeDtypeStruct((M, LANES), F32),
                   jax.ShapeDtypeStruct((12, 4, LANES), F32), vec_shape, vec_shape, vec_shape] + _chip_swap_shapes(swap),
        scratch_shapes=[pltpu.VMEM((T, LANES), F32)] + wide + (_chip_swap_sems(ns) if ns else []),
        compiler_params=pltpu.CompilerParams(vmem_limit_bytes=VMEM_LIMIT_MAX, dimension_semantics=("arbitrary", "arbitrary")),
    )(proj_dn, proj_ab, conv_w, a_log, dt_bias, gain, dmix, *kept, *swap)
    return outs[:6], outs[6:]


SBQ = 256


def _group_rms(x, gain):
    first = _iota(x.shape, 1) < 64
    sq = x * x
    ss_a = jnp.sum(jnp.where(first, sq, 0.0), axis=-1, keepdims=True)
    ss_b = jnp.sum(jnp.where(first, 0.0, sq), axis=-1, keepdims=True)
    ms = jnp.where(first, ss_a, ss_b) * (1.0 / 64)
    return x * lax.rsqrt(ms + EPS) * gain


def _sb_stack(q):
    first = _iota((1, LANES), 1) < 64
    return jnp.concatenate([jnp.where(first, q, 0.0), jnp.where(first, 0.0, q)], axis=0)


def _sb_fold(acc):
    return jnp.where(_iota((1, LANES), 1) < 64, acc[:SBQ], acc[SBQ:])


def _sb_logs(q2, k, diag):
    n = SBQ
    z = _mm(q2, k, ((1,), (1,))) * SB_SCALE
    ls_pos = jnp.minimum(z, 0.0) - jnp.log(1.0 + jnp.exp(-jnp.abs(z)))
    l1m = ls_pos - z
    if not diag:
        return ls_pos, l1m, None
    mask = _iota((2 * n, n), 1) < jnp.bitwise_and(_iota((2 * n, n), 0), n - 1)
    return ls_pos, jnp.where(mask, l1m, 0.0), mask


def _sb_weights(ls_pos, l1m, mask, carry):
    w = jnp.exp(ls_pos + (_mm_ones(_tri_ones(SBQ, False), l1m, False) + carry))
    return w if mask is None else jnp.where(mask, w, 0.0)


def _sb_block(q, k, v, carry, diag):
    ls_pos, l1m, mask = _sb_logs(_sb_stack(q), k, diag)
    w = _sb_weights(ls_pos, l1m, mask, carry)
    return _mm(w, v, ((1,), (0,))), carry + jnp.sum(l1m, axis=-1, keepdims=True), _sb_sum_as_rows(l1m)


SB_ROWS = 16


def _sb_sum_as_rows(l1m):
    ones = jnp.ones((SB_ROWS, SBQ), BF16)
    hi, lo = _split(l1m)
    mm = lambda t: lax.dot_general(ones, t, (NT, ((), ())), preferred_element_type=F32)
    return mm(hi) + mm(lo)


def _sb_rows_as_column(rows):
    pick = jnp.where(_iota((SB_ROWS, SBQ), 0) == 0, 1.0, 0.0).astype(BF16)
    hi = rows.astype(BF16)
    rest = rows - hi.astype(F32)
    mid = rest.astype(BF16)
    lo = (rest - mid.astype(F32)).astype(BF16)
    mm = lambda t: lax.dot_general(t, pick, (TN, ((), ())), preferred_element_type=F32)
    return mm(hi) + (mm(mid) + mm(lo))


def _sb_block_bwd(q, k, v, carry, diag, dpv, dcarry):
    q2 = _sb_stack(q)
    ls_pos, l1m, mask = _sb_logs(q2, k, diag)
    w = _sb_weights(ls_pos, l1m, mask, carry)
    dv = _mm(w, dpv, ((0,), (0,)))
    de = _mm(dpv, v, ((1,), (1,))) * w
    dl1m = jnp.dot(de.astype(BF16), _tri_ones(SBQ, True), preferred_element_type=F32) + dcarry
    if mask is not None:
        dl1m = jnp.where(mask, dl1m, 0.0)
    sig = jnp.exp(ls_pos)
    dz = (de * (1.0 - sig) - dl1m * sig) * SB_SCALE
    dq = _sb_fold(_mm(dz, k, ((1,), (0,))))
    return dq, _mm(dz, q2, ((0,), (0,))), dv, dcarry + jnp.sum(de, axis=-1, keepdims=True)


_SB_Q, _SB_K, _SB_V = (slice(i * LANES, (i + 1) * LANES) for i in range(3))


def _sb_fwd_call(proj_sb, mix, q_gain, k_gain, B, T, gather=()):
    nblk = T // SBQ
    ng = len(gather)
    nsteps = 2 * B

    def body(*refs):
        x_ref, qg_ref, kg_ref = refs[:3]
        out_ref, carry_ref = refs[4 + ng:6 + ng]
        q_s, k_s = refs[6 + 2 * ng:8 + 2 * ng]
        step_id = 2 * pl.program_id(0) + pl.program_id(1)
        if ng:
            send, forward, finish = _gather_phases(refs[4:4 + ng], refs[6 + ng:6 + 2 * ng], *refs[8 + 2 * ng:])
            pl.when(step_id == 0)(send)
            pl.when(step_id == nsteps - 1)(forward)
        q_s[...] = _group_rms(x_ref[:, _SB_Q], qg_ref[...])
        k_s[...] = _group_rms(x_ref[:, _SB_K], kg_ref[...])

        def qblock(i, _):
            ri = pl.ds(pl.multiple_of(i * SBQ, SBQ), SBQ)
            q = q_s[ri, :]

            def kblock(jj, c):
                j = i - 1 - jj
                rj = pl.ds(pl.multiple_of(j * SBQ, SBQ), SBQ)
                carry_ref[0, 0, i, j] = c[2]
                pv, carry, rows = _sb_block(q, k_s[rj, :], x_ref[rj, _SB_V], c[1], False)
                return c[0] + pv, carry, c[2] + rows

            on_diag = _sb_block(q, k_s[ri, :], x_ref[ri, _SB_V], jnp.zeros((2 * SBQ, 1), F32), True)
            acc, _c, _r = lax.fori_loop(0, i, kblock, on_diag)
            out_ref[ri, :] = _sb_fold(acc)
            return 0

        lax.fori_loop(0, nblk, qblock, 0)
        if ng:
            pl.when(step_id == nsteps - 1)(finish)

    vec = pl.BlockSpec((1, LANES), lambda b, p: (0, 0))
    outs = pl.pallas_call(
        body, name="sb_fwd", grid=(B, 2),
        in_specs=[pl.BlockSpec((T, 3 * LANES), lambda b, p: (b, p)), vec, vec, pl.BlockSpec(memory_space=pl.ANY)] + _any_specs(ng),
        out_specs=[pl.BlockSpec((T, LANES), lambda b, p: (b, DN_W // LANES + p)), _sb_carry_spec(nblk)] + _any_specs(ng),
        out_shape=[jax.ShapeDtypeStruct((B * T, D), F32), jax.ShapeDtypeStruct((B, 2, nblk, nblk, SB_ROWS, 2 * SBQ), F32)]
        + _gather_shapes(gather), input_output_aliases={3: 0},
        scratch_shapes=[pltpu.VMEM((T, LANES), F32)] * 2 + (_gather_sems(ng) if ng else []),
        compiler_params=_params(dimension_semantics=("arbitrary", "arbitrary")),
    )(proj_sb, q_gain, k_gain, mix, *gather)
    return outs[0], outs[1], outs[2:]


def _sb_carry_spec(nblk):
    return pl.BlockSpec((1, 1, nblk, nblk, SB_ROWS, 2 * SBQ), lambda b, p: (b, p, 0, 0, 0, 0))


def _sb_bwd_call(proj_sb, dmix, carries, q_gain, k_gain, B, T):
    nblk = T // SBQ

    def body(x_ref, qg_ref, kg_ref, do_ref, carry_ref, dx_ref, dqg_ref, dkg_ref, q_s, k_s, dq_s, dk_s, dv_s):
        b_i, p = pl.program_id(0), pl.program_id(1)
        qn, q_vjp = jax.vjp(_group_rms, x_ref[:, _SB_Q], qg_ref[...])
        kn, k_vjp = jax.vjp(_group_rms, x_ref[:, _SB_K], kg_ref[...])
        q_s[...], k_s[...] = qn, kn
        dk_s[...] = jnp.zeros_like(dk_s)
        dv_s[...] = jnp.zeros_like(dv_s)

        def qblock(i, _):
            ri = pl.ds(pl.multiple_of(i * SBQ, SBQ), SBQ)
            q = q_s[ri, :]
            dacc = _sb_stack(do_ref[ri, :])

            def kblock(j, c):
                rj = pl.ds(pl.multiple_of(j * SBQ, SBQ), SBQ)
                carry = _sb_rows_as_column(carry_ref[0, 0, i, j])
                dq_j, dk_j, dv_j, dc = _sb_block_bwd(q, k_s[rj, :], x_ref[rj, _SB_V], carry, False, dacc, c[1])
                dk_s[rj, :] += dk_j
                dv_s[rj, :] += dv_j
                return c[0] + dq_j, dc

            dq, dc = lax.fori_loop(0, i, kblock, (jnp.zeros((SBQ, LANES), F32), jnp.zeros((2 * SBQ, 1), F32)))
            dq_i, dk_i, dv_i, _dc = _sb_block_bwd(q, k_s[ri, :], x_ref[ri, _SB_V], jnp.zeros((2 * SBQ, 1), F32), True, dacc, dc)
            dk_s[ri, :] += dk_i
            dv_s[ri, :] += dv_i
            dq_s[ri, :] = dq + dq_i
            return 0

        lax.fori_loop(0, nblk, qblock, 0)
        dq_in, dqg = q_vjp(dq_s[...])
        dk_in, dkg = k_vjp(dk_s[...])
        dx_ref[:, _SB_Q], dx_ref[:, _SB_K], dx_ref[:, _SB_V] = dq_in, dk_in, dv_s[...]

        @pl.when(jnp.logical_and(b_i == 0, p == 0))
        def _():
            dqg_ref[...] = jnp.zeros_like(dqg_ref)
            dkg_ref[...] = jnp.zeros_like(dkg_ref)

        dqg_ref[...] += dqg + pltpu.roll(dqg, 64, 1)
        dkg_ref[...] += dkg + pltpu.roll(dkg, 64, 1)

    M = B * T
    vec = pl.BlockSpec((1, LANES), lambda b, p: (0, 0))
    blk = pl.BlockSpec((T, 3 * LANES), lambda b, p: (b, p))
    big = [pltpu.VMEM((T, LANES), F32)]
    return pl.pallas_call(
        body, name="sb_bwd", grid=(B, 2),
        in_specs=[blk, vec, vec, pl.BlockSpec((T, LANES), lambda b, p: (b, DN_W // LANES + p)), _sb_carry_spec(nblk)],
        out_specs=[blk, vec, vec],
        out_shape=[jax.ShapeDtypeStruct((M, 3 * SB_W), F32)] + [jax.ShapeDtypeStruct((1, LANES), F32)] * 2,
        scratch_shapes=big * 5,
        compiler_params=_params(dimension_semantics=("arbitrary", "arbitrary")),
    )(proj_sb, q_gain, k_gain, dmix, carries)


def _sg_chunk(u, v, gain, w_a, w_b, bias):
    n = 128
    row, col = _iota((n, n), 0), _iota((n, n), 1)
    first = _iota((1, LANES), 1) < 64
    vn = _group_rms(_gelu(v), gain)
    tril = col <= row
    mixed = jnp.where(first, _dot(jnp.where(tril, w_a, 0.0), vn), _dot(jnp.where(tril, w_b, 0.0), vn)) + bias
    return _gelu(u) * mixed


_SG_U, _SG_V = slice(0, LANES), slice(LANES, 2 * LANES)


def _sg_fwd_call(proj_sg, mix, gain, sg_w, bias, B, T):
    nchunk = T // 128

    def body(x_ref, g_ref, wa_ref, wb_ref, bias_ref, mix_ref, out_ref):
        del mix_ref

        def step(i, _):
            r = pl.ds(pl.multiple_of(i * 128, 128), 128)
            out_ref[r, :] = _sg_chunk(x_ref[r, _SG_U], x_ref[r, _SG_V], g_ref[...], wa_ref[0], wb_ref[0], bias_ref[...])
            return 0

        lax.fori_loop(0, nchunk, step, 0)

    return pl.pallas_call(
        body, name="sg_fwd", grid=(B, 2),
        in_specs=[pl.BlockSpec((T, 2 * LANES), lambda b, p: (b, p)), pl.BlockSpec((1, LANES), lambda b, p: (0, p)),
                  pl.BlockSpec((1, 128, 128), lambda b, p: (2 * p, 0, 0)), pl.BlockSpec((1, 128, 128), lambda b, p: (2 * p + 1, 0, 0)),
                  pl.BlockSpec((128, LANES), lambda b, p: (0, p)), pl.BlockSpec(memory_space=pl.ANY)],
        out_specs=pl.BlockSpec((T, LANES), lambda b, p: (b, (DN_W + SB_W) // LANES + p)),
        out_shape=jax.ShapeDtypeStruct((B * T, D), F32), input_output_aliases={5: 0},
        compiler_params=_params(dimension_semantics=("arbitrary", "arbitrary")),
    )(proj_sg, gain, sg_w, sg_w, bias, mix)


def _sg_bwd_call(proj_sg, dmix, gain, sg_w, bias, B, T):
    nchunk = T // 128

    def body(x_ref, g_ref, wa_ref, wb_ref, bias_ref, do_ref, dx_ref, dg_ref, dw_ref, db_ref):
        p, b_i = pl.program_id(0), pl.program_id(1)

        def step(i, c):
            r = pl.ds(pl.multiple_of(i * 128, 128), 128)
            _, vjp = jax.vjp(_sg_chunk, x_ref[r, _SG_U], x_ref[r, _SG_V], g_ref[...], wa_ref[0], wb_ref[0], bias_ref[...])
            du, dv, dg, dwa, dwb, dbias = vjp(do_ref[r, :])
            dx_ref[r, _SG_U], dx_ref[r, _SG_V] = du, dv
            return c[0] + dg, c[1] + dwa, c[2] + dwb, c[3] + dbias

        z = jnp.zeros((128, 128), F32)
        dg, dwa, dwb, dbias = lax.fori_loop(0, nchunk, step, (jnp.zeros((1, LANES), F32), z, z, z))
        lane = _iota((1, LANES), 1)
        first = lane < 64
        s_a = jnp.sum(jnp.where(first, dbias, 0.0), axis=-1, keepdims=True)
        s_b = jnp.sum(jnp.where(first, 0.0, dbias), axis=-1, keepdims=True)
        dbg = jnp.where(lane == 2 * p, s_a, 0.0) + jnp.where(lane == 2 * p + 1, s_b, 0.0)

        @pl.when(b_i == 0)
        def _():
            dg_ref[...] = jnp.zeros_like(dg_ref)
            dw_ref[...] = jnp.zeros_like(dw_ref)

        @pl.when(jnp.logical_and(b_i == 0, p == 0))
        def _():
            db_ref[...] = jnp.zeros_like(db_ref)

        dg_ref[...] += dg
        dw_ref[0] += dwa
        dw_ref[1] += dwb
        db_ref[...] += dbg

    M = B * T
    blk = pl.BlockSpec((T, 2 * LANES), lambda p, b: (b, p))
    return pl.pallas_call(
        body, name="sg_bwd", grid=(2, B),
        in_specs=[blk, pl.BlockSpec((1, LANES), lambda p, b: (0, p)),
                  pl.BlockSpec((1, 128, 128), lambda p, b: (2 * p, 0, 0)), pl.BlockSpec((1, 128, 128), lambda p, b: (2 * p + 1, 0, 0)),
                  pl.BlockSpec((128, LANES), lambda p, b: (0, p)),
                  pl.BlockSpec((T, LANES), lambda p, b: (b, (DN_W + SB_W) // LANES + p))],
        out_specs=[blk, pl.BlockSpec((1, LANES), lambda p, b: (0, p)), pl.BlockSpec((2, 128, 128), lambda p, b: (p, 0, 0)),
                   pl.BlockSpec((128, LANES), lambda p, b: (0, 0))],
        out_shape=[jax.ShapeDtypeStruct((M, 2 * SG_W), F32), jax.ShapeDtypeStruct((1, SG_W), F32),
                   jax.ShapeDtypeStruct((4, 128, 128), F32), jax.ShapeDtypeStruct((128, LANES), F32)],
        compiler_params=_params(dimension_semantics=("arbitrary", "arbitrary")),
    )(proj_sg, gain, sg_w, sg_w, bias, dmix)


def _row_tile(m, most=512):
    return min(m, most)


def _in_proj_call(x, gain, wt):
    m = x.shape[0]
    tm = _row_tile(m, 1024)

    def body(x_ref, g_ref, wt_ref, *out_refs):
        h = _rms(x_ref[...], g_ref[...]).astype(BF16)
        out_refs[-1][...] = h
        for (off, width), out_ref in zip(SECTIONS, out_refs):
            out_ref[...] = lax.dot_general(h, wt_ref[off:off + width, :], (((1,), (1,)), ((), ())), preferred_element_type=F32)

    rows = lambda width: pl.BlockSpec((tm, width), lambda i: (i, 0))
    return pl.pallas_call(
        body, name="in_proj", grid=(m // tm,),
        in_specs=[rows(D), pl.BlockSpec((1, D), lambda i: (0, 0)),
                  pl.BlockSpec((NPACK, D), lambda i: (0, 0), pipeline_mode=pl.Buffered(1))],
        out_specs=[rows(w) for _, w in SECTIONS] + [rows(D)],
        out_shape=[jax.ShapeDtypeStruct((m, w), F32) for _, w in SECTIONS] + [jax.ShapeDtypeStruct((m, D), BF16)],
        compiler_params=_params(dimension_semantics=("arbitrary",)),
    )(x, gain, wt)


def _in_proj_bwd_call(dsections, wt, x, gain, dres, swap=()):
    m = x.shape[0]
    tm = _row_tile(m)
    nsec, ns = len(SECTIONS), len(swap)

    def body(*refs):
        ds_refs = refs[:nsec]
        wt_ref, x_ref, g_ref, dres_ref = refs[nsec:nsec + 4]
        dx_ref, dg_ref = refs[nsec + 4 + ns:nsec + 6 + ns]
        step = pl.program_id(0)
        if ns:
            send, finish = _chip_swap_phases(refs[nsec + 4:nsec + 4 + ns], refs[nsec + 6 + ns:nsec + 6 + 2 * ns],
                                             *refs[nsec + 6 + 2 * ns:])
            pl.when(step == 0)(send)

        @pl.when(step == 0)
        def _():
            dg_ref[...] = jnp.zeros_like(dg_ref)

        dh = 0.0
        for (off, width), ds_ref in zip(SECTIONS, ds_refs):
            dh = dh + jnp.dot(ds_ref[...].astype(BF16), wt_ref[off:off + width, :], preferred_element_type=F32)
        _, vjp = jax.vjp(_rms, x_ref[...], g_ref[...])
        dx, dg = vjp(dh)
        dx_ref[...] = dres_ref[...] + dx
        dg_ref[...] += dg
        if ns:
            pl.when(step == m // tm - 1)(finish)

    rows = lambda width: pl.BlockSpec((tm, width), lambda i: (i, 0))
    outs = pl.pallas_call(
        body, name="in_proj_bwd", grid=(m // tm,),
        in_specs=[rows(w) for _, w in SECTIONS] + [pl.BlockSpec((NPACK, D), lambda i: (0, 0), pipeline_mode=pl.Buffered(1)),
                                                   rows(D), pl.BlockSpec((1, D), lambda i: (0, 0)), rows(D)] + _any_specs(ns),
        out_specs=[rows(D), pl.BlockSpec((1, D), lambda i: (0, 0))] + _any_specs(ns),
        out_shape=[jax.ShapeDtypeStruct((m, D), F32), jax.ShapeDtypeStruct((1, D), F32)] + _chip_swap_shapes(swap),
        scratch_shapes=_chip_swap_sems(ns) if ns else [],
        compiler_params=_params(dimension_semantics=("arbitrary",)),
    )(*dsections, wt, x, gain, dres, *swap)
    return outs[:2], outs[2:]


def _in_proj_grad_call(dsections, h):
    m = h.shape[0]
    tm = _row_tile(m)

    def body(*refs):
        ds_refs, (h_ref, out_ref) = refs[:len(SECTIONS)], refs[len(SECTIONS):]

        @pl.when(pl.program_id(0) == 0)
        def _():
            out_ref[...] = jnp.zeros_like(out_ref)

        for (off, width), ds_ref in zip(SECTIONS, ds_refs):
            out_ref[off:off + width, :] += lax.dot_general(ds_ref[...].astype(BF16), h_ref[...], (((0,), (0,)), ((), ())),
                                                           preferred_element_type=F32)

    rows = lambda width: pl.BlockSpec((tm, width), lambda i: (i, 0))
    return pl.pallas_call(
        body, name="grad_w_in", grid=(m // tm,),
        in_specs=[rows(w) for _, w in SECTIONS] + [rows(D)],
        out_specs=pl.BlockSpec((NPACK, D), lambda i: (0, 0), pipeline_mode=pl.Buffered(1)),
        out_shape=jax.ShapeDtypeStruct((NPACK, D), F32),
        compiler_params=_params(dimension_semantics=("arbitrary",)),
    )(*dsections, h)


def _packed_column_of():
    t = np.full(NPACK, -1, np.int64)
    lanes = np.arange(LANES)
    for pair in range(2):
        for s in range(4):
            t[DN_OFF + pair * 1024 + s * 256 + np.arange(256)] = s * DN_W + pair * 256 + np.arange(256)
        for s in range(3):
            t[SB_OFF + pair * 384 + s * LANES + lanes] = 2056 + s * SB_W + pair * LANES + lanes
        for s in range(2):
            t[SG_OFF + pair * 256 + s * LANES + lanes] = 2056 + 3 * SB_W + s * SG_W + pair * LANES + lanes
    t[AB_OFF + np.arange(2 * NH)] = 4 * DN_W + np.arange(2 * NH)
    return t


def _row_tables():
    col = _packed_column_of()
    fwd = np.where(col >= 0, (col // IN_SHARD) * IN_SHARD_PAD + col % IN_SHARD, -1)
    packed_of = np.full(IN_DIM, -1, np.int64)
    packed_of[col[col >= 0]] = np.nonzero(col >= 0)[0]
    r = np.arange(NDEV * IN_SHARD_PAD)
    inside = r % IN_SHARD_PAD < IN_SHARD
    back = np.where(inside, packed_of[np.minimum((r // IN_SHARD_PAD) * IN_SHARD + r % IN_SHARD_PAD, IN_DIM - 1)], -1)
    return fwd, back


def _row_perm_call(src, table, name):
    n_out = table.shape[0]
    touched = [sorted(set((table[b * 128:(b + 1) * 128][table[b * 128:(b + 1) * 128] >= 0] // 128).tolist()))
               for b in range(n_out // 128)]

    def body(tbl_ref, src_ref, out_ref):
        lane = _iota((1, LANES), 1)
        for b, blocks in enumerate(touched):
            want = tbl_ref[b * 128:(b + 1) * 128, :]
            acc = jnp.zeros((128, D), F32)
            for sb in blocks:
                pick = jnp.where(want == sb * 128 + lane, 1.0, 0.0).astype(BF16)
                acc = acc + jnp.dot(pick, src_ref[sb * 128:(sb + 1) * 128, :].astype(BF16), preferred_element_type=F32)
            out_ref[b * 128:(b + 1) * 128, :] = acc.astype(BF16)

    return pl.pallas_call(
        body, name=name, out_shape=jax.ShapeDtypeStruct((n_out, D), BF16),
        in_specs=[pl.BlockSpec(memory_space=pltpu.VMEM)] * 2, out_specs=pl.BlockSpec(memory_space=pltpu.VMEM),
        compiler_params=_params(),
    )(jnp.asarray(table.reshape(-1, 1), jnp.int32), src)


def _out_proj_call(a, w, res):
    m, k = a.shape
    n = w.shape[1]
    tm = _row_tile(m, 1024)

    def body(a_ref, w_ref, res_ref, out_ref):
        out_ref[...] = res_ref[...] + jnp.dot(a_ref[...].astype(BF16), w_ref[...], preferred_element_type=F32)

    return pl.pallas_call(
        body, name="out_proj", grid=(m // tm,),
        in_specs=[pl.BlockSpec((tm, k), lambda i: (i, 0)), pl.BlockSpec((k, n), lambda i: (0, 0)),
                  pl.BlockSpec((tm, n), lambda i: (i, 0))],
        out_specs=pl.BlockSpec((tm, n), lambda i: (i, 0)),
        out_shape=jax.ShapeDtypeStruct((m, n), F32),
        compiler_params=_params(dimension_semantics=("arbitrary",)),
    )(a, w, res)


def _ffn_specs(tm):
    return [pl.BlockSpec((1, D, FF_SHARD), lambda i, j: (j, 0, 0)), pl.BlockSpec((FF_SHARD, D), lambda i, j: (j, 0))]


def _ffn_fwd_call(x, gain, w1, w2):
    m = x.shape[0]
    tm = _row_tile(m, 1024)

    def body(x_ref, g_ref, w1_ref, w2_ref, out_ref, h_s, acc_s):
        j = pl.program_id(1)

        @pl.when(j == 0)
        def _():
            h_s[...] = _rms(x_ref[...], g_ref[...]).astype(BF16)
            acc_s[...] = jnp.zeros_like(acc_s)

        a = jnp.maximum(jnp.dot(h_s[...], w1_ref[0], preferred_element_type=F32), 0.0)
        acc_s[...] += jnp.dot((a * a).astype(BF16), w2_ref[...], preferred_element_type=F32)

        @pl.when(j == NDEV - 1)
        def _():
            out_ref[...] = x_ref[...] + acc_s[...]

    return pl.pallas_call(
        body, name="ffn_fwd", grid=(m // tm, NDEV),
        in_specs=[pl.BlockSpec((tm, D), lambda i, j: (i, 0)), pl.BlockSpec((1, D), lambda i, j: (0, 0))] + _ffn_specs(tm),
        out_specs=pl.BlockSpec((tm, D), lambda i, j: (i, 0)),
        out_shape=jax.ShapeDtypeStruct((m, D), F32),
        scratch_shapes=[pltpu.VMEM((tm, D), BF16), pltpu.VMEM((tm, D), F32)],
        compiler_params=_params(dimension_semantics=("arbitrary", "arbitrary")),
    )(x, gain, w1, w2)


def _ffn_bwd_call(x, dy, gain, w1, w2, swap=()):
    m = x.shape[0]
    tm = _row_tile(m, 1024)
    ns = len(swap)

    def body(*refs):
        x_ref, dy_ref, g_ref, w1_ref, w2_ref = refs[:5]
        dx_ref, da_ref, r_ref, h_ref, dg_ref = refs[5 + ns:10 + ns]
        acc_s = refs[10 + 2 * ns]
        i, j = pl.program_id(0), pl.program_id(1)
        if ns:
            send, finish = _sibling_swap_phases(refs[5:5 + ns], refs[10 + ns:10 + 2 * ns], *refs[11 + 2 * ns:])
            pl.when(jnp.logical_and(i == 0, j == 0))(send)

        @pl.when(j == 0)
        def _():
            h_ref[...] = _rms(x_ref[...], g_ref[...]).astype(BF16)
            acc_s[...] = jnp.zeros_like(acc_s)

        @pl.when(jnp.logical_and(i == 0, j == 0))
        def _():
            dg_ref[...] = jnp.zeros_like(dg_ref)

        a = jnp.maximum(jnp.dot(h_ref[...], w1_ref[0], preferred_element_type=F32), 0.0)
        r_ref[...] = (a * a).astype(BF16)
        dr = lax.dot_general(dy_ref[...].astype(BF16), w2_ref[...], (((1,), (1,)), ((), ())), preferred_element_type=F32)
        da = (dr * (2.0 * a)).astype(BF16)
        da_ref[...] = da
        acc_s[...] += lax.dot_general(da, w1_ref[0], (((1,), (1,)), ((), ())), preferred_element_type=F32)

        @pl.when(j == NDEV - 1)
        def _():
            _, vjp = jax.vjp(_rms, x_ref[...], g_ref[...])
            dx, dg = vjp(acc_s[...])
            dx_ref[...] = dy_ref[...] + dx
            dg_ref[...] += dg

        if ns:
            pl.when(jnp.logical_and(i == m // tm - 1, j == NDEV - 1))(finish)

    outs = pl.pallas_call(
        body, name="ffn_bwd", grid=(m // tm, NDEV),
        in_specs=[pl.BlockSpec((tm, D), lambda i, j: (i, 0)), pl.BlockSpec((tm, D), lambda i, j: (i, 0)),
                  pl.BlockSpec((1, D), lambda i, j: (0, 0))] + _ffn_specs(tm) + _any_specs(ns),
        out_specs=[pl.BlockSpec((tm, D), lambda i, j: (i, 0)), pl.BlockSpec((tm, FF_SHARD), lambda i, j: (i, j)),
                   pl.BlockSpec((tm, FF_SHARD), lambda i, j: (i, j)), pl.BlockSpec((tm, D), lambda i, j: (i, 0)),
                   pl.BlockSpec((1, D), lambda i, j: (0, 0))] + _any_specs(ns),
        out_shape=[jax.ShapeDtypeStruct((m, D), F32), jax.ShapeDtypeStruct((m, DFF), BF16), jax.ShapeDtypeStruct((m, DFF), BF16),
                   jax.ShapeDtypeStruct((m, D), BF16), jax.ShapeDtypeStruct((1, D), F32)] + _sibling_swap_shapes(swap),
        scratch_shapes=[pltpu.VMEM((tm, D), F32)] + (_sibling_swap_sems(ns) if ns else []),
        compiler_params=_params(dimension_semantics=("arbitrary", "arbitrary")),
    )(x, dy, gain, w1, w2, *swap)
    return outs[:5], outs[5:]


def _mm_nt_call(a, b, name):
    m, k = a.shape
    n = b.shape[0]
    tm = _row_tile(m, 1024)

    def body(a_ref, b_ref, out_ref):
        out_ref[...] = lax.dot_general(a_ref[...].astype(BF16), b_ref[...].astype(BF16), (((1,), (1,)), ((), ())),
                                       preferred_element_type=F32)

    return pl.pallas_call(
        body, name=name, grid=(m // tm,),
        in_specs=[pl.BlockSpec((tm, k), lambda i: (i, 0)), pl.BlockSpec((n, k), lambda i: (0, 0))],
        out_specs=pl.BlockSpec((tm, n), lambda i: (i, 0)),
        out_shape=jax.ShapeDtypeStruct((m, n), F32),
        compiler_params=_params(dimension_semantics=("arbitrary",)),
    )(a, b)


def _mm_tn_call(a, b, name, col_shards=False):
    m, k = a.shape
    n = b.shape[1]
    tm, tk = _row_tile(m, 1024), min(k, 1024)
    tn = n // NDEV if col_shards else min(n, 1024)

    def body(a_ref, b_ref, out_ref, acc_s):
        s = pl.program_id(2)

        @pl.when(s == 0)
        def _():
            acc_s[...] = jnp.zeros_like(acc_s)

        acc_s[...] += lax.dot_general(a_ref[...].astype(BF16), b_ref[...].astype(BF16), (((0,), (0,)), ((), ())),
                                      preferred_element_type=F32)

        @pl.when(s == m // tm - 1)
        def _():
            out_ref[...] = acc_s[...].astype(BF16).reshape(out_ref.shape)

    if col_shards:
        out_spec, out_shape = pl.BlockSpec((1, tk, tn), lambda i, j, s: (j, i, 0)), (NDEV, k, tn)
    else:
        out_spec, out_shape = pl.BlockSpec((tk, tn), lambda i, j, s: (i, j)), (k, n)
    return pl.pallas_call(
        body, name=name, grid=(k // tk, n // tn, m // tm),
        in_specs=[pl.BlockSpec((tm, tk), lambda i, j, s: (s, i)), pl.BlockSpec((tm, tn), lambda i, j, s: (s, j))],
        out_specs=out_spec, out_shape=jax.ShapeDtypeStruct(out_shape, BF16),
        scratch_shapes=[pltpu.VMEM((tk, tn), F32)],
        compiler_params=_params(dimension_semantics=("arbitrary", "arbitrary", "arbitrary")),
    )(a, b)


def _loss_call(y, target):
    m = y.shape[0]
    tm = _row_tile(m, 1024)

    def body(y_ref, t_ref, loss_ref, dy_ref):
        @pl.when(pl.program_id(0) == 0)
        def _():
            loss_ref[...] = jnp.zeros_like(loss_ref)

        err = y_ref[...] - t_ref[...]
        dy_ref[...] = err * (1.0 / D)
        per_row = jnp.mean(err * err, axis=-1, keepdims=True)
        loss_ref[...] += jnp.broadcast_to(0.5 * jnp.sum(per_row, axis=0, keepdims=True), (1, LANES))

    return pl.pallas_call(
        body, name="loss", grid=(m // tm,),
        in_specs=[pl.BlockSpec((tm, D), lambda i: (i, 0))] * 2,
        out_specs=[pl.BlockSpec((1, LANES), lambda i: (0, 0)), pl.BlockSpec((tm, D), lambda i: (i, 0))],
        out_shape=[jax.ShapeDtypeStruct((1, LANES), F32), jax.ShapeDtypeStruct((m, D), F32)],
        compiler_params=_params(dimension_semantics=("arbitrary",)),
    )(y, target)


def _adamw_call(w, g, m, v, name):
    shape = w.shape
    cols = shape[-1] if w.ndim > 1 else w.size
    rows = w.size // cols
    tr = rows if (rows <= 512 or rows % 512) else 512
    c1, c2 = 1.0 - ADAM_B1 ** ADAM_STEP, 1.0 - ADAM_B2 ** ADAM_STEP

    def body(w_ref, g_ref, m_ref, v_ref, d_ref, nm_ref, nv_ref):
        g_ = g_ref[...]
        nm = ADAM_B1 * m_ref[...] + (1.0 - ADAM_B1) * g_
        nv = ADAM_B2 * v_ref[...] + (1.0 - ADAM_B2) * (g_ * g_)
        d_ref[...] = -ADAM_LR * ((nm / c1) / (jnp.sqrt(nv / c2) + ADAM_EPS) + ADAM_WD * w_ref[...])
        nm_ref[...], nv_ref[...] = nm, nv

    spec = pl.BlockSpec((tr, cols), lambda i: (i, 0))
    outs = pl.pallas_call(
        body, name=name, grid=(rows // tr,), in_specs=[spec] * 4, out_specs=[spec] * 3,
        out_shape=[jax.ShapeDtypeStruct((rows, cols), F32)] * 3,
        compiler_params=_params(dimension_semantics=("arbitrary",)),
    )(*(t.reshape(rows, cols) for t in (w, g, m, v)))
    return tuple(o.reshape(shape) for o in outs)


def _sum_tile(rows):
    for cand in (2048, 1024, 512, 256, 128):
        if rows > cand and rows % cand == 0:
            return cand
    return rows


def _pair_sum_call(gs, gots, core, name):
    n = len(gs)

    def body(core_ref, *refs):
        del core_ref
        for g_ref, got_ref, out_ref in zip(refs[:n], refs[n:2 * n], refs[2 * n:]):
            out_ref[...] = (g_ref[...].astype(F32) + got_ref[...].astype(F32)).astype(BF16)

    block = lambda g: (1,) + g.shape[1:]
    grid_spec = pltpu.PrefetchScalarGridSpec(
        num_scalar_prefetch=1, grid=(4,),
        in_specs=[pl.BlockSpec(block(g), lambda ch, core_ref: (2 * ch + core_ref[0], 0, 0)) for g in gs]
        + [pl.BlockSpec(block(g), lambda ch, core_ref: (ch, 0, 0)) for g in gs],
        out_specs=[pl.BlockSpec(block(g), lambda ch, core_ref: (ch, 0, 0)) for g in gs])
    return pl.pallas_call(
        body, name=name, grid_spec=grid_spec, out_shape=[jax.ShapeDtypeStruct((4,) + g.shape[1:], BF16) for g in gs],
        compiler_params=_params(dimension_semantics=("arbitrary",)),
    )(jnp.asarray(core, jnp.int32).reshape(1), *gs, *gots)


def _total_sum_call(gs, gots, froms, me, my_chip, name):
    n = len(gs)

    def body(idx_ref, *refs):
        del idx_ref
        for a in range(n):
            g_ref, got_ref, f0, f1, f2 = (refs[k * n + a] for k in range(5))
            acc = g_ref[0].astype(F32) + got_ref[0].astype(F32)
            for f in (f0, f1, f2):
                acc = acc + f[0].astype(F32)
            refs[5 * n + a][...] = acc

    block = lambda g: (1,) + g.shape[1:]
    picked = lambda which: [pl.BlockSpec(block(g), lambda t, idx, which=which: (idx[which], 0, 0)) for g in gs]
    fixed = lambda j: [pl.BlockSpec(block(g), lambda t, idx, j=j: (j, 0, 0)) for g in gs]
    grid_spec = pltpu.PrefetchScalarGridSpec(
        num_scalar_prefetch=1, grid=(1,),
        in_specs=picked(0) + picked(1) + fixed(0) + fixed(1) + fixed(2),
        out_specs=[pl.BlockSpec(g.shape[1:], lambda t, idx: (0, 0)) for g in gs])
    return pl.pallas_call(
        body, name=name, grid_spec=grid_spec, out_shape=[jax.ShapeDtypeStruct(g.shape[1:], F32) for g in gs],
        compiler_params=_params(dimension_semantics=("arbitrary",)),
    )(jnp.stack([jnp.asarray(me, jnp.int32), jnp.asarray(my_chip, jnp.int32)]), *gs, *gots, *froms, *froms, *froms)


def _sum_call(parts, out_dtype, name):
    rows, cols = parts[0][0].shape[1:]
    tr = _sum_tile(rows)
    index = jnp.stack([jnp.asarray(i, jnp.int32) for _, i in parts])

    def body(idx_ref, *refs):
        del idx_ref
        acc = refs[0][0].astype(F32)
        for r in refs[1:-1]:
            acc = acc + r[0].astype(F32)
        refs[-1][...] = acc.astype(out_dtype)

    grid_spec = pltpu.PrefetchScalarGridSpec(
        num_scalar_prefetch=1, grid=(rows // tr,),
        in_specs=[pl.BlockSpec((1, tr, cols), lambda t, idx, n=n: (idx[n], t, 0)) for n in range(len(parts))],
        out_specs=pl.BlockSpec((tr, cols), lambda t, idx: (t, 0)))
    return pl.pallas_call(
        body, name=name, grid_spec=grid_spec, out_shape=jax.ShapeDtypeStruct((rows, cols), out_dtype),
        compiler_params=_params(dimension_semantics=("arbitrary",)),
    )(index, *(a for a, _ in parts))


def _place():
    return lax.axis_index("x"), lax.axis_index("y"), lax.axis_index("c")


def _any_specs(n):
    return [pl.BlockSpec(memory_space=pl.ANY)] * n


def _all_gather_call(xs, name):
    n = len(xs)

    def body(*refs):
        for phase in _gather_phases(refs[:n], refs[n:2 * n], *refs[2 * n:]):
            phase()

    return pl.pallas_call(
        body, name=name, in_specs=_any_specs(n), out_specs=_any_specs(n),
        out_shape=_gather_shapes(xs), scratch_shapes=_gather_sems(n),
    )(*xs)


def _gather_shapes(xs):
    return [jax.ShapeDtypeStruct((NDEV,) + x.shape, x.dtype) for x in xs]


def _gather_sems(n):
    return [pltpu.SemaphoreType.DMA((7 * n,)), pltpu.SemaphoreType.DMA((7 * n,)), pltpu.SemaphoreType.DMA((n,))]


def _gather_phases(x_refs, out_refs, send_sems, recv_sems, local_sems):
    n = len(x_refs)
    ax, ay, ac = _place()
    me, sibling = (ax, ay, ac), (ax, ay, 1 - ac)
    chips = [(1 - ax, ay), (ax, 1 - ay), (1 - ax, 1 - ay)]

    def copy(a, k, block, to, src=None):
        slot = out_refs[a].at[4 * block[0] + 2 * block[1] + block[2]]
        return pltpu.make_async_remote_copy(
            src_ref=slot if src is None else src, dst_ref=slot,
            send_sem=send_sems.at[7 * a + k], recv_sem=recv_sems.at[7 * a + k], device_id=to, device_id_type=MESH)

    local = [pltpu.make_async_copy(x_refs[a], out_refs[a].at[4 * ax + 2 * ay + ac], local_sems.at[a]) for a in range(n)]
    first = []
    for a in range(n):
        first.append(copy(a, 0, me, sibling, src=x_refs[a]))
        first += [copy(a, 1 + j, me, (*chip, ac), src=x_refs[a]) for j, chip in enumerate(chips)]
    passed = [copy(a, 4 + j, (*chip, ac), sibling) for j, chip in enumerate(chips) for a in range(n)]

    def send():
        for cp in local + first:
            cp.start()

    def forward():
        for j, chip in enumerate(chips):
            for a in range(n):
                copy(a, 1 + j, (*chip, ac), me).wait_recv()
                passed[j * n + a].start()

    def finish():
        for a in range(n):
            copy(a, 0, sibling, me).wait_recv()
            for j, chip in enumerate(chips):
                copy(a, 4 + j, (*chip, 1 - ac), me).wait_recv()
        for cp in first + passed:
            cp.wait_send()
        for cp in local:
            cp.wait()

    return send, forward, finish


def _swap_sibling_call(xs, name):
    n = len(xs)

    def body(*refs):
        for phase in _sibling_swap_phases(refs[:n], refs[n:2 * n], *refs[2 * n:]):
            phase()

    return pl.pallas_call(
        body, name=name, in_specs=_any_specs(n), out_specs=_any_specs(n),
        out_shape=_sibling_swap_shapes(xs), scratch_shapes=_sibling_swap_sems(n),
    )(*xs)


def _sibling_swap_shapes(xs):
    return [jax.ShapeDtypeStruct((4,) + x.shape[1:], x.dtype) for x in xs]


def _sibling_swap_sems(n):
    return [pltpu.SemaphoreType.DMA((n,)), pltpu.SemaphoreType.DMA((n,))]


def _sibling_swap_phases(x_refs, out_refs, send_sems, recv_sems):
    ax, ay, ac = _place()
    sibling = (ax, ay, 1 - ac)

    def send():
        for a, (x_ref, out_ref) in enumerate(zip(x_refs, out_refs)):
            for chip in range(4):
                pltpu.make_async_remote_copy(src_ref=x_ref.at[2 * chip + 1 - ac], dst_ref=out_ref.at[chip],
                                             send_sem=send_sems.at[a], recv_sem=recv_sems.at[a],
                                             device_id=sibling, device_id_type=MESH).start()

    def finish():
        for a, (x_ref, out_ref) in enumerate(zip(x_refs, out_refs)):
            pltpu.make_async_remote_copy(src_ref=x_ref.at[pl.ds(0, 4)], dst_ref=out_ref, send_sem=send_sems.at[a],
                                         recv_sem=recv_sems.at[a], device_id=sibling, device_id_type=MESH).wait()

    return send, finish


def _chip_swap_shapes(xs):
    return [jax.ShapeDtypeStruct((3,) + x.shape[1:], x.dtype) for x in xs]


def _chip_swap_sems(n):
    return [pltpu.SemaphoreType.DMA((3 * n,)), pltpu.SemaphoreType.DMA((3 * n,))]


def _chip_swap_phases(x_refs, out_refs, send_sems, recv_sems):
    ax, ay, ac = _place()
    chips = [(1 - ax, ay), (ax, 1 - ay), (1 - ax, 1 - ay)]
    copies = [pltpu.make_async_remote_copy(src_ref=x_refs[a].at[2 * cx + cy], dst_ref=out_refs[a].at[j],
                                           send_sem=send_sems.at[3 * a + j], recv_sem=recv_sems.at[3 * a + j],
                                           device_id=(cx, cy, ac), device_id_type=MESH)
              for a in range(len(x_refs)) for j, (cx, cy) in enumerate(chips)]

    def send():
        for cp in copies:
            cp.start()

    def finish():
        for cp in copies:
            cp.wait()

    return send, finish


def _reduce_begin(gs, name):
    got = _swap_sibling_call(gs, name + "_d2d")
    return got, _pair_sum_call(gs, got, lax.axis_index("c"), name + "_pair")


def _reduce_end(gs, got, from_chips, name):
    ax, ay, ac = _place()
    return _total_sum_call(gs, got, from_chips, 4 * ax + 2 * ay + ac, 2 * ax + ay, name + "_total")


SMALL = ("norm1_g", "conv_w", "a_log", "dt_bias", "dn_out_g", "sb_q_g", "sb_k_g", "sg_v_g", "sg_w", "sg_b", "norm2_g")
WEIGHTS = ("norm1_g", "w_in", "conv_w", "a_log", "dt_bias", "dn_out_g", "sb_q_g", "sb_k_g", "sg_v_g", "sg_w", "sg_b",
           "w_out", "norm2_g", "w_ff1", "w_ff2")
SMALL_SHAPE = {"norm1_g": (D,), "conv_w": (4, 3 * DN_W), "a_log": (NH,), "dt_bias": (NH,), "dn_out_g": (128,), "sb_q_g": (64,),
               "sb_k_g": (64,), "sg_v_g": (SG_W,), "sg_w": (NH, 128, 128), "sg_b": (NH, 128), "norm2_g": (D,)}


def _size(shape):
    n = 1
    for s in shape:
        n *= s
    return n


def _to_rows(flat, multiple):
    pad = (-flat.shape[0]) % (LANES * multiple)
    return jnp.pad(flat, (0, pad)).reshape(-1, LANES)


def _conv_by_pair(conv):
    return conv.reshape(4, 3, 2, 256).transpose(0, 2, 1, 3).reshape(4, 3 * DN_W)


def kernel(x, norm1_g, w_in, conv_w, a_log, dt_bias, dn_out_g, sb_q_g, sb_k_g, sg_v_g, sg_w, sg_b, w_out, norm2_g, w_ff1, w_ff2, loss_target, m_norm1_g, m_w_in, m_conv_w, m_a_log, m_dt_bias, m_dn_out_g, m_sb_q_g, m_sb_k_g, m_sg_v_g, m_sg_w, m_sg_b, m_w_out, m_norm2_g, m_w_ff1, m_w_ff2, v_norm1_g, v_w_in, v_conv_w, v_a_log, v_dt_bias, v_dn_out_g, v_sb_q_g, v_sb_k_g, v_sg_v_g, v_sg_w, v_sg_b, v_w_out, v_norm2_g, v_w_ff1, v_w_ff2):
    given = dict(norm1_g=norm1_g, w_in=w_in, conv_w=conv_w, a_log=a_log, dt_bias=dt_bias, dn_out_g=dn_out_g, sb_q_g=sb_q_g,
                 sb_k_g=sb_k_g, sg_v_g=sg_v_g, sg_w=sg_w, sg_b=sg_b, w_out=w_out, norm2_g=norm2_g, w_ff1=w_ff1, w_ff2=w_ff2)
    mom = dict(norm1_g=m_norm1_g, w_in=m_w_in, conv_w=m_conv_w, a_log=m_a_log, dt_bias=m_dt_bias, dn_out_g=m_dn_out_g,
               sb_q_g=m_sb_q_g, sb_k_g=m_sb_k_g, sg_v_g=m_sg_v_g, sg_w=m_sg_w, sg_b=m_sg_b, w_out=m_w_out, norm2_g=m_norm2_g,
               w_ff1=m_w_ff1, w_ff2=m_w_ff2)
    var = dict(norm1_g=v_norm1_g, w_in=v_w_in, conv_w=v_conv_w, a_log=v_a_log, dt_bias=v_dt_bias, dn_out_g=v_dn_out_g,
               sb_q_g=v_sb_q_g, sb_k_g=v_sb_k_g, sg_v_g=v_sg_v_g, sg_w=v_sg_w, sg_b=v_sg_b, w_out=v_w_out, norm2_g=v_norm2_g,
               w_ff1=v_w_ff1, w_ff2=v_w_ff2)
    B, T, _ = x.shape
    M = B * T
    ax, ay, ac = _place()
    me = 4 * ax + 2 * ay + ac
    table_fwd, table_back = _row_tables()

    send = []
    for l in range(2):
        w_in_t = jnp.pad(w_in[l].T, ((0, IN_SHARD_PAD - IN_SHARD), (0, 0)))
        send.append([w_in_t.astype(BF16), w_out[l].astype(BF16), w_ff1[l].astype(BF16), w_ff2[l].astype(BF16)])
    first_in, conv_rows = _all_gather_call([send[0][0], _to_rows(conv_w.reshape(-1), 8)], "gather_first")
    conv_full = conv_rows.reshape(NDEV, -1)[:, :conv_w.size].reshape(NDEV, 2, 4, -1).transpose(1, 2, 0, 3).reshape(2, 4, 3 * DN_W)
    gathered = [[first_in, None, None, None], [None] * 4]

    pad_vec = lambda v: jnp.zeros((1, LANES), F32).at[0, :v.shape[0]].set(v)
    layer = []
    for l in range(2):
        layer.append(dict(
            g1=norm1_g[l].reshape(1, D), g2=norm2_g[l].reshape(1, D), conv=_conv_by_pair(conv_full[l]),
            a_log=pad_vec(a_log[l]), dt_bias=pad_vec(dt_bias[l]), dn_g=dn_out_g[l].reshape(1, LANES),
            sb_qg=jnp.tile(sb_q_g[l], 2).reshape(1, LANES), sb_kg=jnp.tile(sb_k_g[l], 2).reshape(1, LANES),
            sg_g=sg_v_g[l].reshape(1, SG_W), sg_w=sg_w[l], sg_bias=jnp.repeat(sg_b[l].T, 64, axis=1)))

    cur = x.reshape(M, D)
    saved = []
    for l, p in enumerate(layer):
        p["wt"] = _row_perm_call(gathered[l][0].reshape(NDEV * IN_SHARD_PAD, D), table_fwd, "pack_w_in")
        p_dn, p_sb, p_sg, p_ab, h = _in_proj_call(cur, p["g1"], p["wt"])
        mix, dn_kept, arrived = _dn_fwd_call(p_dn, p_ab, p["conv"], p["a_log"], p["dt_bias"], p["dn_g"], B, T,
                                             gather=send[0][1:] + send[1][:1] if l == 0 else [])
        if l == 0:
            gathered[0][1:], gathered[1][0] = list(arrived[:3]), arrived[3]
        p["w_out"], p["w1"], p["w2"] = gathered[l][1].reshape(D, D), gathered[l][2], gathered[l][3].reshape(DFF, D)
        mix, sb_carries, arrived = _sb_fwd_call(p_sb, mix, p["sb_qg"], p["sb_kg"], B, T, gather=send[1][1:] if l == 0 else [])
        if l == 0:
            gathered[1][1:] = list(arrived)
        mix = _sg_fwd_call(p_sg, mix, p["sg_g"], p["sg_w"], p["sg_bias"], B, T)
        x1 = _out_proj_call(mix, p["w_out"], cur)
        x2 = _ffn_fwd_call(x1, p["g2"], p["w1"], p["w2"])
        saved.append(dict(x0=cur, p_dn=p_dn, p_sb=p_sb, p_sg=p_sg, p_ab=p_ab, h=h, mix=mix, x1=x1, dn_kept=dn_kept, sb_carries=sb_carries))
        cur = x2
    loss_part, dy = _loss_call(cur, loss_target.reshape(M, D))
    loss = lax.psum(loss_part[0, 0], ("x", "y", "c"))

    big_grads = [[None] * 4, [None] * 4]
    small_grads = {n: [None, None] for n in SMALL}
    for l in (1, 0):
        p, s = layer[l], saved[l]
        (dx1, da, r, h2, dg2), got1 = _ffn_bwd_call(s["x1"], dy, p["g2"], p["w1"], p["w2"], swap=big_grads[1] if l == 0 else ())
        big_grads[l][2] = _mm_tn_call(h2, da, "grad_w_ff1", col_shards=True)
        big_grads[l][3] = _mm_tn_call(r, dy, "grad_w_ff2").reshape(NDEV, FF_SHARD, D)
        dmix = _mm_nt_call(dx1, p["w_out"], "dmix")
        big_grads[l][1] = _mm_tn_call(s["mix"], dx1, "grad_w_out").reshape(NDEV, D // NDEV, D)
        if l == 0:
            got0, sums0 = _reduce_begin(big_grads[0][1:], "reduce_early0")
            early_sums = list(_pair_sum_call(big_grads[1], got1, ac, "reduce_early1_pair")) + list(sums0)
        (d_dn, d_ab, dcw, dalog, ddtb, ddn_g), early_from = _dn_bwd_call(
            s["p_dn"], s["p_ab"], dmix, s["dn_kept"], p["conv"], p["a_log"], p["dt_bias"], p["dn_g"], B, T,
            swap=early_sums if l == 0 else ())
        d_sb, dqg, dkg = _sb_bwd_call(s["p_sb"], dmix, s["sb_carries"], p["sb_qg"], p["sb_kg"], B, T)
        d_sg, dsg_g, dsg_w, dsg_b = _sg_bwd_call(s["p_sg"], dmix, p["sg_g"], p["sg_w"], p["sg_bias"], B, T)
        dsections = (d_dn, d_sb, d_sg, d_ab)
        dwt = _in_proj_grad_call(dsections, s["h"])
        big_grads[l][0] = _row_perm_call(dwt, table_back, "unpack_grad_w_in").reshape(NDEV, IN_SHARD_PAD, D)
        if l == 0:
            last = big_grads[0][:1]
            last_got, last_sums = _reduce_begin(last, "reduce_last")
        (dy, dg1), last_from = _in_proj_bwd_call(dsections, p["wt"], s["x0"], p["g1"], dx1, swap=last_sums if l == 0 else ())
        for n, val in (("norm1_g", dg1[0]), ("conv_w", dcw.transpose(1, 0, 2).reshape(4, 3 * DN_W)), ("a_log", dalog[0, :NH]),
                       ("dt_bias", ddtb[0, :NH]), ("dn_out_g", ddn_g[0]), ("sb_q_g", dqg[0, :64]), ("sb_k_g", dkg[0, :64]),
                       ("sg_v_g", dsg_g[0]), ("sg_w", dsg_w), ("sg_b", dsg_b[:, :NH].T), ("norm2_g", dg2[0])):
            small_grads[n][l] = val
    grad_x = dy.reshape(B, T, D)

    mine0 = _reduce_end(last, last_got, last_from, "reduce_last")
    mine1 = (_reduce_end(big_grads[1], got1, early_from[:4], "reduce_early1")
             + _reduce_end(big_grads[0][1:], got0, early_from[4:], "reduce_early0"))
    grads = {"w_in": jnp.stack([mine0[0][:IN_SHARD].T, mine1[0][:IN_SHARD].T]), "w_out": jnp.stack([mine1[4], mine1[1]]),
             "w_ff1": jnp.stack([mine1[5], mine1[2]]), "w_ff2": jnp.stack([mine1[6], mine1[3]])}
    small_flat = jnp.concatenate([jnp.stack(small_grads[n]).reshape(-1) for n in SMALL])
    everyone, = _all_gather_call([_to_rows(small_flat, 8)], "gather_small_grads")
    small_sum = _sum_call([(everyone, k) for k in range(NDEV)], F32, "sum_small_grads").reshape(-1)
    off = 0
    for n in SMALL:
        sz = 2 * _size(SMALL_SHAPE[n])
        grads[n] = small_sum[off:off + sz].reshape((2,) + SMALL_SHAPE[n])
        off += sz
    cshard = conv_w.shape[-1]
    grads["conv_w"] = lax.dynamic_slice_in_dim(grads["conv_w"], me * cshard, cshard, axis=2)

    deltas, new_m, new_v = {}, {}, {}
    for n in WEIGHTS:
        deltas[n], new_m[n], new_v[n] = _adamw_call(given[n], grads[n], mom[n], var[n], "adamw_" + n)
    return (loss, grad_x, *[grads[n] for n in WEIGHTS], *[deltas[n] for n in WEIGHTS], *[new_m[n] for n in WEIGHTS],
            *[new_v[n] for n in WEIGHTS])
```
